```python
import jax, jax.numpy as jnp
from jax import lax
import numpy as np

D_MODEL = 1024
BATCH = 8
SEQ = 4096
DEPTH = 4

N_A = DEPTH // 2
N_B = DEPTH - N_A
HEAD_DIM = 64
MIX_WIDTH = D_MODEL
MEM_LEN = 256
MEM_HEADS = 4
MEM_WIDTH = MEM_HEADS * HEAD_DIM
MAIN_WIDTH = MIX_WIDTH - MEM_WIDTH
POOL_WINDOWS = (2, 4, 8, 16)
POOL_GROUPS = len(POOL_WINDOWS)
POOL_GROUP_DIM = MAIN_WIDTH // POOL_GROUPS
SWA_Q_HEADS = MAIN_WIDTH // HEAD_DIM
SWA_KV_HEADS = 4
SWA_GROUP = SWA_Q_HEADS // SWA_KV_HEADS
KV_WIDTH = 2 * SWA_KV_HEADS * HEAD_DIM
MEM_KV_WIDTH = 2 * MEM_WIDTH
WINDOW = 128
BLOCK = 128
D_FF = 4 * D_MODEL
EPS = 1e-6

kernel_name = "yoco_pool_swa_sink_hybrid"


def rmsnorm(x, g):
    xf = x.astype(jnp.float32)
    y = xf * lax.rsqrt(jnp.mean(xf * xf, axis=-1, keepdims=True) + EPS)
    return (y * g.astype(jnp.float32)).astype(x.dtype)


def alibi_slopes(n):
    return jnp.exp2(-8.0 * jnp.arange(1, n + 1, dtype=jnp.float32) / n)


def pool_mixer(u, pool_w, pool_scale):
    B, S, _ = u.shape
    uf = u.astype(jnp.float32).reshape(B, S, POOL_GROUPS, POOL_GROUP_DIM)
    csum = jnp.concatenate([jnp.zeros((B, 1, POOL_GROUPS, POOL_GROUP_DIM), jnp.float32),
                            jnp.cumsum(uf, axis=1)], axis=1)
    win = jnp.array(POOL_WINDOWS, jnp.int32)
    t = jnp.arange(S, dtype=jnp.int32)[:, None]
    lo = jnp.maximum(t + 1 - win[None, :], 0)
    cnt = jnp.minimum(t + 1, win[None, :]).astype(jnp.float32)
    window_sum = csum[:, 1:] - csum[:, lo, jnp.arange(POOL_GROUPS)[None, :]]
    d = (window_sum / cnt[None, :, :, None] - uf).astype(u.dtype)
    mixed = jnp.einsum('bsgc,gcd->bsgd', d, pool_w)
    return mixed.reshape(B, S, MAIN_WIDTH) * pool_scale


def swa_sink_attention(q, k, v, sinks):
    B, S = q.shape[0], q.shape[1]
    nb = S // BLOCK
    qb = q.reshape(B, nb, BLOCK, SWA_KV_HEADS, SWA_GROUP, HEAD_DIM)

    def with_prev(a):
        ab = a.reshape(B, nb, BLOCK, SWA_KV_HEADS, HEAD_DIM)
        prev = jnp.pad(ab[:, :-1], ((0, 0), (1, 0), (0, 0), (0, 0), (0, 0)))
        return jnp.concatenate([prev, ab], axis=2)

    kb, vb = with_prev(k), with_prev(v)
    s = jnp.einsum('bnqkgd,bnpkd->bnkgqp', qb, kb).astype(jnp.float32) * (HEAD_DIM ** -0.5)
    blk = jnp.arange(nb, dtype=jnp.int32)[:, None] * BLOCK
    qpos = blk + jnp.arange(BLOCK, dtype=jnp.int32)[None, :]
    kpos = blk - BLOCK + jnp.arange(2 * BLOCK, dtype=jnp.int32)[None, :]
    dist = qpos[:, :, None] - kpos[:, None, :]
    valid = (dist >= 0) & (dist < WINDOW) & (kpos[:, None, :] >= 0)
    slopes = alibi_slopes(SWA_Q_HEADS).reshape(SWA_KV_HEADS, SWA_GROUP)
    s = s - slopes[None, None, :, :, None, None] * dist.astype(jnp.float32)[None, :, None, None]
    s = jnp.where(valid[None, :, None, None], s, jnp.finfo(jnp.float32).min)
    sink = sinks.astype(jnp.float32).reshape(SWA_KV_HEADS, SWA_GROUP)[None, None, :, :, None, None]
    m = jnp.maximum(jnp.max(s, axis=-1, keepdims=True), sink)
    e = jnp.exp(s - m)
    p = e / (jnp.sum(e, axis=-1, keepdims=True) + jnp.exp(sink - m))
    o = jnp.einsum('bnkgqp,bnpkd->bnqkgd', p.astype(vb.dtype), vb)
    return o.reshape(B, S, SWA_Q_HEADS * HEAD_DIM)


def memory_attention(q, mk, mv):
    s = jnp.einsum('bshd,bmhd->bhsm', q, mk).astype(jnp.float32) * (HEAD_DIM ** -0.5)
    p = jax.nn.softmax(s, axis=-1)
    o = jnp.einsum('bhsm,bmhd->bshd', p.astype(mv.dtype), mv)
    return o.reshape(q.shape[0], q.shape[1], MEM_WIDTH)


def sq_relu_mlp(x, w_up, w_down):
    h = jax.nn.relu(x @ w_up)
    return (h * h) @ w_down


def _fwd_setup_inputs(seed: int = 0) -> dict:
    key = jax.random.key(seed)
    ks = jax.random.split(key, 20)
    f32 = jnp.float32

    def nrm(k, shape, scale):
        return jax.random.normal(k, shape, f32) * scale

    def gain(k, shape):
        return 1.0 + 0.02 * jax.random.normal(k, shape, f32)

    return {
        "x": nrm(ks[0], (BATCH, SEQ, D_MODEL), 1.0),
        "mem": nrm(ks[1], (BATCH, MEM_LEN, D_MODEL), 1.0),
        "norm_mix": gain(ks[2], (DEPTH, D_MODEL)),
        "w_in": nrm(ks[3], (DEPTH, D_MODEL, MIX_WIDTH), D_MODEL ** -0.5),
        "pool_w": nrm(ks[4], (N_A, POOL_GROUPS, POOL_GROUP_DIM, POOL_GROUP_DIM), POOL_GROUP_DIM ** -0.5),
        "pool_scale": gain(ks[5], (N_A, MAIN_WIDTH)),
        "kv_norm": gain(ks[6], (D_MODEL,)),
        "w_kv": nrm(ks[7], (D_MODEL, KV_WIDTH), D_MODEL ** -0.5),
        "k_norm": gain(ks[8], (HEAD_DIM,)),
        "q_norm": gain(ks[9], (N_B, HEAD_DIM)),
        "sinks": nrm(ks[10], (N_B, SWA_Q_HEADS), 0.5),
        "mem_norm": gain(ks[11], (DEPTH, D_MODEL)),
        "w_mem_kv": nrm(ks[12], (DEPTH, D_MODEL, MEM_KV_WIDTH), D_MODEL ** -0.5),
        "mem_q_norm": gain(ks[13], (DEPTH, HEAD_DIM)),
        "mem_k_norm": gain(ks[14], (DEPTH, HEAD_DIM)),
        "w_out": nrm(ks[15], (DEPTH, MIX_WIDTH, D_MODEL), MIX_WIDTH ** -0.5),
        "norm_mlp": gain(ks[16], (DEPTH, D_MODEL)),
        "w_up": nrm(ks[17], (DEPTH, D_MODEL, D_FF), D_MODEL ** -0.5),
        "w_down": nrm(ks[18], (DEPTH, D_FF, D_MODEL), D_FF ** -0.5),
    }


def _fwd_reference(x, mem, norm_mix, w_in, pool_w, pool_scale, kv_norm, w_kv, k_norm, q_norm, sinks,
              mem_norm, w_mem_kv, mem_q_norm, mem_k_norm, w_out, norm_mlp, w_up, w_down):
    B, S, _ = x.shape
    h = x
    k_shared = None
    v_shared = None
    for l in range(DEPTH):
        if l == N_A:
            kv = rmsnorm(h, kv_norm) @ w_kv
            k_shared = rmsnorm(kv[..., :KV_WIDTH // 2].reshape(B, S, SWA_KV_HEADS, HEAD_DIM), k_norm)
            v_shared = kv[..., KV_WIDTH // 2:].reshape(B, S, SWA_KV_HEADS, HEAD_DIM)

        proj = rmsnorm(h, norm_mix[l]) @ w_in[l]
        main, mq = proj[..., :MAIN_WIDTH], proj[..., MAIN_WIDTH:]
        if l < N_A:
            main_out = pool_mixer(main, pool_w[l], pool_scale[l])
        else:
            j = l - N_A
            q = rmsnorm(main.reshape(B, S, SWA_Q_HEADS, HEAD_DIM), q_norm[j])
            main_out = swa_sink_attention(q, k_shared, v_shared, sinks[j])

        mkv = rmsnorm(mem, mem_norm[l]) @ w_mem_kv[l]
        mk = rmsnorm(mkv[..., :MEM_WIDTH].reshape(B, MEM_LEN, MEM_HEADS, HEAD_DIM), mem_k_norm[l])
        mv = mkv[..., MEM_WIDTH:].reshape(B, MEM_LEN, MEM_HEADS, HEAD_DIM)
        mqh = rmsnorm(mq.reshape(B, S, MEM_HEADS, HEAD_DIM), mem_q_norm[l])
        mem_out = memory_attention(mqh, mk, mv)

        h = h + jnp.concatenate([main_out, mem_out], axis=-1) @ w_out[l]
        h = h + sq_relu_mlp(rmsnorm(h, norm_mlp[l]), w_up[l], w_down[l])
    return h


import jax as _jax
import jax.numpy as _jnp

TWIN_FORMAT = 'train_step'
FWD_PARAMS = ['x', 'mem', 'norm_mix', 'w_in', 'pool_w', 'pool_scale', 'kv_norm', 'w_kv', 'k_norm', 'q_norm', 'sinks', 'mem_norm', 'w_mem_kv', 'mem_q_norm', 'mem_k_norm', 'w_out', 'norm_mlp', 'w_up', 'w_down']
TWIN_WEIGHTS = ['norm_mix', 'w_in', 'pool_w', 'pool_scale', 'kv_norm', 'w_kv', 'k_norm', 'q_norm', 'sinks', 'mem_norm', 'w_mem_kv', 'mem_q_norm', 'mem_k_norm', 'w_out', 'norm_mlp', 'w_up', 'w_down']
TWIN_DIFF_INPUT = 'x'
TWIN_INPUTS = ['x', 'mem', 'norm_mix', 'w_in', 'pool_w', 'pool_scale', 'kv_norm', 'w_kv', 'k_norm', 'q_norm', 'sinks', 'mem_norm', 'w_mem_kv', 'mem_q_norm', 'mem_k_norm', 'w_out', 'norm_mlp', 'w_up', 'w_down', 'loss_target', 'm_norm_mix', 'm_w_in', 'm_pool_w', 'm_pool_scale', 'm_kv_norm', 'm_w_kv', 'm_k_norm', 'm_q_norm', 'm_sinks', 'm_mem_norm', 'm_w_mem_kv', 'm_mem_q_norm', 'm_mem_k_norm', 'm_w_out', 'm_norm_mlp', 'm_w_up', 'm_w_down', 'v_norm_mix', 'v_w_in', 'v_pool_w', 'v_pool_scale', 'v_kv_norm', 'v_w_kv', 'v_k_norm', 'v_q_norm', 'v_sinks', 'v_mem_norm', 'v_w_mem_kv', 'v_mem_q_norm', 'v_mem_k_norm', 'v_w_out', 'v_norm_mlp', 'v_w_up', 'v_w_down']
TWIN_OUTPUTS = ['loss', 'grad_x', 'grad_norm_mix', 'grad_w_in', 'grad_pool_w', 'grad_pool_scale', 'grad_kv_norm', 'grad_w_kv', 'grad_k_norm', 'grad_q_norm', 'grad_sinks', 'grad_mem_norm', 'grad_w_mem_kv', 'grad_mem_q_norm', 'grad_mem_k_norm', 'grad_w_out', 'grad_norm_mlp', 'grad_w_up', 'grad_w_down', 'delta_norm_mix', 'delta_w_in', 'delta_pool_w', 'delta_pool_scale', 'delta_kv_norm', 'delta_w_kv', 'delta_k_norm', 'delta_q_norm', 'delta_sinks', 'delta_mem_norm', 'delta_w_mem_kv', 'delta_mem_q_norm', 'delta_mem_k_norm', 'delta_w_out', 'delta_norm_mlp', 'delta_w_up', 'delta_w_down', 'new_m_norm_mix', 'new_m_w_in', 'new_m_pool_w', 'new_m_pool_scale', 'new_m_kv_norm', 'new_m_w_kv', 'new_m_k_norm', 'new_m_q_norm', 'new_m_sinks', 'new_m_mem_norm', 'new_m_w_mem_kv', 'new_m_mem_q_norm', 'new_m_mem_k_norm', 'new_m_w_out', 'new_m_norm_mlp', 'new_m_w_up', 'new_m_w_down', 'new_v_norm_mix', 'new_v_w_in', 'new_v_pool_w', 'new_v_pool_scale', 'new_v_kv_norm', 'new_v_w_kv', 'new_v_k_norm', 'new_v_q_norm', 'new_v_sinks', 'new_v_mem_norm', 'new_v_w_mem_kv', 'new_v_mem_q_norm', 'new_v_mem_k_norm', 'new_v_w_out', 'new_v_norm_mlp', 'new_v_w_up', 'new_v_w_down']
TWIN_LEAF_KINDS = {'loss': 'loss', 'grad_x': 'grad_x', 'grad_norm_mix': 'grad_w', 'grad_w_in': 'grad_w', 'grad_pool_w': 'grad_w', 'grad_pool_scale': 'grad_w', 'grad_kv_norm': 'grad_w', 'grad_w_kv': 'grad_w', 'grad_k_norm': 'grad_w', 'grad_q_norm': 'grad_w', 'grad_sinks': 'grad_w', 'grad_mem_norm': 'grad_w', 'grad_w_mem_kv': 'grad_w', 'grad_mem_q_norm': 'grad_w', 'grad_mem_k_norm': 'grad_w', 'grad_w_out': 'grad_w', 'grad_norm_mlp': 'grad_w', 'grad_w_up': 'grad_w', 'grad_w_down': 'grad_w', 'delta_norm_mix': 'delta_w', 'delta_w_in': 'delta_w', 'delta_pool_w': 'delta_w', 'delta_pool_scale': 'delta_w', 'delta_kv_norm': 'delta_w', 'delta_w_kv': 'delta_w', 'delta_k_norm': 'delta_w', 'delta_q_norm': 'delta_w', 'delta_sinks': 'delta_w', 'delta_mem_norm': 'delta_w', 'delta_w_mem_kv': 'delta_w', 'delta_mem_q_norm': 'delta_w', 'delta_mem_k_norm': 'delta_w', 'delta_w_out': 'delta_w', 'delta_norm_mlp': 'delta_w', 'delta_w_up': 'delta_w', 'delta_w_down': 'delta_w', 'new_m_norm_mix': 'new_m', 'new_m_w_in': 'new_m', 'new_m_pool_w': 'new_m', 'new_m_pool_scale': 'new_m', 'new_m_kv_norm': 'new_m', 'new_m_w_kv': 'new_m', 'new_m_k_norm': 'new_m', 'new_m_q_norm': 'new_m', 'new_m_sinks': 'new_m', 'new_m_mem_norm': 'new_m', 'new_m_w_mem_kv': 'new_m', 'new_m_mem_q_norm': 'new_m', 'new_m_mem_k_norm': 'new_m', 'new_m_w_out': 'new_m', 'new_m_norm_mlp': 'new_m', 'new_m_w_up': 'new_m', 'new_m_w_down': 'new_m', 'new_v_norm_mix': 'new_v', 'new_v_w_in': 'new_v', 'new_v_pool_w': 'new_v', 'new_v_pool_scale': 'new_v', 'new_v_kv_norm': 'new_v', 'new_v_w_kv': 'new_v', 'new_v_k_norm': 'new_v', 'new_v_q_norm': 'new_v', 'new_v_sinks': 'new_v', 'new_v_mem_norm': 'new_v', 'new_v_w_mem_kv': 'new_v', 'new_v_mem_q_norm': 'new_v', 'new_v_mem_k_norm': 'new_v', 'new_v_w_out': 'new_v', 'new_v_norm_mlp': 'new_v', 'new_v_w_up': 'new_v', 'new_v_w_down': 'new_v'}


def _forward(args):
    return _fwd_reference(*[args[k] for k in FWD_PARAMS])


def _output_shape():
    out = _jax.eval_shape(lambda: _forward(_fwd_setup_inputs(0)))
    return out.shape, out.dtype

N_MICROBATCH = 1
ADAM_LR = 0.001
ADAM_B1 = 0.9
ADAM_B2 = 0.999
ADAM_EPS = 1e-08
ADAM_WD = 0.01
ADAM_STEP = 10
PER_EXAMPLE_BATCH_AXIS = {'x': 0, 'mem': 0, 'loss_target': 0}
SHARED_INPUTS = []
_WEIGHT_DTYPES = {'norm_mix': _jnp.float32, 'w_in': _jnp.float32, 'pool_w': _jnp.float32, 'pool_scale': _jnp.float32, 'kv_norm': _jnp.float32, 'w_kv': _jnp.float32, 'k_norm': _jnp.float32, 'q_norm': _jnp.float32, 'sinks': _jnp.float32, 'mem_norm': _jnp.float32, 'w_mem_kv': _jnp.float32, 'mem_q_norm': _jnp.float32, 'mem_k_norm': _jnp.float32, 'w_out': _jnp.float32, 'norm_mlp': _jnp.float32, 'w_up': _jnp.float32, 'w_down': _jnp.float32}
MOMENT_SCALE = {'norm_mix': 1.187376e+01, 'w_in': 1.656926e+00, 'pool_w': 3.196657e+00, 'pool_scale': 2.331026e+01, 'kv_norm': 2.247844e+01, 'w_kv': 2.707780e+01, 'k_norm': 2.554763e+01, 'q_norm': 1.282140e+01, 'sinks': 4.508403e+01, 'mem_norm': 9.860889e-01, 'w_mem_kv': 1.361844e+00, 'mem_q_norm': 1.826928e+00, 'mem_k_norm': 1.813114e+00, 'w_out': 8.933544e+00, 'norm_mlp': 9.935022e+01, 'w_up': 7.396165e+00, 'w_down': 2.842323e+01}


def _to_microbatches(a, axis):
    t = _jnp.moveaxis(a, axis, 0)
    t = t.reshape((N_MICROBATCH, t.shape[0] // N_MICROBATCH) + t.shape[1:])
    return _jnp.moveaxis(t, 1, axis + 1)


def setup_inputs(seed: int = 0) -> dict:
    inp = _fwd_setup_inputs(seed)
    key = _jax.random.fold_in(_jax.random.key(seed), 7919)
    shape, _ = _output_shape()
    out = dict(inp)
    out["loss_target"] = _jax.random.normal(_jax.random.fold_in(key, 0), shape, _jnp.float32)
    for i, name in enumerate(TWIN_WEIGHTS):
        w = inp[name].astype(_jnp.float32)
        if MOMENT_SCALE is None:
            s = _jnp.sqrt(_jnp.mean(_jnp.square(w)) + 1e-30)
        else:
            s = MOMENT_SCALE[name]
        km, kv = _jax.random.split(_jax.random.fold_in(key, i + 1))
        out[name] = w
        out["m_" + name] = s * _jax.random.normal(km, w.shape, _jnp.float32)
        out["v_" + name] = (s * s) * _jax.random.uniform(kv, w.shape, _jnp.float32, 0.5, 1.5)
    if N_MICROBATCH > 1:
        for name, axis in PER_EXAMPLE_BATCH_AXIS.items():
            out[name] = _to_microbatches(out[name], axis)
    return {'x': out['x'], 'mem': out['mem'], 'norm_mix': out['norm_mix'], 'w_in': out['w_in'], 'pool_w': out['pool_w'], 'pool_scale': out['pool_scale'], 'kv_norm': out['kv_norm'], 'w_kv': out['w_kv'], 'k_norm': out['k_norm'], 'q_norm': out['q_norm'], 'sinks': out['sinks'], 'mem_norm': out['mem_norm'], 'w_mem_kv': out['w_mem_kv'], 'mem_q_norm': out['mem_q_norm'], 'mem_k_norm': out['mem_k_norm'], 'w_out': out['w_out'], 'norm_mlp': out['norm_mlp'], 'w_up': out['w_up'], 'w_down': out['w_down'], 'loss_target': out['loss_target'], 'm_norm_mix': out['m_norm_mix'], 'm_w_in': out['m_w_in'], 'm_pool_w': out['m_pool_w'], 'm_pool_scale': out['m_pool_scale'], 'm_kv_norm': out['m_kv_norm'], 'm_w_kv': out['m_w_kv'], 'm_k_norm': out['m_k_norm'], 'm_q_norm': out['m_q_norm'], 'm_sinks': out['m_sinks'], 'm_mem_norm': out['m_mem_norm'], 'm_w_mem_kv': out['m_w_mem_kv'], 'm_mem_q_norm': out['m_mem_q_norm'], 'm_mem_k_norm': out['m_mem_k_norm'], 'm_w_out': out['m_w_out'], 'm_norm_mlp': out['m_norm_mlp'], 'm_w_up': out['m_w_up'], 'm_w_down': out['m_w_down'], 'v_norm_mix': out['v_norm_mix'], 'v_w_in': out['v_w_in'], 'v_pool_w': out['v_pool_w'], 'v_pool_scale': out['v_pool_scale'], 'v_kv_norm': out['v_kv_norm'], 'v_w_kv': out['v_w_kv'], 'v_k_norm': out['v_k_norm'], 'v_q_norm': out['v_q_norm'], 'v_sinks': out['v_sinks'], 'v_mem_norm': out['v_mem_norm'], 'v_w_mem_kv': out['v_w_mem_kv'], 'v_mem_q_norm': out['v_mem_q_norm'], 'v_mem_k_norm': out['v_mem_k_norm'], 'v_w_out': out['v_w_out'], 'v_norm_mlp': out['v_norm_mlp'], 'v_w_up': out['v_w_up'], 'v_w_down': out['v_w_down']}


def _loss(weights, diff, rest, loss_target):
    with _jax.named_scope("forward"):
        args = {**rest, TWIN_DIFF_INPUT: diff, **{k: w.astype(_WEIGHT_DTYPES[k]) for k, w in weights.items()}}
        y = _forward(args)
    with _jax.named_scope("loss_head"):
        err = _jnp.square(y.astype(_jnp.float32) - loss_target)
        return 0.5 * _jnp.sum(_jnp.mean(err, axis=-1)) if err.ndim else 0.5 * err


def _adamw(w, g, m, v):
    m = ADAM_B1 * m + (1.0 - ADAM_B1) * g
    v = ADAM_B2 * v + (1.0 - ADAM_B2) * _jnp.square(g)
    m_hat = m / (1.0 - ADAM_B1 ** ADAM_STEP)
    v_hat = v / (1.0 - ADAM_B2 ** ADAM_STEP)
    delta = -ADAM_LR * (m_hat / (_jnp.sqrt(v_hat) + ADAM_EPS) + ADAM_WD * w)
    return delta, m, v


def reference(x, mem, norm_mix, w_in, pool_w, pool_scale, kv_norm, w_kv, k_norm, q_norm, sinks, mem_norm, w_mem_kv, mem_q_norm, mem_k_norm, w_out, norm_mlp, w_up, w_down, loss_target, m_norm_mix, m_w_in, m_pool_w, m_pool_scale, m_kv_norm, m_w_kv, m_k_norm, m_q_norm, m_sinks, m_mem_norm, m_w_mem_kv, m_mem_q_norm, m_mem_k_norm, m_w_out, m_norm_mlp, m_w_up, m_w_down, v_norm_mix, v_w_in, v_pool_w, v_pool_scale, v_kv_norm, v_w_kv, v_k_norm, v_q_norm, v_sinks, v_mem_norm, v_w_mem_kv, v_mem_q_norm, v_mem_k_norm, v_w_out, v_norm_mlp, v_w_up, v_w_down):
    given = dict(x=x, mem=mem, norm_mix=norm_mix, w_in=w_in, pool_w=pool_w, pool_scale=pool_scale, kv_norm=kv_norm, w_kv=w_kv, k_norm=k_norm, q_norm=q_norm, sinks=sinks, mem_norm=mem_norm, w_mem_kv=w_mem_kv, mem_q_norm=mem_q_norm, mem_k_norm=mem_k_norm, w_out=w_out, norm_mlp=norm_mlp, w_up=w_up, w_down=w_down, loss_target=loss_target, m_norm_mix=m_norm_mix, m_w_in=m_w_in, m_pool_w=m_pool_w, m_pool_scale=m_pool_scale, m_kv_norm=m_kv_norm, m_w_kv=m_w_kv, m_k_norm=m_k_norm, m_q_norm=m_q_norm, m_sinks=m_sinks, m_mem_norm=m_mem_norm, m_w_mem_kv=m_w_mem_kv, m_mem_q_norm=m_mem_q_norm, m_mem_k_norm=m_mem_k_norm, m_w_out=m_w_out, m_norm_mlp=m_norm_mlp, m_w_up=m_w_up, m_w_down=m_w_down, v_norm_mix=v_norm_mix, v_w_in=v_w_in, v_pool_w=v_pool_w, v_pool_scale=v_pool_scale, v_kv_norm=v_kv_norm, v_w_kv=v_w_kv, v_k_norm=v_k_norm, v_q_norm=v_q_norm, v_sinks=v_sinks, v_mem_norm=v_mem_norm, v_w_mem_kv=v_w_mem_kv, v_mem_q_norm=v_mem_q_norm, v_mem_k_norm=v_mem_k_norm, v_w_out=v_w_out, v_norm_mlp=v_norm_mlp, v_w_up=v_w_up, v_w_down=v_w_down)
    weights = {n: given[n] for n in TWIN_WEIGHTS}
    shared = {n: given[n] for n in SHARED_INPUTS}
    per_example = {n: given[n] for n in ['x', 'mem']}
    grad_fn = _jax.value_and_grad(_loss, argnums=(0, 1))

    def one_microbatch(ex, loss_target):
        ex = dict(ex)
        diff = ex.pop(TWIN_DIFF_INPUT)
        return grad_fn(weights, diff, {**shared, **ex}, loss_target)

    if N_MICROBATCH == 1:
        loss, (grad_w, grad_x) = one_microbatch(per_example, given["loss_target"])
    else:
        def body(carry, xs):
            loss_sum, grad_sum = carry
            l_k, (gw_k, gx_k) = one_microbatch(xs[0], xs[1])
            with _jax.named_scope("update"):
                return (loss_sum + l_k, _jax.tree.map(_jnp.add, grad_sum, gw_k)), gx_k

        init = (_jnp.zeros((), _jnp.float32), _jax.tree.map(_jnp.zeros_like, weights))
        (loss, grad_w), grad_x = _jax.lax.scan(body, init, (per_example, given["loss_target"]))
    with _jax.named_scope("update"):
        delta_w, new_m, new_v = {}, {}, {}
        for n in TWIN_WEIGHTS:
            delta_w[n], new_m[n], new_v[n] = _adamw(weights[n], grad_w[n], given["m_" + n], given["v_" + n])
    return (loss, grad_x, *[grad_w[n] for n in TWIN_WEIGHTS], *[delta_w[n] for n in TWIN_WEIGHTS],
            *[new_m[n] for n in TWIN_WEIGHTS], *[new_v[n] for n in TWIN_WEIGHTS])
```

```python
import math

import jax
import jax.numpy as jnp
from jax import lax
from jax.experimental import pallas as pl
from jax.experimental.pallas import tpu as pltpu

F32 = jnp.float32
BF16 = jnp.bfloat16
MESH = pl.DeviceIdType.MESH
AXES = ("x", "y", "c")

D_MODEL = 1024
DEPTH = 4
N_A = 2
HEAD_DIM = 64
MEM_HEADS = 4
MEM_WIDTH = MEM_HEADS * HEAD_DIM
MAIN_WIDTH = D_MODEL - MEM_WIDTH
POOL_GROUPS = 4
POOL_GROUP_DIM = MAIN_WIDTH // POOL_GROUPS
POOL_HALO = 16
SWA_Q_HEADS = MAIN_WIDTH // HEAD_DIM
SWA_KV_HEADS = 4
SWA_GROUP = SWA_Q_HEADS // SWA_KV_HEADS
KV_HALF = SWA_KV_HEADS * HEAD_DIM
BLOCK = 128
D_FF = 4 * D_MODEL
EPS = 1e-6
SCALE = HEAD_DIM ** -0.5
NEG = float(jnp.finfo(jnp.float32).min)
N_DEV = 8
N_CHIP = 4

ADAM_LR = 0.001
ADAM_B1 = 0.9
ADAM_B2 = 0.999
ADAM_EPS = 1e-08
ADAM_WD = 0.01
ADAM_STEP = 10

PACK_W = 512
VMEM_LIMIT = 52 * 1024 * 1024
MM_TILE = 1024

GATHERED = ("w_in", "w_kv", "w_mem_kv", "w_out", "w_up", "w_down")
SMALL_NAMES = ("norm_mix", "pool_w", "kv_norm", "k_norm", "q_norm", "sinks", "mem_norm", "mem_q_norm", "mem_k_norm",
               "norm_mlp")


ANY = pl.BlockSpec(memory_space=pl.ANY)


def _params(*sem):
    return pltpu.CompilerParams(dimension_semantics=sem, vmem_limit_bytes=VMEM_LIMIT)


def _mm(a, b, *, name, ta=False, tb=False, b_kind=None, layer=0, res=None, relu2=False, mul2=None, out_dtype=F32,
        out_kind=None, out_buf=None):
    if ta:
        K, M = a.shape
    else:
        M, K = a.shape
    if b_kind is None:
        rows_b, cols_b = b.shape
        col_tile = min(cols_b, MM_TILE)
    elif b_kind == "rows":
        rows_b, cols_b = b.shape[0] * b.shape[2], b.shape[3]
        col_tile = min(cols_b, MM_TILE)
    else:
        rows_b, cols_b = b.shape[2], b.shape[0] * b.shape[3]
        col_tile = b.shape[3]
    row_tile = min(rows_b, MM_TILE)
    (N, K2, tn, tk) = (rows_b, cols_b, row_tile, col_tile) if tb else (cols_b, rows_b, col_tile, row_tile)
    assert K == K2, (a.shape, b.shape)
    tm = min(M, MM_TILE)
    if out_kind == "cols":
        assert b_kind is None and not tb
        tn = col_tile = out_buf.shape[3]
    assert M % tm == 0 and N % tn == 0 and K % tk == 0
    nk = K // tk
    a_spec = pl.BlockSpec((tk, tm), lambda i, j, k: (k, i)) if ta else pl.BlockSpec((tm, tk), lambda i, j, k: (i, k))

    def rc(j, k):
        return (j, k) if tb else (k, j)

    if b_kind is None:
        b_spec = pl.BlockSpec((row_tile, col_tile), lambda i, j, k: rc(j, k))
    elif b_kind == "rows":
        per = row_tile // b.shape[2]
        b_spec = pl.BlockSpec((per, None, b.shape[2], col_tile), lambda i, j, k: (rc(j, k)[0], layer, 0, rc(j, k)[1]))
    else:
        b_spec = pl.BlockSpec((None, None, row_tile, col_tile), lambda i, j, k: (rc(j, k)[1], layer, rc(j, k)[0], 0))
    o_spec = pl.BlockSpec((tm, tn), lambda i, j, k: (i, j))
    dn = (((0 if ta else 1,), (1 if tb else 0,)), ((), ()))
    extra = [e for e in (res, mul2) if e is not None]
    n_out = 2 if relu2 else 1
    n_in = 2 + len(extra) + (1 if out_buf is not None else 0)

    def body(*refs):
        a_ref, b_ref = refs[0], refs[1]
        extra_refs = refs[2:2 + len(extra)]
        outs = refs[n_in:n_in + n_out]
        acc = refs[-1] if nk > 1 else None

        def finish(v):
            if res is not None:
                v = extra_refs[0][...] + v
            elif mul2 is not None:
                v = v * (2.0 * extra_refs[0][...].astype(F32))
            if relu2:
                r = jnp.maximum(v, 0.0)
                outs[0][...] = r.astype(BF16)
                outs[1][...] = (r * r).astype(BF16)
            else:
                outs[0][...] = v.astype(outs[0].dtype).reshape(outs[0].shape)

        bv = b_ref[...].astype(BF16)
        part = lax.dot_general(a_ref[...].astype(BF16), bv.reshape(row_tile, col_tile), dn, preferred_element_type=F32)
        if nk == 1:
            finish(part)
        else:
            k = pl.program_id(2)

            @pl.when(k == 0)
            def _():
                acc[...] = part

            @pl.when(k > 0)
            def _():
                acc[...] += part

            @pl.when(k == nk - 1)
            def _():
                finish(acc[...])

    in_specs = [a_spec, b_spec] + [o_spec] * len(extra)
    operands = [a, b, *extra]
    aliases = {}
    if out_kind is None:
        out_shape = jax.ShapeDtypeStruct((M, N), BF16 if relu2 else out_dtype)
        out_specs = o_spec
    else:
        s = out_buf.shape[2]
        if out_kind == "rows":
            out_specs = pl.BlockSpec((tm // s, None, s, tn), lambda i, j, k: (i, layer, 0, j))
        else:
            out_specs = pl.BlockSpec((None, None, tm, tn), lambda i, j, k: (j, layer, i, 0))
        out_shape = jax.ShapeDtypeStruct(out_buf.shape, out_buf.dtype)
        in_specs.append(ANY)
        operands.append(out_buf)
        aliases = {len(operands) - 1: 0}
    if relu2:
        out_shape, out_specs = (out_shape, out_shape), (out_specs, out_specs)
    return pl.pallas_call(
        body, name=name, grid=(M // tm, N // tn, nk), in_specs=in_specs, out_specs=out_specs, out_shape=out_shape,
        scratch_shapes=[pltpu.VMEM((tm, tn), F32)] if nk > 1 else [], input_output_aliases=aliases,
        compiler_params=_params("parallel", "parallel", "arbitrary"),
    )(*operands)


def _row_tile(rows, d):
    t = min(rows, (512 * 1024) // d)
    while rows % t or (t != rows and t % 16):
        t -= 1
    return t


def _rms_fwd(x, g, *, name, out_dtype=BF16):
    R, D = x.shape
    tr = _row_tile(R, D)

    def body(x_ref, g_ref, o_ref):
        xv = x_ref[...].astype(F32)
        r = lax.rsqrt(jnp.mean(xv * xv, axis=-1, keepdims=True) + EPS)
        o_ref[...] = ((xv * r) * g_ref[...]).astype(o_ref.dtype)

    return pl.pallas_call(
        body, name=name, grid=(R // tr,),
        in_specs=[pl.BlockSpec((tr, D), lambda i: (i, 0)), pl.BlockSpec((1, D), lambda i: (0, 0))],
        out_specs=pl.BlockSpec((tr, D), lambda i: (i, 0)), out_shape=jax.ShapeDtypeStruct((R, D), out_dtype),
        compiler_params=_params("parallel"),
    )(x, g.reshape(1, D))


def _rms_bwd(x, g, dys, *, name, res=None, want_dx=True):
    R, D = x.shape
    tr = _row_tile(R, D)
    n_dy = len(dys)
    has_res = res is not None

    def body(*refs):
        x_ref, g_ref = refs[0], refs[1]
        dy_refs = refs[2:2 + n_dy]
        res_ref = refs[2 + n_dy] if has_res else None
        outs = refs[2 + n_dy + (1 if has_res else 0):]
        dg_ref = outs[-1]
        i = pl.program_id(0)
        xv = x_ref[...].astype(F32)
        dy = dy_refs[0][...].astype(F32)
        for extra in dy_refs[1:]:
            dy = dy + extra[...].astype(F32)
        r = lax.rsqrt(jnp.mean(xv * xv, axis=-1, keepdims=True) + EPS)
        xh = xv * r
        part = jnp.sum(dy * xh, axis=0, keepdims=True)

        @pl.when(i == 0)
        def _():
            dg_ref[...] = part

        @pl.when(i > 0)
        def _():
            dg_ref[...] += part

        if want_dx:
            gdy = dy * g_ref[...]
            dx = r * (gdy - xh * jnp.mean(gdy * xh, axis=-1, keepdims=True))
            if has_res:
                dx = res_ref[...] + dx
            outs[0][...] = dx

    row = pl.BlockSpec((tr, D), lambda i: (i, 0))
    vec = pl.BlockSpec((1, D), lambda i: (0, 0))
    out_shape = [jax.ShapeDtypeStruct((1, D), F32)]
    out_specs = [vec]
    if want_dx:
        out_shape = [jax.ShapeDtypeStruct((R, D), F32)] + out_shape
        out_specs = [row] + out_specs
    outs = pl.pallas_call(
        body, name=name, grid=(R // tr,),
        in_specs=[row, vec] + [row] * (n_dy + (1 if has_res else 0)), out_specs=out_specs, out_shape=out_shape,
        compiler_params=_params("arbitrary"),
    )(x, g.reshape(1, D), *dys, *([res] if has_res else []))
    if want_dx:
        return outs[0], outs[1].reshape(D)
    return outs[0].reshape(D)


def _add(a, b, *, name):
    R, D = a.shape
    tr = _row_tile(R, D)

    def body(a_ref, b_ref, o_ref):
        o_ref[...] = a_ref[...] + b_ref[...]

    row = pl.BlockSpec((tr, D), lambda i: (i, 0))
    return pl.pallas_call(body, name=name, grid=(R // tr,), in_specs=[row, row], out_specs=row,
                          out_shape=jax.ShapeDtypeStruct((R, D), a.dtype), compiler_params=_params("parallel"))(a, b)


POOL_TILE = 512


def _pool_window(group):
    return lax.shift_left(jnp.int32(2), group)


def _pool_diff(u_ref, halo_ref, group, tile):
    first = tile == 0
    halo = jnp.where(first, 0.0, halo_ref[...])
    ext = jnp.concatenate([halo, u_ref[...]], axis=0)
    n = ext.shape[0]
    s1 = ext + pltpu.roll(ext, 1, 0)
    s2 = s1 + pltpu.roll(s1, 2, 0)
    s3 = s2 + pltpu.roll(s2, 4, 0)
    s4 = s3 + pltpu.roll(s3, 8, 0)
    ws = jnp.where(group == 0, s1, jnp.where(group == 1, s2, jnp.where(group == 2, s3, s4)))[POOL_HALO:n]
    t = tile * POOL_TILE + lax.broadcasted_iota(jnp.int32, (POOL_TILE, 1), 0)
    cnt = jnp.minimum(t + 1, _pool_window(group)).astype(F32)
    return ws / cnt - u_ref[...], cnt


def _pool_specs():
    per_tile = POOL_TILE // POOL_HALO
    cur = pl.BlockSpec((None, POOL_TILE, POOL_GROUP_DIM), lambda g, i: (g, i, 0))
    prev = pl.BlockSpec((None, POOL_HALO, POOL_GROUP_DIM), lambda g, i: (g, jnp.maximum(i * per_tile - 1, 0), 0))
    pw = pl.BlockSpec((None, POOL_GROUP_DIM, POOL_GROUP_DIM), lambda g, i: (g, 0, 0))
    vec = pl.BlockSpec((None, 1, POOL_GROUP_DIM), lambda g, i: (g, 0, 0))
    return cur, prev, pw, vec


def _pool_fwd(u, pw, scale, *, name):
    G, T, C = u.shape
    assert T % POOL_TILE == 0
    cur, prev, pw_spec, vec = _pool_specs()

    def body(u_ref, halo_ref, pw_ref, sc_ref, o_ref):
        d, _ = _pool_diff(u_ref, halo_ref, pl.program_id(0), pl.program_id(1))
        mixed = jnp.dot(d.astype(BF16), pw_ref[...].astype(BF16), preferred_element_type=F32)
        o_ref[...] = (mixed * sc_ref[...]).astype(o_ref.dtype)

    return pl.pallas_call(
        body, name=name, grid=(G, T // POOL_TILE), in_specs=[cur, prev, pw_spec, vec], out_specs=cur,
        out_shape=jax.ShapeDtypeStruct((G, T, C), BF16), compiler_params=_params("parallel", "parallel"),
    )(u, u, pw, scale)


def _pool_bwd(u, pw, scale, dout, *, name):
    G, T, C = u.shape
    nt = T // POOL_TILE
    per_tile = POOL_TILE // POOL_HALO
    cur, prev, pw_spec, vec = _pool_specs()
    nxt = pl.BlockSpec((None, POOL_HALO, C), lambda g, i: (g, jnp.minimum((i + 1) * per_tile, nt * per_tile - 1), 0))

    def body(u_ref, halo_ref, pw_ref, sc_ref, do_ref, donext_ref, du_ref, dpw_ref, dsc_ref):
        group, tile = pl.program_id(0), pl.program_id(1)
        d, cnt = _pool_diff(u_ref, halo_ref, group, tile)
        pwb = pw_ref[...].astype(BF16)
        db = d.astype(BF16)
        mixed = jnp.dot(db, pwb, preferred_element_type=F32)
        dout = do_ref[...].astype(F32)
        dsc = jnp.sum(dout * mixed, axis=0, keepdims=True)
        sc = sc_ref[...]
        dmix = (dout * sc).astype(BF16)
        dpw = lax.dot_general(db, dmix, (((0,), (0,)), ((), ())), preferred_element_type=F32)

        @pl.when(tile == 0)
        def _():
            dpw_ref[...] = dpw
            dsc_ref[...] = dsc

        @pl.when(tile > 0)
        def _():
            dpw_ref[...] += dpw
            dsc_ref[...] += dsc

        last = tile == nt - 1
        dnext = jnp.where(last, 0.0, donext_ref[...].astype(F32))
        dmix_ext = jnp.concatenate([dmix, (dnext * sc).astype(BF16)], axis=0)
        dd_ext = lax.dot_general(dmix_ext, pwb, (((1,), (1,)), ((), ())), preferred_element_type=F32)
        window = _pool_window(group).astype(F32)
        cnt_ext = jnp.concatenate([cnt, jnp.broadcast_to(window, (POOL_HALO, 1))], axis=0)
        q = dd_ext / cnt_ext
        n = q.shape[0]
        r1 = q + pltpu.roll(q, n - 1, 0)
        r2 = r1 + pltpu.roll(r1, n - 2, 0)
        r3 = r2 + pltpu.roll(r2, n - 4, 0)
        r4 = r3 + pltpu.roll(r3, n - 8, 0)
        back = jnp.where(group == 0, r1, jnp.where(group == 1, r2, jnp.where(group == 2, r3, r4)))
        du_ref[...] = back[0:POOL_TILE] - dd_ext[0:POOL_TILE]

    return pl.pallas_call(
        body, name=name, grid=(G, nt), in_specs=[cur, prev, pw_spec, vec, cur, nxt],
        out_specs=(cur, pw_spec, vec),
        out_shape=(jax.ShapeDtypeStruct((G, T, C), F32), jax.ShapeDtypeStruct((G, C, C), F32),
                   jax.ShapeDtypeStruct((G, 1, C), F32)),
        compiler_params=_params("arbitrary", "arbitrary"),
    )(u, u, pw, scale, dout, dout)


def _softmax(q, k, bias, valid, sink):
    s = lax.dot_general(q, k, (((1,), (1,)), ((), ())), preferred_element_type=F32) * SCALE
    if bias is not None:
        s = s - bias
    if valid is not None:
        s = jnp.where(valid, s, NEG)
    m = jnp.max(s, axis=-1, keepdims=True)
    if sink is not None:
        m = jnp.maximum(m, sink)
    e = jnp.exp(s - m)
    z = jnp.sum(e, axis=-1, keepdims=True)
    if sink is None:
        return e * (1.0 / z), None
    es = jnp.exp(sink - m)
    inv = 1.0 / (z + es)
    return e * inv, es * inv


def _swa_terms(sink_ref, kvh, blk):
    rows = SWA_GROUP * BLOCK
    row = lax.broadcasted_iota(jnp.int32, (rows, 1), 0)
    grp = row // BLOCK
    head = (kvh * SWA_GROUP + grp + 1).astype(F32)
    slope = jnp.exp(head * (-8.0 * math.log(2.0) / SWA_Q_HEADS))
    qi = lax.broadcasted_iota(jnp.int32, (rows, 2 * BLOCK), 0) % BLOCK
    kj = lax.broadcasted_iota(jnp.int32, (rows, 2 * BLOCK), 1)
    dist = qi + BLOCK - kj
    valid = (dist >= 0) & (dist < BLOCK) & ((blk > 0) | (kj >= BLOCK))
    bias = slope * dist.astype(F32)
    s0, s1, s2 = (sink_ref[kvh * SWA_GROUP + g] for g in range(SWA_GROUP))
    sink = jnp.where(grp == 0, s0, jnp.where(grp == 1, s1, s2))
    return bias, valid, sink, grp


def _swa_fwd(q, k, v, sinks, *, name):
    H, G, T, hd = q.shape
    nb = T // BLOCK
    rows = G * BLOCK

    def body(sink_ref, q_ref, kp_ref, kc_ref, vp_ref, vc_ref, o_ref):
        kvh, blk = pl.program_id(0), pl.program_id(1)
        bias, valid, sink, _ = _swa_terms(sink_ref, kvh, blk)
        kk = jnp.concatenate([kp_ref[...], kc_ref[...]], axis=0)
        vv = jnp.concatenate([vp_ref[...], vc_ref[...]], axis=0)
        p, _ = _softmax(q_ref[...].reshape(rows, hd), kk, bias, valid, sink)
        o = jnp.dot(p.astype(BF16), vv, preferred_element_type=F32)
        o_ref[...] = o.reshape(G, BLOCK, hd).astype(o_ref.dtype)

    qs = pl.BlockSpec((None, G, BLOCK, hd), lambda h, n: (h, 0, n, 0))
    prev = pl.BlockSpec((None, BLOCK, hd), lambda h, n: (h, jnp.maximum(n - 1, 0), 0))
    cur = pl.BlockSpec((None, BLOCK, hd), lambda h, n: (h, n, 0))
    return pl.pallas_call(
        body, name=name, grid=(H, nb),
        in_specs=[pl.BlockSpec(memory_space=pltpu.SMEM), qs, prev, cur, prev, cur], out_specs=qs,
        out_shape=jax.ShapeDtypeStruct((H, G, T, hd), BF16), compiler_params=_params("parallel", "parallel"),
    )(sinks, q, k, k, v, v)


def _swa_bwd(q, k, v, sinks, do, *, name):
    H, G, T, hd = q.shape
    nb = T // BLOCK
    rows = G * BLOCK

    def body(sink_ref, q_ref, do_ref, kp_ref, kc_ref, vp_ref, vc_ref, dq_ref, dk_ref, dv_ref, ds_ref, ck, cv):
        kvh, blk = pl.program_id(0), pl.program_id(1)

        @pl.when(blk == 0)
        def _():
            ck[...] = jnp.zeros_like(ck)
            cv[...] = jnp.zeros_like(cv)
            ds_ref[...] = jnp.zeros_like(ds_ref)

        @pl.when(blk < nb)
        def _():
            bias, valid, sink, grp = _swa_terms(sink_ref, kvh, blk)
            kk = jnp.concatenate([kp_ref[...], kc_ref[...]], axis=0)
            vv = jnp.concatenate([vp_ref[...], vc_ref[...]], axis=0)
            qq = q_ref[...].reshape(rows, hd)
            dout = do_ref[...].reshape(rows, hd)
            p, ps = _softmax(qq, kk, bias, valid, sink)
            dp = lax.dot_general(dout, vv, (((1,), (1,)), ((), ())), preferred_element_type=F32)
            dsum = jnp.sum(p * dp, axis=-1, keepdims=True)
            ds = (p * (dp - dsum)).astype(BF16)
            dq = jnp.dot(ds, kk, preferred_element_type=F32) * SCALE
            dq_ref[...] = dq.reshape(G, BLOCK, hd)
            dk = lax.dot_general(ds, qq, (((0,), (0,)), ((), ())), preferred_element_type=F32) * SCALE
            dv = lax.dot_general(p.astype(BF16), dout, (((0,), (0,)), ((), ())), preferred_element_type=F32)
            dk_ref[...] = ck[...] + dk[0:BLOCK]
            dv_ref[...] = cv[...] + dv[0:BLOCK]
            ck[...] = dk[BLOCK:2 * BLOCK]
            cv[...] = dv[BLOCK:2 * BLOCK]
            dsink = -(ps * dsum)
            lane = lax.broadcasted_iota(jnp.int32, (1, 128), 1)
            acc = jnp.zeros((1, 128), F32)
            for g in range(G):
                acc = acc + jnp.where(lane == g, jnp.sum(jnp.where(grp == g, dsink, 0.0)), 0.0)
            ds_ref[...] += acc

        @pl.when(blk == nb)
        def _():
            dk_ref[...] = ck[...]
            dv_ref[...] = cv[...]

    def at(n):
        return jnp.minimum(n, nb - 1)

    qs = pl.BlockSpec((None, G, BLOCK, hd), lambda h, n: (h, 0, at(n), 0))
    prev = pl.BlockSpec((None, BLOCK, hd), lambda h, n: (h, jnp.maximum(at(n) - 1, 0), 0))
    cur = pl.BlockSpec((None, BLOCK, hd), lambda h, n: (h, at(n), 0))
    late = pl.BlockSpec((None, BLOCK, hd), lambda h, n: (h, jnp.maximum(n - 1, 0), 0))
    dsink_spec = pl.BlockSpec((None, 1, 128), lambda h, n: (h, 0, 0))
    return pl.pallas_call(
        body, name=name, grid=(H, nb + 1),
        in_specs=[pl.BlockSpec(memory_space=pltpu.SMEM), qs, qs, prev, cur, prev, cur],
        out_specs=(qs, late, late, dsink_spec),
        out_shape=(jax.ShapeDtypeStruct((H, G, T, hd), F32), jax.ShapeDtypeStruct((H, T, hd), F32),
                   jax.ShapeDtypeStruct((H, T, hd), F32), jax.ShapeDtypeStruct((H, 1, 128), F32)),
        scratch_shapes=[pltpu.VMEM((BLOCK, hd), F32), pltpu.VMEM((BLOCK, hd), F32)],
        compiler_params=_params("arbitrary", "arbitrary"),
    )(sinks, q, do, k, k, v, v)


MEM_Q_TILE = 512


def _mem_fwd(q, k, v, *, name):
    H, T, hd = q.shape
    M = k.shape[1]
    tq = min(T, MEM_Q_TILE)

    def body(q_ref, k_ref, v_ref, o_ref):
        p, _ = _softmax(q_ref[...], k_ref[...], None, None, None)
        o_ref[...] = jnp.dot(p.astype(BF16), v_ref[...], preferred_element_type=F32).astype(o_ref.dtype)

    qs = pl.BlockSpec((None, tq, hd), lambda h, i: (h, i, 0))
    ks = pl.BlockSpec((None, M, hd), lambda h, i: (h, 0, 0))
    return pl.pallas_call(body, name=name, grid=(H, T // tq), in_specs=[qs, ks, ks], out_specs=qs,
                          out_shape=jax.ShapeDtypeStruct((H, T, hd), BF16),
                          compiler_params=_params("parallel", "parallel"))(q, k, v)


def _mem_bwd(q, k, v, do, *, name):
    H, T, hd = q.shape
    M = k.shape[1]
    tq = min(T, MEM_Q_TILE)

    def body(q_ref, do_ref, k_ref, v_ref, dq_ref, dk_ref, dv_ref):
        i = pl.program_id(1)
        qq, kk, vv, dout = q_ref[...], k_ref[...], v_ref[...], do_ref[...]
        p, _ = _softmax(qq, kk, None, None, None)
        dp = lax.dot_general(dout, vv, (((1,), (1,)), ((), ())), preferred_element_type=F32)
        dsum = jnp.sum(p * dp, axis=-1, keepdims=True)
        ds = (p * (dp - dsum)).astype(BF16)
        dq_ref[...] = jnp.dot(ds, kk, preferred_element_type=F32) * SCALE
        dk = lax.dot_general(ds, qq, (((0,), (0,)), ((), ())), preferred_element_type=F32) * SCALE
        dv = lax.dot_general(p.astype(BF16), dout, (((0,), (0,)), ((), ())), preferred_element_type=F32)

        @pl.when(i == 0)
        def _():
            dk_ref[...] = dk
            dv_ref[...] = dv

        @pl.when(i > 0)
        def _():
            dk_ref[...] += dk
            dv_ref[...] += dv

    qs = pl.BlockSpec((None, tq, hd), lambda h, i: (h, i, 0))
    ks = pl.BlockSpec((None, M, hd), lambda h, i: (h, 0, 0))
    return pl.pallas_call(
        body, name=name, grid=(H, T // tq), in_specs=[qs, qs, ks, ks], out_specs=(qs, ks, ks),
        out_shape=(jax.ShapeDtypeStruct((H, T, hd), F32), jax.ShapeDtypeStruct((H, M, hd), F32),
                   jax.ShapeDtypeStruct((H, M, hd), F32)),
        compiler_params=_params("arbitrary", "arbitrary"),
    )(q, do, k, v)


def _loss(y, target, *, name):
    T, D = y.shape
    tr = _row_tile(T, D)

    def body(y_ref, t_ref, l_ref, dy_ref):
        i = pl.program_id(0)
        err = y_ref[...] - t_ref[...]
        dy_ref[...] = err / float(D)
        part = jnp.full((8, 128), 0.5 * jnp.sum(jnp.mean(err * err, axis=-1)), F32)

        @pl.when(i == 0)
        def _():
            l_ref[...] = part

        @pl.when(i > 0)
        def _():
            l_ref[...] += part

    row = pl.BlockSpec((tr, D), lambda i: (i, 0))
    return pl.pallas_call(
        body, name=name, grid=(T // tr,), in_specs=[row, row],
        out_specs=(pl.BlockSpec((8, 128), lambda i: (0, 0)), row),
        out_shape=(jax.ShapeDtypeStruct((8, 128), F32), jax.ShapeDtypeStruct((T, D), F32)),
        compiler_params=_params("arbitrary"),
    )(y, target)


def _position():
    return lax.axis_index("x"), lax.axis_index("y"), lax.axis_index("c")


def _all_gather(arrays, *, name):
    n = len(arrays)

    def body(*refs):
        srcs, outs = refs[:n], refs[n:2 * n]
        send_sems, recv_sems, local_sems = refs[2 * n:]
        x, y, c = _position()
        me, sibling = (x, y, c), (x, y, 1 - c)
        chips = [(1 - x, y), (x, 1 - y), (1 - x, 1 - y)]

        def slot(a, px, py, pc):
            return outs[a].at[4 * px + 2 * py + pc]

        def copy(a, k, block, to, src=None):
            return pltpu.make_async_remote_copy(
                src_ref=slot(a, *block) if src is None else src, dst_ref=slot(a, *block),
                send_sem=send_sems.at[a, k], recv_sem=recv_sems.at[a, k], device_id=to, device_id_type=MESH)

        mine = [pltpu.make_async_copy(srcs[a], slot(a, *me), local_sems.at[a]) for a in range(n)]
        for cp in mine:
            cp.start()
        first, passed = [], []
        for a in range(n):
            first.append(copy(a, 0, me, sibling, src=srcs[a]))
            first += [copy(a, 1 + j, me, (*chip, c), src=srcs[a]) for j, chip in enumerate(chips)]
        for cp in first:
            cp.start()
        for a in range(n):
            for j, chip in enumerate(chips):
                copy(a, 1 + j, (*chip, c), me).wait_recv()
                fwd = copy(a, 4 + j, (*chip, c), sibling)
                fwd.start()
                passed.append(fwd)
        for a in range(n):
            copy(a, 0, sibling, me).wait_recv()
            for j, chip in enumerate(chips):
                copy(a, 4 + j, (*chip, 1 - c), me).wait_recv()
        for cp in first + passed:
            cp.wait_send()
        for cp in mine:
            cp.wait()

    return pl.pallas_call(
        body, name=name, in_specs=[ANY] * n, out_specs=[ANY] * n,
        out_shape=[jax.ShapeDtypeStruct((N_DEV,) + a.shape, a.dtype) for a in arrays],
        scratch_shapes=[pltpu.SemaphoreType.DMA((n, 7)), pltpu.SemaphoreType.DMA((n, 7)), pltpu.SemaphoreType.DMA((n,))],
    )(*arrays)


def _sibling_exchange(by_core, whole, *, name):
    n1, n = len(by_core), len(by_core) + len(whole)

    def body(*refs):
        srcs, outs = refs[:n], refs[n:2 * n]
        send_sems, recv_sems = refs[2 * n:]
        x, y, c = _position()
        copies = [
            pltpu.make_async_remote_copy(src_ref=srcs[a].at[:, 1 - c] if a < n1 else srcs[a], dst_ref=outs[a],
                                         send_sem=send_sems.at[a], recv_sem=recv_sems.at[a], device_id=(x, y, 1 - c),
                                         device_id_type=MESH)
            for a in range(n)]
        for cp in copies:
            cp.start()
        for cp in copies:
            cp.wait()

    out_shape = [jax.ShapeDtypeStruct(a.shape[:1] + a.shape[2:], a.dtype) for a in by_core]
    out_shape += [jax.ShapeDtypeStruct(a.shape, a.dtype) for a in whole]
    outs = pl.pallas_call(
        body, name=name, in_specs=[ANY] * n, out_specs=[ANY] * n, out_shape=out_shape,
        scratch_shapes=[pltpu.SemaphoreType.DMA((n,)), pltpu.SemaphoreType.DMA((n,))],
    )(*by_core, *whole)
    return outs[:n1], outs[n1:]


def _chip_exchange(per_chip, whole, *, name):
    n1, n = len(per_chip), len(per_chip) + len(whole)

    def body(*refs):
        srcs, outs = refs[:n], refs[n:2 * n]
        send_sems, recv_sems, local_sems = refs[2 * n:]
        x, y, c = _position()
        my_chip = 2 * x + y
        chips = [(1 - x, y), (x, 1 - y), (1 - x, 1 - y)]

        def src(a, chip):
            return srcs[a].at[chip] if a < n1 else srcs[a]

        local = [pltpu.make_async_copy(src(a, my_chip), outs[a].at[my_chip], local_sems.at[a]) for a in range(n)]
        for cp in local:
            cp.start()
        copies = [
            pltpu.make_async_remote_copy(src_ref=src(a, 2 * px + py), dst_ref=outs[a].at[my_chip],
                                         send_sem=send_sems.at[a, j], recv_sem=recv_sems.at[a, j], device_id=(px, py, c),
                                         device_id_type=MESH)
            for a in range(n) for j, (px, py) in enumerate(chips)]
        for cp in copies:
            cp.start()
        for cp in copies:
            cp.wait()
        for cp in local:
            cp.wait()

    out_shape = [jax.ShapeDtypeStruct(a.shape, a.dtype) for a in per_chip]
    out_shape += [jax.ShapeDtypeStruct((N_CHIP,) + a.shape, a.dtype) for a in whole]
    outs = pl.pallas_call(
        body, name=name, in_specs=[ANY] * n, out_specs=[ANY] * n, out_shape=out_shape,
        scratch_shapes=[pltpu.SemaphoreType.DMA((n, 3)), pltpu.SemaphoreType.DMA((n, 3)), pltpu.SemaphoreType.DMA((n,))],
    )(*per_chip, *whole)
    return outs[:n1], outs[n1:]


def _view2d(shape):
    return math.prod(shape[:-1]), shape[-1]


def _pair_sum(mine, other, core, *, name, out_dtype):
    by_core = mine.ndim == 4
    n, w = other.shape[-2:]
    tr = _row_tile(n, w * 2)
    lead = other.shape[0] if by_core else 1

    def body(core_ref, a_ref, b_ref, o_ref):
        o_ref[...] = (a_ref[...].astype(F32) + b_ref[...].astype(F32)).astype(o_ref.dtype)

    if by_core:
        a_spec = pl.BlockSpec((None, None, tr, w), lambda j, i, core_ref: (j, core_ref[0], i, 0))
        o_spec = pl.BlockSpec((None, tr, w), lambda j, i, core_ref: (j, i, 0))
    else:
        a_spec = o_spec = pl.BlockSpec((tr, w), lambda j, i, core_ref: (i, 0))
    grid_spec = pltpu.PrefetchScalarGridSpec(num_scalar_prefetch=1, grid=(lead, n // tr), in_specs=[a_spec, o_spec],
                                             out_specs=o_spec)
    return pl.pallas_call(body, name=name, grid_spec=grid_spec, out_shape=jax.ShapeDtypeStruct(other.shape, out_dtype),
                          compiler_params=_params("parallel", "parallel"))(core.reshape(1), mine, other)


def _adamw(parts, w, m, v, *, name):
    n_parts, R, W = parts.shape
    tr = _row_tile(R, W * 2)

    def body(p_ref, w_ref, m_ref, v_ref, g_out, d_out, m_out, v_out):
        g = p_ref[0].astype(F32)
        for j in range(1, n_parts):
            g = g + p_ref[j].astype(F32)
        m_new = ADAM_B1 * m_ref[...] + (1.0 - ADAM_B1) * g
        v_new = ADAM_B2 * v_ref[...] + (1.0 - ADAM_B2) * (g * g)
        m_hat = m_new / (1.0 - ADAM_B1 ** ADAM_STEP)
        v_hat = v_new / (1.0 - ADAM_B2 ** ADAM_STEP)
        g_out[...] = g
        d_out[...] = -ADAM_LR * (m_hat / (jnp.sqrt(v_hat) + ADAM_EPS) + ADAM_WD * w_ref[...])
        m_out[...] = m_new
        v_out[...] = v_new

    row = pl.BlockSpec((tr, W), lambda i: (i, 0))
    out = jax.ShapeDtypeStruct((R, W), F32)
    return pl.pallas_call(
        body, name=name, grid=(R // tr,), in_specs=[pl.BlockSpec((n_parts, tr, W), lambda i: (0, i, 0)), row, row, row],
        out_specs=(row, row, row, row), out_shape=(out, out, out, out), compiler_params=_params("parallel"),
    )(parts, w, m, v)


SMALL_ROWS = 608


def _pack_small(p):
    flat = jnp.concatenate([p[n].reshape(-1).astype(F32) for n in SMALL_NAMES])
    return jnp.pad(flat, (0, SMALL_ROWS * PACK_W - flat.shape[0])).reshape(SMALL_ROWS, PACK_W)


def _unpack_small(buf, like):
    out, at = {}, 0
    flat = buf.reshape(-1)
    for n in SMALL_NAMES:
        size = math.prod(like[n].shape)
        out[n] = flat[at:at + size].reshape(like[n].shape)
        at += size
    return out


def _heads(a, nh):
    T = a.shape[0]
    return a.reshape(T, nh, HEAD_DIM).transpose(1, 0, 2).reshape(nh * T, HEAD_DIM)


def _unheads(a, nh):
    a = a.reshape(nh, -1, HEAD_DIM)
    return a.transpose(1, 0, 2).reshape(a.shape[1], nh * HEAD_DIM)


def _groups(a):
    T = a.shape[0]
    return a.reshape(T, POOL_GROUPS, POOL_GROUP_DIM).transpose(1, 0, 2)


def _ungroups(a):
    return a.transpose(1, 0, 2).reshape(a.shape[1], MAIN_WIDTH)


def _local_step(x, mem, target, w, p):
    T = x.shape[0]
    M = mem.shape[0]
    saved = []
    h = x
    kn = vv = k_raw = h_kv = hn_kv = None
    for l in range(DEPTH):
        s = {}
        if l == N_A:
            h_kv = h
            hn_kv = _rms_fwd(h, p["kv_norm"], name="kv_norm_fwd")
            kv = _mm(hn_kv, w["w_kv"], b_kind="rows", name="kv_proj")
            k_raw = _heads(kv[:, :KV_HALF], SWA_KV_HEADS)
            kn = _rms_fwd(k_raw, p["k_norm"], name="k_norm_fwd").reshape(SWA_KV_HEADS, T, HEAD_DIM)
            vv = _heads(kv[:, KV_HALF:], SWA_KV_HEADS).astype(BF16).reshape(SWA_KV_HEADS, T, HEAD_DIM)
        s["h"] = h
        s["xn1"] = _rms_fwd(h, p["norm_mix"][l], name="norm_mix_fwd")
        proj = _mm(s["xn1"], w["w_in"], b_kind="rows", layer=l, name="in_proj")
        s["mq_raw"] = _heads(proj[:, MAIN_WIDTH:], MEM_HEADS)
        s["mqn"] = _rms_fwd(s["mq_raw"], p["mem_q_norm"][l], name="mem_q_norm_fwd").reshape(MEM_HEADS, T, HEAD_DIM)
        s["memn"] = _rms_fwd(mem, p["mem_norm"][l], name="mem_norm_fwd")
        mkv = _mm(s["memn"], w["w_mem_kv"], b_kind="rows", layer=l, name="mem_kv_proj")
        s["mk_raw"] = _heads(mkv[:, :MEM_WIDTH], MEM_HEADS)
        s["mkn"] = _rms_fwd(s["mk_raw"], p["mem_k_norm"][l], name="mem_k_norm_fwd").reshape(MEM_HEADS, M, HEAD_DIM)
        s["mvv"] = _heads(mkv[:, MEM_WIDTH:], MEM_HEADS).astype(BF16).reshape(MEM_HEADS, M, HEAD_DIM)
        mem_out = _unheads(_mem_fwd(s["mqn"], s["mkn"], s["mvv"], name="mem_attn_fwd"), MEM_HEADS)
        if l < N_A:
            s["u"] = _groups(proj[:, :MAIN_WIDTH])
            s["pw"] = p["pool_w"][l]
            s["ps"] = p["pool_scale"][l].reshape(POOL_GROUPS, 1, POOL_GROUP_DIM)
            main_out = _ungroups(_pool_fwd(s["u"], s["pw"], s["ps"], name="pool_fwd"))
        else:
            j = l - N_A
            s["q_raw"] = _heads(proj[:, :MAIN_WIDTH], SWA_Q_HEADS)
            s["qn"] = _rms_fwd(s["q_raw"], p["q_norm"][j], name="q_norm_fwd").reshape(SWA_KV_HEADS, SWA_GROUP, T, HEAD_DIM)
            main_out = _unheads(_swa_fwd(s["qn"], kn, vv, p["sinks"][j], name="swa_fwd"), SWA_Q_HEADS)
        s["cat"] = jnp.concatenate([main_out, mem_out], axis=-1)
        s["h1"] = _mm(s["cat"], w["w_out"], b_kind="rows", layer=l, res=h, name="out_proj")
        s["xn2"] = _rms_fwd(s["h1"], p["norm_mlp"][l], name="norm_mlp_fwd")
        s["r"], s["a"] = _mm(s["xn2"], w["w_up"], b_kind="cols", layer=l, relu2=True, name="mlp_up")
        h = _mm(s["a"], w["w_down"], b_kind="rows", layer=l, res=s["h1"], name="mlp_down")
        saved.append(s)

    loss, dh = _loss(h, target, name="loss_head")

    g = {n: [None] * DEPTH for n in ("norm_mix", "mem_norm", "mem_q_norm", "mem_k_norm", "norm_mlp")}
    gb = {n: lax.empty(w[n].shape, BF16) for n in GATHERED}
    g.update({n: [None] * N_A for n in ("pool_w", "pool_scale", "q_norm", "sinks")})
    dkn = dvv = None
    for l in reversed(range(DEPTH)):
        s = saved[l]
        gb["w_down"] = _mm(s["a"], dh, ta=True, out_kind="rows", layer=l, out_buf=gb["w_down"], name="mlp_down_dw")
        du = _mm(dh, w["w_down"], tb=True, b_kind="rows", layer=l, mul2=s["r"], out_dtype=BF16, name="mlp_down_dx")
        gb["w_up"] = _mm(s["xn2"], du, ta=True, out_kind="cols", layer=l, out_buf=gb["w_up"], name="mlp_up_dw")
        dxn2 = _mm(du, w["w_up"], tb=True, b_kind="cols", layer=l, name="mlp_up_dx")
        dh1, g["norm_mlp"][l] = _rms_bwd(s["h1"], p["norm_mlp"][l], [dxn2], res=dh, name="norm_mlp_bwd")
        gb["w_out"] = _mm(s["cat"], dh1, ta=True, out_kind="rows", layer=l, out_buf=gb["w_out"], name="out_proj_dw")
        dcat = _mm(dh1, w["w_out"], tb=True, b_kind="rows", layer=l, name="out_proj_dx")
        dmem_out = _heads(dcat[:, MAIN_WIDTH:], MEM_HEADS).astype(BF16).reshape(MEM_HEADS, T, HEAD_DIM)
        dmqn, dmkn, dmvv = _mem_bwd(s["mqn"], s["mkn"], s["mvv"], dmem_out, name="mem_attn_bwd")
        dmq_raw, g["mem_q_norm"][l] = _rms_bwd(s["mq_raw"], p["mem_q_norm"][l], [dmqn.reshape(MEM_HEADS * T, HEAD_DIM)],
                                               name="mem_q_norm_bwd")
        dmk_raw, g["mem_k_norm"][l] = _rms_bwd(s["mk_raw"], p["mem_k_norm"][l], [dmkn.reshape(MEM_HEADS * M, HEAD_DIM)],
                                               name="mem_k_norm_bwd")
        dmkv = jnp.concatenate([_unheads(dmk_raw, MEM_HEADS), _unheads(dmvv, MEM_HEADS)], axis=-1)
        gb["w_mem_kv"] = _mm(s["memn"], dmkv, ta=True, out_kind="rows", layer=l, out_buf=gb["w_mem_kv"], name="mem_kv_proj_dw")
        dmemn = _mm(dmkv, w["w_mem_kv"], tb=True, b_kind="rows", layer=l, name="mem_kv_proj_dx")
        g["mem_norm"][l] = _rms_bwd(mem, p["mem_norm"][l], [dmemn], want_dx=False, name="mem_norm_bwd")
        if l < N_A:
            dmain_out = _groups(dcat[:, :MAIN_WIDTH])
            du_pool, g["pool_w"][l], dps = _pool_bwd(s["u"], s["pw"], s["ps"], dmain_out, name="pool_bwd")
            g["pool_scale"][l] = dps.reshape(MAIN_WIDTH)
            dmain = _ungroups(du_pool)
        else:
            j = l - N_A
            dmain_out = _heads(dcat[:, :MAIN_WIDTH], SWA_Q_HEADS).astype(BF16).reshape(SWA_KV_HEADS, SWA_GROUP, T, HEAD_DIM)
            dqn, dk_l, dv_l, dsink = _swa_bwd(s["qn"], kn, vv, p["sinks"][j], dmain_out, name="swa_bwd")
            g["sinks"][j] = dsink[:, 0, :SWA_GROUP].reshape(SWA_Q_HEADS)
            dq_raw, g["q_norm"][j] = _rms_bwd(s["q_raw"], p["q_norm"][j], [dqn.reshape(SWA_Q_HEADS * T, HEAD_DIM)],
                                              name="q_norm_bwd")
            dmain = _unheads(dq_raw, SWA_Q_HEADS)
            dk_l = dk_l.reshape(SWA_KV_HEADS * T, HEAD_DIM)
            dv_l = dv_l.reshape(SWA_KV_HEADS * T, HEAD_DIM)
            dkn = dk_l if dkn is None else _add(dkn, dk_l, name="dk_sum")
            dvv = dv_l if dvv is None else _add(dvv, dv_l, name="dv_sum")
        dproj = jnp.concatenate([dmain, _unheads(dmq_raw, MEM_HEADS)], axis=-1)
        gb["w_in"] = _mm(s["xn1"], dproj, ta=True, out_kind="rows", layer=l, out_buf=gb["w_in"], name="in_proj_dw")
        dxn1 = _mm(dproj, w["w_in"], tb=True, b_kind="rows", layer=l, name="in_proj_dx")
        dh, g["norm_mix"][l] = _rms_bwd(s["h"], p["norm_mix"][l], [dxn1], res=dh1, name="norm_mix_bwd")
        if l == N_A:
            dk_raw, g["k_norm"] = _rms_bwd(k_raw, p["k_norm"], [dkn], name="k_norm_bwd")
            dkv = jnp.concatenate([_unheads(dk_raw, SWA_KV_HEADS), _unheads(dvv, SWA_KV_HEADS)], axis=-1)
            gb["w_kv"] = _mm(hn_kv, dkv, ta=True, out_kind="rows", out_buf=gb["w_kv"], name="kv_proj_dw")
            dhn = _mm(dkv, w["w_kv"], tb=True, b_kind="rows", name="kv_proj_dx")
            dh, g["kv_norm"] = _rms_bwd(h_kv, p["kv_norm"], [dhn], res=dh, name="kv_norm_bwd")
    grads = {n: (jnp.stack(v) if isinstance(v, list) else v) for n, v in g.items()}
    return loss, dh, grads, gb


def kernel(x, mem, norm_mix, w_in, pool_w, pool_scale, kv_norm, w_kv, k_norm, q_norm, sinks, mem_norm, w_mem_kv, mem_q_norm, mem_k_norm, w_out, norm_mlp, w_up, w_down, loss_target, m_norm_mix, m_w_in, m_pool_w, m_pool_scale, m_kv_norm, m_w_kv, m_k_norm, m_q_norm, m_sinks, m_mem_norm, m_w_mem_kv, m_mem_q_norm, m_mem_k_norm, m_w_out, m_norm_mlp, m_w_up, m_w_down, v_norm_mix, v_w_in, v_pool_w, v_pool_scale, v_kv_norm, v_w_kv, v_k_norm, v_q_norm, v_sinks, v_mem_norm, v_w_mem_kv, v_mem_q_norm, v_mem_k_norm, v_w_out, v_norm_mlp, v_w_up, v_w_down):
    weights = dict(norm_mix=norm_mix, w_in=w_in, pool_w=pool_w, pool_scale=pool_scale, kv_norm=kv_norm, w_kv=w_kv,
                   k_norm=k_norm, q_norm=q_norm, sinks=sinks, mem_norm=mem_norm, w_mem_kv=w_mem_kv,
                   mem_q_norm=mem_q_norm, mem_k_norm=mem_k_norm, w_out=w_out, norm_mlp=norm_mlp, w_up=w_up, w_down=w_down)
    mom1 = dict(norm_mix=m_norm_mix, w_in=m_w_in, pool_w=m_pool_w, pool_scale=m_pool_scale, kv_norm=m_kv_norm, w_kv=m_w_kv,
                k_norm=m_k_norm, q_norm=m_q_norm, sinks=m_sinks, mem_norm=m_mem_norm, w_mem_kv=m_w_mem_kv,
                mem_q_norm=m_mem_q_norm, mem_k_norm=m_mem_k_norm, w_out=m_w_out, norm_mlp=m_norm_mlp, w_up=m_w_up,
                w_down=m_w_down)
    mom2 = dict(norm_mix=v_norm_mix, w_in=v_w_in, pool_w=v_pool_w, pool_scale=v_pool_scale, kv_norm=v_kv_norm, w_kv=v_w_kv,
                k_norm=v_k_norm, q_norm=v_q_norm, sinks=v_sinks, mem_norm=v_mem_norm, w_mem_kv=v_w_mem_kv,
                mem_q_norm=v_mem_q_norm, mem_k_norm=v_mem_k_norm, w_out=v_w_out, norm_mlp=v_norm_mlp, w_up=v_w_up,
                w_down=v_w_down)
    names = list(weights)
    core = lax.axis_index("c").astype(jnp.int32)
    shard = MAIN_WIDTH // N_DEV

    def as_layers(a):
        return a[None] if a.ndim == 2 else a

    scale_block = jnp.pad(pool_scale, ((0, 8 - N_A), (0, 128 - shard)))
    gathered = _all_gather([as_layers(weights[n]).astype(BF16) for n in GATHERED] + [scale_block], name="gather_weights")
    w = dict(zip(GATHERED, gathered))
    p = {n: weights[n] for n in SMALL_NAMES}
    p["pool_scale"] = gathered[-1][:, :N_A, :shard].transpose(1, 0, 2).reshape(N_A, MAIN_WIDTH)

    loss, grad_x, grads, gb = _local_step(x[0], mem[0], loss_target[0], w, p)

    sharded = list(GATHERED) + ["pool_scale"]
    gb["pool_scale"] = grads["pool_scale"].reshape(N_A, N_DEV, shard).transpose(1, 0, 2).astype(BF16)
    views = {n: _view2d(weights[n].shape) for n in sharded}
    by_core = [gb[n].reshape((N_CHIP, 2) + views[n]) for n in sharded]
    small = _pack_small(grads)
    sib_big, (sib_small,) = _sibling_exchange(by_core, [small], name="reduce_sibling")
    chip_big = [_pair_sum(a, b, core, name="chip_sum_" + n, out_dtype=BF16) for n, a, b in zip(sharded, by_core, sib_big)]
    chip_small = _pair_sum(small, sib_small, core, name="chip_sum_small", out_dtype=F32)
    parts_big, (parts_small,) = _chip_exchange(chip_big, [chip_small], name="reduce_chips")

    new = {}
    for n, parts in zip(sharded, parts_big):
        res = _adamw(parts, *(d[n].reshape(views[n]) for d in (weights, mom1, mom2)), name="adamw_" + n)
        new[n] = [r.reshape(weights[n].shape) for r in res]
    res = _adamw(parts_small, _pack_small(weights), _pack_small(mom1), _pack_small(mom2), name="adamw_replicated")
    for n, vals in zip(SMALL_NAMES, zip(*(_unpack_small(r, weights).values() for r in res))):
        new[n] = list(vals)
    outs = [new[n][k] for k in range(4) for n in names]
    total = lax.psum(loss[0, 0], AXES)
    return (total, grad_x[None], *outs)
```

```python
import math

import jax
import jax.numpy as jnp
from jax import lax
from jax.experimental import pallas as pl
from jax.experimental.pallas import tpu as pltpu

F32 = jnp.float32
BF16 = jnp.bfloat16
MESH = pl.DeviceIdType.MESH
AXES = ("x", "y", "c")

D_MODEL = 1024
DEPTH = 4
N_A = 2
HEAD_DIM = 64
MEM_HEADS = 4
MEM_WIDTH = MEM_HEADS * HEAD_DIM
MAIN_WIDTH = D_MODEL - MEM_WIDTH
POOL_GROUPS = 4
POOL_GROUP_DIM = MAIN_WIDTH // POOL_GROUPS
POOL_HALO = 16
SWA_Q_HEADS = MAIN_WIDTH // HEAD_DIM
SWA_KV_HEADS = 4
SWA_GROUP = SWA_Q_HEADS // SWA_KV_HEADS
KV_HALF = SWA_KV_HEADS * HEAD_DIM
BLOCK = 128
D_FF = 4 * D_MODEL
EPS = 1e-6
SCALE = HEAD_DIM ** -0.5
NEG = float(jnp.finfo(jnp.float32).min)
N_DEV = 8
N_CHIP = 4

ADAM_LR = 0.001
ADAM_B1 = 0.9
ADAM_B2 = 0.999
ADAM_EPS = 1e-08
ADAM_WD = 0.01
ADAM_STEP = 10

PACK_W = 512
VMEM_LIMIT = 52 * 1024 * 1024
MM_TILE = 1024

GATHERED = ("w_in", "w_kv", "w_mem_kv", "w_out", "w_up", "w_down")
SMALL_NAMES = ("norm_mix", "pool_w", "kv_norm", "k_norm", "q_norm", "sinks", "mem_norm", "mem_q_norm", "mem_k_norm",
               "norm_mlp")


ANY = pl.BlockSpec(memory_space=pl.ANY)


def _params(*sem):
    return pltpu.CompilerParams(dimension_semantics=sem, vmem_limit_bytes=VMEM_LIMIT)


def _mm(a, b, *, name, ta=False, tb=False, b_kind=None, layer=0, res=None, relu2=False, mul2=None, out_dtype=F32,
        out_kind=None, out_buf=None):
    if ta:
        K, M = a.shape
    else:
        M, K = a.shape
    if b_kind is None:
        rows_b, cols_b = b.shape
    elif b_kind == "rows":
        rows_b, cols_b = b.shape[0] * b.shape[2], b.shape[3]
    else:
        rows_b, cols_b = b.shape[1:]
    N, K2 = (rows_b, cols_b) if tb else (cols_b, rows_b)
    assert K == K2, (a.shape, b.shape)
    tm = min(M, MM_TILE if K <= MM_TILE else MM_TILE // 2)
    tn = min(N, MM_TILE)
    assert M % tm == 0 and N % tn == 0
    row_tile, col_tile = (tn, K) if tb else (K, tn)
    a_spec = pl.BlockSpec((K, tm), lambda j, i: (0, i)) if ta else pl.BlockSpec((tm, K), lambda j, i: (i, 0))

    def rc(j):
        return (j, 0) if tb else (0, j)

    if b_kind is None:
        b_spec = pl.BlockSpec((row_tile, col_tile), lambda j, i: rc(j))
    elif b_kind == "rows":
        per = row_tile // b.shape[2]
        b_spec = pl.BlockSpec((per, None, b.shape[2], col_tile), lambda j, i: (rc(j)[0], layer, 0, rc(j)[1]))
    else:
        b_spec = pl.BlockSpec((None, row_tile, col_tile), lambda j, i: (layer, *rc(j)))
    o_spec = pl.BlockSpec((tm, tn), lambda j, i: (i, j))
    dn = (((0 if ta else 1,), (1 if tb else 0,)), ((), ()))
    extra = [e for e in (res, mul2) if e is not None]
    n_out = 2 if relu2 else 1
    n_in = 2 + len(extra) + (1 if out_buf is not None else 0)

    def body(*refs):
        a_ref, b_ref = refs[0], refs[1]
        extra_refs = refs[2:2 + len(extra)]
        outs = refs[n_in:n_in + n_out]
        bv = b_ref[...].astype(BF16).reshape(row_tile, col_tile)
        v = lax.dot_general(a_ref[...].astype(BF16), bv, dn, preferred_element_type=F32)
        if res is not None:
            v = extra_refs[0][...] + v
        elif mul2 is not None:
            v = v * (2.0 * extra_refs[0][...].astype(F32))
        if relu2:
            r = jnp.maximum(v, 0.0)
            outs[0][...] = r.astype(BF16)
            outs[1][...] = (r * r).astype(BF16)
        else:
            outs[0][...] = v.astype(outs[0].dtype).reshape(outs[0].shape)

    in_specs = [a_spec, b_spec] + [o_spec] * len(extra)
    operands = [a, b, *extra]
    aliases = {}
    if out_kind is None:
        out_shape = jax.ShapeDtypeStruct((M, N), BF16 if relu2 else out_dtype)
        out_specs = o_spec
    else:
        if out_kind == "rows":
            s = out_buf.shape[2]
            out_specs = pl.BlockSpec((tm // s, None, s, tn), lambda j, i: (i, layer, 0, j))
        else:
            out_specs = pl.BlockSpec((None, tm, tn), lambda j, i: (layer, i, j))
        out_shape = jax.ShapeDtypeStruct(out_buf.shape, out_buf.dtype)
        in_specs.append(ANY)
        operands.append(out_buf)
        aliases = {len(operands) - 1: 0}
    if relu2:
        out_shape, out_specs = (out_shape, out_shape), (out_specs, out_specs)
    return pl.pallas_call(
        body, name=name, grid=(N // tn, M // tm), in_specs=in_specs, out_specs=out_specs, out_shape=out_shape,
        input_output_aliases=aliases, compiler_params=_params("parallel", "parallel"),
    )(*operands)


def _row_tile(rows, d):
    t = min(rows, (512 * 1024) // d)
    while rows % t or (t != rows and t % 16):
        t -= 1
    return t


def _rms_fwd(x, g, *, name, out_dtype=BF16):
    R, D = x.shape
    tr = _row_tile(R, D)

    def body(x_ref, g_ref, o_ref):
        xv = x_ref[...].astype(F32)
        r = lax.rsqrt(jnp.mean(xv * xv, axis=-1, keepdims=True) + EPS)
        o_ref[...] = ((xv * r) * g_ref[...]).astype(o_ref.dtype)

    return pl.pallas_call(
        body, name=name, grid=(R // tr,),
        in_specs=[pl.BlockSpec((tr, D), lambda i: (i, 0)), pl.BlockSpec((1, D), lambda i: (0, 0))],
        out_specs=pl.BlockSpec((tr, D), lambda i: (i, 0)), out_shape=jax.ShapeDtypeStruct((R, D), out_dtype),
        compiler_params=_params("parallel"),
    )(x, g.reshape(1, D))


def _rms_bwd(x, g, dys, *, name, res=None, want_dx=True, also_bf16=False):
    R, D = x.shape
    tr = _row_tile(R, D)
    n_dy = len(dys)
    has_res = res is not None

    def body(*refs):
        x_ref, g_ref = refs[0], refs[1]
        dy_refs = refs[2:2 + n_dy]
        res_ref = refs[2 + n_dy] if has_res else None
        outs = refs[2 + n_dy + (1 if has_res else 0):]
        dg_ref = outs[-1]
        i = pl.program_id(0)
        xv = x_ref[...].astype(F32)
        dy = dy_refs[0][...].astype(F32)
        for extra in dy_refs[1:]:
            dy = dy + extra[...].astype(F32)
        r = lax.rsqrt(jnp.mean(xv * xv, axis=-1, keepdims=True) + EPS)
        xh = xv * r
        part = jnp.sum(dy * xh, axis=0, keepdims=True)

        @pl.when(i == 0)
        def _():
            dg_ref[...] = part

        @pl.when(i > 0)
        def _():
            dg_ref[...] += part

        if want_dx:
            gdy = dy * g_ref[...]
            dx = r * (gdy - xh * jnp.mean(gdy * xh, axis=-1, keepdims=True))
            if has_res:
                dx = res_ref[...] + dx
            outs[0][...] = dx
            if also_bf16:
                outs[1][...] = dx.astype(BF16)

    row = pl.BlockSpec((tr, D), lambda i: (i, 0))
    vec = pl.BlockSpec((1, D), lambda i: (0, 0))
    out_shape = [jax.ShapeDtypeStruct((1, D), F32)]
    out_specs = [vec]
    if also_bf16:
        out_shape = [jax.ShapeDtypeStruct((R, D), BF16)] + out_shape
        out_specs = [row] + out_specs
    if want_dx:
        out_shape = [jax.ShapeDtypeStruct((R, D), F32)] + out_shape
        out_specs = [row] + out_specs
    outs = pl.pallas_call(
        body, name=name, grid=(R // tr,),
        in_specs=[row, vec] + [row] * (n_dy + (1 if has_res else 0)), out_specs=out_specs, out_shape=out_shape,
        compiler_params=_params("arbitrary"),
    )(x, g.reshape(1, D), *dys, *([res] if has_res else []))
    return (*outs[:-1], outs[-1].reshape(D)) if want_dx else outs[0].reshape(D)


def _add(a, b, *, name):
    R, D = a.shape
    tr = _row_tile(R, D)

    def body(a_ref, b_ref, o_ref):
        o_ref[...] = a_ref[...] + b_ref[...]

    row = pl.BlockSpec((tr, D), lambda i: (i, 0))
    return pl.pallas_call(body, name=name, grid=(R // tr,), in_specs=[row, row], out_specs=row,
                          out_shape=jax.ShapeDtypeStruct((R, D), a.dtype), compiler_params=_params("parallel"))(a, b)


POOL_TILE = 512


def _pool_window(group):
    return lax.shift_left(jnp.int32(2), group)


def _pool_diff(u_ref, halo_ref, group, tile):
    first = tile == 0
    halo = jnp.where(first, 0.0, halo_ref[...])
    ext = jnp.concatenate([halo, u_ref[...]], axis=0)
    n = ext.shape[0]
    s1 = ext + pltpu.roll(ext, 1, 0)
    s2 = s1 + pltpu.roll(s1, 2, 0)
    s3 = s2 + pltpu.roll(s2, 4, 0)
    s4 = s3 + pltpu.roll(s3, 8, 0)
    ws = jnp.where(group == 0, s1, jnp.where(group == 1, s2, jnp.where(group == 2, s3, s4)))[POOL_HALO:n]
    t = tile * POOL_TILE + lax.broadcasted_iota(jnp.int32, (POOL_TILE, 1), 0)
    cnt = jnp.minimum(t + 1, _pool_window(group)).astype(F32)
    return ws / cnt - u_ref[...], cnt


def _pool_specs():
    per_tile = POOL_TILE // POOL_HALO
    cur = pl.BlockSpec((None, POOL_TILE, POOL_GROUP_DIM), lambda g, i: (g, i, 0))
    prev = pl.BlockSpec((None, POOL_HALO, POOL_GROUP_DIM), lambda g, i: (g, jnp.maximum(i * per_tile - 1, 0), 0))
    pw = pl.BlockSpec((None, POOL_GROUP_DIM, POOL_GROUP_DIM), lambda g, i: (g, 0, 0))
    vec = pl.BlockSpec((None, 1, POOL_GROUP_DIM), lambda g, i: (g, 0, 0))
    return cur, prev, pw, vec


def _pool_fwd(u, pw, scale, *, name):
    G, T, C = u.shape
    assert T % POOL_TILE == 0
    cur, prev, pw_spec, vec = _pool_specs()

    def body(u_ref, halo_ref, pw_ref, sc_ref, o_ref):
        d, _ = _pool_diff(u_ref, halo_ref, pl.program_id(0), pl.program_id(1))
        mixed = jnp.dot(d.astype(BF16), pw_ref[...].astype(BF16), preferred_element_type=F32)
        o_ref[...] = (mixed * sc_ref[...]).astype(o_ref.dtype)

    return pl.pallas_call(
        body, name=name, grid=(G, T // POOL_TILE), in_specs=[cur, prev, pw_spec, vec], out_specs=cur,
        out_shape=jax.ShapeDtypeStruct((G, T, C), BF16), compiler_params=_params("parallel", "parallel"),
    )(u, u, pw, scale)


def _pool_bwd(u, pw, scale, dout, *, name):
    G, T, C = u.shape
    nt = T // POOL_TILE
    per_tile = POOL_TILE // POOL_HALO
    cur, prev, pw_spec, vec = _pool_specs()
    nxt = pl.BlockSpec((None, POOL_HALO, C), lambda g, i: (g, jnp.minimum((i + 1) * per_tile, nt * per_tile - 1), 0))

    def body(u_ref, halo_ref, pw_ref, sc_ref, do_ref, donext_ref, du_ref, dpw_ref, dsc_ref):
        group, tile = pl.program_id(0), pl.program_id(1)
        d, cnt = _pool_diff(u_ref, halo_ref, group, tile)
        pwb = pw_ref[...].astype(BF16)
        db = d.astype(BF16)
        mixed = jnp.dot(db, pwb, preferred_element_type=F32)
        dout = do_ref[...].astype(F32)
        dsc = jnp.sum(dout * mixed, axis=0, keepdims=True)
        sc = sc_ref[...]
        dmix = (dout * sc).astype(BF16)
        dpw = lax.dot_general(db, dmix, (((0,), (0,)), ((), ())), preferred_element_type=F32)

        @pl.when(tile == 0)
        def _():
            dpw_ref[...] = dpw
            dsc_ref[...] = dsc

        @pl.when(tile > 0)
        def _():
            dpw_ref[...] += dpw
            dsc_ref[...] += dsc

        last = tile == nt - 1
        dnext = jnp.where(last, 0.0, donext_ref[...].astype(F32))
        dmix_ext = jnp.concatenate([dmix, (dnext * sc).astype(BF16)], axis=0)
        dd_ext = lax.dot_general(dmix_ext, pwb, (((1,), (1,)), ((), ())), preferred_element_type=F32)
        window = _pool_window(group).astype(F32)
        cnt_ext = jnp.concatenate([cnt, jnp.broadcast_to(window, (POOL_HALO, 1))], axis=0)
        q = dd_ext / cnt_ext
        n = q.shape[0]
        r1 = q + pltpu.roll(q, n - 1, 0)
        r2 = r1 + pltpu.roll(r1, n - 2, 0)
        r3 = r2 + pltpu.roll(r2, n - 4, 0)
        r4 = r3 + pltpu.roll(r3, n - 8, 0)
        back = jnp.where(group == 0, r1, jnp.where(group == 1, r2, jnp.where(group == 2, r3, r4)))
        du_ref[...] = back[0:POOL_TILE] - dd_ext[0:POOL_TILE]

    return pl.pallas_call(
        body, name=name, grid=(G, nt), in_specs=[cur, prev, pw_spec, vec, cur, nxt],
        out_specs=(cur, pw_spec, vec),
        out_shape=(jax.ShapeDtypeStruct((G, T, C), F32), jax.ShapeDtypeStruct((G, C, C), F32),
                   jax.ShapeDtypeStruct((G, 1, C), F32)),
        compiler_params=_params("arbitrary", "arbitrary"),
    )(u, u, pw, scale, dout, dout)


def _softmax(q, k, bias, valid, sink):
    s = lax.dot_general(q, k, (((1,), (1,)), ((), ())), preferred_element_type=F32) * SCALE
    if bias is not None:
        s = s - bias
    if valid is not None:
        s = jnp.where(valid, s, NEG)
    m = jnp.max(s, axis=-1, keepdims=True)
    if sink is not None:
        m = jnp.maximum(m, sink)
    e = jnp.exp(s - m)
    z = jnp.sum(e, axis=-1, keepdims=True)
    if sink is None:
        return e * (1.0 / z), None
    es = jnp.exp(sink - m)
    inv = 1.0 / (z + es)
    return e * inv, es * inv


def _swa_terms(sink_ref, kvh, blk):
    rows = SWA_GROUP * BLOCK
    row = lax.broadcasted_iota(jnp.int32, (rows, 1), 0)
    grp = row // BLOCK
    head = (kvh * SWA_GROUP + grp + 1).astype(F32)
    slope = jnp.exp(head * (-8.0 * math.log(2.0) / SWA_Q_HEADS))
    qi = lax.broadcasted_iota(jnp.int32, (rows, 2 * BLOCK), 0) % BLOCK
    kj = lax.broadcasted_iota(jnp.int32, (rows, 2 * BLOCK), 1)
    dist = qi + BLOCK - kj
    valid = (dist >= 0) & (dist < BLOCK) & ((blk > 0) | (kj >= BLOCK))
    bias = slope * dist.astype(F32)
    s0, s1, s2 = (sink_ref[kvh * SWA_GROUP + g] for g in range(SWA_GROUP))
    sink = jnp.where(grp == 0, s0, jnp.where(grp == 1, s1, s2))
    return bias, valid, sink, grp


def _swa_fwd(q, k, v, sinks, *, name):
    H, G, T, hd = q.shape
    nb = T // BLOCK
    rows = G * BLOCK

    def body(sink_ref, q_ref, kp_ref, kc_ref, vp_ref, vc_ref, o_ref):
        kvh, blk = pl.program_id(0), pl.program_id(1)
        bias, valid, sink, _ = _swa_terms(sink_ref, kvh, blk)
        kk = jnp.concatenate([kp_ref[...], kc_ref[...]], axis=0)
        vv = jnp.concatenate([vp_ref[...], vc_ref[...]], axis=0)
        p, _ = _softmax(q_ref[...].reshape(rows, hd), kk, bias, valid, sink)
        o = jnp.dot(p.astype(BF16), vv, preferred_element_type=F32)
        o_ref[...] = o.reshape(G, BLOCK, hd).astype(o_ref.dtype)

    qs = pl.BlockSpec((None, G, BLOCK, hd), lambda h, n: (h, 0, n, 0))
    prev = pl.BlockSpec((None, BLOCK, hd), lambda h, n: (h, jnp.maximum(n - 1, 0), 0))
    cur = pl.BlockSpec((None, BLOCK, hd), lambda h, n: (h, n, 0))
    return pl.pallas_call(
        body, name=name, grid=(H, nb),
        in_specs=[pl.BlockSpec(memory_space=pltpu.SMEM), qs, prev, cur, prev, cur], out_specs=qs,
        out_shape=jax.ShapeDtypeStruct((H, G, T, hd), BF16), compiler_params=_params("parallel", "parallel"),
    )(sinks, q, k, k, v, v)


def _swa_bwd(q, k, v, sinks, do, *, name):
    H, G, T, hd = q.shape
    nb = T // BLOCK
    rows = G * BLOCK

    def body(sink_ref, q_ref, do_ref, kp_ref, kc_ref, vp_ref, vc_ref, dq_ref, dk_ref, dv_ref, ds_ref, ck, cv):
        kvh, blk = pl.program_id(0), pl.program_id(1)

        @pl.when(blk == 0)
        def _():
            ck[...] = jnp.zeros_like(ck)
            cv[...] = jnp.zeros_like(cv)
            ds_ref[...] = jnp.zeros_like(ds_ref)

        @pl.when(blk < nb)
        def _():
            bias, valid, sink, grp = _swa_terms(sink_ref, kvh, blk)
            kk = jnp.concatenate([kp_ref[...], kc_ref[...]], axis=0)
            vv = jnp.concatenate([vp_ref[...], vc_ref[...]], axis=0)
            qq = q_ref[...].reshape(rows, hd)
            dout = do_ref[...].reshape(rows, hd)
            p, ps = _softmax(qq, kk, bias, valid, sink)
            dp = lax.dot_general(dout, vv, (((1,), (1,)), ((), ())), preferred_element_type=F32)
            dsum = jnp.sum(p * dp, axis=-1, keepdims=True)
            ds = (p * (dp - dsum)).astype(BF16)
            dq = jnp.dot(ds, kk, preferred_element_type=F32) * SCALE
            dq_ref[...] = dq.reshape(G, BLOCK, hd)
            dk = lax.dot_general(ds, qq, (((0,), (0,)), ((), ())), preferred_element_type=F32) * SCALE
            dv = lax.dot_general(p.astype(BF16), dout, (((0,), (0,)), ((), ())), preferred_element_type=F32)
            dk_ref[...] = ck[...] + dk[0:BLOCK]
            dv_ref[...] = cv[...] + dv[0:BLOCK]
            ck[...] = dk[BLOCK:2 * BLOCK]
            cv[...] = dv[BLOCK:2 * BLOCK]
            dsink = -(ps * dsum)
            lane = lax.broadcasted_iota(jnp.int32, (1, 128), 1)
            acc = jnp.zeros((1, 128), F32)
            for g in range(G):
                acc = acc + jnp.where(lane == g, jnp.sum(jnp.where(grp == g, dsink, 0.0)), 0.0)
            ds_ref[...] += acc

        @pl.when(blk == nb)
        def _():
            dk_ref[...] = ck[...]
            dv_ref[...] = cv[...]

    def at(n):
        return jnp.minimum(n, nb - 1)

    qs = pl.BlockSpec((None, G, BLOCK, hd), lambda h, n: (h, 0, at(n), 0))
    prev = pl.BlockSpec((None, BLOCK, hd), lambda h, n: (h, jnp.maximum(at(n) - 1, 0), 0))
    cur = pl.BlockSpec((None, BLOCK, hd), lambda h, n: (h, at(n), 0))
    late = pl.BlockSpec((None, BLOCK, hd), lambda h, n: (h, jnp.maximum(n - 1, 0), 0))
    dsink_spec = pl.BlockSpec((None, 1, 128), lambda h, n: (h, 0, 0))
    return pl.pallas_call(
        body, name=name, grid=(H, nb + 1),
        in_specs=[pl.BlockSpec(memory_space=pltpu.SMEM), qs, qs, prev, cur, prev, cur],
        out_specs=(qs, late, late, dsink_spec),
        out_shape=(jax.ShapeDtypeStruct((H, G, T, hd), F32), jax.ShapeDtypeStruct((H, T, hd), F32),
                   jax.ShapeDtypeStruct((H, T, hd), F32), jax.ShapeDtypeStruct((H, 1, 128), F32)),
        scratch_shapes=[pltpu.VMEM((BLOCK, hd), F32), pltpu.VMEM((BLOCK, hd), F32)],
        compiler_params=_params("arbitrary", "arbitrary"),
    )(sinks, q, do, k, k, v, v)


MEM_Q_TILE = 512


def _mem_fwd(q, k, v, *, name):
    H, T, hd = q.shape
    M = k.shape[1]
    tq = min(T, MEM_Q_TILE)

    def body(q_ref, k_ref, v_ref, o_ref):
        p, _ = _softmax(q_ref[...], k_ref[...], None, None, None)
        o_ref[...] = jnp.dot(p.astype(BF16), v_ref[...], preferred_element_type=F32).astype(o_ref.dtype)

    qs = pl.BlockSpec((None, tq, hd), lambda h, i: (h, i, 0))
    ks = pl.BlockSpec((None, M, hd), lambda h, i: (h, 0, 0))
    return pl.pallas_call(body, name=name, grid=(H, T // tq), in_specs=[qs, ks, ks], out_specs=qs,
                          out_shape=jax.ShapeDtypeStruct((H, T, hd), BF16),
                          compiler_params=_params("parallel", "parallel"))(q, k, v)


def _mem_bwd(q, k, v, do, *, name):
    H, T, hd = q.shape
    M = k.shape[1]
    tq = min(T, MEM_Q_TILE)

    def body(q_ref, do_ref, k_ref, v_ref, dq_ref, dk_ref, dv_ref):
        i = pl.program_id(1)
        qq, kk, vv, dout = q_ref[...], k_ref[...], v_ref[...], do_ref[...]
        p, _ = _softmax(qq, kk, None, None, None)
        dp = lax.dot_general(dout, vv, (((1,), (1,)), ((), ())), preferred_element_type=F32)
        dsum = jnp.sum(p * dp, axis=-1, keepdims=True)
        ds = (p * (dp - dsum)).astype(BF16)
        dq_ref[...] = jnp.dot(ds, kk, preferred_element_type=F32) * SCALE
        dk = lax.dot_general(ds, qq, (((0,), (0,)), ((), ())), preferred_element_type=F32) * SCALE
        dv = lax.dot_general(p.astype(BF16), dout, (((0,), (0,)), ((), ())), preferred_element_type=F32)

        @pl.when(i == 0)
        def _():
            dk_ref[...] = dk
            dv_ref[...] = dv

        @pl.when(i > 0)
        def _():
            dk_ref[...] += dk
            dv_ref[...] += dv

    qs = pl.BlockSpec((None, tq, hd), lambda h, i: (h, i, 0))
    ks = pl.BlockSpec((None, M, hd), lambda h, i: (h, 0, 0))
    return pl.pallas_call(
        body, name=name, grid=(H, T // tq), in_specs=[qs, qs, ks, ks], out_specs=(qs, ks, ks),
        out_shape=(jax.ShapeDtypeStruct((H, T, hd), F32), jax.ShapeDtypeStruct((H, M, hd), F32),
                   jax.ShapeDtypeStruct((H, M, hd), F32)),
        compiler_params=_params("arbitrary", "arbitrary"),
    )(q, do, k, v)


def _loss(y, target, *, name):
    T, D = y.shape
    tr = _row_tile(T, D)

    def body(y_ref, t_ref, l_ref, dy_ref, dyb_ref):
        i = pl.program_id(0)
        err = y_ref[...] - t_ref[...]
        dy = err / float(D)
        dy_ref[...] = dy
        dyb_ref[...] = dy.astype(BF16)
        part = jnp.full((8, 128), 0.5 * jnp.sum(jnp.mean(err * err, axis=-1)), F32)

        @pl.when(i == 0)
        def _():
            l_ref[...] = part

        @pl.when(i > 0)
        def _():
            l_ref[...] += part

    row = pl.BlockSpec((tr, D), lambda i: (i, 0))
    return pl.pallas_call(
        body, name=name, grid=(T // tr,), in_specs=[row, row],
        out_specs=(pl.BlockSpec((8, 128), lambda i: (0, 0)), row, row),
        out_shape=(jax.ShapeDtypeStruct((8, 128), F32), jax.ShapeDtypeStruct((T, D), F32), jax.ShapeDtypeStruct((T, D), BF16)),
        compiler_params=_params("arbitrary"),
    )(y, target)


def _position():
    return lax.axis_index("x"), lax.axis_index("y"), lax.axis_index("c")


def _all_gather(arrays, *, name):
    n = len(arrays)

    def body(*refs):
        srcs, outs = refs[:n], refs[n:2 * n]
        send_sems, recv_sems, local_sems = refs[2 * n:]
        x, y, c = _position()
        me, sibling = (x, y, c), (x, y, 1 - c)
        chips = [(1 - x, y), (x, 1 - y), (1 - x, 1 - y)]

        def slot(a, px, py, pc):
            return outs[a].at[4 * px + 2 * py + pc]

        def copy(a, k, block, to, src=None):
            return pltpu.make_async_remote_copy(
                src_ref=slot(a, *block) if src is None else src, dst_ref=slot(a, *block),
                send_sem=send_sems.at[a, k], recv_sem=recv_sems.at[a, k], device_id=to, device_id_type=MESH)

        mine = [pltpu.make_async_copy(srcs[a], slot(a, *me), local_sems.at[a]) for a in range(n)]
        for cp in mine:
            cp.start()
        first, passed = [], []
        for a in range(n):
            first.append(copy(a, 0, me, sibling, src=srcs[a]))
            first += [copy(a, 1 + j, me, (*chip, c), src=srcs[a]) for j, chip in enumerate(chips)]
        for cp in first:
            cp.start()
        for a in range(n):
            for j, chip in enumerate(chips):
                copy(a, 1 + j, (*chip, c), me).wait_recv()
                fwd = copy(a, 4 + j, (*chip, c), sibling)
                fwd.start()
                passed.append(fwd)
        for a in range(n):
            copy(a, 0, sibling, me).wait_recv()
            for j, chip in enumerate(chips):
                copy(a, 4 + j, (*chip, 1 - c), me).wait_recv()
        for cp in first + passed:
            cp.wait_send()
        for cp in mine:
            cp.wait()

    return pl.pallas_call(
        body, name=name, in_specs=[ANY] * n, out_specs=[ANY] * n,
        out_shape=[jax.ShapeDtypeStruct((N_DEV,) + a.shape, a.dtype) for a in arrays],
        scratch_shapes=[pltpu.SemaphoreType.DMA((n, 7)), pltpu.SemaphoreType.DMA((n, 7)), pltpu.SemaphoreType.DMA((n,))],
    )(*arrays)


def _sibling_exchange(by_core, whole, *, name):
    n1, n = len(by_core), len(by_core) + len(whole)

    def body(*refs):
        srcs, outs = refs[:n], refs[n:2 * n]
        send_sems, recv_sems = refs[2 * n:]
        x, y, c = _position()
        copies = [
            pltpu.make_async_remote_copy(src_ref=srcs[a].at[:, 1 - c] if a < n1 else srcs[a], dst_ref=outs[a],
                                         send_sem=send_sems.at[a], recv_sem=recv_sems.at[a], device_id=(x, y, 1 - c),
                                         device_id_type=MESH)
            for a in range(n)]
        for cp in copies:
            cp.start()
        for cp in copies:
            cp.wait()

    out_shape = [jax.ShapeDtypeStruct(a.shape[:1] + a.shape[2:], a.dtype) for a in by_core]
    out_shape += [jax.ShapeDtypeStruct(a.shape, a.dtype) for a in whole]
    outs = pl.pallas_call(
        body, name=name, in_specs=[ANY] * n, out_specs=[ANY] * n, out_shape=out_shape,
        scratch_shapes=[pltpu.SemaphoreType.DMA((n,)), pltpu.SemaphoreType.DMA((n,))],
    )(*by_core, *whole)
    return outs[:n1], outs[n1:]


def _chip_exchange(per_chip, whole, *, name):
    n1, n = len(per_chip), len(per_chip) + len(whole)

    def body(*refs):
        srcs, outs = refs[:n], refs[n:2 * n]
        send_sems, recv_sems, local_sems = refs[2 * n:]
        x, y, c = _position()
        my_chip = 2 * x + y
        chips = [(1 - x, y), (x, 1 - y), (1 - x, 1 - y)]

        def src(a, chip):
            return srcs[a].at[chip] if a < n1 else srcs[a]

        local = [pltpu.make_async_copy(src(a, my_chip), outs[a].at[my_chip], local_sems.at[a]) for a in range(n)]
        for cp in local:
            cp.start()
        copies = [
            pltpu.make_async_remote_copy(src_ref=src(a, 2 * px + py), dst_ref=outs[a].at[my_chip],
                                         send_sem=send_sems.at[a, j], recv_sem=recv_sems.at[a, j], device_id=(px, py, c),
                                         device_id_type=MESH)
            for a in range(n) for j, (px, py) in enumerate(chips)]
        for cp in copies:
            cp.start()
        for cp in copies:
            cp.wait()
        for cp in local:
            cp.wait()

    out_shape = [jax.ShapeDtypeStruct(a.shape, a.dtype) for a in per_chip]
    out_shape += [jax.ShapeDtypeStruct((N_CHIP,) + a.shape, a.dtype) for a in whole]
    outs = pl.pallas_call(
        body, name=name, in_specs=[ANY] * n, out_specs=[ANY] * n, out_shape=out_shape,
        scratch_shapes=[pltpu.SemaphoreType.DMA((n, 3)), pltpu.SemaphoreType.DMA((n, 3)), pltpu.SemaphoreType.DMA((n,))],
    )(*per_chip, *whole)
    return outs[:n1], outs[n1:]


def _view2d(shape):
    return math.prod(shape[:-1]), shape[-1]


def _pair_sum(mine, other, core, *, name, out_dtype):
    by_core = mine.ndim == 4
    n, w = other.shape[-2:]
    tr = _row_tile(n, w * 2)
    lead = other.shape[0] if by_core else 1

    def body(core_ref, a_ref, b_ref, o_ref):
        o_ref[...] = (a_ref[...].astype(F32) + b_ref[...].astype(F32)).astype(o_ref.dtype)

    if by_core:
        a_spec = pl.BlockSpec((None, None, tr, w), lambda j, i, core_ref: (j, core_ref[0], i, 0))
        o_spec = pl.BlockSpec((None, tr, w), lambda j, i, core_ref: (j, i, 0))
    else:
        a_spec = o_spec = pl.BlockSpec((tr, w), lambda j, i, core_ref: (i, 0))
    grid_spec = pltpu.PrefetchScalarGridSpec(num_scalar_prefetch=1, grid=(lead, n // tr), in_specs=[a_spec, o_spec],
                                             out_specs=o_spec)
    return pl.pallas_call(body, name=name, grid_spec=grid_spec, out_shape=jax.ShapeDtypeStruct(other.shape, out_dtype),
                          compiler_params=_params("parallel", "parallel"))(core.reshape(1), mine, other)


def _adamw(parts, w, m, v, *, name):
    n_parts, R, W = parts.shape
    tr = _row_tile(R, W * 2)

    def body(p_ref, w_ref, m_ref, v_ref, g_out, d_out, m_out, v_out):
        g = p_ref[0].astype(F32)
        for j in range(1, n_parts):
            g = g + p_ref[j].astype(F32)
        m_new = ADAM_B1 * m_ref[...] + (1.0 - ADAM_B1) * g
        v_new = ADAM_B2 * v_ref[...] + (1.0 - ADAM_B2) * (g * g)
        m_hat = m_new / (1.0 - ADAM_B1 ** ADAM_STEP)
        v_hat = v_new / (1.0 - ADAM_B2 ** ADAM_STEP)
        g_out[...] = g
        d_out[...] = -ADAM_LR * (m_hat / (jnp.sqrt(v_hat) + ADAM_EPS) + ADAM_WD * w_ref[...])
        m_out[...] = m_new
        v_out[...] = v_new

    row = pl.BlockSpec((tr, W), lambda i: (i, 0))
    out = jax.ShapeDtypeStruct((R, W), F32)
    return pl.pallas_call(
        body, name=name, grid=(R // tr,), in_specs=[pl.BlockSpec((n_parts, tr, W), lambda i: (0, i, 0)), row, row, row],
        out_specs=(row, row, row, row), out_shape=(out, out, out, out), compiler_params=_params("parallel"),
    )(parts, w, m, v)


SMALL_ROWS = 608


def _pack_small(p):
    flat = jnp.concatenate([p[n].reshape(-1).astype(F32) for n in SMALL_NAMES])
    return jnp.pad(flat, (0, SMALL_ROWS * PACK_W - flat.shape[0])).reshape(SMALL_ROWS, PACK_W)


def _unpack_small(buf, like):
    out, at = {}, 0
    flat = buf.reshape(-1)
    for n in SMALL_NAMES:
        size = math.prod(like[n].shape)
        out[n] = flat[at:at + size].reshape(like[n].shape)
        at += size
    return out


def _heads(a, nh):
    T = a.shape[0]
    return a.reshape(T, nh, HEAD_DIM).transpose(1, 0, 2).reshape(nh * T, HEAD_DIM)


def _unheads(a, nh):
    a = a.reshape(nh, -1, HEAD_DIM)
    return a.transpose(1, 0, 2).reshape(a.shape[1], nh * HEAD_DIM)


def _groups(a):
    T = a.shape[0]
    return a.reshape(T, POOL_GROUPS, POOL_GROUP_DIM).transpose(1, 0, 2)


def _ungroups(a):
    return a.transpose(1, 0, 2).reshape(a.shape[1], MAIN_WIDTH)


def _local_step(x, mem, target, w, p):
    T = x.shape[0]
    M = mem.shape[0]
    saved = []
    h = x
    kn = vv = k_raw = h_kv = hn_kv = None
    for l in range(DEPTH):
        s = {}
        if l == N_A:
            h_kv = h
            hn_kv = _rms_fwd(h, p["kv_norm"], name="kv_norm_fwd")
            kv = _mm(hn_kv, w["w_kv"], b_kind="rows", name="kv_proj")
            k_raw = _heads(kv[:, :KV_HALF], SWA_KV_HEADS)
            kn = _rms_fwd(k_raw, p["k_norm"], name="k_norm_fwd").reshape(SWA_KV_HEADS, T, HEAD_DIM)
            vv = _heads(kv[:, KV_HALF:], SWA_KV_HEADS).astype(BF16).reshape(SWA_KV_HEADS, T, HEAD_DIM)
        s["h"] = h
        s["xn1"] = _rms_fwd(h, p["norm_mix"][l], name="norm_mix_fwd")
        proj = _mm(s["xn1"], w["w_in"], b_kind="rows", layer=l, name="in_proj")
        s["mq_raw"] = _heads(proj[:, MAIN_WIDTH:], MEM_HEADS)
        s["mqn"] = _rms_fwd(s["mq_raw"], p["mem_q_norm"][l], name="mem_q_norm_fwd").reshape(MEM_HEADS, T, HEAD_DIM)
        s["memn"] = _rms_fwd(mem, p["mem_norm"][l], name="mem_norm_fwd")
        mkv = _mm(s["memn"], w["w_mem_kv"], b_kind="rows", layer=l, name="mem_kv_proj")
        s["mk_raw"] = _heads(mkv[:, :MEM_WIDTH], MEM_HEADS)
        s["mkn"] = _rms_fwd(s["mk_raw"], p["mem_k_norm"][l], name="mem_k_norm_fwd").reshape(MEM_HEADS, M, HEAD_DIM)
        s["mvv"] = _heads(mkv[:, MEM_WIDTH:], MEM_HEADS).astype(BF16).reshape(MEM_HEADS, M, HEAD_DIM)
        mem_out = _unheads(_mem_fwd(s["mqn"], s["mkn"], s["mvv"], name="mem_attn_fwd"), MEM_HEADS)
        if l < N_A:
            s["u"] = _groups(proj[:, :MAIN_WIDTH])
            s["pw"] = p["pool_w"][l]
            s["ps"] = p["pool_scale"][l].reshape(POOL_GROUPS, 1, POOL_GROUP_DIM)
            main_out = _ungroups(_pool_fwd(s["u"], s["pw"], s["ps"], name="pool_fwd"))
        else:
            j = l - N_A
            s["q_raw"] = _heads(proj[:, :MAIN_WIDTH], SWA_Q_HEADS)
            s["qn"] = _rms_fwd(s["q_raw"], p["q_norm"][j], name="q_norm_fwd").reshape(SWA_KV_HEADS, SWA_GROUP, T, HEAD_DIM)
            main_out = _unheads(_swa_fwd(s["qn"], kn, vv, p["sinks"][j], name="swa_fwd"), SWA_Q_HEADS)
        s["cat"] = jnp.concatenate([main_out, mem_out], axis=-1)
        s["h1"] = _mm(s["cat"], w["w_out"], b_kind="rows", layer=l, res=h, name="out_proj")
        s["xn2"] = _rms_fwd(s["h1"], p["norm_mlp"][l], name="norm_mlp_fwd")
        s["r"], s["a"] = _mm(s["xn2"], w["w_up"], b_kind="layers", layer=l, relu2=True, name="mlp_up")
        h = _mm(s["a"], w["w_down"], b_kind="rows", layer=l, res=s["h1"], name="mlp_down")
        saved.append(s)

    loss, dh, dh_b = _loss(h, target, name="loss_head")

    g = {n: [None] * DEPTH for n in ("norm_mix", "mem_norm", "mem_q_norm", "mem_k_norm", "norm_mlp")}
    gb = {n: lax.empty(w[n].shape, BF16) for n in GATHERED}
    g.update({n: [None] * N_A for n in ("pool_w", "pool_scale", "q_norm", "sinks")})
    dkn = dvv = None
    for l in reversed(range(DEPTH)):
        s = saved[l]
        gb["w_down"] = _mm(s["a"], dh_b, ta=True, out_kind="rows", layer=l, out_buf=gb["w_down"], name="mlp_down_dw")
        du = _mm(dh_b, w["w_down"], tb=True, b_kind="rows", layer=l, mul2=s["r"], out_dtype=BF16, name="mlp_down_dx")
        gb["w_up"] = _mm(s["xn2"], du, ta=True, out_kind="layers", layer=l, out_buf=gb["w_up"], name="mlp_up_dw")
        dxn2 = _mm(du, w["w_up"], tb=True, b_kind="layers", layer=l, name="mlp_up_dx")
        dh1, dh1_b, g["norm_mlp"][l] = _rms_bwd(s["h1"], p["norm_mlp"][l], [dxn2], res=dh, also_bf16=True,
                                                name="norm_mlp_bwd")
        gb["w_out"] = _mm(s["cat"], dh1_b, ta=True, out_kind="rows", layer=l, out_buf=gb["w_out"], name="out_proj_dw")
        dcat = _mm(dh1_b, w["w_out"], tb=True, b_kind="rows", layer=l, name="out_proj_dx")
        dmem_out = _heads(dcat[:, MAIN_WIDTH:], MEM_HEADS).astype(BF16).reshape(MEM_HEADS, T, HEAD_DIM)
        dmqn, dmkn, dmvv = _mem_bwd(s["mqn"], s["mkn"], s["mvv"], dmem_out, name="mem_attn_bwd")
        dmq_raw, g["mem_q_norm"][l] = _rms_bwd(s["mq_raw"], p["mem_q_norm"][l], [dmqn.reshape(MEM_HEADS * T, HEAD_DIM)],
                                               name="mem_q_norm_bwd")
        dmk_raw, g["mem_k_norm"][l] = _rms_bwd(s["mk_raw"], p["mem_k_norm"][l], [dmkn.reshape(MEM_HEADS * M, HEAD_DIM)],
                                               name="mem_k_norm_bwd")
        dmkv = jnp.concatenate([_unheads(dmk_raw, MEM_HEADS), _unheads(dmvv, MEM_HEADS)], axis=-1).astype(BF16)
        gb["w_mem_kv"] = _mm(s["memn"], dmkv, ta=True, out_kind="rows", layer=l, out_buf=gb["w_mem_kv"], name="mem_kv_proj_dw")
        dmemn = _mm(dmkv, w["w_mem_kv"], tb=True, b_kind="rows", layer=l, name="mem_kv_proj_dx")
        g["mem_norm"][l] = _rms_bwd(mem, p["mem_norm"][l], [dmemn], want_dx=False, name="mem_norm_bwd")
        if l < N_A:
            dmain_out = _groups(dcat[:, :MAIN_WIDTH])
            du_pool, g["pool_w"][l], dps = _pool_bwd(s["u"], s["pw"], s["ps"], dmain_out, name="pool_bwd")
            g["pool_scale"][l] = dps.reshape(MAIN_WIDTH)
            dmain = _ungroups(du_pool)
        else:
            j = l - N_A
            dmain_out = _heads(dcat[:, :MAIN_WIDTH], SWA_Q_HEADS).astype(BF16).reshape(SWA_KV_HEADS, SWA_GROUP, T, HEAD_DIM)
            dqn, dk_l, dv_l, dsink = _swa_bwd(s["qn"], kn, vv, p["sinks"][j], dmain_out, name="swa_bwd")
            g["sinks"][j] = dsink[:, 0, :SWA_GROUP].reshape(SWA_Q_HEADS)
            dq_raw, g["q_norm"][j] = _rms_bwd(s["q_raw"], p["q_norm"][j], [dqn.reshape(SWA_Q_HEADS * T, HEAD_DIM)],
                                              name="q_norm_bwd")
            dmain = _unheads(dq_raw, SWA_Q_HEADS)
            dk_l = dk_l.reshape(SWA_KV_HEADS * T, HEAD_DIM)
            dv_l = dv_l.reshape(SWA_KV_HEADS * T, HEAD_DIM)
            dkn = dk_l if dkn is None else _add(dkn, dk_l, name="dk_sum")
            dvv = dv_l if dvv is None else _add(dvv, dv_l, name="dv_sum")
        dproj = jnp.concatenate([dmain, _unheads(dmq_raw, MEM_HEADS)], axis=-1).astype(BF16)
        gb["w_in"] = _mm(s["xn1"], dproj, ta=True, out_kind="rows", layer=l, out_buf=gb["w_in"], name="in_proj_dw")
        dxn1 = _mm(dproj, w["w_in"], tb=True, b_kind="rows", layer=l, name="in_proj_dx")
        if l in (0, N_A):
            dh, g["norm_mix"][l] = _rms_bwd(s["h"], p["norm_mix"][l], [dxn1], res=dh1, name="norm_mix_bwd")
        else:
            dh, dh_b, g["norm_mix"][l] = _rms_bwd(s["h"], p["norm_mix"][l], [dxn1], res=dh1, also_bf16=True,
                                                  name="norm_mix_bwd")
        if l == N_A:
            dk_raw, g["k_norm"] = _rms_bwd(k_raw, p["k_norm"], [dkn], name="k_norm_bwd")
            dkv = jnp.concatenate([_unheads(dk_raw, SWA_KV_HEADS), _unheads(dvv, SWA_KV_HEADS)], axis=-1).astype(BF16)
            gb["w_kv"] = _mm(hn_kv, dkv, ta=True, out_kind="rows", out_buf=gb["w_kv"], name="kv_proj_dw")
            dhn = _mm(dkv, w["w_kv"], tb=True, b_kind="rows", name="kv_proj_dx")
            dh, dh_b, g["kv_norm"] = _rms_bwd(h_kv, p["kv_norm"], [dhn], res=dh, also_bf16=True, name="kv_norm_bwd")
    grads = {n: (jnp.stack(v) if isinstance(v, list) else v) for n, v in g.items()}
    return loss, dh, grads, gb


def kernel(x, mem, norm_mix, w_in, pool_w, pool_scale, kv_norm, w_kv, k_norm, q_norm, sinks, mem_norm, w_mem_kv, mem_q_norm, mem_k_norm, w_out, norm_mlp, w_up, w_down, loss_target, m_norm_mix, m_w_in, m_pool_w, m_pool_scale, m_kv_norm, m_w_kv, m_k_norm, m_q_norm, m_sinks, m_mem_norm, m_w_mem_kv, m_mem_q_norm, m_mem_k_norm, m_w_out, m_norm_mlp, m_w_up, m_w_down, v_norm_mix, v_w_in, v_pool_w, v_pool_scale, v_kv_norm, v_w_kv, v_k_norm, v_q_norm, v_sinks, v_mem_norm, v_w_mem_kv, v_mem_q_norm, v_mem_k_norm, v_w_out, v_norm_mlp, v_w_up, v_w_down):
    weights = dict(norm_mix=norm_mix, w_in=w_in, pool_w=pool_w, pool_scale=pool_scale, kv_norm=kv_norm, w_kv=w_kv,
                   k_norm=k_norm, q_norm=q_norm, sinks=sinks, mem_norm=mem_norm, w_mem_kv=w_mem_kv,
                   mem_q_norm=mem_q_norm, mem_k_norm=mem_k_norm, w_out=w_out, norm_mlp=norm_mlp, w_up=w_up, w_down=w_down)
    mom1 = dict(norm_mix=m_norm_mix, w_in=m_w_in, pool_w=m_pool_w, pool_scale=m_pool_scale, kv_norm=m_kv_norm, w_kv=m_w_kv,
                k_norm=m_k_norm, q_norm=m_q_norm, sinks=m_sinks, mem_norm=m_mem_norm, w_mem_kv=m_w_mem_kv,
                mem_q_norm=m_mem_q_norm, mem_k_norm=m_mem_k_norm, w_out=m_w_out, norm_mlp=m_norm_mlp, w_up=m_w_up,
                w_down=m_w_down)
    mom2 = dict(norm_mix=v_norm_mix, w_in=v_w_in, pool_w=v_pool_w, pool_scale=v_pool_scale, kv_norm=v_kv_norm, w_kv=v_w_kv,
                k_norm=v_k_norm, q_norm=v_q_norm, sinks=v_sinks, mem_norm=v_mem_norm, w_mem_kv=v_w_mem_kv,
                mem_q_norm=v_mem_q_norm, mem_k_norm=v_mem_k_norm, w_out=v_w_out, norm_mlp=v_norm_mlp, w_up=v_w_up,
                w_down=v_w_down)
    names = list(weights)
    core = lax.axis_index("c").astype(jnp.int32)
    shard = MAIN_WIDTH // N_DEV

    def as_layers(a):
        return a[None] if a.ndim == 2 else a

    scale_block = jnp.pad(pool_scale, ((0, 8 - N_A), (0, 128 - shard)))
    gathered = _all_gather([as_layers(weights[n]).astype(BF16) for n in GATHERED] + [scale_block], name="gather_weights")
    w = dict(zip(GATHERED, gathered))
    w["w_up"] = w["w_up"].transpose(1, 2, 0, 3).reshape(DEPTH, D_MODEL, D_FF)
    p = {n: weights[n] for n in SMALL_NAMES}
    p["pool_scale"] = gathered[-1][:, :N_A, :shard].transpose(1, 0, 2).reshape(N_A, MAIN_WIDTH)

    loss, grad_x, grads, gb = _local_step(x[0], mem[0], loss_target[0], w, p)

    sharded = list(GATHERED) + ["pool_scale"]
    gb["w_up"] = gb["w_up"].reshape(DEPTH, D_MODEL, N_DEV, D_FF // N_DEV).transpose(2, 0, 1, 3)
    gb["pool_scale"] = grads["pool_scale"].reshape(N_A, N_DEV, shard).transpose(1, 0, 2).astype(BF16)
    views = {n: _view2d(weights[n].shape) for n in sharded}
    by_core = [gb[n].reshape((N_CHIP, 2) + views[n]) for n in sharded]
    small = _pack_small(grads)
    sib_big, (sib_small,) = _sibling_exchange(by_core, [small], name="reduce_sibling")
    chip_big = [_pair_sum(a, b, core, name="chip_sum_" + n, out_dtype=BF16) for n, a, b in zip(sharded, by_core, sib_big)]
    chip_small = _pair_sum(small, sib_small, core, name="chip_sum_small", out_dtype=F32)
    parts_big, (parts_small,) = _chip_exchange(chip_big, [chip_small], name="reduce_chips")

    new = {}
    for n, parts in zip(sharded, parts_big):
        res = _adamw(parts, *(d[n].reshape(views[n]) for d in (weights, mom1, mom2)), name="adamw_" + n)
        new[n] = [r.reshape(weights[n].shape) for r in res]
    res = _adamw(parts_small, _pack_small(weights), _pack_small(mom1), _pack_small(mom2), name="adamw_replicated")
    for n, vals in zip(SMALL_NAMES, zip(*(_unpack_small(r, weights).values() for r in res))):
        new[n] = list(vals)
    outs = [new[n][k] for k in range(4) for n in names]
    total = lax.psum(loss[0, 0], AXES)
    return (total, grad_x[None], *outs)
```

```python
import math

import jax
import jax.numpy as jnp
from jax import lax
from jax.experimental import pallas as pl
from jax.experimental.pallas import tpu as pltpu

F32 = jnp.float32
BF16 = jnp.bfloat16
MESH = pl.DeviceIdType.MESH
AXES = ("x", "y", "c")

D_MODEL = 1024
DEPTH = 4
N_A = 2
HEAD_DIM = 64
MEM_HEADS = 4
MEM_WIDTH = MEM_HEADS * HEAD_DIM
MAIN_WIDTH = D_MODEL - MEM_WIDTH
POOL_GROUPS = 4
POOL_GROUP_DIM = MAIN_WIDTH // POOL_GROUPS
POOL_HALO = 16
SWA_Q_HEADS = MAIN_WIDTH // HEAD_DIM
SWA_KV_HEADS = 4
SWA_GROUP = SWA_Q_HEADS // SWA_KV_HEADS
KV_HALF = SWA_KV_HEADS * HEAD_DIM
BLOCK = 128
D_FF = 4 * D_MODEL
EPS = 1e-6
SCALE = HEAD_DIM ** -0.5
NEG = float(jnp.finfo(jnp.float32).min)
N_DEV = 8
N_CHIP = 4

ADAM_LR = 0.001
ADAM_B1 = 0.9
ADAM_B2 = 0.999
ADAM_EPS = 1e-08
ADAM_WD = 0.01
ADAM_STEP = 10

PACK_W = 512
VMEM_LIMIT = 52 * 1024 * 1024
MM_TILE = 1024

LAYERED = ("w_in", "w_mem_kv", "w_out", "w_up", "w_down")
SMALL_NAMES = ("norm_mix", "pool_w", "kv_norm", "k_norm", "q_norm", "sinks", "mem_norm", "mem_q_norm", "mem_k_norm",
               "norm_mlp")


ANY = pl.BlockSpec(memory_space=pl.ANY)


def _params(*sem):
    return pltpu.CompilerParams(dimension_semantics=sem, vmem_limit_bytes=VMEM_LIMIT)


def _mm(a, b, *, name, ta=False, tb=False, b_kind=None, layer=0, res=None, relu2=False, mul2=None, out_dtype=F32,
        out_kind=None, out_buf=None):
    if ta:
        K, M = a.shape
    else:
        M, K = a.shape
    if b_kind is None:
        rows_b, cols_b = b.shape
    elif b_kind == "rows":
        rows_b, cols_b = b.shape[0] * b.shape[2], b.shape[3]
    else:
        rows_b, cols_b = b.shape[1:]
    N, K2 = (rows_b, cols_b) if tb else (cols_b, rows_b)
    assert K == K2, (a.shape, b.shape)
    tm = min(M, MM_TILE if K <= MM_TILE else MM_TILE // 2)
    tn = min(N, MM_TILE)
    assert M % tm == 0 and N % tn == 0
    row_tile, col_tile = (tn, K) if tb else (K, tn)
    a_spec = pl.BlockSpec((K, tm), lambda j, i: (0, i)) if ta else pl.BlockSpec((tm, K), lambda j, i: (i, 0))

    def rc(j):
        return (j, 0) if tb else (0, j)

    if b_kind is None:
        b_spec = pl.BlockSpec((row_tile, col_tile), lambda j, i: rc(j))
    elif b_kind == "rows":
        per = row_tile // b.shape[2]
        b_spec = pl.BlockSpec((per, None, b.shape[2], col_tile), lambda j, i: (rc(j)[0], layer, 0, rc(j)[1]))
    else:
        b_spec = pl.BlockSpec((None, row_tile, col_tile), lambda j, i: (layer, *rc(j)))
    o_spec = pl.BlockSpec((tm, tn), lambda j, i: (i, j))
    dn = (((0 if ta else 1,), (1 if tb else 0,)), ((), ()))
    extra = [e for e in (res, mul2) if e is not None]
    n_out = 2 if relu2 else 1
    n_in = 2 + len(extra) + (1 if out_buf is not None else 0)

    def body(*refs):
        a_ref, b_ref = refs[0], refs[1]
        extra_refs = refs[2:2 + len(extra)]
        outs = refs[n_in:n_in + n_out]
        bv = b_ref[...].astype(BF16).reshape(row_tile, col_tile)
        v = lax.dot_general(a_ref[...].astype(BF16), bv, dn, preferred_element_type=F32)
        if res is not None:
            v = extra_refs[0][...] + v
        elif mul2 is not None:
            v = v * (2.0 * extra_refs[0][...].astype(F32))
        if relu2:
            r = jnp.maximum(v, 0.0)
            outs[0][...] = r.astype(BF16)
            outs[1][...] = (r * r).astype(BF16)
        else:
            outs[0][...] = v.astype(outs[0].dtype).reshape(outs[0].shape)

    in_specs = [a_spec, b_spec] + [o_spec] * len(extra)
    operands = [a, b, *extra]
    aliases = {}
    if out_kind is None:
        out_shape = jax.ShapeDtypeStruct((M, N), BF16 if relu2 else out_dtype)
        out_specs = o_spec
    else:
        if out_kind == "rows":
            s = out_buf.shape[2]
            out_specs = pl.BlockSpec((tm // s, None, s, tn), lambda j, i: (i, layer, 0, j))
        else:
            out_specs = pl.BlockSpec((None, tm, tn), lambda j, i: (layer, i, j))
        out_shape = jax.ShapeDtypeStruct(out_buf.shape, out_buf.dtype)
        in_specs.append(ANY)
        operands.append(out_buf)
        aliases = {len(operands) - 1: 0}
    if relu2:
        out_shape, out_specs = (out_shape, out_shape), (out_specs, out_specs)
    return pl.pallas_call(
        body, name=name, grid=(N // tn, M // tm), in_specs=in_specs, out_specs=out_specs, out_shape=out_shape,
        input_output_aliases=aliases, compiler_params=_params("parallel", "parallel"),
    )(*operands)


def _row_tile(rows, d):
    t = min(rows, (512 * 1024) // d)
    while rows % t or (t != rows and t % 16):
        t -= 1
    return t


def _rms_fwd(x, g, *, name, out_dtype=BF16):
    R, D = x.shape
    tr = _row_tile(R, D)

    def body(x_ref, g_ref, o_ref):
        xv = x_ref[...].astype(F32)
        r = lax.rsqrt(jnp.mean(xv * xv, axis=-1, keepdims=True) + EPS)
        o_ref[...] = ((xv * r) * g_ref[...]).astype(o_ref.dtype)

    return pl.pallas_call(
        body, name=name, grid=(R // tr,),
        in_specs=[pl.BlockSpec((tr, D), lambda i: (i, 0)), pl.BlockSpec((1, D), lambda i: (0, 0))],
        out_specs=pl.BlockSpec((tr, D), lambda i: (i, 0)), out_shape=jax.ShapeDtypeStruct((R, D), out_dtype),
        compiler_params=_params("parallel"),
    )(x, g.reshape(1, D))


def _rms_bwd(x, g, dys, *, name, res=None, want_dx=True, also_bf16=False):
    R, D = x.shape
    tr = _row_tile(R, D)
    n_dy = len(dys)
    has_res = res is not None

    def body(*refs):
        x_ref, g_ref = refs[0], refs[1]
        dy_refs = refs[2:2 + n_dy]
        res_ref = refs[2 + n_dy] if has_res else None
        outs = refs[2 + n_dy + (1 if has_res else 0):]
        dg_ref = outs[-1]
        i = pl.program_id(0)
        xv = x_ref[...].astype(F32)
        dy = dy_refs[0][...].astype(F32)
        for extra in dy_refs[1:]:
            dy = dy + extra[...].astype(F32)
        r = lax.rsqrt(jnp.mean(xv * xv, axis=-1, keepdims=True) + EPS)
        xh = xv * r
        part = jnp.sum(dy * xh, axis=0, keepdims=True)

        @pl.when(i == 0)
        def _():
            dg_ref[...] = part

        @pl.when(i > 0)
        def _():
            dg_ref[...] += part

        if want_dx:
            gdy = dy * g_ref[...]
            dx = r * (gdy - xh * jnp.mean(gdy * xh, axis=-1, keepdims=True))
            if has_res:
                dx = res_ref[...] + dx
            outs[0][...] = dx
            if also_bf16:
                outs[1][...] = dx.astype(BF16)

    row = pl.BlockSpec((tr, D), lambda i: (i, 0))
    vec = pl.BlockSpec((1, D), lambda i: (0, 0))
    out_shape = [jax.ShapeDtypeStruct((1, D), F32)]
    out_specs = [vec]
    if also_bf16:
        out_shape = [jax.ShapeDtypeStruct((R, D), BF16)] + out_shape
        out_specs = [row] + out_specs
    if want_dx:
        out_shape = [jax.ShapeDtypeStruct((R, D), F32)] + out_shape
        out_specs = [row] + out_specs
    outs = pl.pallas_call(
        body, name=name, grid=(R // tr,),
        in_specs=[row, vec] + [row] * (n_dy + (1 if has_res else 0)), out_specs=out_specs, out_shape=out_shape,
        compiler_params=_params("arbitrary"),
    )(x, g.reshape(1, D), *dys, *([res] if has_res else []))
    return (*outs[:-1], outs[-1].reshape(D)) if want_dx else outs[0].reshape(D)


def _add(a, b, *, name):
    R, D = a.shape
    tr = _row_tile(R, D)

    def body(a_ref, b_ref, o_ref):
        o_ref[...] = a_ref[...] + b_ref[...]

    row = pl.BlockSpec((tr, D), lambda i: (i, 0))
    return pl.pallas_call(body, name=name, grid=(R // tr,), in_specs=[row, row], out_specs=row,
                          out_shape=jax.ShapeDtypeStruct((R, D), a.dtype), compiler_params=_params("parallel"))(a, b)


POOL_TILE = 512


def _pool_window(group):
    return lax.shift_left(jnp.int32(2), group)


def _pool_diff(u_ref, halo_ref, group, tile):
    first = tile == 0
    halo = jnp.where(first, 0.0, halo_ref[...])
    ext = jnp.concatenate([halo, u_ref[...]], axis=0)
    n = ext.shape[0]
    s1 = ext + pltpu.roll(ext, 1, 0)
    s2 = s1 + pltpu.roll(s1, 2, 0)
    s3 = s2 + pltpu.roll(s2, 4, 0)
    s4 = s3 + pltpu.roll(s3, 8, 0)
    ws = jnp.where(group == 0, s1, jnp.where(group == 1, s2, jnp.where(group == 2, s3, s4)))[POOL_HALO:n]
    t = tile * POOL_TILE + lax.broadcasted_iota(jnp.int32, (POOL_TILE, 1), 0)
    cnt = jnp.minimum(t + 1, _pool_window(group)).astype(F32)
    return ws / cnt - u_ref[...], cnt


def _pool_specs():
    per_tile = POOL_TILE // POOL_HALO
    cur = pl.BlockSpec((None, POOL_TILE, POOL_GROUP_DIM), lambda g, i: (g, i, 0))
    prev = pl.BlockSpec((None, POOL_HALO, POOL_GROUP_DIM), lambda g, i: (g, jnp.maximum(i * per_tile - 1, 0), 0))
    pw = pl.BlockSpec((None, POOL_GROUP_DIM, POOL_GROUP_DIM), lambda g, i: (g, 0, 0))
    vec = pl.BlockSpec((None, 1, POOL_GROUP_DIM), lambda g, i: (g, 0, 0))
    return cur, prev, pw, vec


def _pool_fwd(u, pw, scale, *, name):
    G, T, C = u.shape
    assert T % POOL_TILE == 0
    cur, prev, pw_spec, vec = _pool_specs()

    def body(u_ref, halo_ref, pw_ref, sc_ref, o_ref):
        d, _ = _pool_diff(u_ref, halo_ref, pl.program_id(0), pl.program_id(1))
        mixed = jnp.dot(d.astype(BF16), pw_ref[...].astype(BF16), preferred_element_type=F32)
        o_ref[...] = (mixed * sc_ref[...]).astype(o_ref.dtype)

    return pl.pallas_call(
        body, name=name, grid=(G, T // POOL_TILE), in_specs=[cur, prev, pw_spec, vec], out_specs=cur,
        out_shape=jax.ShapeDtypeStruct((G, T, C), BF16), compiler_params=_params("parallel", "parallel"),
    )(u, u, pw, scale)


def _pool_bwd(u, pw, scale, dout, *, name):
    G, T, C = u.shape
    nt = T // POOL_TILE
    per_tile = POOL_TILE // POOL_HALO
    cur, prev, pw_spec, vec = _pool_specs()
    nxt = pl.BlockSpec((None, POOL_HALO, C), lambda g, i: (g, jnp.minimum((i + 1) * per_tile, nt * per_tile - 1), 0))

    def body(u_ref, halo_ref, pw_ref, sc_ref, do_ref, donext_ref, du_ref, dpw_ref, dsc_ref):
        group, tile = pl.program_id(0), pl.program_id(1)
        d, cnt = _pool_diff(u_ref, halo_ref, group, tile)
        pwb = pw_ref[...].astype(BF16)
        db = d.astype(BF16)
        mixed = jnp.dot(db, pwb, preferred_element_type=F32)
        dout = do_ref[...].astype(F32)
        dsc = jnp.sum(dout * mixed, axis=0, keepdims=True)
        sc = sc_ref[...]
        dmix = (dout * sc).astype(BF16)
        dpw = lax.dot_general(db, dmix, (((0,), (0,)), ((), ())), preferred_element_type=F32)

        @pl.when(tile == 0)
        def _():
            dpw_ref[...] = dpw
            dsc_ref[...] = dsc

        @pl.when(tile > 0)
        def _():
            dpw_ref[...] += dpw
            dsc_ref[...] += dsc

        last = tile == nt - 1
        dnext = jnp.where(last, 0.0, donext_ref[...].astype(F32))
        dmix_ext = jnp.concatenate([dmix, (dnext * sc).astype(BF16)], axis=0)
        dd_ext = lax.dot_general(dmix_ext, pwb, (((1,), (1,)), ((), ())), preferred_element_type=F32)
        window = _pool_window(group).astype(F32)
        cnt_ext = jnp.concatenate([cnt, jnp.broadcast_to(window, (POOL_HALO, 1))], axis=0)
        q = dd_ext / cnt_ext
        n = q.shape[0]
        r1 = q + pltpu.roll(q, n - 1, 0)
        r2 = r1 + pltpu.roll(r1, n - 2, 0)
        r3 = r2 + pltpu.roll(r2, n - 4, 0)
        r4 = r3 + pltpu.roll(r3, n - 8, 0)
        back = jnp.where(group == 0, r1, jnp.where(group == 1, r2, jnp.where(group == 2, r3, r4)))
        du_ref[...] = back[0:POOL_TILE] - dd_ext[0:POOL_TILE]

    return pl.pallas_call(
        body, name=name, grid=(G, nt), in_specs=[cur, prev, pw_spec, vec, cur, nxt],
        out_specs=(cur, pw_spec, vec),
        out_shape=(jax.ShapeDtypeStruct((G, T, C), F32), jax.ShapeDtypeStruct((G, C, C), F32),
                   jax.ShapeDtypeStruct((G, 1, C), F32)),
        compiler_params=_params("arbitrary", "arbitrary"),
    )(u, u, pw, scale, dout, dout)


def _softmax(q, k, bias, valid, sink):
    s = lax.dot_general(q, k, (((1,), (1,)), ((), ())), preferred_element_type=F32) * SCALE
    if bias is not None:
        s = s - bias
    if valid is not None:
        s = jnp.where(valid, s, NEG)
    m = jnp.max(s, axis=-1, keepdims=True)
    if sink is not None:
        m = jnp.maximum(m, sink)
    e = jnp.exp(s - m)
    z = jnp.sum(e, axis=-1, keepdims=True)
    if sink is None:
        return e * (1.0 / z), None
    es = jnp.exp(sink - m)
    inv = 1.0 / (z + es)
    return e * inv, es * inv


def _swa_terms(sink_ref, kvh, blk):
    rows = SWA_GROUP * BLOCK
    row = lax.broadcasted_iota(jnp.int32, (rows, 1), 0)
    grp = row // BLOCK
    head = (kvh * SWA_GROUP + grp + 1).astype(F32)
    slope = jnp.exp(head * (-8.0 * math.log(2.0) / SWA_Q_HEADS))
    qi = lax.broadcasted_iota(jnp.int32, (rows, 2 * BLOCK), 0) % BLOCK
    kj = lax.broadcasted_iota(jnp.int32, (rows, 2 * BLOCK), 1)
    dist = qi + BLOCK - kj
    valid = (dist >= 0) & (dist < BLOCK) & ((blk > 0) | (kj >= BLOCK))
    bias = slope * dist.astype(F32)
    s0, s1, s2 = (sink_ref[kvh * SWA_GROUP + g] for g in range(SWA_GROUP))
    sink = jnp.where(grp == 0, s0, jnp.where(grp == 1, s1, s2))
    return bias, valid, sink, grp


def _swa_fwd(q, k, v, sinks, *, name):
    H, G, T, hd = q.shape
    nb = T // BLOCK
    rows = G * BLOCK

    def body(sink_ref, q_ref, kp_ref, kc_ref, vp_ref, vc_ref, o_ref):
        kvh, blk = pl.program_id(0), pl.program_id(1)
        bias, valid, sink, _ = _swa_terms(sink_ref, kvh, blk)
        kk = jnp.concatenate([kp_ref[...], kc_ref[...]], axis=0)
        vv = jnp.concatenate([vp_ref[...], vc_ref[...]], axis=0)
        p, _ = _softmax(q_ref[...].reshape(rows, hd), kk, bias, valid, sink)
        o = jnp.dot(p.astype(BF16), vv, preferred_element_type=F32)
        o_ref[...] = o.reshape(G, BLOCK, hd).astype(o_ref.dtype)

    qs = pl.BlockSpec((None, G, BLOCK, hd), lambda h, n: (h, 0, n, 0))
    prev = pl.BlockSpec((None, BLOCK, hd), lambda h, n: (h, jnp.maximum(n - 1, 0), 0))
    cur = pl.BlockSpec((None, BLOCK, hd), lambda h, n: (h, n, 0))
    return pl.pallas_call(
        body, name=name, grid=(H, nb),
        in_specs=[pl.BlockSpec(memory_space=pltpu.SMEM), qs, prev, cur, prev, cur], out_specs=qs,
        out_shape=jax.ShapeDtypeStruct((H, G, T, hd), BF16), compiler_params=_params("parallel", "parallel"),
    )(sinks, q, k, k, v, v)


def _swa_bwd(q, k, v, sinks, do, *, name):
    H, G, T, hd = q.shape
    nb = T // BLOCK
    rows = G * BLOCK

    def body(sink_ref, q_ref, do_ref, kp_ref, kc_ref, vp_ref, vc_ref, dq_ref, dk_ref, dv_ref, ds_ref, ck, cv):
        kvh, blk = pl.program_id(0), pl.program_id(1)

        @pl.when(blk == 0)
        def _():
            ck[...] = jnp.zeros_like(ck)
            cv[...] = jnp.zeros_like(cv)
            ds_ref[...] = jnp.zeros_like(ds_ref)

        @pl.when(blk < nb)
        def _():
            bias, valid, sink, grp = _swa_terms(sink_ref, kvh, blk)
            kk = jnp.concatenate([kp_ref[...], kc_ref[...]], axis=0)
            vv = jnp.concatenate([vp_ref[...], vc_ref[...]], axis=0)
            qq = q_ref[...].reshape(rows, hd)
            dout = do_ref[...].reshape(rows, hd)
            p, ps = _softmax(qq, kk, bias, valid, sink)
            dp = lax.dot_general(dout, vv, (((1,), (1,)), ((), ())), preferred_element_type=F32)
            dsum = jnp.sum(p * dp, axis=-1, keepdims=True)
            ds = (p * (dp - dsum)).astype(BF16)
            dq = jnp.dot(ds, kk, preferred_element_type=F32) * SCALE
            dq_ref[...] = dq.reshape(G, BLOCK, hd)
            dk = lax.dot_general(ds, qq, (((0,), (0,)), ((), ())), preferred_element_type=F32) * SCALE
            dv = lax.dot_general(p.astype(BF16), dout, (((0,), (0,)), ((), ())), preferred_element_type=F32)
            dk_ref[...] = ck[...] + dk[0:BLOCK]
            dv_ref[...] = cv[...] + dv[0:BLOCK]
            ck[...] = dk[BLOCK:2 * BLOCK]
            cv[...] = dv[BLOCK:2 * BLOCK]
            dsink = -(ps * dsum)
            lane = lax.broadcasted_iota(jnp.int32, (1, 128), 1)
            acc = jnp.zeros((1, 128), F32)
            for g in range(G):
                acc = acc + jnp.where(lane == g, jnp.sum(jnp.where(grp == g, dsink, 0.0)), 0.0)
            ds_ref[...] += acc

        @pl.when(blk == nb)
        def _():
            dk_ref[...] = ck[...]
            dv_ref[...] = cv[...]

    def at(n):
        return jnp.minimum(n, nb - 1)

    qs = pl.BlockSpec((None, G, BLOCK, hd), lambda h, n: (h, 0, at(n), 0))
    prev = pl.BlockSpec((None, BLOCK, hd), lambda h, n: (h, jnp.maximum(at(n) - 1, 0), 0))
    cur = pl.BlockSpec((None, BLOCK, hd), lambda h, n: (h, at(n), 0))
    late = pl.BlockSpec((None, BLOCK, hd), lambda h, n: (h, jnp.maximum(n - 1, 0), 0))
    dsink_spec = pl.BlockSpec((None, 1, 128), lambda h, n: (h, 0, 0))
    return pl.pallas_call(
        body, name=name, grid=(H, nb + 1),
        in_specs=[pl.BlockSpec(memory_space=pltpu.SMEM), qs, qs, prev, cur, prev, cur],
        out_specs=(qs, late, late, dsink_spec),
        out_shape=(jax.ShapeDtypeStruct((H, G, T, hd), F32), jax.ShapeDtypeStruct((H, T, hd), F32),
                   jax.ShapeDtypeStruct((H, T, hd), F32), jax.ShapeDtypeStruct((H, 1, 128), F32)),
        scratch_shapes=[pltpu.VMEM((BLOCK, hd), F32), pltpu.VMEM((BLOCK, hd), F32)],
        compiler_params=_params("arbitrary", "arbitrary"),
    )(sinks, q, do, k, k, v, v)


MEM_Q_TILE = 512


def _mem_fwd(q, k, v, *, name):
    H, T, hd = q.shape
    M = k.shape[1]
    tq = min(T, MEM_Q_TILE)

    def body(q_ref, k_ref, v_ref, o_ref):
        p, _ = _softmax(q_ref[...], k_ref[...], None, None, None)
        o_ref[...] = jnp.dot(p.astype(BF16), v_ref[...], preferred_element_type=F32).astype(o_ref.dtype)

    qs = pl.BlockSpec((None, tq, hd), lambda h, i: (h, i, 0))
    ks = pl.BlockSpec((None, M, hd), lambda h, i: (h, 0, 0))
    return pl.pallas_call(body, name=name, grid=(H, T // tq), in_specs=[qs, ks, ks], out_specs=qs,
                          out_shape=jax.ShapeDtypeStruct((H, T, hd), BF16),
                          compiler_params=_params("parallel", "parallel"))(q, k, v)


def _mem_bwd(q, k, v, do, *, name):
    H, T, hd = q.shape
    M = k.shape[1]
    tq = min(T, MEM_Q_TILE)

    def body(q_ref, do_ref, k_ref, v_ref, dq_ref, dk_ref, dv_ref):
        i = pl.program_id(1)
        qq, kk, vv, dout = q_ref[...], k_ref[...], v_ref[...], do_ref[...]
        p, _ = _softmax(qq, kk, None, None, None)
        dp = lax.dot_general(dout, vv, (((1,), (1,)), ((), ())), preferred_element_type=F32)
        dsum = jnp.sum(p * dp, axis=-1, keepdims=True)
        ds = (p * (dp - dsum)).astype(BF16)
        dq_ref[...] = jnp.dot(ds, kk, preferred_element_type=F32) * SCALE
        dk = lax.dot_general(ds, qq, (((0,), (0,)), ((), ())), preferred_element_type=F32) * SCALE
        dv = lax.dot_general(p.astype(BF16), dout, (((0,), (0,)), ((), ())), preferred_element_type=F32)

        @pl.when(i == 0)
        def _():
            dk_ref[...] = dk
            dv_ref[...] = dv

        @pl.when(i > 0)
        def _():
            dk_ref[...] += dk
            dv_ref[...] += dv

    qs = pl.BlockSpec((None, tq, hd), lambda h, i: (h, i, 0))
    ks = pl.BlockSpec((None, M, hd), lambda h, i: (h, 0, 0))
    return pl.pallas_call(
        body, name=name, grid=(H, T // tq), in_specs=[qs, qs, ks, ks], out_specs=(qs, ks, ks),
        out_shape=(jax.ShapeDtypeStruct((H, T, hd), F32), jax.ShapeDtypeStruct((H, M, hd), F32),
                   jax.ShapeDtypeStruct((H, M, hd), F32)),
        compiler_params=_params("arbitrary", "arbitrary"),
    )(q, do, k, v)


def _loss(y, target, *, name):
    T, D = y.shape
    tr = _row_tile(T, D)

    def body(y_ref, t_ref, l_ref, dy_ref, dyb_ref):
        i = pl.program_id(0)
        err = y_ref[...] - t_ref[...]
        dy = err / float(D)
        dy_ref[...] = dy
        dyb_ref[...] = dy.astype(BF16)
        part = jnp.full((8, 128), 0.5 * jnp.sum(jnp.mean(err * err, axis=-1)), F32)

        @pl.when(i == 0)
        def _():
            l_ref[...] = part

        @pl.when(i > 0)
        def _():
            l_ref[...] += part

    row = pl.BlockSpec((tr, D), lambda i: (i, 0))
    return pl.pallas_call(
        body, name=name, grid=(T // tr,), in_specs=[row, row],
        out_specs=(pl.BlockSpec((8, 128), lambda i: (0, 0)), row, row),
        out_shape=(jax.ShapeDtypeStruct((8, 128), F32), jax.ShapeDtypeStruct((T, D), F32), jax.ShapeDtypeStruct((T, D), BF16)),
        compiler_params=_params("arbitrary"),
    )(y, target)


def _position():
    return lax.axis_index("x"), lax.axis_index("y"), lax.axis_index("c")


def _all_gather(arrays, *, name):
    n = len(arrays)

    def body(*refs):
        srcs, outs = refs[:n], refs[n:2 * n]
        send_sems, recv_sems, local_sems = refs[2 * n:]
        x, y, c = _position()
        me, sibling = (x, y, c), (x, y, 1 - c)
        chips = [(1 - x, y), (x, 1 - y), (1 - x, 1 - y)]

        def slot(a, px, py, pc):
            return outs[a].at[4 * px + 2 * py + pc]

        def copy(a, k, block, to, src=None):
            return pltpu.make_async_remote_copy(
                src_ref=slot(a, *block) if src is None else src, dst_ref=slot(a, *block),
                send_sem=send_sems.at[a, k], recv_sem=recv_sems.at[a, k], device_id=to, device_id_type=MESH)

        mine = [pltpu.make_async_copy(srcs[a], slot(a, *me), local_sems.at[a]) for a in range(n)]
        for cp in mine:
            cp.start()
        first, passed = [], []
        for a in range(n):
            first.append(copy(a, 0, me, sibling, src=srcs[a]))
            first += [copy(a, 1 + j, me, (*chip, c), src=srcs[a]) for j, chip in enumerate(chips)]
        for cp in first:
            cp.start()
        for a in range(n):
            for j, chip in enumerate(chips):
                copy(a, 1 + j, (*chip, c), me).wait_recv()
                fwd = copy(a, 4 + j, (*chip, c), sibling)
                fwd.start()
                passed.append(fwd)
        for a in range(n):
            copy(a, 0, sibling, me).wait_recv()
            for j, chip in enumerate(chips):
                copy(a, 4 + j, (*chip, 1 - c), me).wait_recv()
        for cp in first + passed:
            cp.wait_send()
        for cp in mine:
            cp.wait()

    return pl.pallas_call(
        body, name=name, in_specs=[ANY] * n, out_specs=[ANY] * n,
        out_shape=[jax.ShapeDtypeStruct((N_DEV,) + a.shape, a.dtype) for a in arrays],
        scratch_shapes=[pltpu.SemaphoreType.DMA((n, 7)), pltpu.SemaphoreType.DMA((n, 7)), pltpu.SemaphoreType.DMA((n,))],
    )(*arrays)


def _sibling_exchange(by_core, whole, *, name):
    n1, n = len(by_core), len(by_core) + len(whole)

    def body(*refs):
        srcs, outs = refs[:n], refs[n:2 * n]
        send_sems, recv_sems = refs[2 * n:]
        x, y, c = _position()
        copies = [
            pltpu.make_async_remote_copy(src_ref=srcs[a].at[:, 1 - c] if a < n1 else srcs[a], dst_ref=outs[a],
                                         send_sem=send_sems.at[a], recv_sem=recv_sems.at[a], device_id=(x, y, 1 - c),
                                         device_id_type=MESH)
            for a in range(n)]
        for cp in copies:
            cp.start()
        for cp in copies:
            cp.wait()

    out_shape = [jax.ShapeDtypeStruct(a.shape[:1] + a.shape[2:], a.dtype) for a in by_core]
    out_shape += [jax.ShapeDtypeStruct(a.shape, a.dtype) for a in whole]
    outs = pl.pallas_call(
        body, name=name, in_specs=[ANY] * n, out_specs=[ANY] * n, out_shape=out_shape,
        scratch_shapes=[pltpu.SemaphoreType.DMA((n,)), pltpu.SemaphoreType.DMA((n,))],
    )(*by_core, *whole)
    return outs[:n1], outs[n1:]


def _chip_exchange(per_chip, whole, *, name):
    n1, n = len(per_chip), len(per_chip) + len(whole)

    def body(*refs):
        srcs, outs = refs[:n], refs[n:2 * n]
        send_sems, recv_sems, local_sems = refs[2 * n:]
        x, y, c = _position()
        my_chip = 2 * x + y
        chips = [(1 - x, y), (x, 1 - y), (1 - x, 1 - y)]

        def src(a, chip):
            return srcs[a].at[chip] if a < n1 else srcs[a]

        local = [pltpu.make_async_copy(src(a, my_chip), outs[a].at[my_chip], local_sems.at[a]) for a in range(n)]
        for cp in local:
            cp.start()
        copies = [
            pltpu.make_async_remote_copy(src_ref=src(a, 2 * px + py), dst_ref=outs[a].at[my_chip],
                                         send_sem=send_sems.at[a, j], recv_sem=recv_sems.at[a, j], device_id=(px, py, c),
                                         device_id_type=MESH)
            for a in range(n) for j, (px, py) in enumerate(chips)]
        for cp in copies:
            cp.start()
        for cp in copies:
            cp.wait()
        for cp in local:
            cp.wait()

    out_shape = [jax.ShapeDtypeStruct(a.shape, a.dtype) for a in per_chip]
    out_shape += [jax.ShapeDtypeStruct((N_CHIP,) + a.shape, a.dtype) for a in whole]
    outs = pl.pallas_call(
        body, name=name, in_specs=[ANY] * n, out_specs=[ANY] * n, out_shape=out_shape,
        scratch_shapes=[pltpu.SemaphoreType.DMA((n, 3)), pltpu.SemaphoreType.DMA((n, 3)), pltpu.SemaphoreType.DMA((n,))],
    )(*per_chip, *whole)
    return outs[:n1], outs[n1:]


HBM = pl.BlockSpec(memory_space=pltpu.HBM)
SEM = pl.BlockSpec(memory_space=pltpu.SEMAPHORE)
DATAFLOW = pltpu.SideEffectType.DATAFLOW_SIDE_EFFECTING


def _device(flat):
    return flat // 4, (flat // 2) % 2, flat % 2


def _gather_copies(srcs, lands, send_sems, recv_sems, incoming):
    x, y, c = _position()
    me = 4 * x + 2 * y + c
    pairs = []
    for a in range(len(srcs)):
        for d in range(1, N_DEV):
            to, frm = (me + d) % N_DEV, (me + N_DEV - d) % N_DEV
            k = a * (N_DEV - 1) + d - 1
            sems = dict(send_sem=send_sems.at[k], recv_sem=recv_sems.at[k], device_id_type=MESH)
            out = pltpu.make_async_remote_copy(src_ref=srcs[a], dst_ref=lands[a].at[me], device_id=_device(to), **sems)
            inc = pltpu.make_async_remote_copy(src_ref=srcs[a], dst_ref=lands[a].at[frm], device_id=_device(frm),
                                               **sems) if incoming else None
            pairs.append((out, inc))
    return pairs


def _chip_copies(srcs, lands, send_sems, recv_sems, incoming):
    x, y, c = _position()
    my_chip = 2 * x + y
    pairs = []
    for a in range(len(srcs)):
        for k, (px, py) in enumerate([(1 - x, y), (x, 1 - y), (1 - x, 1 - y)]):
            sem = a * (N_CHIP - 1) + k
            sems = dict(send_sem=send_sems.at[sem], recv_sem=recv_sems.at[sem], device_id=(px, py, c), device_id_type=MESH)
            out = pltpu.make_async_remote_copy(src_ref=srcs[a].at[2 * px + py], dst_ref=lands[a].at[my_chip], **sems)
            inc = pltpu.make_async_remote_copy(src_ref=srcs[a].at[2 * px + py], dst_ref=lands[a].at[2 * px + py],
                                               **sems) if incoming else None
            pairs.append((out, inc))
    return pairs


def _push_start(copies, fan, srcs, lands, *, name):
    n = len(srcs)

    def body(*refs):
        src_refs, land_refs = refs[:n], refs[n:2 * n]
        send_sems, recv_sems = refs[2 * n], refs[2 * n + 1]
        token = refs[-1]
        for out, _ in copies(src_refs, land_refs, send_sems, recv_sems, False):
            out.start()
        token[...] = jnp.zeros_like(token)

    outs = pl.pallas_call(
        body, name=name,
        out_shape=(pltpu.SemaphoreType.DMA((n * fan,)), pltpu.SemaphoreType.DMA((n * fan,)),
                   *(pltpu.HBM(a.shape, a.dtype) for a in srcs), *(pltpu.HBM(a.shape, a.dtype) for a in lands),
                   jax.ShapeDtypeStruct((8, 128), F32)),
        in_specs=[HBM] * (2 * n), out_specs=(SEM, SEM, *([HBM] * (2 * n)), pl.BlockSpec(memory_space=pltpu.VMEM)),
        input_output_aliases={i: 2 + i for i in range(2 * n)},
        compiler_params=pltpu.CompilerParams(has_side_effects=DATAFLOW),
    )(*(pltpu.with_memory_space_constraint(a, pltpu.HBM) for a in (*srcs, *lands)))
    return outs[0], outs[1], list(outs[2:2 + n]), list(outs[2 + n:2 + 2 * n]), outs[-1]


def _push_wait(copies, send_sems, recv_sems, srcs, lands, after, *, name):
    n = len(srcs)

    def body(*refs):
        src_refs, land_refs = refs[:n], refs[n:2 * n]
        for out, inc in copies(src_refs, land_refs, refs[2 * n], refs[2 * n + 1], True):
            out.wait_send()
            inc.wait_recv()

    outs = pl.pallas_call(
        body, name=name,
        out_shape=tuple(pltpu.HBM(a.shape, a.dtype) for a in (*srcs, *lands)),
        in_specs=[HBM] * (2 * n) + [SEM, SEM, ANY], out_specs=tuple([HBM] * (2 * n)),
        input_output_aliases={i: i for i in range(2 * n)},
        compiler_params=pltpu.CompilerParams(has_side_effects=DATAFLOW),
    )(*srcs, *lands, send_sems, recv_sems, after)
    return list(outs[n:])


def _with_own_slot(block, index, slots):
    buf = lax.empty((slots,) + block.shape, block.dtype)
    return lax.dynamic_update_slice(buf, block[None], (index,) + (0,) * block.ndim)


def _view2d(shape):
    return math.prod(shape[:-1]), shape[-1]


def _pair_sum(mine, other, core, *, name, out_dtype):
    by_core = mine.ndim == 4
    n, w = other.shape[-2:]
    tr = _row_tile(n, w * 2)
    lead = other.shape[0] if by_core else 1

    def body(core_ref, a_ref, b_ref, o_ref):
        o_ref[...] = (a_ref[...].astype(F32) + b_ref[...].astype(F32)).astype(o_ref.dtype)

    if by_core:
        a_spec = pl.BlockSpec((None, None, tr, w), lambda j, i, core_ref: (j, core_ref[0], i, 0))
        o_spec = pl.BlockSpec((None, tr, w), lambda j, i, core_ref: (j, i, 0))
    else:
        a_spec = o_spec = pl.BlockSpec((tr, w), lambda j, i, core_ref: (i, 0))
    grid_spec = pltpu.PrefetchScalarGridSpec(num_scalar_prefetch=1, grid=(lead, n // tr), in_specs=[a_spec, o_spec],
                                             out_specs=o_spec)
    return pl.pallas_call(body, name=name, grid_spec=grid_spec, out_shape=jax.ShapeDtypeStruct(other.shape, out_dtype),
                          compiler_params=_params("parallel", "parallel"))(core.reshape(1), mine, other)


def _adamw(parts, w, m, v, *, name):
    layers = len(parts)
    n_parts, R, W = parts[0].shape
    tr = _row_tile(R, W * 2)
    per_layer = R // tr

    def update(p_ref, w_ref, m_ref, v_ref, g_out, d_out, m_out, v_out):
        g = p_ref[0].astype(F32)
        for j in range(1, n_parts):
            g = g + p_ref[j].astype(F32)
        m_new = ADAM_B1 * m_ref[...] + (1.0 - ADAM_B1) * g
        v_new = ADAM_B2 * v_ref[...] + (1.0 - ADAM_B2) * (g * g)
        m_hat = m_new / (1.0 - ADAM_B1 ** ADAM_STEP)
        v_hat = v_new / (1.0 - ADAM_B2 ** ADAM_STEP)
        g_out[...] = g
        d_out[...] = -ADAM_LR * (m_hat / (jnp.sqrt(v_hat) + ADAM_EPS) + ADAM_WD * w_ref[...])
        m_out[...] = m_new
        v_out[...] = v_new

    def body(*refs):
        for k in range(layers):
            pl.when(pl.program_id(0) == k)(lambda k=k: update(refs[k], *refs[layers:]))

    def parts_spec(k):
        return pl.BlockSpec((n_parts, tr, W), lambda l, i: (0, jnp.where(l == k, i, 0), 0))

    row = pl.BlockSpec((tr, W), lambda l, i: (l * per_layer + i, 0))
    out = jax.ShapeDtypeStruct((layers * R, W), F32)
    return pl.pallas_call(
        body, name=name, grid=(layers, per_layer), in_specs=[parts_spec(k) for k in range(layers)] + [row, row, row],
        out_specs=(row, row, row, row), out_shape=(out, out, out, out), compiler_params=_params("arbitrary", "arbitrary"),
    )(*parts, w, m, v)


SMALL_ROWS = 608


def _pack_small(p):
    flat = jnp.concatenate([p[n].reshape(-1).astype(F32) for n in SMALL_NAMES])
    return jnp.pad(flat, (0, SMALL_ROWS * PACK_W - flat.shape[0])).reshape(SMALL_ROWS, PACK_W)


def _unpack_small(buf, like):
    out, at = {}, 0
    flat = buf.reshape(-1)
    for n in SMALL_NAMES:
        size = math.prod(like[n].shape)
        out[n] = flat[at:at + size].reshape(like[n].shape)
        at += size
    return out


def _heads(a, nh):
    T = a.shape[0]
    return a.reshape(T, nh, HEAD_DIM).transpose(1, 0, 2).reshape(nh * T, HEAD_DIM)


def _unheads(a, nh):
    a = a.reshape(nh, -1, HEAD_DIM)
    return a.transpose(1, 0, 2).reshape(a.shape[1], nh * HEAD_DIM)


def _groups(a):
    T = a.shape[0]
    return a.reshape(T, POOL_GROUPS, POOL_GROUP_DIM).transpose(1, 0, 2)


def _ungroups(a):
    return a.transpose(1, 0, 2).reshape(a.shape[1], MAIN_WIDTH)


def _local_step(x, mem, target, p, w_kv, fetch, reduce_layer, reduce_wait):
    T = x.shape[0]
    M = mem.shape[0]
    saved = []
    h = x
    kn = vv = k_raw = h_kv = hn_kv = None
    for l in range(DEPTH):
        s = {}
        wl, token = fetch(l, h)
        s["w"] = wl
        if l == N_A:
            h_kv = h
            hn_kv = _rms_fwd(h, p["kv_norm"], name="kv_norm_fwd")
            kv = _mm(hn_kv, w_kv, b_kind="rows", name="kv_proj")
            k_raw = _heads(kv[:, :KV_HALF], SWA_KV_HEADS)
            kn = _rms_fwd(k_raw, p["k_norm"], name="k_norm_fwd").reshape(SWA_KV_HEADS, T, HEAD_DIM)
            vv = _heads(kv[:, KV_HALF:], SWA_KV_HEADS).astype(BF16).reshape(SWA_KV_HEADS, T, HEAD_DIM)
        s["h"] = h
        s["xn1"] = _rms_fwd(h, p["norm_mix"][l] + token, name="norm_mix_fwd")
        proj = _mm(s["xn1"], wl["w_in"], b_kind="rows", name="in_proj")
        s["mq_raw"] = _heads(proj[:, MAIN_WIDTH:], MEM_HEADS)
        s["mqn"] = _rms_fwd(s["mq_raw"], p["mem_q_norm"][l], name="mem_q_norm_fwd").reshape(MEM_HEADS, T, HEAD_DIM)
        s["memn"] = _rms_fwd(mem, p["mem_norm"][l], name="mem_norm_fwd")
        mkv = _mm(s["memn"], wl["w_mem_kv"], b_kind="rows", name="mem_kv_proj")
        s["mk_raw"] = _heads(mkv[:, :MEM_WIDTH], MEM_HEADS)
        s["mkn"] = _rms_fwd(s["mk_raw"], p["mem_k_norm"][l], name="mem_k_norm_fwd").reshape(MEM_HEADS, M, HEAD_DIM)
        s["mvv"] = _heads(mkv[:, MEM_WIDTH:], MEM_HEADS).astype(BF16).reshape(MEM_HEADS, M, HEAD_DIM)
        mem_out = _unheads(_mem_fwd(s["mqn"], s["mkn"], s["mvv"], name="mem_attn_fwd"), MEM_HEADS)
        if l < N_A:
            s["u"] = _groups(proj[:, :MAIN_WIDTH])
            s["pw"] = p["pool_w"][l]
            s["ps"] = p["pool_scale"][l].reshape(POOL_GROUPS, 1, POOL_GROUP_DIM)
            main_out = _ungroups(_pool_fwd(s["u"], s["pw"], s["ps"], name="pool_fwd"))
        else:
            j = l - N_A
            s["q_raw"] = _heads(proj[:, :MAIN_WIDTH], SWA_Q_HEADS)
            s["qn"] = _rms_fwd(s["q_raw"], p["q_norm"][j], name="q_norm_fwd").reshape(SWA_KV_HEADS, SWA_GROUP, T, HEAD_DIM)
            main_out = _unheads(_swa_fwd(s["qn"], kn, vv, p["sinks"][j], name="swa_fwd"), SWA_Q_HEADS)
        s["cat"] = jnp.concatenate([main_out, mem_out], axis=-1)
        s["h1"] = _mm(s["cat"], wl["w_out"], b_kind="rows", res=h, name="out_proj")
        s["xn2"] = _rms_fwd(s["h1"], p["norm_mlp"][l], name="norm_mlp_fwd")
        s["r"], s["a"] = _mm(s["xn2"], wl["w_up"], b_kind="layers", relu2=True, name="mlp_up")
        h = _mm(s["a"], wl["w_down"], b_kind="rows", res=s["h1"], name="mlp_down")
        saved.append(s)

    loss, dh, dh_b = _loss(h, target, name="loss_head")

    g = {n: [None] * DEPTH for n in ("norm_mix", "mem_norm", "mem_q_norm", "mem_k_norm", "norm_mlp")}
    g_kv = None
    token = None
    g.update({n: [None] * N_A for n in ("pool_w", "pool_scale", "q_norm", "sinks")})
    dkn = dvv = None
    for l in reversed(range(DEPTH)):
        s = saved[l]
        wl = s["w"]
        gb = {}

        def dw(a, dy, n):
            return _mm(a, dy, ta=True, out_kind="layers" if n == "w_up" else "rows", out_buf=lax.empty(wl[n].shape, BF16),
                       name=n + "_grad")

        norm_mlp_gain = p["norm_mlp"][l] if token is None else p["norm_mlp"][l] + token
        gb["w_down"] = dw(s["a"], dh_b, "w_down")
        du = _mm(dh_b, wl["w_down"], tb=True, b_kind="rows", mul2=s["r"], out_dtype=BF16, name="mlp_down_dx")
        gb["w_up"] = dw(s["xn2"], du, "w_up")
        dxn2 = _mm(du, wl["w_up"], tb=True, b_kind="layers", name="mlp_up_dx")
        dh1, dh1_b, g["norm_mlp"][l] = _rms_bwd(s["h1"], norm_mlp_gain, [dxn2], res=dh, also_bf16=True,
                                                name="norm_mlp_bwd")
        gb["w_out"] = dw(s["cat"], dh1_b, "w_out")
        dcat = _mm(dh1_b, wl["w_out"], tb=True, b_kind="rows", name="out_proj_dx")
        dmem_out = _heads(dcat[:, MAIN_WIDTH:], MEM_HEADS).astype(BF16).reshape(MEM_HEADS, T, HEAD_DIM)
        dmqn, dmkn, dmvv = _mem_bwd(s["mqn"], s["mkn"], s["mvv"], dmem_out, name="mem_attn_bwd")
        dmq_raw, g["mem_q_norm"][l] = _rms_bwd(s["mq_raw"], p["mem_q_norm"][l], [dmqn.reshape(MEM_HEADS * T, HEAD_DIM)],
                                               name="mem_q_norm_bwd")
        dmk_raw, g["mem_k_norm"][l] = _rms_bwd(s["mk_raw"], p["mem_k_norm"][l], [dmkn.reshape(MEM_HEADS * M, HEAD_DIM)],
                                               name="mem_k_norm_bwd")
        dmkv = jnp.concatenate([_unheads(dmk_raw, MEM_HEADS), _unheads(dmvv, MEM_HEADS)], axis=-1).astype(BF16)
        gb["w_mem_kv"] = dw(s["memn"], dmkv, "w_mem_kv")
        dmemn = _mm(dmkv, wl["w_mem_kv"], tb=True, b_kind="rows", name="mem_kv_proj_dx")
        g["mem_norm"][l] = _rms_bwd(mem, p["mem_norm"][l], [dmemn], want_dx=False, name="mem_norm_bwd")
        if l < N_A:
            dmain_out = _groups(dcat[:, :MAIN_WIDTH])
            du_pool, g["pool_w"][l], dps = _pool_bwd(s["u"], s["pw"], s["ps"], dmain_out, name="pool_bwd")
            g["pool_scale"][l] = dps.reshape(MAIN_WIDTH)
            dmain = _ungroups(du_pool)
        else:
            j = l - N_A
            dmain_out = _heads(dcat[:, :MAIN_WIDTH], SWA_Q_HEADS).astype(BF16).reshape(SWA_KV_HEADS, SWA_GROUP, T, HEAD_DIM)
            dqn, dk_l, dv_l, dsink = _swa_bwd(s["qn"], kn, vv, p["sinks"][j], dmain_out, name="swa_bwd")
            g["sinks"][j] = dsink[:, 0, :SWA_GROUP].reshape(SWA_Q_HEADS)
            dq_raw, g["q_norm"][j] = _rms_bwd(s["q_raw"], p["q_norm"][j], [dqn.reshape(SWA_Q_HEADS * T, HEAD_DIM)],
                                              name="q_norm_bwd")
            dmain = _unheads(dq_raw, SWA_Q_HEADS)
            dk_l = dk_l.reshape(SWA_KV_HEADS * T, HEAD_DIM)
            dv_l = dv_l.reshape(SWA_KV_HEADS * T, HEAD_DIM)
            dkn = dk_l if dkn is None else _add(dkn, dk_l, name="dk_sum")
            dvv = dv_l if dvv is None else _add(dvv, dv_l, name="dv_sum")
        dproj = jnp.concatenate([dmain, _unheads(dmq_raw, MEM_HEADS)], axis=-1).astype(BF16)
        gb["w_in"] = dw(s["xn1"], dproj, "w_in")
        dxn1 = _mm(dproj, wl["w_in"], tb=True, b_kind="rows", name="in_proj_dx")
        if l in (0, N_A):
            dh, g["norm_mix"][l] = _rms_bwd(s["h"], p["norm_mix"][l], [dxn1], res=dh1, name="norm_mix_bwd")
        else:
            dh, dh_b, g["norm_mix"][l] = _rms_bwd(s["h"], p["norm_mix"][l], [dxn1], res=dh1, also_bf16=True,
                                                  name="norm_mix_bwd")
        if l == N_A:
            dk_raw, g["k_norm"] = _rms_bwd(k_raw, p["k_norm"], [dkn], name="k_norm_bwd")
            dkv = jnp.concatenate([_unheads(dk_raw, SWA_KV_HEADS), _unheads(dvv, SWA_KV_HEADS)], axis=-1).astype(BF16)
            g_kv = _mm(hn_kv, dkv, ta=True, out_kind="rows", out_buf=lax.empty(w_kv.shape, BF16), name="w_kv_grad")
            dhn = _mm(dkv, w_kv, tb=True, b_kind="rows", name="kv_proj_dx")
            dh, dh_b, g["kv_norm"] = _rms_bwd(h_kv, p["kv_norm"], [dhn], res=dh, also_bf16=True, name="kv_norm_bwd")
        if l + 1 < DEPTH:
            reduce_wait(l + 1, dh)
        token = reduce_layer(l, gb)
    grads = {n: (jnp.stack(v) if isinstance(v, list) else v) for n, v in g.items()}
    return loss, dh, grads, g_kv


def kernel(x, mem, norm_mix, w_in, pool_w, pool_scale, kv_norm, w_kv, k_norm, q_norm, sinks, mem_norm, w_mem_kv, mem_q_norm, mem_k_norm, w_out, norm_mlp, w_up, w_down, loss_target, m_norm_mix, m_w_in, m_pool_w, m_pool_scale, m_kv_norm, m_w_kv, m_k_norm, m_q_norm, m_sinks, m_mem_norm, m_w_mem_kv, m_mem_q_norm, m_mem_k_norm, m_w_out, m_norm_mlp, m_w_up, m_w_down, v_norm_mix, v_w_in, v_pool_w, v_pool_scale, v_kv_norm, v_w_kv, v_k_norm, v_q_norm, v_sinks, v_mem_norm, v_w_mem_kv, v_mem_q_norm, v_mem_k_norm, v_w_out, v_norm_mlp, v_w_up, v_w_down):
    weights = dict(norm_mix=norm_mix, w_in=w_in, pool_w=pool_w, pool_scale=pool_scale, kv_norm=kv_norm, w_kv=w_kv,
                   k_norm=k_norm, q_norm=q_norm, sinks=sinks, mem_norm=mem_norm, w_mem_kv=w_mem_kv,
                   mem_q_norm=mem_q_norm, mem_k_norm=mem_k_norm, w_out=w_out, norm_mlp=norm_mlp, w_up=w_up, w_down=w_down)
    mom1 = dict(norm_mix=m_norm_mix, w_in=m_w_in, pool_w=m_pool_w, pool_scale=m_pool_scale, kv_norm=m_kv_norm, w_kv=m_w_kv,
                k_norm=m_k_norm, q_norm=m_q_norm, sinks=m_sinks, mem_norm=m_mem_norm, w_mem_kv=m_w_mem_kv,
                mem_q_norm=m_mem_q_norm, mem_k_norm=m_mem_k_norm, w_out=m_w_out, norm_mlp=m_norm_mlp, w_up=m_w_up,
                w_down=m_w_down)
    mom2 = dict(norm_mix=v_norm_mix, w_in=v_w_in, pool_w=v_pool_w, pool_scale=v_pool_scale, kv_norm=v_kv_norm, w_kv=v_w_kv,
                k_norm=v_k_norm, q_norm=v_q_norm, sinks=v_sinks, mem_norm=v_mem_norm, w_mem_kv=v_w_mem_kv,
                mem_q_norm=v_mem_q_norm, mem_k_norm=v_mem_k_norm, w_out=v_w_out, norm_mlp=v_norm_mlp, w_up=v_w_up,
                w_down=v_w_down)
    names = list(weights)
    x_pos, y_pos, core = (lax.axis_index(n).astype(jnp.int32) for n in AXES)
    me, my_chip = 4 * x_pos + 2 * y_pos + core, 2 * x_pos + y_pos
    shard = MAIN_WIDTH // N_DEV

    def layer_shards(l):
        return [weights[n][l:l + 1].astype(BF16) for n in LAYERED]

    def usable(arrays):
        wl = dict(zip(LAYERED, arrays))
        wl["w_up"] = wl["w_up"].transpose(1, 2, 0, 3).reshape(1, D_MODEL, D_FF)
        return wl

    scale_block = jnp.pad(pool_scale, ((0, 8 - N_A), (0, 128 - shard)))
    first = _all_gather(layer_shards(0) + [w_kv[None].astype(BF16), scale_block], name="gather_first")
    p = {n: weights[n] for n in SMALL_NAMES}
    p["pool_scale"] = first[-1][:, :N_A, :shard].transpose(1, 0, 2).reshape(N_A, MAIN_WIDTH)
    gathers, reduces, parts = {}, {}, {}

    def fetch(l, after):
        got = first[:len(LAYERED)] if l == 0 else _push_wait(_gather_copies, *gathers.pop(l), after, name=f"gather_wait_{l}")
        token = 0.0
        if l + 1 < DEPTH:
            srcs = layer_shards(l + 1)
            *handles, block = _push_start(_gather_copies, N_DEV - 1, srcs, [_with_own_slot(a, me, N_DEV) for a in srcs],
                                          name=f"gather_start_{l + 1}")
            gathers[l + 1], token = handles, block[0, 0]
        return usable(got), token

    def by_core(gb):
        gb = dict(gb)
        if "w_up" in gb:
            gb["w_up"] = gb["w_up"].reshape(D_MODEL, N_DEV, D_FF // N_DEV).transpose(1, 0, 2)
        order = [n for n in LAYERED if n in gb] + [n for n in gb if n not in LAYERED]
        return {n: gb[n].reshape((N_CHIP, 2) + _view2d(gb[n].shape[1:] if n == "w_up" else gb[n].shape[2:])) for n in order}

    def chip_sums(gb, tag, whole=()):
        views = by_core(gb)
        sib, sib_whole = _sibling_exchange(list(views.values()), list(whole), name="reduce_sibling_" + tag)
        sums = [_pair_sum(a, b, core, name=f"chip_sum_{n}_{tag}", out_dtype=BF16) for (n, a), b in zip(views.items(), sib)]
        return sums, sib_whole

    def reduce_layer(l, gb):
        if l == 0:
            reduces[0] = gb
            return None
        sums, _ = chip_sums(gb, str(l))
        lands = [_with_own_slot(lax.dynamic_index_in_dim(a, my_chip, 0, keepdims=False), my_chip, N_CHIP) for a in sums]
        *handles, block = _push_start(_chip_copies, N_CHIP - 1, sums, lands, name=f"reduce_start_{l}")
        reduces[l] = handles
        return block[0, 0]

    def reduce_wait(l, after):
        parts[l] = _push_wait(_chip_copies, *reduces.pop(l), after, name=f"reduce_wait_{l}")

    loss, grad_x, grads, g_kv = _local_step(x[0], mem[0], loss_target[0], p, first[len(LAYERED)], fetch, reduce_layer, reduce_wait)

    last = dict(reduces.pop(0))
    last["w_kv"] = g_kv
    last["pool_scale"] = grads["pool_scale"].reshape(N_A, N_DEV, shard).transpose(1, 0, 2).astype(BF16)[:, None]
    small = _pack_small(grads)
    sums, (sib_small,) = chip_sums(last, "0", whole=[small])
    chip_small = _pair_sum(small, sib_small, core, name="chip_sum_small", out_dtype=F32)
    got, (parts_small,) = _chip_exchange(sums, [chip_small], name="reduce_chips_0")
    parts[0] = got[:len(LAYERED)]
    parts_kv, parts_scale = got[len(LAYERED):]

    def adamw(n, n_parts):
        res = _adamw(n_parts, *(d[n].reshape(_view2d(d[n].shape)) for d in (weights, mom1, mom2)), name="adamw_" + n)
        return [r.reshape(weights[n].shape) for r in res]

    new = {n: adamw(n, [parts[l][k] for l in range(DEPTH)]) for k, n in enumerate(LAYERED)}
    new["w_kv"] = adamw("w_kv", [parts_kv])
    new["pool_scale"] = adamw("pool_scale", [parts_scale])
    res = _adamw([parts_small], _pack_small(weights), _pack_small(mom1), _pack_small(mom2), name="adamw_replicated")
    for n, vals in zip(SMALL_NAMES, zip(*(_unpack_small(r, weights).values() for r in res))):
        new[n] = list(vals)
    outs = [new[n][k] for k in range(4) for n in names]
    total = lax.psum(loss[0, 0], AXES)
    return (total, grad_x[None], *outs)
```

```python
import math

import jax
import jax.numpy as jnp
from jax import lax
from jax.experimental import pallas as pl
from jax.experimental.pallas import tpu as pltpu

F32 = jnp.float32
BF16 = jnp.bfloat16
MESH = pl.DeviceIdType.MESH
AXES = ("x", "y", "c")

D_MODEL = 1024
DEPTH = 4
N_A = 2
HEAD_DIM = 64
MEM_HEADS = 4
MEM_WIDTH = MEM_HEADS * HEAD_DIM
MAIN_WIDTH = D_MODEL - MEM_WIDTH
POOL_GROUPS = 4
POOL_GROUP_DIM = MAIN_WIDTH // POOL_GROUPS
POOL_HALO = 16
SWA_Q_HEADS = MAIN_WIDTH // HEAD_DIM
SWA_KV_HEADS = 4
SWA_GROUP = SWA_Q_HEADS // SWA_KV_HEADS
KV_HALF = SWA_KV_HEADS * HEAD_DIM
BLOCK = 128
D_FF = 4 * D_MODEL
EPS = 1e-6
SCALE = HEAD_DIM ** -0.5
NEG = float(jnp.finfo(jnp.float32).min)
N_DEV = 8
N_CHIP = 4

ADAM_LR = 0.001
ADAM_B1 = 0.9
ADAM_B2 = 0.999
ADAM_EPS = 1e-08
ADAM_WD = 0.01
ADAM_STEP = 10

PACK_W = 512
VMEM_LIMIT = 52 * 1024 * 1024
MM_TILE = 1024

LAYERED = ("w_in", "w_mem_kv", "w_out", "w_up", "w_down")
SMALL_NAMES = ("norm_mix", "pool_w", "kv_norm", "k_norm", "q_norm", "sinks", "mem_norm", "mem_q_norm", "mem_k_norm",
               "norm_mlp")


ANY = pl.BlockSpec(memory_space=pl.ANY)


def _params(*sem):
    return pltpu.CompilerParams(dimension_semantics=sem, vmem_limit_bytes=VMEM_LIMIT)


def _mm(a, b, *, name, ta=False, tb=False, b_kind=None, layer=0, res=None, relu2=False, mul2=None, out_dtype=F32,
        out_kind=None, out_buf=None):
    if ta:
        K, M = a.shape
    else:
        M, K = a.shape
    if b_kind is None:
        rows_b, cols_b = b.shape
    elif b_kind == "rows":
        rows_b, cols_b = b.shape[0] * b.shape[2], b.shape[3]
    else:
        rows_b, cols_b = b.shape[1:]
    N, K2 = (rows_b, cols_b) if tb else (cols_b, rows_b)
    assert K == K2, (a.shape, b.shape)
    tm = min(M, MM_TILE if K <= MM_TILE else MM_TILE // 2)
    tn = min(N, MM_TILE)
    assert M % tm == 0 and N % tn == 0
    row_tile, col_tile = (tn, K) if tb else (K, tn)
    a_spec = pl.BlockSpec((K, tm), lambda j, i: (0, i)) if ta else pl.BlockSpec((tm, K), lambda j, i: (i, 0))

    def rc(j):
        return (j, 0) if tb else (0, j)

    if b_kind is None:
        b_spec = pl.BlockSpec((row_tile, col_tile), lambda j, i: rc(j))
    elif b_kind == "rows":
        per = row_tile // b.shape[2]
        b_spec = pl.BlockSpec((per, None, b.shape[2], col_tile), lambda j, i: (rc(j)[0], layer, 0, rc(j)[1]))
    else:
        b_spec = pl.BlockSpec((None, row_tile, col_tile), lambda j, i: (layer, *rc(j)))
    o_spec = pl.BlockSpec((tm, tn), lambda j, i: (i, j))
    dn = (((0 if ta else 1,), (1 if tb else 0,)), ((), ()))
    extra = [e for e in (res, mul2) if e is not None]
    n_out = 2 if relu2 else 1
    n_in = 2 + len(extra) + (1 if out_buf is not None else 0)

    def body(*refs):
        a_ref, b_ref = refs[0], refs[1]
        extra_refs = refs[2:2 + len(extra)]
        outs = refs[n_in:n_in + n_out]
        bv = b_ref[...].astype(BF16).reshape(row_tile, col_tile)
        v = lax.dot_general(a_ref[...].astype(BF16), bv, dn, preferred_element_type=F32)
        if res is not None:
            v = extra_refs[0][...] + v
        elif mul2 is not None:
            v = v * (2.0 * extra_refs[0][...].astype(F32))
        if relu2:
            r = jnp.maximum(v, 0.0)
            outs[0][...] = r.astype(BF16)
            outs[1][...] = (r * r).astype(BF16)
        else:
            outs[0][...] = v.astype(outs[0].dtype).reshape(outs[0].shape)

    in_specs = [a_spec, b_spec] + [o_spec] * len(extra)
    operands = [a, b, *extra]
    aliases = {}
    if out_kind is None:
        out_shape = jax.ShapeDtypeStruct((M, N), BF16 if relu2 else out_dtype)
        out_specs = o_spec
    else:
        if out_kind == "rows":
            s = out_buf.shape[2]
            out_specs = pl.BlockSpec((tm // s, None, s, tn), lambda j, i: (i, layer, 0, j))
        else:
            out_specs = pl.BlockSpec((None, tm, tn), lambda j, i: (layer, i, j))
        out_shape = jax.ShapeDtypeStruct(out_buf.shape, out_buf.dtype)
        in_specs.append(ANY)
        operands.append(out_buf)
        aliases = {len(operands) - 1: 0}
    if relu2:
        out_shape, out_specs = (out_shape, out_shape), (out_specs, out_specs)
    return pl.pallas_call(
        body, name=name, grid=(N // tn, M // tm), in_specs=in_specs, out_specs=out_specs, out_shape=out_shape,
        input_output_aliases=aliases, compiler_params=_params("parallel", "parallel"),
    )(*operands)


def _row_tile(rows, d):
    t = min(rows, (512 * 1024) // d)
    while rows % t or (t != rows and t % 16):
        t -= 1
    return t


def _rms_fwd(x, g, *, name, out_dtype=BF16):
    R, D = x.shape
    tr = _row_tile(R, D)

    def body(x_ref, g_ref, o_ref):
        xv = x_ref[...].astype(F32)
        r = lax.rsqrt(jnp.mean(xv * xv, axis=-1, keepdims=True) + EPS)
        o_ref[...] = ((xv * r) * g_ref[...]).astype(o_ref.dtype)

    return pl.pallas_call(
        body, name=name, grid=(R // tr,),
        in_specs=[pl.BlockSpec((tr, D), lambda i: (i, 0)), pl.BlockSpec((1, D), lambda i: (0, 0))],
        out_specs=pl.BlockSpec((tr, D), lambda i: (i, 0)), out_shape=jax.ShapeDtypeStruct((R, D), out_dtype),
        compiler_params=_params("parallel"),
    )(x, g.reshape(1, D))


def _rms_bwd(x, g, dys, *, name, res=None, want_dx=True, also_bf16=False):
    R, D = x.shape
    tr = _row_tile(R, D)
    n_dy = len(dys)
    has_res = res is not None

    def body(*refs):
        x_ref, g_ref = refs[0], refs[1]
        dy_refs = refs[2:2 + n_dy]
        res_ref = refs[2 + n_dy] if has_res else None
        outs = refs[2 + n_dy + (1 if has_res else 0):]
        dg_ref = outs[-1]
        i = pl.program_id(0)
        xv = x_ref[...].astype(F32)
        dy = dy_refs[0][...].astype(F32)
        for extra in dy_refs[1:]:
            dy = dy + extra[...].astype(F32)
        r = lax.rsqrt(jnp.mean(xv * xv, axis=-1, keepdims=True) + EPS)
        xh = xv * r
        part = jnp.sum(dy * xh, axis=0, keepdims=True)

        @pl.when(i == 0)
        def _():
            dg_ref[...] = part

        @pl.when(i > 0)
        def _():
            dg_ref[...] += part

        if want_dx:
            gdy = dy * g_ref[...]
            dx = r * (gdy - xh * jnp.mean(gdy * xh, axis=-1, keepdims=True))
            if has_res:
                dx = res_ref[...] + dx
            outs[0][...] = dx
            if also_bf16:
                outs[1][...] = dx.astype(BF16)

    row = pl.BlockSpec((tr, D), lambda i: (i, 0))
    vec = pl.BlockSpec((1, D), lambda i: (0, 0))
    out_shape = [jax.ShapeDtypeStruct((1, D), F32)]
    out_specs = [vec]
    if also_bf16:
        out_shape = [jax.ShapeDtypeStruct((R, D), BF16)] + out_shape
        out_specs = [row] + out_specs
    if want_dx:
        out_shape = [jax.ShapeDtypeStruct((R, D), F32)] + out_shape
        out_specs = [row] + out_specs
    outs = pl.pallas_call(
        body, name=name, grid=(R // tr,),
        in_specs=[row, vec] + [row] * (n_dy + (1 if has_res else 0)), out_specs=out_specs, out_shape=out_shape,
        compiler_params=_params("arbitrary"),
    )(x, g.reshape(1, D), *dys, *([res] if has_res else []))
    return (*outs[:-1], outs[-1].reshape(D)) if want_dx else outs[0].reshape(D)


def _add(a, b, *, name):
    R, D = a.shape
    tr = _row_tile(R, D)

    def body(a_ref, b_ref, o_ref):
        o_ref[...] = a_ref[...] + b_ref[...]

    row = pl.BlockSpec((tr, D), lambda i: (i, 0))
    return pl.pallas_call(body, name=name, grid=(R // tr,), in_specs=[row, row], out_specs=row,
                          out_shape=jax.ShapeDtypeStruct((R, D), a.dtype), compiler_params=_params("parallel"))(a, b)


POOL_TILE = 512


def _pool_window(group):
    return lax.shift_left(jnp.int32(2), group)


def _pool_diff(u_ref, halo_ref, group, tile):
    first = tile == 0
    halo = jnp.where(first, 0.0, halo_ref[...])
    ext = jnp.concatenate([halo, u_ref[...]], axis=0)
    n = ext.shape[0]
    s1 = ext + pltpu.roll(ext, 1, 0)
    s2 = s1 + pltpu.roll(s1, 2, 0)
    s3 = s2 + pltpu.roll(s2, 4, 0)
    s4 = s3 + pltpu.roll(s3, 8, 0)
    ws = jnp.where(group == 0, s1, jnp.where(group == 1, s2, jnp.where(group == 2, s3, s4)))[POOL_HALO:n]
    t = tile * POOL_TILE + lax.broadcasted_iota(jnp.int32, (POOL_TILE, 1), 0)
    cnt = jnp.minimum(t + 1, _pool_window(group)).astype(F32)
    return ws / cnt - u_ref[...], cnt


def _pool_specs():
    per_tile = POOL_TILE // POOL_HALO
    cur = pl.BlockSpec((None, POOL_TILE, POOL_GROUP_DIM), lambda g, i: (g, i, 0))
    prev = pl.BlockSpec((None, POOL_HALO, POOL_GROUP_DIM), lambda g, i: (g, jnp.maximum(i * per_tile - 1, 0), 0))
    pw = pl.BlockSpec((None, POOL_GROUP_DIM, POOL_GROUP_DIM), lambda g, i: (g, 0, 0))
    vec = pl.BlockSpec((None, 1, POOL_GROUP_DIM), lambda g, i: (g, 0, 0))
    return cur, prev, pw, vec


def _pool_fwd(u, pw, scale, *, name):
    G, T, C = u.shape
    assert T % POOL_TILE == 0
    cur, prev, pw_spec, vec = _pool_specs()

    def body(u_ref, halo_ref, pw_ref, sc_ref, o_ref):
        d, _ = _pool_diff(u_ref, halo_ref, pl.program_id(0), pl.program_id(1))
        mixed = jnp.dot(d.astype(BF16), pw_ref[...].astype(BF16), preferred_element_type=F32)
        o_ref[...] = (mixed * sc_ref[...]).astype(o_ref.dtype)

    return pl.pallas_call(
        body, name=name, grid=(G, T // POOL_TILE), in_specs=[cur, prev, pw_spec, vec], out_specs=cur,
        out_shape=jax.ShapeDtypeStruct((G, T, C), BF16), compiler_params=_params("parallel", "parallel"),
    )(u, u, pw, scale)


def _pool_bwd(u, pw, scale, dout, *, name):
    G, T, C = u.shape
    nt = T // POOL_TILE
    per_tile = POOL_TILE // POOL_HALO
    cur, prev, pw_spec, vec = _pool_specs()
    nxt = pl.BlockSpec((None, POOL_HALO, C), lambda g, i: (g, jnp.minimum((i + 1) * per_tile, nt * per_tile - 1), 0))

    def body(u_ref, halo_ref, pw_ref, sc_ref, do_ref, donext_ref, du_ref, dpw_ref, dsc_ref):
        group, tile = pl.program_id(0), pl.program_id(1)
        d, cnt = _pool_diff(u_ref, halo_ref, group, tile)
        pwb = pw_ref[...].astype(BF16)
        db = d.astype(BF16)
        mixed = jnp.dot(db, pwb, preferred_element_type=F32)
        dout = do_ref[...].astype(F32)
        dsc = jnp.sum(dout * mixed, axis=0, keepdims=True)
        sc = sc_ref[...]
        dmix = (dout * sc).astype(BF16)
        dpw = lax.dot_general(db, dmix, (((0,), (0,)), ((), ())), preferred_element_type=F32)

        @pl.when(tile == 0)
        def _():
            dpw_ref[...] = dpw
            dsc_ref[...] = dsc

        @pl.when(tile > 0)
        def _():
            dpw_ref[...] += dpw
            dsc_ref[...] += dsc

        last = tile == nt - 1
        dnext = jnp.where(last, 0.0, donext_ref[...].astype(F32))
        dmix_ext = jnp.concatenate([dmix, (dnext * sc).astype(BF16)], axis=0)
        dd_ext = lax.dot_general(dmix_ext, pwb, (((1,), (1,)), ((), ())), preferred_element_type=F32)
        window = _pool_window(group).astype(F32)
        cnt_ext = jnp.concatenate([cnt, jnp.broadcast_to(window, (POOL_HALO, 1))], axis=0)
        q = dd_ext / cnt_ext
        n = q.shape[0]
        r1 = q + pltpu.roll(q, n - 1, 0)
        r2 = r1 + pltpu.roll(r1, n - 2, 0)
        r3 = r2 + pltpu.roll(r2, n - 4, 0)
        r4 = r3 + pltpu.roll(r3, n - 8, 0)
        back = jnp.where(group == 0, r1, jnp.where(group == 1, r2, jnp.where(group == 2, r3, r4)))
        du_ref[...] = back[0:POOL_TILE] - dd_ext[0:POOL_TILE]

    return pl.pallas_call(
        body, name=name, grid=(G, nt), in_specs=[cur, prev, pw_spec, vec, cur, nxt],
        out_specs=(cur, pw_spec, vec),
        out_shape=(jax.ShapeDtypeStruct((G, T, C), F32), jax.ShapeDtypeStruct((G, C, C), F32),
                   jax.ShapeDtypeStruct((G, 1, C), F32)),
        compiler_params=_params("arbitrary", "arbitrary"),
    )(u, u, pw, scale, dout, dout)


def _softmax(q, k, bias, valid, sink):
    s = lax.dot_general(q, k, (((1,), (1,)), ((), ())), preferred_element_type=F32) * SCALE
    if bias is not None:
        s = s - bias
    if valid is not None:
        s = jnp.where(valid, s, NEG)
    m = jnp.max(s, axis=-1, keepdims=True)
    if sink is not None:
        m = jnp.maximum(m, sink)
    e = jnp.exp(s - m)
    z = jnp.sum(e, axis=-1, keepdims=True)
    if sink is None:
        return e * (1.0 / z), None
    es = jnp.exp(sink - m)
    inv = 1.0 / (z + es)
    return e * inv, es * inv


def _swa_terms(sink_ref, kvh, blk):
    rows = SWA_GROUP * BLOCK
    row = lax.broadcasted_iota(jnp.int32, (rows, 1), 0)
    grp = row // BLOCK
    head = (kvh * SWA_GROUP + grp + 1).astype(F32)
    slope = jnp.exp(head * (-8.0 * math.log(2.0) / SWA_Q_HEADS))
    qi = lax.broadcasted_iota(jnp.int32, (rows, 2 * BLOCK), 0) % BLOCK
    kj = lax.broadcasted_iota(jnp.int32, (rows, 2 * BLOCK), 1)
    dist = qi + BLOCK - kj
    valid = (dist >= 0) & (dist < BLOCK) & ((blk > 0) | (kj >= BLOCK))
    bias = slope * dist.astype(F32)
    s0, s1, s2 = (sink_ref[kvh * SWA_GROUP + g] for g in range(SWA_GROUP))
    sink = jnp.where(grp == 0, s0, jnp.where(grp == 1, s1, s2))
    return bias, valid, sink, grp


def _swa_fwd(q, k, v, sinks, *, name):
    H, G, T, hd = q.shape
    nb = T // BLOCK
    rows = G * BLOCK

    def body(sink_ref, q_ref, kp_ref, kc_ref, vp_ref, vc_ref, o_ref):
        kvh, blk = pl.program_id(0), pl.program_id(1)
        bias, valid, sink, _ = _swa_terms(sink_ref, kvh, blk)
        kk = jnp.concatenate([kp_ref[...], kc_ref[...]], axis=0)
        vv = jnp.concatenate([vp_ref[...], vc_ref[...]], axis=0)
        p, _ = _softmax(q_ref[...].reshape(rows, hd), kk, bias, valid, sink)
        o = jnp.dot(p.astype(BF16), vv, preferred_element_type=F32)
        o_ref[...] = o.reshape(G, BLOCK, hd).astype(o_ref.dtype)

    qs = pl.BlockSpec((None, G, BLOCK, hd), lambda h, n: (h, 0, n, 0))
    prev = pl.BlockSpec((None, BLOCK, hd), lambda h, n: (h, jnp.maximum(n - 1, 0), 0))
    cur = pl.BlockSpec((None, BLOCK, hd), lambda h, n: (h, n, 0))
    return pl.pallas_call(
        body, name=name, grid=(H, nb),
        in_specs=[pl.BlockSpec(memory_space=pltpu.SMEM), qs, prev, cur, prev, cur], out_specs=qs,
        out_shape=jax.ShapeDtypeStruct((H, G, T, hd), BF16), compiler_params=_params("parallel", "parallel"),
    )(sinks, q, k, k, v, v)


def _swa_bwd(q, k, v, sinks, do, *, name):
    H, G, T, hd = q.shape
    nb = T // BLOCK
    rows = G * BLOCK

    def body(sink_ref, q_ref, do_ref, kp_ref, kc_ref, vp_ref, vc_ref, dq_ref, dk_ref, dv_ref, ds_ref, ck, cv):
        kvh, blk = pl.program_id(0), pl.program_id(1)

        @pl.when(blk == 0)
        def _():
            ck[...] = jnp.zeros_like(ck)
            cv[...] = jnp.zeros_like(cv)
            ds_ref[...] = jnp.zeros_like(ds_ref)

        @pl.when(blk < nb)
        def _():
            bias, valid, sink, grp = _swa_terms(sink_ref, kvh, blk)
            kk = jnp.concatenate([kp_ref[...], kc_ref[...]], axis=0)
            vv = jnp.concatenate([vp_ref[...], vc_ref[...]], axis=0)
            qq = q_ref[...].reshape(rows, hd)
            dout = do_ref[...].reshape(rows, hd)
            p, ps = _softmax(qq, kk, bias, valid, sink)
            dp = lax.dot_general(dout, vv, (((1,), (1,)), ((), ())), preferred_element_type=F32)
            dsum = jnp.sum(p * dp, axis=-1, keepdims=True)
            ds = (p * (dp - dsum)).astype(BF16)
            dq = jnp.dot(ds, kk, preferred_element_type=F32) * SCALE
            dq_ref[...] = dq.reshape(G, BLOCK, hd)
            dk = lax.dot_general(ds, qq, (((0,), (0,)), ((), ())), preferred_element_type=F32) * SCALE
            dv = lax.dot_general(p.astype(BF16), dout, (((0,), (0,)), ((), ())), preferred_element_type=F32)
            dk_ref[...] = ck[...] + dk[0:BLOCK]
            dv_ref[...] = cv[...] + dv[0:BLOCK]
            ck[...] = dk[BLOCK:2 * BLOCK]
            cv[...] = dv[BLOCK:2 * BLOCK]
            dsink = -(ps * dsum)
            lane = lax.broadcasted_iota(jnp.int32, (1, 128), 1)
            acc = jnp.zeros((1, 128), F32)
            for g in range(G):
                acc = acc + jnp.where(lane == g, jnp.sum(jnp.where(grp == g, dsink, 0.0)), 0.0)
            ds_ref[...] += acc

        @pl.when(blk == nb)
        def _():
            dk_ref[...] = ck[...]
            dv_ref[...] = cv[...]

    def at(n):
        return jnp.minimum(n, nb - 1)

    qs = pl.BlockSpec((None, G, BLOCK, hd), lambda h, n: (h, 0, at(n), 0))
    prev = pl.BlockSpec((None, BLOCK, hd), lambda h, n: (h, jnp.maximum(at(n) - 1, 0), 0))
    cur = pl.BlockSpec((None, BLOCK, hd), lambda h, n: (h, at(n), 0))
    late = pl.BlockSpec((None, BLOCK, hd), lambda h, n: (h, jnp.maximum(n - 1, 0), 0))
    dsink_spec = pl.BlockSpec((None, 1, 128), lambda h, n: (h, 0, 0))
    return pl.pallas_call(
        body, name=name, grid=(H, nb + 1),
        in_specs=[pl.BlockSpec(memory_space=pltpu.SMEM), qs, qs, prev, cur, prev, cur],
        out_specs=(qs, late, late, dsink_spec),
        out_shape=(jax.ShapeDtypeStruct((H, G, T, hd), F32), jax.ShapeDtypeStruct((H, T, hd), F32),
                   jax.ShapeDtypeStruct((H, T, hd), F32), jax.ShapeDtypeStruct((H, 1, 128), F32)),
        scratch_shapes=[pltpu.VMEM((BLOCK, hd), F32), pltpu.VMEM((BLOCK, hd), F32)],
        compiler_params=_params("arbitrary", "arbitrary"),
    )(sinks, q, do, k, k, v, v)


MEM_Q_TILE = 512


def _mem_fwd(q, k, v, *, name):
    H, T, hd = q.shape
    M = k.shape[1]
    tq = min(T, MEM_Q_TILE)

    def body(q_ref, k_ref, v_ref, o_ref):
        p, _ = _softmax(q_ref[...], k_ref[...], None, None, None)
        o_ref[...] = jnp.dot(p.astype(BF16), v_ref[...], preferred_element_type=F32).astype(o_ref.dtype)

    qs = pl.BlockSpec((None, tq, hd), lambda h, i: (h, i, 0))
    ks = pl.BlockSpec((None, M, hd), lambda h, i: (h, 0, 0))
    return pl.pallas_call(body, name=name, grid=(H, T // tq), in_specs=[qs, ks, ks], out_specs=qs,
                          out_shape=jax.ShapeDtypeStruct((H, T, hd), BF16),
                          compiler_params=_params("parallel", "parallel"))(q, k, v)


def _mem_bwd(q, k, v, do, *, name):
    H, T, hd = q.shape
    M = k.shape[1]
    tq = min(T, MEM_Q_TILE)

    def body(q_ref, do_ref, k_ref, v_ref, dq_ref, dk_ref, dv_ref):
        i = pl.program_id(1)
        qq, kk, vv, dout = q_ref[...], k_ref[...], v_ref[...], do_ref[...]
        p, _ = _softmax(qq, kk, None, None, None)
        dp = lax.dot_general(dout, vv, (((1,), (1,)), ((), ())), preferred_element_type=F32)
        dsum = jnp.sum(p * dp, axis=-1, keepdims=True)
        ds = (p * (dp - dsum)).astype(BF16)
        dq_ref[...] = jnp.dot(ds, kk, preferred_element_type=F32) * SCALE
        dk = lax.dot_general(ds, qq, (((0,), (0,)), ((), ())), preferred_element_type=F32) * SCALE
        dv = lax.dot_general(p.astype(BF16), dout, (((0,), (0,)), ((), ())), preferred_element_type=F32)

        @pl.when(i == 0)
        def _():
            dk_ref[...] = dk
            dv_ref[...] = dv

        @pl.when(i > 0)
        def _():
            dk_ref[...] += dk
            dv_ref[...] += dv

    qs = pl.BlockSpec((None, tq, hd), lambda h, i: (h, i, 0))
    ks = pl.BlockSpec((None, M, hd), lambda h, i: (h, 0, 0))
    return pl.pallas_call(
        body, name=name, grid=(H, T // tq), in_specs=[qs, qs, ks, ks], out_specs=(qs, ks, ks),
        out_shape=(jax.ShapeDtypeStruct((H, T, hd), F32), jax.ShapeDtypeStruct((H, M, hd), F32),
                   jax.ShapeDtypeStruct((H, M, hd), F32)),
        compiler_params=_params("arbitrary", "arbitrary"),
    )(q, do, k, v)


def _loss(y, target, *, name):
    T, D = y.shape
    tr = _row_tile(T, D)

    def body(y_ref, t_ref, l_ref, dy_ref, dyb_ref):
        i = pl.program_id(0)
        err = y_ref[...] - t_ref[...]
        dy = err / float(D)
        dy_ref[...] = dy
        dyb_ref[...] = dy.astype(BF16)
        part = jnp.full((8, 128), 0.5 * jnp.sum(jnp.mean(err * err, axis=-1)), F32)

        @pl.when(i == 0)
        def _():
            l_ref[...] = part

        @pl.when(i > 0)
        def _():
            l_ref[...] += part

    row = pl.BlockSpec((tr, D), lambda i: (i, 0))
    return pl.pallas_call(
        body, name=name, grid=(T // tr,), in_specs=[row, row],
        out_specs=(pl.BlockSpec((8, 128), lambda i: (0, 0)), row, row),
        out_shape=(jax.ShapeDtypeStruct((8, 128), F32), jax.ShapeDtypeStruct((T, D), F32), jax.ShapeDtypeStruct((T, D), BF16)),
        compiler_params=_params("arbitrary"),
    )(y, target)


def _position():
    return lax.axis_index("x"), lax.axis_index("y"), lax.axis_index("c")


def _all_gather(arrays, *, name):
    n = len(arrays)

    def body(*refs):
        srcs, outs = refs[:n], refs[n:2 * n]
        token, send_sems, recv_sems, local_sems = refs[2 * n:]
        token[...] = jnp.zeros_like(token)
        x, y, c = _position()
        me, sibling = (x, y, c), (x, y, 1 - c)
        chips = [(1 - x, y), (x, 1 - y), (1 - x, 1 - y)]

        def slot(a, px, py, pc):
            return outs[a].at[4 * px + 2 * py + pc]

        def copy(a, k, block, to, src=None):
            return pltpu.make_async_remote_copy(
                src_ref=slot(a, *block) if src is None else src, dst_ref=slot(a, *block),
                send_sem=send_sems.at[a, k], recv_sem=recv_sems.at[a, k], device_id=to, device_id_type=MESH)

        mine = [pltpu.make_async_copy(srcs[a], slot(a, *me), local_sems.at[a]) for a in range(n)]
        for cp in mine:
            cp.start()
        first, passed = [], []
        for a in range(n):
            first.append(copy(a, 0, me, sibling, src=srcs[a]))
            first += [copy(a, 1 + j, me, (*chip, c), src=srcs[a]) for j, chip in enumerate(chips)]
        for cp in first:
            cp.start()
        for a in range(n):
            for j, chip in enumerate(chips):
                copy(a, 1 + j, (*chip, c), me).wait_recv()
                fwd = copy(a, 4 + j, (*chip, c), sibling)
                fwd.start()
                passed.append(fwd)
        for a in range(n):
            copy(a, 0, sibling, me).wait_recv()
            for j, chip in enumerate(chips):
                copy(a, 4 + j, (*chip, 1 - c), me).wait_recv()
        for cp in first + passed:
            cp.wait_send()
        for cp in mine:
            cp.wait()

    return pl.pallas_call(
        body, name=name, in_specs=[ANY] * n, out_specs=[ANY] * n + [pl.BlockSpec(memory_space=pltpu.VMEM)],
        out_shape=[jax.ShapeDtypeStruct((N_DEV,) + a.shape, a.dtype) for a in arrays] + [jax.ShapeDtypeStruct((8, 128), F32)],
        scratch_shapes=[pltpu.SemaphoreType.DMA((n, 7)), pltpu.SemaphoreType.DMA((n, 7)), pltpu.SemaphoreType.DMA((n,))],
    )(*arrays)


def _sibling_exchange(by_core, whole, *, name):
    n1, n = len(by_core), len(by_core) + len(whole)

    def body(*refs):
        srcs, outs = refs[:n], refs[n:2 * n]
        send_sems, recv_sems = refs[2 * n:]
        x, y, c = _position()
        copies = [
            pltpu.make_async_remote_copy(src_ref=srcs[a].at[:, 1 - c] if a < n1 else srcs[a], dst_ref=outs[a],
                                         send_sem=send_sems.at[a], recv_sem=recv_sems.at[a], device_id=(x, y, 1 - c),
                                         device_id_type=MESH)
            for a in range(n)]
        for cp in copies:
            cp.start()
        for cp in copies:
            cp.wait()

    out_shape = [jax.ShapeDtypeStruct(a.shape[:1] + a.shape[2:], a.dtype) for a in by_core]
    out_shape += [jax.ShapeDtypeStruct(a.shape, a.dtype) for a in whole]
    outs = pl.pallas_call(
        body, name=name, in_specs=[ANY] * n, out_specs=[ANY] * n, out_shape=out_shape,
        scratch_shapes=[pltpu.SemaphoreType.DMA((n,)), pltpu.SemaphoreType.DMA((n,))],
    )(*by_core, *whole)
    return outs[:n1], outs[n1:]


def _chip_exchange(per_chip, whole, *, name):
    n1, n = len(per_chip), len(per_chip) + len(whole)

    def body(*refs):
        srcs, outs = refs[:n], refs[n:2 * n]
        send_sems, recv_sems, local_sems = refs[2 * n:]
        x, y, c = _position()
        my_chip = 2 * x + y
        chips = [(1 - x, y), (x, 1 - y), (1 - x, 1 - y)]

        def src(a, chip):
            return srcs[a].at[chip] if a < n1 else srcs[a]

        local = [pltpu.make_async_copy(src(a, my_chip), outs[a].at[my_chip], local_sems.at[a]) for a in range(n)]
        for cp in local:
            cp.start()
        copies = [
            pltpu.make_async_remote_copy(src_ref=src(a, 2 * px + py), dst_ref=outs[a].at[my_chip],
                                         send_sem=send_sems.at[a, j], recv_sem=recv_sems.at[a, j], device_id=(px, py, c),
                                         device_id_type=MESH)
            for a in range(n) for j, (px, py) in enumerate(chips)]
        for cp in copies:
            cp.start()
        for cp in copies:
            cp.wait()
        for cp in local:
            cp.wait()

    out_shape = [jax.ShapeDtypeStruct(a.shape, a.dtype) for a in per_chip]
    out_shape += [jax.ShapeDtypeStruct((N_CHIP,) + a.shape, a.dtype) for a in whole]
    outs = pl.pallas_call(
        body, name=name, in_specs=[ANY] * n, out_specs=[ANY] * n, out_shape=out_shape,
        scratch_shapes=[pltpu.SemaphoreType.DMA((n, 3)), pltpu.SemaphoreType.DMA((n, 3)), pltpu.SemaphoreType.DMA((n,))],
    )(*per_chip, *whole)
    return outs[:n1], outs[n1:]


HBM = pl.BlockSpec(memory_space=pltpu.HBM)
SEM = pl.BlockSpec(memory_space=pltpu.SEMAPHORE)
DATAFLOW = pltpu.SideEffectType.DATAFLOW_SIDE_EFFECTING


def _device(flat):
    return flat // 4, (flat // 2) % 2, flat % 2


def _gather_copies(srcs, lands, send_sems, recv_sems, incoming):
    x, y, c = _position()
    me = 4 * x + 2 * y + c
    pairs = []
    for a in range(len(srcs)):
        for d in range(1, N_DEV):
            to, frm = (me + d) % N_DEV, (me + N_DEV - d) % N_DEV
            k = a * (N_DEV - 1) + d - 1
            sems = dict(send_sem=send_sems.at[k], recv_sem=recv_sems.at[k], device_id_type=MESH)
            out = pltpu.make_async_remote_copy(src_ref=srcs[a], dst_ref=lands[a].at[me], device_id=_device(to), **sems)
            inc = pltpu.make_async_remote_copy(src_ref=srcs[a], dst_ref=lands[a].at[frm], device_id=_device(frm),
                                               **sems) if incoming else None
            pairs.append((out, inc))
    return pairs


def _chip_copies(srcs, lands, send_sems, recv_sems, incoming):
    x, y, c = _position()
    my_chip = 2 * x + y
    pairs = []
    for a in range(len(srcs)):
        for k, (px, py) in enumerate([(1 - x, y), (x, 1 - y), (1 - x, 1 - y)]):
            sem = a * (N_CHIP - 1) + k
            sems = dict(send_sem=send_sems.at[sem], recv_sem=recv_sems.at[sem], device_id=(px, py, c), device_id_type=MESH)
            out = pltpu.make_async_remote_copy(src_ref=srcs[a].at[2 * px + py], dst_ref=lands[a].at[my_chip], **sems)
            inc = pltpu.make_async_remote_copy(src_ref=srcs[a].at[2 * px + py], dst_ref=lands[a].at[2 * px + py],
                                               **sems) if incoming else None
            pairs.append((out, inc))
    return pairs


def _push_start(copies, fan, srcs, lands, *, name):
    n = len(srcs)

    def body(*refs):
        src_refs, land_refs = refs[:n], refs[n:2 * n]
        send_sems, recv_sems = refs[2 * n], refs[2 * n + 1]
        token = refs[-1]
        for out, _ in copies(src_refs, land_refs, send_sems, recv_sems, False):
            out.start()
        token[...] = jnp.zeros_like(token)

    outs = pl.pallas_call(
        body, name=name,
        out_shape=(pltpu.SemaphoreType.DMA((n * fan,)), pltpu.SemaphoreType.DMA((n * fan,)),
                   *(pltpu.HBM(a.shape, a.dtype) for a in srcs), *(pltpu.HBM(a.shape, a.dtype) for a in lands),
                   jax.ShapeDtypeStruct((8, 128), F32)),
        in_specs=[HBM] * (2 * n), out_specs=(SEM, SEM, *([HBM] * (2 * n)), pl.BlockSpec(memory_space=pltpu.VMEM)),
        input_output_aliases={i: 2 + i for i in range(2 * n)},
        compiler_params=pltpu.CompilerParams(has_side_effects=DATAFLOW),
    )(*(pltpu.with_memory_space_constraint(a, pltpu.HBM) for a in (*srcs, *lands)))
    return outs[0], outs[1], list(outs[2:2 + n]), list(outs[2 + n:2 + 2 * n]), outs[-1]


def _push_wait(copies, send_sems, recv_sems, srcs, lands, after, *, name):
    n = len(srcs)

    def body(*refs):
        src_refs, land_refs = refs[:n], refs[n:2 * n]
        for out, inc in copies(src_refs, land_refs, refs[2 * n], refs[2 * n + 1], True):
            out.wait_send()
            inc.wait_recv()
        refs[-1][...] = jnp.zeros_like(refs[-1])

    outs = pl.pallas_call(
        body, name=name,
        out_shape=(*(pltpu.HBM(a.shape, a.dtype) for a in (*srcs, *lands)), jax.ShapeDtypeStruct((8, 128), F32)),
        in_specs=[HBM] * (2 * n) + [SEM, SEM, ANY], out_specs=(*([HBM] * (2 * n)), pl.BlockSpec(memory_space=pltpu.VMEM)),
        input_output_aliases={i: i for i in range(2 * n)},
        compiler_params=pltpu.CompilerParams(has_side_effects=DATAFLOW),
    )(*srcs, *lands, send_sems, recv_sems, after)
    return list(outs[n:2 * n]), outs[-1]


def _with_own_slot(block, index, slots):
    buf = lax.empty((slots,) + block.shape, block.dtype)
    return lax.dynamic_update_slice(buf, block[None], (index,) + (0,) * block.ndim)


def _view2d(shape):
    return math.prod(shape[:-1]), shape[-1]


def _pair_sum(mine, other, core, *, name, out_dtype):
    by_core = mine.ndim == 4
    n, w = other.shape[-2:]
    tr = _row_tile(n, w * 2)
    lead = other.shape[0] if by_core else 1

    def body(core_ref, a_ref, b_ref, o_ref):
        o_ref[...] = (a_ref[...].astype(F32) + b_ref[...].astype(F32)).astype(o_ref.dtype)

    if by_core:
        a_spec = pl.BlockSpec((None, None, tr, w), lambda j, i, core_ref: (j, core_ref[0], i, 0))
        o_spec = pl.BlockSpec((None, tr, w), lambda j, i, core_ref: (j, i, 0))
    else:
        a_spec = o_spec = pl.BlockSpec((tr, w), lambda j, i, core_ref: (i, 0))
    grid_spec = pltpu.PrefetchScalarGridSpec(num_scalar_prefetch=1, grid=(lead, n // tr), in_specs=[a_spec, o_spec],
                                             out_specs=o_spec)
    return pl.pallas_call(body, name=name, grid_spec=grid_spec, out_shape=jax.ShapeDtypeStruct(other.shape, out_dtype),
                          compiler_params=_params("parallel", "parallel"))(core.reshape(1), mine, other)


def _adamw(parts, w, m, v, *, name):
    layers = len(parts)
    n_parts, R, W = parts[0].shape
    tr = _row_tile(R, W * 2)
    per_layer = R // tr

    def update(p_ref, w_ref, m_ref, v_ref, g_out, d_out, m_out, v_out):
        g = p_ref[0].astype(F32)
        for j in range(1, n_parts):
            g = g + p_ref[j].astype(F32)
        m_new = ADAM_B1 * m_ref[...] + (1.0 - ADAM_B1) * g
        v_new = ADAM_B2 * v_ref[...] + (1.0 - ADAM_B2) * (g * g)
        m_hat = m_new / (1.0 - ADAM_B1 ** ADAM_STEP)
        v_hat = v_new / (1.0 - ADAM_B2 ** ADAM_STEP)
        g_out[...] = g
        d_out[...] = -ADAM_LR * (m_hat / (jnp.sqrt(v_hat) + ADAM_EPS) + ADAM_WD * w_ref[...])
        m_out[...] = m_new
        v_out[...] = v_new

    def body(*refs):
        for k in range(layers):
            pl.when(pl.program_id(0) == k)(lambda k=k: update(refs[k], *refs[layers:]))

    def parts_spec(k):
        return pl.BlockSpec((n_parts, tr, W), lambda l, i: (0, jnp.where(l == k, i, 0), 0))

    row = pl.BlockSpec((tr, W), lambda l, i: (l * per_layer + i, 0))
    out = jax.ShapeDtypeStruct((layers * R, W), F32)
    return pl.pallas_call(
        body, name=name, grid=(layers, per_layer), in_specs=[parts_spec(k) for k in range(layers)] + [row, row, row],
        out_specs=(row, row, row, row), out_shape=(out, out, out, out), compiler_params=_params("arbitrary", "arbitrary"),
    )(*parts, w, m, v)


SMALL_ROWS = 608


def _pack_small(p):
    flat = jnp.concatenate([p[n].reshape(-1).astype(F32) for n in SMALL_NAMES])
    return jnp.pad(flat, (0, SMALL_ROWS * PACK_W - flat.shape[0])).reshape(SMALL_ROWS, PACK_W)


def _unpack_small(buf, like):
    out, at = {}, 0
    flat = buf.reshape(-1)
    for n in SMALL_NAMES:
        size = math.prod(like[n].shape)
        out[n] = flat[at:at + size].reshape(like[n].shape)
        at += size
    return out


def _heads(a, nh):
    T = a.shape[0]
    return a.reshape(T, nh, HEAD_DIM).transpose(1, 0, 2).reshape(nh * T, HEAD_DIM)


def _unheads(a, nh):
    a = a.reshape(nh, -1, HEAD_DIM)
    return a.transpose(1, 0, 2).reshape(a.shape[1], nh * HEAD_DIM)


def _groups(a):
    T = a.shape[0]
    return a.reshape(T, POOL_GROUPS, POOL_GROUP_DIM).transpose(1, 0, 2)


def _ungroups(a):
    return a.transpose(1, 0, 2).reshape(a.shape[1], MAIN_WIDTH)


def _local_step(x, mem, target, p, w_kv, fetch, reduce_layer, reduce_wait):
    T = x.shape[0]
    M = mem.shape[0]
    saved = []
    h = x
    kn = vv = k_raw = h_kv = hn_kv = None
    for l in range(DEPTH):
        s = {}
        wl, token = fetch(l, h)
        s["w"] = wl
        if l == N_A:
            h_kv = h
            hn_kv = _rms_fwd(h, p["kv_norm"], name="kv_norm_fwd")
            kv = _mm(hn_kv, w_kv, b_kind="rows", name="kv_proj")
            k_raw = _heads(kv[:, :KV_HALF], SWA_KV_HEADS)
            kn = _rms_fwd(k_raw, p["k_norm"], name="k_norm_fwd").reshape(SWA_KV_HEADS, T, HEAD_DIM)
            vv = _heads(kv[:, KV_HALF:], SWA_KV_HEADS).astype(BF16).reshape(SWA_KV_HEADS, T, HEAD_DIM)
        s["h"] = h
        s["xn1"] = _rms_fwd(h, p["norm_mix"][l] + token, name="norm_mix_fwd")
        proj = _mm(s["xn1"], wl["w_in"], b_kind="rows", name="in_proj")
        s["mq_raw"] = _heads(proj[:, MAIN_WIDTH:], MEM_HEADS)
        s["mqn"] = _rms_fwd(s["mq_raw"], p["mem_q_norm"][l], name="mem_q_norm_fwd").reshape(MEM_HEADS, T, HEAD_DIM)
        s["memn"] = _rms_fwd(mem, p["mem_norm"][l], name="mem_norm_fwd")
        mkv = _mm(s["memn"], wl["w_mem_kv"], b_kind="rows", name="mem_kv_proj")
        s["mk_raw"] = _heads(mkv[:, :MEM_WIDTH], MEM_HEADS)
        s["mkn"] = _rms_fwd(s["mk_raw"], p["mem_k_norm"][l], name="mem_k_norm_fwd").reshape(MEM_HEADS, M, HEAD_DIM)
        s["mvv"] = _heads(mkv[:, MEM_WIDTH:], MEM_HEADS).astype(BF16).reshape(MEM_HEADS, M, HEAD_DIM)
        mem_out = _unheads(_mem_fwd(s["mqn"], s["mkn"], s["mvv"], name="mem_attn_fwd"), MEM_HEADS)
        if l < N_A:
            s["u"] = _groups(proj[:, :MAIN_WIDTH])
            s["pw"] = p["pool_w"][l]
            s["ps"] = p["pool_scale"][l].reshape(POOL_GROUPS, 1, POOL_GROUP_DIM)
            main_out = _ungroups(_pool_fwd(s["u"], s["pw"], s["ps"], name="pool_fwd"))
        else:
            j = l - N_A
            s["q_raw"] = _heads(proj[:, :MAIN_WIDTH], SWA_Q_HEADS)
            s["qn"] = _rms_fwd(s["q_raw"], p["q_norm"][j], name="q_norm_fwd").reshape(SWA_KV_HEADS, SWA_GROUP, T, HEAD_DIM)
            main_out = _unheads(_swa_fwd(s["qn"], kn, vv, p["sinks"][j], name="swa_fwd"), SWA_Q_HEADS)
        s["cat"] = jnp.concatenate([main_out, mem_out], axis=-1)
        s["h1"] = _mm(s["cat"], wl["w_out"], b_kind="rows", res=h, name="out_proj")
        s["xn2"] = _rms_fwd(s["h1"], p["norm_mlp"][l], name="norm_mlp_fwd")
        s["r"], s["a"] = _mm(s["xn2"], wl["w_up"], b_kind="layers", relu2=True, name="mlp_up")
        h = _mm(s["a"], wl["w_down"], b_kind="rows", res=s["h1"], name="mlp_down")
        saved.append(s)

    loss, dh, dh_b = _loss(h, target, name="loss_head")

    g = {n: [None] * DEPTH for n in ("norm_mix", "mem_norm", "mem_q_norm", "mem_k_norm", "norm_mlp")}
    g_kv = None
    token = None
    g.update({n: [None] * N_A for n in ("pool_w", "pool_scale", "q_norm", "sinks")})
    dkn = dvv = None
    for l in reversed(range(DEPTH)):
        s = saved[l]
        wl = s["w"]
        gb = {}

        def dw(a, dy, n):
            return _mm(a, dy, ta=True, out_kind="layers" if n == "w_up" else "rows", out_buf=lax.empty(wl[n].shape, BF16),
                       name=n + "_grad")

        norm_mlp_gain = p["norm_mlp"][l] if token is None else p["norm_mlp"][l] + token
        gb["w_down"] = dw(s["a"], dh_b, "w_down")
        du = _mm(dh_b, wl["w_down"], tb=True, b_kind="rows", mul2=s["r"], out_dtype=BF16, name="mlp_down_dx")
        gb["w_up"] = dw(s["xn2"], du, "w_up")
        dxn2 = _mm(du, wl["w_up"], tb=True, b_kind="layers", name="mlp_up_dx")
        dh1, dh1_b, g["norm_mlp"][l] = _rms_bwd(s["h1"], norm_mlp_gain, [dxn2], res=dh, also_bf16=True,
                                                name="norm_mlp_bwd")
        gb["w_out"] = dw(s["cat"], dh1_b, "w_out")
        dcat = _mm(dh1_b, wl["w_out"], tb=True, b_kind="rows", name="out_proj_dx")
        dmem_out = _heads(dcat[:, MAIN_WIDTH:], MEM_HEADS).astype(BF16).reshape(MEM_HEADS, T, HEAD_DIM)
        dmqn, dmkn, dmvv = _mem_bwd(s["mqn"], s["mkn"], s["mvv"], dmem_out, name="mem_attn_bwd")
        dmq_raw, g["mem_q_norm"][l] = _rms_bwd(s["mq_raw"], p["mem_q_norm"][l], [dmqn.reshape(MEM_HEADS * T, HEAD_DIM)],
                                               name="mem_q_norm_bwd")
        dmk_raw, g["mem_k_norm"][l] = _rms_bwd(s["mk_raw"], p["mem_k_norm"][l], [dmkn.reshape(MEM_HEADS * M, HEAD_DIM)],
                                               name="mem_k_norm_bwd")
        dmkv = jnp.concatenate([_unheads(dmk_raw, MEM_HEADS), _unheads(dmvv, MEM_HEADS)], axis=-1).astype(BF16)
        gb["w_mem_kv"] = dw(s["memn"], dmkv, "w_mem_kv")
        dmemn = _mm(dmkv, wl["w_mem_kv"], tb=True, b_kind="rows", name="mem_kv_proj_dx")
        g["mem_norm"][l] = _rms_bwd(mem, p["mem_norm"][l], [dmemn], want_dx=False, name="mem_norm_bwd")
        if l < N_A:
            dmain_out = _groups(dcat[:, :MAIN_WIDTH])
            du_pool, g["pool_w"][l], dps = _pool_bwd(s["u"], s["pw"], s["ps"], dmain_out, name="pool_bwd")
            g["pool_scale"][l] = dps.reshape(MAIN_WIDTH)
            dmain = _ungroups(du_pool)
        else:
            j = l - N_A
            dmain_out = _heads(dcat[:, :MAIN_WIDTH], SWA_Q_HEADS).astype(BF16).reshape(SWA_KV_HEADS, SWA_GROUP, T, HEAD_DIM)
            dqn, dk_l, dv_l, dsink = _swa_bwd(s["qn"], kn, vv, p["sinks"][j], dmain_out, name="swa_bwd")
            g["sinks"][j] = dsink[:, 0, :SWA_GROUP].reshape(SWA_Q_HEADS)
            dq_raw, g["q_norm"][j] = _rms_bwd(s["q_raw"], p["q_norm"][j], [dqn.reshape(SWA_Q_HEADS * T, HEAD_DIM)],
                                              name="q_norm_bwd")
            dmain = _unheads(dq_raw, SWA_Q_HEADS)
            dk_l = dk_l.reshape(SWA_KV_HEADS * T, HEAD_DIM)
            dv_l = dv_l.reshape(SWA_KV_HEADS * T, HEAD_DIM)
            dkn = dk_l if dkn is None else _add(dkn, dk_l, name="dk_sum")
            dvv = dv_l if dvv is None else _add(dvv, dv_l, name="dv_sum")
        dproj = jnp.concatenate([dmain, _unheads(dmq_raw, MEM_HEADS)], axis=-1).astype(BF16)
        gb["w_in"] = dw(s["xn1"], dproj, "w_in")
        dxn1 = _mm(dproj, wl["w_in"], tb=True, b_kind="rows", name="in_proj_dx")
        if l in (0, N_A):
            dh, g["norm_mix"][l] = _rms_bwd(s["h"], p["norm_mix"][l], [dxn1], res=dh1, name="norm_mix_bwd")
        else:
            dh, dh_b, g["norm_mix"][l] = _rms_bwd(s["h"], p["norm_mix"][l], [dxn1], res=dh1, also_bf16=True,
                                                  name="norm_mix_bwd")
        if l == N_A:
            dk_raw, g["k_norm"] = _rms_bwd(k_raw, p["k_norm"], [dkn], name="k_norm_bwd")
            dkv = jnp.concatenate([_unheads(dk_raw, SWA_KV_HEADS), _unheads(dvv, SWA_KV_HEADS)], axis=-1).astype(BF16)
            g_kv = _mm(hn_kv, dkv, ta=True, out_kind="rows", out_buf=lax.empty(w_kv.shape, BF16), name="w_kv_grad")
            dhn = _mm(dkv, w_kv, tb=True, b_kind="rows", name="kv_proj_dx")
            dh, dh_b, g["kv_norm"] = _rms_bwd(h_kv, p["kv_norm"], [dhn], res=dh, also_bf16=True, name="kv_norm_bwd")
        if l + 1 < DEPTH:
            reduce_wait(l + 1, dh)
        token = reduce_layer(l, gb)
    grads = {n: (jnp.stack(v) if isinstance(v, list) else v) for n, v in g.items()}
    return loss, dh, grads, g_kv


def kernel(x, mem, norm_mix, w_in, pool_w, pool_scale, kv_norm, w_kv, k_norm, q_norm, sinks, mem_norm, w_mem_kv, mem_q_norm, mem_k_norm, w_out, norm_mlp, w_up, w_down, loss_target, m_norm_mix, m_w_in, m_pool_w, m_pool_scale, m_kv_norm, m_w_kv, m_k_norm, m_q_norm, m_sinks, m_mem_norm, m_w_mem_kv, m_mem_q_norm, m_mem_k_norm, m_w_out, m_norm_mlp, m_w_up, m_w_down, v_norm_mix, v_w_in, v_pool_w, v_pool_scale, v_kv_norm, v_w_kv, v_k_norm, v_q_norm, v_sinks, v_mem_norm, v_w_mem_kv, v_mem_q_norm, v_mem_k_norm, v_w_out, v_norm_mlp, v_w_up, v_w_down):
    weights = dict(norm_mix=norm_mix, w_in=w_in, pool_w=pool_w, pool_scale=pool_scale, kv_norm=kv_norm, w_kv=w_kv,
                   k_norm=k_norm, q_norm=q_norm, sinks=sinks, mem_norm=mem_norm, w_mem_kv=w_mem_kv,
                   mem_q_norm=mem_q_norm, mem_k_norm=mem_k_norm, w_out=w_out, norm_mlp=norm_mlp, w_up=w_up, w_down=w_down)
    mom1 = dict(norm_mix=m_norm_mix, w_in=m_w_in, pool_w=m_pool_w, pool_scale=m_pool_scale, kv_norm=m_kv_norm, w_kv=m_w_kv,
                k_norm=m_k_norm, q_norm=m_q_norm, sinks=m_sinks, mem_norm=m_mem_norm, w_mem_kv=m_w_mem_kv,
                mem_q_norm=m_mem_q_norm, mem_k_norm=m_mem_k_norm, w_out=m_w_out, norm_mlp=m_norm_mlp, w_up=m_w_up,
                w_down=m_w_down)
    mom2 = dict(norm_mix=v_norm_mix, w_in=v_w_in, pool_w=v_pool_w, pool_scale=v_pool_scale, kv_norm=v_kv_norm, w_kv=v_w_kv,
                k_norm=v_k_norm, q_norm=v_q_norm, sinks=v_sinks, mem_norm=v_mem_norm, w_mem_kv=v_w_mem_kv,
                mem_q_norm=v_mem_q_norm, mem_k_norm=v_mem_k_norm, w_out=v_w_out, norm_mlp=v_norm_mlp, w_up=v_w_up,
                w_down=v_w_down)
    names = list(weights)
    x_pos, y_pos, core = (lax.axis_index(n).astype(jnp.int32) for n in AXES)
    me, my_chip = 4 * x_pos + 2 * y_pos + core, 2 * x_pos + y_pos
    shard = MAIN_WIDTH // N_DEV

    def layer_shards(l, zero=0.0):
        return [(weights[n][l:l + 1] + zero).astype(BF16) for n in LAYERED]

    def usable(arrays):
        wl = dict(zip(LAYERED, arrays))
        wl["w_up"] = wl["w_up"].transpose(1, 2, 0, 3).reshape(1, D_MODEL, D_FF)
        return wl

    scale_block = jnp.pad(pool_scale, ((0, 8 - N_A), (0, 128 - shard)))
    *first, first_done = _all_gather(layer_shards(0) + [w_kv[None].astype(BF16), scale_block], name="gather_first")
    p = {n: weights[n] for n in SMALL_NAMES}
    p["pool_scale"] = first[-1][:, :N_A, :shard].transpose(1, 0, 2).reshape(N_A, MAIN_WIDTH)
    gathers, reduces, parts = {}, {}, {}

    def fetch(l, after):
        if l == 0:
            got, done = first[:len(LAYERED)], first_done
        else:
            got, done = _push_wait(_gather_copies, *gathers.pop(l), after, name=f"gather_wait_{l}")
        token = 0.0
        if l + 1 < DEPTH:
            srcs = layer_shards(l + 1, done[0, 0])
            *handles, block = _push_start(_gather_copies, N_DEV - 1, srcs, [_with_own_slot(a, me, N_DEV) for a in srcs],
                                          name=f"gather_start_{l + 1}")
            gathers[l + 1], token = handles, block[0, 0]
        return usable(got), token

    def by_core(gb):
        gb = dict(gb)
        if "w_up" in gb:
            gb["w_up"] = gb["w_up"].reshape(D_MODEL, N_DEV, D_FF // N_DEV).transpose(1, 0, 2)
        order = [n for n in LAYERED if n in gb] + [n for n in gb if n not in LAYERED]
        return {n: gb[n].reshape((N_CHIP, 2) + _view2d(gb[n].shape[1:] if n == "w_up" else gb[n].shape[2:])) for n in order}

    def chip_sums(gb, tag, whole=()):
        views = by_core(gb)
        sib, sib_whole = _sibling_exchange(list(views.values()), list(whole), name="reduce_sibling_" + tag)
        sums = [_pair_sum(a, b, core, name=f"chip_sum_{n}_{tag}", out_dtype=BF16) for (n, a), b in zip(views.items(), sib)]
        return sums, sib_whole

    def reduce_layer(l, gb):
        if l == 0:
            reduces[0] = gb
            return None
        sums, _ = chip_sums(gb, str(l))
        lands = [_with_own_slot(lax.dynamic_index_in_dim(a, my_chip, 0, keepdims=False), my_chip, N_CHIP) for a in sums]
        *handles, block = _push_start(_chip_copies, N_CHIP - 1, sums, lands, name=f"reduce_start_{l}")
        reduces[l] = handles
        return block[0, 0]

    def reduce_wait(l, after):
        parts[l], _ = _push_wait(_chip_copies, *reduces.pop(l), after, name=f"reduce_wait_{l}")

    loss, grad_x, grads, g_kv = _local_step(x[0], mem[0], loss_target[0], p, first[len(LAYERED)], fetch, reduce_layer, reduce_wait)

    last = dict(reduces.pop(0))
    last["w_kv"] = g_kv
    last["pool_scale"] = grads["pool_scale"].reshape(N_A, N_DEV, shard).transpose(1, 0, 2).astype(BF16)[:, None]
    small = _pack_small(grads)
    sums, (sib_small,) = chip_sums(last, "0", whole=[small])
    chip_small = _pair_sum(small, sib_small, core, name="chip_sum_small", out_dtype=F32)
    got, (parts_small,) = _chip_exchange(sums, [chip_small], name="reduce_chips_0")
    parts[0] = got[:len(LAYERED)]
    parts_kv, parts_scale = got[len(LAYERED):]

    def adamw(n, n_parts):
        res = _adamw(n_parts, *(d[n].reshape(_view2d(d[n].shape)) for d in (weights, mom1, mom2)), name="adamw_" + n)
        return [r.reshape(weights[n].shape) for r in res]

    new = {n: adamw(n, [parts[l][k] for l in range(DEPTH)]) for k, n in enumerate(LAYERED)}
    new["w_kv"] = adamw("w_kv", [parts_kv])
    new["pool_scale"] = adamw("pool_scale", [parts_scale])
    res = _adamw([parts_small], _pack_small(weights), _pack_small(mom1), _pack_small(mom2), name="adamw_replicated")
    for n, vals in zip(SMALL_NAMES, zip(*(_unpack_small(r, weights).values() for r in res))):
        new[n] = list(vals)
    outs = [new[n][k] for k in range(4) for n in names]
    total = lax.psum(loss[0, 0], AXES)
    return (total, grad_x[None], *outs)
```

```python
import math

import jax
import jax.numpy as jnp
from jax import lax
from jax.experimental import pallas as pl
from jax.experimental.pallas import tpu as pltpu

F32 = jnp.float32
BF16 = jnp.bfloat16
MESH = pl.DeviceIdType.MESH
AXES = ("x", "y", "c")

D_MODEL = 1024
DEPTH = 4
N_A = 2
HEAD_DIM = 64
MEM_HEADS = 4
MEM_WIDTH = MEM_HEADS * HEAD_DIM
MAIN_WIDTH = D_MODEL - MEM_WIDTH
POOL_GROUPS = 4
POOL_GROUP_DIM = MAIN_WIDTH // POOL_GROUPS
POOL_HALO = 16
SWA_Q_HEADS = MAIN_WIDTH // HEAD_DIM
SWA_KV_HEADS = 4
SWA_GROUP = SWA_Q_HEADS // SWA_KV_HEADS
KV_HALF = SWA_KV_HEADS * HEAD_DIM
BLOCK = 128
D_FF = 4 * D_MODEL
EPS = 1e-6
SCALE = HEAD_DIM ** -0.5
NEG = float(jnp.finfo(jnp.float32).min)
N_DEV = 8
N_CHIP = 4

ADAM_LR = 0.001
ADAM_B1 = 0.9
ADAM_B2 = 0.999
ADAM_EPS = 1e-08
ADAM_WD = 0.01
ADAM_STEP = 10

PACK_W = 512
VMEM_LIMIT = 52 * 1024 * 1024
MM_TILE = 1024

LAYERED = ("w_in", "w_mem_kv", "w_out", "w_up", "w_down")
SMALL_NAMES = ("norm_mix", "pool_w", "kv_norm", "k_norm", "q_norm", "sinks", "mem_norm", "mem_q_norm", "mem_k_norm",
               "norm_mlp")


ANY = pl.BlockSpec(memory_space=pl.ANY)


def _params(*sem):
    return pltpu.CompilerParams(dimension_semantics=sem, vmem_limit_bytes=VMEM_LIMIT)


def _mm(a, b, *, name, ta=False, tb=False, b_kind=None, layer=0, res=None, relu2=False, mul2=None, out_dtype=F32,
        out_kind=None, out_buf=None):
    if ta:
        K, M = a.shape
    else:
        M, K = a.shape
    if b_kind is None:
        rows_b, cols_b = b.shape
    elif b_kind == "rows":
        rows_b, cols_b = b.shape[0] * b.shape[2], b.shape[3]
    else:
        rows_b, cols_b = b.shape[1:]
    N, K2 = (rows_b, cols_b) if tb else (cols_b, rows_b)
    assert K == K2, (a.shape, b.shape)
    tm = min(M, MM_TILE if K <= MM_TILE else MM_TILE // 2)
    tn = min(N, MM_TILE)
    assert M % tm == 0 and N % tn == 0
    row_tile, col_tile = (tn, K) if tb else (K, tn)
    a_spec = pl.BlockSpec((K, tm), lambda j, i: (0, i)) if ta else pl.BlockSpec((tm, K), lambda j, i: (i, 0))

    def rc(j):
        return (j, 0) if tb else (0, j)

    if b_kind is None:
        b_spec = pl.BlockSpec((row_tile, col_tile), lambda j, i: rc(j))
    elif b_kind == "rows":
        per = row_tile // b.shape[2]
        b_spec = pl.BlockSpec((per, None, b.shape[2], col_tile), lambda j, i: (rc(j)[0], layer, 0, rc(j)[1]))
    else:
        b_spec = pl.BlockSpec((None, row_tile, col_tile), lambda j, i: (layer, *rc(j)))
    o_spec = pl.BlockSpec((tm, tn), lambda j, i: (i, j))
    dn = (((0 if ta else 1,), (1 if tb else 0,)), ((), ()))
    extra = [e for e in (res, mul2) if e is not None]
    n_out = 2 if relu2 else 1
    n_in = 2 + len(extra) + (1 if out_buf is not None else 0)

    def body(*refs):
        a_ref, b_ref = refs[0], refs[1]
        extra_refs = refs[2:2 + len(extra)]
        outs = refs[n_in:n_in + n_out]
        bv = b_ref[...].astype(BF16).reshape(row_tile, col_tile)
        v = lax.dot_general(a_ref[...].astype(BF16), bv, dn, preferred_element_type=F32)
        if res is not None:
            v = extra_refs[0][...] + v
        elif mul2 is not None:
            v = v * (2.0 * extra_refs[0][...].astype(F32))
        if relu2:
            r = jnp.maximum(v, 0.0)
            outs[0][...] = r.astype(BF16)
            outs[1][...] = (r * r).astype(BF16)
        else:
            outs[0][...] = v.astype(outs[0].dtype).reshape(outs[0].shape)

    in_specs = [a_spec, b_spec] + [o_spec] * len(extra)
    operands = [a, b, *extra]
    aliases = {}
    if out_kind is None:
        out_shape = jax.ShapeDtypeStruct((M, N), BF16 if relu2 else out_dtype)
        out_specs = o_spec
    else:
        if out_kind == "rows":
            s = out_buf.shape[2]
            out_specs = pl.BlockSpec((tm // s, None, s, tn), lambda j, i: (i, layer, 0, j))
        else:
            out_specs = pl.BlockSpec((None, tm, tn), lambda j, i: (layer, i, j))
        out_shape = jax.ShapeDtypeStruct(out_buf.shape, out_buf.dtype)
        in_specs.append(ANY)
        operands.append(out_buf)
        aliases = {len(operands) - 1: 0}
    if relu2:
        out_shape, out_specs = (out_shape, out_shape), (out_specs, out_specs)
    return pl.pallas_call(
        body, name=name, grid=(N // tn, M // tm), in_specs=in_specs, out_specs=out_specs, out_shape=out_shape,
        input_output_aliases=aliases, compiler_params=_params("parallel", "parallel"),
    )(*operands)


def _row_tile(rows, d):
    t = min(rows, (512 * 1024) // d)
    while rows % t or (t != rows and t % 16):
        t -= 1
    return t


def _rms_fwd(x, g, *, name, out_dtype=BF16):
    R, D = x.shape
    tr = _row_tile(R, D)

    def body(x_ref, g_ref, o_ref):
        xv = x_ref[...].astype(F32)
        r = lax.rsqrt(jnp.mean(xv * xv, axis=-1, keepdims=True) + EPS)
        o_ref[...] = ((xv * r) * g_ref[...]).astype(o_ref.dtype)

    return pl.pallas_call(
        body, name=name, grid=(R // tr,),
        in_specs=[pl.BlockSpec((tr, D), lambda i: (i, 0)), pl.BlockSpec((1, D), lambda i: (0, 0))],
        out_specs=pl.BlockSpec((tr, D), lambda i: (i, 0)), out_shape=jax.ShapeDtypeStruct((R, D), out_dtype),
        compiler_params=_params("parallel"),
    )(x, g.reshape(1, D))


def _rms_bwd(x, g, dys, *, name, res=None, want_dx=True, also_bf16=False):
    R, D = x.shape
    tr = _row_tile(R, D)
    n_dy = len(dys)
    has_res = res is not None

    def body(*refs):
        x_ref, g_ref = refs[0], refs[1]
        dy_refs = refs[2:2 + n_dy]
        res_ref = refs[2 + n_dy] if has_res else None
        outs = refs[2 + n_dy + (1 if has_res else 0):]
        dg_ref = outs[-1]
        i = pl.program_id(0)
        xv = x_ref[...].astype(F32)
        dy = dy_refs[0][...].astype(F32)
        for extra in dy_refs[1:]:
            dy = dy + extra[...].astype(F32)
        r = lax.rsqrt(jnp.mean(xv * xv, axis=-1, keepdims=True) + EPS)
        xh = xv * r
        part = jnp.sum(dy * xh, axis=0, keepdims=True)

        @pl.when(i == 0)
        def _():
            dg_ref[...] = part

        @pl.when(i > 0)
        def _():
            dg_ref[...] += part

        if want_dx:
            gdy = dy * g_ref[...]
            dx = r * (gdy - xh * jnp.mean(gdy * xh, axis=-1, keepdims=True))
            if has_res:
                dx = res_ref[...] + dx
            outs[0][...] = dx
            if also_bf16:
                outs[1][...] = dx.astype(BF16)

    row = pl.BlockSpec((tr, D), lambda i: (i, 0))
    vec = pl.BlockSpec((1, D), lambda i: (0, 0))
    out_shape = [jax.ShapeDtypeStruct((1, D), F32)]
    out_specs = [vec]
    if also_bf16:
        out_shape = [jax.ShapeDtypeStruct((R, D), BF16)] + out_shape
        out_specs = [row] + out_specs
    if want_dx:
        out_shape = [jax.ShapeDtypeStruct((R, D), F32)] + out_shape
        out_specs = [row] + out_specs
    outs = pl.pallas_call(
        body, name=name, grid=(R // tr,),
        in_specs=[row, vec] + [row] * (n_dy + (1 if has_res else 0)), out_specs=out_specs, out_shape=out_shape,
        compiler_params=_params("arbitrary"),
    )(x, g.reshape(1, D), *dys, *([res] if has_res else []))
    return (*outs[:-1], outs[-1].reshape(D)) if want_dx else outs[0].reshape(D)


def _add(a, b, *, name):
    R, D = a.shape
    tr = _row_tile(R, D)

    def body(a_ref, b_ref, o_ref):
        o_ref[...] = a_ref[...] + b_ref[...]

    row = pl.BlockSpec((tr, D), lambda i: (i, 0))
    return pl.pallas_call(body, name=name, grid=(R // tr,), in_specs=[row, row], out_specs=row,
                          out_shape=jax.ShapeDtypeStruct((R, D), a.dtype), compiler_params=_params("parallel"))(a, b)


POOL_TILE = 512


def _pool_window(group):
    return lax.shift_left(jnp.int32(2), group)


def _pool_diff(u_ref, halo_ref, group, tile):
    first = tile == 0
    halo = jnp.where(first, 0.0, halo_ref[...])
    ext = jnp.concatenate([halo, u_ref[...]], axis=0)
    n = ext.shape[0]
    s1 = ext + pltpu.roll(ext, 1, 0)
    s2 = s1 + pltpu.roll(s1, 2, 0)
    s3 = s2 + pltpu.roll(s2, 4, 0)
    s4 = s3 + pltpu.roll(s3, 8, 0)
    ws = jnp.where(group == 0, s1, jnp.where(group == 1, s2, jnp.where(group == 2, s3, s4)))[POOL_HALO:n]
    t = tile * POOL_TILE + lax.broadcasted_iota(jnp.int32, (POOL_TILE, 1), 0)
    cnt = jnp.minimum(t + 1, _pool_window(group)).astype(F32)
    return ws / cnt - u_ref[...], cnt


def _pool_specs():
    per_tile = POOL_TILE // POOL_HALO
    cur = pl.BlockSpec((None, POOL_TILE, POOL_GROUP_DIM), lambda g, i: (g, i, 0))
    prev = pl.BlockSpec((None, POOL_HALO, POOL_GROUP_DIM), lambda g, i: (g, jnp.maximum(i * per_tile - 1, 0), 0))
    pw = pl.BlockSpec((None, POOL_GROUP_DIM, POOL_GROUP_DIM), lambda g, i: (g, 0, 0))
    vec = pl.BlockSpec((None, 1, POOL_GROUP_DIM), lambda g, i: (g, 0, 0))
    return cur, prev, pw, vec


def _pool_fwd(u, pw, scale, *, name):
    G, T, C = u.shape
    assert T % POOL_TILE == 0
    cur, prev, pw_spec, vec = _pool_specs()

    def body(u_ref, halo_ref, pw_ref, sc_ref, o_ref):
        d, _ = _pool_diff(u_ref, halo_ref, pl.program_id(0), pl.program_id(1))
        mixed = jnp.dot(d.astype(BF16), pw_ref[...].astype(BF16), preferred_element_type=F32)
        o_ref[...] = (mixed * sc_ref[...]).astype(o_ref.dtype)

    return pl.pallas_call(
        body, name=name, grid=(G, T // POOL_TILE), in_specs=[cur, prev, pw_spec, vec], out_specs=cur,
        out_shape=jax.ShapeDtypeStruct((G, T, C), BF16), compiler_params=_params("parallel", "parallel"),
    )(u, u, pw, scale)


def _pool_bwd(u, pw, scale, dout, *, name):
    G, T, C = u.shape
    nt = T // POOL_TILE
    per_tile = POOL_TILE // POOL_HALO
    cur, prev, pw_spec, vec = _pool_specs()
    nxt = pl.BlockSpec((None, POOL_HALO, C), lambda g, i: (g, jnp.minimum((i + 1) * per_tile, nt * per_tile - 1), 0))

    def body(u_ref, halo_ref, pw_ref, sc_ref, do_ref, donext_ref, du_ref, dpw_ref, dsc_ref):
        group, tile = pl.program_id(0), pl.program_id(1)
        d, cnt = _pool_diff(u_ref, halo_ref, group, tile)
        pwb = pw_ref[...].astype(BF16)
        db = d.astype(BF16)
        mixed = jnp.dot(db, pwb, preferred_element_type=F32)
        dout = do_ref[...].astype(F32)
        dsc = jnp.sum(dout * mixed, axis=0, keepdims=True)
        sc = sc_ref[...]
        dmix = (dout * sc).astype(BF16)
        dpw = lax.dot_general(db, dmix, (((0,), (0,)), ((), ())), preferred_element_type=F32)

        @pl.when(tile == 0)
        def _():
            dpw_ref[...] = dpw
            dsc_ref[...] = dsc

        @pl.when(tile > 0)
        def _():
            dpw_ref[...] += dpw
            dsc_ref[...] += dsc

        last = tile == nt - 1
        dnext = jnp.where(last, 0.0, donext_ref[...].astype(F32))
        dmix_ext = jnp.concatenate([dmix, (dnext * sc).astype(BF16)], axis=0)
        dd_ext = lax.dot_general(dmix_ext, pwb, (((1,), (1,)), ((), ())), preferred_element_type=F32)
        window = _pool_window(group).astype(F32)
        cnt_ext = jnp.concatenate([cnt, jnp.broadcast_to(window, (POOL_HALO, 1))], axis=0)
        q = dd_ext / cnt_ext
        n = q.shape[0]
        r1 = q + pltpu.roll(q, n - 1, 0)
        r2 = r1 + pltpu.roll(r1, n - 2, 0)
        r3 = r2 + pltpu.roll(r2, n - 4, 0)
        r4 = r3 + pltpu.roll(r3, n - 8, 0)
        back = jnp.where(group == 0, r1, jnp.where(group == 1, r2, jnp.where(group == 2, r3, r4)))
        du_ref[...] = back[0:POOL_TILE] - dd_ext[0:POOL_TILE]

    return pl.pallas_call(
        body, name=name, grid=(G, nt), in_specs=[cur, prev, pw_spec, vec, cur, nxt],
        out_specs=(cur, pw_spec, vec),
        out_shape=(jax.ShapeDtypeStruct((G, T, C), F32), jax.ShapeDtypeStruct((G, C, C), F32),
                   jax.ShapeDtypeStruct((G, 1, C), F32)),
        compiler_params=_params("arbitrary", "arbitrary"),
    )(u, u, pw, scale, dout, dout)


def _softmax(q, k, bias, valid, sink):
    s = lax.dot_general(q, k, (((1,), (1,)), ((), ())), preferred_element_type=F32) * SCALE
    if bias is not None:
        s = s - bias
    if valid is not None:
        s = jnp.where(valid, s, NEG)
    m = jnp.max(s, axis=-1, keepdims=True)
    if sink is not None:
        m = jnp.maximum(m, sink)
    e = jnp.exp(s - m)
    z = jnp.sum(e, axis=-1, keepdims=True)
    if sink is None:
        return e * (1.0 / z), None
    es = jnp.exp(sink - m)
    inv = 1.0 / (z + es)
    return e * inv, es * inv


def _swa_terms(sink_ref, kvh, blk):
    rows = SWA_GROUP * BLOCK
    row = lax.broadcasted_iota(jnp.int32, (rows, 1), 0)
    grp = row // BLOCK
    head = (kvh * SWA_GROUP + grp + 1).astype(F32)
    slope = jnp.exp(head * (-8.0 * math.log(2.0) / SWA_Q_HEADS))
    qi = lax.broadcasted_iota(jnp.int32, (rows, 2 * BLOCK), 0) % BLOCK
    kj = lax.broadcasted_iota(jnp.int32, (rows, 2 * BLOCK), 1)
    dist = qi + BLOCK - kj
    valid = (dist >= 0) & (dist < BLOCK) & ((blk > 0) | (kj >= BLOCK))
    bias = slope * dist.astype(F32)
    s0, s1, s2 = (sink_ref[kvh * SWA_GROUP + g] for g in range(SWA_GROUP))
    sink = jnp.where(grp == 0, s0, jnp.where(grp == 1, s1, s2))
    return bias, valid, sink, grp


def _swa_fwd(q, k, v, sinks, *, name):
    H, G, T, hd = q.shape
    nb = T // BLOCK
    rows = G * BLOCK

    def body(sink_ref, q_ref, kp_ref, kc_ref, vp_ref, vc_ref, o_ref):
        kvh, blk = pl.program_id(0), pl.program_id(1)
        bias, valid, sink, _ = _swa_terms(sink_ref, kvh, blk)
        kk = jnp.concatenate([kp_ref[...], kc_ref[...]], axis=0)
        vv = jnp.concatenate([vp_ref[...], vc_ref[...]], axis=0)
        p, _ = _softmax(q_ref[...].reshape(rows, hd), kk, bias, valid, sink)
        o = jnp.dot(p.astype(BF16), vv, preferred_element_type=F32)
        o_ref[...] = o.reshape(G, BLOCK, hd).astype(o_ref.dtype)

    qs = pl.BlockSpec((None, G, BLOCK, hd), lambda h, n: (h, 0, n, 0))
    prev = pl.BlockSpec((None, BLOCK, hd), lambda h, n: (h, jnp.maximum(n - 1, 0), 0))
    cur = pl.BlockSpec((None, BLOCK, hd), lambda h, n: (h, n, 0))
    return pl.pallas_call(
        body, name=name, grid=(H, nb),
        in_specs=[pl.BlockSpec(memory_space=pltpu.SMEM), qs, prev, cur, prev, cur], out_specs=qs,
        out_shape=jax.ShapeDtypeStruct((H, G, T, hd), BF16), compiler_params=_params("parallel", "parallel"),
    )(sinks, q, k, k, v, v)


def _swa_bwd(q, k, v, sinks, do, *, name):
    H, G, T, hd = q.shape
    nb = T // BLOCK
    rows = G * BLOCK

    def body(sink_ref, q_ref, do_ref, kp_ref, kc_ref, vp_ref, vc_ref, dq_ref, dk_ref, dv_ref, ds_ref, ck, cv):
        kvh, blk = pl.program_id(0), pl.program_id(1)

        @pl.when(blk == 0)
        def _():
            ck[...] = jnp.zeros_like(ck)
            cv[...] = jnp.zeros_like(cv)
            ds_ref[...] = jnp.zeros_like(ds_ref)

        @pl.when(blk < nb)
        def _():
            bias, valid, sink, grp = _swa_terms(sink_ref, kvh, blk)
            kk = jnp.concatenate([kp_ref[...], kc_ref[...]], axis=0)
            vv = jnp.concatenate([vp_ref[...], vc_ref[...]], axis=0)
            qq = q_ref[...].reshape(rows, hd)
            dout = do_ref[...].reshape(rows, hd)
            p, ps = _softmax(qq, kk, bias, valid, sink)
            dp = lax.dot_general(dout, vv, (((1,), (1,)), ((), ())), preferred_element_type=F32)
            dsum = jnp.sum(p * dp, axis=-1, keepdims=True)
            ds = (p * (dp - dsum)).astype(BF16)
            dq = jnp.dot(ds, kk, preferred_element_type=F32) * SCALE
            dq_ref[...] = dq.reshape(G, BLOCK, hd)
            dk = lax.dot_general(ds, qq, (((0,), (0,)), ((), ())), preferred_element_type=F32) * SCALE
            dv = lax.dot_general(p.astype(BF16), dout, (((0,), (0,)), ((), ())), preferred_element_type=F32)
            dk_ref[...] = ck[...] + dk[0:BLOCK]
            dv_ref[...] = cv[...] + dv[0:BLOCK]
            ck[...] = dk[BLOCK:2 * BLOCK]
            cv[...] = dv[BLOCK:2 * BLOCK]
            dsink = -(ps * dsum)
            lane = lax.broadcasted_iota(jnp.int32, (1, 128), 1)
            acc = jnp.zeros((1, 128), F32)
            for g in range(G):
                acc = acc + jnp.where(lane == g, jnp.sum(jnp.where(grp == g, dsink, 0.0)), 0.0)
            ds_ref[...] += acc

        @pl.when(blk == nb)
        def _():
            dk_ref[...] = ck[...]
            dv_ref[...] = cv[...]

    def at(n):
        return jnp.minimum(n, nb - 1)

    qs = pl.BlockSpec((None, G, BLOCK, hd), lambda h, n: (h, 0, at(n), 0))
    prev = pl.BlockSpec((None, BLOCK, hd), lambda h, n: (h, jnp.maximum(at(n) - 1, 0), 0))
    cur = pl.BlockSpec((None, BLOCK, hd), lambda h, n: (h, at(n), 0))
    late = pl.BlockSpec((None, BLOCK, hd), lambda h, n: (h, jnp.maximum(n - 1, 0), 0))
    dsink_spec = pl.BlockSpec((None, 1, 128), lambda h, n: (h, 0, 0))
    return pl.pallas_call(
        body, name=name, grid=(H, nb + 1),
        in_specs=[pl.BlockSpec(memory_space=pltpu.SMEM), qs, qs, prev, cur, prev, cur],
        out_specs=(qs, late, late, dsink_spec),
        out_shape=(jax.ShapeDtypeStruct((H, G, T, hd), F32), jax.ShapeDtypeStruct((H, T, hd), F32),
                   jax.ShapeDtypeStruct((H, T, hd), F32), jax.ShapeDtypeStruct((H, 1, 128), F32)),
        scratch_shapes=[pltpu.VMEM((BLOCK, hd), F32), pltpu.VMEM((BLOCK, hd), F32)],
        compiler_params=_params("arbitrary", "arbitrary"),
    )(sinks, q, do, k, k, v, v)


MEM_Q_TILE = 512


def _mem_fwd(q, k, v, *, name):
    H, T, hd = q.shape
    M = k.shape[1]
    tq = min(T, MEM_Q_TILE)

    def body(q_ref, k_ref, v_ref, o_ref):
        p, _ = _softmax(q_ref[...], k_ref[...], None, None, None)
        o_ref[...] = jnp.dot(p.astype(BF16), v_ref[...], preferred_element_type=F32).astype(o_ref.dtype)

    qs = pl.BlockSpec((None, tq, hd), lambda h, i: (h, i, 0))
    ks = pl.BlockSpec((None, M, hd), lambda h, i: (h, 0, 0))
    return pl.pallas_call(body, name=name, grid=(H, T // tq), in_specs=[qs, ks, ks], out_specs=qs,
                          out_shape=jax.ShapeDtypeStruct((H, T, hd), BF16),
                          compiler_params=_params("parallel", "parallel"))(q, k, v)


def _mem_bwd(q, k, v, do, *, name):
    H, T, hd = q.shape
    M = k.shape[1]
    tq = min(T, MEM_Q_TILE)

    def body(q_ref, do_ref, k_ref, v_ref, dq_ref, dk_ref, dv_ref):
        i = pl.program_id(1)
        qq, kk, vv, dout = q_ref[...], k_ref[...], v_ref[...], do_ref[...]
        p, _ = _softmax(qq, kk, None, None, None)
        dp = lax.dot_general(dout, vv, (((1,), (1,)), ((), ())), preferred_element_type=F32)
        dsum = jnp.sum(p * dp, axis=-1, keepdims=True)
        ds = (p * (dp - dsum)).astype(BF16)
        dq_ref[...] = jnp.dot(ds, kk, preferred_element_type=F32) * SCALE
        dk = lax.dot_general(ds, qq, (((0,), (0,)), ((), ())), preferred_element_type=F32) * SCALE
        dv = lax.dot_general(p.astype(BF16), dout, (((0,), (0,)), ((), ())), preferred_element_type=F32)

        @pl.when(i == 0)
        def _():
            dk_ref[...] = dk
            dv_ref[...] = dv

        @pl.when(i > 0)
        def _():
            dk_ref[...] += dk
            dv_ref[...] += dv

    qs = pl.BlockSpec((None, tq, hd), lambda h, i: (h, i, 0))
    ks = pl.BlockSpec((None, M, hd), lambda h, i: (h, 0, 0))
    return pl.pallas_call(
        body, name=name, grid=(H, T // tq), in_specs=[qs, qs, ks, ks], out_specs=(qs, ks, ks),
        out_shape=(jax.ShapeDtypeStruct((H, T, hd), F32), jax.ShapeDtypeStruct((H, M, hd), F32),
                   jax.ShapeDtypeStruct((H, M, hd), F32)),
        compiler_params=_params("arbitrary", "arbitrary"),
    )(q, do, k, v)


LANES = 128


def _seg_mean(v):
    r = lax.broadcasted_iota(jnp.int32, (LANES, LANES), 0) // HEAD_DIM
    c = lax.broadcasted_iota(jnp.int32, (LANES, LANES), 1) // HEAD_DIM
    seg = jnp.where(r == c, 1.0 / HEAD_DIM, 0.0).astype(BF16)
    hi = v.astype(BF16)
    lo = (v - hi.astype(F32)).astype(BF16)
    parts = []
    for g in range(v.shape[1] // LANES):
        sl = slice(g * LANES, (g + 1) * LANES)
        parts.append(jnp.dot(hi[:, sl], seg, preferred_element_type=F32) + jnp.dot(lo[:, sl], seg, preferred_element_type=F32))
    return parts[0] if len(parts) == 1 else jnp.concatenate(parts, axis=1)


def _cols(rows, width, col):
    return pl.BlockSpec((rows, width), lambda i: (i, col))


def _head_gain(g, heads):
    return jnp.tile(g, heads).reshape(1, heads * HEAD_DIM)


def _fold_heads(dg, heads):
    return dg.reshape(heads, HEAD_DIM).sum(axis=0)


def _seg_rms_fwd(x, gain, *, width, col, name):
    R = x.shape[0]
    tr = _row_tile(R, width)

    def body(x_ref, g_ref, o_ref):
        xv = x_ref[...]
        r = lax.rsqrt(_seg_mean(xv * xv) + EPS)
        o_ref[...] = ((xv * r) * g_ref[...]).astype(o_ref.dtype)

    return pl.pallas_call(
        body, name=name, grid=(R // tr,), in_specs=[_cols(tr, width, col), pl.BlockSpec((1, width), lambda i: (0, 0))],
        out_specs=_cols(tr, width, 0), out_shape=jax.ShapeDtypeStruct((R, width), BF16), compiler_params=_params("parallel"),
    )(x, gain)


def _seg_rms_bwd(x, gain, dys, *, width, col, name, out_buf=None, out_col=0):
    R = x.shape[0]
    tr = _row_tile(R, width)
    n_dy = len(dys)

    def body(*refs):
        x_ref, g_ref = refs[0], refs[1]
        dy_refs = refs[2:2 + n_dy]
        dx_ref, dg_ref = refs[-2], refs[-1]
        i = pl.program_id(0)
        xv = x_ref[...]
        dy = dy_refs[0][...]
        for extra in dy_refs[1:]:
            dy = dy + extra[...]
        r = lax.rsqrt(_seg_mean(xv * xv) + EPS)
        xh = xv * r
        part = jnp.sum(dy * xh, axis=0, keepdims=True)

        @pl.when(i == 0)
        def _():
            dg_ref[...] = part

        @pl.when(i > 0)
        def _():
            dg_ref[...] += part

        gdy = dy * g_ref[...]
        dx_ref[...] = (r * (gdy - xh * _seg_mean(gdy * xh))).astype(dx_ref.dtype)

    vec = pl.BlockSpec((1, width), lambda i: (0, 0))
    in_specs = [_cols(tr, width, col), vec] + [_cols(tr, width, 0)] * n_dy
    operands = [x, gain, *dys]
    aliases = {}
    dx_shape = jax.ShapeDtypeStruct((R, width), BF16)
    if out_buf is not None:
        in_specs.append(ANY)
        operands.append(out_buf)
        aliases = {len(operands) - 1: 0}
        dx_shape = jax.ShapeDtypeStruct(out_buf.shape, out_buf.dtype)
    return pl.pallas_call(
        body, name=name, grid=(R // tr,), in_specs=in_specs, out_specs=(_cols(tr, width, out_col), vec),
        out_shape=(dx_shape, jax.ShapeDtypeStruct((1, width), F32)), input_output_aliases=aliases,
        compiler_params=_params("arbitrary"),
    )(*operands)


def _sum_into(a, b, out_buf, out_col, *, name):
    R, width = a.shape
    tr = _row_tile(R, width)

    def body(a_ref, b_ref, _, o_ref):
        o_ref[...] = (a_ref[...] + b_ref[...]).astype(o_ref.dtype)

    return pl.pallas_call(
        body, name=name, grid=(R // tr,), in_specs=[_cols(tr, width, 0), _cols(tr, width, 0), ANY],
        out_specs=_cols(tr, width, out_col), out_shape=jax.ShapeDtypeStruct(out_buf.shape, out_buf.dtype),
        input_output_aliases={2: 0}, compiler_params=_params("parallel"),
    )(a, b, out_buf)


def _pool_lane_group():
    return lax.broadcasted_iota(jnp.int32, (1, MAIN_WIDTH), 1) // POOL_GROUP_DIM


def _pool_pick(group, per_window):
    s1, s2, s3, s4 = per_window
    return jnp.where(group == 0, s1, jnp.where(group == 1, s2, jnp.where(group == 2, s3, s4)))


def _pool_delta(u_ref, halo_ref, tile):
    group = _pool_lane_group()
    halo = jnp.where(tile == 0, 0.0, halo_ref[...])
    ext = jnp.concatenate([halo, u_ref[...]], axis=0)
    n = ext.shape[0]
    s1 = ext + pltpu.roll(ext, 1, 0)
    s2 = s1 + pltpu.roll(s1, 2, 0)
    s3 = s2 + pltpu.roll(s2, 4, 0)
    s4 = s3 + pltpu.roll(s3, 8, 0)
    ws = _pool_pick(group, (s1, s2, s3, s4))[POOL_HALO:n]
    t = tile * POOL_TILE + lax.broadcasted_iota(jnp.int32, (POOL_TILE, 1), 0)
    cnt = jnp.minimum(t + 1, _pool_pick(group, (2, 4, 8, 16))).astype(F32)
    return ws / cnt - u_ref[...], cnt


def _pool_in_specs():
    per_tile = POOL_TILE // POOL_HALO
    cur = _cols(POOL_TILE, MAIN_WIDTH, 0)
    prev = pl.BlockSpec((POOL_HALO, MAIN_WIDTH), lambda i: (jnp.maximum(i * per_tile - 1, 0), 0))
    mix = pl.BlockSpec((MAIN_WIDTH, MAIN_WIDTH), lambda i: (0, 0))
    vec = pl.BlockSpec((1, MAIN_WIDTH), lambda i: (0, 0))
    return cur, prev, mix, vec


def _pool_mix_fwd(proj, mix, scale, cat, *, name):
    T = proj.shape[0]
    assert T % POOL_TILE == 0
    cur, prev, mix_spec, vec = _pool_in_specs()

    def body(u_ref, halo_ref, mix_ref, sc_ref, _, o_ref):
        d, _cnt = _pool_delta(u_ref, halo_ref, pl.program_id(0))
        mixed = jnp.dot(d.astype(BF16), mix_ref[...].astype(BF16), preferred_element_type=F32)
        o_ref[...] = (mixed * sc_ref[...]).astype(o_ref.dtype)

    return pl.pallas_call(
        body, name=name, grid=(T // POOL_TILE,), in_specs=[cur, prev, mix_spec, vec, ANY], out_specs=cur,
        out_shape=jax.ShapeDtypeStruct(cat.shape, cat.dtype), input_output_aliases={4: 0}, compiler_params=_params("parallel"),
    )(proj, proj, mix, scale, cat)


def _pool_mix_bwd(proj, mix, scale, dcat, *, name):
    T = proj.shape[0]
    nt = T // POOL_TILE
    per_tile = POOL_TILE // POOL_HALO
    cur, prev, mix_spec, vec = _pool_in_specs()
    nxt = pl.BlockSpec((POOL_HALO, MAIN_WIDTH), lambda i: (jnp.minimum((i + 1) * per_tile, nt * per_tile - 1), 0))

    def body(u_ref, halo_ref, mix_ref, sc_ref, do_ref, donext_ref, du_ref, dmix_ref, dsc_ref):
        tile = pl.program_id(0)
        group = _pool_lane_group()
        d, cnt = _pool_delta(u_ref, halo_ref, tile)
        mixb = mix_ref[...].astype(BF16)
        db = d.astype(BF16)
        mixed = jnp.dot(db, mixb, preferred_element_type=F32)
        dout = do_ref[...]
        dsc = jnp.sum(dout * mixed, axis=0, keepdims=True)
        sc = sc_ref[...]
        dmixed = (dout * sc).astype(BF16)
        dmix = lax.dot_general(db, dmixed, (((0,), (0,)), ((), ())), preferred_element_type=F32)

        @pl.when(tile == 0)
        def _():
            dmix_ref[...] = dmix
            dsc_ref[...] = dsc

        @pl.when(tile > 0)
        def _():
            dmix_ref[...] += dmix
            dsc_ref[...] += dsc

        dnext = jnp.where(tile == nt - 1, 0.0, donext_ref[...])
        dmixed_ext = jnp.concatenate([dmixed, (dnext * sc).astype(BF16)], axis=0)
        dd_ext = lax.dot_general(dmixed_ext, mixb, (((1,), (1,)), ((), ())), preferred_element_type=F32)
        window = _pool_pick(group, (2.0, 4.0, 8.0, 16.0))
        cnt_ext = jnp.concatenate([cnt, jnp.broadcast_to(window, (POOL_HALO, MAIN_WIDTH))], axis=0)
        q = dd_ext / cnt_ext
        n = q.shape[0]
        r1 = q + pltpu.roll(q, n - 1, 0)
        r2 = r1 + pltpu.roll(r1, n - 2, 0)
        r3 = r2 + pltpu.roll(r2, n - 4, 0)
        r4 = r3 + pltpu.roll(r3, n - 8, 0)
        back = _pool_pick(group, (r1, r2, r3, r4))
        du_ref[...] = (back[0:POOL_TILE] - dd_ext[0:POOL_TILE]).astype(du_ref.dtype)

    return pl.pallas_call(
        body, name=name, grid=(nt,), in_specs=[cur, prev, mix_spec, vec, cur, nxt], out_specs=(cur, mix_spec, vec),
        out_shape=(jax.ShapeDtypeStruct((T, D_MODEL), BF16), jax.ShapeDtypeStruct((MAIN_WIDTH, MAIN_WIDTH), F32),
                   jax.ShapeDtypeStruct((1, MAIN_WIDTH), F32)),
        compiler_params=_params("arbitrary"),
    )(proj, proj, mix, scale, dcat, dcat)


def _head(a, h):
    return a[:, h * HEAD_DIM:(h + 1) * HEAD_DIM]


def _swa_mask(blk):
    rows = SWA_GROUP * BLOCK
    qi = lax.broadcasted_iota(jnp.int32, (rows, 2 * BLOCK), 0) % BLOCK
    kj = lax.broadcasted_iota(jnp.int32, (rows, 2 * BLOCK), 1)
    dist = qi + BLOCK - kj
    valid = (dist >= 0) & (dist < BLOCK) & ((blk > 0) | (kj >= BLOCK))
    return dist.astype(F32), valid


def _swa_head_terms(sink_ref, kvh, dist):
    grp = lax.broadcasted_iota(jnp.int32, (SWA_GROUP * BLOCK, 1), 0) // BLOCK
    slopes = [2.0 ** (-8.0 * (kvh * SWA_GROUP + g + 1) / SWA_Q_HEADS) for g in range(SWA_GROUP)]
    sinks = [sink_ref[kvh * SWA_GROUP + g] for g in range(SWA_GROUP)]
    slope = jnp.where(grp == 0, slopes[0], jnp.where(grp == 1, slopes[1], slopes[2]))
    sink = jnp.where(grp == 0, sinks[0], jnp.where(grp == 1, sinks[1], sinks[2]))
    return slope * dist, sink


def _stack_heads(a, kvh):
    return jnp.concatenate([_head(a, kvh * SWA_GROUP + g) for g in range(SWA_GROUP)], axis=0)


def _swa_specs(nb):
    def at(n):
        return jnp.minimum(n, nb - 1)

    q = pl.BlockSpec((BLOCK, MAIN_WIDTH), lambda n: (at(n), 0))
    k_prev = pl.BlockSpec((BLOCK, KV_HALF), lambda n: (jnp.maximum(at(n) - 1, 0), 0))
    k_cur = pl.BlockSpec((BLOCK, KV_HALF), lambda n: (at(n), 0))
    v_prev = pl.BlockSpec((BLOCK, KV_HALF), lambda n: (jnp.maximum(at(n) - 1, 0), 1))
    v_cur = pl.BlockSpec((BLOCK, KV_HALF), lambda n: (at(n), 1))
    return q, k_prev, k_cur, v_prev, v_cur


def _swa_attn_fwd(qn, kn, kv, sinks, cat, *, name):
    T = qn.shape[0]
    nb = T // BLOCK
    q_spec, k_prev, k_cur, v_prev, v_cur = _swa_specs(nb)

    def body(sink_ref, q_ref, kp_ref, kc_ref, vp_ref, vc_ref, _, o_ref):
        dist, valid = _swa_mask(pl.program_id(0))
        kk = jnp.concatenate([kp_ref[...], kc_ref[...]], axis=0)
        vv = jnp.concatenate([vp_ref[...], vc_ref[...]], axis=0).astype(BF16)
        q = q_ref[...]
        outs = []
        for kvh in range(SWA_KV_HEADS):
            bias, sink = _swa_head_terms(sink_ref, kvh, dist)
            p, _ps = _softmax(_stack_heads(q, kvh), _head(kk, kvh), bias, valid, sink)
            o = jnp.dot(p.astype(BF16), _head(vv, kvh), preferred_element_type=F32)
            outs += [o[g * BLOCK:(g + 1) * BLOCK] for g in range(SWA_GROUP)]
        o_ref[...] = jnp.concatenate(outs, axis=1).astype(o_ref.dtype)

    return pl.pallas_call(
        body, name=name, grid=(nb,),
        in_specs=[pl.BlockSpec(memory_space=pltpu.SMEM), q_spec, k_prev, k_cur, v_prev, v_cur, ANY], out_specs=q_spec,
        out_shape=jax.ShapeDtypeStruct(cat.shape, cat.dtype), input_output_aliases={6: 0}, compiler_params=_params("parallel"),
    )(sinks, qn, kn, kn, kv, kv, cat)


def _swa_attn_bwd(qn, kn, kv, sinks, dcat, dqn, *, name):
    T = qn.shape[0]
    nb = T // BLOCK
    q_spec, k_prev, k_cur, v_prev, v_cur = _swa_specs(nb)
    late = pl.BlockSpec((BLOCK, KV_HALF), lambda n: (jnp.maximum(n - 1, 0), 0))
    tn_dims = (((0,), (0,)), ((), ()))

    def body(sink_ref, q_ref, do_ref, kp_ref, kc_ref, vp_ref, vc_ref, _, dq_ref, dk_ref, dv_ref, ds_ref, ck, cv):
        blk = pl.program_id(0)

        @pl.when(blk == 0)
        def _():
            ck[...] = jnp.zeros_like(ck)
            cv[...] = jnp.zeros_like(cv)
            ds_ref[...] = jnp.zeros_like(ds_ref)

        @pl.when(blk < nb)
        def _():
            dist, valid = _swa_mask(blk)
            kk = jnp.concatenate([kp_ref[...], kc_ref[...]], axis=0)
            vv = jnp.concatenate([vp_ref[...], vc_ref[...]], axis=0).astype(BF16)
            q = q_ref[...]
            dout = do_ref[...].astype(BF16)
            lane = lax.broadcasted_iota(jnp.int32, (1, LANES), 1)
            dsinks = jnp.zeros((1, LANES), F32)
            dqs, dks, dvs = [], [], []
            for kvh in range(SWA_KV_HEADS):
                bias, sink = _swa_head_terms(sink_ref, kvh, dist)
                qq, kh, vh, dd = _stack_heads(q, kvh), _head(kk, kvh), _head(vv, kvh), _stack_heads(dout, kvh)
                p, ps = _softmax(qq, kh, bias, valid, sink)
                dp = lax.dot_general(dd, vh, (((1,), (1,)), ((), ())), preferred_element_type=F32)
                dsum = jnp.sum(p * dp, axis=-1, keepdims=True)
                ds = (p * (dp - dsum)).astype(BF16)
                dq = jnp.dot(ds, kh, preferred_element_type=F32) * SCALE
                dqs += [dq[g * BLOCK:(g + 1) * BLOCK] for g in range(SWA_GROUP)]
                dks.append(lax.dot_general(ds, qq, tn_dims, preferred_element_type=F32) * SCALE)
                dvs.append(lax.dot_general(p.astype(BF16), dd, tn_dims, preferred_element_type=F32))
                dsink = -(ps * dsum)
                for g in range(SWA_GROUP):
                    dsinks = dsinks + jnp.where(lane == kvh * SWA_GROUP + g, jnp.sum(dsink[g * BLOCK:(g + 1) * BLOCK]), 0.0)
            dq_ref[...] = jnp.concatenate(dqs, axis=1)
            dk = jnp.concatenate(dks, axis=1)
            dv = jnp.concatenate(dvs, axis=1)
            dk_ref[...] = ck[...] + dk[0:BLOCK]
            dv_ref[...] = cv[...] + dv[0:BLOCK]
            ck[...] = dk[BLOCK:2 * BLOCK]
            cv[...] = dv[BLOCK:2 * BLOCK]
            ds_ref[...] += dsinks

        @pl.when(blk == nb)
        def _():
            dk_ref[...] = ck[...]
            dv_ref[...] = cv[...]

    return pl.pallas_call(
        body, name=name, grid=(nb + 1,),
        in_specs=[pl.BlockSpec(memory_space=pltpu.SMEM), q_spec, q_spec, k_prev, k_cur, v_prev, v_cur, ANY],
        out_specs=(q_spec, late, late, pl.BlockSpec((1, LANES), lambda n: (0, 0))),
        out_shape=(jax.ShapeDtypeStruct(dqn.shape, dqn.dtype), jax.ShapeDtypeStruct((T, KV_HALF), F32),
                   jax.ShapeDtypeStruct((T, KV_HALF), F32), jax.ShapeDtypeStruct((1, LANES), F32)),
        scratch_shapes=[pltpu.VMEM((BLOCK, KV_HALF), F32), pltpu.VMEM((BLOCK, KV_HALF), F32)],
        input_output_aliases={7: 0}, compiler_params=_params("arbitrary"),
    )(sinks, qn, dcat, kn, kn, kv, kv, dqn)


def _mem_specs(M, tq, q_col):
    q = _cols(tq, MEM_WIDTH, q_col)
    k = pl.BlockSpec((M, MEM_WIDTH), lambda i: (0, 0))
    v = pl.BlockSpec((M, MEM_WIDTH), lambda i: (0, 1))
    return q, k, v


def _mem_attn_fwd(q, q_col, mkn, mkv, *, name):
    T = q.shape[0]
    M = mkn.shape[0]
    tq = min(T, MEM_Q_TILE)
    q_spec, k_spec, v_spec = _mem_specs(M, tq, q_col)

    def body(q_ref, k_ref, v_ref, o_ref):
        qq, kk, vv = q_ref[...], k_ref[...], v_ref[...].astype(BF16)
        outs = []
        for h in range(MEM_HEADS):
            p, _ps = _softmax(_head(qq, h), _head(kk, h), None, None, None)
            outs.append(jnp.dot(p.astype(BF16), _head(vv, h), preferred_element_type=F32))
        o_ref[...] = jnp.concatenate(outs, axis=1).astype(o_ref.dtype)

    return pl.pallas_call(
        body, name=name, grid=(T // tq,), in_specs=[q_spec, k_spec, v_spec], out_specs=_cols(tq, MEM_WIDTH, MAIN_WIDTH // MEM_WIDTH),
        out_shape=jax.ShapeDtypeStruct((T, D_MODEL), BF16), compiler_params=_params("parallel"),
    )(q, mkn, mkv)


def _mem_attn_bwd(q, q_col, mkn, mkv, dcat, *, dq_width, name):
    T = q.shape[0]
    M = mkn.shape[0]
    tq = min(T, MEM_Q_TILE)
    q_spec, k_spec, v_spec = _mem_specs(M, tq, q_col)
    last = MAIN_WIDTH // MEM_WIDTH
    tn_dims = (((0,), (0,)), ((), ()))

    def body(q_ref, do_ref, k_ref, v_ref, dq_ref, dk_ref, dv_ref):
        i = pl.program_id(0)
        qq, kk, vv, dout = q_ref[...], k_ref[...], v_ref[...].astype(BF16), do_ref[...].astype(BF16)
        dqs, dks, dvs = [], [], []
        for h in range(MEM_HEADS):
            qh, kh, vh, dh = _head(qq, h), _head(kk, h), _head(vv, h), _head(dout, h)
            p, _ps = _softmax(qh, kh, None, None, None)
            dp = lax.dot_general(dh, vh, (((1,), (1,)), ((), ())), preferred_element_type=F32)
            dsum = jnp.sum(p * dp, axis=-1, keepdims=True)
            ds = (p * (dp - dsum)).astype(BF16)
            dqs.append(jnp.dot(ds, kh, preferred_element_type=F32) * SCALE)
            dks.append(lax.dot_general(ds, qh, tn_dims, preferred_element_type=F32) * SCALE)
            dvs.append(lax.dot_general(p.astype(BF16), dh, tn_dims, preferred_element_type=F32))
        dq_ref[...] = jnp.concatenate(dqs, axis=1)
        dk = jnp.concatenate(dks, axis=1)
        dv = jnp.concatenate(dvs, axis=1)

        @pl.when(i == 0)
        def _():
            dk_ref[...] = dk
            dv_ref[...] = dv

        @pl.when(i > 0)
        def _():
            dk_ref[...] += dk
            dv_ref[...] += dv

    acc = pl.BlockSpec((M, MEM_WIDTH), lambda i: (0, 0))
    return pl.pallas_call(
        body, name=name, grid=(T // tq,), in_specs=[q_spec, _cols(tq, MEM_WIDTH, last), k_spec, v_spec],
        out_specs=(_cols(tq, MEM_WIDTH, dq_width // MEM_WIDTH - 1), acc, acc),
        out_shape=(jax.ShapeDtypeStruct((T, dq_width), F32), jax.ShapeDtypeStruct((M, MEM_WIDTH), F32),
                   jax.ShapeDtypeStruct((M, MEM_WIDTH), F32)),
        compiler_params=_params("arbitrary"),
    )(q, dcat, mkn, mkv)


def _loss(y, target, *, name):
    T, D = y.shape
    tr = _row_tile(T, D)

    def body(y_ref, t_ref, l_ref, dy_ref, dyb_ref):
        i = pl.program_id(0)
        err = y_ref[...] - t_ref[...]
        dy = err / float(D)
        dy_ref[...] = dy
        dyb_ref[...] = dy.astype(BF16)
        part = jnp.full((8, 128), 0.5 * jnp.sum(jnp.mean(err * err, axis=-1)), F32)

        @pl.when(i == 0)
        def _():
            l_ref[...] = part

        @pl.when(i > 0)
        def _():
            l_ref[...] += part

    row = pl.BlockSpec((tr, D), lambda i: (i, 0))
    return pl.pallas_call(
        body, name=name, grid=(T // tr,), in_specs=[row, row],
        out_specs=(pl.BlockSpec((8, 128), lambda i: (0, 0)), row, row),
        out_shape=(jax.ShapeDtypeStruct((8, 128), F32), jax.ShapeDtypeStruct((T, D), F32), jax.ShapeDtypeStruct((T, D), BF16)),
        compiler_params=_params("arbitrary"),
    )(y, target)


def _position():
    return lax.axis_index("x"), lax.axis_index("y"), lax.axis_index("c")


def _all_gather(arrays, *, name):
    n = len(arrays)

    def body(*refs):
        srcs, outs = refs[:n], refs[n:2 * n]
        token, send_sems, recv_sems, local_sems = refs[2 * n:]
        token[...] = jnp.zeros_like(token)
        x, y, c = _position()
        me, sibling = (x, y, c), (x, y, 1 - c)
        chips = [(1 - x, y), (x, 1 - y), (1 - x, 1 - y)]

        def slot(a, px, py, pc):
            return outs[a].at[4 * px + 2 * py + pc]

        def copy(a, k, block, to, src=None):
            return pltpu.make_async_remote_copy(
                src_ref=slot(a, *block) if src is None else src, dst_ref=slot(a, *block),
                send_sem=send_sems.at[a, k], recv_sem=recv_sems.at[a, k], device_id=to, device_id_type=MESH)

        mine = [pltpu.make_async_copy(srcs[a], slot(a, *me), local_sems.at[a]) for a in range(n)]
        for cp in mine:
            cp.start()
        first, passed = [], []
        for a in range(n):
            first.append(copy(a, 0, me, sibling, src=srcs[a]))
            first += [copy(a, 1 + j, me, (*chip, c), src=srcs[a]) for j, chip in enumerate(chips)]
        for cp in first:
            cp.start()
        for a in range(n):
            for j, chip in enumerate(chips):
                copy(a, 1 + j, (*chip, c), me).wait_recv()
                fwd = copy(a, 4 + j, (*chip, c), sibling)
                fwd.start()
                passed.append(fwd)
        for a in range(n):
            copy(a, 0, sibling, me).wait_recv()
            for j, chip in enumerate(chips):
                copy(a, 4 + j, (*chip, 1 - c), me).wait_recv()
        for cp in first + passed:
            cp.wait_send()
        for cp in mine:
            cp.wait()

    return pl.pallas_call(
        body, name=name, in_specs=[ANY] * n, out_specs=[ANY] * n + [pl.BlockSpec(memory_space=pltpu.VMEM)],
        out_shape=[jax.ShapeDtypeStruct((N_DEV,) + a.shape, a.dtype) for a in arrays] + [jax.ShapeDtypeStruct((8, 128), F32)],
        scratch_shapes=[pltpu.SemaphoreType.DMA((n, 7)), pltpu.SemaphoreType.DMA((n, 7)), pltpu.SemaphoreType.DMA((n,))],
    )(*arrays)


def _sibling_exchange(by_core, whole, *, name):
    n1, n = len(by_core), len(by_core) + len(whole)

    def body(*refs):
        srcs, outs = refs[:n], refs[n:2 * n]
        send_sems, recv_sems = refs[2 * n:]
        x, y, c = _position()
        copies = [
            pltpu.make_async_remote_copy(src_ref=srcs[a].at[:, 1 - c] if a < n1 else srcs[a], dst_ref=outs[a],
                                         send_sem=send_sems.at[a], recv_sem=recv_sems.at[a], device_id=(x, y, 1 - c),
                                         device_id_type=MESH)
            for a in range(n)]
        for cp in copies:
            cp.start()
        for cp in copies:
            cp.wait()

    out_shape = [jax.ShapeDtypeStruct(a.shape[:1] + a.shape[2:], a.dtype) for a in by_core]
    out_shape += [jax.ShapeDtypeStruct(a.shape, a.dtype) for a in whole]
    outs = pl.pallas_call(
        body, name=name, in_specs=[ANY] * n, out_specs=[ANY] * n, out_shape=out_shape,
        scratch_shapes=[pltpu.SemaphoreType.DMA((n,)), pltpu.SemaphoreType.DMA((n,))],
    )(*by_core, *whole)
    return outs[:n1], outs[n1:]


def _chip_exchange(per_chip, whole, *, name):
    n1, n = len(per_chip), len(per_chip) + len(whole)

    def body(*refs):
        srcs, outs = refs[:n], refs[n:2 * n]
        send_sems, recv_sems, local_sems = refs[2 * n:]
        x, y, c = _position()
        my_chip = 2 * x + y
        chips = [(1 - x, y), (x, 1 - y), (1 - x, 1 - y)]

        def src(a, chip):
            return srcs[a].at[chip] if a < n1 else srcs[a]

        local = [pltpu.make_async_copy(src(a, my_chip), outs[a].at[my_chip], local_sems.at[a]) for a in range(n)]
        for cp in local:
            cp.start()
        copies = [
            pltpu.make_async_remote_copy(src_ref=src(a, 2 * px + py), dst_ref=outs[a].at[my_chip],
                                         send_sem=send_sems.at[a, j], recv_sem=recv_sems.at[a, j], device_id=(px, py, c),
                                         device_id_type=MESH)
            for a in range(n) for j, (px, py) in enumerate(chips)]
        for cp in copies:
            cp.start()
        for cp in copies:
            cp.wait()
        for cp in local:
            cp.wait()

    out_shape = [jax.ShapeDtypeStruct(a.shape, a.dtype) for a in per_chip]
    out_shape += [jax.ShapeDtypeStruct((N_CHIP,) + a.shape, a.dtype) for a in whole]
    outs = pl.pallas_call(
        body, name=name, in_specs=[ANY] * n, out_specs=[ANY] * n, out_shape=out_shape,
        scratch_shapes=[pltpu.SemaphoreType.DMA((n, 3)), pltpu.SemaphoreType.DMA((n, 3)), pltpu.SemaphoreType.DMA((n,))],
    )(*per_chip, *whole)
    return outs[:n1], outs[n1:]


HBM = pl.BlockSpec(memory_space=pltpu.HBM)
SEM = pl.BlockSpec(memory_space=pltpu.SEMAPHORE)
DATAFLOW = pltpu.SideEffectType.DATAFLOW_SIDE_EFFECTING


def _device(flat):
    return flat // 4, (flat // 2) % 2, flat % 2


def _gather_copies(srcs, lands, send_sems, recv_sems, incoming):
    x, y, c = _position()
    me = 4 * x + 2 * y + c
    pairs = []
    for a in range(len(srcs)):
        for d in range(1, N_DEV):
            to, frm = (me + d) % N_DEV, (me + N_DEV - d) % N_DEV
            k = a * (N_DEV - 1) + d - 1
            sems = dict(send_sem=send_sems.at[k], recv_sem=recv_sems.at[k], device_id_type=MESH)
            out = pltpu.make_async_remote_copy(src_ref=srcs[a], dst_ref=lands[a].at[me], device_id=_device(to), **sems)
            inc = pltpu.make_async_remote_copy(src_ref=srcs[a], dst_ref=lands[a].at[frm], device_id=_device(frm),
                                               **sems) if incoming else None
            pairs.append((out, inc))
    return pairs


def _chip_copies(srcs, lands, send_sems, recv_sems, incoming):
    x, y, c = _position()
    my_chip = 2 * x + y
    pairs = []
    for a in range(len(srcs)):
        for k, (px, py) in enumerate([(1 - x, y), (x, 1 - y), (1 - x, 1 - y)]):
            sem = a * (N_CHIP - 1) + k
            sems = dict(send_sem=send_sems.at[sem], recv_sem=recv_sems.at[sem], device_id=(px, py, c), device_id_type=MESH)
            out = pltpu.make_async_remote_copy(src_ref=srcs[a].at[2 * px + py], dst_ref=lands[a].at[my_chip], **sems)
            inc = pltpu.make_async_remote_copy(src_ref=srcs[a].at[2 * px + py], dst_ref=lands[a].at[2 * px + py],
                                               **sems) if incoming else None
            pairs.append((out, inc))
    return pairs


def _push_start(copies, fan, srcs, lands, *, name):
    n = len(srcs)

    def body(*refs):
        src_refs, land_refs = refs[:n], refs[n:2 * n]
        send_sems, recv_sems = refs[2 * n], refs[2 * n + 1]
        token = refs[-1]
        for out, _ in copies(src_refs, land_refs, send_sems, recv_sems, False):
            out.start()
        token[...] = jnp.zeros_like(token)

    outs = pl.pallas_call(
        body, name=name,
        out_shape=(pltpu.SemaphoreType.DMA((n * fan,)), pltpu.SemaphoreType.DMA((n * fan,)),
                   *(pltpu.HBM(a.shape, a.dtype) for a in srcs), *(pltpu.HBM(a.shape, a.dtype) for a in lands),
                   jax.ShapeDtypeStruct((8, 128), F32)),
        in_specs=[HBM] * (2 * n), out_specs=(SEM, SEM, *([HBM] * (2 * n)), pl.BlockSpec(memory_space=pltpu.VMEM)),
        input_output_aliases={i: 2 + i for i in range(2 * n)},
        compiler_params=pltpu.CompilerParams(has_side_effects=DATAFLOW),
    )(*(pltpu.with_memory_space_constraint(a, pltpu.HBM) for a in (*srcs, *lands)))
    return outs[0], outs[1], list(outs[2:2 + n]), list(outs[2 + n:2 + 2 * n]), outs[-1]


def _push_wait(copies, send_sems, recv_sems, srcs, lands, after, *, name):
    n = len(srcs)

    def body(*refs):
        src_refs, land_refs = refs[:n], refs[n:2 * n]
        for out, inc in copies(src_refs, land_refs, refs[2 * n], refs[2 * n + 1], True):
            out.wait_send()
            inc.wait_recv()
        refs[-1][...] = jnp.zeros_like(refs[-1])

    outs = pl.pallas_call(
        body, name=name,
        out_shape=(*(pltpu.HBM(a.shape, a.dtype) for a in (*srcs, *lands)), jax.ShapeDtypeStruct((8, 128), F32)),
        in_specs=[HBM] * (2 * n) + [SEM, SEM, ANY], out_specs=(*([HBM] * (2 * n)), pl.BlockSpec(memory_space=pltpu.VMEM)),
        input_output_aliases={i: i for i in range(2 * n)},
        compiler_params=pltpu.CompilerParams(has_side_effects=DATAFLOW),
    )(*srcs, *lands, send_sems, recv_sems, after)
    return list(outs[n:2 * n]), outs[-1]


def _with_own_slot(block, index, slots):
    buf = lax.empty((slots,) + block.shape, block.dtype)
    return lax.dynamic_update_slice(buf, block[None], (index,) + (0,) * block.ndim)


def _view2d(shape):
    return math.prod(shape[:-1]), shape[-1]


def _pair_sum(mine, other, core, *, name, out_dtype):
    by_core = mine.ndim == 4
    n, w = other.shape[-2:]
    tr = _row_tile(n, w * 2)
    lead = other.shape[0] if by_core else 1

    def body(core_ref, a_ref, b_ref, o_ref):
        o_ref[...] = (a_ref[...].astype(F32) + b_ref[...].astype(F32)).astype(o_ref.dtype)

    if by_core:
        a_spec = pl.BlockSpec((None, None, tr, w), lambda j, i, core_ref: (j, core_ref[0], i, 0))
        o_spec = pl.BlockSpec((None, tr, w), lambda j, i, core_ref: (j, i, 0))
    else:
        a_spec = o_spec = pl.BlockSpec((tr, w), lambda j, i, core_ref: (i, 0))
    grid_spec = pltpu.PrefetchScalarGridSpec(num_scalar_prefetch=1, grid=(lead, n // tr), in_specs=[a_spec, o_spec],
                                             out_specs=o_spec)
    return pl.pallas_call(body, name=name, grid_spec=grid_spec, out_shape=jax.ShapeDtypeStruct(other.shape, out_dtype),
                          compiler_params=_params("parallel", "parallel"))(core.reshape(1), mine, other)


def _adamw(parts, w, m, v, *, name):
    layers = len(parts)
    n_parts, R, W = parts[0].shape
    tr = _row_tile(R, W * 2)
    per_layer = R // tr

    def update(p_ref, w_ref, m_ref, v_ref, g_out, d_out, m_out, v_out):
        g = p_ref[0].astype(F32)
        for j in range(1, n_parts):
            g = g + p_ref[j].astype(F32)
        m_new = ADAM_B1 * m_ref[...] + (1.0 - ADAM_B1) * g
        v_new = ADAM_B2 * v_ref[...] + (1.0 - ADAM_B2) * (g * g)
        m_hat = m_new / (1.0 - ADAM_B1 ** ADAM_STEP)
        v_hat = v_new / (1.0 - ADAM_B2 ** ADAM_STEP)
        g_out[...] = g
        d_out[...] = -ADAM_LR * (m_hat / (jnp.sqrt(v_hat) + ADAM_EPS) + ADAM_WD * w_ref[...])
        m_out[...] = m_new
        v_out[...] = v_new

    def body(*refs):
        for k in range(layers):
            pl.when(pl.program_id(0) == k)(lambda k=k: update(refs[k], *refs[layers:]))

    def parts_spec(k):
        return pl.BlockSpec((n_parts, tr, W), lambda l, i: (0, jnp.where(l == k, i, 0), 0))

    row = pl.BlockSpec((tr, W), lambda l, i: (l * per_layer + i, 0))
    out = jax.ShapeDtypeStruct((layers * R, W), F32)
    return pl.pallas_call(
        body, name=name, grid=(layers, per_layer), in_specs=[parts_spec(k) for k in range(layers)] + [row, row, row],
        out_specs=(row, row, row, row), out_shape=(out, out, out, out), compiler_params=_params("arbitrary", "arbitrary"),
    )(*parts, w, m, v)


SMALL_ROWS = 608


def _pack_small(p):
    flat = jnp.concatenate([p[n].reshape(-1).astype(F32) for n in SMALL_NAMES])
    return jnp.pad(flat, (0, SMALL_ROWS * PACK_W - flat.shape[0])).reshape(SMALL_ROWS, PACK_W)


def _unpack_small(buf, like):
    out, at = {}, 0
    flat = buf.reshape(-1)
    for n in SMALL_NAMES:
        size = math.prod(like[n].shape)
        out[n] = flat[at:at + size].reshape(like[n].shape)
        at += size
    return out


def _heads(a, nh):
    T = a.shape[0]
    return a.reshape(T, nh, HEAD_DIM).transpose(1, 0, 2).reshape(nh * T, HEAD_DIM)


def _unheads(a, nh):
    a = a.reshape(nh, -1, HEAD_DIM)
    return a.transpose(1, 0, 2).reshape(a.shape[1], nh * HEAD_DIM)


def _groups(a):
    T = a.shape[0]
    return a.reshape(T, POOL_GROUPS, POOL_GROUP_DIM).transpose(1, 0, 2)


def _ungroups(a):
    return a.transpose(1, 0, 2).reshape(a.shape[1], MAIN_WIDTH)


def _local_step(x, mem, target, p, w_kv, fetch, reduce_layer, reduce_wait):
    T = x.shape[0]
    M = mem.shape[0]
    saved = []
    h = x
    kn = vv = k_raw = h_kv = hn_kv = None
    for l in range(DEPTH):
        s = {}
        wl, token = fetch(l, h)
        s["w"] = wl
        if l == N_A:
            h_kv = h
            hn_kv = _rms_fwd(h, p["kv_norm"], name="kv_norm_fwd")
            kv = _mm(hn_kv, w_kv, b_kind="rows", name="kv_proj")
            k_raw = _heads(kv[:, :KV_HALF], SWA_KV_HEADS)
            kn = _rms_fwd(k_raw, p["k_norm"], name="k_norm_fwd").reshape(SWA_KV_HEADS, T, HEAD_DIM)
            vv = _heads(kv[:, KV_HALF:], SWA_KV_HEADS).astype(BF16).reshape(SWA_KV_HEADS, T, HEAD_DIM)
        s["h"] = h
        s["xn1"] = _rms_fwd(h, p["norm_mix"][l] + token, name="norm_mix_fwd")
        proj = _mm(s["xn1"], wl["w_in"], b_kind="rows", name="in_proj")
        s["mq_raw"] = _heads(proj[:, MAIN_WIDTH:], MEM_HEADS)
        s["mqn"] = _rms_fwd(s["mq_raw"], p["mem_q_norm"][l], name="mem_q_norm_fwd").reshape(MEM_HEADS, T, HEAD_DIM)
        s["memn"] = _rms_fwd(mem, p["mem_norm"][l], name="mem_norm_fwd")
        mkv = _mm(s["memn"], wl["w_mem_kv"], b_kind="rows", name="mem_kv_proj")
        s["mk_raw"] = _heads(mkv[:, :MEM_WIDTH], MEM_HEADS)
        s["mkn"] = _rms_fwd(s["mk_raw"], p["mem_k_norm"][l], name="mem_k_norm_fwd").reshape(MEM_HEADS, M, HEAD_DIM)
        s["mvv"] = _heads(mkv[:, MEM_WIDTH:], MEM_HEADS).astype(BF16).reshape(MEM_HEADS, M, HEAD_DIM)
        mem_out = _unheads(_mem_fwd(s["mqn"], s["mkn"], s["mvv"], name="mem_attn_fwd"), MEM_HEADS)
        if l < N_A:
            s["u"] = _groups(proj[:, :MAIN_WIDTH])
            s["pw"] = p["pool_w"][l]
            s["ps"] = p["pool_scale"][l].reshape(POOL_GROUPS, 1, POOL_GROUP_DIM)
            main_out = _ungroups(_pool_fwd(s["u"], s["pw"], s["ps"], name="pool_fwd"))
        else:
            j = l - N_A
            s["q_raw"] = _heads(proj[:, :MAIN_WIDTH], SWA_Q_HEADS)
            s["qn"] = _rms_fwd(s["q_raw"], p["q_norm"][j], name="q_norm_fwd").reshape(SWA_KV_HEADS, SWA_GROUP, T, HEAD_DIM)
            main_out = _unheads(_swa_fwd(s["qn"], kn, vv, p["sinks"][j], name="swa_fwd"), SWA_Q_HEADS)
        s["cat"] = jnp.concatenate([main_out, mem_out], axis=-1)
        s["h1"] = _mm(s["cat"], wl["w_out"], b_kind="rows", res=h, name="out_proj")
        s["xn2"] = _rms_fwd(s["h1"], p["norm_mlp"][l], name="norm_mlp_fwd")
        s["r"], s["a"] = _mm(s["xn2"], wl["w_up"], b_kind="layers", relu2=True, name="mlp_up")
        h = _mm(s["a"], wl["w_down"], b_kind="rows", res=s["h1"], name="mlp_down")
        saved.append(s)

    loss, dh, dh_b = _loss(h, target, name="loss_head")

    g = {n: [None] * DEPTH for n in ("norm_mix", "mem_norm", "mem_q_norm", "mem_k_norm", "norm_mlp")}
    g_kv = None
    token = None
    g.update({n: [None] * N_A for n in ("pool_w", "pool_scale", "q_norm", "sinks")})
    dkn = dvv = None
    for l in reversed(range(DEPTH)):
        s = saved[l]
        wl = s["w"]
        gb = {}

        def dw(a, dy, n):
            return _mm(a, dy, ta=True, out_kind="layers" if n == "w_up" else "rows", out_buf=lax.empty(wl[n].shape, BF16),
                       name=n + "_grad")

        norm_mlp_gain = p["norm_mlp"][l] if token is None else p["norm_mlp"][l] + token
        gb["w_down"] = dw(s["a"], dh_b, "w_down")
        du = _mm(dh_b, wl["w_down"], tb=True, b_kind="rows", mul2=s["r"], out_dtype=BF16, name="mlp_down_dx")
        gb["w_up"] = dw(s["xn2"], du, "w_up")
        dxn2 = _mm(du, wl["w_up"], tb=True, b_kind="layers", name="mlp_up_dx")
        dh1, dh1_b, g["norm_mlp"][l] = _rms_bwd(s["h1"], norm_mlp_gain, [dxn2], res=dh, also_bf16=True,
                                                name="norm_mlp_bwd")
        gb["w_out"] = dw(s["cat"], dh1_b, "w_out")
        dcat = _mm(dh1_b, wl["w_out"], tb=True, b_kind="rows", name="out_proj_dx")
        dmem_out = _heads(dcat[:, MAIN_WIDTH:], MEM_HEADS).astype(BF16).reshape(MEM_HEADS, T, HEAD_DIM)
        dmqn, dmkn, dmvv = _mem_bwd(s["mqn"], s["mkn"], s["mvv"], dmem_out, name="mem_attn_bwd")
        dmq_raw, g["mem_q_norm"][l] = _rms_bwd(s["mq_raw"], p["mem_q_norm"][l], [dmqn.reshape(MEM_HEADS * T, HEAD_DIM)],
                                               name="mem_q_norm_bwd")
        dmk_raw, g["mem_k_norm"][l] = _rms_bwd(s["mk_raw"], p["mem_k_norm"][l], [dmkn.reshape(MEM_HEADS * M, HEAD_DIM)],
                                               name="mem_k_norm_bwd")
        dmkv = jnp.concatenate([_unheads(dmk_raw, MEM_HEADS), _unheads(dmvv, MEM_HEADS)], axis=-1).astype(BF16)
        gb["w_mem_kv"] = dw(s["memn"], dmkv, "w_mem_kv")
        dmemn = _mm(dmkv, wl["w_mem_kv"], tb=True, b_kind="rows", name="mem_kv_proj_dx")
        g["mem_norm"][l] = _rms_bwd(mem, p["mem_norm"][l], [dmemn], want_dx=False, name="mem_norm_bwd")
        if l < N_A:
            dmain_out = _groups(dcat[:, :MAIN_WIDTH])
            du_pool, g["pool_w"][l], dps = _pool_bwd(s["u"], s["pw"], s["ps"], dmain_out, name="pool_bwd")
            g["pool_scale"][l] = dps.reshape(MAIN_WIDTH)
            dmain = _ungroups(du_pool)
        else:
            j = l - N_A
            dmain_out = _heads(dcat[:, :MAIN_WIDTH], SWA_Q_HEADS).astype(BF16).reshape(SWA_KV_HEADS, SWA_GROUP, T, HEAD_DIM)
            dqn, dk_l, dv_l, dsink = _swa_bwd(s["qn"], kn, vv, p["sinks"][j], dmain_out, name="swa_bwd")
            g["sinks"][j] = dsink[:, 0, :SWA_GROUP].reshape(SWA_Q_HEADS)
            dq_raw, g["q_norm"][j] = _rms_bwd(s["q_raw"], p["q_norm"][j], [dqn.reshape(SWA_Q_HEADS * T, HEAD_DIM)],
                                              name="q_norm_bwd")
            dmain = _unheads(dq_raw, SWA_Q_HEADS)
            dk_l = dk_l.reshape(SWA_KV_HEADS * T, HEAD_DIM)
            dv_l = dv_l.reshape(SWA_KV_HEADS * T, HEAD_DIM)
            dkn = dk_l if dkn is None else _add(dkn, dk_l, name="dk_sum")
            dvv = dv_l if dvv is None else _add(dvv, dv_l, name="dv_sum")
        dproj = jnp.concatenate([dmain, _unheads(dmq_raw, MEM_HEADS)], axis=-1).astype(BF16)
        gb["w_in"] = dw(s["xn1"], dproj, "w_in")
        dxn1 = _mm(dproj, wl["w_in"], tb=True, b_kind="rows", name="in_proj_dx")
        if l in (0, N_A):
            dh, g["norm_mix"][l] = _rms_bwd(s["h"], p["norm_mix"][l], [dxn1], res=dh1, name="norm_mix_bwd")
        else:
            dh, dh_b, g["norm_mix"][l] = _rms_bwd(s["h"], p["norm_mix"][l], [dxn1], res=dh1, also_bf16=True,
                                                  name="norm_mix_bwd")
        if l == N_A:
            dk_raw, g["k_norm"] = _rms_bwd(k_raw, p["k_norm"], [dkn], name="k_norm_bwd")
            dkv = jnp.concatenate([_unheads(dk_raw, SWA_KV_HEADS), _unheads(dvv, SWA_KV_HEADS)], axis=-1).astype(BF16)
            g_kv = _mm(hn_kv, dkv, ta=True, out_kind="rows", out_buf=lax.empty(w_kv.shape, BF16), name="w_kv_grad")
            dhn = _mm(dkv, w_kv, tb=True, b_kind="rows", name="kv_proj_dx")
            dh, dh_b, g["kv_norm"] = _rms_bwd(h_kv, p["kv_norm"], [dhn], res=dh, also_bf16=True, name="kv_norm_bwd")
        if l + 1 < DEPTH:
            reduce_wait(l + 1, dh)
        token = reduce_layer(l, gb)
    grads = {n: (jnp.stack(v) if isinstance(v, list) else v) for n, v in g.items()}
    return loss, dh, grads, g_kv


def _block_diag(pw):
    out = jnp.zeros((MAIN_WIDTH, MAIN_WIDTH), pw.dtype)
    for g in range(POOL_GROUPS):
        out = lax.dynamic_update_slice(out, pw[g], (g * POOL_GROUP_DIM, g * POOL_GROUP_DIM))
    return out


def _diag_blocks(m):
    return jnp.stack([m[g * POOL_GROUP_DIM:(g + 1) * POOL_GROUP_DIM, g * POOL_GROUP_DIM:(g + 1) * POOL_GROUP_DIM]
                      for g in range(POOL_GROUPS)])


def _train_pass(x, mem, target, p, w_kv, fetch, reduce_layer, reduce_wait):
    T = x.shape[0]
    mem_cols = MAIN_WIDTH // MEM_WIDTH
    k_gain = _head_gain(p["k_norm"], SWA_KV_HEADS)
    saved = []
    h = x
    kn = kv = h_kv = hn_kv = None
    for l in range(DEPTH):
        s = {}
        wl, token = fetch(l, h)
        s["w"] = wl
        if l == N_A:
            h_kv = h
            hn_kv = _rms_fwd(h, p["kv_norm"], name="kv_norm_fwd")
            kv = _mm(hn_kv, w_kv, b_kind="rows", name="kv_proj")
            kn = _seg_rms_fwd(kv, k_gain, width=KV_HALF, col=0, name="k_norm_fwd")
        s["h"] = h
        s["xn1"] = _rms_fwd(h, p["norm_mix"][l] + token, name="norm_mix_fwd")
        s["proj"] = proj = _mm(s["xn1"], wl["w_in"], b_kind="rows", name="in_proj")
        s["memn"] = _rms_fwd(mem, p["mem_norm"][l], name="mem_norm_fwd")
        s["mkv"] = _mm(s["memn"], wl["w_mem_kv"], b_kind="rows", name="mem_kv_proj")
        s["mk_gain"] = _head_gain(p["mem_k_norm"][l], MEM_HEADS)
        s["mkn"] = _seg_rms_fwd(s["mkv"], s["mk_gain"], width=MEM_WIDTH, col=0, name="mem_k_norm_fwd")
        if l < N_A:
            s["q_gain"] = _head_gain(p["mem_q_norm"][l], MEM_HEADS)
            s["qn"] = _seg_rms_fwd(proj, s["q_gain"], width=MEM_WIDTH, col=mem_cols, name="mem_q_norm_fwd")
            s["q_col"] = 0
        else:
            j = l - N_A
            s["q_gain"] = jnp.concatenate([_head_gain(p["q_norm"][j], SWA_Q_HEADS), _head_gain(p["mem_q_norm"][l], MEM_HEADS)],
                                          axis=1)
            s["qn"] = _seg_rms_fwd(proj, s["q_gain"], width=D_MODEL, col=0, name="q_norm_fwd")
            s["q_col"] = mem_cols
        cat = _mem_attn_fwd(s["qn"], s["q_col"], s["mkn"], s["mkv"], name="mem_attn_fwd")
        if l < N_A:
            s["mix"] = _block_diag(p["pool_w"][l])
            s["scale"] = p["pool_scale"][l].reshape(1, MAIN_WIDTH)
            s["cat"] = _pool_mix_fwd(proj, s["mix"], s["scale"], cat, name="pool_fwd")
        else:
            s["cat"] = _swa_attn_fwd(s["qn"], kn, kv, p["sinks"][l - N_A], cat, name="swa_fwd")
        s["h1"] = _mm(s["cat"], wl["w_out"], b_kind="rows", res=h, name="out_proj")
        s["xn2"] = _rms_fwd(s["h1"], p["norm_mlp"][l], name="norm_mlp_fwd")
        s["r"], s["a"] = _mm(s["xn2"], wl["w_up"], b_kind="layers", relu2=True, name="mlp_up")
        h = _mm(s["a"], wl["w_down"], b_kind="rows", res=s["h1"], name="mlp_down")
        saved.append(s)

    loss, dh, dh_b = _loss(h, target, name="loss_head")

    g = {n: [None] * DEPTH for n in ("norm_mix", "mem_norm", "mem_q_norm", "mem_k_norm", "norm_mlp")}
    g.update({n: [None] * N_A for n in ("pool_w", "pool_scale", "q_norm", "sinks")})
    g_kv = None
    token = None
    dks, dvs = [], []
    for l in reversed(range(DEPTH)):
        s = saved[l]
        wl = s["w"]
        gb = {}

        def dw(a, dy, n):
            return _mm(a, dy, ta=True, out_kind="layers" if n == "w_up" else "rows", out_buf=lax.empty(wl[n].shape, BF16),
                       name=n + "_grad")

        norm_mlp_gain = p["norm_mlp"][l] if token is None else p["norm_mlp"][l] + token
        gb["w_down"] = dw(s["a"], dh_b, "w_down")
        du = _mm(dh_b, wl["w_down"], tb=True, b_kind="rows", mul2=s["r"], out_dtype=BF16, name="mlp_down_dx")
        gb["w_up"] = dw(s["xn2"], du, "w_up")
        dxn2 = _mm(du, wl["w_up"], tb=True, b_kind="layers", name="mlp_up_dx")
        dh1, dh1_b, g["norm_mlp"][l] = _rms_bwd(s["h1"], norm_mlp_gain, [dxn2], res=dh, also_bf16=True,
                                                name="norm_mlp_bwd")
        gb["w_out"] = dw(s["cat"], dh1_b, "w_out")
        dcat = _mm(dh1_b, wl["w_out"], tb=True, b_kind="rows", name="out_proj_dx")
        if l < N_A:
            dq, dmk, dmv = _mem_attn_bwd(s["qn"], s["q_col"], s["mkn"], s["mkv"], dcat, dq_width=MEM_WIDTH, name="mem_attn_bwd")
            dproj, dmix, dscale = _pool_mix_bwd(s["proj"], s["mix"], s["scale"], dcat, name="pool_bwd")
            g["pool_w"][l] = _diag_blocks(dmix)
            g["pool_scale"][l] = dscale.reshape(MAIN_WIDTH)
            dproj, dgain = _seg_rms_bwd(s["proj"], s["q_gain"], [dq], width=MEM_WIDTH, col=mem_cols, out_buf=dproj,
                                        out_col=mem_cols, name="mem_q_norm_bwd")
            g["mem_q_norm"][l] = _fold_heads(dgain, MEM_HEADS)
        else:
            j = l - N_A
            dqn, dmk, dmv = _mem_attn_bwd(s["qn"], s["q_col"], s["mkn"], s["mkv"], dcat, dq_width=D_MODEL, name="mem_attn_bwd")
            dqn, dk_l, dv_l, dsinks = _swa_attn_bwd(s["qn"], kn, kv, p["sinks"][j], dcat, dqn, name="swa_bwd")
            dks.append(dk_l)
            dvs.append(dv_l)
            g["sinks"][j] = dsinks[0, :SWA_Q_HEADS]
            dproj, dgain = _seg_rms_bwd(s["proj"], s["q_gain"], [dqn], width=D_MODEL, col=0, name="q_norm_bwd")
            g["q_norm"][j] = _fold_heads(dgain[:, :MAIN_WIDTH], SWA_Q_HEADS)
            g["mem_q_norm"][l] = _fold_heads(dgain[:, MAIN_WIDTH:], MEM_HEADS)
        dmk_raw, dgain = _seg_rms_bwd(s["mkv"], s["mk_gain"], [dmk], width=MEM_WIDTH, col=0, name="mem_k_norm_bwd")
        g["mem_k_norm"][l] = _fold_heads(dgain, MEM_HEADS)
        dmkv = jnp.concatenate([dmk_raw, dmv.astype(BF16)], axis=1)
        gb["w_mem_kv"] = dw(s["memn"], dmkv, "w_mem_kv")
        dmemn = _mm(dmkv, wl["w_mem_kv"], tb=True, b_kind="rows", name="mem_kv_proj_dx")
        g["mem_norm"][l] = _rms_bwd(mem, p["mem_norm"][l], [dmemn], want_dx=False, name="mem_norm_bwd")
        gb["w_in"] = dw(s["xn1"], dproj, "w_in")
        dxn1 = _mm(dproj, wl["w_in"], tb=True, b_kind="rows", name="in_proj_dx")
        if l in (0, N_A):
            dh, g["norm_mix"][l] = _rms_bwd(s["h"], p["norm_mix"][l], [dxn1], res=dh1, name="norm_mix_bwd")
        else:
            dh, dh_b, g["norm_mix"][l] = _rms_bwd(s["h"], p["norm_mix"][l], [dxn1], res=dh1, also_bf16=True,
                                                  name="norm_mix_bwd")
        if l == N_A:
            dkv, dgain = _seg_rms_bwd(kv, k_gain, dks, width=KV_HALF, col=0, out_buf=lax.empty((T, 2 * KV_HALF), BF16),
                                      name="k_norm_bwd")
            g["k_norm"] = _fold_heads(dgain, SWA_KV_HEADS)
            dkv = _sum_into(dvs[0], dvs[1], dkv, 1, name="dv_sum")
            g_kv = _mm(hn_kv, dkv, ta=True, out_kind="rows", out_buf=lax.empty(w_kv.shape, BF16), name="w_kv_grad")
            dhn = _mm(dkv, w_kv, tb=True, b_kind="rows", name="kv_proj_dx")
            dh, dh_b, g["kv_norm"] = _rms_bwd(h_kv, p["kv_norm"], [dhn], res=dh, also_bf16=True, name="kv_norm_bwd")
        if l + 1 < DEPTH:
            reduce_wait(l + 1, dh)
        token = reduce_layer(l, gb)
    grads = {n: (jnp.stack(v) if isinstance(v, list) else v) for n, v in g.items()}
    return loss, dh, grads, g_kv


def kernel(x, mem, norm_mix, w_in, pool_w, pool_scale, kv_norm, w_kv, k_norm, q_norm, sinks, mem_norm, w_mem_kv, mem_q_norm, mem_k_norm, w_out, norm_mlp, w_up, w_down, loss_target, m_norm_mix, m_w_in, m_pool_w, m_pool_scale, m_kv_norm, m_w_kv, m_k_norm, m_q_norm, m_sinks, m_mem_norm, m_w_mem_kv, m_mem_q_norm, m_mem_k_norm, m_w_out, m_norm_mlp, m_w_up, m_w_down, v_norm_mix, v_w_in, v_pool_w, v_pool_scale, v_kv_norm, v_w_kv, v_k_norm, v_q_norm, v_sinks, v_mem_norm, v_w_mem_kv, v_mem_q_norm, v_mem_k_norm, v_w_out, v_norm_mlp, v_w_up, v_w_down):
    weights = dict(norm_mix=norm_mix, w_in=w_in, pool_w=pool_w, pool_scale=pool_scale, kv_norm=kv_norm, w_kv=w_kv,
                   k_norm=k_norm, q_norm=q_norm, sinks=sinks, mem_norm=mem_norm, w_mem_kv=w_mem_kv,
                   mem_q_norm=mem_q_norm, mem_k_norm=mem_k_norm, w_out=w_out, norm_mlp=norm_mlp, w_up=w_up, w_down=w_down)
    mom1 = dict(norm_mix=m_norm_mix, w_in=m_w_in, pool_w=m_pool_w, pool_scale=m_pool_scale, kv_norm=m_kv_norm, w_kv=m_w_kv,
                k_norm=m_k_norm, q_norm=m_q_norm, sinks=m_sinks, mem_norm=m_mem_norm, w_mem_kv=m_w_mem_kv,
                mem_q_norm=m_mem_q_norm, mem_k_norm=m_mem_k_norm, w_out=m_w_out, norm_mlp=m_norm_mlp, w_up=m_w_up,
                w_down=m_w_down)
    mom2 = dict(norm_mix=v_norm_mix, w_in=v_w_in, pool_w=v_pool_w, pool_scale=v_pool_scale, kv_norm=v_kv_norm, w_kv=v_w_kv,
                k_norm=v_k_norm, q_norm=v_q_norm, sinks=v_sinks, mem_norm=v_mem_norm, w_mem_kv=v_w_mem_kv,
                mem_q_norm=v_mem_q_norm, mem_k_norm=v_mem_k_norm, w_out=v_w_out, norm_mlp=v_norm_mlp, w_up=v_w_up,
                w_down=v_w_down)
    names = list(weights)
    x_pos, y_pos, core = (lax.axis_index(n).astype(jnp.int32) for n in AXES)
    me, my_chip = 4 * x_pos + 2 * y_pos + core, 2 * x_pos + y_pos
    shard = MAIN_WIDTH // N_DEV

    def layer_shards(l, zero=0.0):
        return [(weights[n][l:l + 1] + zero).astype(BF16) for n in LAYERED]

    def usable(arrays):
        wl = dict(zip(LAYERED, arrays))
        wl["w_up"] = wl["w_up"].transpose(1, 2, 0, 3).reshape(1, D_MODEL, D_FF)
        return wl

    scale_block = jnp.pad(pool_scale, ((0, 8 - N_A), (0, 128 - shard)))
    *first, first_done = _all_gather(layer_shards(0) + [w_kv[None].astype(BF16), scale_block], name="gather_first")
    p = {n: weights[n] for n in SMALL_NAMES}
    p["pool_scale"] = first[-1][:, :N_A, :shard].transpose(1, 0, 2).reshape(N_A, MAIN_WIDTH)
    gathers, reduces, parts = {}, {}, {}

    def fetch(l, after):
        if l == 0:
            got, done = first[:len(LAYERED)], first_done
        else:
            got, done = _push_wait(_gather_copies, *gathers.pop(l), after, name=f"gather_wait_{l}")
        token = 0.0
        if l + 1 < DEPTH:
            srcs = layer_shards(l + 1, done[0, 0])
            *handles, block = _push_start(_gather_copies, N_DEV - 1, srcs, [_with_own_slot(a, me, N_DEV) for a in srcs],
                                          name=f"gather_start_{l + 1}")
            gathers[l + 1], token = handles, block[0, 0]
        return usable(got), token

    def by_core(gb):
        gb = dict(gb)
        if "w_up" in gb:
            gb["w_up"] = gb["w_up"].reshape(D_MODEL, N_DEV, D_FF // N_DEV).transpose(1, 0, 2)
        order = [n for n in LAYERED if n in gb] + [n for n in gb if n not in LAYERED]
        return {n: gb[n].reshape((N_CHIP, 2) + _view2d(gb[n].shape[1:] if n == "w_up" else gb[n].shape[2:])) for n in order}

    def chip_sums(gb, tag, whole=()):
        views = by_core(gb)
        sib, sib_whole = _sibling_exchange(list(views.values()), list(whole), name="reduce_sibling_" + tag)
        sums = [_pair_sum(a, b, core, name=f"chip_sum_{n}_{tag}", out_dtype=BF16) for (n, a), b in zip(views.items(), sib)]
        return sums, sib_whole

    def reduce_layer(l, gb):
        if l == 0:
            reduces[0] = gb
            return None
        sums, _ = chip_sums(gb, str(l))
        lands = [_with_own_slot(lax.dynamic_index_in_dim(a, my_chip, 0, keepdims=False), my_chip, N_CHIP) for a in sums]
        *handles, block = _push_start(_chip_copies, N_CHIP - 1, sums, lands, name=f"reduce_start_{l}")
        reduces[l] = handles
        return block[0, 0]

    def reduce_wait(l, after):
        parts[l], _ = _push_wait(_chip_copies, *reduces.pop(l), after, name=f"reduce_wait_{l}")

    loss, grad_x, grads, g_kv = _train_pass(x[0], mem[0], loss_target[0], p, first[len(LAYERED)], fetch, reduce_layer, reduce_wait)

    last = dict(reduces.pop(0))
    last["w_kv"] = g_kv
    last["pool_scale"] = grads["pool_scale"].reshape(N_A, N_DEV, shard).transpose(1, 0, 2).astype(BF16)[:, None]
    small = _pack_small(grads)
    sums, (sib_small,) = chip_sums(last, "0", whole=[small])
    chip_small = _pair_sum(small, sib_small, core, name="chip_sum_small", out_dtype=F32)
    got, (parts_small,) = _chip_exchange(sums, [chip_small], name="reduce_chips_0")
    parts[0] = got[:len(LAYERED)]
    parts_kv, parts_scale = got[len(LAYERED):]

    def adamw(n, n_parts):
        res = _adamw(n_parts, *(d[n].reshape(_view2d(d[n].shape)) for d in (weights, mom1, mom2)), name="adamw_" + n)
        return [r.reshape(weights[n].shape) for r in res]

    new = {n: adamw(n, [parts[l][k] for l in range(DEPTH)]) for k, n in enumerate(LAYERED)}
    new["w_kv"] = adamw("w_kv", [parts_kv])
    new["pool_scale"] = adamw("pool_scale", [parts_scale])
    res = _adamw([parts_small], _pack_small(weights), _pack_small(mom1), _pack_small(mom2), name="adamw_replicated")
    for n, vals in zip(SMALL_NAMES, zip(*(_unpack_small(r, weights).values() for r in res))):
        new[n] = list(vals)
    outs = [new[n][k] for k in range(4) for n in names]
    total = lax.psum(loss[0, 0], AXES)
    return (total, grad_x[None], *outs)
```

```python
import math

import jax
import jax.numpy as jnp
from jax import lax
from jax.experimental import pallas as pl
from jax.experimental.pallas import tpu as pltpu

F32 = jnp.float32
BF16 = jnp.bfloat16
MESH = pl.DeviceIdType.MESH
AXES = ("x", "y", "c")

D_MODEL = 1024
DEPTH = 4
N_A = 2
HEAD_DIM = 64
MEM_HEADS = 4
MEM_WIDTH = MEM_HEADS * HEAD_DIM
MAIN_WIDTH = D_MODEL - MEM_WIDTH
POOL_GROUPS = 4
POOL_GROUP_DIM = MAIN_WIDTH // POOL_GROUPS
POOL_HALO = 16
SWA_Q_HEADS = MAIN_WIDTH // HEAD_DIM
SWA_KV_HEADS = 4
SWA_GROUP = SWA_Q_HEADS // SWA_KV_HEADS
KV_HALF = SWA_KV_HEADS * HEAD_DIM
BLOCK = 128
D_FF = 4 * D_MODEL
EPS = 1e-6
SCALE = HEAD_DIM ** -0.5
NEG = float(jnp.finfo(jnp.float32).min)
N_DEV = 8
N_CHIP = 4

ADAM_LR = 0.001
ADAM_B1 = 0.9
ADAM_B2 = 0.999
ADAM_EPS = 1e-08
ADAM_WD = 0.01
ADAM_STEP = 10

PACK_W = 512
VMEM_LIMIT = 52 * 1024 * 1024
MM_TILE = 1024

LAYERED = ("w_in", "w_mem_kv", "w_out", "w_up", "w_down")
SMALL_NAMES = ("norm_mix", "pool_w", "kv_norm", "k_norm", "q_norm", "sinks", "mem_norm", "mem_q_norm", "mem_k_norm",
               "norm_mlp")


ANY = pl.BlockSpec(memory_space=pl.ANY)


def _params(*sem):
    return pltpu.CompilerParams(dimension_semantics=sem, vmem_limit_bytes=VMEM_LIMIT)


def _mm(a, b, *, name, ta=False, tb=False, b_kind=None, layer=0, res=None, relu2=False, mul2=None, out_dtype=F32,
        out_kind=None, out_buf=None):
    if ta:
        K, M = a.shape
    else:
        M, K = a.shape
    if b_kind is None:
        rows_b, cols_b = b.shape
    elif b_kind == "rows":
        rows_b, cols_b = b.shape[0] * b.shape[2], b.shape[3]
    else:
        rows_b, cols_b = b.shape[1:]
    N, K2 = (rows_b, cols_b) if tb else (cols_b, rows_b)
    assert K == K2, (a.shape, b.shape)
    tm = min(M, MM_TILE if K <= MM_TILE else MM_TILE // 2)
    tn = min(N, MM_TILE)
    assert M % tm == 0 and N % tn == 0
    row_tile, col_tile = (tn, K) if tb else (K, tn)
    a_spec = pl.BlockSpec((K, tm), lambda j, i: (0, i)) if ta else pl.BlockSpec((tm, K), lambda j, i: (i, 0))

    def rc(j):
        return (j, 0) if tb else (0, j)

    if b_kind is None:
        b_spec = pl.BlockSpec((row_tile, col_tile), lambda j, i: rc(j))
    elif b_kind == "rows":
        per = row_tile // b.shape[2]
        b_spec = pl.BlockSpec((per, None, b.shape[2], col_tile), lambda j, i: (rc(j)[0], layer, 0, rc(j)[1]))
    else:
        b_spec = pl.BlockSpec((None, row_tile, col_tile), lambda j, i: (layer, *rc(j)))
    o_spec = pl.BlockSpec((tm, tn), lambda j, i: (i, j))
    dn = (((0 if ta else 1,), (1 if tb else 0,)), ((), ()))
    extra = [e for e in (res, mul2) if e is not None]
    n_in = 2 + len(extra) + (1 if out_buf is not None else 0)

    def body(*refs):
        a_ref, b_ref = refs[0], refs[1]
        extra_refs = refs[2:2 + len(extra)]
        out = refs[n_in]
        bv = b_ref[...].astype(BF16).reshape(row_tile, col_tile)
        v = lax.dot_general(a_ref[...].astype(BF16), bv, dn, preferred_element_type=F32)
        if res is not None:
            v = extra_refs[0][...] + v
        elif mul2 is not None:
            v = v * (2.0 * jnp.sqrt(extra_refs[0][...].astype(F32)))
        if relu2:
            r = jnp.maximum(v, 0.0)
            v = r * r
        out[...] = v.astype(out.dtype).reshape(out.shape)

    in_specs = [a_spec, b_spec] + [o_spec] * len(extra)
    operands = [a, b, *extra]
    aliases = {}
    if out_kind is None:
        out_shape = jax.ShapeDtypeStruct((M, N), BF16 if relu2 else out_dtype)
        out_specs = o_spec
    else:
        if out_kind == "rows":
            s = out_buf.shape[2]
            out_specs = pl.BlockSpec((tm // s, None, s, tn), lambda j, i: (i, layer, 0, j))
        else:
            out_specs = pl.BlockSpec((None, tm, tn), lambda j, i: (layer, i, j))
        out_shape = jax.ShapeDtypeStruct(out_buf.shape, out_buf.dtype)
        in_specs.append(ANY)
        operands.append(out_buf)
        aliases = {len(operands) - 1: 0}
    return pl.pallas_call(
        body, name=name, grid=(N // tn, M // tm), in_specs=in_specs, out_specs=out_specs, out_shape=out_shape,
        input_output_aliases=aliases, compiler_params=_params("parallel", "parallel"),
    )(*operands)


def _row_tile(rows, d):
    t = min(rows, (512 * 1024) // d)
    while rows % t or (t != rows and t % 16):
        t -= 1
    return t


def _rms_fwd(x, g, *, name, out_dtype=BF16):
    R, D = x.shape
    tr = _row_tile(R, D)

    def body(x_ref, g_ref, o_ref):
        xv = x_ref[...].astype(F32)
        r = lax.rsqrt(jnp.mean(xv * xv, axis=-1, keepdims=True) + EPS)
        o_ref[...] = ((xv * r) * g_ref[...]).astype(o_ref.dtype)

    return pl.pallas_call(
        body, name=name, grid=(R // tr,),
        in_specs=[pl.BlockSpec((tr, D), lambda i: (i, 0)), pl.BlockSpec((1, D), lambda i: (0, 0))],
        out_specs=pl.BlockSpec((tr, D), lambda i: (i, 0)), out_shape=jax.ShapeDtypeStruct((R, D), out_dtype),
        compiler_params=_params("parallel"),
    )(x, g.reshape(1, D))


def _rms_bwd(x, g, dys, *, name, res=None, want_dx=True, also_bf16=False):
    R, D = x.shape
    tr = _row_tile(R, D)
    n_dy = len(dys)
    has_res = res is not None

    def body(*refs):
        x_ref, g_ref = refs[0], refs[1]
        dy_refs = refs[2:2 + n_dy]
        res_ref = refs[2 + n_dy] if has_res else None
        outs = refs[2 + n_dy + (1 if has_res else 0):]
        dg_ref = outs[-1]
        i = pl.program_id(0)
        xv = x_ref[...].astype(F32)
        dy = dy_refs[0][...].astype(F32)
        for extra in dy_refs[1:]:
            dy = dy + extra[...].astype(F32)
        r = lax.rsqrt(jnp.mean(xv * xv, axis=-1, keepdims=True) + EPS)
        xh = xv * r
        part = jnp.sum(dy * xh, axis=0, keepdims=True)

        @pl.when(i == 0)
        def _():
            dg_ref[...] = part

        @pl.when(i > 0)
        def _():
            dg_ref[...] += part

        if want_dx:
            gdy = dy * g_ref[...]
            dx = r * (gdy - xh * jnp.mean(gdy * xh, axis=-1, keepdims=True))
            if has_res:
                dx = res_ref[...] + dx
            outs[0][...] = dx
            if also_bf16:
                outs[1][...] = dx.astype(BF16)

    row = pl.BlockSpec((tr, D), lambda i: (i, 0))
    vec = pl.BlockSpec((1, D), lambda i: (0, 0))
    out_shape = [jax.ShapeDtypeStruct((1, D), F32)]
    out_specs = [vec]
    if also_bf16:
        out_shape = [jax.ShapeDtypeStruct((R, D), BF16)] + out_shape
        out_specs = [row] + out_specs
    if want_dx:
        out_shape = [jax.ShapeDtypeStruct((R, D), F32)] + out_shape
        out_specs = [row] + out_specs
    outs = pl.pallas_call(
        body, name=name, grid=(R // tr,),
        in_specs=[row, vec] + [row] * (n_dy + (1 if has_res else 0)), out_specs=out_specs, out_shape=out_shape,
        compiler_params=_params("arbitrary"),
    )(x, g.reshape(1, D), *dys, *([res] if has_res else []))
    return (*outs[:-1], outs[-1].reshape(D)) if want_dx else outs[0].reshape(D)


def _add(a, b, *, name):
    R, D = a.shape
    tr = _row_tile(R, D)

    def body(a_ref, b_ref, o_ref):
        o_ref[...] = a_ref[...] + b_ref[...]

    row = pl.BlockSpec((tr, D), lambda i: (i, 0))
    return pl.pallas_call(body, name=name, grid=(R // tr,), in_specs=[row, row], out_specs=row,
                          out_shape=jax.ShapeDtypeStruct((R, D), a.dtype), compiler_params=_params("parallel"))(a, b)


POOL_TILE = 512


def _pool_window(group):
    return lax.shift_left(jnp.int32(2), group)


def _pool_diff(u_ref, halo_ref, group, tile):
    first = tile == 0
    halo = jnp.where(first, 0.0, halo_ref[...])
    ext = jnp.concatenate([halo, u_ref[...]], axis=0)
    n = ext.shape[0]
    s1 = ext + pltpu.roll(ext, 1, 0)
    s2 = s1 + pltpu.roll(s1, 2, 0)
    s3 = s2 + pltpu.roll(s2, 4, 0)
    s4 = s3 + pltpu.roll(s3, 8, 0)
    ws = jnp.where(group == 0, s1, jnp.where(group == 1, s2, jnp.where(group == 2, s3, s4)))[POOL_HALO:n]
    t = tile * POOL_TILE + lax.broadcasted_iota(jnp.int32, (POOL_TILE, 1), 0)
    cnt = jnp.minimum(t + 1, _pool_window(group)).astype(F32)
    return ws / cnt - u_ref[...], cnt


def _pool_specs():
    per_tile = POOL_TILE // POOL_HALO
    cur = pl.BlockSpec((None, POOL_TILE, POOL_GROUP_DIM), lambda g, i: (g, i, 0))
    prev = pl.BlockSpec((None, POOL_HALO, POOL_GROUP_DIM), lambda g, i: (g, jnp.maximum(i * per_tile - 1, 0), 0))
    pw = pl.BlockSpec((None, POOL_GROUP_DIM, POOL_GROUP_DIM), lambda g, i: (g, 0, 0))
    vec = pl.BlockSpec((None, 1, POOL_GROUP_DIM), lambda g, i: (g, 0, 0))
    return cur, prev, pw, vec


def _pool_fwd(u, pw, scale, *, name):
    G, T, C = u.shape
    assert T % POOL_TILE == 0
    cur, prev, pw_spec, vec = _pool_specs()

    def body(u_ref, halo_ref, pw_ref, sc_ref, o_ref):
        d, _ = _pool_diff(u_ref, halo_ref, pl.program_id(0), pl.program_id(1))
        mixed = jnp.dot(d.astype(BF16), pw_ref[...].astype(BF16), preferred_element_type=F32)
        o_ref[...] = (mixed * sc_ref[...]).astype(o_ref.dtype)

    return pl.pallas_call(
        body, name=name, grid=(G, T // POOL_TILE), in_specs=[cur, prev, pw_spec, vec], out_specs=cur,
        out_shape=jax.ShapeDtypeStruct((G, T, C), BF16), compiler_params=_params("parallel", "parallel"),
    )(u, u, pw, scale)


def _pool_bwd(u, pw, scale, dout, *, name):
    G, T, C = u.shape
    nt = T // POOL_TILE
    per_tile = POOL_TILE // POOL_HALO
    cur, prev, pw_spec, vec = _pool_specs()
    nxt = pl.BlockSpec((None, POOL_HALO, C), lambda g, i: (g, jnp.minimum((i + 1) * per_tile, nt * per_tile - 1), 0))

    def body(u_ref, halo_ref, pw_ref, sc_ref, do_ref, donext_ref, du_ref, dpw_ref, dsc_ref):
        group, tile = pl.program_id(0), pl.program_id(1)
        d, cnt = _pool_diff(u_ref, halo_ref, group, tile)
        pwb = pw_ref[...].astype(BF16)
        db = d.astype(BF16)
        mixed = jnp.dot(db, pwb, preferred_element_type=F32)
        dout = do_ref[...].astype(F32)
        dsc = jnp.sum(dout * mixed, axis=0, keepdims=True)
        sc = sc_ref[...]
        dmix = (dout * sc).astype(BF16)
        dpw = lax.dot_general(db, dmix, (((0,), (0,)), ((), ())), preferred_element_type=F32)

        @pl.when(tile == 0)
        def _():
            dpw_ref[...] = dpw
            dsc_ref[...] = dsc

        @pl.when(tile > 0)
        def _():
            dpw_ref[...] += dpw
            dsc_ref[...] += dsc

        last = tile == nt - 1
        dnext = jnp.where(last, 0.0, donext_ref[...].astype(F32))
        dmix_ext = jnp.concatenate([dmix, (dnext * sc).astype(BF16)], axis=0)
        dd_ext = lax.dot_general(dmix_ext, pwb, (((1,), (1,)), ((), ())), preferred_element_type=F32)
        window = _pool_window(group).astype(F32)
        cnt_ext = jnp.concatenate([cnt, jnp.broadcast_to(window, (POOL_HALO, 1))], axis=0)
        q = dd_ext / cnt_ext
        n = q.shape[0]
        r1 = q + pltpu.roll(q, n - 1, 0)
        r2 = r1 + pltpu.roll(r1, n - 2, 0)
        r3 = r2 + pltpu.roll(r2, n - 4, 0)
        r4 = r3 + pltpu.roll(r3, n - 8, 0)
        back = jnp.where(group == 0, r1, jnp.where(group == 1, r2, jnp.where(group == 2, r3, r4)))
        du_ref[...] = back[0:POOL_TILE] - dd_ext[0:POOL_TILE]

    return pl.pallas_call(
        body, name=name, grid=(G, nt), in_specs=[cur, prev, pw_spec, vec, cur, nxt],
        out_specs=(cur, pw_spec, vec),
        out_shape=(jax.ShapeDtypeStruct((G, T, C), F32), jax.ShapeDtypeStruct((G, C, C), F32),
                   jax.ShapeDtypeStruct((G, 1, C), F32)),
        compiler_params=_params("arbitrary", "arbitrary"),
    )(u, u, pw, scale, dout, dout)


def _softmax(q, k, bias, valid, sink):
    s = lax.dot_general(q, k, (((1,), (1,)), ((), ())), preferred_element_type=F32) * SCALE
    if bias is not None:
        s = s - bias
    if valid is not None:
        s = jnp.where(valid, s, NEG)
    m = jnp.max(s, axis=-1, keepdims=True)
    if sink is not None:
        m = jnp.maximum(m, sink)
    e = jnp.exp(s - m)
    z = jnp.sum(e, axis=-1, keepdims=True)
    if sink is None:
        return e * (1.0 / z), None
    es = jnp.exp(sink - m)
    inv = 1.0 / (z + es)
    return e * inv, es * inv


def _swa_terms(sink_ref, kvh, blk):
    rows = SWA_GROUP * BLOCK
    row = lax.broadcasted_iota(jnp.int32, (rows, 1), 0)
    grp = row // BLOCK
    head = (kvh * SWA_GROUP + grp + 1).astype(F32)
    slope = jnp.exp(head * (-8.0 * math.log(2.0) / SWA_Q_HEADS))
    qi = lax.broadcasted_iota(jnp.int32, (rows, 2 * BLOCK), 0) % BLOCK
    kj = lax.broadcasted_iota(jnp.int32, (rows, 2 * BLOCK), 1)
    dist = qi + BLOCK - kj
    valid = (dist >= 0) & (dist < BLOCK) & ((blk > 0) | (kj >= BLOCK))
    bias = slope * dist.astype(F32)
    s0, s1, s2 = (sink_ref[kvh * SWA_GROUP + g] for g in range(SWA_GROUP))
    sink = jnp.where(grp == 0, s0, jnp.where(grp == 1, s1, s2))
    return bias, valid, sink, grp


def _swa_fwd(q, k, v, sinks, *, name):
    H, G, T, hd = q.shape
    nb = T // BLOCK
    rows = G * BLOCK

    def body(sink_ref, q_ref, kp_ref, kc_ref, vp_ref, vc_ref, o_ref):
        kvh, blk = pl.program_id(0), pl.program_id(1)
        bias, valid, sink, _ = _swa_terms(sink_ref, kvh, blk)
        kk = jnp.concatenate([kp_ref[...], kc_ref[...]], axis=0)
        vv = jnp.concatenate([vp_ref[...], vc_ref[...]], axis=0)
        p, _ = _softmax(q_ref[...].reshape(rows, hd), kk, bias, valid, sink)
        o = jnp.dot(p.astype(BF16), vv, preferred_element_type=F32)
        o_ref[...] = o.reshape(G, BLOCK, hd).astype(o_ref.dtype)

    qs = pl.BlockSpec((None, G, BLOCK, hd), lambda h, n: (h, 0, n, 0))
    prev = pl.BlockSpec((None, BLOCK, hd), lambda h, n: (h, jnp.maximum(n - 1, 0), 0))
    cur = pl.BlockSpec((None, BLOCK, hd), lambda h, n: (h, n, 0))
    return pl.pallas_call(
        body, name=name, grid=(H, nb),
        in_specs=[pl.BlockSpec(memory_space=pltpu.SMEM), qs, prev, cur, prev, cur], out_specs=qs,
        out_shape=jax.ShapeDtypeStruct((H, G, T, hd), BF16), compiler_params=_params("parallel", "parallel"),
    )(sinks, q, k, k, v, v)


def _swa_bwd(q, k, v, sinks, do, *, name):
    H, G, T, hd = q.shape
    nb = T // BLOCK
    rows = G * BLOCK

    def body(sink_ref, q_ref, do_ref, kp_ref, kc_ref, vp_ref, vc_ref, dq_ref, dk_ref, dv_ref, ds_ref, ck, cv):
        kvh, blk = pl.program_id(0), pl.program_id(1)

        @pl.when(blk == 0)
        def _():
            ck[...] = jnp.zeros_like(ck)
            cv[...] = jnp.zeros_like(cv)
            ds_ref[...] = jnp.zeros_like(ds_ref)

        @pl.when(blk < nb)
        def _():
            bias, valid, sink, grp = _swa_terms(sink_ref, kvh, blk)
            kk = jnp.concatenate([kp_ref[...], kc_ref[...]], axis=0)
            vv = jnp.concatenate([vp_ref[...], vc_ref[...]], axis=0)
            qq = q_ref[...].reshape(rows, hd)
            dout = do_ref[...].reshape(rows, hd)
            p, ps = _softmax(qq, kk, bias, valid, sink)
            dp = lax.dot_general(dout, vv, (((1,), (1,)), ((), ())), preferred_element_type=F32)
            dsum = jnp.sum(p * dp, axis=-1, keepdims=True)
            ds = (p * (dp - dsum)).astype(BF16)
            dq = jnp.dot(ds, kk, preferred_element_type=F32) * SCALE
            dq_ref[...] = dq.reshape(G, BLOCK, hd)
            dk = lax.dot_general(ds, qq, (((0,), (0,)), ((), ())), preferred_element_type=F32) * SCALE
            dv = lax.dot_general(p.astype(BF16), dout, (((0,), (0,)), ((), ())), preferred_element_type=F32)
            dk_ref[...] = ck[...] + dk[0:BLOCK]
            dv_ref[...] = cv[...] + dv[0:BLOCK]
            ck[...] = dk[BLOCK:2 * BLOCK]
            cv[...] = dv[BLOCK:2 * BLOCK]
            dsink = -(ps * dsum)
            lane = lax.broadcasted_iota(jnp.int32, (1, 128), 1)
            acc = jnp.zeros((1, 128), F32)
            for g in range(G):
                acc = acc + jnp.where(lane == g, jnp.sum(jnp.where(grp == g, dsink, 0.0)), 0.0)
            ds_ref[...] += acc

        @pl.when(blk == nb)
        def _():
            dk_ref[...] = ck[...]
            dv_ref[...] = cv[...]

    def at(n):
        return jnp.minimum(n, nb - 1)

    qs = pl.BlockSpec((None, G, BLOCK, hd), lambda h, n: (h, 0, at(n), 0))
    prev = pl.BlockSpec((None, BLOCK, hd), lambda h, n: (h, jnp.maximum(at(n) - 1, 0), 0))
    cur = pl.BlockSpec((None, BLOCK, hd), lambda h, n: (h, at(n), 0))
    late = pl.BlockSpec((None, BLOCK, hd), lambda h, n: (h, jnp.maximum(n - 1, 0), 0))
    dsink_spec = pl.BlockSpec((None, 1, 128), lambda h, n: (h, 0, 0))
    return pl.pallas_call(
        body, name=name, grid=(H, nb + 1),
        in_specs=[pl.BlockSpec(memory_space=pltpu.SMEM), qs, qs, prev, cur, prev, cur],
        out_specs=(qs, late, late, dsink_spec),
        out_shape=(jax.ShapeDtypeStruct((H, G, T, hd), F32), jax.ShapeDtypeStruct((H, T, hd), F32),
                   jax.ShapeDtypeStruct((H, T, hd), F32), jax.ShapeDtypeStruct((H, 1, 128), F32)),
        scratch_shapes=[pltpu.VMEM((BLOCK, hd), F32), pltpu.VMEM((BLOCK, hd), F32)],
        compiler_params=_params("arbitrary", "arbitrary"),
    )(sinks, q, do, k, k, v, v)


MEM_Q_TILE = 512


def _mem_fwd(q, k, v, *, name):
    H, T, hd = q.shape
    M = k.shape[1]
    tq = min(T, MEM_Q_TILE)

    def body(q_ref, k_ref, v_ref, o_ref):
        p, _ = _softmax(q_ref[...], k_ref[...], None, None, None)
        o_ref[...] = jnp.dot(p.astype(BF16), v_ref[...], preferred_element_type=F32).astype(o_ref.dtype)

    qs = pl.BlockSpec((None, tq, hd), lambda h, i: (h, i, 0))
    ks = pl.BlockSpec((None, M, hd), lambda h, i: (h, 0, 0))
    return pl.pallas_call(body, name=name, grid=(H, T // tq), in_specs=[qs, ks, ks], out_specs=qs,
                          out_shape=jax.ShapeDtypeStruct((H, T, hd), BF16),
                          compiler_params=_params("parallel", "parallel"))(q, k, v)


def _mem_bwd(q, k, v, do, *, name):
    H, T, hd = q.shape
    M = k.shape[1]
    tq = min(T, MEM_Q_TILE)

    def body(q_ref, do_ref, k_ref, v_ref, dq_ref, dk_ref, dv_ref):
        i = pl.program_id(1)
        qq, kk, vv, dout = q_ref[...], k_ref[...], v_ref[...], do_ref[...]
        p, _ = _softmax(qq, kk, None, None, None)
        dp = lax.dot_general(dout, vv, (((1,), (1,)), ((), ())), preferred_element_type=F32)
        dsum = jnp.sum(p * dp, axis=-1, keepdims=True)
        ds = (p * (dp - dsum)).astype(BF16)
        dq_ref[...] = jnp.dot(ds, kk, preferred_element_type=F32) * SCALE
        dk = lax.dot_general(ds, qq, (((0,), (0,)), ((), ())), preferred_element_type=F32) * SCALE
        dv = lax.dot_general(p.astype(BF16), dout, (((0,), (0,)), ((), ())), preferred_element_type=F32)

        @pl.when(i == 0)
        def _():
            dk_ref[...] = dk
            dv_ref[...] = dv

        @pl.when(i > 0)
        def _():
            dk_ref[...] += dk
            dv_ref[...] += dv

    qs = pl.BlockSpec((None, tq, hd), lambda h, i: (h, i, 0))
    ks = pl.BlockSpec((None, M, hd), lambda h, i: (h, 0, 0))
    return pl.pallas_call(
        body, name=name, grid=(H, T // tq), in_specs=[qs, qs, ks, ks], out_specs=(qs, ks, ks),
        out_shape=(jax.ShapeDtypeStruct((H, T, hd), F32), jax.ShapeDtypeStruct((H, M, hd), F32),
                   jax.ShapeDtypeStruct((H, M, hd), F32)),
        compiler_params=_params("arbitrary", "arbitrary"),
    )(q, do, k, v)


LANES = 128


def _seg_mean(v):
    r = lax.broadcasted_iota(jnp.int32, (LANES, LANES), 0) // HEAD_DIM
    c = lax.broadcasted_iota(jnp.int32, (LANES, LANES), 1) // HEAD_DIM
    seg = jnp.where(r == c, 1.0 / HEAD_DIM, 0.0).astype(BF16)
    hi = v.astype(BF16)
    lo = (v - hi.astype(F32)).astype(BF16)
    parts = []
    for g in range(v.shape[1] // LANES):
        sl = slice(g * LANES, (g + 1) * LANES)
        parts.append(jnp.dot(hi[:, sl], seg, preferred_element_type=F32) + jnp.dot(lo[:, sl], seg, preferred_element_type=F32))
    return parts[0] if len(parts) == 1 else jnp.concatenate(parts, axis=1)


def _cols(rows, width, col):
    return pl.BlockSpec((rows, width), lambda i: (i, col))


def _head_gain(g, heads):
    return jnp.tile(g, heads).reshape(1, heads * HEAD_DIM)


def _fold_heads(dg, heads):
    return dg.reshape(heads, HEAD_DIM).sum(axis=0)


def _seg_rms_fwd(x, gain, *, width, col, name):
    R = x.shape[0]
    tr = _row_tile(R, width)

    def body(x_ref, g_ref, o_ref):
        xv = x_ref[...]
        r = lax.rsqrt(_seg_mean(xv * xv) + EPS)
        o_ref[...] = ((xv * r) * g_ref[...]).astype(o_ref.dtype)

    return pl.pallas_call(
        body, name=name, grid=(R // tr,), in_specs=[_cols(tr, width, col), pl.BlockSpec((1, width), lambda i: (0, 0))],
        out_specs=_cols(tr, width, 0), out_shape=jax.ShapeDtypeStruct((R, width), BF16), compiler_params=_params("parallel"),
    )(x, gain)


def _seg_rms_bwd(x, gain, dys, *, width, col, name, out_buf=None, out_col=0):
    R = x.shape[0]
    tr = _row_tile(R, width)
    n_dy = len(dys)

    def body(*refs):
        x_ref, g_ref = refs[0], refs[1]
        dy_refs = refs[2:2 + n_dy]
        dx_ref, dg_ref = refs[-2], refs[-1]
        i = pl.program_id(0)
        xv = x_ref[...]
        dy = dy_refs[0][...]
        for extra in dy_refs[1:]:
            dy = dy + extra[...]
        r = lax.rsqrt(_seg_mean(xv * xv) + EPS)
        xh = xv * r
        part = jnp.sum(dy * xh, axis=0, keepdims=True)

        @pl.when(i == 0)
        def _():
            dg_ref[...] = part

        @pl.when(i > 0)
        def _():
            dg_ref[...] += part

        gdy = dy * g_ref[...]
        dx_ref[...] = (r * (gdy - xh * _seg_mean(gdy * xh))).astype(dx_ref.dtype)

    vec = pl.BlockSpec((1, width), lambda i: (0, 0))
    in_specs = [_cols(tr, width, col), vec] + [_cols(tr, width, 0)] * n_dy
    operands = [x, gain, *dys]
    aliases = {}
    dx_shape = jax.ShapeDtypeStruct((R, width), BF16)
    if out_buf is not None:
        in_specs.append(ANY)
        operands.append(out_buf)
        aliases = {len(operands) - 1: 0}
        dx_shape = jax.ShapeDtypeStruct(out_buf.shape, out_buf.dtype)
    return pl.pallas_call(
        body, name=name, grid=(R // tr,), in_specs=in_specs, out_specs=(_cols(tr, width, out_col), vec),
        out_shape=(dx_shape, jax.ShapeDtypeStruct((1, width), F32)), input_output_aliases=aliases,
        compiler_params=_params("arbitrary"),
    )(*operands)


def _sum_into(a, b, out_buf, out_col, *, name):
    R, width = a.shape
    tr = _row_tile(R, width)

    def body(a_ref, b_ref, _, o_ref):
        o_ref[...] = (a_ref[...] + b_ref[...]).astype(o_ref.dtype)

    return pl.pallas_call(
        body, name=name, grid=(R // tr,), in_specs=[_cols(tr, width, 0), _cols(tr, width, 0), ANY],
        out_specs=_cols(tr, width, out_col), out_shape=jax.ShapeDtypeStruct(out_buf.shape, out_buf.dtype),
        input_output_aliases={2: 0}, compiler_params=_params("parallel"),
    )(a, b, out_buf)


def _pool_lane_group():
    return lax.broadcasted_iota(jnp.int32, (1, MAIN_WIDTH), 1) // POOL_GROUP_DIM


def _pool_pick(group, per_window):
    s1, s2, s3, s4 = per_window
    return jnp.where(group == 0, s1, jnp.where(group == 1, s2, jnp.where(group == 2, s3, s4)))


def _pool_delta(u_ref, halo_ref, tile):
    group = _pool_lane_group()
    halo = jnp.where(tile == 0, 0.0, halo_ref[...])
    ext = jnp.concatenate([halo, u_ref[...]], axis=0)
    n = ext.shape[0]
    s1 = ext + pltpu.roll(ext, 1, 0)
    s2 = s1 + pltpu.roll(s1, 2, 0)
    s3 = s2 + pltpu.roll(s2, 4, 0)
    s4 = s3 + pltpu.roll(s3, 8, 0)
    ws = _pool_pick(group, (s1, s2, s3, s4))[POOL_HALO:n]
    t = tile * POOL_TILE + lax.broadcasted_iota(jnp.int32, (POOL_TILE, 1), 0)
    cnt = jnp.minimum(t + 1, _pool_pick(group, (2, 4, 8, 16))).astype(F32)
    return ws / cnt - u_ref[...], cnt


def _pool_in_specs():
    per_tile = POOL_TILE // POOL_HALO
    cur = _cols(POOL_TILE, MAIN_WIDTH, 0)
    prev = pl.BlockSpec((POOL_HALO, MAIN_WIDTH), lambda i: (jnp.maximum(i * per_tile - 1, 0), 0))
    mix = pl.BlockSpec((MAIN_WIDTH, MAIN_WIDTH), lambda i: (0, 0))
    vec = pl.BlockSpec((1, MAIN_WIDTH), lambda i: (0, 0))
    return cur, prev, mix, vec


def _pool_mix_fwd(proj, mix, scale, cat, *, name):
    T = proj.shape[0]
    assert T % POOL_TILE == 0
    cur, prev, mix_spec, vec = _pool_in_specs()

    def body(u_ref, halo_ref, mix_ref, sc_ref, _, o_ref):
        d, _cnt = _pool_delta(u_ref, halo_ref, pl.program_id(0))
        mixed = jnp.dot(d.astype(BF16), mix_ref[...].astype(BF16), preferred_element_type=F32)
        o_ref[...] = (mixed * sc_ref[...]).astype(o_ref.dtype)

    return pl.pallas_call(
        body, name=name, grid=(T // POOL_TILE,), in_specs=[cur, prev, mix_spec, vec, ANY], out_specs=cur,
        out_shape=jax.ShapeDtypeStruct(cat.shape, cat.dtype), input_output_aliases={4: 0}, compiler_params=_params("parallel"),
    )(proj, proj, mix, scale, cat)


def _pool_mix_bwd(proj, mix, scale, dcat, *, name):
    T = proj.shape[0]
    nt = T // POOL_TILE
    per_tile = POOL_TILE // POOL_HALO
    cur, prev, mix_spec, vec = _pool_in_specs()
    nxt = pl.BlockSpec((POOL_HALO, MAIN_WIDTH), lambda i: (jnp.minimum((i + 1) * per_tile, nt * per_tile - 1), 0))

    def body(u_ref, halo_ref, mix_ref, sc_ref, do_ref, donext_ref, du_ref, dmix_ref, dsc_ref):
        tile = pl.program_id(0)
        group = _pool_lane_group()
        d, cnt = _pool_delta(u_ref, halo_ref, tile)
        mixb = mix_ref[...].astype(BF16)
        db = d.astype(BF16)
        mixed = jnp.dot(db, mixb, preferred_element_type=F32)
        dout = do_ref[...]
        dsc = jnp.sum(dout * mixed, axis=0, keepdims=True)
        sc = sc_ref[...]
        dmixed = (dout * sc).astype(BF16)
        dmix = lax.dot_general(db, dmixed, (((0,), (0,)), ((), ())), preferred_element_type=F32)

        @pl.when(tile == 0)
        def _():
            dmix_ref[...] = dmix
            dsc_ref[...] = dsc

        @pl.when(tile > 0)
        def _():
            dmix_ref[...] += dmix
            dsc_ref[...] += dsc

        dnext = jnp.where(tile == nt - 1, 0.0, donext_ref[...])
        dmixed_ext = jnp.concatenate([dmixed, (dnext * sc).astype(BF16)], axis=0)
        dd_ext = lax.dot_general(dmixed_ext, mixb, (((1,), (1,)), ((), ())), preferred_element_type=F32)
        window = _pool_pick(group, (2.0, 4.0, 8.0, 16.0))
        cnt_ext = jnp.concatenate([cnt, jnp.broadcast_to(window, (POOL_HALO, MAIN_WIDTH))], axis=0)
        q = dd_ext / cnt_ext
        n = q.shape[0]
        r1 = q + pltpu.roll(q, n - 1, 0)
        r2 = r1 + pltpu.roll(r1, n - 2, 0)
        r3 = r2 + pltpu.roll(r2, n - 4, 0)
        r4 = r3 + pltpu.roll(r3, n - 8, 0)
        back = _pool_pick(group, (r1, r2, r3, r4))
        du_ref[...] = (back[0:POOL_TILE] - dd_ext[0:POOL_TILE]).astype(du_ref.dtype)

    return pl.pallas_call(
        body, name=name, grid=(nt,), in_specs=[cur, prev, mix_spec, vec, cur, nxt], out_specs=(cur, mix_spec, vec),
        out_shape=(jax.ShapeDtypeStruct((T, D_MODEL), BF16), jax.ShapeDtypeStruct((MAIN_WIDTH, MAIN_WIDTH), F32),
                   jax.ShapeDtypeStruct((1, MAIN_WIDTH), F32)),
        compiler_params=_params("arbitrary"),
    )(proj, proj, mix, scale, dcat, dcat)


def _head(a, h):
    return a[:, h * HEAD_DIM:(h + 1) * HEAD_DIM]


def _swa_mask(blk):
    rows = SWA_GROUP * BLOCK
    qi = lax.broadcasted_iota(jnp.int32, (rows, 2 * BLOCK), 0) % BLOCK
    kj = lax.broadcasted_iota(jnp.int32, (rows, 2 * BLOCK), 1)
    dist = qi + BLOCK - kj
    valid = (dist >= 0) & (dist < BLOCK) & ((blk > 0) | (kj >= BLOCK))
    return dist.astype(F32), valid


def _swa_head_terms(sink_ref, kvh, dist):
    grp = lax.broadcasted_iota(jnp.int32, (SWA_GROUP * BLOCK, 1), 0) // BLOCK
    slopes = [2.0 ** (-8.0 * (kvh * SWA_GROUP + g + 1) / SWA_Q_HEADS) for g in range(SWA_GROUP)]
    sinks = [sink_ref[kvh * SWA_GROUP + g] for g in range(SWA_GROUP)]
    slope = jnp.where(grp == 0, slopes[0], jnp.where(grp == 1, slopes[1], slopes[2]))
    sink = jnp.where(grp == 0, sinks[0], jnp.where(grp == 1, sinks[1], sinks[2]))
    return slope * dist, sink


def _stack_heads(a, kvh):
    return jnp.concatenate([_head(a, kvh * SWA_GROUP + g) for g in range(SWA_GROUP)], axis=0)


def _swa_specs(nb):
    def at(n):
        return jnp.minimum(n, nb - 1)

    q = pl.BlockSpec((BLOCK, MAIN_WIDTH), lambda n: (at(n), 0))
    k_prev = pl.BlockSpec((BLOCK, KV_HALF), lambda n: (jnp.maximum(at(n) - 1, 0), 0))
    k_cur = pl.BlockSpec((BLOCK, KV_HALF), lambda n: (at(n), 0))
    v_prev = pl.BlockSpec((BLOCK, KV_HALF), lambda n: (jnp.maximum(at(n) - 1, 0), 1))
    v_cur = pl.BlockSpec((BLOCK, KV_HALF), lambda n: (at(n), 1))
    return q, k_prev, k_cur, v_prev, v_cur


def _swa_attn_fwd(qn, kn, kv, sinks, cat, *, name):
    T = qn.shape[0]
    nb = T // BLOCK
    q_spec, k_prev, k_cur, v_prev, v_cur = _swa_specs(nb)

    def body(sink_ref, q_ref, kp_ref, kc_ref, vp_ref, vc_ref, _, o_ref):
        dist, valid = _swa_mask(pl.program_id(0))
        kk = jnp.concatenate([kp_ref[...], kc_ref[...]], axis=0)
        vv = jnp.concatenate([vp_ref[...], vc_ref[...]], axis=0).astype(BF16)
        q = q_ref[...]
        outs = []
        for kvh in range(SWA_KV_HEADS):
            bias, sink = _swa_head_terms(sink_ref, kvh, dist)
            p, _ps = _softmax(_stack_heads(q, kvh), _head(kk, kvh), bias, valid, sink)
            o = jnp.dot(p.astype(BF16), _head(vv, kvh), preferred_element_type=F32)
            outs += [o[g * BLOCK:(g + 1) * BLOCK] for g in range(SWA_GROUP)]
        o_ref[...] = jnp.concatenate(outs, axis=1).astype(o_ref.dtype)

    return pl.pallas_call(
        body, name=name, grid=(nb,),
        in_specs=[pl.BlockSpec(memory_space=pltpu.SMEM), q_spec, k_prev, k_cur, v_prev, v_cur, ANY], out_specs=q_spec,
        out_shape=jax.ShapeDtypeStruct(cat.shape, cat.dtype), input_output_aliases={6: 0}, compiler_params=_params("parallel"),
    )(sinks, qn, kn, kn, kv, kv, cat)


def _swa_attn_bwd(qn, kn, kv, sinks, dcat, dqn, *, name):
    T = qn.shape[0]
    nb = T // BLOCK
    q_spec, k_prev, k_cur, v_prev, v_cur = _swa_specs(nb)
    late = pl.BlockSpec((BLOCK, KV_HALF), lambda n: (jnp.maximum(n - 1, 0), 0))
    tn_dims = (((0,), (0,)), ((), ()))

    def body(sink_ref, q_ref, do_ref, kp_ref, kc_ref, vp_ref, vc_ref, _, dq_ref, dk_ref, dv_ref, ds_ref, ck, cv):
        blk = pl.program_id(0)

        @pl.when(blk == 0)
        def _():
            ck[...] = jnp.zeros_like(ck)
            cv[...] = jnp.zeros_like(cv)
            ds_ref[...] = jnp.zeros_like(ds_ref)

        @pl.when(blk < nb)
        def _():
            dist, valid = _swa_mask(blk)
            kk = jnp.concatenate([kp_ref[...], kc_ref[...]], axis=0)
            vv = jnp.concatenate([vp_ref[...], vc_ref[...]], axis=0).astype(BF16)
            q = q_ref[...]
            dout = do_ref[...].astype(BF16)
            lane = lax.broadcasted_iota(jnp.int32, (1, LANES), 1)
            dsinks = jnp.zeros((1, LANES), F32)
            dqs, dks, dvs = [], [], []
            for kvh in range(SWA_KV_HEADS):
                bias, sink = _swa_head_terms(sink_ref, kvh, dist)
                qq, kh, vh, dd = _stack_heads(q, kvh), _head(kk, kvh), _head(vv, kvh), _stack_heads(dout, kvh)
                p, ps = _softmax(qq, kh, bias, valid, sink)
                dp = lax.dot_general(dd, vh, (((1,), (1,)), ((), ())), preferred_element_type=F32)
                dsum = jnp.sum(p * dp, axis=-1, keepdims=True)
                ds = (p * (dp - dsum)).astype(BF16)
                dq = jnp.dot(ds, kh, preferred_element_type=F32) * SCALE
                dqs += [dq[g * BLOCK:(g + 1) * BLOCK] for g in range(SWA_GROUP)]
                dks.append(lax.dot_general(ds, qq, tn_dims, preferred_element_type=F32) * SCALE)
                dvs.append(lax.dot_general(p.astype(BF16), dd, tn_dims, preferred_element_type=F32))
                dsink = -(ps * dsum)
                for g in range(SWA_GROUP):
                    dsinks = dsinks + jnp.where(lane == kvh * SWA_GROUP + g, jnp.sum(dsink[g * BLOCK:(g + 1) * BLOCK]), 0.0)
            dq_ref[...] = jnp.concatenate(dqs, axis=1)
            dk = jnp.concatenate(dks, axis=1)
            dv = jnp.concatenate(dvs, axis=1)
            dk_ref[...] = ck[...] + dk[0:BLOCK]
            dv_ref[...] = cv[...] + dv[0:BLOCK]
            ck[...] = dk[BLOCK:2 * BLOCK]
            cv[...] = dv[BLOCK:2 * BLOCK]
            ds_ref[...] += dsinks

        @pl.when(blk == nb)
        def _():
            dk_ref[...] = ck[...]
            dv_ref[...] = cv[...]

    return pl.pallas_call(
        body, name=name, grid=(nb + 1,),
        in_specs=[pl.BlockSpec(memory_space=pltpu.SMEM), q_spec, q_spec, k_prev, k_cur, v_prev, v_cur, ANY],
        out_specs=(q_spec, late, late, pl.BlockSpec((1, LANES), lambda n: (0, 0))),
        out_shape=(jax.ShapeDtypeStruct(dqn.shape, dqn.dtype), jax.ShapeDtypeStruct((T, KV_HALF), F32),
                   jax.ShapeDtypeStruct((T, KV_HALF), F32), jax.ShapeDtypeStruct((1, LANES), F32)),
        scratch_shapes=[pltpu.VMEM((BLOCK, KV_HALF), F32), pltpu.VMEM((BLOCK, KV_HALF), F32)],
        input_output_aliases={7: 0}, compiler_params=_params("arbitrary"),
    )(sinks, qn, dcat, kn, kn, kv, kv, dqn)


def _mem_specs(M, tq, q_col):
    q = _cols(tq, MEM_WIDTH, q_col)
    k = pl.BlockSpec((M, MEM_WIDTH), lambda i: (0, 0))
    v = pl.BlockSpec((M, MEM_WIDTH), lambda i: (0, 1))
    return q, k, v


def _mem_attn_fwd(q, q_col, mkn, mkv, *, name):
    T = q.shape[0]
    M = mkn.shape[0]
    tq = min(T, MEM_Q_TILE)
    q_spec, k_spec, v_spec = _mem_specs(M, tq, q_col)

    def body(q_ref, k_ref, v_ref, o_ref):
        qq, kk, vv = q_ref[...], k_ref[...], v_ref[...].astype(BF16)
        outs = []
        for h in range(MEM_HEADS):
            p, _ps = _softmax(_head(qq, h), _head(kk, h), None, None, None)
            outs.append(jnp.dot(p.astype(BF16), _head(vv, h), preferred_element_type=F32))
        o_ref[...] = jnp.concatenate(outs, axis=1).astype(o_ref.dtype)

    return pl.pallas_call(
        body, name=name, grid=(T // tq,), in_specs=[q_spec, k_spec, v_spec], out_specs=_cols(tq, MEM_WIDTH, MAIN_WIDTH // MEM_WIDTH),
        out_shape=jax.ShapeDtypeStruct((T, D_MODEL), BF16), compiler_params=_params("parallel"),
    )(q, mkn, mkv)


def _mem_attn_bwd(q, q_col, mkn, mkv, dcat, *, dq_width, name):
    T = q.shape[0]
    M = mkn.shape[0]
    tq = min(T, MEM_Q_TILE)
    q_spec, k_spec, v_spec = _mem_specs(M, tq, q_col)
    last = MAIN_WIDTH // MEM_WIDTH
    tn_dims = (((0,), (0,)), ((), ()))

    def body(q_ref, do_ref, k_ref, v_ref, dq_ref, dk_ref, dv_ref):
        i = pl.program_id(0)
        qq, kk, vv, dout = q_ref[...], k_ref[...], v_ref[...].astype(BF16), do_ref[...].astype(BF16)
        dqs, dks, dvs = [], [], []
        for h in range(MEM_HEADS):
            qh, kh, vh, dh = _head(qq, h), _head(kk, h), _head(vv, h), _head(dout, h)
            p, _ps = _softmax(qh, kh, None, None, None)
            dp = lax.dot_general(dh, vh, (((1,), (1,)), ((), ())), preferred_element_type=F32)
            dsum = jnp.sum(p * dp, axis=-1, keepdims=True)
            ds = (p * (dp - dsum)).astype(BF16)
            dqs.append(jnp.dot(ds, kh, preferred_element_type=F32) * SCALE)
            dks.append(lax.dot_general(ds, qh, tn_dims, preferred_element_type=F32) * SCALE)
            dvs.append(lax.dot_general(p.astype(BF16), dh, tn_dims, preferred_element_type=F32))
        dq_ref[...] = jnp.concatenate(dqs, axis=1)
        dk = jnp.concatenate(dks, axis=1)
        dv = jnp.concatenate(dvs, axis=1)

        @pl.when(i == 0)
        def _():
            dk_ref[...] = dk
            dv_ref[...] = dv

        @pl.when(i > 0)
        def _():
            dk_ref[...] += dk
            dv_ref[...] += dv

    acc = pl.BlockSpec((M, MEM_WIDTH), lambda i: (0, 0))
    return pl.pallas_call(
        body, name=name, grid=(T // tq,), in_specs=[q_spec, _cols(tq, MEM_WIDTH, last), k_spec, v_spec],
        out_specs=(_cols(tq, MEM_WIDTH, dq_width // MEM_WIDTH - 1), acc, acc),
        out_shape=(jax.ShapeDtypeStruct((T, dq_width), F32), jax.ShapeDtypeStruct((M, MEM_WIDTH), F32),
                   jax.ShapeDtypeStruct((M, MEM_WIDTH), F32)),
        compiler_params=_params("arbitrary"),
    )(q, dcat, mkn, mkv)


def _loss(y, target, *, name):
    T, D = y.shape
    tr = _row_tile(T, D)

    def body(y_ref, t_ref, l_ref, dy_ref, dyb_ref):
        i = pl.program_id(0)
        err = y_ref[...] - t_ref[...]
        dy = err / float(D)
        dy_ref[...] = dy
        dyb_ref[...] = dy.astype(BF16)
        part = jnp.full((8, 128), 0.5 * jnp.sum(jnp.mean(err * err, axis=-1)), F32)

        @pl.when(i == 0)
        def _():
            l_ref[...] = part

        @pl.when(i > 0)
        def _():
            l_ref[...] += part

    row = pl.BlockSpec((tr, D), lambda i: (i, 0))
    return pl.pallas_call(
        body, name=name, grid=(T // tr,), in_specs=[row, row],
        out_specs=(pl.BlockSpec((8, 128), lambda i: (0, 0)), row, row),
        out_shape=(jax.ShapeDtypeStruct((8, 128), F32), jax.ShapeDtypeStruct((T, D), F32), jax.ShapeDtypeStruct((T, D), BF16)),
        compiler_params=_params("arbitrary"),
    )(y, target)


def _position():
    return lax.axis_index("x"), lax.axis_index("y"), lax.axis_index("c")


def _all_gather(arrays, *, name):
    n = len(arrays)

    def body(*refs):
        srcs, outs = refs[:n], refs[n:2 * n]
        token, send_sems, recv_sems, local_sems = refs[2 * n:]
        token[...] = jnp.zeros_like(token)
        x, y, c = _position()
        me, sibling = (x, y, c), (x, y, 1 - c)
        chips = [(1 - x, y), (x, 1 - y), (1 - x, 1 - y)]

        def slot(a, px, py, pc):
            return outs[a].at[4 * px + 2 * py + pc]

        def copy(a, k, block, to, src=None):
            return pltpu.make_async_remote_copy(
                src_ref=slot(a, *block) if src is None else src, dst_ref=slot(a, *block),
                send_sem=send_sems.at[a, k], recv_sem=recv_sems.at[a, k], device_id=to, device_id_type=MESH)

        mine = [pltpu.make_async_copy(srcs[a], slot(a, *me), local_sems.at[a]) for a in range(n)]
        for cp in mine:
            cp.start()
        first, passed = [], []
        for a in range(n):
            first.append(copy(a, 0, me, sibling, src=srcs[a]))
            first += [copy(a, 1 + j, me, (*chip, c), src=srcs[a]) for j, chip in enumerate(chips)]
        for cp in first:
            cp.start()
        for a in range(n):
            for j, chip in enumerate(chips):
                copy(a, 1 + j, (*chip, c), me).wait_recv()
                fwd = copy(a, 4 + j, (*chip, c), sibling)
                fwd.start()
                passed.append(fwd)
        for a in range(n):
            copy(a, 0, sibling, me).wait_recv()
            for j, chip in enumerate(chips):
                copy(a, 4 + j, (*chip, 1 - c), me).wait_recv()
        for cp in first + passed:
            cp.wait_send()
        for cp in mine:
            cp.wait()

    return pl.pallas_call(
        body, name=name, in_specs=[ANY] * n, out_specs=[ANY] * n + [pl.BlockSpec(memory_space=pltpu.VMEM)],
        out_shape=[jax.ShapeDtypeStruct((N_DEV,) + a.shape, a.dtype) for a in arrays] + [jax.ShapeDtypeStruct((8, 128), F32)],
        scratch_shapes=[pltpu.SemaphoreType.DMA((n, 7)), pltpu.SemaphoreType.DMA((n, 7)), pltpu.SemaphoreType.DMA((n,))],
    )(*arrays)


def _sibling_exchange(by_core, whole, *, name):
    n1, n = len(by_core), len(by_core) + len(whole)

    def body(*refs):
        srcs, outs = refs[:n], refs[n:2 * n]
        send_sems, recv_sems = refs[2 * n:]
        x, y, c = _position()
        copies = [
            pltpu.make_async_remote_copy(src_ref=srcs[a].at[:, 1 - c] if a < n1 else srcs[a], dst_ref=outs[a],
                                         send_sem=send_sems.at[a], recv_sem=recv_sems.at[a], device_id=(x, y, 1 - c),
                                         device_id_type=MESH)
            for a in range(n)]
        for cp in copies:
            cp.start()
        for cp in copies:
            cp.wait()

    out_shape = [jax.ShapeDtypeStruct(a.shape[:1] + a.shape[2:], a.dtype) for a in by_core]
    out_shape += [jax.ShapeDtypeStruct(a.shape, a.dtype) for a in whole]
    outs = pl.pallas_call(
        body, name=name, in_specs=[ANY] * n, out_specs=[ANY] * n, out_shape=out_shape,
        scratch_shapes=[pltpu.SemaphoreType.DMA((n,)), pltpu.SemaphoreType.DMA((n,))],
    )(*by_core, *whole)
    return outs[:n1], outs[n1:]


def _chip_exchange(per_chip, whole, *, name):
    n1, n = len(per_chip), len(per_chip) + len(whole)

    def body(*refs):
        srcs, outs = refs[:n], refs[n:2 * n]
        send_sems, recv_sems, local_sems = refs[2 * n:]
        x, y, c = _position()
        my_chip = 2 * x + y
        chips = [(1 - x, y), (x, 1 - y), (1 - x, 1 - y)]

        def src(a, chip):
            return srcs[a].at[chip] if a < n1 else srcs[a]

        local = [pltpu.make_async_copy(src(a, my_chip), outs[a].at[my_chip], local_sems.at[a]) for a in range(n)]
        for cp in local:
            cp.start()
        copies = [
            pltpu.make_async_remote_copy(src_ref=src(a, 2 * px + py), dst_ref=outs[a].at[my_chip],
                                         send_sem=send_sems.at[a, j], recv_sem=recv_sems.at[a, j], device_id=(px, py, c),
                                         device_id_type=MESH)
            for a in range(n) for j, (px, py) in enumerate(chips)]
        for cp in copies:
            cp.start()
        for cp in copies:
            cp.wait()
        for cp in local:
            cp.wait()

    out_shape = [jax.ShapeDtypeStruct(a.shape, a.dtype) for a in per_chip]
    out_shape += [jax.ShapeDtypeStruct((N_CHIP,) + a.shape, a.dtype) for a in whole]
    outs = pl.pallas_call(
        body, name=name, in_specs=[ANY] * n, out_specs=[ANY] * n, out_shape=out_shape,
        scratch_shapes=[pltpu.SemaphoreType.DMA((n, 3)), pltpu.SemaphoreType.DMA((n, 3)), pltpu.SemaphoreType.DMA((n,))],
    )(*per_chip, *whole)
    return outs[:n1], outs[n1:]


HBM = pl.BlockSpec(memory_space=pltpu.HBM)
SEM = pl.BlockSpec(memory_space=pltpu.SEMAPHORE)
DATAFLOW = pltpu.SideEffectType.DATAFLOW_SIDE_EFFECTING


def _device(flat):
    return flat // 4, (flat // 2) % 2, flat % 2


def _gather_copies(srcs, lands, send_sems, recv_sems, incoming):
    x, y, c = _position()
    me = 4 * x + 2 * y + c
    pairs = []
    for a in range(len(srcs)):
        for d in range(1, N_DEV):
            to, frm = (me + d) % N_DEV, (me + N_DEV - d) % N_DEV
            k = a * (N_DEV - 1) + d - 1
            sems = dict(send_sem=send_sems.at[k], recv_sem=recv_sems.at[k], device_id_type=MESH)
            out = pltpu.make_async_remote_copy(src_ref=srcs[a], dst_ref=lands[a].at[me], device_id=_device(to), **sems)
            inc = pltpu.make_async_remote_copy(src_ref=srcs[a], dst_ref=lands[a].at[frm], device_id=_device(frm),
                                               **sems) if incoming else None
            pairs.append((out, inc))
    return pairs


def _chip_copies(srcs, lands, send_sems, recv_sems, incoming):
    x, y, c = _position()
    my_chip = 2 * x + y
    pairs = []
    for a in range(len(srcs)):
        for k, (px, py) in enumerate([(1 - x, y), (x, 1 - y), (1 - x, 1 - y)]):
            sem = a * (N_CHIP - 1) + k
            sems = dict(send_sem=send_sems.at[sem], recv_sem=recv_sems.at[sem], device_id=(px, py, c), device_id_type=MESH)
            out = pltpu.make_async_remote_copy(src_ref=srcs[a].at[2 * px + py], dst_ref=lands[a].at[my_chip], **sems)
            inc = pltpu.make_async_remote_copy(src_ref=srcs[a].at[2 * px + py], dst_ref=lands[a].at[2 * px + py],
                                               **sems) if incoming else None
            pairs.append((out, inc))
    return pairs


def _push_start(copies, fan, srcs, lands, *, name):
    n = len(srcs)

    def body(*refs):
        src_refs, land_refs = refs[:n], refs[n:2 * n]
        send_sems, recv_sems = refs[2 * n], refs[2 * n + 1]
        token = refs[-1]
        for out, _ in copies(src_refs, land_refs, send_sems, recv_sems, False):
            out.start()
        token[...] = jnp.zeros_like(token)

    outs = pl.pallas_call(
        body, name=name,
        out_shape=(pltpu.SemaphoreType.DMA((n * fan,)), pltpu.SemaphoreType.DMA((n * fan,)),
                   *(pltpu.HBM(a.shape, a.dtype) for a in srcs), *(pltpu.HBM(a.shape, a.dtype) for a in lands),
                   jax.ShapeDtypeStruct((8, 128), F32)),
        in_specs=[HBM] * (2 * n), out_specs=(SEM, SEM, *([HBM] * (2 * n)), pl.BlockSpec(memory_space=pltpu.VMEM)),
        input_output_aliases={i: 2 + i for i in range(2 * n)},
        compiler_params=pltpu.CompilerParams(has_side_effects=DATAFLOW),
    )(*(pltpu.with_memory_space_constraint(a, pltpu.HBM) for a in (*srcs, *lands)))
    return outs[0], outs[1], list(outs[2:2 + n]), list(outs[2 + n:2 + 2 * n]), outs[-1]


def _push_wait(copies, send_sems, recv_sems, srcs, lands, after, *, name):
    n = len(srcs)

    def body(*refs):
        src_refs, land_refs = refs[:n], refs[n:2 * n]
        for out, inc in copies(src_refs, land_refs, refs[2 * n], refs[2 * n + 1], True):
            out.wait_send()
            inc.wait_recv()
        refs[-1][...] = jnp.zeros_like(refs[-1])

    outs = pl.pallas_call(
        body, name=name,
        out_shape=(*(pltpu.HBM(a.shape, a.dtype) for a in (*srcs, *lands)), jax.ShapeDtypeStruct((8, 128), F32)),
        in_specs=[HBM] * (2 * n) + [SEM, SEM, ANY], out_specs=(*([HBM] * (2 * n)), pl.BlockSpec(memory_space=pltpu.VMEM)),
        input_output_aliases={i: i for i in range(2 * n)},
        compiler_params=pltpu.CompilerParams(has_side_effects=DATAFLOW),
    )(*srcs, *lands, send_sems, recv_sems, after)
    return list(outs[n:2 * n]), outs[-1]


def _with_own_slot(block, index, slots):
    buf = lax.empty((slots,) + block.shape, block.dtype)
    return lax.dynamic_update_slice(buf, block[None], (index,) + (0,) * block.ndim)


def _view2d(shape):
    return math.prod(shape[:-1]), shape[-1]


def _pair_sum(mine, other, core, *, name, out_dtype):
    by_core = mine.ndim == 4
    n, w = other.shape[-2:]
    tr = _row_tile(n, w * 2)
    lead = other.shape[0] if by_core else 1

    def body(core_ref, a_ref, b_ref, o_ref):
        o_ref[...] = (a_ref[...].astype(F32) + b_ref[...].astype(F32)).astype(o_ref.dtype)

    if by_core:
        a_spec = pl.BlockSpec((None, None, tr, w), lambda j, i, core_ref: (j, core_ref[0], i, 0))
        o_spec = pl.BlockSpec((None, tr, w), lambda j, i, core_ref: (j, i, 0))
    else:
        a_spec = o_spec = pl.BlockSpec((tr, w), lambda j, i, core_ref: (i, 0))
    grid_spec = pltpu.PrefetchScalarGridSpec(num_scalar_prefetch=1, grid=(lead, n // tr), in_specs=[a_spec, o_spec],
                                             out_specs=o_spec)
    return pl.pallas_call(body, name=name, grid_spec=grid_spec, out_shape=jax.ShapeDtypeStruct(other.shape, out_dtype),
                          compiler_params=_params("parallel", "parallel"))(core.reshape(1), mine, other)


def _adamw(parts, w, m, v, *, name):
    layers = len(parts)
    n_parts, R, W = parts[0].shape
    tr = _row_tile(R, W * 2)
    per_layer = R // tr

    def update(p_ref, w_ref, m_ref, v_ref, g_out, d_out, m_out, v_out):
        g = p_ref[0].astype(F32)
        for j in range(1, n_parts):
            g = g + p_ref[j].astype(F32)
        m_new = ADAM_B1 * m_ref[...] + (1.0 - ADAM_B1) * g
        v_new = ADAM_B2 * v_ref[...] + (1.0 - ADAM_B2) * (g * g)
        m_hat = m_new / (1.0 - ADAM_B1 ** ADAM_STEP)
        v_hat = v_new / (1.0 - ADAM_B2 ** ADAM_STEP)
        g_out[...] = g
        d_out[...] = -ADAM_LR * (m_hat / (jnp.sqrt(v_hat) + ADAM_EPS) + ADAM_WD * w_ref[...])
        m_out[...] = m_new
        v_out[...] = v_new

    def body(*refs):
        for k in range(layers):
            pl.when(pl.program_id(0) == k)(lambda k=k: update(refs[k], *refs[layers:]))

    def parts_spec(k):
        return pl.BlockSpec((n_parts, tr, W), lambda l, i: (0, jnp.where(l == k, i, 0), 0))

    row = pl.BlockSpec((tr, W), lambda l, i: (l * per_layer + i, 0))
    out = jax.ShapeDtypeStruct((layers * R, W), F32)
    return pl.pallas_call(
        body, name=name, grid=(layers, per_layer), in_specs=[parts_spec(k) for k in range(layers)] + [row, row, row],
        out_specs=(row, row, row, row), out_shape=(out, out, out, out), compiler_params=_params("arbitrary", "arbitrary"),
    )(*parts, w, m, v)


SMALL_ROWS = 608


def _pack_small(p):
    flat = jnp.concatenate([p[n].reshape(-1).astype(F32) for n in SMALL_NAMES])
    return jnp.pad(flat, (0, SMALL_ROWS * PACK_W - flat.shape[0])).reshape(SMALL_ROWS, PACK_W)


def _unpack_small(buf, like):
    out, at = {}, 0
    flat = buf.reshape(-1)
    for n in SMALL_NAMES:
        size = math.prod(like[n].shape)
        out[n] = flat[at:at + size].reshape(like[n].shape)
        at += size
    return out


def _heads(a, nh):
    T = a.shape[0]
    return a.reshape(T, nh, HEAD_DIM).transpose(1, 0, 2).reshape(nh * T, HEAD_DIM)


def _unheads(a, nh):
    a = a.reshape(nh, -1, HEAD_DIM)
    return a.transpose(1, 0, 2).reshape(a.shape[1], nh * HEAD_DIM)


def _groups(a):
    T = a.shape[0]
    return a.reshape(T, POOL_GROUPS, POOL_GROUP_DIM).transpose(1, 0, 2)


def _ungroups(a):
    return a.transpose(1, 0, 2).reshape(a.shape[1], MAIN_WIDTH)


def _local_step(x, mem, target, p, w_kv, fetch, reduce_layer, reduce_wait):
    T = x.shape[0]
    M = mem.shape[0]
    saved = []
    h = x
    kn = vv = k_raw = h_kv = hn_kv = None
    for l in range(DEPTH):
        s = {}
        wl, token = fetch(l, h)
        s["w"] = wl
        if l == N_A:
            h_kv = h
            hn_kv = _rms_fwd(h, p["kv_norm"], name="kv_norm_fwd")
            kv = _mm(hn_kv, w_kv, b_kind="rows", name="kv_proj")
            k_raw = _heads(kv[:, :KV_HALF], SWA_KV_HEADS)
            kn = _rms_fwd(k_raw, p["k_norm"], name="k_norm_fwd").reshape(SWA_KV_HEADS, T, HEAD_DIM)
            vv = _heads(kv[:, KV_HALF:], SWA_KV_HEADS).astype(BF16).reshape(SWA_KV_HEADS, T, HEAD_DIM)
        s["h"] = h
        s["xn1"] = _rms_fwd(h, p["norm_mix"][l] + token, name="norm_mix_fwd")
        proj = _mm(s["xn1"], wl["w_in"], b_kind="rows", name="in_proj")
        s["mq_raw"] = _heads(proj[:, MAIN_WIDTH:], MEM_HEADS)
        s["mqn"] = _rms_fwd(s["mq_raw"], p["mem_q_norm"][l], name="mem_q_norm_fwd").reshape(MEM_HEADS, T, HEAD_DIM)
        s["memn"] = _rms_fwd(mem, p["mem_norm"][l], name="mem_norm_fwd")
        mkv = _mm(s["memn"], wl["w_mem_kv"], b_kind="rows", name="mem_kv_proj")
        s["mk_raw"] = _heads(mkv[:, :MEM_WIDTH], MEM_HEADS)
        s["mkn"] = _rms_fwd(s["mk_raw"], p["mem_k_norm"][l], name="mem_k_norm_fwd").reshape(MEM_HEADS, M, HEAD_DIM)
        s["mvv"] = _heads(mkv[:, MEM_WIDTH:], MEM_HEADS).astype(BF16).reshape(MEM_HEADS, M, HEAD_DIM)
        mem_out = _unheads(_mem_fwd(s["mqn"], s["mkn"], s["mvv"], name="mem_attn_fwd"), MEM_HEADS)
        if l < N_A:
            s["u"] = _groups(proj[:, :MAIN_WIDTH])
            s["pw"] = p["pool_w"][l]
            s["ps"] = p["pool_scale"][l].reshape(POOL_GROUPS, 1, POOL_GROUP_DIM)
            main_out = _ungroups(_pool_fwd(s["u"], s["pw"], s["ps"], name="pool_fwd"))
        else:
            j = l - N_A
            s["q_raw"] = _heads(proj[:, :MAIN_WIDTH], SWA_Q_HEADS)
            s["qn"] = _rms_fwd(s["q_raw"], p["q_norm"][j], name="q_norm_fwd").reshape(SWA_KV_HEADS, SWA_GROUP, T, HEAD_DIM)
            main_out = _unheads(_swa_fwd(s["qn"], kn, vv, p["sinks"][j], name="swa_fwd"), SWA_Q_HEADS)
        s["cat"] = jnp.concatenate([main_out, mem_out], axis=-1)
        s["h1"] = _mm(s["cat"], wl["w_out"], b_kind="rows", res=h, name="out_proj")
        s["xn2"] = _rms_fwd(s["h1"], p["norm_mlp"][l], name="norm_mlp_fwd")
        s["r"], s["a"] = _mm(s["xn2"], wl["w_up"], b_kind="layers", relu2=True, name="mlp_up")
        h = _mm(s["a"], wl["w_down"], b_kind="rows", res=s["h1"], name="mlp_down")
        saved.append(s)

    loss, dh, dh_b = _loss(h, target, name="loss_head")

    g = {n: [None] * DEPTH for n in ("norm_mix", "mem_norm", "mem_q_norm", "mem_k_norm", "norm_mlp")}
    g_kv = None
    token = None
    g.update({n: [None] * N_A for n in ("pool_w", "pool_scale", "q_norm", "sinks")})
    dkn = dvv = None
    for l in reversed(range(DEPTH)):
        s = saved[l]
        wl = s["w"]
        gb = {}

        def dw(a, dy, n):
            return _mm(a, dy, ta=True, out_kind="layers" if n == "w_up" else "rows", out_buf=lax.empty(wl[n].shape, BF16),
                       name=n + "_grad")

        norm_mlp_gain = p["norm_mlp"][l] if token is None else p["norm_mlp"][l] + token
        gb["w_down"] = dw(s["a"], dh_b, "w_down")
        du = _mm(dh_b, wl["w_down"], tb=True, b_kind="rows", mul2=s["a"], out_dtype=BF16, name="mlp_down_dx")
        gb["w_up"] = dw(s["xn2"], du, "w_up")
        dxn2 = _mm(du, wl["w_up"], tb=True, b_kind="layers", name="mlp_up_dx")
        dh1, dh1_b, g["norm_mlp"][l] = _rms_bwd(s["h1"], norm_mlp_gain, [dxn2], res=dh, also_bf16=True,
                                                name="norm_mlp_bwd")
        gb["w_out"] = dw(s["cat"], dh1_b, "w_out")
        dcat = _mm(dh1_b, wl["w_out"], tb=True, b_kind="rows", name="out_proj_dx")
        dmem_out = _heads(dcat[:, MAIN_WIDTH:], MEM_HEADS).astype(BF16).reshape(MEM_HEADS, T, HEAD_DIM)
        dmqn, dmkn, dmvv = _mem_bwd(s["mqn"], s["mkn"], s["mvv"], dmem_out, name="mem_attn_bwd")
        dmq_raw, g["mem_q_norm"][l] = _rms_bwd(s["mq_raw"], p["mem_q_norm"][l], [dmqn.reshape(MEM_HEADS * T, HEAD_DIM)],
                                               name="mem_q_norm_bwd")
        dmk_raw, g["mem_k_norm"][l] = _rms_bwd(s["mk_raw"], p["mem_k_norm"][l], [dmkn.reshape(MEM_HEADS * M, HEAD_DIM)],
                                               name="mem_k_norm_bwd")
        dmkv = jnp.concatenate([_unheads(dmk_raw, MEM_HEADS), _unheads(dmvv, MEM_HEADS)], axis=-1).astype(BF16)
        gb["w_mem_kv"] = dw(s["memn"], dmkv, "w_mem_kv")
        dmemn = _mm(dmkv, wl["w_mem_kv"], tb=True, b_kind="rows", name="mem_kv_proj_dx")
        g["mem_norm"][l] = _rms_bwd(mem, p["mem_norm"][l], [dmemn], want_dx=False, name="mem_norm_bwd")
        if l < N_A:
            dmain_out = _groups(dcat[:, :MAIN_WIDTH])
            du_pool, g["pool_w"][l], dps = _pool_bwd(s["u"], s["pw"], s["ps"], dmain_out, name="pool_bwd")
            g["pool_scale"][l] = dps.reshape(MAIN_WIDTH)
            dmain = _ungroups(du_pool)
        else:
            j = l - N_A
            dmain_out = _heads(dcat[:, :MAIN_WIDTH], SWA_Q_HEADS).astype(BF16).reshape(SWA_KV_HEADS, SWA_GROUP, T, HEAD_DIM)
            dqn, dk_l, dv_l, dsink = _swa_bwd(s["qn"], kn, vv, p["sinks"][j], dmain_out, name="swa_bwd")
            g["sinks"][j] = dsink[:, 0, :SWA_GROUP].reshape(SWA_Q_HEADS)
            dq_raw, g["q_norm"][j] = _rms_bwd(s["q_raw"], p["q_norm"][j], [dqn.reshape(SWA_Q_HEADS * T, HEAD_DIM)],
                                              name="q_norm_bwd")
            dmain = _unheads(dq_raw, SWA_Q_HEADS)
            dk_l = dk_l.reshape(SWA_KV_HEADS * T, HEAD_DIM)
            dv_l = dv_l.reshape(SWA_KV_HEADS * T, HEAD_DIM)
            dkn = dk_l if dkn is None else _add(dkn, dk_l, name="dk_sum")
            dvv = dv_l if dvv is None else _add(dvv, dv_l, name="dv_sum")
        dproj = jnp.concatenate([dmain, _unheads(dmq_raw, MEM_HEADS)], axis=-1).astype(BF16)
        gb["w_in"] = dw(s["xn1"], dproj, "w_in")
        dxn1 = _mm(dproj, wl["w_in"], tb=True, b_kind="rows", name="in_proj_dx")
        if l in (0, N_A):
            dh, g["norm_mix"][l] = _rms_bwd(s["h"], p["norm_mix"][l], [dxn1], res=dh1, name="norm_mix_bwd")
        else:
            dh, dh_b, g["norm_mix"][l] = _rms_bwd(s["h"], p["norm_mix"][l], [dxn1], res=dh1, also_bf16=True,
                                                  name="norm_mix_bwd")
        if l == N_A:
            dk_raw, g["k_norm"] = _rms_bwd(k_raw, p["k_norm"], [dkn], name="k_norm_bwd")
            dkv = jnp.concatenate([_unheads(dk_raw, SWA_KV_HEADS), _unheads(dvv, SWA_KV_HEADS)], axis=-1).astype(BF16)
            g_kv = _mm(hn_kv, dkv, ta=True, out_kind="rows", out_buf=lax.empty(w_kv.shape, BF16), name="w_kv_grad")
            dhn = _mm(dkv, w_kv, tb=True, b_kind="rows", name="kv_proj_dx")
            dh, dh_b, g["kv_norm"] = _rms_bwd(h_kv, p["kv_norm"], [dhn], res=dh, also_bf16=True, name="kv_norm_bwd")
        if l + 1 < DEPTH:
            reduce_wait(l + 1, dh)
        token = reduce_layer(l, gb)
    grads = {n: (jnp.stack(v) if isinstance(v, list) else v) for n, v in g.items()}
    return loss, dh, grads, g_kv


def _block_diag(pw):
    out = jnp.zeros((MAIN_WIDTH, MAIN_WIDTH), pw.dtype)
    for g in range(POOL_GROUPS):
        out = lax.dynamic_update_slice(out, pw[g], (g * POOL_GROUP_DIM, g * POOL_GROUP_DIM))
    return out


def _diag_blocks(m):
    return jnp.stack([m[g * POOL_GROUP_DIM:(g + 1) * POOL_GROUP_DIM, g * POOL_GROUP_DIM:(g + 1) * POOL_GROUP_DIM]
                      for g in range(POOL_GROUPS)])


def _train_pass(x, mem, target, p, w_kv, fetch, reduce_layer, reduce_wait):
    T = x.shape[0]
    mem_cols = MAIN_WIDTH // MEM_WIDTH
    k_gain = _head_gain(p["k_norm"], SWA_KV_HEADS)
    saved = []
    h = x
    kn = kv = h_kv = hn_kv = None
    for l in range(DEPTH):
        s = {}
        wl, token = fetch(l, h)
        s["w"] = wl
        if l == N_A:
            h_kv = h
            hn_kv = _rms_fwd(h, p["kv_norm"], name="kv_norm_fwd")
            kv = _mm(hn_kv, w_kv, b_kind="rows", name="kv_proj")
            kn = _seg_rms_fwd(kv, k_gain, width=KV_HALF, col=0, name="k_norm_fwd")
        s["h"] = h
        s["xn1"] = _rms_fwd(h, p["norm_mix"][l] + token, name="norm_mix_fwd")
        s["proj"] = proj = _mm(s["xn1"], wl["w_in"], b_kind="rows", name="in_proj")
        s["memn"] = _rms_fwd(mem, p["mem_norm"][l], name="mem_norm_fwd")
        s["mkv"] = _mm(s["memn"], wl["w_mem_kv"], b_kind="rows", name="mem_kv_proj")
        s["mk_gain"] = _head_gain(p["mem_k_norm"][l], MEM_HEADS)
        s["mkn"] = _seg_rms_fwd(s["mkv"], s["mk_gain"], width=MEM_WIDTH, col=0, name="mem_k_norm_fwd")
        if l < N_A:
            s["q_gain"] = _head_gain(p["mem_q_norm"][l], MEM_HEADS)
            s["qn"] = _seg_rms_fwd(proj, s["q_gain"], width=MEM_WIDTH, col=mem_cols, name="mem_q_norm_fwd")
            s["q_col"] = 0
        else:
            j = l - N_A
            s["q_gain"] = jnp.concatenate([_head_gain(p["q_norm"][j], SWA_Q_HEADS), _head_gain(p["mem_q_norm"][l], MEM_HEADS)],
                                          axis=1)
            s["qn"] = _seg_rms_fwd(proj, s["q_gain"], width=D_MODEL, col=0, name="q_norm_fwd")
            s["q_col"] = mem_cols
        cat = _mem_attn_fwd(s["qn"], s["q_col"], s["mkn"], s["mkv"], name="mem_attn_fwd")
        if l < N_A:
            s["mix"] = _block_diag(p["pool_w"][l])
            s["scale"] = p["pool_scale"][l].reshape(1, MAIN_WIDTH)
            s["cat"] = _pool_mix_fwd(proj, s["mix"], s["scale"], cat, name="pool_fwd")
        else:
            s["cat"] = _swa_attn_fwd(s["qn"], kn, kv, p["sinks"][l - N_A], cat, name="swa_fwd")
        s["h1"] = _mm(s["cat"], wl["w_out"], b_kind="rows", res=h, name="out_proj")
        s["xn2"] = _rms_fwd(s["h1"], p["norm_mlp"][l], name="norm_mlp_fwd")
        s["a"] = _mm(s["xn2"], wl["w_up"], b_kind="layers", relu2=True, name="mlp_up")
        h = _mm(s["a"], wl["w_down"], b_kind="rows", res=s["h1"], name="mlp_down")
        saved.append(s)

    loss, dh, dh_b = _loss(h, target, name="loss_head")

    g = {n: [None] * DEPTH for n in ("norm_mix", "mem_norm", "mem_q_norm", "mem_k_norm", "norm_mlp")}
    g.update({n: [None] * N_A for n in ("pool_w", "pool_scale", "q_norm", "sinks")})
    g_kv = None
    token = None
    dks, dvs = [], []
    for l in reversed(range(DEPTH)):
        s = saved[l]
        wl = s["w"]
        gb = {}

        def dw(a, dy, n):
            return _mm(a, dy, ta=True, out_kind="layers" if n == "w_up" else "rows", out_buf=lax.empty(wl[n].shape, BF16),
                       name=n + "_grad")

        norm_mlp_gain = p["norm_mlp"][l] if token is None else p["norm_mlp"][l] + token
        gb["w_down"] = dw(s["a"], dh_b, "w_down")
        du = _mm(dh_b, wl["w_down"], tb=True, b_kind="rows", mul2=s["a"], out_dtype=BF16, name="mlp_down_dx")
        gb["w_up"] = dw(s["xn2"], du, "w_up")
        dxn2 = _mm(du, wl["w_up"], tb=True, b_kind="layers", name="mlp_up_dx")
        early = reduce_layer(l, gb, early=True)
        if early is not None:
            norm_mlp_gain = norm_mlp_gain + early
        dh1, dh1_b, g["norm_mlp"][l] = _rms_bwd(s["h1"], norm_mlp_gain, [dxn2], res=dh, also_bf16=True,
                                                name="norm_mlp_bwd")
        gb["w_out"] = dw(s["cat"], dh1_b, "w_out")
        dcat = _mm(dh1_b, wl["w_out"], tb=True, b_kind="rows", name="out_proj_dx")
        if l < N_A:
            dq, dmk, dmv = _mem_attn_bwd(s["qn"], s["q_col"], s["mkn"], s["mkv"], dcat, dq_width=MEM_WIDTH, name="mem_attn_bwd")
            dproj, dmix, dscale = _pool_mix_bwd(s["proj"], s["mix"], s["scale"], dcat, name="pool_bwd")
            g["pool_w"][l] = _diag_blocks(dmix)
            g["pool_scale"][l] = dscale.reshape(MAIN_WIDTH)
            dproj, dgain = _seg_rms_bwd(s["proj"], s["q_gain"], [dq], width=MEM_WIDTH, col=mem_cols, out_buf=dproj,
                                        out_col=mem_cols, name="mem_q_norm_bwd")
            g["mem_q_norm"][l] = _fold_heads(dgain, MEM_HEADS)
        else:
            j = l - N_A
            dqn, dmk, dmv = _mem_attn_bwd(s["qn"], s["q_col"], s["mkn"], s["mkv"], dcat, dq_width=D_MODEL, name="mem_attn_bwd")
            dqn, dk_l, dv_l, dsinks = _swa_attn_bwd(s["qn"], kn, kv, p["sinks"][j], dcat, dqn, name="swa_bwd")
            dks.append(dk_l)
            dvs.append(dv_l)
            g["sinks"][j] = dsinks[0, :SWA_Q_HEADS]
            dproj, dgain = _seg_rms_bwd(s["proj"], s["q_gain"], [dqn], width=D_MODEL, col=0, name="q_norm_bwd")
            g["q_norm"][j] = _fold_heads(dgain[:, :MAIN_WIDTH], SWA_Q_HEADS)
            g["mem_q_norm"][l] = _fold_heads(dgain[:, MAIN_WIDTH:], MEM_HEADS)
        dmk_raw, dgain = _seg_rms_bwd(s["mkv"], s["mk_gain"], [dmk], width=MEM_WIDTH, col=0, name="mem_k_norm_bwd")
        g["mem_k_norm"][l] = _fold_heads(dgain, MEM_HEADS)
        dmkv = jnp.concatenate([dmk_raw, dmv.astype(BF16)], axis=1)
        gb["w_mem_kv"] = dw(s["memn"], dmkv, "w_mem_kv")
        dmemn = _mm(dmkv, wl["w_mem_kv"], tb=True, b_kind="rows", name="mem_kv_proj_dx")
        g["mem_norm"][l] = _rms_bwd(mem, p["mem_norm"][l], [dmemn], want_dx=False, name="mem_norm_bwd")
        gb["w_in"] = dw(s["xn1"], dproj, "w_in")
        dxn1 = _mm(dproj, wl["w_in"], tb=True, b_kind="rows", name="in_proj_dx")
        if l in (0, N_A):
            dh, g["norm_mix"][l] = _rms_bwd(s["h"], p["norm_mix"][l], [dxn1], res=dh1, name="norm_mix_bwd")
        else:
            dh, dh_b, g["norm_mix"][l] = _rms_bwd(s["h"], p["norm_mix"][l], [dxn1], res=dh1, also_bf16=True,
                                                  name="norm_mix_bwd")
        if l == N_A:
            dkv, dgain = _seg_rms_bwd(kv, k_gain, dks, width=KV_HALF, col=0, out_buf=lax.empty((T, 2 * KV_HALF), BF16),
                                      name="k_norm_bwd")
            g["k_norm"] = _fold_heads(dgain, SWA_KV_HEADS)
            dkv = _sum_into(dvs[0], dvs[1], dkv, 1, name="dv_sum")
            g_kv = _mm(hn_kv, dkv, ta=True, out_kind="rows", out_buf=lax.empty(w_kv.shape, BF16), name="w_kv_grad")
            dhn = _mm(dkv, w_kv, tb=True, b_kind="rows", name="kv_proj_dx")
            dh, dh_b, g["kv_norm"] = _rms_bwd(h_kv, p["kv_norm"], [dhn], res=dh, also_bf16=True, name="kv_norm_bwd")
        if l + 1 < DEPTH:
            reduce_wait(l + 1, dh)
        token = reduce_layer(l, gb)
    grads = {n: (jnp.stack(v) if isinstance(v, list) else v) for n, v in g.items()}
    return loss, dh, grads, g_kv


def kernel(x, mem, norm_mix, w_in, pool_w, pool_scale, kv_norm, w_kv, k_norm, q_norm, sinks, mem_norm, w_mem_kv, mem_q_norm, mem_k_norm, w_out, norm_mlp, w_up, w_down, loss_target, m_norm_mix, m_w_in, m_pool_w, m_pool_scale, m_kv_norm, m_w_kv, m_k_norm, m_q_norm, m_sinks, m_mem_norm, m_w_mem_kv, m_mem_q_norm, m_mem_k_norm, m_w_out, m_norm_mlp, m_w_up, m_w_down, v_norm_mix, v_w_in, v_pool_w, v_pool_scale, v_kv_norm, v_w_kv, v_k_norm, v_q_norm, v_sinks, v_mem_norm, v_w_mem_kv, v_mem_q_norm, v_mem_k_norm, v_w_out, v_norm_mlp, v_w_up, v_w_down):
    weights = dict(norm_mix=norm_mix, w_in=w_in, pool_w=pool_w, pool_scale=pool_scale, kv_norm=kv_norm, w_kv=w_kv,
                   k_norm=k_norm, q_norm=q_norm, sinks=sinks, mem_norm=mem_norm, w_mem_kv=w_mem_kv,
                   mem_q_norm=mem_q_norm, mem_k_norm=mem_k_norm, w_out=w_out, norm_mlp=norm_mlp, w_up=w_up, w_down=w_down)
    mom1 = dict(norm_mix=m_norm_mix, w_in=m_w_in, pool_w=m_pool_w, pool_scale=m_pool_scale, kv_norm=m_kv_norm, w_kv=m_w_kv,
                k_norm=m_k_norm, q_norm=m_q_norm, sinks=m_sinks, mem_norm=m_mem_norm, w_mem_kv=m_w_mem_kv,
                mem_q_norm=m_mem_q_norm, mem_k_norm=m_mem_k_norm, w_out=m_w_out, norm_mlp=m_norm_mlp, w_up=m_w_up,
                w_down=m_w_down)
    mom2 = dict(norm_mix=v_norm_mix, w_in=v_w_in, pool_w=v_pool_w, pool_scale=v_pool_scale, kv_norm=v_kv_norm, w_kv=v_w_kv,
                k_norm=v_k_norm, q_norm=v_q_norm, sinks=v_sinks, mem_norm=v_mem_norm, w_mem_kv=v_w_mem_kv,
                mem_q_norm=v_mem_q_norm, mem_k_norm=v_mem_k_norm, w_out=v_w_out, norm_mlp=v_norm_mlp, w_up=v_w_up,
                w_down=v_w_down)
    names = list(weights)
    x_pos, y_pos, core = (lax.axis_index(n).astype(jnp.int32) for n in AXES)
    me, my_chip = 4 * x_pos + 2 * y_pos + core, 2 * x_pos + y_pos
    shard = MAIN_WIDTH // N_DEV

    def layer_shards(l, zero=0.0):
        return [(weights[n][l:l + 1] + zero).astype(BF16) for n in LAYERED]

    def usable(arrays):
        wl = dict(zip(LAYERED, arrays))
        wl["w_up"] = wl["w_up"].transpose(1, 2, 0, 3).reshape(1, D_MODEL, D_FF)
        return wl

    scale_block = jnp.pad(pool_scale, ((0, 8 - N_A), (0, 128 - shard)))
    *first, first_done = _all_gather(layer_shards(0) + [w_kv[None].astype(BF16), scale_block], name="gather_first")
    p = {n: weights[n] for n in SMALL_NAMES}
    p["pool_scale"] = first[-1][:, :N_A, :shard].transpose(1, 0, 2).reshape(N_A, MAIN_WIDTH)
    gathers, reduces, parts = {}, {}, {}

    def fetch(l, after):
        if l == 0:
            got, done = first[:len(LAYERED)], first_done
        else:
            got, done = _push_wait(_gather_copies, *gathers.pop(l), after, name=f"gather_wait_{l}")
        token = 0.0
        if l + 1 < DEPTH:
            srcs = layer_shards(l + 1, done[0, 0])
            *handles, block = _push_start(_gather_copies, N_DEV - 1, srcs, [_with_own_slot(a, me, N_DEV) for a in srcs],
                                          name=f"gather_start_{l + 1}")
            gathers[l + 1], token = handles, block[0, 0]
        return usable(got), token

    def by_core(gb):
        gb = dict(gb)
        if "w_up" in gb:
            gb["w_up"] = gb["w_up"].reshape(D_MODEL, N_DEV, D_FF // N_DEV).transpose(1, 0, 2)
        order = [n for n in LAYERED if n in gb] + [n for n in gb if n not in LAYERED]
        return {n: gb[n].reshape((N_CHIP, 2) + _view2d(gb[n].shape[1:] if n == "w_up" else gb[n].shape[2:])) for n in order}

    def chip_sums(gb, tag, whole=()):
        views = by_core(gb)
        sib, sib_whole = _sibling_exchange(list(views.values()), list(whole), name="reduce_sibling_" + tag)
        sums = [_pair_sum(a, b, core, name=f"chip_sum_{n}_{tag}", out_dtype=BF16) for (n, a), b in zip(views.items(), sib)]
        return sums, sib_whole

    mlp = ("w_up", "w_down")

    def reduce_layer(l, gb, early=False):
        if early and l > 0:
            return None
        if l == 0 and not early:
            reduces["rest"] = {n: a for n, a in gb.items() if n not in mlp}
            return None
        tag = "0_mlp" if early else str(l)
        sums, _ = chip_sums({n: gb[n] for n in mlp} if early else gb, tag)
        lands = [_with_own_slot(lax.dynamic_index_in_dim(a, my_chip, 0, keepdims=False), my_chip, N_CHIP) for a in sums]
        *handles, block = _push_start(_chip_copies, N_CHIP - 1, sums, lands, name="reduce_start_" + tag)
        reduces[l] = handles
        return block[0, 0]

    def reduce_wait(l, after):
        parts[l], _ = _push_wait(_chip_copies, *reduces.pop(l), after, name=f"reduce_wait_{l}")

    loss, grad_x, grads, g_kv = _train_pass(x[0], mem[0], loss_target[0], p, first[len(LAYERED)], fetch, reduce_layer, reduce_wait)

    last = dict(reduces.pop("rest"))
    last["w_kv"] = g_kv
    last["pool_scale"] = grads["pool_scale"].reshape(N_A, N_DEV, shard).transpose(1, 0, 2).astype(BF16)[:, None]
    small = _pack_small(grads)
    sums, (sib_small,) = chip_sums(last, "0", whole=[small])
    chip_small = _pair_sum(small, sib_small, core, name="chip_sum_small", out_dtype=F32)
    (p_in, p_mem_kv, p_out, parts_kv, parts_scale), (parts_small,) = _chip_exchange(sums, [chip_small], name="reduce_chips_0")
    (p_up, p_down), _ = _push_wait(_chip_copies, *reduces.pop(0), parts_small, name="reduce_wait_0_mlp")
    parts[0] = [p_in, p_mem_kv, p_out, p_up, p_down]

    def adamw(n, n_parts):
        res = _adamw(n_parts, *(d[n].reshape(_view2d(d[n].shape)) for d in (weights, mom1, mom2)), name="adamw_" + n)
        return [r.reshape(weights[n].shape) for r in res]

    new = {n: adamw(n, [parts[l][k] for l in range(DEPTH)]) for k, n in enumerate(LAYERED)}
    new["w_kv"] = adamw("w_kv", [parts_kv])
    new["pool_scale"] = adamw("pool_scale", [parts_scale])
    res = _adamw([parts_small], _pack_small(weights), _pack_small(mom1), _pack_small(mom2), name="adamw_replicated")
    for n, vals in zip(SMALL_NAMES, zip(*(_unpack_small(r, weights).values() for r in res))):
        new[n] = list(vals)
    outs = [new[n][k] for k in range(4) for n in names]
    total = lax.psum(loss[0, 0], AXES)
    return (total, grad_x[None], *outs)
```

```python
import math

import jax
import jax.numpy as jnp
from jax import lax
from jax.experimental import pallas as pl
from jax.experimental.pallas import tpu as pltpu

F32 = jnp.float32
BF16 = jnp.bfloat16
MESH = pl.DeviceIdType.MESH
AXES = ("x", "y", "c")

D_MODEL = 1024
DEPTH = 4
N_A = 2
HEAD_DIM = 64
MEM_HEADS = 4
MEM_WIDTH = MEM_HEADS * HEAD_DIM
MAIN_WIDTH = D_MODEL - MEM_WIDTH
POOL_GROUPS = 4
POOL_GROUP_DIM = MAIN_WIDTH // POOL_GROUPS
POOL_HALO = 16
SWA_Q_HEADS = MAIN_WIDTH // HEAD_DIM
SWA_KV_HEADS = 4
SWA_GROUP = SWA_Q_HEADS // SWA_KV_HEADS
KV_HALF = SWA_KV_HEADS * HEAD_DIM
BLOCK = 128
D_FF = 4 * D_MODEL
EPS = 1e-6
SCALE = HEAD_DIM ** -0.5
NEG = float(jnp.finfo(jnp.float32).min)
N_DEV = 8
N_CHIP = 4

ADAM_LR = 0.001
ADAM_B1 = 0.9
ADAM_B2 = 0.999
ADAM_EPS = 1e-08
ADAM_WD = 0.01
ADAM_STEP = 10

PACK_W = 512
VMEM_LIMIT = 52 * 1024 * 1024
MM_TILE = 1024

LAYERED = ("w_in", "w_mem_kv", "w_out", "w_up", "w_down")
SMALL_NAMES = ("norm_mix", "pool_w", "kv_norm", "k_norm", "q_norm", "sinks", "mem_norm", "mem_q_norm", "mem_k_norm",
               "norm_mlp")


ANY = pl.BlockSpec(memory_space=pl.ANY)


def _params(*sem):
    return pltpu.CompilerParams(dimension_semantics=sem, vmem_limit_bytes=VMEM_LIMIT)


def _mm(a, b, *, name, ta=False, tb=False, b_kind=None, layer=0, res=None, relu2=False, mul2=None, out_dtype=F32,
        out_kind=None, out_buf=None):
    if ta:
        K, M = a.shape
    else:
        M, K = a.shape
    if b_kind is None:
        rows_b, cols_b = b.shape
    elif b_kind == "rows":
        rows_b, cols_b = b.shape[0] * b.shape[2], b.shape[3]
    else:
        rows_b, cols_b = b.shape[1:]
    N, K2 = (rows_b, cols_b) if tb else (cols_b, rows_b)
    assert K == K2, (a.shape, b.shape)
    tm = min(M, MM_TILE if K <= MM_TILE else MM_TILE // 2)
    tn = min(N, MM_TILE)
    assert M % tm == 0 and N % tn == 0
    row_tile, col_tile = (tn, K) if tb else (K, tn)
    a_spec = pl.BlockSpec((K, tm), lambda j, i: (0, i)) if ta else pl.BlockSpec((tm, K), lambda j, i: (i, 0))

    def rc(j):
        return (j, 0) if tb else (0, j)

    if b_kind is None:
        b_spec = pl.BlockSpec((row_tile, col_tile), lambda j, i: rc(j))
    elif b_kind == "rows":
        per = row_tile // b.shape[2]
        b_spec = pl.BlockSpec((per, None, b.shape[2], col_tile), lambda j, i: (rc(j)[0], layer, 0, rc(j)[1]))
    else:
        b_spec = pl.BlockSpec((None, row_tile, col_tile), lambda j, i: (layer, *rc(j)))
    o_spec = pl.BlockSpec((tm, tn), lambda j, i: (i, j))
    dn = (((0 if ta else 1,), (1 if tb else 0,)), ((), ()))
    extra = [e for e in (res, mul2) if e is not None]
    n_in = 2 + len(extra) + (1 if out_buf is not None else 0)

    def body(*refs):
        a_ref, b_ref = refs[0], refs[1]
        extra_refs = refs[2:2 + len(extra)]
        out = refs[n_in]
        bv = b_ref[...].astype(BF16).reshape(row_tile, col_tile)
        v = lax.dot_general(a_ref[...].astype(BF16), bv, dn, preferred_element_type=F32)
        if res is not None:
            v = extra_refs[0][...] + v
        elif mul2 is not None:
            v = v * (2.0 * jnp.sqrt(extra_refs[0][...].astype(F32)))
        if relu2:
            r = jnp.maximum(v, 0.0)
            v = r * r
        out[...] = v.astype(out.dtype).reshape(out.shape)

    in_specs = [a_spec, b_spec] + [o_spec] * len(extra)
    operands = [a, b, *extra]
    aliases = {}
    if out_kind is None:
        out_shape = jax.ShapeDtypeStruct((M, N), BF16 if relu2 else out_dtype)
        out_specs = o_spec
    else:
        if out_kind == "rows":
            s = out_buf.shape[2]
            out_specs = pl.BlockSpec((tm // s, None, s, tn), lambda j, i: (i, layer, 0, j))
        else:
            out_specs = pl.BlockSpec((None, tm, tn), lambda j, i: (layer, i, j))
        out_shape = jax.ShapeDtypeStruct(out_buf.shape, out_buf.dtype)
        in_specs.append(ANY)
        operands.append(out_buf)
        aliases = {len(operands) - 1: 0}
    return pl.pallas_call(
        body, name=name, grid=(N // tn, M // tm), in_specs=in_specs, out_specs=out_specs, out_shape=out_shape,
        input_output_aliases=aliases, compiler_params=_params("parallel", "parallel"),
    )(*operands)


def _weight_block(b, b_kind, transposed, tn):
    if b_kind == "rows":
        s = b.shape[2]
        rows, cols = b.shape[0] * s, b.shape[3]
        if transposed:
            return (lambda at: pl.BlockSpec((b.shape[0], None, s, cols), lambda *g: (0, 0, 0, 0))), rows, cols
        return (lambda at: pl.BlockSpec((b.shape[0], None, s, tn), lambda *g: (0, 0, 0, at(*g)))), rows, cols
    rows, cols = b.shape[1:]
    if transposed:
        return (lambda at: pl.BlockSpec((None, rows, cols), lambda *g: (0, 0, 0))), rows, cols
    return (lambda at: pl.BlockSpec((None, rows, tn), lambda *g: (0, 0, at(*g)))), rows, cols


def _norm_mm(x, gain, b, *, b_kind, name, relu2=False):
    M, K = x.shape
    tm = min(M, MM_TILE)
    spec_of, rows, N = _weight_block(b, b_kind, False, min(MM_TILE, b.shape[-1]))
    tn = min(N, MM_TILE)
    assert rows == K and M % tm == 0 and N % tn == 0

    def body(x_ref, g_ref, b_ref, xn_ref, o_ref):
        @pl.when(pl.program_id(1) == 0)
        def _():
            xv = x_ref[...]
            r = lax.rsqrt(jnp.mean(xv * xv, axis=-1, keepdims=True) + EPS)
            xn_ref[...] = ((xv * r) * g_ref[...]).astype(xn_ref.dtype)

        v = jnp.dot(xn_ref[...], b_ref[...].astype(BF16).reshape(K, tn), preferred_element_type=F32)
        if relu2:
            r2 = jnp.maximum(v, 0.0)
            v = r2 * r2
        o_ref[...] = v.astype(o_ref.dtype)

    rows_spec = pl.BlockSpec((tm, K), lambda i, j: (i, 0))
    return pl.pallas_call(
        body, name=name, grid=(M // tm, N // tn),
        in_specs=[rows_spec, pl.BlockSpec((1, K), lambda i, j: (0, 0)), spec_of(lambda i, j: j)],
        out_specs=(rows_spec, pl.BlockSpec((tm, tn), lambda i, j: (i, j))),
        out_shape=(jax.ShapeDtypeStruct((M, K), BF16), jax.ShapeDtypeStruct((M, N), BF16 if relu2 else F32)),
        compiler_params=_params("parallel", "arbitrary"),
    )(x, gain.reshape(1, K), b)


def _mm_rms_bwd(a, b, x, gain, res, *, b_kind, name, also_bf16):
    M, K = a.shape
    spec_of, N, cols = _weight_block(b, b_kind, True, None)
    assert cols == K and x.shape == (M, N)
    tm = min(M, MM_TILE if K <= MM_TILE else MM_TILE // 2)
    assert M % tm == 0

    def body(a_ref, b_ref, x_ref, g_ref, res_ref, *outs):
        i = pl.program_id(0)
        dy = lax.dot_general(a_ref[...].astype(BF16), b_ref[...].astype(BF16).reshape(N, K), (((1,), (1,)), ((), ())),
                             preferred_element_type=F32)
        xv = x_ref[...]
        r = lax.rsqrt(jnp.mean(xv * xv, axis=-1, keepdims=True) + EPS)
        xh = xv * r
        part = jnp.sum(dy * xh, axis=0, keepdims=True)
        dg_ref = outs[-1]

        @pl.when(i == 0)
        def _():
            dg_ref[...] = part

        @pl.when(i > 0)
        def _():
            dg_ref[...] += part

        gdy = dy * g_ref[...]
        dx = res_ref[...] + r * (gdy - xh * jnp.mean(gdy * xh, axis=-1, keepdims=True))
        outs[0][...] = dx
        if also_bf16:
            outs[1][...] = dx.astype(BF16)

    row = pl.BlockSpec((tm, N), lambda i: (i, 0))
    vec = pl.BlockSpec((1, N), lambda i: (0, 0))
    out_specs = [row] + ([row] if also_bf16 else []) + [vec]
    out_shape = [jax.ShapeDtypeStruct((M, N), F32)] + ([jax.ShapeDtypeStruct((M, N), BF16)] if also_bf16 else [])
    outs = pl.pallas_call(
        body, name=name, grid=(M // tm,),
        in_specs=[pl.BlockSpec((tm, K), lambda i: (i, 0)), spec_of(None), row, vec, row], out_specs=out_specs,
        out_shape=out_shape + [jax.ShapeDtypeStruct((1, N), F32)], compiler_params=_params("arbitrary"),
    )(a, b, x, gain.reshape(1, N), res)
    return (*outs[:-1], outs[-1].reshape(N))


def _row_tile(rows, d):
    t = min(rows, (512 * 1024) // d)
    while rows % t or (t != rows and t % 16):
        t -= 1
    return t


def _rms_fwd(x, g, *, name, out_dtype=BF16):
    R, D = x.shape
    tr = _row_tile(R, D)

    def body(x_ref, g_ref, o_ref):
        xv = x_ref[...].astype(F32)
        r = lax.rsqrt(jnp.mean(xv * xv, axis=-1, keepdims=True) + EPS)
        o_ref[...] = ((xv * r) * g_ref[...]).astype(o_ref.dtype)

    return pl.pallas_call(
        body, name=name, grid=(R // tr,),
        in_specs=[pl.BlockSpec((tr, D), lambda i: (i, 0)), pl.BlockSpec((1, D), lambda i: (0, 0))],
        out_specs=pl.BlockSpec((tr, D), lambda i: (i, 0)), out_shape=jax.ShapeDtypeStruct((R, D), out_dtype),
        compiler_params=_params("parallel"),
    )(x, g.reshape(1, D))


def _rms_bwd(x, g, dys, *, name, res=None, want_dx=True, also_bf16=False):
    R, D = x.shape
    tr = _row_tile(R, D)
    n_dy = len(dys)
    has_res = res is not None

    def body(*refs):
        x_ref, g_ref = refs[0], refs[1]
        dy_refs = refs[2:2 + n_dy]
        res_ref = refs[2 + n_dy] if has_res else None
        outs = refs[2 + n_dy + (1 if has_res else 0):]
        dg_ref = outs[-1]
        i = pl.program_id(0)
        xv = x_ref[...].astype(F32)
        dy = dy_refs[0][...].astype(F32)
        for extra in dy_refs[1:]:
            dy = dy + extra[...].astype(F32)
        r = lax.rsqrt(jnp.mean(xv * xv, axis=-1, keepdims=True) + EPS)
        xh = xv * r
        part = jnp.sum(dy * xh, axis=0, keepdims=True)

        @pl.when(i == 0)
        def _():
            dg_ref[...] = part

        @pl.when(i > 0)
        def _():
            dg_ref[...] += part

        if want_dx:
            gdy = dy * g_ref[...]
            dx = r * (gdy - xh * jnp.mean(gdy * xh, axis=-1, keepdims=True))
            if has_res:
                dx = res_ref[...] + dx
            outs[0][...] = dx
            if also_bf16:
                outs[1][...] = dx.astype(BF16)

    row = pl.BlockSpec((tr, D), lambda i: (i, 0))
    vec = pl.BlockSpec((1, D), lambda i: (0, 0))
    out_shape = [jax.ShapeDtypeStruct((1, D), F32)]
    out_specs = [vec]
    if also_bf16:
        out_shape = [jax.ShapeDtypeStruct((R, D), BF16)] + out_shape
        out_specs = [row] + out_specs
    if want_dx:
        out_shape = [jax.ShapeDtypeStruct((R, D), F32)] + out_shape
        out_specs = [row] + out_specs
    outs = pl.pallas_call(
        body, name=name, grid=(R // tr,),
        in_specs=[row, vec] + [row] * (n_dy + (1 if has_res else 0)), out_specs=out_specs, out_shape=out_shape,
        compiler_params=_params("arbitrary"),
    )(x, g.reshape(1, D), *dys, *([res] if has_res else []))
    return (*outs[:-1], outs[-1].reshape(D)) if want_dx else outs[0].reshape(D)


def _add(a, b, *, name):
    R, D = a.shape
    tr = _row_tile(R, D)

    def body(a_ref, b_ref, o_ref):
        o_ref[...] = a_ref[...] + b_ref[...]

    row = pl.BlockSpec((tr, D), lambda i: (i, 0))
    return pl.pallas_call(body, name=name, grid=(R // tr,), in_specs=[row, row], out_specs=row,
                          out_shape=jax.ShapeDtypeStruct((R, D), a.dtype), compiler_params=_params("parallel"))(a, b)


POOL_TILE = 512


def _pool_window(group):
    return lax.shift_left(jnp.int32(2), group)


def _pool_diff(u_ref, halo_ref, group, tile):
    first = tile == 0
    halo = jnp.where(first, 0.0, halo_ref[...])
    ext = jnp.concatenate([halo, u_ref[...]], axis=0)
    n = ext.shape[0]
    s1 = ext + pltpu.roll(ext, 1, 0)
    s2 = s1 + pltpu.roll(s1, 2, 0)
    s3 = s2 + pltpu.roll(s2, 4, 0)
    s4 = s3 + pltpu.roll(s3, 8, 0)
    ws = jnp.where(group == 0, s1, jnp.where(group == 1, s2, jnp.where(group == 2, s3, s4)))[POOL_HALO:n]
    t = tile * POOL_TILE + lax.broadcasted_iota(jnp.int32, (POOL_TILE, 1), 0)
    cnt = jnp.minimum(t + 1, _pool_window(group)).astype(F32)
    return ws / cnt - u_ref[...], cnt


def _pool_specs():
    per_tile = POOL_TILE // POOL_HALO
    cur = pl.BlockSpec((None, POOL_TILE, POOL_GROUP_DIM), lambda g, i: (g, i, 0))
    prev = pl.BlockSpec((None, POOL_HALO, POOL_GROUP_DIM), lambda g, i: (g, jnp.maximum(i * per_tile - 1, 0), 0))
    pw = pl.BlockSpec((None, POOL_GROUP_DIM, POOL_GROUP_DIM), lambda g, i: (g, 0, 0))
    vec = pl.BlockSpec((None, 1, POOL_GROUP_DIM), lambda g, i: (g, 0, 0))
    return cur, prev, pw, vec


def _pool_fwd(u, pw, scale, *, name):
    G, T, C = u.shape
    assert T % POOL_TILE == 0
    cur, prev, pw_spec, vec = _pool_specs()

    def body(u_ref, halo_ref, pw_ref, sc_ref, o_ref):
        d, _ = _pool_diff(u_ref, halo_ref, pl.program_id(0), pl.program_id(1))
        mixed = jnp.dot(d.astype(BF16), pw_ref[...].astype(BF16), preferred_element_type=F32)
        o_ref[...] = (mixed * sc_ref[...]).astype(o_ref.dtype)

    return pl.pallas_call(
        body, name=name, grid=(G, T // POOL_TILE), in_specs=[cur, prev, pw_spec, vec], out_specs=cur,
        out_shape=jax.ShapeDtypeStruct((G, T, C), BF16), compiler_params=_params("parallel", "parallel"),
    )(u, u, pw, scale)


def _pool_bwd(u, pw, scale, dout, *, name):
    G, T, C = u.shape
    nt = T // POOL_TILE
    per_tile = POOL_TILE // POOL_HALO
    cur, prev, pw_spec, vec = _pool_specs()
    nxt = pl.BlockSpec((None, POOL_HALO, C), lambda g, i: (g, jnp.minimum((i + 1) * per_tile, nt * per_tile - 1), 0))

    def body(u_ref, halo_ref, pw_ref, sc_ref, do_ref, donext_ref, du_ref, dpw_ref, dsc_ref):
        group, tile = pl.program_id(0), pl.program_id(1)
        d, cnt = _pool_diff(u_ref, halo_ref, group, tile)
        pwb = pw_ref[...].astype(BF16)
        db = d.astype(BF16)
        mixed = jnp.dot(db, pwb, preferred_element_type=F32)
        dout = do_ref[...].astype(F32)
        dsc = jnp.sum(dout * mixed, axis=0, keepdims=True)
        sc = sc_ref[...]
        dmix = (dout * sc).astype(BF16)
        dpw = lax.dot_general(db, dmix, (((0,), (0,)), ((), ())), preferred_element_type=F32)

        @pl.when(tile == 0)
        def _():
            dpw_ref[...] = dpw
            dsc_ref[...] = dsc

        @pl.when(tile > 0)
        def _():
            dpw_ref[...] += dpw
            dsc_ref[...] += dsc

        last = tile == nt - 1
        dnext = jnp.where(last, 0.0, donext_ref[...].astype(F32))
        dmix_ext = jnp.concatenate([dmix, (dnext * sc).astype(BF16)], axis=0)
        dd_ext = lax.dot_general(dmix_ext, pwb, (((1,), (1,)), ((), ())), preferred_element_type=F32)
        window = _pool_window(group).astype(F32)
        cnt_ext = jnp.concatenate([cnt, jnp.broadcast_to(window, (POOL_HALO, 1))], axis=0)
        q = dd_ext / cnt_ext
        n = q.shape[0]
        r1 = q + pltpu.roll(q, n - 1, 0)
        r2 = r1 + pltpu.roll(r1, n - 2, 0)
        r3 = r2 + pltpu.roll(r2, n - 4, 0)
        r4 = r3 + pltpu.roll(r3, n - 8, 0)
        back = jnp.where(group == 0, r1, jnp.where(group == 1, r2, jnp.where(group == 2, r3, r4)))
        du_ref[...] = back[0:POOL_TILE] - dd_ext[0:POOL_TILE]

    return pl.pallas_call(
        body, name=name, grid=(G, nt), in_specs=[cur, prev, pw_spec, vec, cur, nxt],
        out_specs=(cur, pw_spec, vec),
        out_shape=(jax.ShapeDtypeStruct((G, T, C), F32), jax.ShapeDtypeStruct((G, C, C), F32),
                   jax.ShapeDtypeStruct((G, 1, C), F32)),
        compiler_params=_params("arbitrary", "arbitrary"),
    )(u, u, pw, scale, dout, dout)


def _softmax(q, k, bias, valid, sink):
    s = lax.dot_general(q, k, (((1,), (1,)), ((), ())), preferred_element_type=F32) * SCALE
    if bias is not None:
        s = s - bias
    if valid is not None:
        s = jnp.where(valid, s, NEG)
    m = jnp.max(s, axis=-1, keepdims=True)
    if sink is not None:
        m = jnp.maximum(m, sink)
    e = jnp.exp(s - m)
    z = jnp.sum(e, axis=-1, keepdims=True)
    if sink is None:
        return e * (1.0 / z), None
    es = jnp.exp(sink - m)
    inv = 1.0 / (z + es)
    return e * inv, es * inv


def _swa_terms(sink_ref, kvh, blk):
    rows = SWA_GROUP * BLOCK
    row = lax.broadcasted_iota(jnp.int32, (rows, 1), 0)
    grp = row // BLOCK
    head = (kvh * SWA_GROUP + grp + 1).astype(F32)
    slope = jnp.exp(head * (-8.0 * math.log(2.0) / SWA_Q_HEADS))
    qi = lax.broadcasted_iota(jnp.int32, (rows, 2 * BLOCK), 0) % BLOCK
    kj = lax.broadcasted_iota(jnp.int32, (rows, 2 * BLOCK), 1)
    dist = qi + BLOCK - kj
    valid = (dist >= 0) & (dist < BLOCK) & ((blk > 0) | (kj >= BLOCK))
    bias = slope * dist.astype(F32)
    s0, s1, s2 = (sink_ref[kvh * SWA_GROUP + g] for g in range(SWA_GROUP))
    sink = jnp.where(grp == 0, s0, jnp.where(grp == 1, s1, s2))
    return bias, valid, sink, grp


def _swa_fwd(q, k, v, sinks, *, name):
    H, G, T, hd = q.shape
    nb = T // BLOCK
    rows = G * BLOCK

    def body(sink_ref, q_ref, kp_ref, kc_ref, vp_ref, vc_ref, o_ref):
        kvh, blk = pl.program_id(0), pl.program_id(1)
        bias, valid, sink, _ = _swa_terms(sink_ref, kvh, blk)
        kk = jnp.concatenate([kp_ref[...], kc_ref[...]], axis=0)
        vv = jnp.concatenate([vp_ref[...], vc_ref[...]], axis=0)
        p, _ = _softmax(q_ref[...].reshape(rows, hd), kk, bias, valid, sink)
        o = jnp.dot(p.astype(BF16), vv, preferred_element_type=F32)
        o_ref[...] = o.reshape(G, BLOCK, hd).astype(o_ref.dtype)

    qs = pl.BlockSpec((None, G, BLOCK, hd), lambda h, n: (h, 0, n, 0))
    prev = pl.BlockSpec((None, BLOCK, hd), lambda h, n: (h, jnp.maximum(n - 1, 0), 0))
    cur = pl.BlockSpec((None, BLOCK, hd), lambda h, n: (h, n, 0))
    return pl.pallas_call(
        body, name=name, grid=(H, nb),
        in_specs=[pl.BlockSpec(memory_space=pltpu.SMEM), qs, prev, cur, prev, cur], out_specs=qs,
        out_shape=jax.ShapeDtypeStruct((H, G, T, hd), BF16), compiler_params=_params("parallel", "parallel"),
    )(sinks, q, k, k, v, v)


def _swa_bwd(q, k, v, sinks, do, *, name):
    H, G, T, hd = q.shape
    nb = T // BLOCK
    rows = G * BLOCK

    def body(sink_ref, q_ref, do_ref, kp_ref, kc_ref, vp_ref, vc_ref, dq_ref, dk_ref, dv_ref, ds_ref, ck, cv):
        kvh, blk = pl.program_id(0), pl.program_id(1)

        @pl.when(blk == 0)
        def _():
            ck[...] = jnp.zeros_like(ck)
            cv[...] = jnp.zeros_like(cv)
            ds_ref[...] = jnp.zeros_like(ds_ref)

        @pl.when(blk < nb)
        def _():
            bias, valid, sink, grp = _swa_terms(sink_ref, kvh, blk)
            kk = jnp.concatenate([kp_ref[...], kc_ref[...]], axis=0)
            vv = jnp.concatenate([vp_ref[...], vc_ref[...]], axis=0)
            qq = q_ref[...].reshape(rows, hd)
            dout = do_ref[...].reshape(rows, hd)
            p, ps = _softmax(qq, kk, bias, valid, sink)
            dp = lax.dot_general(dout, vv, (((1,), (1,)), ((), ())), preferred_element_type=F32)
            dsum = jnp.sum(p * dp, axis=-1, keepdims=True)
            ds = (p * (dp - dsum)).astype(BF16)
            dq = jnp.dot(ds, kk, preferred_element_type=F32) * SCALE
            dq_ref[...] = dq.reshape(G, BLOCK, hd)
            dk = lax.dot_general(ds, qq, (((0,), (0,)), ((), ())), preferred_element_type=F32) * SCALE
            dv = lax.dot_general(p.astype(BF16), dout, (((0,), (0,)), ((), ())), preferred_element_type=F32)
            dk_ref[...] = ck[...] + dk[0:BLOCK]
            dv_ref[...] = cv[...] + dv[0:BLOCK]
            ck[...] = dk[BLOCK:2 * BLOCK]
            cv[...] = dv[BLOCK:2 * BLOCK]
            dsink = -(ps * dsum)
            lane = lax.broadcasted_iota(jnp.int32, (1, 128), 1)
            acc = jnp.zeros((1, 128), F32)
            for g in range(G):
                acc = acc + jnp.where(lane == g, jnp.sum(jnp.where(grp == g, dsink, 0.0)), 0.0)
            ds_ref[...] += acc

        @pl.when(blk == nb)
        def _():
            dk_ref[...] = ck[...]
            dv_ref[...] = cv[...]

    def at(n):
        return jnp.minimum(n, nb - 1)

    qs = pl.BlockSpec((None, G, BLOCK, hd), lambda h, n: (h, 0, at(n), 0))
    prev = pl.BlockSpec((None, BLOCK, hd), lambda h, n: (h, jnp.maximum(at(n) - 1, 0), 0))
    cur = pl.BlockSpec((None, BLOCK, hd), lambda h, n: (h, at(n), 0))
    late = pl.BlockSpec((None, BLOCK, hd), lambda h, n: (h, jnp.maximum(n - 1, 0), 0))
    dsink_spec = pl.BlockSpec((None, 1, 128), lambda h, n: (h, 0, 0))
    return pl.pallas_call(
        body, name=name, grid=(H, nb + 1),
        in_specs=[pl.BlockSpec(memory_space=pltpu.SMEM), qs, qs, prev, cur, prev, cur],
        out_specs=(qs, late, late, dsink_spec),
        out_shape=(jax.ShapeDtypeStruct((H, G, T, hd), F32), jax.ShapeDtypeStruct((H, T, hd), F32),
                   jax.ShapeDtypeStruct((H, T, hd), F32), jax.ShapeDtypeStruct((H, 1, 128), F32)),
        scratch_shapes=[pltpu.VMEM((BLOCK, hd), F32), pltpu.VMEM((BLOCK, hd), F32)],
        compiler_params=_params("arbitrary", "arbitrary"),
    )(sinks, q, do, k, k, v, v)


MEM_Q_TILE = 512


def _mem_fwd(q, k, v, *, name):
    H, T, hd = q.shape
    M = k.shape[1]
    tq = min(T, MEM_Q_TILE)

    def body(q_ref, k_ref, v_ref, o_ref):
        p, _ = _softmax(q_ref[...], k_ref[...], None, None, None)
        o_ref[...] = jnp.dot(p.astype(BF16), v_ref[...], preferred_element_type=F32).astype(o_ref.dtype)

    qs = pl.BlockSpec((None, tq, hd), lambda h, i: (h, i, 0))
    ks = pl.BlockSpec((None, M, hd), lambda h, i: (h, 0, 0))
    return pl.pallas_call(body, name=name, grid=(H, T // tq), in_specs=[qs, ks, ks], out_specs=qs,
                          out_shape=jax.ShapeDtypeStruct((H, T, hd), BF16),
                          compiler_params=_params("parallel", "parallel"))(q, k, v)


def _mem_bwd(q, k, v, do, *, name):
    H, T, hd = q.shape
    M = k.shape[1]
    tq = min(T, MEM_Q_TILE)

    def body(q_ref, do_ref, k_ref, v_ref, dq_ref, dk_ref, dv_ref):
        i = pl.program_id(1)
        qq, kk, vv, dout = q_ref[...], k_ref[...], v_ref[...], do_ref[...]
        p, _ = _softmax(qq, kk, None, None, None)
        dp = lax.dot_general(dout, vv, (((1,), (1,)), ((), ())), preferred_element_type=F32)
        dsum = jnp.sum(p * dp, axis=-1, keepdims=True)
        ds = (p * (dp - dsum)).astype(BF16)
        dq_ref[...] = jnp.dot(ds, kk, preferred_element_type=F32) * SCALE
        dk = lax.dot_general(ds, qq, (((0,), (0,)), ((), ())), preferred_element_type=F32) * SCALE
        dv = lax.dot_general(p.astype(BF16), dout, (((0,), (0,)), ((), ())), preferred_element_type=F32)

        @pl.when(i == 0)
        def _():
            dk_ref[...] = dk
            dv_ref[...] = dv

        @pl.when(i > 0)
        def _():
            dk_ref[...] += dk
            dv_ref[...] += dv

    qs = pl.BlockSpec((None, tq, hd), lambda h, i: (h, i, 0))
    ks = pl.BlockSpec((None, M, hd), lambda h, i: (h, 0, 0))
    return pl.pallas_call(
        body, name=name, grid=(H, T // tq), in_specs=[qs, qs, ks, ks], out_specs=(qs, ks, ks),
        out_shape=(jax.ShapeDtypeStruct((H, T, hd), F32), jax.ShapeDtypeStruct((H, M, hd), F32),
                   jax.ShapeDtypeStruct((H, M, hd), F32)),
        compiler_params=_params("arbitrary", "arbitrary"),
    )(q, do, k, v)


LANES = 128


def _seg_mean(v):
    r = lax.broadcasted_iota(jnp.int32, (LANES, LANES), 0) // HEAD_DIM
    c = lax.broadcasted_iota(jnp.int32, (LANES, LANES), 1) // HEAD_DIM
    seg = jnp.where(r == c, 1.0 / HEAD_DIM, 0.0).astype(BF16)
    hi = v.astype(BF16)
    lo = (v - hi.astype(F32)).astype(BF16)
    parts = []
    for g in range(v.shape[1] // LANES):
        sl = slice(g * LANES, (g + 1) * LANES)
        parts.append(jnp.dot(hi[:, sl], seg, preferred_element_type=F32) + jnp.dot(lo[:, sl], seg, preferred_element_type=F32))
    return parts[0] if len(parts) == 1 else jnp.concatenate(parts, axis=1)


def _cols(rows, width, col):
    return pl.BlockSpec((rows, width), lambda i: (i, col))


def _head_gain(g, heads):
    return jnp.tile(g, heads).reshape(1, heads * HEAD_DIM)


def _fold_heads(dg, heads):
    return dg.reshape(heads, HEAD_DIM).sum(axis=0)


def _seg_rms_fwd(x, gain, *, width, col, name):
    R = x.shape[0]
    tr = _row_tile(R, width)

    def body(x_ref, g_ref, o_ref):
        xv = x_ref[...]
        r = lax.rsqrt(_seg_mean(xv * xv) + EPS)
        o_ref[...] = ((xv * r) * g_ref[...]).astype(o_ref.dtype)

    return pl.pallas_call(
        body, name=name, grid=(R // tr,), in_specs=[_cols(tr, width, col), pl.BlockSpec((1, width), lambda i: (0, 0))],
        out_specs=_cols(tr, width, 0), out_shape=jax.ShapeDtypeStruct((R, width), BF16), compiler_params=_params("parallel"),
    )(x, gain)


def _seg_rms_bwd(x, gain, dys, *, width, col, name, out_buf=None, out_col=0):
    R = x.shape[0]
    tr = _row_tile(R, width)
    n_dy = len(dys)

    def body(*refs):
        x_ref, g_ref = refs[0], refs[1]
        dy_refs = refs[2:2 + n_dy]
        dx_ref, dg_ref = refs[-2], refs[-1]
        i = pl.program_id(0)
        xv = x_ref[...]
        dy = dy_refs[0][...]
        for extra in dy_refs[1:]:
            dy = dy + extra[...]
        r = lax.rsqrt(_seg_mean(xv * xv) + EPS)
        xh = xv * r
        part = jnp.sum(dy * xh, axis=0, keepdims=True)

        @pl.when(i == 0)
        def _():
            dg_ref[...] = part

        @pl.when(i > 0)
        def _():
            dg_ref[...] += part

        gdy = dy * g_ref[...]
        dx_ref[...] = (r * (gdy - xh * _seg_mean(gdy * xh))).astype(dx_ref.dtype)

    vec = pl.BlockSpec((1, width), lambda i: (0, 0))
    in_specs = [_cols(tr, width, col), vec] + [_cols(tr, width, 0)] * n_dy
    operands = [x, gain, *dys]
    aliases = {}
    dx_shape = jax.ShapeDtypeStruct((R, width), BF16)
    if out_buf is not None:
        in_specs.append(ANY)
        operands.append(out_buf)
        aliases = {len(operands) - 1: 0}
        dx_shape = jax.ShapeDtypeStruct(out_buf.shape, out_buf.dtype)
    return pl.pallas_call(
        body, name=name, grid=(R // tr,), in_specs=in_specs, out_specs=(_cols(tr, width, out_col), vec),
        out_shape=(dx_shape, jax.ShapeDtypeStruct((1, width), F32)), input_output_aliases=aliases,
        compiler_params=_params("arbitrary"),
    )(*operands)


def _sum_into(a, b, out_buf, out_col, *, name):
    R, width = a.shape
    tr = _row_tile(R, width)

    def body(a_ref, b_ref, _, o_ref):
        o_ref[...] = (a_ref[...] + b_ref[...]).astype(o_ref.dtype)

    return pl.pallas_call(
        body, name=name, grid=(R // tr,), in_specs=[_cols(tr, width, 0), _cols(tr, width, 0), ANY],
        out_specs=_cols(tr, width, out_col), out_shape=jax.ShapeDtypeStruct(out_buf.shape, out_buf.dtype),
        input_output_aliases={2: 0}, compiler_params=_params("parallel"),
    )(a, b, out_buf)


def _pool_lane_group():
    return lax.broadcasted_iota(jnp.int32, (1, MAIN_WIDTH), 1) // POOL_GROUP_DIM


def _pool_pick(group, per_window):
    s1, s2, s3, s4 = per_window
    return jnp.where(group == 0, s1, jnp.where(group == 1, s2, jnp.where(group == 2, s3, s4)))


def _pool_delta(u_ref, halo_ref, tile):
    group = _pool_lane_group()
    halo = jnp.where(tile == 0, 0.0, halo_ref[...])
    ext = jnp.concatenate([halo, u_ref[...]], axis=0)
    n = ext.shape[0]
    s1 = ext + pltpu.roll(ext, 1, 0)
    s2 = s1 + pltpu.roll(s1, 2, 0)
    s3 = s2 + pltpu.roll(s2, 4, 0)
    s4 = s3 + pltpu.roll(s3, 8, 0)
    ws = _pool_pick(group, (s1, s2, s3, s4))[POOL_HALO:n]
    t = tile * POOL_TILE + lax.broadcasted_iota(jnp.int32, (POOL_TILE, 1), 0)
    cnt = jnp.minimum(t + 1, _pool_pick(group, (2, 4, 8, 16))).astype(F32)
    return ws / cnt - u_ref[...], cnt


def _pool_in_specs():
    per_tile = POOL_TILE // POOL_HALO
    cur = _cols(POOL_TILE, MAIN_WIDTH, 0)
    prev = pl.BlockSpec((POOL_HALO, MAIN_WIDTH), lambda i: (jnp.maximum(i * per_tile - 1, 0), 0))
    mix = pl.BlockSpec((MAIN_WIDTH, MAIN_WIDTH), lambda i: (0, 0))
    vec = pl.BlockSpec((1, MAIN_WIDTH), lambda i: (0, 0))
    return cur, prev, mix, vec


def _pool_mix_fwd(proj, mix, scale, cat, *, name):
    T = proj.shape[0]
    assert T % POOL_TILE == 0
    cur, prev, mix_spec, vec = _pool_in_specs()

    def body(u_ref, halo_ref, mix_ref, sc_ref, _, o_ref):
        d, _cnt = _pool_delta(u_ref, halo_ref, pl.program_id(0))
        mixed = jnp.dot(d.astype(BF16), mix_ref[...].astype(BF16), preferred_element_type=F32)
        o_ref[...] = (mixed * sc_ref[...]).astype(o_ref.dtype)

    return pl.pallas_call(
        body, name=name, grid=(T // POOL_TILE,), in_specs=[cur, prev, mix_spec, vec, ANY], out_specs=cur,
        out_shape=jax.ShapeDtypeStruct(cat.shape, cat.dtype), input_output_aliases={4: 0}, compiler_params=_params("parallel"),
    )(proj, proj, mix, scale, cat)


def _pool_mix_bwd(proj, mix, scale, dcat, *, name):
    T = proj.shape[0]
    nt = T // POOL_TILE
    per_tile = POOL_TILE // POOL_HALO
    cur, prev, mix_spec, vec = _pool_in_specs()
    nxt = pl.BlockSpec((POOL_HALO, MAIN_WIDTH), lambda i: (jnp.minimum((i + 1) * per_tile, nt * per_tile - 1), 0))

    def body(u_ref, halo_ref, mix_ref, sc_ref, do_ref, donext_ref, du_ref, dmix_ref, dsc_ref):
        tile = pl.program_id(0)
        group = _pool_lane_group()
        d, cnt = _pool_delta(u_ref, halo_ref, tile)
        mixb = mix_ref[...].astype(BF16)
        db = d.astype(BF16)
        mixed = jnp.dot(db, mixb, preferred_element_type=F32)
        dout = do_ref[...]
        dsc = jnp.sum(dout * mixed, axis=0, keepdims=True)
        sc = sc_ref[...]
        dmixed = (dout * sc).astype(BF16)
        dmix = lax.dot_general(db, dmixed, (((0,), (0,)), ((), ())), preferred_element_type=F32)

        @pl.when(tile == 0)
        def _():
            dmix_ref[...] = dmix
            dsc_ref[...] = dsc

        @pl.when(tile > 0)
        def _():
            dmix_ref[...] += dmix
            dsc_ref[...] += dsc

        dnext = jnp.where(tile == nt - 1, 0.0, donext_ref[...])
        dmixed_ext = jnp.concatenate([dmixed, (dnext * sc).astype(BF16)], axis=0)
        dd_ext = lax.dot_general(dmixed_ext, mixb, (((1,), (1,)), ((), ())), preferred_element_type=F32)
        window = _pool_pick(group, (2.0, 4.0, 8.0, 16.0))
        cnt_ext = jnp.concatenate([cnt, jnp.broadcast_to(window, (POOL_HALO, MAIN_WIDTH))], axis=0)
        q = dd_ext / cnt_ext
        n = q.shape[0]
        r1 = q + pltpu.roll(q, n - 1, 0)
        r2 = r1 + pltpu.roll(r1, n - 2, 0)
        r3 = r2 + pltpu.roll(r2, n - 4, 0)
        r4 = r3 + pltpu.roll(r3, n - 8, 0)
        back = _pool_pick(group, (r1, r2, r3, r4))
        du_ref[...] = (back[0:POOL_TILE] - dd_ext[0:POOL_TILE]).astype(du_ref.dtype)

    return pl.pallas_call(
        body, name=name, grid=(nt,), in_specs=[cur, prev, mix_spec, vec, cur, nxt], out_specs=(cur, mix_spec, vec),
        out_shape=(jax.ShapeDtypeStruct((T, D_MODEL), BF16), jax.ShapeDtypeStruct((MAIN_WIDTH, MAIN_WIDTH), F32),
                   jax.ShapeDtypeStruct((1, MAIN_WIDTH), F32)),
        compiler_params=_params("arbitrary"),
    )(proj, proj, mix, scale, dcat, dcat)


def _head(a, h):
    return a[:, h * HEAD_DIM:(h + 1) * HEAD_DIM]


def _swa_mask(blk):
    rows = SWA_GROUP * BLOCK
    qi = lax.broadcasted_iota(jnp.int32, (rows, 2 * BLOCK), 0) % BLOCK
    kj = lax.broadcasted_iota(jnp.int32, (rows, 2 * BLOCK), 1)
    dist = qi + BLOCK - kj
    valid = (dist >= 0) & (dist < BLOCK) & ((blk > 0) | (kj >= BLOCK))
    return dist.astype(F32), valid


def _swa_head_terms(sink_ref, kvh, dist):
    grp = lax.broadcasted_iota(jnp.int32, (SWA_GROUP * BLOCK, 1), 0) // BLOCK
    slopes = [2.0 ** (-8.0 * (kvh * SWA_GROUP + g + 1) / SWA_Q_HEADS) for g in range(SWA_GROUP)]
    sinks = [sink_ref[kvh * SWA_GROUP + g] for g in range(SWA_GROUP)]
    slope = jnp.where(grp == 0, slopes[0], jnp.where(grp == 1, slopes[1], slopes[2]))
    sink = jnp.where(grp == 0, sinks[0], jnp.where(grp == 1, sinks[1], sinks[2]))
    return slope * dist, sink


def _stack_heads(a, kvh):
    return jnp.concatenate([_head(a, kvh * SWA_GROUP + g) for g in range(SWA_GROUP)], axis=0)


def _swa_specs(nb):
    def at(n):
        return jnp.minimum(n, nb - 1)

    q = pl.BlockSpec((BLOCK, MAIN_WIDTH), lambda n: (at(n), 0))
    k_prev = pl.BlockSpec((BLOCK, KV_HALF), lambda n: (jnp.maximum(at(n) - 1, 0), 0))
    k_cur = pl.BlockSpec((BLOCK, KV_HALF), lambda n: (at(n), 0))
    v_prev = pl.BlockSpec((BLOCK, KV_HALF), lambda n: (jnp.maximum(at(n) - 1, 0), 1))
    v_cur = pl.BlockSpec((BLOCK, KV_HALF), lambda n: (at(n), 1))
    return q, k_prev, k_cur, v_prev, v_cur


def _swa_attn_fwd(qn, kn, kv, sinks, cat, *, name):
    T = qn.shape[0]
    nb = T // BLOCK
    q_spec, k_prev, k_cur, v_prev, v_cur = _swa_specs(nb)

    def body(sink_ref, q_ref, kp_ref, kc_ref, vp_ref, vc_ref, _, o_ref):
        dist, valid = _swa_mask(pl.program_id(0))
        kk = jnp.concatenate([kp_ref[...], kc_ref[...]], axis=0)
        vv = jnp.concatenate([vp_ref[...], vc_ref[...]], axis=0).astype(BF16)
        q = q_ref[...]
        outs = []
        for kvh in range(SWA_KV_HEADS):
            bias, sink = _swa_head_terms(sink_ref, kvh, dist)
            p, _ps = _softmax(_stack_heads(q, kvh), _head(kk, kvh), bias, valid, sink)
            o = jnp.dot(p.astype(BF16), _head(vv, kvh), preferred_element_type=F32)
            outs += [o[g * BLOCK:(g + 1) * BLOCK] for g in range(SWA_GROUP)]
        o_ref[...] = jnp.concatenate(outs, axis=1).astype(o_ref.dtype)

    return pl.pallas_call(
        body, name=name, grid=(nb,),
        in_specs=[pl.BlockSpec(memory_space=pltpu.SMEM), q_spec, k_prev, k_cur, v_prev, v_cur, ANY], out_specs=q_spec,
        out_shape=jax.ShapeDtypeStruct(cat.shape, cat.dtype), input_output_aliases={6: 0}, compiler_params=_params("parallel"),
    )(sinks, qn, kn, kn, kv, kv, cat)


def _swa_attn_bwd(qn, kn, kv, sinks, dcat, dqn, *, name):
    T = qn.shape[0]
    nb = T // BLOCK
    q_spec, k_prev, k_cur, v_prev, v_cur = _swa_specs(nb)
    late = pl.BlockSpec((BLOCK, KV_HALF), lambda n: (jnp.maximum(n - 1, 0), 0))
    tn_dims = (((0,), (0,)), ((), ()))

    def body(sink_ref, q_ref, do_ref, kp_ref, kc_ref, vp_ref, vc_ref, _, dq_ref, dk_ref, dv_ref, ds_ref, ck, cv):
        blk = pl.program_id(0)

        @pl.when(blk == 0)
        def _():
            ck[...] = jnp.zeros_like(ck)
            cv[...] = jnp.zeros_like(cv)
            ds_ref[...] = jnp.zeros_like(ds_ref)

        @pl.when(blk < nb)
        def _():
            dist, valid = _swa_mask(blk)
            kk = jnp.concatenate([kp_ref[...], kc_ref[...]], axis=0)
            vv = jnp.concatenate([vp_ref[...], vc_ref[...]], axis=0).astype(BF16)
            q = q_ref[...]
            dout = do_ref[...].astype(BF16)
            lane = lax.broadcasted_iota(jnp.int32, (1, LANES), 1)
            dsinks = jnp.zeros((1, LANES), F32)
            dqs, dks, dvs = [], [], []
            for kvh in range(SWA_KV_HEADS):
                bias, sink = _swa_head_terms(sink_ref, kvh, dist)
                qq, kh, vh, dd = _stack_heads(q, kvh), _head(kk, kvh), _head(vv, kvh), _stack_heads(dout, kvh)
                p, ps = _softmax(qq, kh, bias, valid, sink)
                dp = lax.dot_general(dd, vh, (((1,), (1,)), ((), ())), preferred_element_type=F32)
                dsum = jnp.sum(p * dp, axis=-1, keepdims=True)
                ds = (p * (dp - dsum)).astype(BF16)
                dq = jnp.dot(ds, kh, preferred_element_type=F32) * SCALE
                dqs += [dq[g * BLOCK:(g + 1) * BLOCK] for g in range(SWA_GROUP)]
                dks.append(lax.dot_general(ds, qq, tn_dims, preferred_element_type=F32) * SCALE)
                dvs.append(lax.dot_general(p.astype(BF16), dd, tn_dims, preferred_element_type=F32))
                dsink = -(ps * dsum)
                for g in range(SWA_GROUP):
                    dsinks = dsinks + jnp.where(lane == kvh * SWA_GROUP + g, jnp.sum(dsink[g * BLOCK:(g + 1) * BLOCK]), 0.0)
            dq_ref[...] = jnp.concatenate(dqs, axis=1)
            dk = jnp.concatenate(dks, axis=1)
            dv = jnp.concatenate(dvs, axis=1)
            dk_ref[...] = ck[...] + dk[0:BLOCK]
            dv_ref[...] = cv[...] + dv[0:BLOCK]
            ck[...] = dk[BLOCK:2 * BLOCK]
            cv[...] = dv[BLOCK:2 * BLOCK]
            ds_ref[...] += dsinks

        @pl.when(blk == nb)
        def _():
            dk_ref[...] = ck[...]
            dv_ref[...] = cv[...]

    return pl.pallas_call(
        body, name=name, grid=(nb + 1,),
        in_specs=[pl.BlockSpec(memory_space=pltpu.SMEM), q_spec, q_spec, k_prev, k_cur, v_prev, v_cur, ANY],
        out_specs=(q_spec, late, late, pl.BlockSpec((1, LANES), lambda n: (0, 0))),
        out_shape=(jax.ShapeDtypeStruct(dqn.shape, dqn.dtype), jax.ShapeDtypeStruct((T, KV_HALF), F32),
                   jax.ShapeDtypeStruct((T, KV_HALF), F32), jax.ShapeDtypeStruct((1, LANES), F32)),
        scratch_shapes=[pltpu.VMEM((BLOCK, KV_HALF), F32), pltpu.VMEM((BLOCK, KV_HALF), F32)],
        input_output_aliases={7: 0}, compiler_params=_params("arbitrary"),
    )(sinks, qn, dcat, kn, kn, kv, kv, dqn)


def _mem_specs(M, tq, q_col):
    q = _cols(tq, MEM_WIDTH, q_col)
    k = pl.BlockSpec((M, MEM_WIDTH), lambda i: (0, 0))
    v = pl.BlockSpec((M, MEM_WIDTH), lambda i: (0, 1))
    return q, k, v


def _mem_attn_fwd(q, q_col, mkn, mkv, *, name):
    T = q.shape[0]
    M = mkn.shape[0]
    tq = min(T, MEM_Q_TILE)
    q_spec, k_spec, v_spec = _mem_specs(M, tq, q_col)

    def body(q_ref, k_ref, v_ref, o_ref):
        qq, kk, vv = q_ref[...], k_ref[...], v_ref[...].astype(BF16)
        outs = []
        for h in range(MEM_HEADS):
            p, _ps = _softmax(_head(qq, h), _head(kk, h), None, None, None)
            outs.append(jnp.dot(p.astype(BF16), _head(vv, h), preferred_element_type=F32))
        o_ref[...] = jnp.concatenate(outs, axis=1).astype(o_ref.dtype)

    return pl.pallas_call(
        body, name=name, grid=(T // tq,), in_specs=[q_spec, k_spec, v_spec], out_specs=_cols(tq, MEM_WIDTH, MAIN_WIDTH // MEM_WIDTH),
        out_shape=jax.ShapeDtypeStruct((T, D_MODEL), BF16), compiler_params=_params("parallel"),
    )(q, mkn, mkv)


def _mem_attn_bwd(q, q_col, mkn, mkv, dcat, *, dq_width, name):
    T = q.shape[0]
    M = mkn.shape[0]
    tq = min(T, MEM_Q_TILE)
    q_spec, k_spec, v_spec = _mem_specs(M, tq, q_col)
    last = MAIN_WIDTH // MEM_WIDTH
    tn_dims = (((0,), (0,)), ((), ()))

    def body(q_ref, do_ref, k_ref, v_ref, dq_ref, dk_ref, dv_ref):
        i = pl.program_id(0)
        qq, kk, vv, dout = q_ref[...], k_ref[...], v_ref[...].astype(BF16), do_ref[...].astype(BF16)
        dqs, dks, dvs = [], [], []
        for h in range(MEM_HEADS):
            qh, kh, vh, dh = _head(qq, h), _head(kk, h), _head(vv, h), _head(dout, h)
            p, _ps = _softmax(qh, kh, None, None, None)
            dp = lax.dot_general(dh, vh, (((1,), (1,)), ((), ())), preferred_element_type=F32)
            dsum = jnp.sum(p * dp, axis=-1, keepdims=True)
            ds = (p * (dp - dsum)).astype(BF16)
            dqs.append(jnp.dot(ds, kh, preferred_element_type=F32) * SCALE)
            dks.append(lax.dot_general(ds, qh, tn_dims, preferred_element_type=F32) * SCALE)
            dvs.append(lax.dot_general(p.astype(BF16), dh, tn_dims, preferred_element_type=F32))
        dq_ref[...] = jnp.concatenate(dqs, axis=1)
        dk = jnp.concatenate(dks, axis=1)
        dv = jnp.concatenate(dvs, axis=1)

        @pl.when(i == 0)
        def _():
            dk_ref[...] = dk
            dv_ref[...] = dv

        @pl.when(i > 0)
        def _():
            dk_ref[...] += dk
            dv_ref[...] += dv

    acc = pl.BlockSpec((M, MEM_WIDTH), lambda i: (0, 0))
    return pl.pallas_call(
        body, name=name, grid=(T // tq,), in_specs=[q_spec, _cols(tq, MEM_WIDTH, last), k_spec, v_spec],
        out_specs=(_cols(tq, MEM_WIDTH, dq_width // MEM_WIDTH - 1), acc, acc),
        out_shape=(jax.ShapeDtypeStruct((T, dq_width), F32), jax.ShapeDtypeStruct((M, MEM_WIDTH), F32),
                   jax.ShapeDtypeStruct((M, MEM_WIDTH), F32)),
        compiler_params=_params("arbitrary"),
    )(q, dcat, mkn, mkv)


def _loss(y, target, *, name):
    T, D = y.shape
    tr = _row_tile(T, D)

    def body(y_ref, t_ref, l_ref, dy_ref, dyb_ref):
        i = pl.program_id(0)
        err = y_ref[...] - t_ref[...]
        dy = err / float(D)
        dy_ref[...] = dy
        dyb_ref[...] = dy.astype(BF16)
        part = jnp.full((8, 128), 0.5 * jnp.sum(jnp.mean(err * err, axis=-1)), F32)

        @pl.when(i == 0)
        def _():
            l_ref[...] = part

        @pl.when(i > 0)
        def _():
            l_ref[...] += part

    row = pl.BlockSpec((tr, D), lambda i: (i, 0))
    return pl.pallas_call(
        body, name=name, grid=(T // tr,), in_specs=[row, row],
        out_specs=(pl.BlockSpec((8, 128), lambda i: (0, 0)), row, row),
        out_shape=(jax.ShapeDtypeStruct((8, 128), F32), jax.ShapeDtypeStruct((T, D), F32), jax.ShapeDtypeStruct((T, D), BF16)),
        compiler_params=_params("arbitrary"),
    )(y, target)


def _position():
    return lax.axis_index("x"), lax.axis_index("y"), lax.axis_index("c")


def _all_gather(arrays, *, name):
    n = len(arrays)

    def body(*refs):
        srcs, outs = refs[:n], refs[n:2 * n]
        token, send_sems, recv_sems, local_sems = refs[2 * n:]
        token[...] = jnp.zeros_like(token)
        x, y, c = _position()
        me, sibling = (x, y, c), (x, y, 1 - c)
        chips = [(1 - x, y), (x, 1 - y), (1 - x, 1 - y)]

        def slot(a, px, py, pc):
            return outs[a].at[4 * px + 2 * py + pc]

        def copy(a, k, block, to, src=None):
            return pltpu.make_async_remote_copy(
                src_ref=slot(a, *block) if src is None else src, dst_ref=slot(a, *block),
                send_sem=send_sems.at[a, k], recv_sem=recv_sems.at[a, k], device_id=to, device_id_type=MESH)

        mine = [pltpu.make_async_copy(srcs[a], slot(a, *me), local_sems.at[a]) for a in range(n)]
        for cp in mine:
            cp.start()
        first, passed = [], []
        for a in range(n):
            first.append(copy(a, 0, me, sibling, src=srcs[a]))
            first += [copy(a, 1 + j, me, (*chip, c), src=srcs[a]) for j, chip in enumerate(chips)]
        for cp in first:
            cp.start()
        for a in range(n):
            for j, chip in enumerate(chips):
                copy(a, 1 + j, (*chip, c), me).wait_recv()
                fwd = copy(a, 4 + j, (*chip, c), sibling)
                fwd.start()
                passed.append(fwd)
        for a in range(n):
            copy(a, 0, sibling, me).wait_recv()
            for j, chip in enumerate(chips):
                copy(a, 4 + j, (*chip, 1 - c), me).wait_recv()
        for cp in first + passed:
            cp.wait_send()
        for cp in mine:
            cp.wait()

    return pl.pallas_call(
        body, name=name, in_specs=[ANY] * n, out_specs=[ANY] * n + [pl.BlockSpec(memory_space=pltpu.VMEM)],
        out_shape=[jax.ShapeDtypeStruct((N_DEV,) + a.shape, a.dtype) for a in arrays] + [jax.ShapeDtypeStruct((8, 128), F32)],
        scratch_shapes=[pltpu.SemaphoreType.DMA((n, 7)), pltpu.SemaphoreType.DMA((n, 7)), pltpu.SemaphoreType.DMA((n,))],
    )(*arrays)


def _sibling_exchange(by_core, whole, *, name):
    n1, n = len(by_core), len(by_core) + len(whole)

    def body(*refs):
        srcs, outs = refs[:n], refs[n:2 * n]
        send_sems, recv_sems = refs[2 * n:]
        x, y, c = _position()
        copies = [
            pltpu.make_async_remote_copy(src_ref=srcs[a].at[:, 1 - c] if a < n1 else srcs[a], dst_ref=outs[a],
                                         send_sem=send_sems.at[a], recv_sem=recv_sems.at[a], device_id=(x, y, 1 - c),
                                         device_id_type=MESH)
            for a in range(n)]
        for cp in copies:
            cp.start()
        for cp in copies:
            cp.wait()

    out_shape = [jax.ShapeDtypeStruct(a.shape[:1] + a.shape[2:], a.dtype) for a in by_core]
    out_shape += [jax.ShapeDtypeStruct(a.shape, a.dtype) for a in whole]
    outs = pl.pallas_call(
        body, name=name, in_specs=[ANY] * n, out_specs=[ANY] * n, out_shape=out_shape,
        scratch_shapes=[pltpu.SemaphoreType.DMA((n,)), pltpu.SemaphoreType.DMA((n,))],
    )(*by_core, *whole)
    return outs[:n1], outs[n1:]


def _chip_exchange(per_chip, whole, *, name):
    n1, n = len(per_chip), len(per_chip) + len(whole)

    def body(*refs):
        srcs, outs = refs[:n], refs[n:2 * n]
        send_sems, recv_sems, local_sems = refs[2 * n:]
        x, y, c = _position()
        my_chip = 2 * x + y
        chips = [(1 - x, y), (x, 1 - y), (1 - x, 1 - y)]

        def src(a, chip):
            return srcs[a].at[chip] if a < n1 else srcs[a]

        local = [pltpu.make_async_copy(src(a, my_chip), outs[a].at[my_chip], local_sems.at[a]) for a in range(n)]
        for cp in local:
            cp.start()
        copies = [
            pltpu.make_async_remote_copy(src_ref=src(a, 2 * px + py), dst_ref=outs[a].at[my_chip],
                                         send_sem=send_sems.at[a, j], recv_sem=recv_sems.at[a, j], device_id=(px, py, c),
                                         device_id_type=MESH)
            for a in range(n) for j, (px, py) in enumerate(chips)]
        for cp in copies:
            cp.start()
        for cp in copies:
            cp.wait()
        for cp in local:
            cp.wait()

    out_shape = [jax.ShapeDtypeStruct(a.shape, a.dtype) for a in per_chip]
    out_shape += [jax.ShapeDtypeStruct((N_CHIP,) + a.shape, a.dtype) for a in whole]
    outs = pl.pallas_call(
        body, name=name, in_specs=[ANY] * n, out_specs=[ANY] * n, out_shape=out_shape,
        scratch_shapes=[pltpu.SemaphoreType.DMA((n, 3)), pltpu.SemaphoreType.DMA((n, 3)), pltpu.SemaphoreType.DMA((n,))],
    )(*per_chip, *whole)
    return outs[:n1], outs[n1:]


HBM = pl.BlockSpec(memory_space=pltpu.HBM)
SEM = pl.BlockSpec(memory_space=pltpu.SEMAPHORE)
DATAFLOW = pltpu.SideEffectType.DATAFLOW_SIDE_EFFECTING


def _device(flat):
    return flat // 4, (flat // 2) % 2, flat % 2


def _gather_copies(srcs, lands, send_sems, recv_sems, incoming):
    x, y, c = _position()
    me = 4 * x + 2 * y + c
    pairs = []
    for a in range(len(srcs)):
        for d in range(1, N_DEV):
            to, frm = (me + d) % N_DEV, (me + N_DEV - d) % N_DEV
            k = a * (N_DEV - 1) + d - 1
            sems = dict(send_sem=send_sems.at[k], recv_sem=recv_sems.at[k], device_id_type=MESH)
            out = pltpu.make_async_remote_copy(src_ref=srcs[a], dst_ref=lands[a].at[me], device_id=_device(to), **sems)
            inc = pltpu.make_async_remote_copy(src_ref=srcs[a], dst_ref=lands[a].at[frm], device_id=_device(frm),
                                               **sems) if incoming else None
            pairs.append((out, inc))
    return pairs


def _chip_copies(srcs, lands, send_sems, recv_sems, incoming):
    x, y, c = _position()
    my_chip = 2 * x + y
    pairs = []
    for a in range(len(srcs)):
        for k, (px, py) in enumerate([(1 - x, y), (x, 1 - y), (1 - x, 1 - y)]):
            sem = a * (N_CHIP - 1) + k
            sems = dict(send_sem=send_sems.at[sem], recv_sem=recv_sems.at[sem], device_id=(px, py, c), device_id_type=MESH)
            out = pltpu.make_async_remote_copy(src_ref=srcs[a].at[2 * px + py], dst_ref=lands[a].at[my_chip], **sems)
            inc = pltpu.make_async_remote_copy(src_ref=srcs[a].at[2 * px + py], dst_ref=lands[a].at[2 * px + py],
                                               **sems) if incoming else None
            pairs.append((out, inc))
    return pairs


def _push_start(copies, fan, srcs, lands, *, name):
    n = len(srcs)

    def body(*refs):
        src_refs, land_refs = refs[:n], refs[n:2 * n]
        send_sems, recv_sems = refs[2 * n], refs[2 * n + 1]
        token = refs[-1]
        for out, _ in copies(src_refs, land_refs, send_sems, recv_sems, False):
            out.start()
        token[...] = jnp.zeros_like(token)

    outs = pl.pallas_call(
        body, name=name,
        out_shape=(pltpu.SemaphoreType.DMA((n * fan,)), pltpu.SemaphoreType.DMA((n * fan,)),
                   *(pltpu.HBM(a.shape, a.dtype) for a in srcs), *(pltpu.HBM(a.shape, a.dtype) for a in lands),
                   jax.ShapeDtypeStruct((8, 128), F32)),
        in_specs=[HBM] * (2 * n), out_specs=(SEM, SEM, *([HBM] * (2 * n)), pl.BlockSpec(memory_space=pltpu.VMEM)),
        input_output_aliases={i: 2 + i for i in range(2 * n)},
        compiler_params=pltpu.CompilerParams(has_side_effects=DATAFLOW),
    )(*(pltpu.with_memory_space_constraint(a, pltpu.HBM) for a in (*srcs, *lands)))
    return outs[0], outs[1], list(outs[2:2 + n]), list(outs[2 + n:2 + 2 * n]), outs[-1]


def _push_wait(copies, send_sems, recv_sems, srcs, lands, after, *, name):
    n = len(srcs)

    def body(*refs):
        src_refs, land_refs = refs[:n], refs[n:2 * n]
        for out, inc in copies(src_refs, land_refs, refs[2 * n], refs[2 * n + 1], True):
            out.wait_send()
            inc.wait_recv()
        refs[-1][...] = jnp.zeros_like(refs[-1])

    outs = pl.pallas_call(
        body, name=name,
        out_shape=(*(pltpu.HBM(a.shape, a.dtype) for a in (*srcs, *lands)), jax.ShapeDtypeStruct((8, 128), F32)),
        in_specs=[HBM] * (2 * n) + [SEM, SEM, ANY], out_specs=(*([HBM] * (2 * n)), pl.BlockSpec(memory_space=pltpu.VMEM)),
        input_output_aliases={i: i for i in range(2 * n)},
        compiler_params=pltpu.CompilerParams(has_side_effects=DATAFLOW),
    )(*srcs, *lands, send_sems, recv_sems, after)
    return list(outs[n:2 * n]), outs[-1]


def _with_own_slot(block, index, slots):
    buf = lax.empty((slots,) + block.shape, block.dtype)
    return lax.dynamic_update_slice(buf, block[None], (index,) + (0,) * block.ndim)


def _view2d(shape):
    return math.prod(shape[:-1]), shape[-1]


def _pair_sum(mine, other, core, *, name, out_dtype):
    by_core = mine.ndim == 4
    n, w = other.shape[-2:]
    tr = _row_tile(n, w * 2)
    lead = other.shape[0] if by_core else 1

    def body(core_ref, a_ref, b_ref, o_ref):
        o_ref[...] = (a_ref[...].astype(F32) + b_ref[...].astype(F32)).astype(o_ref.dtype)

    if by_core:
        a_spec = pl.BlockSpec((None, None, tr, w), lambda j, i, core_ref: (j, core_ref[0], i, 0))
        o_spec = pl.BlockSpec((None, tr, w), lambda j, i, core_ref: (j, i, 0))
    else:
        a_spec = o_spec = pl.BlockSpec((tr, w), lambda j, i, core_ref: (i, 0))
    grid_spec = pltpu.PrefetchScalarGridSpec(num_scalar_prefetch=1, grid=(lead, n // tr), in_specs=[a_spec, o_spec],
                                             out_specs=o_spec)
    return pl.pallas_call(body, name=name, grid_spec=grid_spec, out_shape=jax.ShapeDtypeStruct(other.shape, out_dtype),
                          compiler_params=_params("parallel", "parallel"))(core.reshape(1), mine, other)


def _adamw(parts, w, m, v, *, name):
    layers = len(parts)
    n_parts, R, W = parts[0].shape
    tr = _row_tile(R, W * 2)
    per_layer = R // tr

    def update(p_ref, w_ref, m_ref, v_ref, g_out, d_out, m_out, v_out):
        g = p_ref[0].astype(F32)
        for j in range(1, n_parts):
            g = g + p_ref[j].astype(F32)
        m_new = ADAM_B1 * m_ref[...] + (1.0 - ADAM_B1) * g
        v_new = ADAM_B2 * v_ref[...] + (1.0 - ADAM_B2) * (g * g)
        m_hat = m_new / (1.0 - ADAM_B1 ** ADAM_STEP)
        v_hat = v_new / (1.0 - ADAM_B2 ** ADAM_STEP)
        g_out[...] = g
        d_out[...] = -ADAM_LR * (m_hat / (jnp.sqrt(v_hat) + ADAM_EPS) + ADAM_WD * w_ref[...])
        m_out[...] = m_new
        v_out[...] = v_new

    def body(*refs):
        for k in range(layers):
            pl.when(pl.program_id(0) == k)(lambda k=k: update(refs[k], *refs[layers:]))

    def parts_spec(k):
        return pl.BlockSpec((n_parts, tr, W), lambda l, i: (0, jnp.where(l == k, i, 0), 0))

    row = pl.BlockSpec((tr, W), lambda l, i: (l * per_layer + i, 0))
    out = jax.ShapeDtypeStruct((layers * R, W), F32)
    return pl.pallas_call(
        body, name=name, grid=(layers, per_layer), in_specs=[parts_spec(k) for k in range(layers)] + [row, row, row],
        out_specs=(row, row, row, row), out_shape=(out, out, out, out), compiler_params=_params("arbitrary", "arbitrary"),
    )(*parts, w, m, v)


SMALL_ROWS = 608


def _pack_small(p):
    flat = jnp.concatenate([p[n].reshape(-1).astype(F32) for n in SMALL_NAMES])
    return jnp.pad(flat, (0, SMALL_ROWS * PACK_W - flat.shape[0])).reshape(SMALL_ROWS, PACK_W)


def _unpack_small(buf, like):
    out, at = {}, 0
    flat = buf.reshape(-1)
    for n in SMALL_NAMES:
        size = math.prod(like[n].shape)
        out[n] = flat[at:at + size].reshape(like[n].shape)
        at += size
    return out


def _heads(a, nh):
    T = a.shape[0]
    return a.reshape(T, nh, HEAD_DIM).transpose(1, 0, 2).reshape(nh * T, HEAD_DIM)


def _unheads(a, nh):
    a = a.reshape(nh, -1, HEAD_DIM)
    return a.transpose(1, 0, 2).reshape(a.shape[1], nh * HEAD_DIM)


def _groups(a):
    T = a.shape[0]
    return a.reshape(T, POOL_GROUPS, POOL_GROUP_DIM).transpose(1, 0, 2)


def _ungroups(a):
    return a.transpose(1, 0, 2).reshape(a.shape[1], MAIN_WIDTH)


def _local_step(x, mem, target, p, w_kv, fetch, reduce_layer, reduce_wait):
    T = x.shape[0]
    M = mem.shape[0]
    saved = []
    h = x
    kn = vv = k_raw = h_kv = hn_kv = None
    for l in range(DEPTH):
        s = {}
        wl, token = fetch(l, h)
        s["w"] = wl
        if l == N_A:
            h_kv = h
            hn_kv = _rms_fwd(h, p["kv_norm"], name="kv_norm_fwd")
            kv = _mm(hn_kv, w_kv, b_kind="rows", name="kv_proj")
            k_raw = _heads(kv[:, :KV_HALF], SWA_KV_HEADS)
            kn = _rms_fwd(k_raw, p["k_norm"], name="k_norm_fwd").reshape(SWA_KV_HEADS, T, HEAD_DIM)
            vv = _heads(kv[:, KV_HALF:], SWA_KV_HEADS).astype(BF16).reshape(SWA_KV_HEADS, T, HEAD_DIM)
        s["h"] = h
        s["xn1"] = _rms_fwd(h, p["norm_mix"][l] + token, name="norm_mix_fwd")
        proj = _mm(s["xn1"], wl["w_in"], b_kind="rows", name="in_proj")
        s["mq_raw"] = _heads(proj[:, MAIN_WIDTH:], MEM_HEADS)
        s["mqn"] = _rms_fwd(s["mq_raw"], p["mem_q_norm"][l], name="mem_q_norm_fwd").reshape(MEM_HEADS, T, HEAD_DIM)
        s["memn"] = _rms_fwd(mem, p["mem_norm"][l], name="mem_norm_fwd")
        mkv = _mm(s["memn"], wl["w_mem_kv"], b_kind="rows", name="mem_kv_proj")
        s["mk_raw"] = _heads(mkv[:, :MEM_WIDTH], MEM_HEADS)
        s["mkn"] = _rms_fwd(s["mk_raw"], p["mem_k_norm"][l], name="mem_k_norm_fwd").reshape(MEM_HEADS, M, HEAD_DIM)
        s["mvv"] = _heads(mkv[:, MEM_WIDTH:], MEM_HEADS).astype(BF16).reshape(MEM_HEADS, M, HEAD_DIM)
        mem_out = _unheads(_mem_fwd(s["mqn"], s["mkn"], s["mvv"], name="mem_attn_fwd"), MEM_HEADS)
        if l < N_A:
            s["u"] = _groups(proj[:, :MAIN_WIDTH])
            s["pw"] = p["pool_w"][l]
            s["ps"] = p["pool_scale"][l].reshape(POOL_GROUPS, 1, POOL_GROUP_DIM)
            main_out = _ungroups(_pool_fwd(s["u"], s["pw"], s["ps"], name="pool_fwd"))
        else:
            j = l - N_A
            s["q_raw"] = _heads(proj[:, :MAIN_WIDTH], SWA_Q_HEADS)
            s["qn"] = _rms_fwd(s["q_raw"], p["q_norm"][j], name="q_norm_fwd").reshape(SWA_KV_HEADS, SWA_GROUP, T, HEAD_DIM)
            main_out = _unheads(_swa_fwd(s["qn"], kn, vv, p["sinks"][j], name="swa_fwd"), SWA_Q_HEADS)
        s["cat"] = jnp.concatenate([main_out, mem_out], axis=-1)
        s["h1"] = _mm(s["cat"], wl["w_out"], b_kind="rows", res=h, name="out_proj")
        s["xn2"] = _rms_fwd(s["h1"], p["norm_mlp"][l], name="norm_mlp_fwd")
        s["r"], s["a"] = _mm(s["xn2"], wl["w_up"], b_kind="layers", relu2=True, name="mlp_up")
        h = _mm(s["a"], wl["w_down"], b_kind="rows", res=s["h1"], name="mlp_down")
        saved.append(s)

    loss, dh, dh_b = _loss(h, target, name="loss_head")

    g = {n: [None] * DEPTH for n in ("norm_mix", "mem_norm", "mem_q_norm", "mem_k_norm", "norm_mlp")}
    g_kv = None
    token = None
    g.update({n: [None] * N_A for n in ("pool_w", "pool_scale", "q_norm", "sinks")})
    dkn = dvv = None
    for l in reversed(range(DEPTH)):
        s = saved[l]
        wl = s["w"]
        gb = {}

        def dw(a, dy, n):
            return _mm(a, dy, ta=True, out_kind="layers" if n == "w_up" else "rows", out_buf=lax.empty(wl[n].shape, BF16),
                       name=n + "_grad")

        norm_mlp_gain = p["norm_mlp"][l] if token is None else p["norm_mlp"][l] + token
        gb["w_down"] = dw(s["a"], dh_b, "w_down")
        du = _mm(dh_b, wl["w_down"], tb=True, b_kind="rows", mul2=s["a"], out_dtype=BF16, name="mlp_down_dx")
        gb["w_up"] = dw(s["xn2"], du, "w_up")
        dxn2 = _mm(du, wl["w_up"], tb=True, b_kind="layers", name="mlp_up_dx")
        dh1, dh1_b, g["norm_mlp"][l] = _rms_bwd(s["h1"], norm_mlp_gain, [dxn2], res=dh, also_bf16=True,
                                                name="norm_mlp_bwd")
        gb["w_out"] = dw(s["cat"], dh1_b, "w_out")
        dcat = _mm(dh1_b, wl["w_out"], tb=True, b_kind="rows", name="out_proj_dx")
        dmem_out = _heads(dcat[:, MAIN_WIDTH:], MEM_HEADS).astype(BF16).reshape(MEM_HEADS, T, HEAD_DIM)
        dmqn, dmkn, dmvv = _mem_bwd(s["mqn"], s["mkn"], s["mvv"], dmem_out, name="mem_attn_bwd")
        dmq_raw, g["mem_q_norm"][l] = _rms_bwd(s["mq_raw"], p["mem_q_norm"][l], [dmqn.reshape(MEM_HEADS * T, HEAD_DIM)],
                                               name="mem_q_norm_bwd")
        dmk_raw, g["mem_k_norm"][l] = _rms_bwd(s["mk_raw"], p["mem_k_norm"][l], [dmkn.reshape(MEM_HEADS * M, HEAD_DIM)],
                                               name="mem_k_norm_bwd")
        dmkv = jnp.concatenate([_unheads(dmk_raw, MEM_HEADS), _unheads(dmvv, MEM_HEADS)], axis=-1).astype(BF16)
        gb["w_mem_kv"] = dw(s["memn"], dmkv, "w_mem_kv")
        dmemn = _mm(dmkv, wl["w_mem_kv"], tb=True, b_kind="rows", name="mem_kv_proj_dx")
        g["mem_norm"][l] = _rms_bwd(mem, p["mem_norm"][l], [dmemn], want_dx=False, name="mem_norm_bwd")
        if l < N_A:
            dmain_out = _groups(dcat[:, :MAIN_WIDTH])
            du_pool, g["pool_w"][l], dps = _pool_bwd(s["u"], s["pw"], s["ps"], dmain_out, name="pool_bwd")
            g["pool_scale"][l] = dps.reshape(MAIN_WIDTH)
            dmain = _ungroups(du_pool)
        else:
            j = l - N_A
            dmain_out = _heads(dcat[:, :MAIN_WIDTH], SWA_Q_HEADS).astype(BF16).reshape(SWA_KV_HEADS, SWA_GROUP, T, HEAD_DIM)
            dqn, dk_l, dv_l, dsink = _swa_bwd(s["qn"], kn, vv, p["sinks"][j], dmain_out, name="swa_bwd")
            g["sinks"][j] = dsink[:, 0, :SWA_GROUP].reshape(SWA_Q_HEADS)
            dq_raw, g["q_norm"][j] = _rms_bwd(s["q_raw"], p["q_norm"][j], [dqn.reshape(SWA_Q_HEADS * T, HEAD_DIM)],
                                              name="q_norm_bwd")
            dmain = _unheads(dq_raw, SWA_Q_HEADS)
            dk_l = dk_l.reshape(SWA_KV_HEADS * T, HEAD_DIM)
            dv_l = dv_l.reshape(SWA_KV_HEADS * T, HEAD_DIM)
            dkn = dk_l if dkn is None else _add(dkn, dk_l, name="dk_sum")
            dvv = dv_l if dvv is None else _add(dvv, dv_l, name="dv_sum")
        dproj = jnp.concatenate([dmain, _unheads(dmq_raw, MEM_HEADS)], axis=-1).astype(BF16)
        gb["w_in"] = dw(s["xn1"], dproj, "w_in")
        dxn1 = _mm(dproj, wl["w_in"], tb=True, b_kind="rows", name="in_proj_dx")
        if l in (0, N_A):
            dh, g["norm_mix"][l] = _rms_bwd(s["h"], p["norm_mix"][l], [dxn1], res=dh1, name="norm_mix_bwd")
        else:
            dh, dh_b, g["norm_mix"][l] = _rms_bwd(s["h"], p["norm_mix"][l], [dxn1], res=dh1, also_bf16=True,
                                                  name="norm_mix_bwd")
        if l == N_A:
            dk_raw, g["k_norm"] = _rms_bwd(k_raw, p["k_norm"], [dkn], name="k_norm_bwd")
            dkv = jnp.concatenate([_unheads(dk_raw, SWA_KV_HEADS), _unheads(dvv, SWA_KV_HEADS)], axis=-1).astype(BF16)
            g_kv = _mm(hn_kv, dkv, ta=True, out_kind="rows", out_buf=lax.empty(w_kv.shape, BF16), name="w_kv_grad")
            dhn = _mm(dkv, w_kv, tb=True, b_kind="rows", name="kv_proj_dx")
            dh, dh_b, g["kv_norm"] = _rms_bwd(h_kv, p["kv_norm"], [dhn], res=dh, also_bf16=True, name="kv_norm_bwd")
        if l + 1 < DEPTH:
            reduce_wait(l + 1, dh)
        token = reduce_layer(l, gb)
    grads = {n: (jnp.stack(v) if isinstance(v, list) else v) for n, v in g.items()}
    return loss, dh, grads, g_kv


def _block_diag(pw):
    out = jnp.zeros((MAIN_WIDTH, MAIN_WIDTH), pw.dtype)
    for g in range(POOL_GROUPS):
        out = lax.dynamic_update_slice(out, pw[g], (g * POOL_GROUP_DIM, g * POOL_GROUP_DIM))
    return out


def _diag_blocks(m):
    return jnp.stack([m[g * POOL_GROUP_DIM:(g + 1) * POOL_GROUP_DIM, g * POOL_GROUP_DIM:(g + 1) * POOL_GROUP_DIM]
                      for g in range(POOL_GROUPS)])


def _train_pass(x, mem, target, p, w_kv, fetch, reduce_layer, reduce_wait):
    T = x.shape[0]
    mem_cols = MAIN_WIDTH // MEM_WIDTH
    k_gain = _head_gain(p["k_norm"], SWA_KV_HEADS)
    saved = []
    h = x
    kn = kv = h_kv = hn_kv = None
    for l in range(DEPTH):
        s = {}
        wl, token = fetch(l, h)
        s["w"] = wl
        if l == N_A:
            h_kv = h
            hn_kv, kv = _norm_mm(h, p["kv_norm"], w_kv, b_kind="rows", name="kv_proj")
            kn = _seg_rms_fwd(kv, k_gain, width=KV_HALF, col=0, name="k_norm_fwd")
        s["h"] = h
        s["xn1"], proj = _norm_mm(h, p["norm_mix"][l] + token, wl["w_in"], b_kind="rows", name="in_proj")
        s["proj"] = proj
        s["memn"] = _rms_fwd(mem, p["mem_norm"][l], name="mem_norm_fwd")
        s["mkv"] = _mm(s["memn"], wl["w_mem_kv"], b_kind="rows", name="mem_kv_proj")
        s["mk_gain"] = _head_gain(p["mem_k_norm"][l], MEM_HEADS)
        s["mkn"] = _seg_rms_fwd(s["mkv"], s["mk_gain"], width=MEM_WIDTH, col=0, name="mem_k_norm_fwd")
        if l < N_A:
            s["q_gain"] = _head_gain(p["mem_q_norm"][l], MEM_HEADS)
            s["qn"] = _seg_rms_fwd(proj, s["q_gain"], width=MEM_WIDTH, col=mem_cols, name="mem_q_norm_fwd")
            s["q_col"] = 0
        else:
            j = l - N_A
            s["q_gain"] = jnp.concatenate([_head_gain(p["q_norm"][j], SWA_Q_HEADS), _head_gain(p["mem_q_norm"][l], MEM_HEADS)],
                                          axis=1)
            s["qn"] = _seg_rms_fwd(proj, s["q_gain"], width=D_MODEL, col=0, name="q_norm_fwd")
            s["q_col"] = mem_cols
        cat = _mem_attn_fwd(s["qn"], s["q_col"], s["mkn"], s["mkv"], name="mem_attn_fwd")
        if l < N_A:
            s["mix"] = _block_diag(p["pool_w"][l])
            s["scale"] = p["pool_scale"][l].reshape(1, MAIN_WIDTH)
            s["cat"] = _pool_mix_fwd(proj, s["mix"], s["scale"], cat, name="pool_fwd")
        else:
            s["cat"] = _swa_attn_fwd(s["qn"], kn, kv, p["sinks"][l - N_A], cat, name="swa_fwd")
        s["h1"] = _mm(s["cat"], wl["w_out"], b_kind="rows", res=h, name="out_proj")
        s["xn2"], s["a"] = _norm_mm(s["h1"], p["norm_mlp"][l], wl["w_up"], b_kind="layers", relu2=True, name="mlp_up")
        h = _mm(s["a"], wl["w_down"], b_kind="rows", res=s["h1"], name="mlp_down")
        saved.append(s)

    loss, dh, dh_b = _loss(h, target, name="loss_head")

    g = {n: [None] * DEPTH for n in ("norm_mix", "mem_norm", "mem_q_norm", "mem_k_norm", "norm_mlp")}
    g.update({n: [None] * N_A for n in ("pool_w", "pool_scale", "q_norm", "sinks")})
    g_kv = None
    token = None
    dks, dvs = [], []
    for l in reversed(range(DEPTH)):
        s = saved[l]
        wl = s["w"]
        gb = {}

        def dw(a, dy, n):
            return _mm(a, dy, ta=True, out_kind="layers" if n == "w_up" else "rows", out_buf=lax.empty(wl[n].shape, BF16),
                       name=n + "_grad")

        norm_mlp_gain = p["norm_mlp"][l] if token is None else p["norm_mlp"][l] + token
        gb["w_down"] = dw(s["a"], dh_b, "w_down")
        du = _mm(dh_b, wl["w_down"], tb=True, b_kind="rows", mul2=s["a"], out_dtype=BF16, name="mlp_down_dx")
        gb["w_up"] = dw(s["xn2"], du, "w_up")
        early = reduce_layer(l, gb, early=True)
        if early is not None:
            norm_mlp_gain = norm_mlp_gain + early
        dh1, dh1_b, g["norm_mlp"][l] = _mm_rms_bwd(du, wl["w_up"], s["h1"], norm_mlp_gain, dh, b_kind="layers", also_bf16=True,
                                                   name="mlp_up_dx")
        gb["w_out"] = dw(s["cat"], dh1_b, "w_out")
        dcat = _mm(dh1_b, wl["w_out"], tb=True, b_kind="rows", name="out_proj_dx")
        if l < N_A:
            dq, dmk, dmv = _mem_attn_bwd(s["qn"], s["q_col"], s["mkn"], s["mkv"], dcat, dq_width=MEM_WIDTH, name="mem_attn_bwd")
            dproj, dmix, dscale = _pool_mix_bwd(s["proj"], s["mix"], s["scale"], dcat, name="pool_bwd")
            g["pool_w"][l] = _diag_blocks(dmix)
            g["pool_scale"][l] = dscale.reshape(MAIN_WIDTH)
            dproj, dgain = _seg_rms_bwd(s["proj"], s["q_gain"], [dq], width=MEM_WIDTH, col=mem_cols, out_buf=dproj,
                                        out_col=mem_cols, name="mem_q_norm_bwd")
            g["mem_q_norm"][l] = _fold_heads(dgain, MEM_HEADS)
        else:
            j = l - N_A
            dqn, dmk, dmv = _mem_attn_bwd(s["qn"], s["q_col"], s["mkn"], s["mkv"], dcat, dq_width=D_MODEL, name="mem_attn_bwd")
            dqn, dk_l, dv_l, dsinks = _swa_attn_bwd(s["qn"], kn, kv, p["sinks"][j], dcat, dqn, name="swa_bwd")
            dks.append(dk_l)
            dvs.append(dv_l)
            g["sinks"][j] = dsinks[0, :SWA_Q_HEADS]
            dproj, dgain = _seg_rms_bwd(s["proj"], s["q_gain"], [dqn], width=D_MODEL, col=0, name="q_norm_bwd")
            g["q_norm"][j] = _fold_heads(dgain[:, :MAIN_WIDTH], SWA_Q_HEADS)
            g["mem_q_norm"][l] = _fold_heads(dgain[:, MAIN_WIDTH:], MEM_HEADS)
        dmk_raw, dgain = _seg_rms_bwd(s["mkv"], s["mk_gain"], [dmk], width=MEM_WIDTH, col=0, name="mem_k_norm_bwd")
        g["mem_k_norm"][l] = _fold_heads(dgain, MEM_HEADS)
        dmkv = jnp.concatenate([dmk_raw, dmv.astype(BF16)], axis=1)
        gb["w_mem_kv"] = dw(s["memn"], dmkv, "w_mem_kv")
        dmemn = _mm(dmkv, wl["w_mem_kv"], tb=True, b_kind="rows", name="mem_kv_proj_dx")
        g["mem_norm"][l] = _rms_bwd(mem, p["mem_norm"][l], [dmemn], want_dx=False, name="mem_norm_bwd")
        gb["w_in"] = dw(s["xn1"], dproj, "w_in")
        if l in (0, N_A):
            dh, g["norm_mix"][l] = _mm_rms_bwd(dproj, wl["w_in"], s["h"], p["norm_mix"][l], dh1, b_kind="rows", also_bf16=False,
                                               name="in_proj_dx")
        else:
            dh, dh_b, g["norm_mix"][l] = _mm_rms_bwd(dproj, wl["w_in"], s["h"], p["norm_mix"][l], dh1, b_kind="rows",
                                                     also_bf16=True, name="in_proj_dx")
        if l == N_A:
            dkv, dgain = _seg_rms_bwd(kv, k_gain, dks, width=KV_HALF, col=0, out_buf=lax.empty((T, 2 * KV_HALF), BF16),
                                      name="k_norm_bwd")
            g["k_norm"] = _fold_heads(dgain, SWA_KV_HEADS)
            dkv = _sum_into(dvs[0], dvs[1], dkv, 1, name="dv_sum")
            g_kv = _mm(hn_kv, dkv, ta=True, out_kind="rows", out_buf=lax.empty(w_kv.shape, BF16), name="w_kv_grad")
            dh, dh_b, g["kv_norm"] = _mm_rms_bwd(dkv, w_kv, h_kv, p["kv_norm"], dh, b_kind="rows", also_bf16=True,
                                                 name="kv_proj_dx")
        if l + 1 < DEPTH:
            reduce_wait(l + 1, dh)
        token = reduce_layer(l, gb)
    grads = {n: (jnp.stack(v) if isinstance(v, list) else v) for n, v in g.items()}
    return loss, dh, grads, g_kv


def kernel(x, mem, norm_mix, w_in, pool_w, pool_scale, kv_norm, w_kv, k_norm, q_norm, sinks, mem_norm, w_mem_kv, mem_q_norm, mem_k_norm, w_out, norm_mlp, w_up, w_down, loss_target, m_norm_mix, m_w_in, m_pool_w, m_pool_scale, m_kv_norm, m_w_kv, m_k_norm, m_q_norm, m_sinks, m_mem_norm, m_w_mem_kv, m_mem_q_norm, m_mem_k_norm, m_w_out, m_norm_mlp, m_w_up, m_w_down, v_norm_mix, v_w_in, v_pool_w, v_pool_scale, v_kv_norm, v_w_kv, v_k_norm, v_q_norm, v_sinks, v_mem_norm, v_w_mem_kv, v_mem_q_norm, v_mem_k_norm, v_w_out, v_norm_mlp, v_w_up, v_w_down):
    weights = dict(norm_mix=norm_mix, w_in=w_in, pool_w=pool_w, pool_scale=pool_scale, kv_norm=kv_norm, w_kv=w_kv,
                   k_norm=k_norm, q_norm=q_norm, sinks=sinks, mem_norm=mem_norm, w_mem_kv=w_mem_kv,
                   mem_q_norm=mem_q_norm, mem_k_norm=mem_k_norm, w_out=w_out, norm_mlp=norm_mlp, w_up=w_up, w_down=w_down)
    mom1 = dict(norm_mix=m_norm_mix, w_in=m_w_in, pool_w=m_pool_w, pool_scale=m_pool_scale, kv_norm=m_kv_norm, w_kv=m_w_kv,
                k_norm=m_k_norm, q_norm=m_q_norm, sinks=m_sinks, mem_norm=m_mem_norm, w_mem_kv=m_w_mem_kv,
                mem_q_norm=m_mem_q_norm, mem_k_norm=m_mem_k_norm, w_out=m_w_out, norm_mlp=m_norm_mlp, w_up=m_w_up,
                w_down=m_w_down)
    mom2 = dict(norm_mix=v_norm_mix, w_in=v_w_in, pool_w=v_pool_w, pool_scale=v_pool_scale, kv_norm=v_kv_norm, w_kv=v_w_kv,
                k_norm=v_k_norm, q_norm=v_q_norm, sinks=v_sinks, mem_norm=v_mem_norm, w_mem_kv=v_w_mem_kv,
                mem_q_norm=v_mem_q_norm, mem_k_norm=v_mem_k_norm, w_out=v_w_out, norm_mlp=v_norm_mlp, w_up=v_w_up,
                w_down=v_w_down)
    names = list(weights)
    x_pos, y_pos, core = (lax.axis_index(n).astype(jnp.int32) for n in AXES)
    me, my_chip = 4 * x_pos + 2 * y_pos + core, 2 * x_pos + y_pos
    shard = MAIN_WIDTH // N_DEV

    def layer_shards(l, zero=0.0):
        return [(weights[n][l:l + 1] + zero).astype(BF16) for n in LAYERED]

    def usable(arrays):
        wl = dict(zip(LAYERED, arrays))
        wl["w_up"] = wl["w_up"].transpose(1, 2, 0, 3).reshape(1, D_MODEL, D_FF)
        return wl

    scale_block = jnp.pad(pool_scale, ((0, 8 - N_A), (0, 128 - shard)))
    *first, first_done = _all_gather(layer_shards(0) + [w_kv[None].astype(BF16), scale_block], name="gather_first")
    p = {n: weights[n] for n in SMALL_NAMES}
    p["pool_scale"] = first[-1][:, :N_A, :shard].transpose(1, 0, 2).reshape(N_A, MAIN_WIDTH)
    gathers, reduces, parts = {}, {}, {}

    def fetch(l, after):
        if l == 0:
            got, done = first[:len(LAYERED)], first_done
        else:
            got, done = _push_wait(_gather_copies, *gathers.pop(l), after, name=f"gather_wait_{l}")
        token = 0.0
        if l + 1 < DEPTH:
            srcs = layer_shards(l + 1, done[0, 0])
            *handles, block = _push_start(_gather_copies, N_DEV - 1, srcs, [_with_own_slot(a, me, N_DEV) for a in srcs],
                                          name=f"gather_start_{l + 1}")
            gathers[l + 1], token = handles, block[0, 0]
        return usable(got), token

    def by_core(gb):
        gb = dict(gb)
        if "w_up" in gb:
            gb["w_up"] = gb["w_up"].reshape(D_MODEL, N_DEV, D_FF // N_DEV).transpose(1, 0, 2)
        order = [n for n in LAYERED if n in gb] + [n for n in gb if n not in LAYERED]
        return {n: gb[n].reshape((N_CHIP, 2) + _view2d(gb[n].shape[1:] if n == "w_up" else gb[n].shape[2:])) for n in order}

    def chip_sums(gb, tag, whole=()):
        views = by_core(gb)
        sib, sib_whole = _sibling_exchange(list(views.values()), list(whole), name="reduce_sibling_" + tag)
        sums = [_pair_sum(a, b, core, name=f"chip_sum_{n}_{tag}", out_dtype=BF16) for (n, a), b in zip(views.items(), sib)]
        return sums, sib_whole

    mlp = ("w_up", "w_down")

    def reduce_layer(l, gb, early=False):
        if early and l > 0:
            return None
        if l == 0 and not early:
            reduces["rest"] = {n: a for n, a in gb.items() if n not in mlp}
            return None
        tag = "0_mlp" if early else str(l)
        sums, _ = chip_sums({n: gb[n] for n in mlp} if early else gb, tag)
        lands = [_with_own_slot(lax.dynamic_index_in_dim(a, my_chip, 0, keepdims=False), my_chip, N_CHIP) for a in sums]
        *handles, block = _push_start(_chip_copies, N_CHIP - 1, sums, lands, name="reduce_start_" + tag)
        reduces[l] = handles
        return block[0, 0]

    def reduce_wait(l, after):
        parts[l], _ = _push_wait(_chip_copies, *reduces.pop(l), after, name=f"reduce_wait_{l}")

    loss, grad_x, grads, g_kv = _train_pass(x[0], mem[0], loss_target[0], p, first[len(LAYERED)], fetch, reduce_layer, reduce_wait)

    last = dict(reduces.pop("rest"))
    last["w_kv"] = g_kv
    last["pool_scale"] = grads["pool_scale"].reshape(N_A, N_DEV, shard).transpose(1, 0, 2).astype(BF16)[:, None]
    small = _pack_small(grads)
    sums, (sib_small,) = chip_sums(last, "0", whole=[small])
    chip_small = _pair_sum(small, sib_small, core, name="chip_sum_small", out_dtype=F32)
    (p_in, p_mem_kv, p_out, parts_kv, parts_scale), (parts_small,) = _chip_exchange(sums, [chip_small], name="reduce_chips_0")
    (p_up, p_down), _ = _push_wait(_chip_copies, *reduces.pop(0), parts_small, name="reduce_wait_0_mlp")
    parts[0] = [p_in, p_mem_kv, p_out, p_up, p_down]

    def adamw(n, n_parts):
        res = _adamw(n_parts, *(d[n].reshape(_view2d(d[n].shape)) for d in (weights, mom1, mom2)), name="adamw_" + n)
        return [r.reshape(weights[n].shape) for r in res]

    new = {n: adamw(n, [parts[l][k] for l in range(DEPTH)]) for k, n in enumerate(LAYERED)}
    new["w_kv"] = adamw("w_kv", [parts_kv])
    new["pool_scale"] = adamw("pool_scale", [parts_scale])
    res = _adamw([parts_small], _pack_small(weights), _pack_small(mom1), _pack_small(mom2), name="adamw_replicated")
    for n, vals in zip(SMALL_NAMES, zip(*(_unpack_small(r, weights).values() for r in res))):
        new[n] = list(vals)
    outs = [new[n][k] for k in range(4) for n in names]
    total = lax.psum(loss[0, 0], AXES)
    return (total, grad_x[None], *outs)
```

```python
import functools
import math

import jax
import jax.numpy as jnp
from jax import lax
from jax.experimental import pallas as pl
from jax.experimental.pallas import tpu as pltpu

F32 = jnp.float32
BF16 = jnp.bfloat16
MESH = pl.DeviceIdType.MESH
AXES = ("x", "y", "c")

D_MODEL = 1024
DEPTH = 4
N_A = 2
HEAD_DIM = 64
MEM_HEADS = 4
MEM_WIDTH = MEM_HEADS * HEAD_DIM
MAIN_WIDTH = D_MODEL - MEM_WIDTH
POOL_GROUPS = 4
POOL_GROUP_DIM = MAIN_WIDTH // POOL_GROUPS
POOL_HALO = 16
SWA_Q_HEADS = MAIN_WIDTH // HEAD_DIM
SWA_KV_HEADS = 4
SWA_GROUP = SWA_Q_HEADS // SWA_KV_HEADS
KV_HALF = SWA_KV_HEADS * HEAD_DIM
BLOCK = 128
D_FF = 4 * D_MODEL
EPS = 1e-6
SCALE = HEAD_DIM ** -0.5
NEG = float(jnp.finfo(jnp.float32).min)
N_DEV = 8
N_CHIP = 4

ADAM_LR = 0.001
ADAM_B1 = 0.9
ADAM_B2 = 0.999
ADAM_EPS = 1e-08
ADAM_WD = 0.01
ADAM_STEP = 10

PACK_W = 512
VMEM_LIMIT = 52 * 1024 * 1024
MM_TILE = 1024

LAYERED = ("w_in", "w_mem_kv", "w_out", "w_up", "w_down")
SMALL_NAMES = ("norm_mix", "pool_w", "kv_norm", "k_norm", "q_norm", "sinks", "mem_norm", "mem_q_norm", "mem_k_norm",
               "norm_mlp")


ANY = pl.BlockSpec(memory_space=pl.ANY)


def _params(*sem):
    return pltpu.CompilerParams(dimension_semantics=sem, vmem_limit_bytes=VMEM_LIMIT)


def _mm(a, b, *, name, ta=False, tb=False, b_kind=None, layer=0, res=None, relu2=False, mul2=None, out_dtype=F32,
        out_kind=None, out_buf=None):
    if ta:
        K, M = a.shape
    else:
        M, K = a.shape
    if b_kind is None:
        rows_b, cols_b = b.shape
    elif b_kind == "rows":
        rows_b, cols_b = b.shape[0] * b.shape[2], b.shape[3]
    else:
        rows_b, cols_b = b.shape[1:]
    N, K2 = (rows_b, cols_b) if tb else (cols_b, rows_b)
    assert K == K2, (a.shape, b.shape)
    tm = min(M, MM_TILE if K <= MM_TILE else MM_TILE // 2)
    tn = min(N, MM_TILE)
    assert M % tm == 0 and N % tn == 0
    row_tile, col_tile = (tn, K) if tb else (K, tn)
    a_spec = pl.BlockSpec((K, tm), lambda j, i: (0, i)) if ta else pl.BlockSpec((tm, K), lambda j, i: (i, 0))

    def rc(j):
        return (j, 0) if tb else (0, j)

    if b_kind is None:
        b_spec = pl.BlockSpec((row_tile, col_tile), lambda j, i: rc(j))
    elif b_kind == "rows":
        per = row_tile // b.shape[2]
        b_spec = pl.BlockSpec((per, None, b.shape[2], col_tile), lambda j, i: (rc(j)[0], layer, 0, rc(j)[1]))
    else:
        b_spec = pl.BlockSpec((None, row_tile, col_tile), lambda j, i: (layer, *rc(j)))
    o_spec = pl.BlockSpec((tm, tn), lambda j, i: (i, j))
    dn = (((0 if ta else 1,), (1 if tb else 0,)), ((), ()))
    extra = [e for e in (res, mul2) if e is not None]
    n_in = 2 + len(extra) + (1 if out_buf is not None else 0)

    def body(*refs):
        a_ref, b_ref = refs[0], refs[1]
        extra_refs = refs[2:2 + len(extra)]
        out = refs[n_in]
        bv = b_ref[...].astype(BF16).reshape(row_tile, col_tile)
        v = lax.dot_general(a_ref[...].astype(BF16), bv, dn, preferred_element_type=F32)
        if res is not None:
            v = extra_refs[0][...] + v
        elif mul2 is not None:
            v = v * (2.0 * jnp.sqrt(extra_refs[0][...].astype(F32)))
        if relu2:
            r = jnp.maximum(v, 0.0)
            v = r * r
        out[...] = v.astype(out.dtype).reshape(out.shape)

    in_specs = [a_spec, b_spec] + [o_spec] * len(extra)
    operands = [a, b, *extra]
    aliases = {}
    if out_kind is None:
        out_shape = jax.ShapeDtypeStruct((M, N), BF16 if relu2 else out_dtype)
        out_specs = o_spec
    else:
        if out_kind == "rows":
            s = out_buf.shape[2]
            out_specs = pl.BlockSpec((tm // s, None, s, tn), lambda j, i: (i, layer, 0, j))
        else:
            out_specs = pl.BlockSpec((None, tm, tn), lambda j, i: (layer, i, j))
        out_shape = jax.ShapeDtypeStruct(out_buf.shape, out_buf.dtype)
        in_specs.append(ANY)
        operands.append(out_buf)
        aliases = {len(operands) - 1: 0}
    return pl.pallas_call(
        body, name=name, grid=(N // tn, M // tm), in_specs=in_specs, out_specs=out_specs, out_shape=out_shape,
        input_output_aliases=aliases, compiler_params=_params("parallel", "parallel"),
    )(*operands)


def _weight_block(b, b_kind, transposed, tn):
    if b_kind == "rows":
        s = b.shape[2]
        rows, cols = b.shape[0] * s, b.shape[3]
        if transposed:
            return (lambda at: pl.BlockSpec((b.shape[0], None, s, cols), lambda *g: (0, 0, 0, 0))), rows, cols
        return (lambda at: pl.BlockSpec((b.shape[0], None, s, tn), lambda *g: (0, 0, 0, at(*g)))), rows, cols
    rows, cols = b.shape[1:]
    if transposed:
        return (lambda at: pl.BlockSpec((None, rows, cols), lambda *g: (0, 0, 0))), rows, cols
    return (lambda at: pl.BlockSpec((None, rows, tn), lambda *g: (0, 0, at(*g)))), rows, cols


def _norm_mm(x, gain, b, *, b_kind, name, relu2=False):
    M, K = x.shape
    tm = min(M, MM_TILE)
    spec_of, rows, N = _weight_block(b, b_kind, False, min(MM_TILE, b.shape[-1]))
    tn = min(N, MM_TILE)
    assert rows == K and M % tm == 0 and N % tn == 0

    def body(x_ref, g_ref, b_ref, xn_ref, o_ref):
        @pl.when(pl.program_id(1) == 0)
        def _():
            xv = x_ref[...]
            r = lax.rsqrt(jnp.mean(xv * xv, axis=-1, keepdims=True) + EPS)
            xn_ref[...] = ((xv * r) * g_ref[...]).astype(xn_ref.dtype)

        v = jnp.dot(xn_ref[...], b_ref[...].astype(BF16).reshape(K, tn), preferred_element_type=F32)
        if relu2:
            r2 = jnp.maximum(v, 0.0)
            v = r2 * r2
        o_ref[...] = v.astype(o_ref.dtype)

    rows_spec = pl.BlockSpec((tm, K), lambda i, j: (i, 0))
    return pl.pallas_call(
        body, name=name, grid=(M // tm, N // tn),
        in_specs=[rows_spec, pl.BlockSpec((1, K), lambda i, j: (0, 0)), spec_of(lambda i, j: j)],
        out_specs=(rows_spec, pl.BlockSpec((tm, tn), lambda i, j: (i, j))),
        out_shape=(jax.ShapeDtypeStruct((M, K), BF16), jax.ShapeDtypeStruct((M, N), BF16 if relu2 else F32)),
        compiler_params=_params("parallel", "arbitrary"),
    )(x, gain.reshape(1, K), b)


def _mm_rms_bwd(a, b, x, gain, res, *, b_kind, name, also_bf16):
    M, K = a.shape
    spec_of, N, cols = _weight_block(b, b_kind, True, None)
    assert cols == K and x.shape == (M, N)
    tm = min(M, MM_TILE if K <= MM_TILE else MM_TILE // 2)
    assert M % tm == 0

    def body(a_ref, b_ref, x_ref, g_ref, res_ref, *outs):
        i = pl.program_id(0)
        dy = lax.dot_general(a_ref[...].astype(BF16), b_ref[...].astype(BF16).reshape(N, K), (((1,), (1,)), ((), ())),
                             preferred_element_type=F32)
        xv = x_ref[...]
        r = lax.rsqrt(jnp.mean(xv * xv, axis=-1, keepdims=True) + EPS)
        xh = xv * r
        part = jnp.sum(dy * xh, axis=0, keepdims=True)
        dg_ref = outs[-1]

        @pl.when(i == 0)
        def _():
            dg_ref[...] = part

        @pl.when(i > 0)
        def _():
            dg_ref[...] += part

        gdy = dy * g_ref[...]
        dx = res_ref[...] + r * (gdy - xh * jnp.mean(gdy * xh, axis=-1, keepdims=True))
        outs[0][...] = dx
        if also_bf16:
            outs[1][...] = dx.astype(BF16)

    row = pl.BlockSpec((tm, N), lambda i: (i, 0))
    vec = pl.BlockSpec((1, N), lambda i: (0, 0))
    out_specs = [row] + ([row] if also_bf16 else []) + [vec]
    out_shape = [jax.ShapeDtypeStruct((M, N), F32)] + ([jax.ShapeDtypeStruct((M, N), BF16)] if also_bf16 else [])
    outs = pl.pallas_call(
        body, name=name, grid=(M // tm,),
        in_specs=[pl.BlockSpec((tm, K), lambda i: (i, 0)), spec_of(None), row, vec, row], out_specs=out_specs,
        out_shape=out_shape + [jax.ShapeDtypeStruct((1, N), F32)], compiler_params=_params("arbitrary"),
    )(a, b, x, gain.reshape(1, N), res)
    return (*outs[:-1], outs[-1].reshape(N))


def _row_tile(rows, d):
    t = min(rows, (512 * 1024) // d)
    while rows % t or (t != rows and t % 16):
        t -= 1
    return t


def _rms_fwd(x, g, *, name, out_dtype=BF16):
    R, D = x.shape
    tr = _row_tile(R, D)

    def body(x_ref, g_ref, o_ref):
        xv = x_ref[...].astype(F32)
        r = lax.rsqrt(jnp.mean(xv * xv, axis=-1, keepdims=True) + EPS)
        o_ref[...] = ((xv * r) * g_ref[...]).astype(o_ref.dtype)

    return pl.pallas_call(
        body, name=name, grid=(R // tr,),
        in_specs=[pl.BlockSpec((tr, D), lambda i: (i, 0)), pl.BlockSpec((1, D), lambda i: (0, 0))],
        out_specs=pl.BlockSpec((tr, D), lambda i: (i, 0)), out_shape=jax.ShapeDtypeStruct((R, D), out_dtype),
        compiler_params=_params("parallel"),
    )(x, g.reshape(1, D))


def _rms_bwd(x, g, dys, *, name, res=None, want_dx=True, also_bf16=False):
    R, D = x.shape
    tr = _row_tile(R, D)
    n_dy = len(dys)
    has_res = res is not None

    def body(*refs):
        x_ref, g_ref = refs[0], refs[1]
        dy_refs = refs[2:2 + n_dy]
        res_ref = refs[2 + n_dy] if has_res else None
        outs = refs[2 + n_dy + (1 if has_res else 0):]
        dg_ref = outs[-1]
        i = pl.program_id(0)
        xv = x_ref[...].astype(F32)
        dy = dy_refs[0][...].astype(F32)
        for extra in dy_refs[1:]:
            dy = dy + extra[...].astype(F32)
        r = lax.rsqrt(jnp.mean(xv * xv, axis=-1, keepdims=True) + EPS)
        xh = xv * r
        part = jnp.sum(dy * xh, axis=0, keepdims=True)

        @pl.when(i == 0)
        def _():
            dg_ref[...] = part

        @pl.when(i > 0)
        def _():
            dg_ref[...] += part

        if want_dx:
            gdy = dy * g_ref[...]
            dx = r * (gdy - xh * jnp.mean(gdy * xh, axis=-1, keepdims=True))
            if has_res:
                dx = res_ref[...] + dx
            outs[0][...] = dx
            if also_bf16:
                outs[1][...] = dx.astype(BF16)

    row = pl.BlockSpec((tr, D), lambda i: (i, 0))
    vec = pl.BlockSpec((1, D), lambda i: (0, 0))
    out_shape = [jax.ShapeDtypeStruct((1, D), F32)]
    out_specs = [vec]
    if also_bf16:
        out_shape = [jax.ShapeDtypeStruct((R, D), BF16)] + out_shape
        out_specs = [row] + out_specs
    if want_dx:
        out_shape = [jax.ShapeDtypeStruct((R, D), F32)] + out_shape
        out_specs = [row] + out_specs
    outs = pl.pallas_call(
        body, name=name, grid=(R // tr,),
        in_specs=[row, vec] + [row] * (n_dy + (1 if has_res else 0)), out_specs=out_specs, out_shape=out_shape,
        compiler_params=_params("arbitrary"),
    )(x, g.reshape(1, D), *dys, *([res] if has_res else []))
    return (*outs[:-1], outs[-1].reshape(D)) if want_dx else outs[0].reshape(D)


def _add(a, b, *, name):
    R, D = a.shape
    tr = _row_tile(R, D)

    def body(a_ref, b_ref, o_ref):
        o_ref[...] = a_ref[...] + b_ref[...]

    row = pl.BlockSpec((tr, D), lambda i: (i, 0))
    return pl.pallas_call(body, name=name, grid=(R // tr,), in_specs=[row, row], out_specs=row,
                          out_shape=jax.ShapeDtypeStruct((R, D), a.dtype), compiler_params=_params("parallel"))(a, b)


POOL_TILE = 512


def _pool_window(group):
    return lax.shift_left(jnp.int32(2), group)


def _pool_diff(u_ref, halo_ref, group, tile):
    first = tile == 0
    halo = jnp.where(first, 0.0, halo_ref[...])
    ext = jnp.concatenate([halo, u_ref[...]], axis=0)
    n = ext.shape[0]
    s1 = ext + pltpu.roll(ext, 1, 0)
    s2 = s1 + pltpu.roll(s1, 2, 0)
    s3 = s2 + pltpu.roll(s2, 4, 0)
    s4 = s3 + pltpu.roll(s3, 8, 0)
    ws = jnp.where(group == 0, s1, jnp.where(group == 1, s2, jnp.where(group == 2, s3, s4)))[POOL_HALO:n]
    t = tile * POOL_TILE + lax.broadcasted_iota(jnp.int32, (POOL_TILE, 1), 0)
    cnt = jnp.minimum(t + 1, _pool_window(group)).astype(F32)
    return ws / cnt - u_ref[...], cnt


def _pool_specs():
    per_tile = POOL_TILE // POOL_HALO
    cur = pl.BlockSpec((None, POOL_TILE, POOL_GROUP_DIM), lambda g, i: (g, i, 0))
    prev = pl.BlockSpec((None, POOL_HALO, POOL_GROUP_DIM), lambda g, i: (g, jnp.maximum(i * per_tile - 1, 0), 0))
    pw = pl.BlockSpec((None, POOL_GROUP_DIM, POOL_GROUP_DIM), lambda g, i: (g, 0, 0))
    vec = pl.BlockSpec((None, 1, POOL_GROUP_DIM), lambda g, i: (g, 0, 0))
    return cur, prev, pw, vec


def _pool_fwd(u, pw, scale, *, name):
    G, T, C = u.shape
    assert T % POOL_TILE == 0
    cur, prev, pw_spec, vec = _pool_specs()

    def body(u_ref, halo_ref, pw_ref, sc_ref, o_ref):
        d, _ = _pool_diff(u_ref, halo_ref, pl.program_id(0), pl.program_id(1))
        mixed = jnp.dot(d.astype(BF16), pw_ref[...].astype(BF16), preferred_element_type=F32)
        o_ref[...] = (mixed * sc_ref[...]).astype(o_ref.dtype)

    return pl.pallas_call(
        body, name=name, grid=(G, T // POOL_TILE), in_specs=[cur, prev, pw_spec, vec], out_specs=cur,
        out_shape=jax.ShapeDtypeStruct((G, T, C), BF16), compiler_params=_params("parallel", "parallel"),
    )(u, u, pw, scale)


def _pool_bwd(u, pw, scale, dout, *, name):
    G, T, C = u.shape
    nt = T // POOL_TILE
    per_tile = POOL_TILE // POOL_HALO
    cur, prev, pw_spec, vec = _pool_specs()
    nxt = pl.BlockSpec((None, POOL_HALO, C), lambda g, i: (g, jnp.minimum((i + 1) * per_tile, nt * per_tile - 1), 0))

    def body(u_ref, halo_ref, pw_ref, sc_ref, do_ref, donext_ref, du_ref, dpw_ref, dsc_ref):
        group, tile = pl.program_id(0), pl.program_id(1)
        d, cnt = _pool_diff(u_ref, halo_ref, group, tile)
        pwb = pw_ref[...].astype(BF16)
        db = d.astype(BF16)
        mixed = jnp.dot(db, pwb, preferred_element_type=F32)
        dout = do_ref[...].astype(F32)
        dsc = jnp.sum(dout * mixed, axis=0, keepdims=True)
        sc = sc_ref[...]
        dmix = (dout * sc).astype(BF16)
        dpw = lax.dot_general(db, dmix, (((0,), (0,)), ((), ())), preferred_element_type=F32)

        @pl.when(tile == 0)
        def _():
            dpw_ref[...] = dpw
            dsc_ref[...] = dsc

        @pl.when(tile > 0)
        def _():
            dpw_ref[...] += dpw
            dsc_ref[...] += dsc

        last = tile == nt - 1
        dnext = jnp.where(last, 0.0, donext_ref[...].astype(F32))
        dmix_ext = jnp.concatenate([dmix, (dnext * sc).astype(BF16)], axis=0)
        dd_ext = lax.dot_general(dmix_ext, pwb, (((1,), (1,)), ((), ())), preferred_element_type=F32)
        window = _pool_window(group).astype(F32)
        cnt_ext = jnp.concatenate([cnt, jnp.broadcast_to(window, (POOL_HALO, 1))], axis=0)
        q = dd_ext / cnt_ext
        n = q.shape[0]
        r1 = q + pltpu.roll(q, n - 1, 0)
        r2 = r1 + pltpu.roll(r1, n - 2, 0)
        r3 = r2 + pltpu.roll(r2, n - 4, 0)
        r4 = r3 + pltpu.roll(r3, n - 8, 0)
        back = jnp.where(group == 0, r1, jnp.where(group == 1, r2, jnp.where(group == 2, r3, r4)))
        du_ref[...] = back[0:POOL_TILE] - dd_ext[0:POOL_TILE]

    return pl.pallas_call(
        body, name=name, grid=(G, nt), in_specs=[cur, prev, pw_spec, vec, cur, nxt],
        out_specs=(cur, pw_spec, vec),
        out_shape=(jax.ShapeDtypeStruct((G, T, C), F32), jax.ShapeDtypeStruct((G, C, C), F32),
                   jax.ShapeDtypeStruct((G, 1, C), F32)),
        compiler_params=_params("arbitrary", "arbitrary"),
    )(u, u, pw, scale, dout, dout)


def _softmax(q, k, bias, valid, sink):
    s = lax.dot_general(q, k, (((1,), (1,)), ((), ())), preferred_element_type=F32) * SCALE
    if bias is not None:
        s = s - bias
    if valid is not None:
        s = jnp.where(valid, s, NEG)
    m = jnp.max(s, axis=-1, keepdims=True)
    if sink is not None:
        m = jnp.maximum(m, sink)
    e = jnp.exp(s - m)
    z = jnp.sum(e, axis=-1, keepdims=True)
    if sink is None:
        return e * (1.0 / z), None
    es = jnp.exp(sink - m)
    inv = 1.0 / (z + es)
    return e * inv, es * inv


def _swa_terms(sink_ref, kvh, blk):
    rows = SWA_GROUP * BLOCK
    row = lax.broadcasted_iota(jnp.int32, (rows, 1), 0)
    grp = row // BLOCK
    head = (kvh * SWA_GROUP + grp + 1).astype(F32)
    slope = jnp.exp(head * (-8.0 * math.log(2.0) / SWA_Q_HEADS))
    qi = lax.broadcasted_iota(jnp.int32, (rows, 2 * BLOCK), 0) % BLOCK
    kj = lax.broadcasted_iota(jnp.int32, (rows, 2 * BLOCK), 1)
    dist = qi + BLOCK - kj
    valid = (dist >= 0) & (dist < BLOCK) & ((blk > 0) | (kj >= BLOCK))
    bias = slope * dist.astype(F32)
    s0, s1, s2 = (sink_ref[kvh * SWA_GROUP + g] for g in range(SWA_GROUP))
    sink = jnp.where(grp == 0, s0, jnp.where(grp == 1, s1, s2))
    return bias, valid, sink, grp


def _swa_fwd(q, k, v, sinks, *, name):
    H, G, T, hd = q.shape
    nb = T // BLOCK
    rows = G * BLOCK

    def body(sink_ref, q_ref, kp_ref, kc_ref, vp_ref, vc_ref, o_ref):
        kvh, blk = pl.program_id(0), pl.program_id(1)
        bias, valid, sink, _ = _swa_terms(sink_ref, kvh, blk)
        kk = jnp.concatenate([kp_ref[...], kc_ref[...]], axis=0)
        vv = jnp.concatenate([vp_ref[...], vc_ref[...]], axis=0)
        p, _ = _softmax(q_ref[...].reshape(rows, hd), kk, bias, valid, sink)
        o = jnp.dot(p.astype(BF16), vv, preferred_element_type=F32)
        o_ref[...] = o.reshape(G, BLOCK, hd).astype(o_ref.dtype)

    qs = pl.BlockSpec((None, G, BLOCK, hd), lambda h, n: (h, 0, n, 0))
    prev = pl.BlockSpec((None, BLOCK, hd), lambda h, n: (h, jnp.maximum(n - 1, 0), 0))
    cur = pl.BlockSpec((None, BLOCK, hd), lambda h, n: (h, n, 0))
    return pl.pallas_call(
        body, name=name, grid=(H, nb),
        in_specs=[pl.BlockSpec(memory_space=pltpu.SMEM), qs, prev, cur, prev, cur], out_specs=qs,
        out_shape=jax.ShapeDtypeStruct((H, G, T, hd), BF16), compiler_params=_params("parallel", "parallel"),
    )(sinks, q, k, k, v, v)


def _swa_bwd(q, k, v, sinks, do, *, name):
    H, G, T, hd = q.shape
    nb = T // BLOCK
    rows = G * BLOCK

    def body(sink_ref, q_ref, do_ref, kp_ref, kc_ref, vp_ref, vc_ref, dq_ref, dk_ref, dv_ref, ds_ref, ck, cv):
        kvh, blk = pl.program_id(0), pl.program_id(1)

        @pl.when(blk == 0)
        def _():
            ck[...] = jnp.zeros_like(ck)
            cv[...] = jnp.zeros_like(cv)
            ds_ref[...] = jnp.zeros_like(ds_ref)

        @pl.when(blk < nb)
        def _():
            bias, valid, sink, grp = _swa_terms(sink_ref, kvh, blk)
            kk = jnp.concatenate([kp_ref[...], kc_ref[...]], axis=0)
            vv = jnp.concatenate([vp_ref[...], vc_ref[...]], axis=0)
            qq = q_ref[...].reshape(rows, hd)
            dout = do_ref[...].reshape(rows, hd)
            p, ps = _softmax(qq, kk, bias, valid, sink)
            dp = lax.dot_general(dout, vv, (((1,), (1,)), ((), ())), preferred_element_type=F32)
            dsum = jnp.sum(p * dp, axis=-1, keepdims=True)
            ds = (p * (dp - dsum)).astype(BF16)
            dq = jnp.dot(ds, kk, preferred_element_type=F32) * SCALE
            dq_ref[...] = dq.reshape(G, BLOCK, hd)
            dk = lax.dot_general(ds, qq, (((0,), (0,)), ((), ())), preferred_element_type=F32) * SCALE
            dv = lax.dot_general(p.astype(BF16), dout, (((0,), (0,)), ((), ())), preferred_element_type=F32)
            dk_ref[...] = ck[...] + dk[0:BLOCK]
            dv_ref[...] = cv[...] + dv[0:BLOCK]
            ck[...] = dk[BLOCK:2 * BLOCK]
            cv[...] = dv[BLOCK:2 * BLOCK]
            dsink = -(ps * dsum)
            lane = lax.broadcasted_iota(jnp.int32, (1, 128), 1)
            acc = jnp.zeros((1, 128), F32)
            for g in range(G):
                acc = acc + jnp.where(lane == g, jnp.sum(jnp.where(grp == g, dsink, 0.0)), 0.0)
            ds_ref[...] += acc

        @pl.when(blk == nb)
        def _():
            dk_ref[...] = ck[...]
            dv_ref[...] = cv[...]

    def at(n):
        return jnp.minimum(n, nb - 1)

    qs = pl.BlockSpec((None, G, BLOCK, hd), lambda h, n: (h, 0, at(n), 0))
    prev = pl.BlockSpec((None, BLOCK, hd), lambda h, n: (h, jnp.maximum(at(n) - 1, 0), 0))
    cur = pl.BlockSpec((None, BLOCK, hd), lambda h, n: (h, at(n), 0))
    late = pl.BlockSpec((None, BLOCK, hd), lambda h, n: (h, jnp.maximum(n - 1, 0), 0))
    dsink_spec = pl.BlockSpec((None, 1, 128), lambda h, n: (h, 0, 0))
    return pl.pallas_call(
        body, name=name, grid=(H, nb + 1),
        in_specs=[pl.BlockSpec(memory_space=pltpu.SMEM), qs, qs, prev, cur, prev, cur],
        out_specs=(qs, late, late, dsink_spec),
        out_shape=(jax.ShapeDtypeStruct((H, G, T, hd), F32), jax.ShapeDtypeStruct((H, T, hd), F32),
                   jax.ShapeDtypeStruct((H, T, hd), F32), jax.ShapeDtypeStruct((H, 1, 128), F32)),
        scratch_shapes=[pltpu.VMEM((BLOCK, hd), F32), pltpu.VMEM((BLOCK, hd), F32)],
        compiler_params=_params("arbitrary", "arbitrary"),
    )(sinks, q, do, k, k, v, v)


MEM_Q_TILE = 512


def _mem_fwd(q, k, v, *, name):
    H, T, hd = q.shape
    M = k.shape[1]
    tq = min(T, MEM_Q_TILE)

    def body(q_ref, k_ref, v_ref, o_ref):
        p, _ = _softmax(q_ref[...], k_ref[...], None, None, None)
        o_ref[...] = jnp.dot(p.astype(BF16), v_ref[...], preferred_element_type=F32).astype(o_ref.dtype)

    qs = pl.BlockSpec((None, tq, hd), lambda h, i: (h, i, 0))
    ks = pl.BlockSpec((None, M, hd), lambda h, i: (h, 0, 0))
    return pl.pallas_call(body, name=name, grid=(H, T // tq), in_specs=[qs, ks, ks], out_specs=qs,
                          out_shape=jax.ShapeDtypeStruct((H, T, hd), BF16),
                          compiler_params=_params("parallel", "parallel"))(q, k, v)


def _mem_bwd(q, k, v, do, *, name):
    H, T, hd = q.shape
    M = k.shape[1]
    tq = min(T, MEM_Q_TILE)

    def body(q_ref, do_ref, k_ref, v_ref, dq_ref, dk_ref, dv_ref):
        i = pl.program_id(1)
        qq, kk, vv, dout = q_ref[...], k_ref[...], v_ref[...], do_ref[...]
        p, _ = _softmax(qq, kk, None, None, None)
        dp = lax.dot_general(dout, vv, (((1,), (1,)), ((), ())), preferred_element_type=F32)
        dsum = jnp.sum(p * dp, axis=-1, keepdims=True)
        ds = (p * (dp - dsum)).astype(BF16)
        dq_ref[...] = jnp.dot(ds, kk, preferred_element_type=F32) * SCALE
        dk = lax.dot_general(ds, qq, (((0,), (0,)), ((), ())), preferred_element_type=F32) * SCALE
        dv = lax.dot_general(p.astype(BF16), dout, (((0,), (0,)), ((), ())), preferred_element_type=F32)

        @pl.when(i == 0)
        def _():
            dk_ref[...] = dk
            dv_ref[...] = dv

        @pl.when(i > 0)
        def _():
            dk_ref[...] += dk
            dv_ref[...] += dv

    qs = pl.BlockSpec((None, tq, hd), lambda h, i: (h, i, 0))
    ks = pl.BlockSpec((None, M, hd), lambda h, i: (h, 0, 0))
    return pl.pallas_call(
        body, name=name, grid=(H, T // tq), in_specs=[qs, qs, ks, ks], out_specs=(qs, ks, ks),
        out_shape=(jax.ShapeDtypeStruct((H, T, hd), F32), jax.ShapeDtypeStruct((H, M, hd), F32),
                   jax.ShapeDtypeStruct((H, M, hd), F32)),
        compiler_params=_params("arbitrary", "arbitrary"),
    )(q, do, k, v)


LANES = 128


def _seg_mean(v):
    r = lax.broadcasted_iota(jnp.int32, (LANES, LANES), 0) // HEAD_DIM
    c = lax.broadcasted_iota(jnp.int32, (LANES, LANES), 1) // HEAD_DIM
    seg = jnp.where(r == c, 1.0 / HEAD_DIM, 0.0).astype(BF16)
    hi = v.astype(BF16)
    lo = (v - hi.astype(F32)).astype(BF16)
    parts = []
    for g in range(v.shape[1] // LANES):
        sl = slice(g * LANES, (g + 1) * LANES)
        parts.append(jnp.dot(hi[:, sl], seg, preferred_element_type=F32) + jnp.dot(lo[:, sl], seg, preferred_element_type=F32))
    return parts[0] if len(parts) == 1 else jnp.concatenate(parts, axis=1)


def _cols(rows, width, col):
    return pl.BlockSpec((rows, width), lambda i: (i, col))


def _head_gain(g, heads):
    return jnp.tile(g, heads).reshape(1, heads * HEAD_DIM)


def _fold_heads(dg, heads):
    return dg.reshape(heads, HEAD_DIM).sum(axis=0)


def _seg_rms_fwd(x, gain, *, width, col, name):
    R = x.shape[0]
    tr = _row_tile(R, width)

    def body(x_ref, g_ref, o_ref):
        xv = x_ref[...]
        r = lax.rsqrt(_seg_mean(xv * xv) + EPS)
        o_ref[...] = ((xv * r) * g_ref[...]).astype(o_ref.dtype)

    return pl.pallas_call(
        body, name=name, grid=(R // tr,), in_specs=[_cols(tr, width, col), pl.BlockSpec((1, width), lambda i: (0, 0))],
        out_specs=_cols(tr, width, 0), out_shape=jax.ShapeDtypeStruct((R, width), BF16), compiler_params=_params("parallel"),
    )(x, gain)


def _seg_rms_bwd(x, gain, dys, *, width, col, name, out_buf=None, out_col=0):
    R = x.shape[0]
    tr = _row_tile(R, width)
    n_dy = len(dys)

    def body(*refs):
        x_ref, g_ref = refs[0], refs[1]
        dy_refs = refs[2:2 + n_dy]
        dx_ref, dg_ref = refs[-2], refs[-1]
        i = pl.program_id(0)
        xv = x_ref[...]
        dy = dy_refs[0][...]
        for extra in dy_refs[1:]:
            dy = dy + extra[...]
        r = lax.rsqrt(_seg_mean(xv * xv) + EPS)
        xh = xv * r
        part = jnp.sum(dy * xh, axis=0, keepdims=True)

        @pl.when(i == 0)
        def _():
            dg_ref[...] = part

        @pl.when(i > 0)
        def _():
            dg_ref[...] += part

        gdy = dy * g_ref[...]
        dx_ref[...] = (r * (gdy - xh * _seg_mean(gdy * xh))).astype(dx_ref.dtype)

    vec = pl.BlockSpec((1, width), lambda i: (0, 0))
    in_specs = [_cols(tr, width, col), vec] + [_cols(tr, width, 0)] * n_dy
    operands = [x, gain, *dys]
    aliases = {}
    dx_shape = jax.ShapeDtypeStruct((R, width), BF16)
    if out_buf is not None:
        in_specs.append(ANY)
        operands.append(out_buf)
        aliases = {len(operands) - 1: 0}
        dx_shape = jax.ShapeDtypeStruct(out_buf.shape, out_buf.dtype)
    return pl.pallas_call(
        body, name=name, grid=(R // tr,), in_specs=in_specs, out_specs=(_cols(tr, width, out_col), vec),
        out_shape=(dx_shape, jax.ShapeDtypeStruct((1, width), F32)), input_output_aliases=aliases,
        compiler_params=_params("arbitrary"),
    )(*operands)


def _sum_into(a, b, out_buf, out_col, *, name):
    R, width = a.shape
    tr = _row_tile(R, width)

    def body(a_ref, b_ref, _, o_ref):
        o_ref[...] = (a_ref[...] + b_ref[...]).astype(o_ref.dtype)

    return pl.pallas_call(
        body, name=name, grid=(R // tr,), in_specs=[_cols(tr, width, 0), _cols(tr, width, 0), ANY],
        out_specs=_cols(tr, width, out_col), out_shape=jax.ShapeDtypeStruct(out_buf.shape, out_buf.dtype),
        input_output_aliases={2: 0}, compiler_params=_params("parallel"),
    )(a, b, out_buf)


def _pool_lane_group():
    return lax.broadcasted_iota(jnp.int32, (1, MAIN_WIDTH), 1) // POOL_GROUP_DIM


def _pool_pick(group, per_window):
    s1, s2, s3, s4 = per_window
    return jnp.where(group == 0, s1, jnp.where(group == 1, s2, jnp.where(group == 2, s3, s4)))


def _pool_delta(u_ref, halo_ref, tile):
    group = _pool_lane_group()
    halo = jnp.where(tile == 0, 0.0, halo_ref[...])
    ext = jnp.concatenate([halo, u_ref[...]], axis=0)
    n = ext.shape[0]
    s1 = ext + pltpu.roll(ext, 1, 0)
    s2 = s1 + pltpu.roll(s1, 2, 0)
    s3 = s2 + pltpu.roll(s2, 4, 0)
    s4 = s3 + pltpu.roll(s3, 8, 0)
    ws = _pool_pick(group, (s1, s2, s3, s4))[POOL_HALO:n]
    t = tile * POOL_TILE + lax.broadcasted_iota(jnp.int32, (POOL_TILE, 1), 0)
    cnt = jnp.minimum(t + 1, _pool_pick(group, (2, 4, 8, 16))).astype(F32)
    return ws / cnt - u_ref[...], cnt


def _pool_in_specs():
    per_tile = POOL_TILE // POOL_HALO
    cur = _cols(POOL_TILE, MAIN_WIDTH, 0)
    prev = pl.BlockSpec((POOL_HALO, MAIN_WIDTH), lambda i: (jnp.maximum(i * per_tile - 1, 0), 0))
    mix = pl.BlockSpec((MAIN_WIDTH, MAIN_WIDTH), lambda i: (0, 0))
    vec = pl.BlockSpec((1, MAIN_WIDTH), lambda i: (0, 0))
    return cur, prev, mix, vec


def _pool_mix_fwd(proj, mix, scale, cat, *, name):
    T = proj.shape[0]
    assert T % POOL_TILE == 0
    cur, prev, mix_spec, vec = _pool_in_specs()

    def body(u_ref, halo_ref, mix_ref, sc_ref, _, o_ref):
        d, _cnt = _pool_delta(u_ref, halo_ref, pl.program_id(0))
        mixed = jnp.dot(d.astype(BF16), mix_ref[...].astype(BF16), preferred_element_type=F32)
        o_ref[...] = (mixed * sc_ref[...]).astype(o_ref.dtype)

    return pl.pallas_call(
        body, name=name, grid=(T // POOL_TILE,), in_specs=[cur, prev, mix_spec, vec, ANY], out_specs=cur,
        out_shape=jax.ShapeDtypeStruct(cat.shape, cat.dtype), input_output_aliases={4: 0}, compiler_params=_params("parallel"),
    )(proj, proj, mix, scale, cat)


def _pool_mix_bwd(proj, mix, scale, dcat, *, name):
    T = proj.shape[0]
    nt = T // POOL_TILE
    per_tile = POOL_TILE // POOL_HALO
    cur, prev, mix_spec, vec = _pool_in_specs()
    nxt = pl.BlockSpec((POOL_HALO, MAIN_WIDTH), lambda i: (jnp.minimum((i + 1) * per_tile, nt * per_tile - 1), 0))

    def body(u_ref, halo_ref, mix_ref, sc_ref, do_ref, donext_ref, du_ref, dmix_ref, dsc_ref):
        tile = pl.program_id(0)
        group = _pool_lane_group()
        d, cnt = _pool_delta(u_ref, halo_ref, tile)
        mixb = mix_ref[...].astype(BF16)
        db = d.astype(BF16)
        mixed = jnp.dot(db, mixb, preferred_element_type=F32)
        dout = do_ref[...]
        dsc = jnp.sum(dout * mixed, axis=0, keepdims=True)
        sc = sc_ref[...]
        dmixed = (dout * sc).astype(BF16)
        dmix = lax.dot_general(db, dmixed, (((0,), (0,)), ((), ())), preferred_element_type=F32)

        @pl.when(tile == 0)
        def _():
            dmix_ref[...] = dmix
            dsc_ref[...] = dsc

        @pl.when(tile > 0)
        def _():
            dmix_ref[...] += dmix
            dsc_ref[...] += dsc

        dnext = jnp.where(tile == nt - 1, 0.0, donext_ref[...])
        dmixed_ext = jnp.concatenate([dmixed, (dnext * sc).astype(BF16)], axis=0)
        dd_ext = lax.dot_general(dmixed_ext, mixb, (((1,), (1,)), ((), ())), preferred_element_type=F32)
        window = _pool_pick(group, (2.0, 4.0, 8.0, 16.0))
        cnt_ext = jnp.concatenate([cnt, jnp.broadcast_to(window, (POOL_HALO, MAIN_WIDTH))], axis=0)
        q = dd_ext / cnt_ext
        n = q.shape[0]
        r1 = q + pltpu.roll(q, n - 1, 0)
        r2 = r1 + pltpu.roll(r1, n - 2, 0)
        r3 = r2 + pltpu.roll(r2, n - 4, 0)
        r4 = r3 + pltpu.roll(r3, n - 8, 0)
        back = _pool_pick(group, (r1, r2, r3, r4))
        du_ref[...] = (back[0:POOL_TILE] - dd_ext[0:POOL_TILE]).astype(du_ref.dtype)

    return pl.pallas_call(
        body, name=name, grid=(nt,), in_specs=[cur, prev, mix_spec, vec, cur, nxt], out_specs=(cur, mix_spec, vec),
        out_shape=(jax.ShapeDtypeStruct((T, D_MODEL), BF16), jax.ShapeDtypeStruct((MAIN_WIDTH, MAIN_WIDTH), F32),
                   jax.ShapeDtypeStruct((1, MAIN_WIDTH), F32)),
        compiler_params=_params("arbitrary"),
    )(proj, proj, mix, scale, dcat, dcat)


def _head(a, h):
    return a[:, h * HEAD_DIM:(h + 1) * HEAD_DIM]


def _swa_mask(blk):
    rows = SWA_GROUP * BLOCK
    qi = lax.broadcasted_iota(jnp.int32, (rows, 2 * BLOCK), 0) % BLOCK
    kj = lax.broadcasted_iota(jnp.int32, (rows, 2 * BLOCK), 1)
    dist = qi + BLOCK - kj
    valid = (dist >= 0) & (dist < BLOCK) & ((blk > 0) | (kj >= BLOCK))
    return dist.astype(F32), valid


def _swa_head_terms(sink_ref, kvh, dist):
    grp = lax.broadcasted_iota(jnp.int32, (SWA_GROUP * BLOCK, 1), 0) // BLOCK
    slopes = [2.0 ** (-8.0 * (kvh * SWA_GROUP + g + 1) / SWA_Q_HEADS) for g in range(SWA_GROUP)]
    sinks = [sink_ref[kvh * SWA_GROUP + g] for g in range(SWA_GROUP)]
    slope = jnp.where(grp == 0, slopes[0], jnp.where(grp == 1, slopes[1], slopes[2]))
    sink = jnp.where(grp == 0, sinks[0], jnp.where(grp == 1, sinks[1], sinks[2]))
    return slope * dist, sink


def _stack_heads(a, kvh):
    return jnp.concatenate([_head(a, kvh * SWA_GROUP + g) for g in range(SWA_GROUP)], axis=0)


def _swa_specs(nb):
    def at(n):
        return jnp.minimum(n, nb - 1)

    q = pl.BlockSpec((BLOCK, MAIN_WIDTH), lambda n: (at(n), 0))
    k_prev = pl.BlockSpec((BLOCK, KV_HALF), lambda n: (jnp.maximum(at(n) - 1, 0), 0))
    k_cur = pl.BlockSpec((BLOCK, KV_HALF), lambda n: (at(n), 0))
    v_prev = pl.BlockSpec((BLOCK, KV_HALF), lambda n: (jnp.maximum(at(n) - 1, 0), 1))
    v_cur = pl.BlockSpec((BLOCK, KV_HALF), lambda n: (at(n), 1))
    return q, k_prev, k_cur, v_prev, v_cur


def _swa_attn_fwd(qn, kn, kv, sinks, cat, *, name):
    T = qn.shape[0]
    nb = T // BLOCK
    q_spec, k_prev, k_cur, v_prev, v_cur = _swa_specs(nb)

    def body(sink_ref, q_ref, kp_ref, kc_ref, vp_ref, vc_ref, _, o_ref):
        dist, valid = _swa_mask(pl.program_id(0))
        kk = jnp.concatenate([kp_ref[...], kc_ref[...]], axis=0)
        vv = jnp.concatenate([vp_ref[...], vc_ref[...]], axis=0).astype(BF16)
        q = q_ref[...]
        outs = []
        for kvh in range(SWA_KV_HEADS):
            bias, sink = _swa_head_terms(sink_ref, kvh, dist)
            p, _ps = _softmax(_stack_heads(q, kvh), _head(kk, kvh), bias, valid, sink)
            o = jnp.dot(p.astype(BF16), _head(vv, kvh), preferred_element_type=F32)
            outs += [o[g * BLOCK:(g + 1) * BLOCK] for g in range(SWA_GROUP)]
        o_ref[...] = jnp.concatenate(outs, axis=1).astype(o_ref.dtype)

    return pl.pallas_call(
        body, name=name, grid=(nb,),
        in_specs=[pl.BlockSpec(memory_space=pltpu.SMEM), q_spec, k_prev, k_cur, v_prev, v_cur, ANY], out_specs=q_spec,
        out_shape=jax.ShapeDtypeStruct(cat.shape, cat.dtype), input_output_aliases={6: 0}, compiler_params=_params("parallel"),
    )(sinks, qn, kn, kn, kv, kv, cat)


def _swa_attn_bwd(qn, kn, kv, sinks, dcat, dqn, *, name):
    T = qn.shape[0]
    nb = T // BLOCK
    q_spec, k_prev, k_cur, v_prev, v_cur = _swa_specs(nb)
    late = pl.BlockSpec((BLOCK, KV_HALF), lambda n: (jnp.maximum(n - 1, 0), 0))
    tn_dims = (((0,), (0,)), ((), ()))

    def body(sink_ref, q_ref, do_ref, kp_ref, kc_ref, vp_ref, vc_ref, _, dq_ref, dk_ref, dv_ref, ds_ref, ck, cv):
        blk = pl.program_id(0)

        @pl.when(blk == 0)
        def _():
            ck[...] = jnp.zeros_like(ck)
            cv[...] = jnp.zeros_like(cv)
            ds_ref[...] = jnp.zeros_like(ds_ref)

        @pl.when(blk < nb)
        def _():
            dist, valid = _swa_mask(blk)
            kk = jnp.concatenate([kp_ref[...], kc_ref[...]], axis=0)
            vv = jnp.concatenate([vp_ref[...], vc_ref[...]], axis=0).astype(BF16)
            q = q_ref[...]
            dout = do_ref[...].astype(BF16)
            lane = lax.broadcasted_iota(jnp.int32, (1, LANES), 1)
            dsinks = jnp.zeros((1, LANES), F32)
            dqs, dks, dvs = [], [], []
            for kvh in range(SWA_KV_HEADS):
                bias, sink = _swa_head_terms(sink_ref, kvh, dist)
                qq, kh, vh, dd = _stack_heads(q, kvh), _head(kk, kvh), _head(vv, kvh), _stack_heads(dout, kvh)
                p, ps = _softmax(qq, kh, bias, valid, sink)
                dp = lax.dot_general(dd, vh, (((1,), (1,)), ((), ())), preferred_element_type=F32)
                dsum = jnp.sum(p * dp, axis=-1, keepdims=True)
                ds = (p * (dp - dsum)).astype(BF16)
                dq = jnp.dot(ds, kh, preferred_element_type=F32) * SCALE
                dqs += [dq[g * BLOCK:(g + 1) * BLOCK] for g in range(SWA_GROUP)]
                dks.append(lax.dot_general(ds, qq, tn_dims, preferred_element_type=F32) * SCALE)
                dvs.append(lax.dot_general(p.astype(BF16), dd, tn_dims, preferred_element_type=F32))
                dsink = -(ps * dsum)
                for g in range(SWA_GROUP):
                    dsinks = dsinks + jnp.where(lane == kvh * SWA_GROUP + g, jnp.sum(dsink[g * BLOCK:(g + 1) * BLOCK]), 0.0)
            dq_ref[...] = jnp.concatenate(dqs, axis=1)
            dk = jnp.concatenate(dks, axis=1)
            dv = jnp.concatenate(dvs, axis=1)
            dk_ref[...] = ck[...] + dk[0:BLOCK]
            dv_ref[...] = cv[...] + dv[0:BLOCK]
            ck[...] = dk[BLOCK:2 * BLOCK]
            cv[...] = dv[BLOCK:2 * BLOCK]
            ds_ref[...] += dsinks

        @pl.when(blk == nb)
        def _():
            dk_ref[...] = ck[...]
            dv_ref[...] = cv[...]

    return pl.pallas_call(
        body, name=name, grid=(nb + 1,),
        in_specs=[pl.BlockSpec(memory_space=pltpu.SMEM), q_spec, q_spec, k_prev, k_cur, v_prev, v_cur, ANY],
        out_specs=(q_spec, late, late, pl.BlockSpec((1, LANES), lambda n: (0, 0))),
        out_shape=(jax.ShapeDtypeStruct(dqn.shape, dqn.dtype), jax.ShapeDtypeStruct((T, KV_HALF), F32),
                   jax.ShapeDtypeStruct((T, KV_HALF), F32), jax.ShapeDtypeStruct((1, LANES), F32)),
        scratch_shapes=[pltpu.VMEM((BLOCK, KV_HALF), F32), pltpu.VMEM((BLOCK, KV_HALF), F32)],
        input_output_aliases={7: 0}, compiler_params=_params("arbitrary"),
    )(sinks, qn, dcat, kn, kn, kv, kv, dqn)


def _mem_specs(M, tq, q_col):
    q = _cols(tq, MEM_WIDTH, q_col)
    k = pl.BlockSpec((M, MEM_WIDTH), lambda i: (0, 0))
    v = pl.BlockSpec((M, MEM_WIDTH), lambda i: (0, 1))
    return q, k, v


def _mem_attn_fwd(q, q_col, mkn, mkv, *, name):
    T = q.shape[0]
    M = mkn.shape[0]
    tq = min(T, MEM_Q_TILE)
    q_spec, k_spec, v_spec = _mem_specs(M, tq, q_col)

    def body(q_ref, k_ref, v_ref, o_ref):
        qq, kk, vv = q_ref[...], k_ref[...], v_ref[...].astype(BF16)
        outs = []
        for h in range(MEM_HEADS):
            p, _ps = _softmax(_head(qq, h), _head(kk, h), None, None, None)
            outs.append(jnp.dot(p.astype(BF16), _head(vv, h), preferred_element_type=F32))
        o_ref[...] = jnp.concatenate(outs, axis=1).astype(o_ref.dtype)

    return pl.pallas_call(
        body, name=name, grid=(T // tq,), in_specs=[q_spec, k_spec, v_spec], out_specs=_cols(tq, MEM_WIDTH, MAIN_WIDTH // MEM_WIDTH),
        out_shape=jax.ShapeDtypeStruct((T, D_MODEL), BF16), compiler_params=_params("parallel"),
    )(q, mkn, mkv)


def _mem_attn_bwd(q, q_col, mkn, mkv, dcat, *, dq_width, name):
    T = q.shape[0]
    M = mkn.shape[0]
    tq = min(T, MEM_Q_TILE)
    q_spec, k_spec, v_spec = _mem_specs(M, tq, q_col)
    last = MAIN_WIDTH // MEM_WIDTH
    tn_dims = (((0,), (0,)), ((), ()))

    def body(q_ref, do_ref, k_ref, v_ref, dq_ref, dk_ref, dv_ref):
        i = pl.program_id(0)
        qq, kk, vv, dout = q_ref[...], k_ref[...], v_ref[...].astype(BF16), do_ref[...].astype(BF16)
        dqs, dks, dvs = [], [], []
        for h in range(MEM_HEADS):
            qh, kh, vh, dh = _head(qq, h), _head(kk, h), _head(vv, h), _head(dout, h)
            p, _ps = _softmax(qh, kh, None, None, None)
            dp = lax.dot_general(dh, vh, (((1,), (1,)), ((), ())), preferred_element_type=F32)
            dsum = jnp.sum(p * dp, axis=-1, keepdims=True)
            ds = (p * (dp - dsum)).astype(BF16)
            dqs.append(jnp.dot(ds, kh, preferred_element_type=F32) * SCALE)
            dks.append(lax.dot_general(ds, qh, tn_dims, preferred_element_type=F32) * SCALE)
            dvs.append(lax.dot_general(p.astype(BF16), dh, tn_dims, preferred_element_type=F32))
        dq_ref[...] = jnp.concatenate(dqs, axis=1)
        dk = jnp.concatenate(dks, axis=1)
        dv = jnp.concatenate(dvs, axis=1)

        @pl.when(i == 0)
        def _():
            dk_ref[...] = dk
            dv_ref[...] = dv

        @pl.when(i > 0)
        def _():
            dk_ref[...] += dk
            dv_ref[...] += dv

    acc = pl.BlockSpec((M, MEM_WIDTH), lambda i: (0, 0))
    return pl.pallas_call(
        body, name=name, grid=(T // tq,), in_specs=[q_spec, _cols(tq, MEM_WIDTH, last), k_spec, v_spec],
        out_specs=(_cols(tq, MEM_WIDTH, dq_width // MEM_WIDTH - 1), acc, acc),
        out_shape=(jax.ShapeDtypeStruct((T, dq_width), F32), jax.ShapeDtypeStruct((M, MEM_WIDTH), F32),
                   jax.ShapeDtypeStruct((M, MEM_WIDTH), F32)),
        compiler_params=_params("arbitrary"),
    )(q, dcat, mkn, mkv)


def _loss(y, target, *, name):
    T, D = y.shape
    tr = _row_tile(T, D)

    def body(y_ref, t_ref, l_ref, dy_ref, dyb_ref):
        i = pl.program_id(0)
        err = y_ref[...] - t_ref[...]
        dy = err / float(D)
        dy_ref[...] = dy
        dyb_ref[...] = dy.astype(BF16)
        part = jnp.full((8, 128), 0.5 * jnp.sum(jnp.mean(err * err, axis=-1)), F32)

        @pl.when(i == 0)
        def _():
            l_ref[...] = part

        @pl.when(i > 0)
        def _():
            l_ref[...] += part

    row = pl.BlockSpec((tr, D), lambda i: (i, 0))
    return pl.pallas_call(
        body, name=name, grid=(T // tr,), in_specs=[row, row],
        out_specs=(pl.BlockSpec((8, 128), lambda i: (0, 0)), row, row),
        out_shape=(jax.ShapeDtypeStruct((8, 128), F32), jax.ShapeDtypeStruct((T, D), F32), jax.ShapeDtypeStruct((T, D), BF16)),
        compiler_params=_params("arbitrary"),
    )(y, target)


def _position():
    return lax.axis_index("x"), lax.axis_index("y"), lax.axis_index("c")


def _all_gather(arrays, *, name):
    n = len(arrays)

    def body(*refs):
        srcs, outs = refs[:n], refs[n:2 * n]
        token, send_sems, recv_sems, local_sems = refs[2 * n:]
        token[...] = jnp.zeros_like(token)
        x, y, c = _position()
        me, sibling = (x, y, c), (x, y, 1 - c)
        chips = [(1 - x, y), (x, 1 - y), (1 - x, 1 - y)]

        def slot(a, px, py, pc):
            return outs[a].at[4 * px + 2 * py + pc]

        def copy(a, k, block, to, src=None):
            return pltpu.make_async_remote_copy(
                src_ref=slot(a, *block) if src is None else src, dst_ref=slot(a, *block),
                send_sem=send_sems.at[a, k], recv_sem=recv_sems.at[a, k], device_id=to, device_id_type=MESH)

        mine = [pltpu.make_async_copy(srcs[a], slot(a, *me), local_sems.at[a]) for a in range(n)]
        for cp in mine:
            cp.start()
        first, passed = [], []
        for a in range(n):
            first.append(copy(a, 0, me, sibling, src=srcs[a]))
            first += [copy(a, 1 + j, me, (*chip, c), src=srcs[a]) for j, chip in enumerate(chips)]
        for cp in first:
            cp.start()
        for a in range(n):
            for j, chip in enumerate(chips):
                copy(a, 1 + j, (*chip, c), me).wait_recv()
                fwd = copy(a, 4 + j, (*chip, c), sibling)
                fwd.start()
                passed.append(fwd)
        for a in range(n):
            copy(a, 0, sibling, me).wait_recv()
            for j, chip in enumerate(chips):
                copy(a, 4 + j, (*chip, 1 - c), me).wait_recv()
        for cp in first + passed:
            cp.wait_send()
        for cp in mine:
            cp.wait()

    return pl.pallas_call(
        body, name=name, in_specs=[ANY] * n, out_specs=[ANY] * n + [pl.BlockSpec(memory_space=pltpu.VMEM)],
        out_shape=[jax.ShapeDtypeStruct((N_DEV,) + a.shape, a.dtype) for a in arrays] + [jax.ShapeDtypeStruct((8, 128), F32)],
        scratch_shapes=[pltpu.SemaphoreType.DMA((n, 7)), pltpu.SemaphoreType.DMA((n, 7)), pltpu.SemaphoreType.DMA((n,))],
    )(*arrays)


def _sibling_exchange(by_core, whole, *, name):
    n1, n = len(by_core), len(by_core) + len(whole)

    def body(*refs):
        srcs, outs = refs[:n], refs[n:2 * n]
        send_sems, recv_sems = refs[2 * n:]
        x, y, c = _position()
        copies = [
            pltpu.make_async_remote_copy(src_ref=srcs[a].at[:, 1 - c] if a < n1 else srcs[a], dst_ref=outs[a],
                                         send_sem=send_sems.at[a], recv_sem=recv_sems.at[a], device_id=(x, y, 1 - c),
                                         device_id_type=MESH)
            for a in range(n)]
        for cp in copies:
            cp.start()
        for cp in copies:
            cp.wait()

    out_shape = [jax.ShapeDtypeStruct(a.shape[:1] + a.shape[2:], a.dtype) for a in by_core]
    out_shape += [jax.ShapeDtypeStruct(a.shape, a.dtype) for a in whole]
    outs = pl.pallas_call(
        body, name=name, in_specs=[ANY] * n, out_specs=[ANY] * n, out_shape=out_shape,
        scratch_shapes=[pltpu.SemaphoreType.DMA((n,)), pltpu.SemaphoreType.DMA((n,))],
    )(*by_core, *whole)
    return outs[:n1], outs[n1:]


def _chip_exchange(per_chip, whole, *, name):
    n1, n = len(per_chip), len(per_chip) + len(whole)

    def body(*refs):
        srcs, outs = refs[:n], refs[n:2 * n]
        send_sems, recv_sems, local_sems = refs[2 * n:]
        x, y, c = _position()
        my_chip = 2 * x + y
        chips = [(1 - x, y), (x, 1 - y), (1 - x, 1 - y)]

        def src(a, chip):
            return srcs[a].at[chip] if a < n1 else srcs[a]

        local = [pltpu.make_async_copy(src(a, my_chip), outs[a].at[my_chip], local_sems.at[a]) for a in range(n)]
        for cp in local:
            cp.start()
        copies = [
            pltpu.make_async_remote_copy(src_ref=src(a, 2 * px + py), dst_ref=outs[a].at[my_chip],
                                         send_sem=send_sems.at[a, j], recv_sem=recv_sems.at[a, j], device_id=(px, py, c),
                                         device_id_type=MESH)
            for a in range(n) for j, (px, py) in enumerate(chips)]
        for cp in copies:
            cp.start()
        for cp in copies:
            cp.wait()
        for cp in local:
            cp.wait()

    out_shape = [jax.ShapeDtypeStruct(a.shape, a.dtype) for a in per_chip]
    out_shape += [jax.ShapeDtypeStruct((N_CHIP,) + a.shape, a.dtype) for a in whole]
    outs = pl.pallas_call(
        body, name=name, in_specs=[ANY] * n, out_specs=[ANY] * n, out_shape=out_shape,
        scratch_shapes=[pltpu.SemaphoreType.DMA((n, 3)), pltpu.SemaphoreType.DMA((n, 3)), pltpu.SemaphoreType.DMA((n,))],
    )(*per_chip, *whole)
    return outs[:n1], outs[n1:]


HBM = pl.BlockSpec(memory_space=pltpu.HBM)
SEM = pl.BlockSpec(memory_space=pltpu.SEMAPHORE)
DATAFLOW = pltpu.SideEffectType.DATAFLOW_SIDE_EFFECTING


def _device(flat):
    return flat // 4, (flat // 2) % 2, flat % 2


def _gather_copies(srcs, lands, send_sems, recv_sems, incoming):
    x, y, c = _position()
    me = 4 * x + 2 * y + c
    pairs = []
    for a in range(len(srcs)):
        for d in range(1, N_DEV):
            to, frm = (me + d) % N_DEV, (me + N_DEV - d) % N_DEV
            k = a * (N_DEV - 1) + d - 1
            sems = dict(send_sem=send_sems.at[k], recv_sem=recv_sems.at[k], device_id_type=MESH)
            out = pltpu.make_async_remote_copy(src_ref=srcs[a], dst_ref=lands[a].at[me], device_id=_device(to), **sems)
            inc = pltpu.make_async_remote_copy(src_ref=srcs[a], dst_ref=lands[a].at[frm], device_id=_device(frm),
                                               **sems) if incoming else None
            pairs.append((out, inc))
    return pairs


def _chip_copies(srcs, lands, send_sems, recv_sems, incoming, n_whole=0):
    x, y, c = _position()
    my_chip = 2 * x + y
    pairs = []
    for a in range(len(srcs)):
        for k, (px, py) in enumerate([(1 - x, y), (x, 1 - y), (1 - x, 1 - y)]):
            sem = a * (N_CHIP - 1) + k
            sems = dict(send_sem=send_sems.at[sem], recv_sem=recv_sems.at[sem], device_id=(px, py, c), device_id_type=MESH)
            src = srcs[a] if a >= len(srcs) - n_whole else srcs[a].at[2 * px + py]
            out = pltpu.make_async_remote_copy(src_ref=src, dst_ref=lands[a].at[my_chip], **sems)
            inc = pltpu.make_async_remote_copy(src_ref=src, dst_ref=lands[a].at[2 * px + py], **sems) if incoming else None
            pairs.append((out, inc))
    return pairs


def _sibling_copies(srcs, lands, send_sems, recv_sems, incoming):
    x, y, c = _position()
    pairs = []
    for a in range(len(srcs)):
        copy = pltpu.make_async_remote_copy(src_ref=srcs[a].at[:, 1 - c], dst_ref=lands[a], send_sem=send_sems.at[a],
                                            recv_sem=recv_sems.at[a], device_id=(x, y, 1 - c), device_id_type=MESH)
        pairs.append((copy, copy if incoming else None))
    return pairs


def _push_start(copies, fan, srcs, lands, *, name):
    n = len(srcs)

    def body(*refs):
        src_refs, land_refs = refs[:n], refs[n:2 * n]
        send_sems, recv_sems = refs[2 * n], refs[2 * n + 1]
        token = refs[-1]
        for out, _ in copies(src_refs, land_refs, send_sems, recv_sems, False):
            out.start()
        token[...] = jnp.zeros_like(token)

    outs = pl.pallas_call(
        body, name=name,
        out_shape=(pltpu.SemaphoreType.DMA((n * fan,)), pltpu.SemaphoreType.DMA((n * fan,)),
                   *(pltpu.HBM(a.shape, a.dtype) for a in srcs), *(pltpu.HBM(a.shape, a.dtype) for a in lands),
                   jax.ShapeDtypeStruct((8, 128), F32)),
        in_specs=[HBM] * (2 * n), out_specs=(SEM, SEM, *([HBM] * (2 * n)), pl.BlockSpec(memory_space=pltpu.VMEM)),
        input_output_aliases={i: 2 + i for i in range(2 * n)},
        compiler_params=pltpu.CompilerParams(has_side_effects=DATAFLOW),
    )(*(pltpu.with_memory_space_constraint(a, pltpu.HBM) for a in (*srcs, *lands)))
    return outs[0], outs[1], list(outs[2:2 + n]), list(outs[2 + n:2 + 2 * n]), outs[-1]


def _push_wait(copies, send_sems, recv_sems, srcs, lands, after, *, name):
    n = len(srcs)

    def body(*refs):
        src_refs, land_refs = refs[:n], refs[n:2 * n]
        for out, inc in copies(src_refs, land_refs, refs[2 * n], refs[2 * n + 1], True):
            out.wait_send()
            inc.wait_recv()
        refs[-1][...] = jnp.zeros_like(refs[-1])

    outs = pl.pallas_call(
        body, name=name,
        out_shape=(*(pltpu.HBM(a.shape, a.dtype) for a in (*srcs, *lands)), jax.ShapeDtypeStruct((8, 128), F32)),
        in_specs=[HBM] * (2 * n) + [SEM, SEM, ANY], out_specs=(*([HBM] * (2 * n)), pl.BlockSpec(memory_space=pltpu.VMEM)),
        input_output_aliases={i: i for i in range(2 * n)},
        compiler_params=pltpu.CompilerParams(has_side_effects=DATAFLOW),
    )(*srcs, *lands, send_sems, recv_sems, after)
    return list(outs[n:2 * n]), outs[-1]


def _with_own_slot(block, index, slots):
    buf = lax.empty((slots,) + block.shape, block.dtype)
    return lax.dynamic_update_slice(buf, block[None], (index,) + (0,) * block.ndim)


def _view2d(shape):
    return math.prod(shape[:-1]), shape[-1]


def _pair_sum(mine, other, core, *, name, out_dtype):
    by_core = mine.ndim == 4
    n, w = other.shape[-2:]
    tr = _row_tile(n, w * 2)
    lead = other.shape[0] if by_core else 1

    def body(core_ref, a_ref, b_ref, o_ref):
        o_ref[...] = (a_ref[...].astype(F32) + b_ref[...].astype(F32)).astype(o_ref.dtype)

    if by_core:
        a_spec = pl.BlockSpec((None, None, tr, w), lambda j, i, core_ref: (j, core_ref[0], i, 0))
        o_spec = pl.BlockSpec((None, tr, w), lambda j, i, core_ref: (j, i, 0))
    else:
        a_spec = o_spec = pl.BlockSpec((tr, w), lambda j, i, core_ref: (i, 0))
    grid_spec = pltpu.PrefetchScalarGridSpec(num_scalar_prefetch=1, grid=(lead, n // tr), in_specs=[a_spec, o_spec],
                                             out_specs=o_spec)
    return pl.pallas_call(body, name=name, grid_spec=grid_spec, out_shape=jax.ShapeDtypeStruct(other.shape, out_dtype),
                          compiler_params=_params("parallel", "parallel"))(core.reshape(1), mine, other)


def _adamw(parts, w, m, v, *, name):
    layers = len(parts)
    n_parts, R, W = parts[0].shape
    tr = _row_tile(R, W * 2)
    per_layer = R // tr

    def update(p_ref, w_ref, m_ref, v_ref, g_out, d_out, m_out, v_out):
        g = p_ref[0].astype(F32)
        for j in range(1, n_parts):
            g = g + p_ref[j].astype(F32)
        m_new = ADAM_B1 * m_ref[...] + (1.0 - ADAM_B1) * g
        v_new = ADAM_B2 * v_ref[...] + (1.0 - ADAM_B2) * (g * g)
        m_hat = m_new / (1.0 - ADAM_B1 ** ADAM_STEP)
        v_hat = v_new / (1.0 - ADAM_B2 ** ADAM_STEP)
        g_out[...] = g
        d_out[...] = -ADAM_LR * (m_hat / (jnp.sqrt(v_hat) + ADAM_EPS) + ADAM_WD * w_ref[...])
        m_out[...] = m_new
        v_out[...] = v_new

    def body(*refs):
        for k in range(layers):
            pl.when(pl.program_id(0) == k)(lambda k=k: update(refs[k], *refs[layers:]))

    def parts_spec(k):
        return pl.BlockSpec((n_parts, tr, W), lambda l, i: (0, jnp.where(l == k, i, 0), 0))

    row = pl.BlockSpec((tr, W), lambda l, i: (l * per_layer + i, 0))
    out = jax.ShapeDtypeStruct((layers * R, W), F32)
    return pl.pallas_call(
        body, name=name, grid=(layers, per_layer), in_specs=[parts_spec(k) for k in range(layers)] + [row, row, row],
        out_specs=(row, row, row, row), out_shape=(out, out, out, out), compiler_params=_params("arbitrary", "arbitrary"),
    )(*parts, w, m, v)


SMALL_ROWS = 608


def _pack_small(p):
    flat = jnp.concatenate([p[n].reshape(-1).astype(F32) for n in SMALL_NAMES])
    return jnp.pad(flat, (0, SMALL_ROWS * PACK_W - flat.shape[0])).reshape(SMALL_ROWS, PACK_W)


def _unpack_small(buf, like):
    out, at = {}, 0
    flat = buf.reshape(-1)
    for n in SMALL_NAMES:
        size = math.prod(like[n].shape)
        out[n] = flat[at:at + size].reshape(like[n].shape)
        at += size
    return out


def _heads(a, nh):
    T = a.shape[0]
    return a.reshape(T, nh, HEAD_DIM).transpose(1, 0, 2).reshape(nh * T, HEAD_DIM)


def _unheads(a, nh):
    a = a.reshape(nh, -1, HEAD_DIM)
    return a.transpose(1, 0, 2).reshape(a.shape[1], nh * HEAD_DIM)


def _groups(a):
    T = a.shape[0]
    return a.reshape(T, POOL_GROUPS, POOL_GROUP_DIM).transpose(1, 0, 2)


def _ungroups(a):
    return a.transpose(1, 0, 2).reshape(a.shape[1], MAIN_WIDTH)


def _local_step(x, mem, target, p, w_kv, fetch, reduce_layer, reduce_wait):
    T = x.shape[0]
    M = mem.shape[0]
    saved = []
    h = x
    kn = vv = k_raw = h_kv = hn_kv = None
    for l in range(DEPTH):
        s = {}
        wl, token = fetch(l, h)
        s["w"] = wl
        if l == N_A:
            h_kv = h
            hn_kv = _rms_fwd(h, p["kv_norm"], name="kv_norm_fwd")
            kv = _mm(hn_kv, w_kv, b_kind="rows", name="kv_proj")
            k_raw = _heads(kv[:, :KV_HALF], SWA_KV_HEADS)
            kn = _rms_fwd(k_raw, p["k_norm"], name="k_norm_fwd").reshape(SWA_KV_HEADS, T, HEAD_DIM)
            vv = _heads(kv[:, KV_HALF:], SWA_KV_HEADS).astype(BF16).reshape(SWA_KV_HEADS, T, HEAD_DIM)
        s["h"] = h
        s["xn1"] = _rms_fwd(h, p["norm_mix"][l] + token, name="norm_mix_fwd")
        proj = _mm(s["xn1"], wl["w_in"], b_kind="rows", name="in_proj")
        s["mq_raw"] = _heads(proj[:, MAIN_WIDTH:], MEM_HEADS)
        s["mqn"] = _rms_fwd(s["mq_raw"], p["mem_q_norm"][l], name="mem_q_norm_fwd").reshape(MEM_HEADS, T, HEAD_DIM)
        s["memn"] = _rms_fwd(mem, p["mem_norm"][l], name="mem_norm_fwd")
        mkv = _mm(s["memn"], wl["w_mem_kv"], b_kind="rows", name="mem_kv_proj")
        s["mk_raw"] = _heads(mkv[:, :MEM_WIDTH], MEM_HEADS)
        s["mkn"] = _rms_fwd(s["mk_raw"], p["mem_k_norm"][l], name="mem_k_norm_fwd").reshape(MEM_HEADS, M, HEAD_DIM)
        s["mvv"] = _heads(mkv[:, MEM_WIDTH:], MEM_HEADS).astype(BF16).reshape(MEM_HEADS, M, HEAD_DIM)
        mem_out = _unheads(_mem_fwd(s["mqn"], s["mkn"], s["mvv"], name="mem_attn_fwd"), MEM_HEADS)
        if l < N_A:
            s["u"] = _groups(proj[:, :MAIN_WIDTH])
            s["pw"] = p["pool_w"][l]
            s["ps"] = p["pool_scale"][l].reshape(POOL_GROUPS, 1, POOL_GROUP_DIM)
            main_out = _ungroups(_pool_fwd(s["u"], s["pw"], s["ps"], name="pool_fwd"))
        else:
            j = l - N_A
            s["q_raw"] = _heads(proj[:, :MAIN_WIDTH], SWA_Q_HEADS)
            s["qn"] = _rms_fwd(s["q_raw"], p["q_norm"][j], name="q_norm_fwd").reshape(SWA_KV_HEADS, SWA_GROUP, T, HEAD_DIM)
            main_out = _unheads(_swa_fwd(s["qn"], kn, vv, p["sinks"][j], name="swa_fwd"), SWA_Q_HEADS)
        s["cat"] = jnp.concatenate([main_out, mem_out], axis=-1)
        s["h1"] = _mm(s["cat"], wl["w_out"], b_kind="rows", res=h, name="out_proj")
        s["xn2"] = _rms_fwd(s["h1"], p["norm_mlp"][l], name="norm_mlp_fwd")
        s["r"], s["a"] = _mm(s["xn2"], wl["w_up"], b_kind="layers", relu2=True, name="mlp_up")
        h = _mm(s["a"], wl["w_down"], b_kind="rows", res=s["h1"], name="mlp_down")
        saved.append(s)

    loss, dh, dh_b = _loss(h, target, name="loss_head")

    g = {n: [None] * DEPTH for n in ("norm_mix", "mem_norm", "mem_q_norm", "mem_k_norm", "norm_mlp")}
    g_kv = None
    token = None
    g.update({n: [None] * N_A for n in ("pool_w", "pool_scale", "q_norm", "sinks")})
    dkn = dvv = None
    for l in reversed(range(DEPTH)):
        s = saved[l]
        wl = s["w"]
        gb = {}

        def dw(a, dy, n):
            return _mm(a, dy, ta=True, out_kind="layers" if n == "w_up" else "rows", out_buf=lax.empty(wl[n].shape, BF16),
                       name=n + "_grad")

        norm_mlp_gain = p["norm_mlp"][l] if token is None else p["norm_mlp"][l] + token
        gb["w_down"] = dw(s["a"], dh_b, "w_down")
        du = _mm(dh_b, wl["w_down"], tb=True, b_kind="rows", mul2=s["a"], out_dtype=BF16, name="mlp_down_dx")
        gb["w_up"] = dw(s["xn2"], du, "w_up")
        dxn2 = _mm(du, wl["w_up"], tb=True, b_kind="layers", name="mlp_up_dx")
        dh1, dh1_b, g["norm_mlp"][l] = _rms_bwd(s["h1"], norm_mlp_gain, [dxn2], res=dh, also_bf16=True,
                                                name="norm_mlp_bwd")
        gb["w_out"] = dw(s["cat"], dh1_b, "w_out")
        dcat = _mm(dh1_b, wl["w_out"], tb=True, b_kind="rows", name="out_proj_dx")
        dmem_out = _heads(dcat[:, MAIN_WIDTH:], MEM_HEADS).astype(BF16).reshape(MEM_HEADS, T, HEAD_DIM)
        dmqn, dmkn, dmvv = _mem_bwd(s["mqn"], s["mkn"], s["mvv"], dmem_out, name="mem_attn_bwd")
        dmq_raw, g["mem_q_norm"][l] = _rms_bwd(s["mq_raw"], p["mem_q_norm"][l], [dmqn.reshape(MEM_HEADS * T, HEAD_DIM)],
                                               name="mem_q_norm_bwd")
        dmk_raw, g["mem_k_norm"][l] = _rms_bwd(s["mk_raw"], p["mem_k_norm"][l], [dmkn.reshape(MEM_HEADS * M, HEAD_DIM)],
                                               name="mem_k_norm_bwd")
        dmkv = jnp.concatenate([_unheads(dmk_raw, MEM_HEADS), _unheads(dmvv, MEM_HEADS)], axis=-1).astype(BF16)
        gb["w_mem_kv"] = dw(s["memn"], dmkv, "w_mem_kv")
        dmemn = _mm(dmkv, wl["w_mem_kv"], tb=True, b_kind="rows", name="mem_kv_proj_dx")
        g["mem_norm"][l] = _rms_bwd(mem, p["mem_norm"][l], [dmemn], want_dx=False, name="mem_norm_bwd")
        if l < N_A:
            dmain_out = _groups(dcat[:, :MAIN_WIDTH])
            du_pool, g["pool_w"][l], dps = _pool_bwd(s["u"], s["pw"], s["ps"], dmain_out, name="pool_bwd")
            g["pool_scale"][l] = dps.reshape(MAIN_WIDTH)
            dmain = _ungroups(du_pool)
        else:
            j = l - N_A
            dmain_out = _heads(dcat[:, :MAIN_WIDTH], SWA_Q_HEADS).astype(BF16).reshape(SWA_KV_HEADS, SWA_GROUP, T, HEAD_DIM)
            dqn, dk_l, dv_l, dsink = _swa_bwd(s["qn"], kn, vv, p["sinks"][j], dmain_out, name="swa_bwd")
            g["sinks"][j] = dsink[:, 0, :SWA_GROUP].reshape(SWA_Q_HEADS)
            dq_raw, g["q_norm"][j] = _rms_bwd(s["q_raw"], p["q_norm"][j], [dqn.reshape(SWA_Q_HEADS * T, HEAD_DIM)],
                                              name="q_norm_bwd")
            dmain = _unheads(dq_raw, SWA_Q_HEADS)
            dk_l = dk_l.reshape(SWA_KV_HEADS * T, HEAD_DIM)
            dv_l = dv_l.reshape(SWA_KV_HEADS * T, HEAD_DIM)
            dkn = dk_l if dkn is None else _add(dkn, dk_l, name="dk_sum")
            dvv = dv_l if dvv is None else _add(dvv, dv_l, name="dv_sum")
        dproj = jnp.concatenate([dmain, _unheads(dmq_raw, MEM_HEADS)], axis=-1).astype(BF16)
        gb["w_in"] = dw(s["xn1"], dproj, "w_in")
        dxn1 = _mm(dproj, wl["w_in"], tb=True, b_kind="rows", name="in_proj_dx")
        if l in (0, N_A):
            dh, g["norm_mix"][l] = _rms_bwd(s["h"], p["norm_mix"][l], [dxn1], res=dh1, name="norm_mix_bwd")
        else:
            dh, dh_b, g["norm_mix"][l] = _rms_bwd(s["h"], p["norm_mix"][l], [dxn1], res=dh1, also_bf16=True,
                                                  name="norm_mix_bwd")
        if l == N_A:
            dk_raw, g["k_norm"] = _rms_bwd(k_raw, p["k_norm"], [dkn], name="k_norm_bwd")
            dkv = jnp.concatenate([_unheads(dk_raw, SWA_KV_HEADS), _unheads(dvv, SWA_KV_HEADS)], axis=-1).astype(BF16)
            g_kv = _mm(hn_kv, dkv, ta=True, out_kind="rows", out_buf=lax.empty(w_kv.shape, BF16), name="w_kv_grad")
            dhn = _mm(dkv, w_kv, tb=True, b_kind="rows", name="kv_proj_dx")
            dh, dh_b, g["kv_norm"] = _rms_bwd(h_kv, p["kv_norm"], [dhn], res=dh, also_bf16=True, name="kv_norm_bwd")
        if l + 1 < DEPTH:
            reduce_wait(l + 1, dh)
        token = reduce_layer(l, gb)
    grads = {n: (jnp.stack(v) if isinstance(v, list) else v) for n, v in g.items()}
    return loss, dh, grads, g_kv


def _block_diag(pw):
    out = jnp.zeros((MAIN_WIDTH, MAIN_WIDTH), pw.dtype)
    for g in range(POOL_GROUPS):
        out = lax.dynamic_update_slice(out, pw[g], (g * POOL_GROUP_DIM, g * POOL_GROUP_DIM))
    return out


def _diag_blocks(m):
    return jnp.stack([m[g * POOL_GROUP_DIM:(g + 1) * POOL_GROUP_DIM, g * POOL_GROUP_DIM:(g + 1) * POOL_GROUP_DIM]
                      for g in range(POOL_GROUPS)])


def _train_pass(x, mem, target, p, w_kv, fetch, reduce_layer, reduce_wait):
    T = x.shape[0]
    mem_cols = MAIN_WIDTH // MEM_WIDTH
    k_gain = _head_gain(p["k_norm"], SWA_KV_HEADS)
    saved = []
    h = x
    kn = kv = h_kv = hn_kv = None
    for l in range(DEPTH):
        s = {}
        wl, token = fetch(l, h)
        s["w"] = wl
        if l == N_A:
            h_kv = h
            hn_kv, kv = _norm_mm(h, p["kv_norm"], w_kv, b_kind="rows", name="kv_proj")
            kn = _seg_rms_fwd(kv, k_gain, width=KV_HALF, col=0, name="k_norm_fwd")
        s["h"] = h
        s["xn1"], proj = _norm_mm(h, p["norm_mix"][l] + token, wl["w_in"], b_kind="rows", name="in_proj")
        s["proj"] = proj
        s["memn"] = _rms_fwd(mem, p["mem_norm"][l], name="mem_norm_fwd")
        s["mkv"] = _mm(s["memn"], wl["w_mem_kv"], b_kind="rows", name="mem_kv_proj")
        s["mk_gain"] = _head_gain(p["mem_k_norm"][l], MEM_HEADS)
        s["mkn"] = _seg_rms_fwd(s["mkv"], s["mk_gain"], width=MEM_WIDTH, col=0, name="mem_k_norm_fwd")
        if l < N_A:
            s["q_gain"] = _head_gain(p["mem_q_norm"][l], MEM_HEADS)
            s["qn"] = _seg_rms_fwd(proj, s["q_gain"], width=MEM_WIDTH, col=mem_cols, name="mem_q_norm_fwd")
            s["q_col"] = 0
        else:
            j = l - N_A
            s["q_gain"] = jnp.concatenate([_head_gain(p["q_norm"][j], SWA_Q_HEADS), _head_gain(p["mem_q_norm"][l], MEM_HEADS)],
                                          axis=1)
            s["qn"] = _seg_rms_fwd(proj, s["q_gain"], width=D_MODEL, col=0, name="q_norm_fwd")
            s["q_col"] = mem_cols
        cat = _mem_attn_fwd(s["qn"], s["q_col"], s["mkn"], s["mkv"], name="mem_attn_fwd")
        if l < N_A:
            s["mix"] = _block_diag(p["pool_w"][l])
            s["scale"] = p["pool_scale"][l].reshape(1, MAIN_WIDTH)
            s["cat"] = _pool_mix_fwd(proj, s["mix"], s["scale"], cat, name="pool_fwd")
        else:
            s["cat"] = _swa_attn_fwd(s["qn"], kn, kv, p["sinks"][l - N_A], cat, name="swa_fwd")
        s["h1"] = _mm(s["cat"], wl["w_out"], b_kind="rows", res=h, name="out_proj")
        s["xn2"], s["a"] = _norm_mm(s["h1"], p["norm_mlp"][l], wl["w_up"], b_kind="layers", relu2=True, name="mlp_up")
        h = _mm(s["a"], wl["w_down"], b_kind="rows", res=s["h1"], name="mlp_down")
        saved.append(s)

    loss, dh, dh_b = _loss(h, target, name="loss_head")

    g = {n: [None] * DEPTH for n in ("norm_mix", "mem_norm", "mem_q_norm", "mem_k_norm", "norm_mlp")}
    g.update({n: [None] * N_A for n in ("pool_w", "pool_scale", "q_norm", "sinks")})
    g_kv = None
    token = None
    dks, dvs = [], []
    for l in reversed(range(DEPTH)):
        s = saved[l]
        wl = s["w"]
        gb = {}

        def dw(a, dy, n):
            return _mm(a, dy, ta=True, out_kind="layers" if n == "w_up" else "rows", out_buf=lax.empty(wl[n].shape, BF16),
                       name=n + "_grad")

        norm_mlp_gain = p["norm_mlp"][l] if token is None else p["norm_mlp"][l] + token
        gb["w_down"] = dw(s["a"], dh_b, "w_down")
        du = _mm(dh_b, wl["w_down"], tb=True, b_kind="rows", mul2=s["a"], out_dtype=BF16, name="mlp_down_dx")
        gb["w_up"] = dw(s["xn2"], du, "w_up")
        early = reduce_layer(l, gb, early=True)
        if early is not None:
            norm_mlp_gain = norm_mlp_gain + early
        dh1, dh1_b, g["norm_mlp"][l] = _mm_rms_bwd(du, wl["w_up"], s["h1"], norm_mlp_gain, dh, b_kind="layers", also_bf16=True,
                                                   name="mlp_up_dx")
        gb["w_out"] = dw(s["cat"], dh1_b, "w_out")
        dcat = _mm(dh1_b, wl["w_out"], tb=True, b_kind="rows", name="out_proj_dx")
        if l < N_A:
            dq, dmk, dmv = _mem_attn_bwd(s["qn"], s["q_col"], s["mkn"], s["mkv"], dcat, dq_width=MEM_WIDTH, name="mem_attn_bwd")
            dproj, dmix, dscale = _pool_mix_bwd(s["proj"], s["mix"], s["scale"], dcat, name="pool_bwd")
            g["pool_w"][l] = _diag_blocks(dmix)
            g["pool_scale"][l] = dscale.reshape(MAIN_WIDTH)
            dproj, dgain = _seg_rms_bwd(s["proj"], s["q_gain"], [dq], width=MEM_WIDTH, col=mem_cols, out_buf=dproj,
                                        out_col=mem_cols, name="mem_q_norm_bwd")
            g["mem_q_norm"][l] = _fold_heads(dgain, MEM_HEADS)
        else:
            j = l - N_A
            dqn, dmk, dmv = _mem_attn_bwd(s["qn"], s["q_col"], s["mkn"], s["mkv"], dcat, dq_width=D_MODEL, name="mem_attn_bwd")
            dqn, dk_l, dv_l, dsinks = _swa_attn_bwd(s["qn"], kn, kv, p["sinks"][j], dcat, dqn, name="swa_bwd")
            dks.append(dk_l)
            dvs.append(dv_l)
            g["sinks"][j] = dsinks[0, :SWA_Q_HEADS]
            dproj, dgain = _seg_rms_bwd(s["proj"], s["q_gain"], [dqn], width=D_MODEL, col=0, name="q_norm_bwd")
            g["q_norm"][j] = _fold_heads(dgain[:, :MAIN_WIDTH], SWA_Q_HEADS)
            g["mem_q_norm"][l] = _fold_heads(dgain[:, MAIN_WIDTH:], MEM_HEADS)
        dmk_raw, dgain = _seg_rms_bwd(s["mkv"], s["mk_gain"], [dmk], width=MEM_WIDTH, col=0, name="mem_k_norm_bwd")
        g["mem_k_norm"][l] = _fold_heads(dgain, MEM_HEADS)
        dmkv = jnp.concatenate([dmk_raw, dmv.astype(BF16)], axis=1)
        gb["w_mem_kv"] = dw(s["memn"], dmkv, "w_mem_kv")
        dmemn = _mm(dmkv, wl["w_mem_kv"], tb=True, b_kind="rows", name="mem_kv_proj_dx")
        g["mem_norm"][l] = _rms_bwd(mem, p["mem_norm"][l], [dmemn], want_dx=False, name="mem_norm_bwd")
        gb["w_in"] = dw(s["xn1"], dproj, "w_in")
        if l in (0, N_A):
            dh, g["norm_mix"][l] = _mm_rms_bwd(dproj, wl["w_in"], s["h"], p["norm_mix"][l], dh1, b_kind="rows", also_bf16=False,
                                               name="in_proj_dx")
        else:
            dh, dh_b, g["norm_mix"][l] = _mm_rms_bwd(dproj, wl["w_in"], s["h"], p["norm_mix"][l], dh1, b_kind="rows",
                                                     also_bf16=True, name="in_proj_dx")
        if l == N_A:
            dkv, dgain = _seg_rms_bwd(kv, k_gain, dks, width=KV_HALF, col=0, out_buf=lax.empty((T, 2 * KV_HALF), BF16),
                                      name="k_norm_bwd")
            g["k_norm"] = _fold_heads(dgain, SWA_KV_HEADS)
            dkv = _sum_into(dvs[0], dvs[1], dkv, 1, name="dv_sum")
            g_kv = _mm(hn_kv, dkv, ta=True, out_kind="rows", out_buf=lax.empty(w_kv.shape, BF16), name="w_kv_grad")
            dh, dh_b, g["kv_norm"] = _mm_rms_bwd(dkv, w_kv, h_kv, p["kv_norm"], dh, b_kind="rows", also_bf16=True,
                                                 name="kv_proj_dx")
        if l + 1 < DEPTH:
            reduce_wait(l + 1, dh)
        token = reduce_layer(l, gb)
    grads = {n: (jnp.stack(v) if isinstance(v, list) else v) for n, v in g.items()}
    return loss, dh, grads, g_kv


def kernel(x, mem, norm_mix, w_in, pool_w, pool_scale, kv_norm, w_kv, k_norm, q_norm, sinks, mem_norm, w_mem_kv, mem_q_norm, mem_k_norm, w_out, norm_mlp, w_up, w_down, loss_target, m_norm_mix, m_w_in, m_pool_w, m_pool_scale, m_kv_norm, m_w_kv, m_k_norm, m_q_norm, m_sinks, m_mem_norm, m_w_mem_kv, m_mem_q_norm, m_mem_k_norm, m_w_out, m_norm_mlp, m_w_up, m_w_down, v_norm_mix, v_w_in, v_pool_w, v_pool_scale, v_kv_norm, v_w_kv, v_k_norm, v_q_norm, v_sinks, v_mem_norm, v_w_mem_kv, v_mem_q_norm, v_mem_k_norm, v_w_out, v_norm_mlp, v_w_up, v_w_down):
    weights = dict(norm_mix=norm_mix, w_in=w_in, pool_w=pool_w, pool_scale=pool_scale, kv_norm=kv_norm, w_kv=w_kv,
                   k_norm=k_norm, q_norm=q_norm, sinks=sinks, mem_norm=mem_norm, w_mem_kv=w_mem_kv,
                   mem_q_norm=mem_q_norm, mem_k_norm=mem_k_norm, w_out=w_out, norm_mlp=norm_mlp, w_up=w_up, w_down=w_down)
    mom1 = dict(norm_mix=m_norm_mix, w_in=m_w_in, pool_w=m_pool_w, pool_scale=m_pool_scale, kv_norm=m_kv_norm, w_kv=m_w_kv,
                k_norm=m_k_norm, q_norm=m_q_norm, sinks=m_sinks, mem_norm=m_mem_norm, w_mem_kv=m_w_mem_kv,
                mem_q_norm=m_mem_q_norm, mem_k_norm=m_mem_k_norm, w_out=m_w_out, norm_mlp=m_norm_mlp, w_up=m_w_up,
                w_down=m_w_down)
    mom2 = dict(norm_mix=v_norm_mix, w_in=v_w_in, pool_w=v_pool_w, pool_scale=v_pool_scale, kv_norm=v_kv_norm, w_kv=v_w_kv,
                k_norm=v_k_norm, q_norm=v_q_norm, sinks=v_sinks, mem_norm=v_mem_norm, w_mem_kv=v_w_mem_kv,
                mem_q_norm=v_mem_q_norm, mem_k_norm=v_mem_k_norm, w_out=v_w_out, norm_mlp=v_norm_mlp, w_up=v_w_up,
                w_down=v_w_down)
    names = list(weights)
    x_pos, y_pos, core = (lax.axis_index(n).astype(jnp.int32) for n in AXES)
    me, my_chip = 4 * x_pos + 2 * y_pos + core, 2 * x_pos + y_pos
    shard = MAIN_WIDTH // N_DEV

    def layer_shards(l, zero=0.0):
        return [(weights[n][l:l + 1] + zero).astype(BF16) for n in LAYERED]

    def usable(arrays):
        wl = dict(zip(LAYERED, arrays))
        wl["w_up"] = wl["w_up"].transpose(1, 2, 0, 3).reshape(1, D_MODEL, D_FF)
        return wl

    scale_block = jnp.pad(pool_scale, ((0, 8 - N_A), (0, 128 - shard)))
    *first, first_done = _all_gather(layer_shards(0) + [w_kv[None].astype(BF16), scale_block], name="gather_first")
    p = {n: weights[n] for n in SMALL_NAMES}
    p["pool_scale"] = first[-1][:, :N_A, :shard].transpose(1, 0, 2).reshape(N_A, MAIN_WIDTH)
    gathers, reduces, parts = {}, {}, {}

    def fetch(l, after):
        if l == 0:
            got, done = first[:len(LAYERED)], first_done
        else:
            got, done = _push_wait(_gather_copies, *gathers.pop(l), after, name=f"gather_wait_{l}")
        token = 0.0
        if l + 1 < DEPTH:
            srcs = layer_shards(l + 1, done[0, 0])
            *handles, block = _push_start(_gather_copies, N_DEV - 1, srcs, [_with_own_slot(a, me, N_DEV) for a in srcs],
                                          name=f"gather_start_{l + 1}")
            gathers[l + 1], token = handles, block[0, 0]
        return usable(got), token

    def by_core(gb):
        gb = dict(gb)
        if "w_up" in gb:
            gb["w_up"] = gb["w_up"].reshape(D_MODEL, N_DEV, D_FF // N_DEV).transpose(1, 0, 2)
        order = [n for n in LAYERED if n in gb] + [n for n in gb if n not in LAYERED]
        return {n: gb[n].reshape((N_CHIP, 2) + _view2d(gb[n].shape[1:] if n == "w_up" else gb[n].shape[2:])) for n in order}

    def pair_sums(views, sib, tag):
        return [_pair_sum(a, b, core, name=f"chip_sum_{n}_{tag}", out_dtype=BF16) for (n, a), b in zip(views.items(), sib)]

    def chip_sums(gb, tag, whole=()):
        views = by_core(gb)
        sib, sib_whole = _sibling_exchange(list(views.values()), list(whole), name="reduce_sibling_" + tag)
        return pair_sums(views, sib, tag), sib_whole

    def start_chip_exchange(sums, tag, whole=()):
        lands = [_with_own_slot(lax.dynamic_index_in_dim(a, my_chip, 0, keepdims=False), my_chip, N_CHIP) for a in sums]
        lands += [_with_own_slot(a, my_chip, N_CHIP) for a in whole]
        copies = functools.partial(_chip_copies, n_whole=len(whole))
        *handles, block = _push_start(copies, N_CHIP - 1, [*sums, *whole], lands, name="reduce_start_" + tag)
        return (copies, *handles), block[0, 0]

    mlp = ("w_up", "w_down")
    siblings = {}

    def reduce_layer(l, gb, early=False):
        if early:
            token = None
            if l + 1 in siblings:
                views, handles = siblings.pop(l + 1)
                sib, _ = _push_wait(_sibling_copies, *handles, gb["w_up"], name=f"sibling_wait_{l + 1}")
                reduces[l + 1], token = start_chip_exchange(pair_sums(views, sib, str(l + 1)), str(l + 1))
            if l == 0:
                sums, _ = chip_sums({n: gb[n] for n in mlp}, "0_mlp")
                reduces[0], token = start_chip_exchange(sums, "0_mlp")
            return token
        if l == 0:
            reduces["rest"] = {n: a for n, a in gb.items() if n not in mlp}
            return None
        views = by_core(gb)
        lands = [lax.empty(v.shape[:1] + v.shape[2:], v.dtype) for v in views.values()]
        *handles, block = _push_start(_sibling_copies, 1, list(views.values()), lands, name=f"sibling_start_{l}")
        siblings[l] = (views, handles)
        return block[0, 0]

    def reduce_wait(l, after):
        copies, *handles = reduces.pop(l)
        return _push_wait(copies, *handles, after, name=f"reduce_wait_{l}")[0]

    def layer_wait(l, after):
        parts[l] = reduce_wait(l, after)

    loss, grad_x, grads, g_kv = _train_pass(x[0], mem[0], loss_target[0], p, first[len(LAYERED)], fetch, reduce_layer, layer_wait)

    last = dict(reduces.pop("rest"))
    last["w_kv"] = g_kv
    last["pool_scale"] = grads["pool_scale"].reshape(N_A, N_DEV, shard).transpose(1, 0, 2).astype(BF16)[:, None]
    small = _pack_small(grads)
    sums, (sib_small,) = chip_sums(last, "0", whole=[small])
    chip_small = _pair_sum(small, sib_small, core, name="chip_sum_small", out_dtype=F32)
    reduces["rest"], _ = start_chip_exchange(sums, "0_rest", whole=[chip_small])

    def adamw(n, n_parts):
        res = _adamw(n_parts, *(d[n].reshape(_view2d(d[n].shape)) for d in (weights, mom1, mom2)), name="adamw_" + n)
        return [r.reshape(weights[n].shape) for r in res]

    p_up, p_down = reduce_wait(0, chip_small)
    parts[0] = [None, None, None, p_up, p_down]
    new = {n: adamw(n, [parts[l][LAYERED.index(n)] for l in range(DEPTH)]) for n in mlp}
    p_in, p_mem_kv, p_out, parts_kv, parts_scale, parts_small = reduce_wait("rest", new["w_down"][0])
    parts[0][:3] = [p_in, p_mem_kv, p_out]
    new.update({n: adamw(n, [parts[l][k] for l in range(DEPTH)]) for k, n in enumerate(LAYERED) if n not in mlp})
    new["w_kv"] = adamw("w_kv", [parts_kv])
    new["pool_scale"] = adamw("pool_scale", [parts_scale])
    res = _adamw([parts_small], _pack_small(weights), _pack_small(mom1), _pack_small(mom2), name="adamw_replicated")
    for n, vals in zip(SMALL_NAMES, zip(*(_unpack_small(r, weights).values() for r in res))):
        new[n] = list(vals)
    outs = [new[n][k] for k in range(4) for n in names]
    total = lax.psum(loss[0, 0], AXES)
    return (total, grad_x[None], *outs)
```

```python
import functools
import math

import jax
import jax.numpy as jnp
from jax import lax
from jax.experimental import pallas as pl
from jax.experimental.pallas import tpu as pltpu

F32 = jnp.float32
BF16 = jnp.bfloat16
MESH = pl.DeviceIdType.MESH
AXES = ("x", "y", "c")

D_MODEL = 1024
DEPTH = 4
N_A = 2
HEAD_DIM = 64
MEM_HEADS = 4
MEM_WIDTH = MEM_HEADS * HEAD_DIM
MAIN_WIDTH = D_MODEL - MEM_WIDTH
POOL_GROUPS = 4
POOL_GROUP_DIM = MAIN_WIDTH // POOL_GROUPS
POOL_HALO = 16
SWA_Q_HEADS = MAIN_WIDTH // HEAD_DIM
SWA_KV_HEADS = 4
SWA_GROUP = SWA_Q_HEADS // SWA_KV_HEADS
KV_HALF = SWA_KV_HEADS * HEAD_DIM
BLOCK = 128
D_FF = 4 * D_MODEL
EPS = 1e-6
SCALE = HEAD_DIM ** -0.5
NEG = float(jnp.finfo(jnp.float32).min)
N_DEV = 8
N_CHIP = 4

ADAM_LR = 0.001
ADAM_B1 = 0.9
ADAM_B2 = 0.999
ADAM_EPS = 1e-08
ADAM_WD = 0.01
ADAM_STEP = 10

PACK_W = 512
VMEM_LIMIT = 52 * 1024 * 1024
MM_TILE = 1024

LAYERED = ("w_in", "w_mem_kv", "w_out", "w_up", "w_down")
SMALL_NAMES = ("norm_mix", "pool_w", "kv_norm", "k_norm", "q_norm", "sinks", "mem_norm", "mem_q_norm", "mem_k_norm",
               "norm_mlp")


ANY = pl.BlockSpec(memory_space=pl.ANY)


def _params(*sem):
    return pltpu.CompilerParams(dimension_semantics=sem, vmem_limit_bytes=VMEM_LIMIT)


def _mm(a, b, *, name, ta=False, tb=False, b_kind=None, layer=0, res=None, relu2=False, mul2=None, out_dtype=F32,
        out_kind=None, out_buf=None):
    if ta:
        K, M = a.shape
    else:
        M, K = a.shape
    if b_kind is None:
        rows_b, cols_b = b.shape
    elif b_kind == "rows":
        rows_b, cols_b = b.shape[0] * b.shape[2], b.shape[3]
    else:
        rows_b, cols_b = b.shape[1:]
    N, K2 = (rows_b, cols_b) if tb else (cols_b, rows_b)
    assert K == K2, (a.shape, b.shape)
    tm = min(M, MM_TILE if K <= MM_TILE else MM_TILE // 2)
    tn = min(N, MM_TILE)
    assert M % tm == 0 and N % tn == 0
    row_tile, col_tile = (tn, K) if tb else (K, tn)
    a_spec = pl.BlockSpec((K, tm), lambda j, i: (0, i)) if ta else pl.BlockSpec((tm, K), lambda j, i: (i, 0))

    def rc(j):
        return (j, 0) if tb else (0, j)

    if b_kind is None:
        b_spec = pl.BlockSpec((row_tile, col_tile), lambda j, i: rc(j))
    elif b_kind == "rows":
        per = row_tile // b.shape[2]
        b_spec = pl.BlockSpec((per, None, b.shape[2], col_tile), lambda j, i: (rc(j)[0], layer, 0, rc(j)[1]))
    else:
        b_spec = pl.BlockSpec((None, row_tile, col_tile), lambda j, i: (layer, *rc(j)))
    o_spec = pl.BlockSpec((tm, tn), lambda j, i: (i, j))
    dn = (((0 if ta else 1,), (1 if tb else 0,)), ((), ()))
    extra = [e for e in (res, mul2) if e is not None]
    n_in = 2 + len(extra) + (1 if out_buf is not None else 0)

    def body(*refs):
        a_ref, b_ref = refs[0], refs[1]
        extra_refs = refs[2:2 + len(extra)]
        out = refs[n_in]
        bv = b_ref[...].astype(BF16).reshape(row_tile, col_tile)
        v = lax.dot_general(a_ref[...].astype(BF16), bv, dn, preferred_element_type=F32)
        if res is not None:
            v = extra_refs[0][...] + v
        elif mul2 is not None:
            v = v * (2.0 * jnp.sqrt(extra_refs[0][...].astype(F32)))
        if relu2:
            r = jnp.maximum(v, 0.0)
            v = r * r
        out[...] = v.astype(out.dtype).reshape(out.shape)

    in_specs = [a_spec, b_spec] + [o_spec] * len(extra)
    operands = [a, b, *extra]
    aliases = {}
    if out_kind is None:
        out_shape = jax.ShapeDtypeStruct((M, N), BF16 if relu2 else out_dtype)
        out_specs = o_spec
    else:
        if out_kind == "rows":
            s = out_buf.shape[2]
            out_specs = pl.BlockSpec((tm // s, None, s, tn), lambda j, i: (i, layer, 0, j))
        else:
            out_specs = pl.BlockSpec((None, tm, tn), lambda j, i: (layer, i, j))
        out_shape = jax.ShapeDtypeStruct(out_buf.shape, out_buf.dtype)
        in_specs.append(ANY)
        operands.append(out_buf)
        aliases = {len(operands) - 1: 0}
    return pl.pallas_call(
        body, name=name, grid=(N // tn, M // tm), in_specs=in_specs, out_specs=out_specs, out_shape=out_shape,
        input_output_aliases=aliases, compiler_params=_params("parallel", "parallel"),
    )(*operands)


def _weight_block(b, b_kind, transposed, tn):
    if b_kind == "rows":
        s = b.shape[2]
        rows, cols = b.shape[0] * s, b.shape[3]
        if transposed:
            return (lambda at: pl.BlockSpec((b.shape[0], None, s, cols), lambda *g: (0, 0, 0, 0))), rows, cols
        return (lambda at: pl.BlockSpec((b.shape[0], None, s, tn), lambda *g: (0, 0, 0, at(*g)))), rows, cols
    rows, cols = b.shape[1:]
    if transposed:
        return (lambda at: pl.BlockSpec((None, rows, cols), lambda *g: (0, 0, 0))), rows, cols
    return (lambda at: pl.BlockSpec((None, rows, tn), lambda *g: (0, 0, at(*g)))), rows, cols


def _norm_mm(x, gain, b, *, b_kind, name, relu2=False):
    M, K = x.shape
    tm = min(M, MM_TILE)
    spec_of, rows, N = _weight_block(b, b_kind, False, min(MM_TILE, b.shape[-1]))
    tn = min(N, MM_TILE)
    assert rows == K and M % tm == 0 and N % tn == 0

    def body(x_ref, g_ref, b_ref, xn_ref, o_ref):
        @pl.when(pl.program_id(1) == 0)
        def _():
            xv = x_ref[...]
            r = lax.rsqrt(jnp.mean(xv * xv, axis=-1, keepdims=True) + EPS)
            xn_ref[...] = ((xv * r) * g_ref[...]).astype(xn_ref.dtype)

        v = jnp.dot(xn_ref[...], b_ref[...].astype(BF16).reshape(K, tn), preferred_element_type=F32)
        if relu2:
            r2 = jnp.maximum(v, 0.0)
            v = r2 * r2
        o_ref[...] = v.astype(o_ref.dtype)

    rows_spec = pl.BlockSpec((tm, K), lambda i, j: (i, 0))
    return pl.pallas_call(
        body, name=name, grid=(M // tm, N // tn),
        in_specs=[rows_spec, pl.BlockSpec((1, K), lambda i, j: (0, 0)), spec_of(lambda i, j: j)],
        out_specs=(rows_spec, pl.BlockSpec((tm, tn), lambda i, j: (i, j))),
        out_shape=(jax.ShapeDtypeStruct((M, K), BF16), jax.ShapeDtypeStruct((M, N), BF16 if relu2 else F32)),
        compiler_params=_params("parallel", "arbitrary"),
    )(x, gain.reshape(1, K), b)


def _mm_rms_bwd(a, b, x, gain, res, *, b_kind, name, also_bf16):
    M, K = a.shape
    spec_of, N, cols = _weight_block(b, b_kind, True, None)
    assert cols == K and x.shape == (M, N)
    tm = min(M, MM_TILE if K <= MM_TILE else MM_TILE // 2)
    assert M % tm == 0

    def body(a_ref, b_ref, x_ref, g_ref, res_ref, *outs):
        i = pl.program_id(0)
        dy = lax.dot_general(a_ref[...].astype(BF16), b_ref[...].astype(BF16).reshape(N, K), (((1,), (1,)), ((), ())),
                             preferred_element_type=F32)
        xv = x_ref[...]
        r = lax.rsqrt(jnp.mean(xv * xv, axis=-1, keepdims=True) + EPS)
        xh = xv * r
        part = jnp.sum(dy * xh, axis=0, keepdims=True)
        dg_ref = outs[-1]

        @pl.when(i == 0)
        def _():
            dg_ref[...] = part

        @pl.when(i > 0)
        def _():
            dg_ref[...] += part

        gdy = dy * g_ref[...]
        dx = res_ref[...] + r * (gdy - xh * jnp.mean(gdy * xh, axis=-1, keepdims=True))
        outs[0][...] = dx
        if also_bf16:
            outs[1][...] = dx.astype(BF16)

    row = pl.BlockSpec((tm, N), lambda i: (i, 0))
    vec = pl.BlockSpec((1, N), lambda i: (0, 0))
    out_specs = [row] + ([row] if also_bf16 else []) + [vec]
    out_shape = [jax.ShapeDtypeStruct((M, N), F32)] + ([jax.ShapeDtypeStruct((M, N), BF16)] if also_bf16 else [])
    outs = pl.pallas_call(
        body, name=name, grid=(M // tm,),
        in_specs=[pl.BlockSpec((tm, K), lambda i: (i, 0)), spec_of(None), row, vec, row], out_specs=out_specs,
        out_shape=out_shape + [jax.ShapeDtypeStruct((1, N), F32)], compiler_params=_params("arbitrary"),
    )(a, b, x, gain.reshape(1, N), res)
    return (*outs[:-1], outs[-1].reshape(N))


def _row_tile(rows, d):
    t = min(rows, (512 * 1024) // d)
    while rows % t or (t != rows and t % 16):
        t -= 1
    return t


def _rms_fwd(x, g, *, name, out_dtype=BF16):
    R, D = x.shape
    tr = _row_tile(R, D)

    def body(x_ref, g_ref, o_ref):
        xv = x_ref[...].astype(F32)
        r = lax.rsqrt(jnp.mean(xv * xv, axis=-1, keepdims=True) + EPS)
        o_ref[...] = ((xv * r) * g_ref[...]).astype(o_ref.dtype)

    return pl.pallas_call(
        body, name=name, grid=(R // tr,),
        in_specs=[pl.BlockSpec((tr, D), lambda i: (i, 0)), pl.BlockSpec((1, D), lambda i: (0, 0))],
        out_specs=pl.BlockSpec((tr, D), lambda i: (i, 0)), out_shape=jax.ShapeDtypeStruct((R, D), out_dtype),
        compiler_params=_params("parallel"),
    )(x, g.reshape(1, D))


def _rms_bwd(x, g, dys, *, name, res=None, want_dx=True, also_bf16=False):
    R, D = x.shape
    tr = _row_tile(R, D)
    n_dy = len(dys)
    has_res = res is not None

    def body(*refs):
        x_ref, g_ref = refs[0], refs[1]
        dy_refs = refs[2:2 + n_dy]
        res_ref = refs[2 + n_dy] if has_res else None
        outs = refs[2 + n_dy + (1 if has_res else 0):]
        dg_ref = outs[-1]
        i = pl.program_id(0)
        xv = x_ref[...].astype(F32)
        dy = dy_refs[0][...].astype(F32)
        for extra in dy_refs[1:]:
            dy = dy + extra[...].astype(F32)
        r = lax.rsqrt(jnp.mean(xv * xv, axis=-1, keepdims=True) + EPS)
        xh = xv * r
        part = jnp.sum(dy * xh, axis=0, keepdims=True)

        @pl.when(i == 0)
        def _():
            dg_ref[...] = part

        @pl.when(i > 0)
        def _():
            dg_ref[...] += part

        if want_dx:
            gdy = dy * g_ref[...]
            dx = r * (gdy - xh * jnp.mean(gdy * xh, axis=-1, keepdims=True))
            if has_res:
                dx = res_ref[...] + dx
            outs[0][...] = dx
            if also_bf16:
                outs[1][...] = dx.astype(BF16)

    row = pl.BlockSpec((tr, D), lambda i: (i, 0))
    vec = pl.BlockSpec((1, D), lambda i: (0, 0))
    out_shape = [jax.ShapeDtypeStruct((1, D), F32)]
    out_specs = [vec]
    if also_bf16:
        out_shape = [jax.ShapeDtypeStruct((R, D), BF16)] + out_shape
        out_specs = [row] + out_specs
    if want_dx:
        out_shape = [jax.ShapeDtypeStruct((R, D), F32)] + out_shape
        out_specs = [row] + out_specs
    outs = pl.pallas_call(
        body, name=name, grid=(R // tr,),
        in_specs=[row, vec] + [row] * (n_dy + (1 if has_res else 0)), out_specs=out_specs, out_shape=out_shape,
        compiler_params=_params("arbitrary"),
    )(x, g.reshape(1, D), *dys, *([res] if has_res else []))
    return (*outs[:-1], outs[-1].reshape(D)) if want_dx else outs[0].reshape(D)


def _add(a, b, *, name):
    R, D = a.shape
    tr = _row_tile(R, D)

    def body(a_ref, b_ref, o_ref):
        o_ref[...] = a_ref[...] + b_ref[...]

    row = pl.BlockSpec((tr, D), lambda i: (i, 0))
    return pl.pallas_call(body, name=name, grid=(R // tr,), in_specs=[row, row], out_specs=row,
                          out_shape=jax.ShapeDtypeStruct((R, D), a.dtype), compiler_params=_params("parallel"))(a, b)


POOL_TILE = 512


def _pool_window(group):
    return lax.shift_left(jnp.int32(2), group)


def _pool_diff(u_ref, halo_ref, group, tile):
    first = tile == 0
    halo = jnp.where(first, 0.0, halo_ref[...])
    ext = jnp.concatenate([halo, u_ref[...]], axis=0)
    n = ext.shape[0]
    s1 = ext + pltpu.roll(ext, 1, 0)
    s2 = s1 + pltpu.roll(s1, 2, 0)
    s3 = s2 + pltpu.roll(s2, 4, 0)
    s4 = s3 + pltpu.roll(s3, 8, 0)
    ws = jnp.where(group == 0, s1, jnp.where(group == 1, s2, jnp.where(group == 2, s3, s4)))[POOL_HALO:n]
    t = tile * POOL_TILE + lax.broadcasted_iota(jnp.int32, (POOL_TILE, 1), 0)
    cnt = jnp.minimum(t + 1, _pool_window(group)).astype(F32)
    return ws / cnt - u_ref[...], cnt


def _pool_specs():
    per_tile = POOL_TILE // POOL_HALO
    cur = pl.BlockSpec((None, POOL_TILE, POOL_GROUP_DIM), lambda g, i: (g, i, 0))
    prev = pl.BlockSpec((None, POOL_HALO, POOL_GROUP_DIM), lambda g, i: (g, jnp.maximum(i * per_tile - 1, 0), 0))
    pw = pl.BlockSpec((None, POOL_GROUP_DIM, POOL_GROUP_DIM), lambda g, i: (g, 0, 0))
    vec = pl.BlockSpec((None, 1, POOL_GROUP_DIM), lambda g, i: (g, 0, 0))
    return cur, prev, pw, vec


def _pool_fwd(u, pw, scale, *, name):
    G, T, C = u.shape
    assert T % POOL_TILE == 0
    cur, prev, pw_spec, vec = _pool_specs()

    def body(u_ref, halo_ref, pw_ref, sc_ref, o_ref):
        d, _ = _pool_diff(u_ref, halo_ref, pl.program_id(0), pl.program_id(1))
        mixed = jnp.dot(d.astype(BF16), pw_ref[...].astype(BF16), preferred_element_type=F32)
        o_ref[...] = (mixed * sc_ref[...]).astype(o_ref.dtype)

    return pl.pallas_call(
        body, name=name, grid=(G, T // POOL_TILE), in_specs=[cur, prev, pw_spec, vec], out_specs=cur,
        out_shape=jax.ShapeDtypeStruct((G, T, C), BF16), compiler_params=_params("parallel", "parallel"),
    )(u, u, pw, scale)


def _pool_bwd(u, pw, scale, dout, *, name):
    G, T, C = u.shape
    nt = T // POOL_TILE
    per_tile = POOL_TILE // POOL_HALO
    cur, prev, pw_spec, vec = _pool_specs()
    nxt = pl.BlockSpec((None, POOL_HALO, C), lambda g, i: (g, jnp.minimum((i + 1) * per_tile, nt * per_tile - 1), 0))

    def body(u_ref, halo_ref, pw_ref, sc_ref, do_ref, donext_ref, du_ref, dpw_ref, dsc_ref):
        group, tile = pl.program_id(0), pl.program_id(1)
        d, cnt = _pool_diff(u_ref, halo_ref, group, tile)
        pwb = pw_ref[...].astype(BF16)
        db = d.astype(BF16)
        mixed = jnp.dot(db, pwb, preferred_element_type=F32)
        dout = do_ref[...].astype(F32)
        dsc = jnp.sum(dout * mixed, axis=0, keepdims=True)
        sc = sc_ref[...]
        dmix = (dout * sc).astype(BF16)
        dpw = lax.dot_general(db, dmix, (((0,), (0,)), ((), ())), preferred_element_type=F32)

        @pl.when(tile == 0)
        def _():
            dpw_ref[...] = dpw
            dsc_ref[...] = dsc

        @pl.when(tile > 0)
        def _():
            dpw_ref[...] += dpw
            dsc_ref[...] += dsc

        last = tile == nt - 1
        dnext = jnp.where(last, 0.0, donext_ref[...].astype(F32))
        dmix_ext = jnp.concatenate([dmix, (dnext * sc).astype(BF16)], axis=0)
        dd_ext = lax.dot_general(dmix_ext, pwb, (((1,), (1,)), ((), ())), preferred_element_type=F32)
        window = _pool_window(group).astype(F32)
        cnt_ext = jnp.concatenate([cnt, jnp.broadcast_to(window, (POOL_HALO, 1))], axis=0)
        q = dd_ext / cnt_ext
        n = q.shape[0]
        r1 = q + pltpu.roll(q, n - 1, 0)
        r2 = r1 + pltpu.roll(r1, n - 2, 0)
        r3 = r2 + pltpu.roll(r2, n - 4, 0)
        r4 = r3 + pltpu.roll(r3, n - 8, 0)
        back = jnp.where(group == 0, r1, jnp.where(group == 1, r2, jnp.where(group == 2, r3, r4)))
        du_ref[...] = back[0:POOL_TILE] - dd_ext[0:POOL_TILE]

    return pl.pallas_call(
        body, name=name, grid=(G, nt), in_specs=[cur, prev, pw_spec, vec, cur, nxt],
        out_specs=(cur, pw_spec, vec),
        out_shape=(jax.ShapeDtypeStruct((G, T, C), F32), jax.ShapeDtypeStruct((G, C, C), F32),
                   jax.ShapeDtypeStruct((G, 1, C), F32)),
        compiler_params=_params("arbitrary", "arbitrary"),
    )(u, u, pw, scale, dout, dout)


def _softmax(q, k, bias, valid, sink):
    s = lax.dot_general(q, k, (((1,), (1,)), ((), ())), preferred_element_type=F32) * SCALE
    if bias is not None:
        s = s - bias
    if valid is not None:
        s = jnp.where(valid, s, NEG)
    m = jnp.max(s, axis=-1, keepdims=True)
    if sink is not None:
        m = jnp.maximum(m, sink)
    e = jnp.exp(s - m)
    z = jnp.sum(e, axis=-1, keepdims=True)
    if sink is None:
        return e * (1.0 / z), None
    es = jnp.exp(sink - m)
    inv = 1.0 / (z + es)
    return e * inv, es * inv


def _swa_terms(sink_ref, kvh, blk):
    rows = SWA_GROUP * BLOCK
    row = lax.broadcasted_iota(jnp.int32, (rows, 1), 0)
    grp = row // BLOCK
    head = (kvh * SWA_GROUP + grp + 1).astype(F32)
    slope = jnp.exp(head * (-8.0 * math.log(2.0) / SWA_Q_HEADS))
    qi = lax.broadcasted_iota(jnp.int32, (rows, 2 * BLOCK), 0) % BLOCK
    kj = lax.broadcasted_iota(jnp.int32, (rows, 2 * BLOCK), 1)
    dist = qi + BLOCK - kj
    valid = (dist >= 0) & (dist < BLOCK) & ((blk > 0) | (kj >= BLOCK))
    bias = slope * dist.astype(F32)
    s0, s1, s2 = (sink_ref[kvh * SWA_GROUP + g] for g in range(SWA_GROUP))
    sink = jnp.where(grp == 0, s0, jnp.where(grp == 1, s1, s2))
    return bias, valid, sink, grp


def _swa_fwd(q, k, v, sinks, *, name):
    H, G, T, hd = q.shape
    nb = T // BLOCK
    rows = G * BLOCK

    def body(sink_ref, q_ref, kp_ref, kc_ref, vp_ref, vc_ref, o_ref):
        kvh, blk = pl.program_id(0), pl.program_id(1)
        bias, valid, sink, _ = _swa_terms(sink_ref, kvh, blk)
        kk = jnp.concatenate([kp_ref[...], kc_ref[...]], axis=0)
        vv = jnp.concatenate([vp_ref[...], vc_ref[...]], axis=0)
        p, _ = _softmax(q_ref[...].reshape(rows, hd), kk, bias, valid, sink)
        o = jnp.dot(p.astype(BF16), vv, preferred_element_type=F32)
        o_ref[...] = o.reshape(G, BLOCK, hd).astype(o_ref.dtype)

    qs = pl.BlockSpec((None, G, BLOCK, hd), lambda h, n: (h, 0, n, 0))
    prev = pl.BlockSpec((None, BLOCK, hd), lambda h, n: (h, jnp.maximum(n - 1, 0), 0))
    cur = pl.BlockSpec((None, BLOCK, hd), lambda h, n: (h, n, 0))
    return pl.pallas_call(
        body, name=name, grid=(H, nb),
        in_specs=[pl.BlockSpec(memory_space=pltpu.SMEM), qs, prev, cur, prev, cur], out_specs=qs,
        out_shape=jax.ShapeDtypeStruct((H, G, T, hd), BF16), compiler_params=_params("parallel", "parallel"),
    )(sinks, q, k, k, v, v)


def _swa_bwd(q, k, v, sinks, do, *, name):
    H, G, T, hd = q.shape
    nb = T // BLOCK
    rows = G * BLOCK

    def body(sink_ref, q_ref, do_ref, kp_ref, kc_ref, vp_ref, vc_ref, dq_ref, dk_ref, dv_ref, ds_ref, ck, cv):
        kvh, blk = pl.program_id(0), pl.program_id(1)

        @pl.when(blk == 0)
        def _():
            ck[...] = jnp.zeros_like(ck)
            cv[...] = jnp.zeros_like(cv)
            ds_ref[...] = jnp.zeros_like(ds_ref)

        @pl.when(blk < nb)
        def _():
            bias, valid, sink, grp = _swa_terms(sink_ref, kvh, blk)
            kk = jnp.concatenate([kp_ref[...], kc_ref[...]], axis=0)
            vv = jnp.concatenate([vp_ref[...], vc_ref[...]], axis=0)
            qq = q_ref[...].reshape(rows, hd)
            dout = do_ref[...].reshape(rows, hd)
            p, ps = _softmax(qq, kk, bias, valid, sink)
            dp = lax.dot_general(dout, vv, (((1,), (1,)), ((), ())), preferred_element_type=F32)
            dsum = jnp.sum(p * dp, axis=-1, keepdims=True)
            ds = (p * (dp - dsum)).astype(BF16)
            dq = jnp.dot(ds, kk, preferred_element_type=F32) * SCALE
            dq_ref[...] = dq.reshape(G, BLOCK, hd)
            dk = lax.dot_general(ds, qq, (((0,), (0,)), ((), ())), preferred_element_type=F32) * SCALE
            dv = lax.dot_general(p.astype(BF16), dout, (((0,), (0,)), ((), ())), preferred_element_type=F32)
            dk_ref[...] = ck[...] + dk[0:BLOCK]
            dv_ref[...] = cv[...] + dv[0:BLOCK]
            ck[...] = dk[BLOCK:2 * BLOCK]
            cv[...] = dv[BLOCK:2 * BLOCK]
            dsink = -(ps * dsum)
            lane = lax.broadcasted_iota(jnp.int32, (1, 128), 1)
            acc = jnp.zeros((1, 128), F32)
            for g in range(G):
                acc = acc + jnp.where(lane == g, jnp.sum(jnp.where(grp == g, dsink, 0.0)), 0.0)
            ds_ref[...] += acc

        @pl.when(blk == nb)
        def _():
            dk_ref[...] = ck[...]
            dv_ref[...] = cv[...]

    def at(n):
        return jnp.minimum(n, nb - 1)

    qs = pl.BlockSpec((None, G, BLOCK, hd), lambda h, n: (h, 0, at(n), 0))
    prev = pl.BlockSpec((None, BLOCK, hd), lambda h, n: (h, jnp.maximum(at(n) - 1, 0), 0))
    cur = pl.BlockSpec((None, BLOCK, hd), lambda h, n: (h, at(n), 0))
    late = pl.BlockSpec((None, BLOCK, hd), lambda h, n: (h, jnp.maximum(n - 1, 0), 0))
    dsink_spec = pl.BlockSpec((None, 1, 128), lambda h, n: (h, 0, 0))
    return pl.pallas_call(
        body, name=name, grid=(H, nb + 1),
        in_specs=[pl.BlockSpec(memory_space=pltpu.SMEM), qs, qs, prev, cur, prev, cur],
        out_specs=(qs, late, late, dsink_spec),
        out_shape=(jax.ShapeDtypeStruct((H, G, T, hd), F32), jax.ShapeDtypeStruct((H, T, hd), F32),
                   jax.ShapeDtypeStruct((H, T, hd), F32), jax.ShapeDtypeStruct((H, 1, 128), F32)),
        scratch_shapes=[pltpu.VMEM((BLOCK, hd), F32), pltpu.VMEM((BLOCK, hd), F32)],
        compiler_params=_params("arbitrary", "arbitrary"),
    )(sinks, q, do, k, k, v, v)


MEM_Q_TILE = 512


def _mem_fwd(q, k, v, *, name):
    H, T, hd = q.shape
    M = k.shape[1]
    tq = min(T, MEM_Q_TILE)

    def body(q_ref, k_ref, v_ref, o_ref):
        p, _ = _softmax(q_ref[...], k_ref[...], None, None, None)
        o_ref[...] = jnp.dot(p.astype(BF16), v_ref[...], preferred_element_type=F32).astype(o_ref.dtype)

    qs = pl.BlockSpec((None, tq, hd), lambda h, i: (h, i, 0))
    ks = pl.BlockSpec((None, M, hd), lambda h, i: (h, 0, 0))
    return pl.pallas_call(body, name=name, grid=(H, T // tq), in_specs=[qs, ks, ks], out_specs=qs,
                          out_shape=jax.ShapeDtypeStruct((H, T, hd), BF16),
                          compiler_params=_params("parallel", "parallel"))(q, k, v)


def _mem_bwd(q, k, v, do, *, name):
    H, T, hd = q.shape
    M = k.shape[1]
    tq = min(T, MEM_Q_TILE)

    def body(q_ref, do_ref, k_ref, v_ref, dq_ref, dk_ref, dv_ref):
        i = pl.program_id(1)
        qq, kk, vv, dout = q_ref[...], k_ref[...], v_ref[...], do_ref[...]
        p, _ = _softmax(qq, kk, None, None, None)
        dp = lax.dot_general(dout, vv, (((1,), (1,)), ((), ())), preferred_element_type=F32)
        dsum = jnp.sum(p * dp, axis=-1, keepdims=True)
        ds = (p * (dp - dsum)).astype(BF16)
        dq_ref[...] = jnp.dot(ds, kk, preferred_element_type=F32) * SCALE
        dk = lax.dot_general(ds, qq, (((0,), (0,)), ((), ())), preferred_element_type=F32) * SCALE
        dv = lax.dot_general(p.astype(BF16), dout, (((0,), (0,)), ((), ())), preferred_element_type=F32)

        @pl.when(i == 0)
        def _():
            dk_ref[...] = dk
            dv_ref[...] = dv

        @pl.when(i > 0)
        def _():
            dk_ref[...] += dk
            dv_ref[...] += dv

    qs = pl.BlockSpec((None, tq, hd), lambda h, i: (h, i, 0))
    ks = pl.BlockSpec((None, M, hd), lambda h, i: (h, 0, 0))
    return pl.pallas_call(
        body, name=name, grid=(H, T // tq), in_specs=[qs, qs, ks, ks], out_specs=(qs, ks, ks),
        out_shape=(jax.ShapeDtypeStruct((H, T, hd), F32), jax.ShapeDtypeStruct((H, M, hd), F32),
                   jax.ShapeDtypeStruct((H, M, hd), F32)),
        compiler_params=_params("arbitrary", "arbitrary"),
    )(q, do, k, v)


LANES = 128


def _seg_mean(v):
    r = lax.broadcasted_iota(jnp.int32, (LANES, LANES), 0) // HEAD_DIM
    c = lax.broadcasted_iota(jnp.int32, (LANES, LANES), 1) // HEAD_DIM
    seg = jnp.where(r == c, 1.0 / HEAD_DIM, 0.0).astype(BF16)
    hi = v.astype(BF16)
    lo = (v - hi.astype(F32)).astype(BF16)
    parts = []
    for g in range(v.shape[1] // LANES):
        sl = slice(g * LANES, (g + 1) * LANES)
        parts.append(jnp.dot(hi[:, sl], seg, preferred_element_type=F32) + jnp.dot(lo[:, sl], seg, preferred_element_type=F32))
    return parts[0] if len(parts) == 1 else jnp.concatenate(parts, axis=1)


def _cols(rows, width, col):
    return pl.BlockSpec((rows, width), lambda i: (i, col))


def _head_gain(g, heads):
    return jnp.tile(g, heads).reshape(1, heads * HEAD_DIM)


def _fold_heads(dg, heads):
    return dg.reshape(heads, HEAD_DIM).sum(axis=0)


def _seg_rms_fwd(x, gain, *, width, col, name):
    R = x.shape[0]
    tr = _row_tile(R, width)

    def body(x_ref, g_ref, o_ref):
        xv = x_ref[...]
        r = lax.rsqrt(_seg_mean(xv * xv) + EPS)
        o_ref[...] = ((xv * r) * g_ref[...]).astype(o_ref.dtype)

    return pl.pallas_call(
        body, name=name, grid=(R // tr,), in_specs=[_cols(tr, width, col), pl.BlockSpec((1, width), lambda i: (0, 0))],
        out_specs=_cols(tr, width, 0), out_shape=jax.ShapeDtypeStruct((R, width), BF16), compiler_params=_params("parallel"),
    )(x, gain)


def _seg_rms_bwd(x, gain, dys, *, width, col, name, out_buf=None, out_col=0):
    R = x.shape[0]
    tr = _row_tile(R, width)
    n_dy = len(dys)

    def body(*refs):
        x_ref, g_ref = refs[0], refs[1]
        dy_refs = refs[2:2 + n_dy]
        dx_ref, dg_ref = refs[-2], refs[-1]
        i = pl.program_id(0)
        xv = x_ref[...]
        dy = dy_refs[0][...]
        for extra in dy_refs[1:]:
            dy = dy + extra[...]
        r = lax.rsqrt(_seg_mean(xv * xv) + EPS)
        xh = xv * r
        part = jnp.sum(dy * xh, axis=0, keepdims=True)

        @pl.when(i == 0)
        def _():
            dg_ref[...] = part

        @pl.when(i > 0)
        def _():
            dg_ref[...] += part

        gdy = dy * g_ref[...]
        dx_ref[...] = (r * (gdy - xh * _seg_mean(gdy * xh))).astype(dx_ref.dtype)

    vec = pl.BlockSpec((1, width), lambda i: (0, 0))
    in_specs = [_cols(tr, width, col), vec] + [_cols(tr, width, 0)] * n_dy
    operands = [x, gain, *dys]
    aliases = {}
    dx_shape = jax.ShapeDtypeStruct((R, width), BF16)
    if out_buf is not None:
        in_specs.append(ANY)
        operands.append(out_buf)
        aliases = {len(operands) - 1: 0}
        dx_shape = jax.ShapeDtypeStruct(out_buf.shape, out_buf.dtype)
    return pl.pallas_call(
        body, name=name, grid=(R // tr,), in_specs=in_specs, out_specs=(_cols(tr, width, out_col), vec),
        out_shape=(dx_shape, jax.ShapeDtypeStruct((1, width), F32)), input_output_aliases=aliases,
        compiler_params=_params("arbitrary"),
    )(*operands)


def _sum_into(a, b, out_buf, out_col, *, name):
    R, width = a.shape
    tr = _row_tile(R, width)

    def body(a_ref, b_ref, _, o_ref):
        o_ref[...] = (a_ref[...] + b_ref[...]).astype(o_ref.dtype)

    return pl.pallas_call(
        body, name=name, grid=(R // tr,), in_specs=[_cols(tr, width, 0), _cols(tr, width, 0), ANY],
        out_specs=_cols(tr, width, out_col), out_shape=jax.ShapeDtypeStruct(out_buf.shape, out_buf.dtype),
        input_output_aliases={2: 0}, compiler_params=_params("parallel"),
    )(a, b, out_buf)


def _pool_lane_group():
    return lax.broadcasted_iota(jnp.int32, (1, MAIN_WIDTH), 1) // POOL_GROUP_DIM


def _pool_pick(group, per_window):
    s1, s2, s3, s4 = per_window
    return jnp.where(group == 0, s1, jnp.where(group == 1, s2, jnp.where(group == 2, s3, s4)))


def _pool_delta(u_ref, halo_ref, tile):
    group = _pool_lane_group()
    halo = jnp.where(tile == 0, 0.0, halo_ref[...])
    ext = jnp.concatenate([halo, u_ref[...]], axis=0)
    n = ext.shape[0]
    s1 = ext + pltpu.roll(ext, 1, 0)
    s2 = s1 + pltpu.roll(s1, 2, 0)
    s3 = s2 + pltpu.roll(s2, 4, 0)
    s4 = s3 + pltpu.roll(s3, 8, 0)
    ws = _pool_pick(group, (s1, s2, s3, s4))[POOL_HALO:n]
    t = tile * POOL_TILE + lax.broadcasted_iota(jnp.int32, (POOL_TILE, 1), 0)
    cnt = jnp.minimum(t + 1, _pool_pick(group, (2, 4, 8, 16))).astype(F32)
    return ws / cnt - u_ref[...], cnt


def _pool_in_specs():
    per_tile = POOL_TILE // POOL_HALO
    cur = _cols(POOL_TILE, MAIN_WIDTH, 0)
    prev = pl.BlockSpec((POOL_HALO, MAIN_WIDTH), lambda i: (jnp.maximum(i * per_tile - 1, 0), 0))
    mix = pl.BlockSpec((MAIN_WIDTH, MAIN_WIDTH), lambda i: (0, 0))
    vec = pl.BlockSpec((1, MAIN_WIDTH), lambda i: (0, 0))
    return cur, prev, mix, vec


def _pool_mix_fwd(proj, mix, scale, cat, *, name):
    T = proj.shape[0]
    assert T % POOL_TILE == 0
    cur, prev, mix_spec, vec = _pool_in_specs()

    def body(u_ref, halo_ref, mix_ref, sc_ref, _, o_ref):
        d, _cnt = _pool_delta(u_ref, halo_ref, pl.program_id(0))
        mixed = jnp.dot(d.astype(BF16), mix_ref[...].astype(BF16), preferred_element_type=F32)
        o_ref[...] = (mixed * sc_ref[...]).astype(o_ref.dtype)

    return pl.pallas_call(
        body, name=name, grid=(T // POOL_TILE,), in_specs=[cur, prev, mix_spec, vec, ANY], out_specs=cur,
        out_shape=jax.ShapeDtypeStruct(cat.shape, cat.dtype), input_output_aliases={4: 0}, compiler_params=_params("parallel"),
    )(proj, proj, mix, scale, cat)


def _pool_mix_bwd(proj, mix, scale, dcat, *, name):
    T = proj.shape[0]
    nt = T // POOL_TILE
    per_tile = POOL_TILE // POOL_HALO
    cur, prev, mix_spec, vec = _pool_in_specs()
    nxt = pl.BlockSpec((POOL_HALO, MAIN_WIDTH), lambda i: (jnp.minimum((i + 1) * per_tile, nt * per_tile - 1), 0))

    def body(u_ref, halo_ref, mix_ref, sc_ref, do_ref, donext_ref, du_ref, dmix_ref, dsc_ref):
        tile = pl.program_id(0)
        group = _pool_lane_group()
        d, cnt = _pool_delta(u_ref, halo_ref, tile)
        mixb = mix_ref[...].astype(BF16)
        db = d.astype(BF16)
        mixed = jnp.dot(db, mixb, preferred_element_type=F32)
        dout = do_ref[...]
        dsc = jnp.sum(dout * mixed, axis=0, keepdims=True)
        sc = sc_ref[...]
        dmixed = (dout * sc).astype(BF16)
        dmix = lax.dot_general(db, dmixed, (((0,), (0,)), ((), ())), preferred_element_type=F32)

        @pl.when(tile == 0)
        def _():
            dmix_ref[...] = dmix
            dsc_ref[...] = dsc

        @pl.when(tile > 0)
        def _():
            dmix_ref[...] += dmix
            dsc_ref[...] += dsc

        dnext = jnp.where(tile == nt - 1, 0.0, donext_ref[...])
        dmixed_ext = jnp.concatenate([dmixed, (dnext * sc).astype(BF16)], axis=0)
        dd_ext = lax.dot_general(dmixed_ext, mixb, (((1,), (1,)), ((), ())), preferred_element_type=F32)
        window = _pool_pick(group, (2.0, 4.0, 8.0, 16.0))
        cnt_ext = jnp.concatenate([cnt, jnp.broadcast_to(window, (POOL_HALO, MAIN_WIDTH))], axis=0)
        q = dd_ext / cnt_ext
        n = q.shape[0]
        r1 = q + pltpu.roll(q, n - 1, 0)
        r2 = r1 + pltpu.roll(r1, n - 2, 0)
        r3 = r2 + pltpu.roll(r2, n - 4, 0)
        r4 = r3 + pltpu.roll(r3, n - 8, 0)
        back = _pool_pick(group, (r1, r2, r3, r4))
        du_ref[...] = (back[0:POOL_TILE] - dd_ext[0:POOL_TILE]).astype(du_ref.dtype)

    return pl.pallas_call(
        body, name=name, grid=(nt,), in_specs=[cur, prev, mix_spec, vec, cur, nxt], out_specs=(cur, mix_spec, vec),
        out_shape=(jax.ShapeDtypeStruct((T, D_MODEL), BF16), jax.ShapeDtypeStruct((MAIN_WIDTH, MAIN_WIDTH), F32),
                   jax.ShapeDtypeStruct((1, MAIN_WIDTH), F32)),
        compiler_params=_params("arbitrary"),
    )(proj, proj, mix, scale, dcat, dcat)


def _head(a, h):
    return a[:, h * HEAD_DIM:(h + 1) * HEAD_DIM]


def _swa_mask(blk):
    rows = SWA_GROUP * BLOCK
    qi = lax.broadcasted_iota(jnp.int32, (rows, 2 * BLOCK), 0) % BLOCK
    kj = lax.broadcasted_iota(jnp.int32, (rows, 2 * BLOCK), 1)
    dist = qi + BLOCK - kj
    valid = (dist >= 0) & (dist < BLOCK) & ((blk > 0) | (kj >= BLOCK))
    return dist.astype(F32), valid


def _swa_head_terms(sink_ref, kvh, dist):
    grp = lax.broadcasted_iota(jnp.int32, (SWA_GROUP * BLOCK, 1), 0) // BLOCK
    slopes = [2.0 ** (-8.0 * (kvh * SWA_GROUP + g + 1) / SWA_Q_HEADS) for g in range(SWA_GROUP)]
    sinks = [sink_ref[kvh * SWA_GROUP + g] for g in range(SWA_GROUP)]
    slope = jnp.where(grp == 0, slopes[0], jnp.where(grp == 1, slopes[1], slopes[2]))
    sink = jnp.where(grp == 0, sinks[0], jnp.where(grp == 1, sinks[1], sinks[2]))
    return slope * dist, sink


def _stack_heads(a, kvh):
    return jnp.concatenate([_head(a, kvh * SWA_GROUP + g) for g in range(SWA_GROUP)], axis=0)


def _swa_specs(nb):
    def at(n):
        return jnp.minimum(n, nb - 1)

    q = pl.BlockSpec((BLOCK, MAIN_WIDTH), lambda n: (at(n), 0))
    k_prev = pl.BlockSpec((BLOCK, KV_HALF), lambda n: (jnp.maximum(at(n) - 1, 0), 0))
    k_cur = pl.BlockSpec((BLOCK, KV_HALF), lambda n: (at(n), 0))
    v_prev = pl.BlockSpec((BLOCK, KV_HALF), lambda n: (jnp.maximum(at(n) - 1, 0), 1))
    v_cur = pl.BlockSpec((BLOCK, KV_HALF), lambda n: (at(n), 1))
    return q, k_prev, k_cur, v_prev, v_cur


def _swa_attn_fwd(qn, kn, kv, sinks, cat, *, name):
    T = qn.shape[0]
    nb = T // BLOCK
    q_spec, k_prev, k_cur, v_prev, v_cur = _swa_specs(nb)

    def body(sink_ref, q_ref, kp_ref, kc_ref, vp_ref, vc_ref, _, o_ref):
        dist, valid = _swa_mask(pl.program_id(0))
        kk = jnp.concatenate([kp_ref[...], kc_ref[...]], axis=0)
        vv = jnp.concatenate([vp_ref[...], vc_ref[...]], axis=0).astype(BF16)
        q = q_ref[...]
        outs = []
        for kvh in range(SWA_KV_HEADS):
            bias, sink = _swa_head_terms(sink_ref, kvh, dist)
            p, _ps = _softmax(_stack_heads(q, kvh), _head(kk, kvh), bias, valid, sink)
            o = jnp.dot(p.astype(BF16), _head(vv, kvh), preferred_element_type=F32)
            outs += [o[g * BLOCK:(g + 1) * BLOCK] for g in range(SWA_GROUP)]
        o_ref[...] = jnp.concatenate(outs, axis=1).astype(o_ref.dtype)

    return pl.pallas_call(
        body, name=name, grid=(nb,),
        in_specs=[pl.BlockSpec(memory_space=pltpu.SMEM), q_spec, k_prev, k_cur, v_prev, v_cur, ANY], out_specs=q_spec,
        out_shape=jax.ShapeDtypeStruct(cat.shape, cat.dtype), input_output_aliases={6: 0}, compiler_params=_params("parallel"),
    )(sinks, qn, kn, kn, kv, kv, cat)


def _swa_attn_bwd(qn, kn, kv, sinks, dcat, dqn, *, name):
    T = qn.shape[0]
    nb = T // BLOCK
    q_spec, k_prev, k_cur, v_prev, v_cur = _swa_specs(nb)
    late = pl.BlockSpec((BLOCK, KV_HALF), lambda n: (jnp.maximum(n - 1, 0), 0))
    tn_dims = (((0,), (0,)), ((), ()))

    def body(sink_ref, q_ref, do_ref, kp_ref, kc_ref, vp_ref, vc_ref, _, dq_ref, dk_ref, dv_ref, ds_ref, ck, cv):
        blk = pl.program_id(0)

        @pl.when(blk == 0)
        def _():
            ck[...] = jnp.zeros_like(ck)
            cv[...] = jnp.zeros_like(cv)
            ds_ref[...] = jnp.zeros_like(ds_ref)

        @pl.when(blk < nb)
        def _():
            dist, valid = _swa_mask(blk)
            kk = jnp.concatenate([kp_ref[...], kc_ref[...]], axis=0)
            vv = jnp.concatenate([vp_ref[...], vc_ref[...]], axis=0).astype(BF16)
            q = q_ref[...]
            dout = do_ref[...].astype(BF16)
            lane = lax.broadcasted_iota(jnp.int32, (1, LANES), 1)
            dsinks = jnp.zeros((1, LANES), F32)
            dqs, dks, dvs = [], [], []
            for kvh in range(SWA_KV_HEADS):
                bias, sink = _swa_head_terms(sink_ref, kvh, dist)
                qq, kh, vh, dd = _stack_heads(q, kvh), _head(kk, kvh), _head(vv, kvh), _stack_heads(dout, kvh)
                p, ps = _softmax(qq, kh, bias, valid, sink)
                dp = lax.dot_general(dd, vh, (((1,), (1,)), ((), ())), preferred_element_type=F32)
                dsum = jnp.sum(p * dp, axis=-1, keepdims=True)
                ds = (p * (dp - dsum)).astype(BF16)
                dq = jnp.dot(ds, kh, preferred_element_type=F32) * SCALE
                dqs += [dq[g * BLOCK:(g + 1) * BLOCK] for g in range(SWA_GROUP)]
                dks.append(lax.dot_general(ds, qq, tn_dims, preferred_element_type=F32) * SCALE)
                dvs.append(lax.dot_general(p.astype(BF16), dd, tn_dims, preferred_element_type=F32))
                dsink = -(ps * dsum)
                for g in range(SWA_GROUP):
                    dsinks = dsinks + jnp.where(lane == kvh * SWA_GROUP + g, jnp.sum(dsink[g * BLOCK:(g + 1) * BLOCK]), 0.0)
            dq_ref[...] = jnp.concatenate(dqs, axis=1)
            dk = jnp.concatenate(dks, axis=1)
            dv = jnp.concatenate(dvs, axis=1)
            dk_ref[...] = ck[...] + dk[0:BLOCK]
            dv_ref[...] = cv[...] + dv[0:BLOCK]
            ck[...] = dk[BLOCK:2 * BLOCK]
            cv[...] = dv[BLOCK:2 * BLOCK]
            ds_ref[...] += dsinks

        @pl.when(blk == nb)
        def _():
            dk_ref[...] = ck[...]
            dv_ref[...] = cv[...]

    return pl.pallas_call(
        body, name=name, grid=(nb + 1,),
        in_specs=[pl.BlockSpec(memory_space=pltpu.SMEM), q_spec, q_spec, k_prev, k_cur, v_prev, v_cur, ANY],
        out_specs=(q_spec, late, late, pl.BlockSpec((1, LANES), lambda n: (0, 0))),
        out_shape=(jax.ShapeDtypeStruct(dqn.shape, dqn.dtype), jax.ShapeDtypeStruct((T, KV_HALF), F32),
                   jax.ShapeDtypeStruct((T, KV_HALF), F32), jax.ShapeDtypeStruct((1, LANES), F32)),
        scratch_shapes=[pltpu.VMEM((BLOCK, KV_HALF), F32), pltpu.VMEM((BLOCK, KV_HALF), F32)],
        input_output_aliases={7: 0}, compiler_params=_params("arbitrary"),
    )(sinks, qn, dcat, kn, kn, kv, kv, dqn)


def _mem_specs(M, tq, q_col):
    q = _cols(tq, MEM_WIDTH, q_col)
    k = pl.BlockSpec((M, MEM_WIDTH), lambda i: (0, 0))
    v = pl.BlockSpec((M, MEM_WIDTH), lambda i: (0, 1))
    return q, k, v


def _mem_attn_fwd(q, q_col, mkn, mkv, *, name):
    T = q.shape[0]
    M = mkn.shape[0]
    tq = min(T, MEM_Q_TILE)
    q_spec, k_spec, v_spec = _mem_specs(M, tq, q_col)

    def body(q_ref, k_ref, v_ref, o_ref):
        qq, kk, vv = q_ref[...], k_ref[...], v_ref[...].astype(BF16)
        outs = []
        for h in range(MEM_HEADS):
            p, _ps = _softmax(_head(qq, h), _head(kk, h), None, None, None)
            outs.append(jnp.dot(p.astype(BF16), _head(vv, h), preferred_element_type=F32))
        o_ref[...] = jnp.concatenate(outs, axis=1).astype(o_ref.dtype)

    return pl.pallas_call(
        body, name=name, grid=(T // tq,), in_specs=[q_spec, k_spec, v_spec], out_specs=_cols(tq, MEM_WIDTH, MAIN_WIDTH // MEM_WIDTH),
        out_shape=jax.ShapeDtypeStruct((T, D_MODEL), BF16), compiler_params=_params("parallel"),
    )(q, mkn, mkv)


def _mem_attn_bwd(q, q_col, mkn, mkv, dcat, *, dq_width, name):
    T = q.shape[0]
    M = mkn.shape[0]
    tq = min(T, MEM_Q_TILE)
    q_spec, k_spec, v_spec = _mem_specs(M, tq, q_col)
    last = MAIN_WIDTH // MEM_WIDTH
    tn_dims = (((0,), (0,)), ((), ()))

    def body(q_ref, do_ref, k_ref, v_ref, dq_ref, dk_ref, dv_ref):
        i = pl.program_id(0)
        qq, kk, vv, dout = q_ref[...], k_ref[...], v_ref[...].astype(BF16), do_ref[...].astype(BF16)
        dqs, dks, dvs = [], [], []
        for h in range(MEM_HEADS):
            qh, kh, vh, dh = _head(qq, h), _head(kk, h), _head(vv, h), _head(dout, h)
            p, _ps = _softmax(qh, kh, None, None, None)
            dp = lax.dot_general(dh, vh, (((1,), (1,)), ((), ())), preferred_element_type=F32)
            dsum = jnp.sum(p * dp, axis=-1, keepdims=True)
            ds = (p * (dp - dsum)).astype(BF16)
            dqs.append(jnp.dot(ds, kh, preferred_element_type=F32) * SCALE)
            dks.append(lax.dot_general(ds, qh, tn_dims, preferred_element_type=F32) * SCALE)
            dvs.append(lax.dot_general(p.astype(BF16), dh, tn_dims, preferred_element_type=F32))
        dq_ref[...] = jnp.concatenate(dqs, axis=1)
        dk = jnp.concatenate(dks, axis=1)
        dv = jnp.concatenate(dvs, axis=1)

        @pl.when(i == 0)
        def _():
            dk_ref[...] = dk
            dv_ref[...] = dv

        @pl.when(i > 0)
        def _():
            dk_ref[...] += dk
            dv_ref[...] += dv

    acc = pl.BlockSpec((M, MEM_WIDTH), lambda i: (0, 0))
    return pl.pallas_call(
        body, name=name, grid=(T // tq,), in_specs=[q_spec, _cols(tq, MEM_WIDTH, last), k_spec, v_spec],
        out_specs=(_cols(tq, MEM_WIDTH, dq_width // MEM_WIDTH - 1), acc, acc),
        out_shape=(jax.ShapeDtypeStruct((T, dq_width), F32), jax.ShapeDtypeStruct((M, MEM_WIDTH), F32),
                   jax.ShapeDtypeStruct((M, MEM_WIDTH), F32)),
        compiler_params=_params("arbitrary"),
    )(q, dcat, mkn, mkv)


def _loss(y, target, *, name):
    T, D = y.shape
    tr = _row_tile(T, D)

    def body(y_ref, t_ref, l_ref, dy_ref, dyb_ref):
        i = pl.program_id(0)
        err = y_ref[...] - t_ref[...]
        dy = err / float(D)
        dy_ref[...] = dy
        dyb_ref[...] = dy.astype(BF16)
        part = jnp.full((8, 128), 0.5 * jnp.sum(jnp.mean(err * err, axis=-1)), F32)

        @pl.when(i == 0)
        def _():
            l_ref[...] = part

        @pl.when(i > 0)
        def _():
            l_ref[...] += part

    row = pl.BlockSpec((tr, D), lambda i: (i, 0))
    return pl.pallas_call(
        body, name=name, grid=(T // tr,), in_specs=[row, row],
        out_specs=(pl.BlockSpec((8, 128), lambda i: (0, 0)), row, row),
        out_shape=(jax.ShapeDtypeStruct((8, 128), F32), jax.ShapeDtypeStruct((T, D), F32), jax.ShapeDtypeStruct((T, D), BF16)),
        compiler_params=_params("arbitrary"),
    )(y, target)


def _position():
    return lax.axis_index("x"), lax.axis_index("y"), lax.axis_index("c")


def _all_gather(arrays, *, name):
    n = len(arrays)

    def body(*refs):
        srcs, outs = refs[:n], refs[n:2 * n]
        token, send_sems, recv_sems, local_sems = refs[2 * n:]
        token[...] = jnp.zeros_like(token)
        x, y, c = _position()
        me, sibling = (x, y, c), (x, y, 1 - c)
        chips = [(1 - x, y), (x, 1 - y), (1 - x, 1 - y)]

        def slot(a, px, py, pc):
            return outs[a].at[4 * px + 2 * py + pc]

        def copy(a, k, block, to, src=None):
            return pltpu.make_async_remote_copy(
                src_ref=slot(a, *block) if src is None else src, dst_ref=slot(a, *block),
                send_sem=send_sems.at[a, k], recv_sem=recv_sems.at[a, k], device_id=to, device_id_type=MESH)

        mine = [pltpu.make_async_copy(srcs[a], slot(a, *me), local_sems.at[a]) for a in range(n)]
        for cp in mine:
            cp.start()
        first, passed = [], []
        for a in range(n):
            first.append(copy(a, 0, me, sibling, src=srcs[a]))
            first += [copy(a, 1 + j, me, (*chip, c), src=srcs[a]) for j, chip in enumerate(chips)]
        for cp in first:
            cp.start()
        for a in range(n):
            for j, chip in enumerate(chips):
                copy(a, 1 + j, (*chip, c), me).wait_recv()
                fwd = copy(a, 4 + j, (*chip, c), sibling)
                fwd.start()
                passed.append(fwd)
        for a in range(n):
            copy(a, 0, sibling, me).wait_recv()
            for j, chip in enumerate(chips):
                copy(a, 4 + j, (*chip, 1 - c), me).wait_recv()
        for cp in first + passed:
            cp.wait_send()
        for cp in mine:
            cp.wait()

    return pl.pallas_call(
        body, name=name, in_specs=[ANY] * n, out_specs=[ANY] * n + [pl.BlockSpec(memory_space=pltpu.VMEM)],
        out_shape=[jax.ShapeDtypeStruct((N_DEV,) + a.shape, a.dtype) for a in arrays] + [jax.ShapeDtypeStruct((8, 128), F32)],
        scratch_shapes=[pltpu.SemaphoreType.DMA((n, 7)), pltpu.SemaphoreType.DMA((n, 7)), pltpu.SemaphoreType.DMA((n,))],
    )(*arrays)


def _sibling_exchange(by_core, whole, *, name):
    n1, n = len(by_core), len(by_core) + len(whole)

    def body(*refs):
        srcs, outs = refs[:n], refs[n:2 * n]
        send_sems, recv_sems = refs[2 * n:]
        x, y, c = _position()
        copies = [
            pltpu.make_async_remote_copy(src_ref=srcs[a].at[:, 1 - c] if a < n1 else srcs[a], dst_ref=outs[a],
                                         send_sem=send_sems.at[a], recv_sem=recv_sems.at[a], device_id=(x, y, 1 - c),
                                         device_id_type=MESH)
            for a in range(n)]
        for cp in copies:
            cp.start()
        for cp in copies:
            cp.wait()

    out_shape = [jax.ShapeDtypeStruct(a.shape[:1] + a.shape[2:], a.dtype) for a in by_core]
    out_shape += [jax.ShapeDtypeStruct(a.shape, a.dtype) for a in whole]
    outs = pl.pallas_call(
        body, name=name, in_specs=[ANY] * n, out_specs=[ANY] * n, out_shape=out_shape,
        scratch_shapes=[pltpu.SemaphoreType.DMA((n,)), pltpu.SemaphoreType.DMA((n,))],
    )(*by_core, *whole)
    return outs[:n1], outs[n1:]


def _chip_exchange(per_chip, whole, *, name):
    n1, n = len(per_chip), len(per_chip) + len(whole)

    def body(*refs):
        srcs, outs = refs[:n], refs[n:2 * n]
        send_sems, recv_sems, local_sems = refs[2 * n:]
        x, y, c = _position()
        my_chip = 2 * x + y
        chips = [(1 - x, y), (x, 1 - y), (1 - x, 1 - y)]

        def src(a, chip):
            return srcs[a].at[chip] if a < n1 else srcs[a]

        local = [pltpu.make_async_copy(src(a, my_chip), outs[a].at[my_chip], local_sems.at[a]) for a in range(n)]
        for cp in local:
            cp.start()
        copies = [
            pltpu.make_async_remote_copy(src_ref=src(a, 2 * px + py), dst_ref=outs[a].at[my_chip],
                                         send_sem=send_sems.at[a, j], recv_sem=recv_sems.at[a, j], device_id=(px, py, c),
                                         device_id_type=MESH)
            for a in range(n) for j, (px, py) in enumerate(chips)]
        for cp in copies:
            cp.start()
        for cp in copies:
            cp.wait()
        for cp in local:
            cp.wait()

    out_shape = [jax.ShapeDtypeStruct(a.shape, a.dtype) for a in per_chip]
    out_shape += [jax.ShapeDtypeStruct((N_CHIP,) + a.shape, a.dtype) for a in whole]
    outs = pl.pallas_call(
        body, name=name, in_specs=[ANY] * n, out_specs=[ANY] * n, out_shape=out_shape,
        scratch_shapes=[pltpu.SemaphoreType.DMA((n, 3)), pltpu.SemaphoreType.DMA((n, 3)), pltpu.SemaphoreType.DMA((n,))],
    )(*per_chip, *whole)
    return outs[:n1], outs[n1:]


HBM = pl.BlockSpec(memory_space=pltpu.HBM)
SEM = pl.BlockSpec(memory_space=pltpu.SEMAPHORE)
DATAFLOW = pltpu.SideEffectType.DATAFLOW_SIDE_EFFECTING


def _device(flat):
    return flat // 4, (flat // 2) % 2, flat % 2


def _gather_copies(srcs, lands, send_sems, recv_sems, incoming):
    x, y, c = _position()
    me = 4 * x + 2 * y + c
    pairs = []
    for a in range(len(srcs)):
        for d in range(1, N_DEV):
            to, frm = (me + d) % N_DEV, (me + N_DEV - d) % N_DEV
            k = a * (N_DEV - 1) + d - 1
            sems = dict(send_sem=send_sems.at[k], recv_sem=recv_sems.at[k], device_id_type=MESH)
            out = pltpu.make_async_remote_copy(src_ref=srcs[a], dst_ref=lands[a].at[me], device_id=_device(to), **sems)
            inc = pltpu.make_async_remote_copy(src_ref=srcs[a], dst_ref=lands[a].at[frm], device_id=_device(frm),
                                               **sems) if incoming else None
            pairs.append((out, inc))
    return pairs


def _chip_copies(srcs, lands, send_sems, recv_sems, incoming, n_whole=0):
    x, y, c = _position()
    my_chip = 2 * x + y
    pairs = []
    for a in range(len(srcs)):
        for k, (px, py) in enumerate([(1 - x, y), (x, 1 - y), (1 - x, 1 - y)]):
            sem = a * (N_CHIP - 1) + k
            sems = dict(send_sem=send_sems.at[sem], recv_sem=recv_sems.at[sem], device_id=(px, py, c), device_id_type=MESH)
            src = srcs[a] if a >= len(srcs) - n_whole else srcs[a].at[2 * px + py]
            out = pltpu.make_async_remote_copy(src_ref=src, dst_ref=lands[a].at[my_chip], **sems)
            inc = pltpu.make_async_remote_copy(src_ref=src, dst_ref=lands[a].at[2 * px + py], **sems) if incoming else None
            pairs.append((out, inc))
    return pairs


def _sibling_copies(srcs, lands, send_sems, recv_sems, incoming):
    x, y, c = _position()
    pairs = []
    for a in range(len(srcs)):
        copy = pltpu.make_async_remote_copy(src_ref=srcs[a].at[:, 1 - c], dst_ref=lands[a], send_sem=send_sems.at[a],
                                            recv_sem=recv_sems.at[a], device_id=(x, y, 1 - c), device_id_type=MESH)
        pairs.append((copy, copy if incoming else None))
    return pairs


def _gather_own(srcs, lands):
    x, y, c = _position()
    return [(s, l.at[4 * x + 2 * y + c]) for s, l in zip(srcs, lands)]


def _chip_own(srcs, lands, n_whole=0):
    x, y, _ = _position()
    my_chip = 2 * x + y
    return [(s if a >= len(srcs) - n_whole else s.at[my_chip], l.at[my_chip]) for a, (s, l) in enumerate(zip(srcs, lands))]


def _push_start(copies, own, fan, srcs, lands, *, name):
    n = len(srcs)

    def body(*refs):
        src_refs, land_refs = refs[:n], refs[n:2 * n]
        send_sems, recv_sems = refs[2 * n], refs[2 * n + 1]
        token, local_sems = refs[-2], refs[-1]
        local = [pltpu.make_async_copy(s, d, local_sems.at[i]) for i, (s, d) in enumerate(own(src_refs, land_refs))]
        for cp in local:
            cp.start()
        for out, _ in copies(src_refs, land_refs, send_sems, recv_sems, False):
            out.start()
        token[...] = jnp.zeros_like(token)
        for cp in local:
            cp.wait()

    outs = pl.pallas_call(
        body, name=name,
        out_shape=(pltpu.SemaphoreType.DMA((n * fan,)), pltpu.SemaphoreType.DMA((n * fan,)),
                   *(pltpu.HBM(a.shape, a.dtype) for a in srcs), *(pltpu.HBM(a.shape, a.dtype) for a in lands),
                   jax.ShapeDtypeStruct((8, 128), F32)),
        in_specs=[HBM] * (2 * n), out_specs=(SEM, SEM, *([HBM] * (2 * n)), pl.BlockSpec(memory_space=pltpu.VMEM)),
        input_output_aliases={i: 2 + i for i in range(2 * n)}, scratch_shapes=[pltpu.SemaphoreType.DMA((n,))],
        compiler_params=pltpu.CompilerParams(has_side_effects=DATAFLOW),
    )(*(pltpu.with_memory_space_constraint(a, pltpu.HBM) for a in (*srcs, *lands)))
    return outs[0], outs[1], list(outs[2:2 + n]), list(outs[2 + n:2 + 2 * n]), outs[-1]


def _push_wait(copies, send_sems, recv_sems, srcs, lands, after, *, name):
    n = len(srcs)

    def body(*refs):
        src_refs, land_refs = refs[:n], refs[n:2 * n]
        for out, inc in copies(src_refs, land_refs, refs[2 * n], refs[2 * n + 1], True):
            out.wait_send()
            inc.wait_recv()
        refs[-1][...] = jnp.zeros_like(refs[-1])

    outs = pl.pallas_call(
        body, name=name,
        out_shape=(*(pltpu.HBM(a.shape, a.dtype) for a in (*srcs, *lands)), jax.ShapeDtypeStruct((8, 128), F32)),
        in_specs=[HBM] * (2 * n) + [SEM, SEM, ANY], out_specs=(*([HBM] * (2 * n)), pl.BlockSpec(memory_space=pltpu.VMEM)),
        input_output_aliases={i: i for i in range(2 * n)},
        compiler_params=pltpu.CompilerParams(has_side_effects=DATAFLOW),
    )(*srcs, *lands, send_sems, recv_sems, after)
    return list(outs[n:2 * n]), outs[-1]


def _with_own_slot(block, index, slots):
    buf = lax.empty((slots,) + block.shape, block.dtype)
    return lax.dynamic_update_slice(buf, block[None], (index,) + (0,) * block.ndim)


def _view2d(shape):
    return math.prod(shape[:-1]), shape[-1]


def _pair_sum(mine, other, core, *, name, out_dtype):
    by_core = mine.ndim == 4
    n, w = other.shape[-2:]
    tr = _row_tile(n, w * 2)
    lead = other.shape[0] if by_core else 1

    def body(core_ref, a_ref, b_ref, o_ref):
        o_ref[...] = (a_ref[...].astype(F32) + b_ref[...].astype(F32)).astype(o_ref.dtype)

    if by_core:
        a_spec = pl.BlockSpec((None, None, tr, w), lambda j, i, core_ref: (j, core_ref[0], i, 0))
        o_spec = pl.BlockSpec((None, tr, w), lambda j, i, core_ref: (j, i, 0))
    else:
        a_spec = o_spec = pl.BlockSpec((tr, w), lambda j, i, core_ref: (i, 0))
    grid_spec = pltpu.PrefetchScalarGridSpec(num_scalar_prefetch=1, grid=(lead, n // tr), in_specs=[a_spec, o_spec],
                                             out_specs=o_spec)
    return pl.pallas_call(body, name=name, grid_spec=grid_spec, out_shape=jax.ShapeDtypeStruct(other.shape, out_dtype),
                          compiler_params=_params("parallel", "parallel"))(core.reshape(1), mine, other)


def _adamw(parts, w, m, v, *, name):
    layers = len(parts)
    n_parts, R, W = parts[0].shape
    tr = _row_tile(R, W * 2)
    per_layer = R // tr

    def update(p_ref, w_ref, m_ref, v_ref, g_out, d_out, m_out, v_out):
        g = p_ref[0].astype(F32)
        for j in range(1, n_parts):
            g = g + p_ref[j].astype(F32)
        m_new = ADAM_B1 * m_ref[...] + (1.0 - ADAM_B1) * g
        v_new = ADAM_B2 * v_ref[...] + (1.0 - ADAM_B2) * (g * g)
        m_hat = m_new / (1.0 - ADAM_B1 ** ADAM_STEP)
        v_hat = v_new / (1.0 - ADAM_B2 ** ADAM_STEP)
        g_out[...] = g
        d_out[...] = -ADAM_LR * (m_hat / (jnp.sqrt(v_hat) + ADAM_EPS) + ADAM_WD * w_ref[...])
        m_out[...] = m_new
        v_out[...] = v_new

    def body(*refs):
        for k in range(layers):
            pl.when(pl.program_id(0) == k)(lambda k=k: update(refs[k], *refs[layers:]))

    def parts_spec(k):
        return pl.BlockSpec((n_parts, tr, W), lambda l, i: (0, jnp.where(l == k, i, 0), 0))

    row = pl.BlockSpec((tr, W), lambda l, i: (l * per_layer + i, 0))
    out = jax.ShapeDtypeStruct((layers * R, W), F32)
    return pl.pallas_call(
        body, name=name, grid=(layers, per_layer), in_specs=[parts_spec(k) for k in range(layers)] + [row, row, row],
        out_specs=(row, row, row, row), out_shape=(out, out, out, out), compiler_params=_params("arbitrary", "arbitrary"),
    )(*parts, w, m, v)


SMALL_ROWS = 608


def _pack_small(p):
    flat = jnp.concatenate([p[n].reshape(-1).astype(F32) for n in SMALL_NAMES])
    return jnp.pad(flat, (0, SMALL_ROWS * PACK_W - flat.shape[0])).reshape(SMALL_ROWS, PACK_W)


def _unpack_small(buf, like):
    out, at = {}, 0
    flat = buf.reshape(-1)
    for n in SMALL_NAMES:
        size = math.prod(like[n].shape)
        out[n] = flat[at:at + size].reshape(like[n].shape)
        at += size
    return out


def _heads(a, nh):
    T = a.shape[0]
    return a.reshape(T, nh, HEAD_DIM).transpose(1, 0, 2).reshape(nh * T, HEAD_DIM)


def _unheads(a, nh):
    a = a.reshape(nh, -1, HEAD_DIM)
    return a.transpose(1, 0, 2).reshape(a.shape[1], nh * HEAD_DIM)


def _groups(a):
    T = a.shape[0]
    return a.reshape(T, POOL_GROUPS, POOL_GROUP_DIM).transpose(1, 0, 2)


def _ungroups(a):
    return a.transpose(1, 0, 2).reshape(a.shape[1], MAIN_WIDTH)


def _local_step(x, mem, target, p, w_kv, fetch, reduce_layer, reduce_wait):
    T = x.shape[0]
    M = mem.shape[0]
    saved = []
    h = x
    kn = vv = k_raw = h_kv = hn_kv = None
    for l in range(DEPTH):
        s = {}
        wl, token = fetch(l, h)
        s["w"] = wl
        if l == N_A:
            h_kv = h
            hn_kv = _rms_fwd(h, p["kv_norm"], name="kv_norm_fwd")
            kv = _mm(hn_kv, w_kv, b_kind="rows", name="kv_proj")
            k_raw = _heads(kv[:, :KV_HALF], SWA_KV_HEADS)
            kn = _rms_fwd(k_raw, p["k_norm"], name="k_norm_fwd").reshape(SWA_KV_HEADS, T, HEAD_DIM)
            vv = _heads(kv[:, KV_HALF:], SWA_KV_HEADS).astype(BF16).reshape(SWA_KV_HEADS, T, HEAD_DIM)
        s["h"] = h
        s["xn1"] = _rms_fwd(h, p["norm_mix"][l] + token, name="norm_mix_fwd")
        proj = _mm(s["xn1"], wl["w_in"], b_kind="rows", name="in_proj")
        s["mq_raw"] = _heads(proj[:, MAIN_WIDTH:], MEM_HEADS)
        s["mqn"] = _rms_fwd(s["mq_raw"], p["mem_q_norm"][l], name="mem_q_norm_fwd").reshape(MEM_HEADS, T, HEAD_DIM)
        s["memn"] = _rms_fwd(mem, p["mem_norm"][l], name="mem_norm_fwd")
        mkv = _mm(s["memn"], wl["w_mem_kv"], b_kind="rows", name="mem_kv_proj")
        s["mk_raw"] = _heads(mkv[:, :MEM_WIDTH], MEM_HEADS)
        s["mkn"] = _rms_fwd(s["mk_raw"], p["mem_k_norm"][l], name="mem_k_norm_fwd").reshape(MEM_HEADS, M, HEAD_DIM)
        s["mvv"] = _heads(mkv[:, MEM_WIDTH:], MEM_HEADS).astype(BF16).reshape(MEM_HEADS, M, HEAD_DIM)
        mem_out = _unheads(_mem_fwd(s["mqn"], s["mkn"], s["mvv"], name="mem_attn_fwd"), MEM_HEADS)
        if l < N_A:
            s["u"] = _groups(proj[:, :MAIN_WIDTH])
            s["pw"] = p["pool_w"][l]
            s["ps"] = p["pool_scale"][l].reshape(POOL_GROUPS, 1, POOL_GROUP_DIM)
            main_out = _ungroups(_pool_fwd(s["u"], s["pw"], s["ps"], name="pool_fwd"))
        else:
            j = l - N_A
            s["q_raw"] = _heads(proj[:, :MAIN_WIDTH], SWA_Q_HEADS)
            s["qn"] = _rms_fwd(s["q_raw"], p["q_norm"][j], name="q_norm_fwd").reshape(SWA_KV_HEADS, SWA_GROUP, T, HEAD_DIM)
            main_out = _unheads(_swa_fwd(s["qn"], kn, vv, p["sinks"][j], name="swa_fwd"), SWA_Q_HEADS)
        s["cat"] = jnp.concatenate([main_out, mem_out], axis=-1)
        s["h1"] = _mm(s["cat"], wl["w_out"], b_kind="rows", res=h, name="out_proj")
        s["xn2"] = _rms_fwd(s["h1"], p["norm_mlp"][l], name="norm_mlp_fwd")
        s["r"], s["a"] = _mm(s["xn2"], wl["w_up"], b_kind="layers", relu2=True, name="mlp_up")
        h = _mm(s["a"], wl["w_down"], b_kind="rows", res=s["h1"], name="mlp_down")
        saved.append(s)

    loss, dh, dh_b = _loss(h, target, name="loss_head")

    g = {n: [None] * DEPTH for n in ("norm_mix", "mem_norm", "mem_q_norm", "mem_k_norm", "norm_mlp")}
    g_kv = None
    token = None
    g.update({n: [None] * N_A for n in ("pool_w", "pool_scale", "q_norm", "sinks")})
    dkn = dvv = None
    for l in reversed(range(DEPTH)):
        s = saved[l]
        wl = s["w"]
        gb = {}

        def dw(a, dy, n):
            return _mm(a, dy, ta=True, out_kind="layers" if n == "w_up" else "rows", out_buf=lax.empty(wl[n].shape, BF16),
                       name=n + "_grad")

        norm_mlp_gain = p["norm_mlp"][l] if token is None else p["norm_mlp"][l] + token
        gb["w_down"] = dw(s["a"], dh_b, "w_down")
        du = _mm(dh_b, wl["w_down"], tb=True, b_kind="rows", mul2=s["a"], out_dtype=BF16, name="mlp_down_dx")
        gb["w_up"] = dw(s["xn2"], du, "w_up")
        dxn2 = _mm(du, wl["w_up"], tb=True, b_kind="layers", name="mlp_up_dx")
        dh1, dh1_b, g["norm_mlp"][l] = _rms_bwd(s["h1"], norm_mlp_gain, [dxn2], res=dh, also_bf16=True,
                                                name="norm_mlp_bwd")
        gb["w_out"] = dw(s["cat"], dh1_b, "w_out")
        dcat = _mm(dh1_b, wl["w_out"], tb=True, b_kind="rows", name="out_proj_dx")
        dmem_out = _heads(dcat[:, MAIN_WIDTH:], MEM_HEADS).astype(BF16).reshape(MEM_HEADS, T, HEAD_DIM)
        dmqn, dmkn, dmvv = _mem_bwd(s["mqn"], s["mkn"], s["mvv"], dmem_out, name="mem_attn_bwd")
        dmq_raw, g["mem_q_norm"][l] = _rms_bwd(s["mq_raw"], p["mem_q_norm"][l], [dmqn.reshape(MEM_HEADS * T, HEAD_DIM)],
                                               name="mem_q_norm_bwd")
        dmk_raw, g["mem_k_norm"][l] = _rms_bwd(s["mk_raw"], p["mem_k_norm"][l], [dmkn.reshape(MEM_HEADS * M, HEAD_DIM)],
                                               name="mem_k_norm_bwd")
        dmkv = jnp.concatenate([_unheads(dmk_raw, MEM_HEADS), _unheads(dmvv, MEM_HEADS)], axis=-1).astype(BF16)
        gb["w_mem_kv"] = dw(s["memn"], dmkv, "w_mem_kv")
        dmemn = _mm(dmkv, wl["w_mem_kv"], tb=True, b_kind="rows", name="mem_kv_proj_dx")
        g["mem_norm"][l] = _rms_bwd(mem, p["mem_norm"][l], [dmemn], want_dx=False, name="mem_norm_bwd")
        if l < N_A:
            dmain_out = _groups(dcat[:, :MAIN_WIDTH])
            du_pool, g["pool_w"][l], dps = _pool_bwd(s["u"], s["pw"], s["ps"], dmain_out, name="pool_bwd")
            g["pool_scale"][l] = dps.reshape(MAIN_WIDTH)
            dmain = _ungroups(du_pool)
        else:
            j = l - N_A
            dmain_out = _heads(dcat[:, :MAIN_WIDTH], SWA_Q_HEADS).astype(BF16).reshape(SWA_KV_HEADS, SWA_GROUP, T, HEAD_DIM)
            dqn, dk_l, dv_l, dsink = _swa_bwd(s["qn"], kn, vv, p["sinks"][j], dmain_out, name="swa_bwd")
            g["sinks"][j] = dsink[:, 0, :SWA_GROUP].reshape(SWA_Q_HEADS)
            dq_raw, g["q_norm"][j] = _rms_bwd(s["q_raw"], p["q_norm"][j], [dqn.reshape(SWA_Q_HEADS * T, HEAD_DIM)],
                                              name="q_norm_bwd")
            dmain = _unheads(dq_raw, SWA_Q_HEADS)
            dk_l = dk_l.reshape(SWA_KV_HEADS * T, HEAD_DIM)
            dv_l = dv_l.reshape(SWA_KV_HEADS * T, HEAD_DIM)
            dkn = dk_l if dkn is None else _add(dkn, dk_l, name="dk_sum")
            dvv = dv_l if dvv is None else _add(dvv, dv_l, name="dv_sum")
        dproj = jnp.concatenate([dmain, _unheads(dmq_raw, MEM_HEADS)], axis=-1).astype(BF16)
        gb["w_in"] = dw(s["xn1"], dproj, "w_in")
        dxn1 = _mm(dproj, wl["w_in"], tb=True, b_kind="rows", name="in_proj_dx")
        if l in (0, N_A):
            dh, g["norm_mix"][l] = _rms_bwd(s["h"], p["norm_mix"][l], [dxn1], res=dh1, name="norm_mix_bwd")
        else:
            dh, dh_b, g["norm_mix"][l] = _rms_bwd(s["h"], p["norm_mix"][l], [dxn1], res=dh1, also_bf16=True,
                                                  name="norm_mix_bwd")
        if l == N_A:
            dk_raw, g["k_norm"] = _rms_bwd(k_raw, p["k_norm"], [dkn], name="k_norm_bwd")
            dkv = jnp.concatenate([_unheads(dk_raw, SWA_KV_HEADS), _unheads(dvv, SWA_KV_HEADS)], axis=-1).astype(BF16)
            g_kv = _mm(hn_kv, dkv, ta=True, out_kind="rows", out_buf=lax.empty(w_kv.shape, BF16), name="w_kv_grad")
            dhn = _mm(dkv, w_kv, tb=True, b_kind="rows", name="kv_proj_dx")
            dh, dh_b, g["kv_norm"] = _rms_bwd(h_kv, p["kv_norm"], [dhn], res=dh, also_bf16=True, name="kv_norm_bwd")
        if l + 1 < DEPTH:
            reduce_wait(l + 1, dh)
        token = reduce_layer(l, gb)
    grads = {n: (jnp.stack(v) if isinstance(v, list) else v) for n, v in g.items()}
    return loss, dh, grads, g_kv


def _block_diag(pw):
    out = jnp.zeros((MAIN_WIDTH, MAIN_WIDTH), pw.dtype)
    for g in range(POOL_GROUPS):
        out = lax.dynamic_update_slice(out, pw[g], (g * POOL_GROUP_DIM, g * POOL_GROUP_DIM))
    return out


def _diag_blocks(m):
    return jnp.stack([m[g * POOL_GROUP_DIM:(g + 1) * POOL_GROUP_DIM, g * POOL_GROUP_DIM:(g + 1) * POOL_GROUP_DIM]
                      for g in range(POOL_GROUPS)])


def _train_pass(x, mem, target, p, w_kv, fetch, reduce_layer, reduce_wait):
    T = x.shape[0]
    mem_cols = MAIN_WIDTH // MEM_WIDTH
    k_gain = _head_gain(p["k_norm"], SWA_KV_HEADS)
    saved = []
    h = x
    kn = kv = h_kv = hn_kv = None
    for l in range(DEPTH):
        s = {}
        wl, token = fetch(l, h)
        s["w"] = wl
        if l == N_A:
            h_kv = h
            hn_kv, kv = _norm_mm(h, p["kv_norm"], w_kv, b_kind="rows", name="kv_proj")
            kn = _seg_rms_fwd(kv, k_gain, width=KV_HALF, col=0, name="k_norm_fwd")
        s["h"] = h
        s["xn1"], proj = _norm_mm(h, p["norm_mix"][l] + token, wl["w_in"], b_kind="rows", name="in_proj")
        s["proj"] = proj
        s["memn"] = _rms_fwd(mem, p["mem_norm"][l], name="mem_norm_fwd")
        s["mkv"] = _mm(s["memn"], wl["w_mem_kv"], b_kind="rows", name="mem_kv_proj")
        s["mk_gain"] = _head_gain(p["mem_k_norm"][l], MEM_HEADS)
        s["mkn"] = _seg_rms_fwd(s["mkv"], s["mk_gain"], width=MEM_WIDTH, col=0, name="mem_k_norm_fwd")
        if l < N_A:
            s["q_gain"] = _head_gain(p["mem_q_norm"][l], MEM_HEADS)
            s["qn"] = _seg_rms_fwd(proj, s["q_gain"], width=MEM_WIDTH, col=mem_cols, name="mem_q_norm_fwd")
            s["q_col"] = 0
        else:
            j = l - N_A
            s["q_gain"] = jnp.concatenate([_head_gain(p["q_norm"][j], SWA_Q_HEADS), _head_gain(p["mem_q_norm"][l], MEM_HEADS)],
                                          axis=1)
            s["qn"] = _seg_rms_fwd(proj, s["q_gain"], width=D_MODEL, col=0, name="q_norm_fwd")
            s["q_col"] = mem_cols
        cat = _mem_attn_fwd(s["qn"], s["q_col"], s["mkn"], s["mkv"], name="mem_attn_fwd")
        if l < N_A:
            s["mix"] = _block_diag(p["pool_w"][l])
            s["scale"] = p["pool_scale"][l].reshape(1, MAIN_WIDTH)
            s["cat"] = _pool_mix_fwd(proj, s["mix"], s["scale"], cat, name="pool_fwd")
        else:
            s["cat"] = _swa_attn_fwd(s["qn"], kn, kv, p["sinks"][l - N_A], cat, name="swa_fwd")
        s["h1"] = _mm(s["cat"], wl["w_out"], b_kind="rows", res=h, name="out_proj")
        s["xn2"], s["a"] = _norm_mm(s["h1"], p["norm_mlp"][l], wl["w_up"], b_kind="layers", relu2=True, name="mlp_up")
        h = _mm(s["a"], wl["w_down"], b_kind="rows", res=s["h1"], name="mlp_down")
        saved.append(s)

    loss, dh, dh_b = _loss(h, target, name="loss_head")

    g = {n: [None] * DEPTH for n in ("norm_mix", "mem_norm", "mem_q_norm", "mem_k_norm", "norm_mlp")}
    g.update({n: [None] * N_A for n in ("pool_w", "pool_scale", "q_norm", "sinks")})
    g_kv = None
    token = None
    dks, dvs = [], []
    for l in reversed(range(DEPTH)):
        s = saved[l]
        wl = s["w"]
        gb = {}

        def dw(a, dy, n):
            return _mm(a, dy, ta=True, out_kind="layers" if n == "w_up" else "rows", out_buf=lax.empty(wl[n].shape, BF16),
                       name=n + "_grad")

        norm_mlp_gain = p["norm_mlp"][l] if token is None else p["norm_mlp"][l] + token
        gb["w_down"] = dw(s["a"], dh_b, "w_down")
        du = _mm(dh_b, wl["w_down"], tb=True, b_kind="rows", mul2=s["a"], out_dtype=BF16, name="mlp_down_dx")
        gb["w_up"] = dw(s["xn2"], du, "w_up")
        early = reduce_layer(l, gb, early=True)
        if early is not None:
            norm_mlp_gain = norm_mlp_gain + early
        dh1, dh1_b, g["norm_mlp"][l] = _mm_rms_bwd(du, wl["w_up"], s["h1"], norm_mlp_gain, dh, b_kind="layers", also_bf16=True,
                                                   name="mlp_up_dx")
        gb["w_out"] = dw(s["cat"], dh1_b, "w_out")
        dcat = _mm(dh1_b, wl["w_out"], tb=True, b_kind="rows", name="out_proj_dx")
        if l < N_A:
            dq, dmk, dmv = _mem_attn_bwd(s["qn"], s["q_col"], s["mkn"], s["mkv"], dcat, dq_width=MEM_WIDTH, name="mem_attn_bwd")
            dproj, dmix, dscale = _pool_mix_bwd(s["proj"], s["mix"], s["scale"], dcat, name="pool_bwd")
            g["pool_w"][l] = _diag_blocks(dmix)
            g["pool_scale"][l] = dscale.reshape(MAIN_WIDTH)
            dproj, dgain = _seg_rms_bwd(s["proj"], s["q_gain"], [dq], width=MEM_WIDTH, col=mem_cols, out_buf=dproj,
                                        out_col=mem_cols, name="mem_q_norm_bwd")
            g["mem_q_norm"][l] = _fold_heads(dgain, MEM_HEADS)
        else:
            j = l - N_A
            dqn, dmk, dmv = _mem_attn_bwd(s["qn"], s["q_col"], s["mkn"], s["mkv"], dcat, dq_width=D_MODEL, name="mem_attn_bwd")
            dqn, dk_l, dv_l, dsinks = _swa_attn_bwd(s["qn"], kn, kv, p["sinks"][j], dcat, dqn, name="swa_bwd")
            dks.append(dk_l)
            dvs.append(dv_l)
            g["sinks"][j] = dsinks[0, :SWA_Q_HEADS]
            dproj, dgain = _seg_rms_bwd(s["proj"], s["q_gain"], [dqn], width=D_MODEL, col=0, name="q_norm_bwd")
            g["q_norm"][j] = _fold_heads(dgain[:, :MAIN_WIDTH], SWA_Q_HEADS)
            g["mem_q_norm"][l] = _fold_heads(dgain[:, MAIN_WIDTH:], MEM_HEADS)
        dmk_raw, dgain = _seg_rms_bwd(s["mkv"], s["mk_gain"], [dmk], width=MEM_WIDTH, col=0, name="mem_k_norm_bwd")
        g["mem_k_norm"][l] = _fold_heads(dgain, MEM_HEADS)
        dmkv = jnp.concatenate([dmk_raw, dmv.astype(BF16)], axis=1)
        gb["w_mem_kv"] = dw(s["memn"], dmkv, "w_mem_kv")
        dmemn = _mm(dmkv, wl["w_mem_kv"], tb=True, b_kind="rows", name="mem_kv_proj_dx")
        g["mem_norm"][l] = _rms_bwd(mem, p["mem_norm"][l], [dmemn], want_dx=False, name="mem_norm_bwd")
        gb["w_in"] = dw(s["xn1"], dproj, "w_in")
        if l in (0, N_A):
            dh, g["norm_mix"][l] = _mm_rms_bwd(dproj, wl["w_in"], s["h"], p["norm_mix"][l], dh1, b_kind="rows", also_bf16=False,
                                               name="in_proj_dx")
        else:
            dh, dh_b, g["norm_mix"][l] = _mm_rms_bwd(dproj, wl["w_in"], s["h"], p["norm_mix"][l], dh1, b_kind="rows",
                                                     also_bf16=True, name="in_proj_dx")
        if l == N_A:
            dkv, dgain = _seg_rms_bwd(kv, k_gain, dks, width=KV_HALF, col=0, out_buf=lax.empty((T, 2 * KV_HALF), BF16),
                                      name="k_norm_bwd")
            g["k_norm"] = _fold_heads(dgain, SWA_KV_HEADS)
            dkv = _sum_into(dvs[0], dvs[1], dkv, 1, name="dv_sum")
            g_kv = _mm(hn_kv, dkv, ta=True, out_kind="rows", out_buf=lax.empty(w_kv.shape, BF16), name="w_kv_grad")
            dh, dh_b, g["kv_norm"] = _mm_rms_bwd(dkv, w_kv, h_kv, p["kv_norm"], dh, b_kind="rows", also_bf16=True,
                                                 name="kv_proj_dx")
        if l + 1 < DEPTH:
            reduce_wait(l + 1, dh)
        token = reduce_layer(l, gb)
    grads = {n: (jnp.stack(v) if isinstance(v, list) else v) for n, v in g.items()}
    return loss, dh, grads, g_kv


def kernel(x, mem, norm_mix, w_in, pool_w, pool_scale, kv_norm, w_kv, k_norm, q_norm, sinks, mem_norm, w_mem_kv, mem_q_norm, mem_k_norm, w_out, norm_mlp, w_up, w_down, loss_target, m_norm_mix, m_w_in, m_pool_w, m_pool_scale, m_kv_norm, m_w_kv, m_k_norm, m_q_norm, m_sinks, m_mem_norm, m_w_mem_kv, m_mem_q_norm, m_mem_k_norm, m_w_out, m_norm_mlp, m_w_up, m_w_down, v_norm_mix, v_w_in, v_pool_w, v_pool_scale, v_kv_norm, v_w_kv, v_k_norm, v_q_norm, v_sinks, v_mem_norm, v_w_mem_kv, v_mem_q_norm, v_mem_k_norm, v_w_out, v_norm_mlp, v_w_up, v_w_down):
    weights = dict(norm_mix=norm_mix, w_in=w_in, pool_w=pool_w, pool_scale=pool_scale, kv_norm=kv_norm, w_kv=w_kv,
                   k_norm=k_norm, q_norm=q_norm, sinks=sinks, mem_norm=mem_norm, w_mem_kv=w_mem_kv,
                   mem_q_norm=mem_q_norm, mem_k_norm=mem_k_norm, w_out=w_out, norm_mlp=norm_mlp, w_up=w_up, w_down=w_down)
    mom1 = dict(norm_mix=m_norm_mix, w_in=m_w_in, pool_w=m_pool_w, pool_scale=m_pool_scale, kv_norm=m_kv_norm, w_kv=m_w_kv,
                k_norm=m_k_norm, q_norm=m_q_norm, sinks=m_sinks, mem_norm=m_mem_norm, w_mem_kv=m_w_mem_kv,
                mem_q_norm=m_mem_q_norm, mem_k_norm=m_mem_k_norm, w_out=m_w_out, norm_mlp=m_norm_mlp, w_up=m_w_up,
                w_down=m_w_down)
    mom2 = dict(norm_mix=v_norm_mix, w_in=v_w_in, pool_w=v_pool_w, pool_scale=v_pool_scale, kv_norm=v_kv_norm, w_kv=v_w_kv,
                k_norm=v_k_norm, q_norm=v_q_norm, sinks=v_sinks, mem_norm=v_mem_norm, w_mem_kv=v_w_mem_kv,
                mem_q_norm=v_mem_q_norm, mem_k_norm=v_mem_k_norm, w_out=v_w_out, norm_mlp=v_norm_mlp, w_up=v_w_up,
                w_down=v_w_down)
    names = list(weights)
    x_pos, y_pos, core = (lax.axis_index(n).astype(jnp.int32) for n in AXES)
    me, my_chip = 4 * x_pos + 2 * y_pos + core, 2 * x_pos + y_pos
    shard = MAIN_WIDTH // N_DEV

    def layer_shards(l, zero=0.0):
        return [(weights[n][l:l + 1] + zero).astype(BF16) for n in LAYERED]

    def usable(arrays):
        wl = dict(zip(LAYERED, arrays))
        wl["w_up"] = wl["w_up"].transpose(1, 2, 0, 3).reshape(1, D_MODEL, D_FF)
        return wl

    scale_block = jnp.pad(pool_scale, ((0, 8 - N_A), (0, 128 - shard)))
    *first, first_done = _all_gather(layer_shards(0) + [w_kv[None].astype(BF16), scale_block], name="gather_first")
    p = {n: weights[n] for n in SMALL_NAMES}
    p["pool_scale"] = first[-1][:, :N_A, :shard].transpose(1, 0, 2).reshape(N_A, MAIN_WIDTH)
    gathers, reduces, parts = {}, {}, {}

    def fetch(l, after):
        if l == 0:
            got, done = first[:len(LAYERED)], first_done
        else:
            got, done = _push_wait(_gather_copies, *gathers.pop(l), after, name=f"gather_wait_{l}")
        token = 0.0
        if l + 1 < DEPTH:
            srcs = layer_shards(l + 1, done[0, 0])
            lands = [lax.empty((N_DEV,) + a.shape, a.dtype) for a in srcs]
            *handles, block = _push_start(_gather_copies, _gather_own, N_DEV - 1, srcs, lands, name=f"gather_start_{l + 1}")
            gathers[l + 1], token = handles, block[0, 0]
        return usable(got), token

    def by_core(gb):
        gb = dict(gb)
        if "w_up" in gb:
            gb["w_up"] = gb["w_up"].reshape(D_MODEL, N_DEV, D_FF // N_DEV).transpose(1, 0, 2)
        order = [n for n in LAYERED if n in gb] + [n for n in gb if n not in LAYERED]
        return {n: gb[n].reshape((N_CHIP, 2) + _view2d(gb[n].shape[1:] if n == "w_up" else gb[n].shape[2:])) for n in order}

    def pair_sums(views, sib, tag):
        return [_pair_sum(a, b, core, name=f"chip_sum_{n}_{tag}", out_dtype=BF16) for (n, a), b in zip(views.items(), sib)]

    def chip_sums(gb, tag, whole=()):
        views = by_core(gb)
        sib, sib_whole = _sibling_exchange(list(views.values()), list(whole), name="reduce_sibling_" + tag)
        return pair_sums(views, sib, tag), sib_whole

    def start_chip_exchange(sums, tag, whole=()):
        lands = [lax.empty(a.shape, a.dtype) for a in sums] + [lax.empty((N_CHIP,) + a.shape, a.dtype) for a in whole]
        copies = functools.partial(_chip_copies, n_whole=len(whole))
        own = functools.partial(_chip_own, n_whole=len(whole))
        *handles, block = _push_start(copies, own, N_CHIP - 1, [*sums, *whole], lands, name="reduce_start_" + tag)
        return (copies, *handles), block[0, 0]

    mlp = ("w_up", "w_down")

    def reduce_layer(l, gb, early=False):
        if early and l > 0:
            return None
        if l == 0 and not early:
            reduces["rest"] = {n: a for n, a in gb.items() if n not in mlp}
            return None
        tag = "0_mlp" if early else str(l)
        sums, _ = chip_sums({n: gb[n] for n in mlp} if early else gb, tag)
        reduces[l], token = start_chip_exchange(sums, tag)
        return token

    def reduce_wait(l, after):
        copies, *handles = reduces.pop(l)
        return _push_wait(copies, *handles, after, name=f"reduce_wait_{l}")[0]

    def layer_wait(l, after):
        parts[l] = reduce_wait(l, after)

    loss, grad_x, grads, g_kv = _train_pass(x[0], mem[0], loss_target[0], p, first[len(LAYERED)], fetch, reduce_layer, layer_wait)

    last = dict(reduces.pop("rest"))
    last["w_kv"] = g_kv
    last["pool_scale"] = grads["pool_scale"].reshape(N_A, N_DEV, shard).transpose(1, 0, 2).astype(BF16)[:, None]
    small = _pack_small(grads)
    sums, (sib_small,) = chip_sums(last, "0", whole=[small])
    chip_small = _pair_sum(small, sib_small, core, name="chip_sum_small", out_dtype=F32)
    reduces["rest"], _ = start_chip_exchange(sums, "0_rest", whole=[chip_small])

    def adamw(n, n_parts):
        res = _adamw(n_parts, *(d[n].reshape(_view2d(d[n].shape)) for d in (weights, mom1, mom2)), name="adamw_" + n)
        return [r.reshape(weights[n].shape) for r in res]

    p_up, p_down = reduce_wait(0, chip_small)
    parts[0] = [None, None, None, p_up, p_down]
    new = {n: adamw(n, [parts[l][LAYERED.index(n)] for l in range(DEPTH)]) for n in mlp}
    p_in, p_mem_kv, p_out, parts_kv, parts_scale, parts_small = reduce_wait("rest", new["w_down"][0])
    parts[0][:3] = [p_in, p_mem_kv, p_out]
    new.update({n: adamw(n, [parts[l][k] for l in range(DEPTH)]) for k, n in enumerate(LAYERED) if n not in mlp})
    new["w_kv"] = adamw("w_kv", [parts_kv])
    new["pool_scale"] = adamw("pool_scale", [parts_scale])
    res = _adamw([parts_small], _pack_small(weights), _pack_small(mom1), _pack_small(mom2), name="adamw_replicated")
    for n, vals in zip(SMALL_NAMES, zip(*(_unpack_small(r, weights).values() for r in res))):
        new[n] = list(vals)
    outs = [new[n][k] for k in range(4) for n in names]
    total = lax.psum(loss[0, 0], AXES)
    return (total, grad_x[None], *outs)
```

```python
import functools
import math

import jax
import jax.numpy as jnp
from jax import lax
from jax.experimental import pallas as pl
from jax.experimental.pallas import tpu as pltpu

F32 = jnp.float32
BF16 = jnp.bfloat16
MESH = pl.DeviceIdType.MESH
AXES = ("x", "y", "c")

D_MODEL = 1024
DEPTH = 4
N_A = 2
HEAD_DIM = 64
MEM_HEADS = 4
MEM_WIDTH = MEM_HEADS * HEAD_DIM
MAIN_WIDTH = D_MODEL - MEM_WIDTH
POOL_GROUPS = 4
POOL_GROUP_DIM = MAIN_WIDTH // POOL_GROUPS
POOL_HALO = 16
SWA_Q_HEADS = MAIN_WIDTH // HEAD_DIM
SWA_KV_HEADS = 4
SWA_GROUP = SWA_Q_HEADS // SWA_KV_HEADS
KV_HALF = SWA_KV_HEADS * HEAD_DIM
BLOCK = 128
D_FF = 4 * D_MODEL
EPS = 1e-6
SCALE = HEAD_DIM ** -0.5
NEG = float(jnp.finfo(jnp.float32).min)
N_DEV = 8
N_CHIP = 4

ADAM_LR = 0.001
ADAM_B1 = 0.9
ADAM_B2 = 0.999
ADAM_EPS = 1e-08
ADAM_WD = 0.01
ADAM_STEP = 10

PACK_W = 512
VMEM_LIMIT = 52 * 1024 * 1024
MM_TILE = 1024

LAYERED = ("w_in", "w_mem_kv", "w_out", "w_up", "w_down")
SMALL_NAMES = ("norm_mix", "pool_w", "kv_norm", "k_norm", "q_norm", "sinks", "mem_norm", "mem_q_norm", "mem_k_norm",
               "norm_mlp")


ANY = pl.BlockSpec(memory_space=pl.ANY)


def _params(*sem):
    return pltpu.CompilerParams(dimension_semantics=sem, vmem_limit_bytes=VMEM_LIMIT)


def _mm(a, b, *, name, ta=False, tb=False, b_kind=None, layer=0, res=None, relu2=False, mul2=None, out_dtype=F32,
        out_kind=None, out_buf=None):
    if ta:
        K, M = a.shape
    else:
        M, K = a.shape
    if b_kind is None:
        rows_b, cols_b = b.shape
    elif b_kind == "rows":
        rows_b, cols_b = b.shape[0] * b.shape[2], b.shape[3]
    else:
        rows_b, cols_b = b.shape[1:]
    N, K2 = (rows_b, cols_b) if tb else (cols_b, rows_b)
    assert K == K2, (a.shape, b.shape)
    tm = min(M, MM_TILE if K <= MM_TILE else MM_TILE // 2)
    tn = min(N, MM_TILE)
    assert M % tm == 0 and N % tn == 0
    row_tile, col_tile = (tn, K) if tb else (K, tn)
    a_spec = pl.BlockSpec((K, tm), lambda j, i: (0, i)) if ta else pl.BlockSpec((tm, K), lambda j, i: (i, 0))

    def rc(j):
        return (j, 0) if tb else (0, j)

    if b_kind is None:
        b_spec = pl.BlockSpec((row_tile, col_tile), lambda j, i: rc(j))
    elif b_kind == "rows":
        per = row_tile // b.shape[2]
        b_spec = pl.BlockSpec((per, None, b.shape[2], col_tile), lambda j, i: (rc(j)[0], layer, 0, rc(j)[1]))
    else:
        b_spec = pl.BlockSpec((None, row_tile, col_tile), lambda j, i: (layer, *rc(j)))
    o_spec = pl.BlockSpec((tm, tn), lambda j, i: (i, j))
    dn = (((0 if ta else 1,), (1 if tb else 0,)), ((), ()))
    extra = [e for e in (res, mul2) if e is not None]
    n_in = 2 + len(extra) + (1 if out_buf is not None else 0)

    def body(*refs):
        a_ref, b_ref = refs[0], refs[1]
        extra_refs = refs[2:2 + len(extra)]
        out = refs[n_in]
        bv = b_ref[...].astype(BF16).reshape(row_tile, col_tile)
        v = lax.dot_general(a_ref[...].astype(BF16), bv, dn, preferred_element_type=F32)
        if res is not None:
            v = extra_refs[0][...] + v
        elif mul2 is not None:
            v = v * (2.0 * jnp.sqrt(extra_refs[0][...].astype(F32)))
        if relu2:
            r = jnp.maximum(v, 0.0)
            v = r * r
        out[...] = v.astype(out.dtype).reshape(out.shape)

    in_specs = [a_spec, b_spec] + [o_spec] * len(extra)
    operands = [a, b, *extra]
    aliases = {}
    if out_kind is None:
        out_shape = jax.ShapeDtypeStruct((M, N), BF16 if relu2 else out_dtype)
        out_specs = o_spec
    else:
        if out_kind == "rows":
            s = out_buf.shape[2]
            out_specs = pl.BlockSpec((tm // s, None, s, tn), lambda j, i: (i, layer, 0, j))
        else:
            out_specs = pl.BlockSpec((None, tm, tn), lambda j, i: (layer, i, j))
        out_shape = jax.ShapeDtypeStruct(out_buf.shape, out_buf.dtype)
        in_specs.append(ANY)
        operands.append(out_buf)
        aliases = {len(operands) - 1: 0}
    return pl.pallas_call(
        body, name=name, grid=(N // tn, M // tm), in_specs=in_specs, out_specs=out_specs, out_shape=out_shape,
        input_output_aliases=aliases, compiler_params=_params("parallel", "parallel"),
    )(*operands)


def _weight_block(b, b_kind, transposed, tn):
    if b_kind == "rows":
        s = b.shape[2]
        rows, cols = b.shape[0] * s, b.shape[3]
        if transposed:
            return (lambda at: pl.BlockSpec((b.shape[0], None, s, cols), lambda *g: (0, 0, 0, 0))), rows, cols
        return (lambda at: pl.BlockSpec((b.shape[0], None, s, tn), lambda *g: (0, 0, 0, at(*g)))), rows, cols
    rows, cols = b.shape[1:]
    if transposed:
        return (lambda at: pl.BlockSpec((None, rows, cols), lambda *g: (0, 0, 0))), rows, cols
    return (lambda at: pl.BlockSpec((None, rows, tn), lambda *g: (0, 0, at(*g)))), rows, cols


def _norm_mm(x, gain, b, *, b_kind, name, relu2=False):
    M, K = x.shape
    tm = min(M, MM_TILE)
    spec_of, rows, N = _weight_block(b, b_kind, False, min(MM_TILE, b.shape[-1]))
    tn = min(N, MM_TILE)
    assert rows == K and M % tm == 0 and N % tn == 0

    def body(x_ref, g_ref, b_ref, xn_ref, o_ref):
        @pl.when(pl.program_id(1) == 0)
        def _():
            xv = x_ref[...]
            r = lax.rsqrt(jnp.mean(xv * xv, axis=-1, keepdims=True) + EPS)
            xn_ref[...] = ((xv * r) * g_ref[...]).astype(xn_ref.dtype)

        v = jnp.dot(xn_ref[...], b_ref[...].astype(BF16).reshape(K, tn), preferred_element_type=F32)
        if relu2:
            r2 = jnp.maximum(v, 0.0)
            v = r2 * r2
        o_ref[...] = v.astype(o_ref.dtype)

    rows_spec = pl.BlockSpec((tm, K), lambda i, j: (i, 0))
    return pl.pallas_call(
        body, name=name, grid=(M // tm, N // tn),
        in_specs=[rows_spec, pl.BlockSpec((1, K), lambda i, j: (0, 0)), spec_of(lambda i, j: j)],
        out_specs=(rows_spec, pl.BlockSpec((tm, tn), lambda i, j: (i, j))),
        out_shape=(jax.ShapeDtypeStruct((M, K), BF16), jax.ShapeDtypeStruct((M, N), BF16 if relu2 else F32)),
        compiler_params=_params("parallel", "arbitrary"),
    )(x, gain.reshape(1, K), b)


def _mm_rms_bwd(a, b, x, gain, res, *, b_kind, name, also_bf16):
    M, K = a.shape
    spec_of, N, cols = _weight_block(b, b_kind, True, None)
    assert cols == K and x.shape == (M, N)
    tm = min(M, MM_TILE if K <= MM_TILE else MM_TILE // 2)
    assert M % tm == 0

    def body(a_ref, b_ref, x_ref, g_ref, res_ref, *outs):
        i = pl.program_id(0)
        dy = lax.dot_general(a_ref[...].astype(BF16), b_ref[...].astype(BF16).reshape(N, K), (((1,), (1,)), ((), ())),
                             preferred_element_type=F32)
        xv = x_ref[...]
        r = lax.rsqrt(jnp.mean(xv * xv, axis=-1, keepdims=True) + EPS)
        xh = xv * r
        part = jnp.sum(dy * xh, axis=0, keepdims=True)
        dg_ref = outs[-1]

        @pl.when(i == 0)
        def _():
            dg_ref[...] = part

        @pl.when(i > 0)
        def _():
            dg_ref[...] += part

        gdy = dy * g_ref[...]
        dx = res_ref[...] + r * (gdy - xh * jnp.mean(gdy * xh, axis=-1, keepdims=True))
        outs[0][...] = dx
        if also_bf16:
            outs[1][...] = dx.astype(BF16)

    row = pl.BlockSpec((tm, N), lambda i: (i, 0))
    vec = pl.BlockSpec((1, N), lambda i: (0, 0))
    out_specs = [row] + ([row] if also_bf16 else []) + [vec]
    out_shape = [jax.ShapeDtypeStruct((M, N), F32)] + ([jax.ShapeDtypeStruct((M, N), BF16)] if also_bf16 else [])
    outs = pl.pallas_call(
        body, name=name, grid=(M // tm,),
        in_specs=[pl.BlockSpec((tm, K), lambda i: (i, 0)), spec_of(None), row, vec, row], out_specs=out_specs,
        out_shape=out_shape + [jax.ShapeDtypeStruct((1, N), F32)], compiler_params=_params("arbitrary"),
    )(a, b, x, gain.reshape(1, N), res)
    return (*outs[:-1], outs[-1].reshape(N))


def _row_tile(rows, d):
    t = min(rows, (512 * 1024) // d)
    while rows % t or (t != rows and t % 16):
        t -= 1
    return t


def _rms_fwd(x, g, *, name, out_dtype=BF16):
    R, D = x.shape
    tr = _row_tile(R, D)

    def body(x_ref, g_ref, o_ref):
        xv = x_ref[...].astype(F32)
        r = lax.rsqrt(jnp.mean(xv * xv, axis=-1, keepdims=True) + EPS)
        o_ref[...] = ((xv * r) * g_ref[...]).astype(o_ref.dtype)

    return pl.pallas_call(
        body, name=name, grid=(R // tr,),
        in_specs=[pl.BlockSpec((tr, D), lambda i: (i, 0)), pl.BlockSpec((1, D), lambda i: (0, 0))],
        out_specs=pl.BlockSpec((tr, D), lambda i: (i, 0)), out_shape=jax.ShapeDtypeStruct((R, D), out_dtype),
        compiler_params=_params("parallel"),
    )(x, g.reshape(1, D))


def _rms_bwd(x, g, dys, *, name, res=None, want_dx=True, also_bf16=False):
    R, D = x.shape
    tr = _row_tile(R, D)
    n_dy = len(dys)
    has_res = res is not None

    def body(*refs):
        x_ref, g_ref = refs[0], refs[1]
        dy_refs = refs[2:2 + n_dy]
        res_ref = refs[2 + n_dy] if has_res else None
        outs = refs[2 + n_dy + (1 if has_res else 0):]
        dg_ref = outs[-1]
        i = pl.program_id(0)
        xv = x_ref[...].astype(F32)
        dy = dy_refs[0][...].astype(F32)
        for extra in dy_refs[1:]:
            dy = dy + extra[...].astype(F32)
        r = lax.rsqrt(jnp.mean(xv * xv, axis=-1, keepdims=True) + EPS)
        xh = xv * r
        part = jnp.sum(dy * xh, axis=0, keepdims=True)

        @pl.when(i == 0)
        def _():
            dg_ref[...] = part

        @pl.when(i > 0)
        def _():
            dg_ref[...] += part

        if want_dx:
            gdy = dy * g_ref[...]
            dx = r * (gdy - xh * jnp.mean(gdy * xh, axis=-1, keepdims=True))
            if has_res:
                dx = res_ref[...] + dx
            outs[0][...] = dx
            if also_bf16:
                outs[1][...] = dx.astype(BF16)

    row = pl.BlockSpec((tr, D), lambda i: (i, 0))
    vec = pl.BlockSpec((1, D), lambda i: (0, 0))
    out_shape = [jax.ShapeDtypeStruct((1, D), F32)]
    out_specs = [vec]
    if also_bf16:
        out_shape = [jax.ShapeDtypeStruct((R, D), BF16)] + out_shape
        out_specs = [row] + out_specs
    if want_dx:
        out_shape = [jax.ShapeDtypeStruct((R, D), F32)] + out_shape
        out_specs = [row] + out_specs
    outs = pl.pallas_call(
        body, name=name, grid=(R // tr,),
        in_specs=[row, vec] + [row] * (n_dy + (1 if has_res else 0)), out_specs=out_specs, out_shape=out_shape,
        compiler_params=_params("arbitrary"),
    )(x, g.reshape(1, D), *dys, *([res] if has_res else []))
    return (*outs[:-1], outs[-1].reshape(D)) if want_dx else outs[0].reshape(D)


def _add(a, b, *, name):
    R, D = a.shape
    tr = _row_tile(R, D)

    def body(a_ref, b_ref, o_ref):
        o_ref[...] = a_ref[...] + b_ref[...]

    row = pl.BlockSpec((tr, D), lambda i: (i, 0))
    return pl.pallas_call(body, name=name, grid=(R // tr,), in_specs=[row, row], out_specs=row,
                          out_shape=jax.ShapeDtypeStruct((R, D), a.dtype), compiler_params=_params("parallel"))(a, b)


POOL_TILE = 512


def _pool_window(group):
    return lax.shift_left(jnp.int32(2), group)


def _pool_diff(u_ref, halo_ref, group, tile):
    first = tile == 0
    halo = jnp.where(first, 0.0, halo_ref[...])
    ext = jnp.concatenate([halo, u_ref[...]], axis=0)
    n = ext.shape[0]
    s1 = ext + pltpu.roll(ext, 1, 0)
    s2 = s1 + pltpu.roll(s1, 2, 0)
    s3 = s2 + pltpu.roll(s2, 4, 0)
    s4 = s3 + pltpu.roll(s3, 8, 0)
    ws = jnp.where(group == 0, s1, jnp.where(group == 1, s2, jnp.where(group == 2, s3, s4)))[POOL_HALO:n]
    t = tile * POOL_TILE + lax.broadcasted_iota(jnp.int32, (POOL_TILE, 1), 0)
    cnt = jnp.minimum(t + 1, _pool_window(group)).astype(F32)
    return ws / cnt - u_ref[...], cnt


def _pool_specs():
    per_tile = POOL_TILE // POOL_HALO
    cur = pl.BlockSpec((None, POOL_TILE, POOL_GROUP_DIM), lambda g, i: (g, i, 0))
    prev = pl.BlockSpec((None, POOL_HALO, POOL_GROUP_DIM), lambda g, i: (g, jnp.maximum(i * per_tile - 1, 0), 0))
    pw = pl.BlockSpec((None, POOL_GROUP_DIM, POOL_GROUP_DIM), lambda g, i: (g, 0, 0))
    vec = pl.BlockSpec((None, 1, POOL_GROUP_DIM), lambda g, i: (g, 0, 0))
    return cur, prev, pw, vec


def _pool_fwd(u, pw, scale, *, name):
    G, T, C = u.shape
    assert T % POOL_TILE == 0
    cur, prev, pw_spec, vec = _pool_specs()

    def body(u_ref, halo_ref, pw_ref, sc_ref, o_ref):
        d, _ = _pool_diff(u_ref, halo_ref, pl.program_id(0), pl.program_id(1))
        mixed = jnp.dot(d.astype(BF16), pw_ref[...].astype(BF16), preferred_element_type=F32)
        o_ref[...] = (mixed * sc_ref[...]).astype(o_ref.dtype)

    return pl.pallas_call(
        body, name=name, grid=(G, T // POOL_TILE), in_specs=[cur, prev, pw_spec, vec], out_specs=cur,
        out_shape=jax.ShapeDtypeStruct((G, T, C), BF16), compiler_params=_params("parallel", "parallel"),
    )(u, u, pw, scale)


def _pool_bwd(u, pw, scale, dout, *, name):
    G, T, C = u.shape
    nt = T // POOL_TILE
    per_tile = POOL_TILE // POOL_HALO
    cur, prev, pw_spec, vec = _pool_specs()
    nxt = pl.BlockSpec((None, POOL_HALO, C), lambda g, i: (g, jnp.minimum((i + 1) * per_tile, nt * per_tile - 1), 0))

    def body(u_ref, halo_ref, pw_ref, sc_ref, do_ref, donext_ref, du_ref, dpw_ref, dsc_ref):
        group, tile = pl.program_id(0), pl.program_id(1)
        d, cnt = _pool_diff(u_ref, halo_ref, group, tile)
        pwb = pw_ref[...].astype(BF16)
        db = d.astype(BF16)
        mixed = jnp.dot(db, pwb, preferred_element_type=F32)
        dout = do_ref[...].astype(F32)
        dsc = jnp.sum(dout * mixed, axis=0, keepdims=True)
        sc = sc_ref[...]
        dmix = (dout * sc).astype(BF16)
        dpw = lax.dot_general(db, dmix, (((0,), (0,)), ((), ())), preferred_element_type=F32)

        @pl.when(tile == 0)
        def _():
            dpw_ref[...] = dpw
            dsc_ref[...] = dsc

        @pl.when(tile > 0)
        def _():
            dpw_ref[...] += dpw
            dsc_ref[...] += dsc

        last = tile == nt - 1
        dnext = jnp.where(last, 0.0, donext_ref[...].astype(F32))
        dmix_ext = jnp.concatenate([dmix, (dnext * sc).astype(BF16)], axis=0)
        dd_ext = lax.dot_general(dmix_ext, pwb, (((1,), (1,)), ((), ())), preferred_element_type=F32)
        window = _pool_window(group).astype(F32)
        cnt_ext = jnp.concatenate([cnt, jnp.broadcast_to(window, (POOL_HALO, 1))], axis=0)
        q = dd_ext / cnt_ext
        n = q.shape[0]
        r1 = q + pltpu.roll(q, n - 1, 0)
        r2 = r1 + pltpu.roll(r1, n - 2, 0)
        r3 = r2 + pltpu.roll(r2, n - 4, 0)
        r4 = r3 + pltpu.roll(r3, n - 8, 0)
        back = jnp.where(group == 0, r1, jnp.where(group == 1, r2, jnp.where(group == 2, r3, r4)))
        du_ref[...] = back[0:POOL_TILE] - dd_ext[0:POOL_TILE]

    return pl.pallas_call(
        body, name=name, grid=(G, nt), in_specs=[cur, prev, pw_spec, vec, cur, nxt],
        out_specs=(cur, pw_spec, vec),
        out_shape=(jax.ShapeDtypeStruct((G, T, C), F32), jax.ShapeDtypeStruct((G, C, C), F32),
                   jax.ShapeDtypeStruct((G, 1, C), F32)),
        compiler_params=_params("arbitrary", "arbitrary"),
    )(u, u, pw, scale, dout, dout)


def _softmax(q, k, bias, valid, sink):
    s = lax.dot_general(q, k, (((1,), (1,)), ((), ())), preferred_element_type=F32) * SCALE
    if bias is not None:
        s = s - bias
    if valid is not None:
        s = jnp.where(valid, s, NEG)
    m = jnp.max(s, axis=-1, keepdims=True)
    if sink is not None:
        m = jnp.maximum(m, sink)
    e = jnp.exp(s - m)
    z = jnp.sum(e, axis=-1, keepdims=True)
    if sink is None:
        return e * (1.0 / z), None
    es = jnp.exp(sink - m)
    inv = 1.0 / (z + es)
    return e * inv, es * inv


def _swa_terms(sink_ref, kvh, blk):
    rows = SWA_GROUP * BLOCK
    row = lax.broadcasted_iota(jnp.int32, (rows, 1), 0)
    grp = row // BLOCK
    head = (kvh * SWA_GROUP + grp + 1).astype(F32)
    slope = jnp.exp(head * (-8.0 * math.log(2.0) / SWA_Q_HEADS))
    qi = lax.broadcasted_iota(jnp.int32, (rows, 2 * BLOCK), 0) % BLOCK
    kj = lax.broadcasted_iota(jnp.int32, (rows, 2 * BLOCK), 1)
    dist = qi + BLOCK - kj
    valid = (dist >= 0) & (dist < BLOCK) & ((blk > 0) | (kj >= BLOCK))
    bias = slope * dist.astype(F32)
    s0, s1, s2 = (sink_ref[kvh * SWA_GROUP + g] for g in range(SWA_GROUP))
    sink = jnp.where(grp == 0, s0, jnp.where(grp == 1, s1, s2))
    return bias, valid, sink, grp


def _swa_fwd(q, k, v, sinks, *, name):
    H, G, T, hd = q.shape
    nb = T // BLOCK
    rows = G * BLOCK

    def body(sink_ref, q_ref, kp_ref, kc_ref, vp_ref, vc_ref, o_ref):
        kvh, blk = pl.program_id(0), pl.program_id(1)
        bias, valid, sink, _ = _swa_terms(sink_ref, kvh, blk)
        kk = jnp.concatenate([kp_ref[...], kc_ref[...]], axis=0)
        vv = jnp.concatenate([vp_ref[...], vc_ref[...]], axis=0)
        p, _ = _softmax(q_ref[...].reshape(rows, hd), kk, bias, valid, sink)
        o = jnp.dot(p.astype(BF16), vv, preferred_element_type=F32)
        o_ref[...] = o.reshape(G, BLOCK, hd).astype(o_ref.dtype)

    qs = pl.BlockSpec((None, G, BLOCK, hd), lambda h, n: (h, 0, n, 0))
    prev = pl.BlockSpec((None, BLOCK, hd), lambda h, n: (h, jnp.maximum(n - 1, 0), 0))
    cur = pl.BlockSpec((None, BLOCK, hd), lambda h, n: (h, n, 0))
    return pl.pallas_call(
        body, name=name, grid=(H, nb),
        in_specs=[pl.BlockSpec(memory_space=pltpu.SMEM), qs, prev, cur, prev, cur], out_specs=qs,
        out_shape=jax.ShapeDtypeStruct((H, G, T, hd), BF16), compiler_params=_params("parallel", "parallel"),
    )(sinks, q, k, k, v, v)


def _swa_bwd(q, k, v, sinks, do, *, name):
    H, G, T, hd = q.shape
    nb = T // BLOCK
    rows = G * BLOCK

    def body(sink_ref, q_ref, do_ref, kp_ref, kc_ref, vp_ref, vc_ref, dq_ref, dk_ref, dv_ref, ds_ref, ck, cv):
        kvh, blk = pl.program_id(0), pl.program_id(1)

        @pl.when(blk == 0)
        def _():
            ck[...] = jnp.zeros_like(ck)
            cv[...] = jnp.zeros_like(cv)
            ds_ref[...] = jnp.zeros_like(ds_ref)

        @pl.when(blk < nb)
        def _():
            bias, valid, sink, grp = _swa_terms(sink_ref, kvh, blk)
            kk = jnp.concatenate([kp_ref[...], kc_ref[...]], axis=0)
            vv = jnp.concatenate([vp_ref[...], vc_ref[...]], axis=0)
            qq = q_ref[...].reshape(rows, hd)
            dout = do_ref[...].reshape(rows, hd)
            p, ps = _softmax(qq, kk, bias, valid, sink)
            dp = lax.dot_general(dout, vv, (((1,), (1,)), ((), ())), preferred_element_type=F32)
            dsum = jnp.sum(p * dp, axis=-1, keepdims=True)
            ds = (p * (dp - dsum)).astype(BF16)
            dq = jnp.dot(ds, kk, preferred_element_type=F32) * SCALE
            dq_ref[...] = dq.reshape(G, BLOCK, hd)
            dk = lax.dot_general(ds, qq, (((0,), (0,)), ((), ())), preferred_element_type=F32) * SCALE
            dv = lax.dot_general(p.astype(BF16), dout, (((0,), (0,)), ((), ())), preferred_element_type=F32)
            dk_ref[...] = ck[...] + dk[0:BLOCK]
            dv_ref[...] = cv[...] + dv[0:BLOCK]
            ck[...] = dk[BLOCK:2 * BLOCK]
            cv[...] = dv[BLOCK:2 * BLOCK]
            dsink = -(ps * dsum)
            lane = lax.broadcasted_iota(jnp.int32, (1, 128), 1)
            acc = jnp.zeros((1, 128), F32)
            for g in range(G):
                acc = acc + jnp.where(lane == g, jnp.sum(jnp.where(grp == g, dsink, 0.0)), 0.0)
            ds_ref[...] += acc

        @pl.when(blk == nb)
        def _():
            dk_ref[...] = ck[...]
            dv_ref[...] = cv[...]

    def at(n):
        return jnp.minimum(n, nb - 1)

    qs = pl.BlockSpec((None, G, BLOCK, hd), lambda h, n: (h, 0, at(n), 0))
    prev = pl.BlockSpec((None, BLOCK, hd), lambda h, n: (h, jnp.maximum(at(n) - 1, 0), 0))
    cur = pl.BlockSpec((None, BLOCK, hd), lambda h, n: (h, at(n), 0))
    late = pl.BlockSpec((None, BLOCK, hd), lambda h, n: (h, jnp.maximum(n - 1, 0), 0))
    dsink_spec = pl.BlockSpec((None, 1, 128), lambda h, n: (h, 0, 0))
    return pl.pallas_call(
        body, name=name, grid=(H, nb + 1),
        in_specs=[pl.BlockSpec(memory_space=pltpu.SMEM), qs, qs, prev, cur, prev, cur],
        out_specs=(qs, late, late, dsink_spec),
        out_shape=(jax.ShapeDtypeStruct((H, G, T, hd), F32), jax.ShapeDtypeStruct((H, T, hd), F32),
                   jax.ShapeDtypeStruct((H, T, hd), F32), jax.ShapeDtypeStruct((H, 1, 128), F32)),
        scratch_shapes=[pltpu.VMEM((BLOCK, hd), F32), pltpu.VMEM((BLOCK, hd), F32)],
        compiler_params=_params("arbitrary", "arbitrary"),
    )(sinks, q, do, k, k, v, v)


MEM_Q_TILE = 512


def _mem_fwd(q, k, v, *, name):
    H, T, hd = q.shape
    M = k.shape[1]
    tq = min(T, MEM_Q_TILE)

    def body(q_ref, k_ref, v_ref, o_ref):
        p, _ = _softmax(q_ref[...], k_ref[...], None, None, None)
        o_ref[...] = jnp.dot(p.astype(BF16), v_ref[...], preferred_element_type=F32).astype(o_ref.dtype)

    qs = pl.BlockSpec((None, tq, hd), lambda h, i: (h, i, 0))
    ks = pl.BlockSpec((None, M, hd), lambda h, i: (h, 0, 0))
    return pl.pallas_call(body, name=name, grid=(H, T // tq), in_specs=[qs, ks, ks], out_specs=qs,
                          out_shape=jax.ShapeDtypeStruct((H, T, hd), BF16),
                          compiler_params=_params("parallel", "parallel"))(q, k, v)


def _mem_bwd(q, k, v, do, *, name):
    H, T, hd = q.shape
    M = k.shape[1]
    tq = min(T, MEM_Q_TILE)

    def body(q_ref, do_ref, k_ref, v_ref, dq_ref, dk_ref, dv_ref):
        i = pl.program_id(1)
        qq, kk, vv, dout = q_ref[...], k_ref[...], v_ref[...], do_ref[...]
        p, _ = _softmax(qq, kk, None, None, None)
        dp = lax.dot_general(dout, vv, (((1,), (1,)), ((), ())), preferred_element_type=F32)
        dsum = jnp.sum(p * dp, axis=-1, keepdims=True)
        ds = (p * (dp - dsum)).astype(BF16)
        dq_ref[...] = jnp.dot(ds, kk, preferred_element_type=F32) * SCALE
        dk = lax.dot_general(ds, qq, (((0,), (0,)), ((), ())), preferred_element_type=F32) * SCALE
        dv = lax.dot_general(p.astype(BF16), dout, (((0,), (0,)), ((), ())), preferred_element_type=F32)

        @pl.when(i == 0)
        def _():
            dk_ref[...] = dk
            dv_ref[...] = dv

        @pl.when(i > 0)
        def _():
            dk_ref[...] += dk
            dv_ref[...] += dv

    qs = pl.BlockSpec((None, tq, hd), lambda h, i: (h, i, 0))
    ks = pl.BlockSpec((None, M, hd), lambda h, i: (h, 0, 0))
    return pl.pallas_call(
        body, name=name, grid=(H, T // tq), in_specs=[qs, qs, ks, ks], out_specs=(qs, ks, ks),
        out_shape=(jax.ShapeDtypeStruct((H, T, hd), F32), jax.ShapeDtypeStruct((H, M, hd), F32),
                   jax.ShapeDtypeStruct((H, M, hd), F32)),
        compiler_params=_params("arbitrary", "arbitrary"),
    )(q, do, k, v)


LANES = 128


def _seg_mean(v):
    r = lax.broadcasted_iota(jnp.int32, (LANES, LANES), 0) // HEAD_DIM
    c = lax.broadcasted_iota(jnp.int32, (LANES, LANES), 1) // HEAD_DIM
    seg = jnp.where(r == c, 1.0 / HEAD_DIM, 0.0).astype(BF16)
    hi = v.astype(BF16)
    lo = (v - hi.astype(F32)).astype(BF16)
    parts = []
    for g in range(v.shape[1] // LANES):
        sl = slice(g * LANES, (g + 1) * LANES)
        parts.append(jnp.dot(hi[:, sl], seg, preferred_element_type=F32) + jnp.dot(lo[:, sl], seg, preferred_element_type=F32))
    return parts[0] if len(parts) == 1 else jnp.concatenate(parts, axis=1)


def _cols(rows, width, col):
    return pl.BlockSpec((rows, width), lambda i: (i, col))


def _head_gain(g, heads):
    return jnp.tile(g, heads).reshape(1, heads * HEAD_DIM)


def _fold_heads(dg, heads):
    return dg.reshape(heads, HEAD_DIM).sum(axis=0)


def _seg_rms_fwd(x, gain, *, width, col, name):
    R = x.shape[0]
    tr = _row_tile(R, width)

    def body(x_ref, g_ref, o_ref):
        xv = x_ref[...]
        r = lax.rsqrt(_seg_mean(xv * xv) + EPS)
        o_ref[...] = ((xv * r) * g_ref[...]).astype(o_ref.dtype)

    return pl.pallas_call(
        body, name=name, grid=(R // tr,), in_specs=[_cols(tr, width, col), pl.BlockSpec((1, width), lambda i: (0, 0))],
        out_specs=_cols(tr, width, 0), out_shape=jax.ShapeDtypeStruct((R, width), BF16), compiler_params=_params("parallel"),
    )(x, gain)


def _seg_rms_bwd(x, gain, dys, *, width, col, name, out_buf=None, out_col=0):
    R = x.shape[0]
    tr = _row_tile(R, width)
    n_dy = len(dys)

    def body(*refs):
        x_ref, g_ref = refs[0], refs[1]
        dy_refs = refs[2:2 + n_dy]
        dx_ref, dg_ref = refs[-2], refs[-1]
        i = pl.program_id(0)
        xv = x_ref[...]
        dy = dy_refs[0][...]
        for extra in dy_refs[1:]:
            dy = dy + extra[...]
        r = lax.rsqrt(_seg_mean(xv * xv) + EPS)
        xh = xv * r
        part = jnp.sum(dy * xh, axis=0, keepdims=True)

        @pl.when(i == 0)
        def _():
            dg_ref[...] = part

        @pl.when(i > 0)
        def _():
            dg_ref[...] += part

        gdy = dy * g_ref[...]
        dx_ref[...] = (r * (gdy - xh * _seg_mean(gdy * xh))).astype(dx_ref.dtype)

    vec = pl.BlockSpec((1, width), lambda i: (0, 0))
    in_specs = [_cols(tr, width, col), vec] + [_cols(tr, width, 0)] * n_dy
    operands = [x, gain, *dys]
    aliases = {}
    dx_shape = jax.ShapeDtypeStruct((R, width), BF16)
    if out_buf is not None:
        in_specs.append(ANY)
        operands.append(out_buf)
        aliases = {len(operands) - 1: 0}
        dx_shape = jax.ShapeDtypeStruct(out_buf.shape, out_buf.dtype)
    return pl.pallas_call(
        body, name=name, grid=(R // tr,), in_specs=in_specs, out_specs=(_cols(tr, width, out_col), vec),
        out_shape=(dx_shape, jax.ShapeDtypeStruct((1, width), F32)), input_output_aliases=aliases,
        compiler_params=_params("arbitrary"),
    )(*operands)


def _sum_into(a, b, out_buf, out_col, *, name):
    R, width = a.shape
    tr = _row_tile(R, width)

    def body(a_ref, b_ref, _, o_ref):
        o_ref[...] = (a_ref[...] + b_ref[...]).astype(o_ref.dtype)

    return pl.pallas_call(
        body, name=name, grid=(R // tr,), in_specs=[_cols(tr, width, 0), _cols(tr, width, 0), ANY],
        out_specs=_cols(tr, width, out_col), out_shape=jax.ShapeDtypeStruct(out_buf.shape, out_buf.dtype),
        input_output_aliases={2: 0}, compiler_params=_params("parallel"),
    )(a, b, out_buf)


def _pool_lane_group():
    return lax.broadcasted_iota(jnp.int32, (1, MAIN_WIDTH), 1) // POOL_GROUP_DIM


def _pool_pick(group, per_window):
    s1, s2, s3, s4 = per_window
    return jnp.where(group == 0, s1, jnp.where(group == 1, s2, jnp.where(group == 2, s3, s4)))


def _pool_delta(u_ref, halo_ref, tile):
    group = _pool_lane_group()
    halo = jnp.where(tile == 0, 0.0, halo_ref[...])
    ext = jnp.concatenate([halo, u_ref[...]], axis=0)
    n = ext.shape[0]
    s1 = ext + pltpu.roll(ext, 1, 0)
    s2 = s1 + pltpu.roll(s1, 2, 0)
    s3 = s2 + pltpu.roll(s2, 4, 0)
    s4 = s3 + pltpu.roll(s3, 8, 0)
    ws = _pool_pick(group, (s1, s2, s3, s4))[POOL_HALO:n]
    t = tile * POOL_TILE + lax.broadcasted_iota(jnp.int32, (POOL_TILE, 1), 0)
    cnt = jnp.minimum(t + 1, _pool_pick(group, (2, 4, 8, 16))).astype(F32)
    return ws / cnt - u_ref[...], cnt


def _pool_in_specs():
    per_tile = POOL_TILE // POOL_HALO
    cur = _cols(POOL_TILE, MAIN_WIDTH, 0)
    prev = pl.BlockSpec((POOL_HALO, MAIN_WIDTH), lambda i: (jnp.maximum(i * per_tile - 1, 0), 0))
    mix = pl.BlockSpec((MAIN_WIDTH, MAIN_WIDTH), lambda i: (0, 0))
    vec = pl.BlockSpec((1, MAIN_WIDTH), lambda i: (0, 0))
    return cur, prev, mix, vec


def _pool_mix_fwd(proj, mix, scale, cat, *, name):
    T = proj.shape[0]
    assert T % POOL_TILE == 0
    cur, prev, mix_spec, vec = _pool_in_specs()

    def body(u_ref, halo_ref, mix_ref, sc_ref, _, o_ref):
        d, _cnt = _pool_delta(u_ref, halo_ref, pl.program_id(0))
        mixed = jnp.dot(d.astype(BF16), mix_ref[...].astype(BF16), preferred_element_type=F32)
        o_ref[...] = (mixed * sc_ref[...]).astype(o_ref.dtype)

    return pl.pallas_call(
        body, name=name, grid=(T // POOL_TILE,), in_specs=[cur, prev, mix_spec, vec, ANY], out_specs=cur,
        out_shape=jax.ShapeDtypeStruct(cat.shape, cat.dtype), input_output_aliases={4: 0}, compiler_params=_params("parallel"),
    )(proj, proj, mix, scale, cat)


def _pool_mix_bwd(proj, mix, scale, dcat, *, name):
    T = proj.shape[0]
    nt = T // POOL_TILE
    per_tile = POOL_TILE // POOL_HALO
    cur, prev, mix_spec, vec = _pool_in_specs()
    nxt = pl.BlockSpec((POOL_HALO, MAIN_WIDTH), lambda i: (jnp.minimum((i + 1) * per_tile, nt * per_tile - 1), 0))

    def body(u_ref, halo_ref, mix_ref, sc_ref, do_ref, donext_ref, du_ref, dmix_ref, dsc_ref):
        tile = pl.program_id(0)
        group = _pool_lane_group()
        d, cnt = _pool_delta(u_ref, halo_ref, tile)
        mixb = mix_ref[...].astype(BF16)
        db = d.astype(BF16)
        mixed = jnp.dot(db, mixb, preferred_element_type=F32)
        dout = do_ref[...]
        dsc = jnp.sum(dout * mixed, axis=0, keepdims=True)
        sc = sc_ref[...]
        dmixed = (dout * sc).astype(BF16)
        dmix = lax.dot_general(db, dmixed, (((0,), (0,)), ((), ())), preferred_element_type=F32)

        @pl.when(tile == 0)
        def _():
            dmix_ref[...] = dmix
            dsc_ref[...] = dsc

        @pl.when(tile > 0)
        def _():
            dmix_ref[...] += dmix
            dsc_ref[...] += dsc

        dnext = jnp.where(tile == nt - 1, 0.0, donext_ref[...])
        dmixed_ext = jnp.concatenate([dmixed, (dnext * sc).astype(BF16)], axis=0)
        dd_ext = lax.dot_general(dmixed_ext, mixb, (((1,), (1,)), ((), ())), preferred_element_type=F32)
        window = _pool_pick(group, (2.0, 4.0, 8.0, 16.0))
        cnt_ext = jnp.concatenate([cnt, jnp.broadcast_to(window, (POOL_HALO, MAIN_WIDTH))], axis=0)
        q = dd_ext / cnt_ext
        n = q.shape[0]
        r1 = q + pltpu.roll(q, n - 1, 0)
        r2 = r1 + pltpu.roll(r1, n - 2, 0)
        r3 = r2 + pltpu.roll(r2, n - 4, 0)
        r4 = r3 + pltpu.roll(r3, n - 8, 0)
        back = _pool_pick(group, (r1, r2, r3, r4))
        du_ref[...] = (back[0:POOL_TILE] - dd_ext[0:POOL_TILE]).astype(du_ref.dtype)

    return pl.pallas_call(
        body, name=name, grid=(nt,), in_specs=[cur, prev, mix_spec, vec, cur, nxt], out_specs=(cur, mix_spec, vec),
        out_shape=(jax.ShapeDtypeStruct((T, D_MODEL), BF16), jax.ShapeDtypeStruct((MAIN_WIDTH, MAIN_WIDTH), F32),
                   jax.ShapeDtypeStruct((1, MAIN_WIDTH), F32)),
        compiler_params=_params("arbitrary"),
    )(proj, proj, mix, scale, dcat, dcat)


def _head(a, h):
    return a[:, h * HEAD_DIM:(h + 1) * HEAD_DIM]


def _swa_mask(blk):
    rows = SWA_GROUP * BLOCK
    qi = lax.broadcasted_iota(jnp.int32, (rows, 2 * BLOCK), 0) % BLOCK
    kj = lax.broadcasted_iota(jnp.int32, (rows, 2 * BLOCK), 1)
    dist = qi + BLOCK - kj
    valid = (dist >= 0) & (dist < BLOCK) & ((blk > 0) | (kj >= BLOCK))
    return dist.astype(F32), valid


def _swa_head_terms(sink_ref, kvh, dist):
    grp = lax.broadcasted_iota(jnp.int32, (SWA_GROUP * BLOCK, 1), 0) // BLOCK
    slopes = [2.0 ** (-8.0 * (kvh * SWA_GROUP + g + 1) / SWA_Q_HEADS) for g in range(SWA_GROUP)]
    sinks = [sink_ref[kvh * SWA_GROUP + g] for g in range(SWA_GROUP)]
    slope = jnp.where(grp == 0, slopes[0], jnp.where(grp == 1, slopes[1], slopes[2]))
    sink = jnp.where(grp == 0, sinks[0], jnp.where(grp == 1, sinks[1], sinks[2]))
    return slope * dist, sink


def _stack_heads(a, kvh):
    return jnp.concatenate([_head(a, kvh * SWA_GROUP + g) for g in range(SWA_GROUP)], axis=0)


def _swa_specs(nb):
    def at(n):
        return jnp.minimum(n, nb - 1)

    q = pl.BlockSpec((BLOCK, MAIN_WIDTH), lambda n: (at(n), 0))
    k_prev = pl.BlockSpec((BLOCK, KV_HALF), lambda n: (jnp.maximum(at(n) - 1, 0), 0))
    k_cur = pl.BlockSpec((BLOCK, KV_HALF), lambda n: (at(n), 0))
    v_prev = pl.BlockSpec((BLOCK, KV_HALF), lambda n: (jnp.maximum(at(n) - 1, 0), 1))
    v_cur = pl.BlockSpec((BLOCK, KV_HALF), lambda n: (at(n), 1))
    return q, k_prev, k_cur, v_prev, v_cur


def _swa_attn_fwd(qn, kn, kv, sinks, cat, *, name):
    T = qn.shape[0]
    nb = T // BLOCK
    q_spec, k_prev, k_cur, v_prev, v_cur = _swa_specs(nb)

    def body(sink_ref, q_ref, kp_ref, kc_ref, vp_ref, vc_ref, _, o_ref):
        dist, valid = _swa_mask(pl.program_id(0))
        kk = jnp.concatenate([kp_ref[...], kc_ref[...]], axis=0)
        vv = jnp.concatenate([vp_ref[...], vc_ref[...]], axis=0).astype(BF16)
        q = q_ref[...]
        outs = []
        for kvh in range(SWA_KV_HEADS):
            bias, sink = _swa_head_terms(sink_ref, kvh, dist)
            p, _ps = _softmax(_stack_heads(q, kvh), _head(kk, kvh), bias, valid, sink)
            o = jnp.dot(p.astype(BF16), _head(vv, kvh), preferred_element_type=F32)
            outs += [o[g * BLOCK:(g + 1) * BLOCK] for g in range(SWA_GROUP)]
        o_ref[...] = jnp.concatenate(outs, axis=1).astype(o_ref.dtype)

    return pl.pallas_call(
        body, name=name, grid=(nb,),
        in_specs=[pl.BlockSpec(memory_space=pltpu.SMEM), q_spec, k_prev, k_cur, v_prev, v_cur, ANY], out_specs=q_spec,
        out_shape=jax.ShapeDtypeStruct(cat.shape, cat.dtype), input_output_aliases={6: 0}, compiler_params=_params("parallel"),
    )(sinks, qn, kn, kn, kv, kv, cat)


def _swa_attn_bwd(qn, kn, kv, sinks, dcat, dqn, *, name):
    T = qn.shape[0]
    nb = T // BLOCK
    q_spec, k_prev, k_cur, v_prev, v_cur = _swa_specs(nb)
    late = pl.BlockSpec((BLOCK, KV_HALF), lambda n: (jnp.maximum(n - 1, 0), 0))
    tn_dims = (((0,), (0,)), ((), ()))

    def body(sink_ref, q_ref, do_ref, kp_ref, kc_ref, vp_ref, vc_ref, _, dq_ref, dk_ref, dv_ref, ds_ref, ck, cv):
        blk = pl.program_id(0)

        @pl.when(blk == 0)
        def _():
            ck[...] = jnp.zeros_like(ck)
            cv[...] = jnp.zeros_like(cv)
            ds_ref[...] = jnp.zeros_like(ds_ref)

        @pl.when(blk < nb)
        def _():
            dist, valid = _swa_mask(blk)
            kk = jnp.concatenate([kp_ref[...], kc_ref[...]], axis=0)
            vv = jnp.concatenate([vp_ref[...], vc_ref[...]], axis=0).astype(BF16)
            q = q_ref[...]
            dout = do_ref[...].astype(BF16)
            lane = lax.broadcasted_iota(jnp.int32, (1, LANES), 1)
            dsinks = jnp.zeros((1, LANES), F32)
            dqs, dks, dvs = [], [], []
            for kvh in range(SWA_KV_HEADS):
                bias, sink = _swa_head_terms(sink_ref, kvh, dist)
                qq, kh, vh, dd = _stack_heads(q, kvh), _head(kk, kvh), _head(vv, kvh), _stack_heads(dout, kvh)
                p, ps = _softmax(qq, kh, bias, valid, sink)
                dp = lax.dot_general(dd, vh, (((1,), (1,)), ((), ())), preferred_element_type=F32)
                dsum = jnp.sum(p * dp, axis=-1, keepdims=True)
                ds = (p * (dp - dsum)).astype(BF16)
                dq = jnp.dot(ds, kh, preferred_element_type=F32) * SCALE
                dqs += [dq[g * BLOCK:(g + 1) * BLOCK] for g in range(SWA_GROUP)]
                dks.append(lax.dot_general(ds, qq, tn_dims, preferred_element_type=F32) * SCALE)
                dvs.append(lax.dot_general(p.astype(BF16), dd, tn_dims, preferred_element_type=F32))
                dsink = -(ps * dsum)
                for g in range(SWA_GROUP):
                    dsinks = dsinks + jnp.where(lane == kvh * SWA_GROUP + g, jnp.sum(dsink[g * BLOCK:(g + 1) * BLOCK]), 0.0)
            dq_ref[...] = jnp.concatenate(dqs, axis=1)
            dk = jnp.concatenate(dks, axis=1)
            dv = jnp.concatenate(dvs, axis=1)
            dk_ref[...] = ck[...] + dk[0:BLOCK]
            dv_ref[...] = cv[...] + dv[0:BLOCK]
            ck[...] = dk[BLOCK:2 * BLOCK]
            cv[...] = dv[BLOCK:2 * BLOCK]
            ds_ref[...] += dsinks

        @pl.when(blk == nb)
        def _():
            dk_ref[...] = ck[...]
            dv_ref[...] = cv[...]

    return pl.pallas_call(
        body, name=name, grid=(nb + 1,),
        in_specs=[pl.BlockSpec(memory_space=pltpu.SMEM), q_spec, q_spec, k_prev, k_cur, v_prev, v_cur, ANY],
        out_specs=(q_spec, late, late, pl.BlockSpec((1, LANES), lambda n: (0, 0))),
        out_shape=(jax.ShapeDtypeStruct(dqn.shape, dqn.dtype), jax.ShapeDtypeStruct((T, KV_HALF), F32),
                   jax.ShapeDtypeStruct((T, KV_HALF), F32), jax.ShapeDtypeStruct((1, LANES), F32)),
        scratch_shapes=[pltpu.VMEM((BLOCK, KV_HALF), F32), pltpu.VMEM((BLOCK, KV_HALF), F32)],
        input_output_aliases={7: 0}, compiler_params=_params("arbitrary"),
    )(sinks, qn, dcat, kn, kn, kv, kv, dqn)


def _mem_specs(M, tq, q_col):
    q = _cols(tq, MEM_WIDTH, q_col)
    k = pl.BlockSpec((M, MEM_WIDTH), lambda i: (0, 0))
    v = pl.BlockSpec((M, MEM_WIDTH), lambda i: (0, 1))
    return q, k, v


def _mem_attn_fwd(q, q_col, mkn, mkv, *, name):
    T = q.shape[0]
    M = mkn.shape[0]
    tq = min(T, MEM_Q_TILE)
    q_spec, k_spec, v_spec = _mem_specs(M, tq, q_col)

    def body(q_ref, k_ref, v_ref, o_ref):
        qq, kk, vv = q_ref[...], k_ref[...], v_ref[...].astype(BF16)
        outs = []
        for h in range(MEM_HEADS):
            p, _ps = _softmax(_head(qq, h), _head(kk, h), None, None, None)
            outs.append(jnp.dot(p.astype(BF16), _head(vv, h), preferred_element_type=F32))
        o_ref[...] = jnp.concatenate(outs, axis=1).astype(o_ref.dtype)

    return pl.pallas_call(
        body, name=name, grid=(T // tq,), in_specs=[q_spec, k_spec, v_spec], out_specs=_cols(tq, MEM_WIDTH, MAIN_WIDTH // MEM_WIDTH),
        out_shape=jax.ShapeDtypeStruct((T, D_MODEL), BF16), compiler_params=_params("parallel"),
    )(q, mkn, mkv)


def _mem_attn_bwd(q, q_col, mkn, mkv, dcat, *, dq_width, name):
    T = q.shape[0]
    M = mkn.shape[0]
    tq = min(T, MEM_Q_TILE)
    q_spec, k_spec, v_spec = _mem_specs(M, tq, q_col)
    last = MAIN_WIDTH // MEM_WIDTH
    tn_dims = (((0,), (0,)), ((), ()))

    def body(q_ref, do_ref, k_ref, v_ref, dq_ref, dk_ref, dv_ref):
        i = pl.program_id(0)
        qq, kk, vv, dout = q_ref[...], k_ref[...], v_ref[...].astype(BF16), do_ref[...].astype(BF16)
        dqs, dks, dvs = [], [], []
        for h in range(MEM_HEADS):
            qh, kh, vh, dh = _head(qq, h), _head(kk, h), _head(vv, h), _head(dout, h)
            p, _ps = _softmax(qh, kh, None, None, None)
            dp = lax.dot_general(dh, vh, (((1,), (1,)), ((), ())), preferred_element_type=F32)
            dsum = jnp.sum(p * dp, axis=-1, keepdims=True)
            ds = (p * (dp - dsum)).astype(BF16)
            dqs.append(jnp.dot(ds, kh, preferred_element_type=F32) * SCALE)
            dks.append(lax.dot_general(ds, qh, tn_dims, preferred_element_type=F32) * SCALE)
            dvs.append(lax.dot_general(p.astype(BF16), dh, tn_dims, preferred_element_type=F32))
        dq_ref[...] = jnp.concatenate(dqs, axis=1)
        dk = jnp.concatenate(dks, axis=1)
        dv = jnp.concatenate(dvs, axis=1)

        @pl.when(i == 0)
        def _():
            dk_ref[...] = dk
            dv_ref[...] = dv

        @pl.when(i > 0)
        def _():
            dk_ref[...] += dk
            dv_ref[...] += dv

    acc = pl.BlockSpec((M, MEM_WIDTH), lambda i: (0, 0))
    return pl.pallas_call(
        body, name=name, grid=(T // tq,), in_specs=[q_spec, _cols(tq, MEM_WIDTH, last), k_spec, v_spec],
        out_specs=(_cols(tq, MEM_WIDTH, dq_width // MEM_WIDTH - 1), acc, acc),
        out_shape=(jax.ShapeDtypeStruct((T, dq_width), F32), jax.ShapeDtypeStruct((M, MEM_WIDTH), F32),
                   jax.ShapeDtypeStruct((M, MEM_WIDTH), F32)),
        compiler_params=_params("arbitrary"),
    )(q, dcat, mkn, mkv)


def _loss(y, target, *, name):
    T, D = y.shape
    tr = _row_tile(T, D)

    def body(y_ref, t_ref, l_ref, dy_ref, dyb_ref):
        i = pl.program_id(0)
        err = y_ref[...] - t_ref[...]
        dy = err / float(D)
        dy_ref[...] = dy
        dyb_ref[...] = dy.astype(BF16)
        part = jnp.full((8, 128), 0.5 * jnp.sum(jnp.mean(err * err, axis=-1)), F32)

        @pl.when(i == 0)
        def _():
            l_ref[...] = part

        @pl.when(i > 0)
        def _():
            l_ref[...] += part

    row = pl.BlockSpec((tr, D), lambda i: (i, 0))
    return pl.pallas_call(
        body, name=name, grid=(T // tr,), in_specs=[row, row],
        out_specs=(pl.BlockSpec((8, 128), lambda i: (0, 0)), row, row),
        out_shape=(jax.ShapeDtypeStruct((8, 128), F32), jax.ShapeDtypeStruct((T, D), F32), jax.ShapeDtypeStruct((T, D), BF16)),
        compiler_params=_params("arbitrary"),
    )(y, target)


def _position():
    return lax.axis_index("x"), lax.axis_index("y"), lax.axis_index("c")


def _all_gather(arrays, *, name):
    n = len(arrays)

    def body(*refs):
        srcs, outs = refs[:n], refs[n:2 * n]
        token, send_sems, recv_sems, local_sems = refs[2 * n:]
        token[...] = jnp.zeros_like(token)
        x, y, c = _position()
        me, sibling = (x, y, c), (x, y, 1 - c)
        chips = [(1 - x, y), (x, 1 - y), (1 - x, 1 - y)]

        def slot(a, px, py, pc):
            return outs[a].at[4 * px + 2 * py + pc]

        def copy(a, k, block, to, src=None):
            return pltpu.make_async_remote_copy(
                src_ref=slot(a, *block) if src is None else src, dst_ref=slot(a, *block),
                send_sem=send_sems.at[a, k], recv_sem=recv_sems.at[a, k], device_id=to, device_id_type=MESH)

        mine = [pltpu.make_async_copy(srcs[a], slot(a, *me), local_sems.at[a]) for a in range(n)]
        for cp in mine:
            cp.start()
        first, passed = [], []
        for a in range(n):
            first.append(copy(a, 0, me, sibling, src=srcs[a]))
            first += [copy(a, 1 + j, me, (*chip, c), src=srcs[a]) for j, chip in enumerate(chips)]
        for cp in first:
            cp.start()
        for a in range(n):
            for j, chip in enumerate(chips):
                copy(a, 1 + j, (*chip, c), me).wait_recv()
                fwd = copy(a, 4 + j, (*chip, c), sibling)
                fwd.start()
                passed.append(fwd)
        for a in range(n):
            copy(a, 0, sibling, me).wait_recv()
            for j, chip in enumerate(chips):
                copy(a, 4 + j, (*chip, 1 - c), me).wait_recv()
        for cp in first + passed:
            cp.wait_send()
        for cp in mine:
            cp.wait()

    return pl.pallas_call(
        body, name=name, in_specs=[ANY] * n, out_specs=[ANY] * n + [pl.BlockSpec(memory_space=pltpu.VMEM)],
        out_shape=[jax.ShapeDtypeStruct((N_DEV,) + a.shape, a.dtype) for a in arrays] + [jax.ShapeDtypeStruct((8, 128), F32)],
        scratch_shapes=[pltpu.SemaphoreType.DMA((n, 7)), pltpu.SemaphoreType.DMA((n, 7)), pltpu.SemaphoreType.DMA((n,))],
    )(*arrays)


def _sibling_exchange(by_core, whole, *, name):
    n1, n = len(by_core), len(by_core) + len(whole)

    def body(*refs):
        srcs, outs = refs[:n], refs[n:2 * n]
        send_sems, recv_sems = refs[2 * n:]
        x, y, c = _position()
        copies = [
            pltpu.make_async_remote_copy(src_ref=srcs[a].at[:, 1 - c] if a < n1 else srcs[a], dst_ref=outs[a],
                                         send_sem=send_sems.at[a], recv_sem=recv_sems.at[a], device_id=(x, y, 1 - c),
                                         device_id_type=MESH)
            for a in range(n)]
        for cp in copies:
            cp.start()
        for cp in copies:
            cp.wait()

    out_shape = [jax.ShapeDtypeStruct(a.shape[:1] + a.shape[2:], a.dtype) for a in by_core]
    out_shape += [jax.ShapeDtypeStruct(a.shape, a.dtype) for a in whole]
    outs = pl.pallas_call(
        body, name=name, in_specs=[ANY] * n, out_specs=[ANY] * n, out_shape=out_shape,
        scratch_shapes=[pltpu.SemaphoreType.DMA((n,)), pltpu.SemaphoreType.DMA((n,))],
    )(*by_core, *whole)
    return outs[:n1], outs[n1:]


def _chip_exchange(per_chip, whole, *, name):
    n1, n = len(per_chip), len(per_chip) + len(whole)

    def body(*refs):
        srcs, outs = refs[:n], refs[n:2 * n]
        send_sems, recv_sems, local_sems = refs[2 * n:]
        x, y, c = _position()
        my_chip = 2 * x + y
        chips = [(1 - x, y), (x, 1 - y), (1 - x, 1 - y)]

        def src(a, chip):
            return srcs[a].at[chip] if a < n1 else srcs[a]

        local = [pltpu.make_async_copy(src(a, my_chip), outs[a].at[my_chip], local_sems.at[a]) for a in range(n)]
        for cp in local:
            cp.start()
        copies = [
            pltpu.make_async_remote_copy(src_ref=src(a, 2 * px + py), dst_ref=outs[a].at[my_chip],
                                         send_sem=send_sems.at[a, j], recv_sem=recv_sems.at[a, j], device_id=(px, py, c),
                                         device_id_type=MESH)
            for a in range(n) for j, (px, py) in enumerate(chips)]
        for cp in copies:
            cp.start()
        for cp in copies:
            cp.wait()
        for cp in local:
            cp.wait()

    out_shape = [jax.ShapeDtypeStruct(a.shape, a.dtype) for a in per_chip]
    out_shape += [jax.ShapeDtypeStruct((N_CHIP,) + a.shape, a.dtype) for a in whole]
    outs = pl.pallas_call(
        body, name=name, in_specs=[ANY] * n, out_specs=[ANY] * n, out_shape=out_shape,
        scratch_shapes=[pltpu.SemaphoreType.DMA((n, 3)), pltpu.SemaphoreType.DMA((n, 3)), pltpu.SemaphoreType.DMA((n,))],
    )(*per_chip, *whole)
    return outs[:n1], outs[n1:]


HBM = pl.BlockSpec(memory_space=pltpu.HBM)
SEM = pl.BlockSpec(memory_space=pltpu.SEMAPHORE)
DATAFLOW = pltpu.SideEffectType.DATAFLOW_SIDE_EFFECTING


def _device(flat):
    return flat // 4, (flat // 2) % 2, flat % 2


def _gather_copies(srcs, lands, send_sems, recv_sems, incoming):
    x, y, c = _position()
    me = 4 * x + 2 * y + c
    pairs = []
    for a in range(len(srcs)):
        for d in range(1, N_DEV):
            to, frm = (me + d) % N_DEV, (me + N_DEV - d) % N_DEV
            k = a * (N_DEV - 1) + d - 1
            sems = dict(send_sem=send_sems.at[k], recv_sem=recv_sems.at[k], device_id_type=MESH)
            out = pltpu.make_async_remote_copy(src_ref=srcs[a], dst_ref=lands[a].at[me], device_id=_device(to), **sems)
            inc = pltpu.make_async_remote_copy(src_ref=srcs[a], dst_ref=lands[a].at[frm], device_id=_device(frm),
                                               **sems) if incoming else None
            pairs.append((out, inc))
    return pairs


def _chip_copies(srcs, lands, send_sems, recv_sems, incoming, n_whole=0):
    x, y, c = _position()
    my_chip = 2 * x + y
    pairs = []
    for a in range(len(srcs)):
        for k, (px, py) in enumerate([(1 - x, y), (x, 1 - y), (1 - x, 1 - y)]):
            sem = a * (N_CHIP - 1) + k
            sems = dict(send_sem=send_sems.at[sem], recv_sem=recv_sems.at[sem], device_id=(px, py, c), device_id_type=MESH)
            src = srcs[a] if a >= len(srcs) - n_whole else srcs[a].at[2 * px + py]
            out = pltpu.make_async_remote_copy(src_ref=src, dst_ref=lands[a].at[my_chip], **sems)
            inc = pltpu.make_async_remote_copy(src_ref=src, dst_ref=lands[a].at[2 * px + py], **sems) if incoming else None
            pairs.append((out, inc))
    return pairs


def _sibling_copies(srcs, lands, send_sems, recv_sems, incoming):
    x, y, c = _position()
    pairs = []
    for a in range(len(srcs)):
        copy = pltpu.make_async_remote_copy(src_ref=srcs[a].at[:, 1 - c], dst_ref=lands[a], send_sem=send_sems.at[a],
                                            recv_sem=recv_sems.at[a], device_id=(x, y, 1 - c), device_id_type=MESH)
        pairs.append((copy, copy if incoming else None))
    return pairs


def _push_start(copies, fan, srcs, lands, *, name):
    n = len(srcs)

    def body(*refs):
        src_refs, land_refs = refs[:n], refs[n:2 * n]
        send_sems, recv_sems = refs[2 * n], refs[2 * n + 1]
        token = refs[-1]
        for out, _ in copies(src_refs, land_refs, send_sems, recv_sems, False):
            out.start()
        token[...] = jnp.zeros_like(token)

    outs = pl.pallas_call(
        body, name=name,
        out_shape=(pltpu.SemaphoreType.DMA((n * fan,)), pltpu.SemaphoreType.DMA((n * fan,)),
                   *(pltpu.HBM(a.shape, a.dtype) for a in srcs), *(pltpu.HBM(a.shape, a.dtype) for a in lands),
                   jax.ShapeDtypeStruct((8, 128), F32)),
        in_specs=[HBM] * (2 * n), out_specs=(SEM, SEM, *([HBM] * (2 * n)), pl.BlockSpec(memory_space=pltpu.VMEM)),
        input_output_aliases={i: 2 + i for i in range(2 * n)},
        compiler_params=pltpu.CompilerParams(has_side_effects=DATAFLOW),
    )(*(pltpu.with_memory_space_constraint(a, pltpu.HBM) for a in (*srcs, *lands)))
    return outs[0], outs[1], list(outs[2:2 + n]), list(outs[2 + n:2 + 2 * n]), outs[-1]


def _push_wait(copies, send_sems, recv_sems, srcs, lands, after, *, name):
    n = len(srcs)

    def body(*refs):
        src_refs, land_refs = refs[:n], refs[n:2 * n]
        for out, inc in copies(src_refs, land_refs, refs[2 * n], refs[2 * n + 1], True):
            out.wait_send()
            inc.wait_recv()
        refs[-1][...] = jnp.zeros_like(refs[-1])

    outs = pl.pallas_call(
        body, name=name,
        out_shape=(*(pltpu.HBM(a.shape, a.dtype) for a in (*srcs, *lands)), jax.ShapeDtypeStruct((8, 128), F32)),
        in_specs=[HBM] * (2 * n) + [SEM, SEM, ANY], out_specs=(*([HBM] * (2 * n)), pl.BlockSpec(memory_space=pltpu.VMEM)),
        input_output_aliases={i: i for i in range(2 * n)},
        compiler_params=pltpu.CompilerParams(has_side_effects=DATAFLOW),
    )(*srcs, *lands, send_sems, recv_sems, after)
    return list(outs[n:2 * n]), outs[-1]


def _with_own_slot(block, index, slots):
    buf = lax.empty((slots,) + block.shape, block.dtype)
    return lax.dynamic_update_slice(buf, block[None], (index,) + (0,) * block.ndim)


def _view2d(shape):
    return math.prod(shape[:-1]), shape[-1]


def _pair_sum(mine, other, core, *, name, out_dtype):
    by_core = mine.ndim == 4
    n, w = other.shape[-2:]
    tr = _row_tile(n, w * 2)
    lead = other.shape[0] if by_core else 1

    def body(core_ref, a_ref, b_ref, o_ref):
        o_ref[...] = (a_ref[...].astype(F32) + b_ref[...].astype(F32)).astype(o_ref.dtype)

    if by_core:
        a_spec = pl.BlockSpec((None, None, tr, w), lambda j, i, core_ref: (j, core_ref[0], i, 0))
        o_spec = pl.BlockSpec((None, tr, w), lambda j, i, core_ref: (j, i, 0))
    else:
        a_spec = o_spec = pl.BlockSpec((tr, w), lambda j, i, core_ref: (i, 0))
    grid_spec = pltpu.PrefetchScalarGridSpec(num_scalar_prefetch=1, grid=(lead, n // tr), in_specs=[a_spec, o_spec],
                                             out_specs=o_spec)
    return pl.pallas_call(body, name=name, grid_spec=grid_spec, out_shape=jax.ShapeDtypeStruct(other.shape, out_dtype),
                          compiler_params=_params("parallel", "parallel"))(core.reshape(1), mine, other)


def _adamw(parts, w, m, v, *, name):
    layers = len(parts)
    n_parts, R, W = parts[0].shape
    tr = _row_tile(R, W * 2)
    per_layer = R // tr

    def update(p_ref, w_ref, m_ref, v_ref, g_out, d_out, m_out, v_out):
        g = p_ref[0].astype(F32)
        for j in range(1, n_parts):
            g = g + p_ref[j].astype(F32)
        m_new = ADAM_B1 * m_ref[...] + (1.0 - ADAM_B1) * g
        v_new = ADAM_B2 * v_ref[...] + (1.0 - ADAM_B2) * (g * g)
        m_hat = m_new / (1.0 - ADAM_B1 ** ADAM_STEP)
        v_hat = v_new / (1.0 - ADAM_B2 ** ADAM_STEP)
        g_out[...] = g
        d_out[...] = -ADAM_LR * (m_hat / (jnp.sqrt(v_hat) + ADAM_EPS) + ADAM_WD * w_ref[...])
        m_out[...] = m_new
        v_out[...] = v_new

    def body(*refs):
        for k in range(layers):
            pl.when(pl.program_id(0) == k)(lambda k=k: update(refs[k], *refs[layers:]))

    def parts_spec(k):
        return pl.BlockSpec((n_parts, tr, W), lambda l, i: (0, jnp.where(l == k, i, 0), 0))

    row = pl.BlockSpec((tr, W), lambda l, i: (l * per_layer + i, 0))
    out = jax.ShapeDtypeStruct((layers * R, W), F32)
    return pl.pallas_call(
        body, name=name, grid=(layers, per_layer), in_specs=[parts_spec(k) for k in range(layers)] + [row, row, row],
        out_specs=(row, row, row, row), out_shape=(out, out, out, out), compiler_params=_params("arbitrary", "arbitrary"),
    )(*parts, w, m, v)


SMALL_ROWS = 608


def _pack_small(p):
    flat = jnp.concatenate([p[n].reshape(-1).astype(F32) for n in SMALL_NAMES])
    return jnp.pad(flat, (0, SMALL_ROWS * PACK_W - flat.shape[0])).reshape(SMALL_ROWS, PACK_W)


def _unpack_small(buf, like):
    out, at = {}, 0
    flat = buf.reshape(-1)
    for n in SMALL_NAMES:
        size = math.prod(like[n].shape)
        out[n] = flat[at:at + size].reshape(like[n].shape)
        at += size
    return out


def _heads(a, nh):
    T = a.shape[0]
    return a.reshape(T, nh, HEAD_DIM).transpose(1, 0, 2).reshape(nh * T, HEAD_DIM)


def _unheads(a, nh):
    a = a.reshape(nh, -1, HEAD_DIM)
    return a.transpose(1, 0, 2).reshape(a.shape[1], nh * HEAD_DIM)


def _groups(a):
    T = a.shape[0]
    return a.reshape(T, POOL_GROUPS, POOL_GROUP_DIM).transpose(1, 0, 2)


def _ungroups(a):
    return a.transpose(1, 0, 2).reshape(a.shape[1], MAIN_WIDTH)


def _local_step(x, mem, target, p, w_kv, fetch, reduce_layer, reduce_wait):
    T = x.shape[0]
    M = mem.shape[0]
    saved = []
    h = x
    kn = vv = k_raw = h_kv = hn_kv = None
    for l in range(DEPTH):
        s = {}
        wl, token = fetch(l, h)
        s["w"] = wl
        if l == N_A:
            h_kv = h
            hn_kv = _rms_fwd(h, p["kv_norm"], name="kv_norm_fwd")
            kv = _mm(hn_kv, w_kv, b_kind="rows", name="kv_proj")
            k_raw = _heads(kv[:, :KV_HALF], SWA_KV_HEADS)
            kn = _rms_fwd(k_raw, p["k_norm"], name="k_norm_fwd").reshape(SWA_KV_HEADS, T, HEAD_DIM)
            vv = _heads(kv[:, KV_HALF:], SWA_KV_HEADS).astype(BF16).reshape(SWA_KV_HEADS, T, HEAD_DIM)
        s["h"] = h
        s["xn1"] = _rms_fwd(h, p["norm_mix"][l] + token, name="norm_mix_fwd")
        proj = _mm(s["xn1"], wl["w_in"], b_kind="rows", name="in_proj")
        s["mq_raw"] = _heads(proj[:, MAIN_WIDTH:], MEM_HEADS)
        s["mqn"] = _rms_fwd(s["mq_raw"], p["mem_q_norm"][l], name="mem_q_norm_fwd").reshape(MEM_HEADS, T, HEAD_DIM)
        s["memn"] = _rms_fwd(mem, p["mem_norm"][l], name="mem_norm_fwd")
        mkv = _mm(s["memn"], wl["w_mem_kv"], b_kind="rows", name="mem_kv_proj")
        s["mk_raw"] = _heads(mkv[:, :MEM_WIDTH], MEM_HEADS)
        s["mkn"] = _rms_fwd(s["mk_raw"], p["mem_k_norm"][l], name="mem_k_norm_fwd").reshape(MEM_HEADS, M, HEAD_DIM)
        s["mvv"] = _heads(mkv[:, MEM_WIDTH:], MEM_HEADS).astype(BF16).reshape(MEM_HEADS, M, HEAD_DIM)
        mem_out = _unheads(_mem_fwd(s["mqn"], s["mkn"], s["mvv"], name="mem_attn_fwd"), MEM_HEADS)
        if l < N_A:
            s["u"] = _groups(proj[:, :MAIN_WIDTH])
            s["pw"] = p["pool_w"][l]
            s["ps"] = p["pool_scale"][l].reshape(POOL_GROUPS, 1, POOL_GROUP_DIM)
            main_out = _ungroups(_pool_fwd(s["u"], s["pw"], s["ps"], name="pool_fwd"))
        else:
            j = l - N_A
            s["q_raw"] = _heads(proj[:, :MAIN_WIDTH], SWA_Q_HEADS)
            s["qn"] = _rms_fwd(s["q_raw"], p["q_norm"][j], name="q_norm_fwd").reshape(SWA_KV_HEADS, SWA_GROUP, T, HEAD_DIM)
            main_out = _unheads(_swa_fwd(s["qn"], kn, vv, p["sinks"][j], name="swa_fwd"), SWA_Q_HEADS)
        s["cat"] = jnp.concatenate([main_out, mem_out], axis=-1)
        s["h1"] = _mm(s["cat"], wl["w_out"], b_kind="rows", res=h, name="out_proj")
        s["xn2"] = _rms_fwd(s["h1"], p["norm_mlp"][l], name="norm_mlp_fwd")
        s["r"], s["a"] = _mm(s["xn2"], wl["w_up"], b_kind="layers", relu2=True, name="mlp_up")
        h = _mm(s["a"], wl["w_down"], b_kind="rows", res=s["h1"], name="mlp_down")
        saved.append(s)

    loss, dh, dh_b = _loss(h, target, name="loss_head")

    g = {n: [None] * DEPTH for n in ("norm_mix", "mem_norm", "mem_q_norm", "mem_k_norm", "norm_mlp")}
    g_kv = None
    token = None
    g.update({n: [None] * N_A for n in ("pool_w", "pool_scale", "q_norm", "sinks")})
    dkn = dvv = None
    for l in reversed(range(DEPTH)):
        s = saved[l]
        wl = s["w"]
        gb = {}

        def dw(a, dy, n):
            return _mm(a, dy, ta=True, out_kind="layers" if n == "w_up" else "rows", out_buf=lax.empty(wl[n].shape, BF16),
                       name=n + "_grad")

        norm_mlp_gain = p["norm_mlp"][l] if token is None else p["norm_mlp"][l] + token
        gb["w_down"] = dw(s["a"], dh_b, "w_down")
        du = _mm(dh_b, wl["w_down"], tb=True, b_kind="rows", mul2=s["a"], out_dtype=BF16, name="mlp_down_dx")
        gb["w_up"] = dw(s["xn2"], du, "w_up")
        dxn2 = _mm(du, wl["w_up"], tb=True, b_kind="layers", name="mlp_up_dx")
        dh1, dh1_b, g["norm_mlp"][l] = _rms_bwd(s["h1"], norm_mlp_gain, [dxn2], res=dh, also_bf16=True,
                                                name="norm_mlp_bwd")
        gb["w_out"] = dw(s["cat"], dh1_b, "w_out")
        dcat = _mm(dh1_b, wl["w_out"], tb=True, b_kind="rows", name="out_proj_dx")
        dmem_out = _heads(dcat[:, MAIN_WIDTH:], MEM_HEADS).astype(BF16).reshape(MEM_HEADS, T, HEAD_DIM)
        dmqn, dmkn, dmvv = _mem_bwd(s["mqn"], s["mkn"], s["mvv"], dmem_out, name="mem_attn_bwd")
        dmq_raw, g["mem_q_norm"][l] = _rms_bwd(s["mq_raw"], p["mem_q_norm"][l], [dmqn.reshape(MEM_HEADS * T, HEAD_DIM)],
                                               name="mem_q_norm_bwd")
        dmk_raw, g["mem_k_norm"][l] = _rms_bwd(s["mk_raw"], p["mem_k_norm"][l], [dmkn.reshape(MEM_HEADS * M, HEAD_DIM)],
                                               name="mem_k_norm_bwd")
        dmkv = jnp.concatenate([_unheads(dmk_raw, MEM_HEADS), _unheads(dmvv, MEM_HEADS)], axis=-1).astype(BF16)
        gb["w_mem_kv"] = dw(s["memn"], dmkv, "w_mem_kv")
        dmemn = _mm(dmkv, wl["w_mem_kv"], tb=True, b_kind="rows", name="mem_kv_proj_dx")
        g["mem_norm"][l] = _rms_bwd(mem, p["mem_norm"][l], [dmemn], want_dx=False, name="mem_norm_bwd")
        if l < N_A:
            dmain_out = _groups(dcat[:, :MAIN_WIDTH])
            du_pool, g["pool_w"][l], dps = _pool_bwd(s["u"], s["pw"], s["ps"], dmain_out, name="pool_bwd")
            g["pool_scale"][l] = dps.reshape(MAIN_WIDTH)
            dmain = _ungroups(du_pool)
        else:
            j = l - N_A
            dmain_out = _heads(dcat[:, :MAIN_WIDTH], SWA_Q_HEADS).astype(BF16).reshape(SWA_KV_HEADS, SWA_GROUP, T, HEAD_DIM)
            dqn, dk_l, dv_l, dsink = _swa_bwd(s["qn"], kn, vv, p["sinks"][j], dmain_out, name="swa_bwd")
            g["sinks"][j] = dsink[:, 0, :SWA_GROUP].reshape(SWA_Q_HEADS)
            dq_raw, g["q_norm"][j] = _rms_bwd(s["q_raw"], p["q_norm"][j], [dqn.reshape(SWA_Q_HEADS * T, HEAD_DIM)],
                                              name="q_norm_bwd")
            dmain = _unheads(dq_raw, SWA_Q_HEADS)
            dk_l = dk_l.reshape(SWA_KV_HEADS * T, HEAD_DIM)
            dv_l = dv_l.reshape(SWA_KV_HEADS * T, HEAD_DIM)
            dkn = dk_l if dkn is None else _add(dkn, dk_l, name="dk_sum")
            dvv = dv_l if dvv is None else _add(dvv, dv_l, name="dv_sum")
        dproj = jnp.concatenate([dmain, _unheads(dmq_raw, MEM_HEADS)], axis=-1).astype(BF16)
        gb["w_in"] = dw(s["xn1"], dproj, "w_in")
        dxn1 = _mm(dproj, wl["w_in"], tb=True, b_kind="rows", name="in_proj_dx")
        if l in (0, N_A):
            dh, g["norm_mix"][l] = _rms_bwd(s["h"], p["norm_mix"][l], [dxn1], res=dh1, name="norm_mix_bwd")
        else:
            dh, dh_b, g["norm_mix"][l] = _rms_bwd(s["h"], p["norm_mix"][l], [dxn1], res=dh1, also_bf16=True,
                                                  name="norm_mix_bwd")
        if l == N_A:
            dk_raw, g["k_norm"] = _rms_bwd(k_raw, p["k_norm"], [dkn], name="k_norm_bwd")
            dkv = jnp.concatenate([_unheads(dk_raw, SWA_KV_HEADS), _unheads(dvv, SWA_KV_HEADS)], axis=-1).astype(BF16)
            g_kv = _mm(hn_kv, dkv, ta=True, out_kind="rows", out_buf=lax.empty(w_kv.shape, BF16), name="w_kv_grad")
            dhn = _mm(dkv, w_kv, tb=True, b_kind="rows", name="kv_proj_dx")
            dh, dh_b, g["kv_norm"] = _rms_bwd(h_kv, p["kv_norm"], [dhn], res=dh, also_bf16=True, name="kv_norm_bwd")
        if l + 1 < DEPTH:
            reduce_wait(l + 1, dh)
        token = reduce_layer(l, gb)
    grads = {n: (jnp.stack(v) if isinstance(v, list) else v) for n, v in g.items()}
    return loss, dh, grads, g_kv


def _block_diag(pw):
    out = jnp.zeros((MAIN_WIDTH, MAIN_WIDTH), pw.dtype)
    for g in range(POOL_GROUPS):
        out = lax.dynamic_update_slice(out, pw[g], (g * POOL_GROUP_DIM, g * POOL_GROUP_DIM))
    return out


def _diag_blocks(m):
    return jnp.stack([m[g * POOL_GROUP_DIM:(g + 1) * POOL_GROUP_DIM, g * POOL_GROUP_DIM:(g + 1) * POOL_GROUP_DIM]
                      for g in range(POOL_GROUPS)])


def _train_pass(x, mem, target, p, w_kv, fetch, reduce_layer, reduce_wait):
    T = x.shape[0]
    mem_cols = MAIN_WIDTH // MEM_WIDTH
    k_gain = _head_gain(p["k_norm"], SWA_KV_HEADS)
    saved = []
    h = x
    kn = kv = h_kv = hn_kv = None
    for l in range(DEPTH):
        s = {}
        wl, token = fetch(l, h)
        s["w"] = wl
        if l == N_A:
            h_kv = h
            hn_kv, kv = _norm_mm(h, p["kv_norm"], w_kv, b_kind="rows", name="kv_proj")
            kn = _seg_rms_fwd(kv, k_gain, width=KV_HALF, col=0, name="k_norm_fwd")
        s["h"] = h
        s["xn1"], proj = _norm_mm(h, p["norm_mix"][l] + token, wl["w_in"], b_kind="rows", name="in_proj")
        s["proj"] = proj
        s["memn"] = _rms_fwd(mem, p["mem_norm"][l], name="mem_norm_fwd")
        s["mkv"] = _mm(s["memn"], wl["w_mem_kv"], b_kind="rows", name="mem_kv_proj")
        s["mk_gain"] = _head_gain(p["mem_k_norm"][l], MEM_HEADS)
        s["mkn"] = _seg_rms_fwd(s["mkv"], s["mk_gain"], width=MEM_WIDTH, col=0, name="mem_k_norm_fwd")
        if l < N_A:
            s["q_gain"] = _head_gain(p["mem_q_norm"][l], MEM_HEADS)
            s["qn"] = _seg_rms_fwd(proj, s["q_gain"], width=MEM_WIDTH, col=mem_cols, name="mem_q_norm_fwd")
            s["q_col"] = 0
        else:
            j = l - N_A
            s["q_gain"] = jnp.concatenate([_head_gain(p["q_norm"][j], SWA_Q_HEADS), _head_gain(p["mem_q_norm"][l], MEM_HEADS)],
                                          axis=1)
            s["qn"] = _seg_rms_fwd(proj, s["q_gain"], width=D_MODEL, col=0, name="q_norm_fwd")
            s["q_col"] = mem_cols
        cat = _mem_attn_fwd(s["qn"], s["q_col"], s["mkn"], s["mkv"], name="mem_attn_fwd")
        if l < N_A:
            s["mix"] = _block_diag(p["pool_w"][l])
            s["scale"] = p["pool_scale"][l].reshape(1, MAIN_WIDTH)
            s["cat"] = _pool_mix_fwd(proj, s["mix"], s["scale"], cat, name="pool_fwd")
        else:
            s["cat"] = _swa_attn_fwd(s["qn"], kn, kv, p["sinks"][l - N_A], cat, name="swa_fwd")
        s["h1"] = _mm(s["cat"], wl["w_out"], b_kind="rows", res=h, name="out_proj")
        s["xn2"], s["a"] = _norm_mm(s["h1"], p["norm_mlp"][l], wl["w_up"], b_kind="layers", relu2=True, name="mlp_up")
        h = _mm(s["a"], wl["w_down"], b_kind="rows", res=s["h1"], name="mlp_down")
        saved.append(s)

    loss, dh, dh_b = _loss(h, target, name="loss_head")

    g = {n: [None] * DEPTH for n in ("norm_mix", "mem_norm", "mem_q_norm", "mem_k_norm", "norm_mlp")}
    g.update({n: [None] * N_A for n in ("pool_w", "pool_scale", "q_norm", "sinks")})
    g_kv = None
    token = None
    dks, dvs = [], []
    for l in reversed(range(DEPTH)):
        s = saved[l]
        wl = s["w"]
        gb = {}

        def dw(a, dy, n):
            return _mm(a, dy, ta=True, out_kind="layers" if n == "w_up" else "rows", out_buf=lax.empty(wl[n].shape, BF16),
                       name=n + "_grad")

        norm_mlp_gain = p["norm_mlp"][l] if token is None else p["norm_mlp"][l] + token
        gb["w_down"] = dw(s["a"], dh_b, "w_down")
        du = _mm(dh_b, wl["w_down"], tb=True, b_kind="rows", mul2=s["a"], out_dtype=BF16, name="mlp_down_dx")
        gb["w_up"] = dw(s["xn2"], du, "w_up")
        early = reduce_layer(l, gb, early=True)
        if early is not None:
            norm_mlp_gain = norm_mlp_gain + early
        dh1, dh1_b, g["norm_mlp"][l] = _mm_rms_bwd(du, wl["w_up"], s["h1"], norm_mlp_gain, dh, b_kind="layers", also_bf16=True,
                                                   name="mlp_up_dx")
        gb["w_out"] = dw(s["cat"], dh1_b, "w_out")
        dcat = _mm(dh1_b, wl["w_out"], tb=True, b_kind="rows", name="out_proj_dx")
        if l < N_A:
            dq, dmk, dmv = _mem_attn_bwd(s["qn"], s["q_col"], s["mkn"], s["mkv"], dcat, dq_width=MEM_WIDTH, name="mem_attn_bwd")
            dproj, dmix, dscale = _pool_mix_bwd(s["proj"], s["mix"], s["scale"], dcat, name="pool_bwd")
            g["pool_w"][l] = _diag_blocks(dmix)
            g["pool_scale"][l] = dscale.reshape(MAIN_WIDTH)
            dproj, dgain = _seg_rms_bwd(s["proj"], s["q_gain"], [dq], width=MEM_WIDTH, col=mem_cols, out_buf=dproj,
                                        out_col=mem_cols, name="mem_q_norm_bwd")
            g["mem_q_norm"][l] = _fold_heads(dgain, MEM_HEADS)
        else:
            j = l - N_A
            dqn, dmk, dmv = _mem_attn_bwd(s["qn"], s["q_col"], s["mkn"], s["mkv"], dcat, dq_width=D_MODEL, name="mem_attn_bwd")
            dqn, dk_l, dv_l, dsinks = _swa_attn_bwd(s["qn"], kn, kv, p["sinks"][j], dcat, dqn, name="swa_bwd")
            dks.append(dk_l)
            dvs.append(dv_l)
            g["sinks"][j] = dsinks[0, :SWA_Q_HEADS]
            dproj, dgain = _seg_rms_bwd(s["proj"], s["q_gain"], [dqn], width=D_MODEL, col=0, name="q_norm_bwd")
            g["q_norm"][j] = _fold_heads(dgain[:, :MAIN_WIDTH], SWA_Q_HEADS)
            g["mem_q_norm"][l] = _fold_heads(dgain[:, MAIN_WIDTH:], MEM_HEADS)
        dmk_raw, dgain = _seg_rms_bwd(s["mkv"], s["mk_gain"], [dmk], width=MEM_WIDTH, col=0, name="mem_k_norm_bwd")
        g["mem_k_norm"][l] = _fold_heads(dgain, MEM_HEADS)
        dmkv = jnp.concatenate([dmk_raw, dmv.astype(BF16)], axis=1)
        gb["w_mem_kv"] = dw(s["memn"], dmkv, "w_mem_kv")
        dmemn = _mm(dmkv, wl["w_mem_kv"], tb=True, b_kind="rows", name="mem_kv_proj_dx")
        g["mem_norm"][l] = _rms_bwd(mem, p["mem_norm"][l], [dmemn], want_dx=False, name="mem_norm_bwd")
        gb["w_in"] = dw(s["xn1"], dproj, "w_in")
        if l in (0, N_A):
            dh, g["norm_mix"][l] = _mm_rms_bwd(dproj, wl["w_in"], s["h"], p["norm_mix"][l], dh1, b_kind="rows", also_bf16=False,
                                               name="in_proj_dx")
        else:
            dh, dh_b, g["norm_mix"][l] = _mm_rms_bwd(dproj, wl["w_in"], s["h"], p["norm_mix"][l], dh1, b_kind="rows",
                                                     also_bf16=True, name="in_proj_dx")
        if l == N_A:
            dkv, dgain = _seg_rms_bwd(kv, k_gain, dks, width=KV_HALF, col=0, out_buf=lax.empty((T, 2 * KV_HALF), BF16),
                                      name="k_norm_bwd")
            g["k_norm"] = _fold_heads(dgain, SWA_KV_HEADS)
            dkv = _sum_into(dvs[0], dvs[1], dkv, 1, name="dv_sum")
            g_kv = _mm(hn_kv, dkv, ta=True, out_kind="rows", out_buf=lax.empty(w_kv.shape, BF16), name="w_kv_grad")
            dh, dh_b, g["kv_norm"] = _mm_rms_bwd(dkv, w_kv, h_kv, p["kv_norm"], dh, b_kind="rows", also_bf16=True,
                                                 name="kv_proj_dx")
        if l + 1 < DEPTH:
            reduce_wait(l + 1, dh)
        token = reduce_layer(l, gb)
    grads = {n: (jnp.stack(v) if isinstance(v, list) else v) for n, v in g.items()}
    return loss, dh, grads, g_kv


def kernel(x, mem, norm_mix, w_in, pool_w, pool_scale, kv_norm, w_kv, k_norm, q_norm, sinks, mem_norm, w_mem_kv, mem_q_norm, mem_k_norm, w_out, norm_mlp, w_up, w_down, loss_target, m_norm_mix, m_w_in, m_pool_w, m_pool_scale, m_kv_norm, m_w_kv, m_k_norm, m_q_norm, m_sinks, m_mem_norm, m_w_mem_kv, m_mem_q_norm, m_mem_k_norm, m_w_out, m_norm_mlp, m_w_up, m_w_down, v_norm_mix, v_w_in, v_pool_w, v_pool_scale, v_kv_norm, v_w_kv, v_k_norm, v_q_norm, v_sinks, v_mem_norm, v_w_mem_kv, v_mem_q_norm, v_mem_k_norm, v_w_out, v_norm_mlp, v_w_up, v_w_down):
    weights = dict(norm_mix=norm_mix, w_in=w_in, pool_w=pool_w, pool_scale=pool_scale, kv_norm=kv_norm, w_kv=w_kv,
                   k_norm=k_norm, q_norm=q_norm, sinks=sinks, mem_norm=mem_norm, w_mem_kv=w_mem_kv,
                   mem_q_norm=mem_q_norm, mem_k_norm=mem_k_norm, w_out=w_out, norm_mlp=norm_mlp, w_up=w_up, w_down=w_down)
    mom1 = dict(norm_mix=m_norm_mix, w_in=m_w_in, pool_w=m_pool_w, pool_scale=m_pool_scale, kv_norm=m_kv_norm, w_kv=m_w_kv,
                k_norm=m_k_norm, q_norm=m_q_norm, sinks=m_sinks, mem_norm=m_mem_norm, w_mem_kv=m_w_mem_kv,
                mem_q_norm=m_mem_q_norm, mem_k_norm=m_mem_k_norm, w_out=m_w_out, norm_mlp=m_norm_mlp, w_up=m_w_up,
                w_down=m_w_down)
    mom2 = dict(norm_mix=v_norm_mix, w_in=v_w_in, pool_w=v_pool_w, pool_scale=v_pool_scale, kv_norm=v_kv_norm, w_kv=v_w_kv,
                k_norm=v_k_norm, q_norm=v_q_norm, sinks=v_sinks, mem_norm=v_mem_norm, w_mem_kv=v_w_mem_kv,
                mem_q_norm=v_mem_q_norm, mem_k_norm=v_mem_k_norm, w_out=v_w_out, norm_mlp=v_norm_mlp, w_up=v_w_up,
                w_down=v_w_down)
    names = list(weights)
    x_pos, y_pos, core = (lax.axis_index(n).astype(jnp.int32) for n in AXES)
    me, my_chip = 4 * x_pos + 2 * y_pos + core, 2 * x_pos + y_pos
    shard = MAIN_WIDTH // N_DEV

    def layer_shards(l, zero=0.0):
        return [(weights[n][l:l + 1] + zero).astype(BF16) for n in LAYERED]

    def usable(arrays):
        wl = dict(zip(LAYERED, arrays))
        wl["w_up"] = wl["w_up"].transpose(1, 2, 0, 3).reshape(1, D_MODEL, D_FF)
        return wl

    scale_block = jnp.pad(pool_scale, ((0, 8 - N_A), (0, 128 - shard)))
    *first, first_done = _all_gather(layer_shards(0) + [w_kv[None].astype(BF16), scale_block], name="gather_first")
    p = {n: weights[n] for n in SMALL_NAMES}
    p["pool_scale"] = first[-1][:, :N_A, :shard].transpose(1, 0, 2).reshape(N_A, MAIN_WIDTH)
    gathers, reduces, parts = {}, {}, {}

    def fetch(l, after):
        if l == 0:
            got, done = first[:len(LAYERED)], first_done
        else:
            got, done = _push_wait(_gather_copies, *gathers.pop(l), after, name=f"gather_wait_{l}")
        token = 0.0
        if l + 1 < DEPTH:
            srcs = layer_shards(l + 1, done[0, 0])
            lands = [_with_own_slot(a, me, N_DEV) for a in srcs]
            *handles, block = _push_start(_gather_copies, N_DEV - 1, srcs, lands, name=f"gather_start_{l + 1}")
            gathers[l + 1], token = handles, block[0, 0]
        return usable(got), token

    def by_core(gb):
        gb = dict(gb)
        if "w_up" in gb:
            gb["w_up"] = gb["w_up"].reshape(D_MODEL, N_DEV, D_FF // N_DEV).transpose(1, 0, 2)
        order = [n for n in LAYERED if n in gb] + [n for n in gb if n not in LAYERED]
        return {n: gb[n].reshape((N_CHIP, 2) + _view2d(gb[n].shape[1:] if n == "w_up" else gb[n].shape[2:])) for n in order}

    def pair_sums(views, sib, tag):
        return [_pair_sum(a, b, core, name=f"chip_sum_{n}_{tag}", out_dtype=BF16) for (n, a), b in zip(views.items(), sib)]

    def chip_sums(gb, tag, whole=()):
        views = by_core(gb)
        sib, sib_whole = _sibling_exchange(list(views.values()), list(whole), name="reduce_sibling_" + tag)
        return pair_sums(views, sib, tag), sib_whole

    def start_chip_exchange(sums, tag, whole=()):
        lands = [_with_own_slot(lax.dynamic_index_in_dim(a, my_chip, 0, keepdims=False), my_chip, N_CHIP) for a in sums]
        lands += [_with_own_slot(a, my_chip, N_CHIP) for a in whole]
        copies = functools.partial(_chip_copies, n_whole=len(whole))
        *handles, block = _push_start(copies, N_CHIP - 1, [*sums, *whole], lands, name="reduce_start_" + tag)
        return (copies, *handles), block[0, 0]

    mlp = ("w_up", "w_down")

    def reduce_layer(l, gb, early=False):
        if early and l > 0:
            return None
        if l == 0 and not early:
            reduces["rest"] = {n: a for n, a in gb.items() if n not in mlp}
            return None
        tag = "0_mlp" if early else str(l)
        sums, _ = chip_sums({n: gb[n] for n in mlp} if early else gb, tag)
        reduces[l], token = start_chip_exchange(sums, tag)
        return token

    def reduce_wait(l, after):
        copies, *handles = reduces.pop(l)
        return _push_wait(copies, *handles, after, name=f"reduce_wait_{l}")[0]

    def layer_wait(l, after):
        parts[l] = reduce_wait(l, after)

    loss, grad_x, grads, g_kv = _train_pass(x[0], mem[0], loss_target[0], p, first[len(LAYERED)], fetch, reduce_layer, layer_wait)

    last = dict(reduces.pop("rest"))
    last["w_kv"] = g_kv
    last["pool_scale"] = grads["pool_scale"].reshape(N_A, N_DEV, shard).transpose(1, 0, 2).astype(BF16)[:, None]
    small = _pack_small(grads)
    sums, (sib_small,) = chip_sums(last, "0", whole=[small])
    chip_small = _pair_sum(small, sib_small, core, name="chip_sum_small", out_dtype=F32)
    reduces["rest"], _ = start_chip_exchange(sums, "0_rest", whole=[chip_small])

    def adamw(n, n_parts):
        res = _adamw(n_parts, *(d[n].reshape(_view2d(d[n].shape)) for d in (weights, mom1, mom2)), name="adamw_" + n)
        return [r.reshape(weights[n].shape) for r in res]

    p_up, p_down = reduce_wait(0, chip_small)
    parts[0] = [None, None, None, p_up, p_down]
    new = {n: adamw(n, [parts[l][LAYERED.index(n)] for l in range(DEPTH)]) for n in mlp}
    p_in, p_mem_kv, p_out, parts_kv, parts_scale, parts_small = reduce_wait("rest", new["w_down"][0])
    parts[0][:3] = [p_in, p_mem_kv, p_out]
    new.update({n: adamw(n, [parts[l][k] for l in range(DEPTH)]) for k, n in enumerate(LAYERED) if n not in mlp})
    new["w_kv"] = adamw("w_kv", [parts_kv])
    new["pool_scale"] = adamw("pool_scale", [parts_scale])
    res = _adamw([parts_small], _pack_small(weights), _pack_small(mom1), _pack_small(mom2), name="adamw_replicated")
    for n, vals in zip(SMALL_NAMES, zip(*(_unpack_small(r, weights).values() for r in res))):
        new[n] = list(vals)
    outs = [new[n][k] for k in range(4) for n in names]
    total = lax.psum(loss[0, 0], AXES)
    return (total, grad_x[None], *outs)
```

```python
import functools
import math

import jax
import jax.numpy as jnp
from jax import lax
from jax.experimental import pallas as pl
from jax.experimental.pallas import tpu as pltpu

F32 = jnp.float32
BF16 = jnp.bfloat16
MESH = pl.DeviceIdType.MESH
AXES = ("x", "y", "c")

D_MODEL = 1024
DEPTH = 4
N_A = 2
HEAD_DIM = 64
MEM_HEADS = 4
MEM_WIDTH = MEM_HEADS * HEAD_DIM
MAIN_WIDTH = D_MODEL - MEM_WIDTH
POOL_GROUPS = 4
POOL_GROUP_DIM = MAIN_WIDTH // POOL_GROUPS
POOL_HALO = 16
SWA_Q_HEADS = MAIN_WIDTH // HEAD_DIM
SWA_KV_HEADS = 4
SWA_GROUP = SWA_Q_HEADS // SWA_KV_HEADS
KV_HALF = SWA_KV_HEADS * HEAD_DIM
BLOCK = 128
D_FF = 4 * D_MODEL
EPS = 1e-6
SCALE = HEAD_DIM ** -0.5
NEG = float(jnp.finfo(jnp.float32).min)
N_DEV = 8
N_CHIP = 4

ADAM_LR = 0.001
ADAM_B1 = 0.9
ADAM_B2 = 0.999
ADAM_EPS = 1e-08
ADAM_WD = 0.01
ADAM_STEP = 10

PACK_W = 512
VMEM_LIMIT = 52 * 1024 * 1024
MM_TILE = 1024
SUB_BLOCKS = 4

LAYERED = ("w_in", "w_mem_kv", "w_out", "w_up", "w_down")
SMALL_NAMES = ("norm_mix", "pool_w", "kv_norm", "k_norm", "q_norm", "sinks", "mem_norm", "mem_q_norm", "mem_k_norm",
               "norm_mlp")


ANY = pl.BlockSpec(memory_space=pl.ANY)


def _params(*sem):
    return pltpu.CompilerParams(dimension_semantics=sem, vmem_limit_bytes=VMEM_LIMIT)


def _mm(a, b, *, name, ta=False, tb=False, b_kind=None, layer=0, res=None, relu2=False, mul2=None, out_dtype=F32,
        out_kind=None, out_buf=None):
    if ta:
        K, M = a.shape
    else:
        M, K = a.shape
    if b_kind is None:
        rows_b, cols_b = b.shape
    elif b_kind == "rows":
        rows_b, cols_b = b.shape[0] * b.shape[2], b.shape[3]
    else:
        rows_b, cols_b = b.shape[1:]
    N, K2 = (rows_b, cols_b) if tb else (cols_b, rows_b)
    assert K == K2, (a.shape, b.shape)
    tm = min(M, MM_TILE if K <= MM_TILE else MM_TILE // 2)
    tn = min(N, MM_TILE)
    assert M % tm == 0 and N % tn == 0
    row_tile, col_tile = (tn, K) if tb else (K, tn)
    a_spec = pl.BlockSpec((K, tm), lambda j, i: (0, i)) if ta else pl.BlockSpec((tm, K), lambda j, i: (i, 0))

    def rc(j):
        return (j, 0) if tb else (0, j)

    if b_kind is None:
        b_spec = pl.BlockSpec((row_tile, col_tile), lambda j, i: rc(j))
    elif b_kind == "rows":
        per = row_tile // b.shape[2]
        b_spec = pl.BlockSpec((per, None, b.shape[2], col_tile), lambda j, i: (rc(j)[0], layer, 0, rc(j)[1]))
    else:
        b_spec = pl.BlockSpec((None, row_tile, col_tile), lambda j, i: (layer, *rc(j)))
    o_spec = pl.BlockSpec((tm, tn), lambda j, i: (i, j))
    dn = (((0 if ta else 1,), (1 if tb else 0,)), ((), ()))
    extra = [e for e in (res, mul2) if e is not None]
    n_in = 2 + len(extra) + (1 if out_buf is not None else 0)

    def body(*refs):
        a_ref, b_ref = refs[0], refs[1]
        extra_refs = refs[2:2 + len(extra)]
        out = refs[n_in]
        bv = b_ref[...].astype(BF16).reshape(row_tile, col_tile)
        v = lax.dot_general(a_ref[...].astype(BF16), bv, dn, preferred_element_type=F32)
        if res is not None:
            v = extra_refs[0][...] + v
        elif mul2 is not None:
            v = v * (2.0 * jnp.sqrt(extra_refs[0][...].astype(F32)))
        if relu2:
            r = jnp.maximum(v, 0.0)
            v = r * r
        out[...] = v.astype(out.dtype).reshape(out.shape)

    in_specs = [a_spec, b_spec] + [o_spec] * len(extra)
    operands = [a, b, *extra]
    aliases = {}
    if out_kind is None:
        out_shape = jax.ShapeDtypeStruct((M, N), BF16 if relu2 else out_dtype)
        out_specs = o_spec
    else:
        if out_kind == "rows":
            s = out_buf.shape[2]
            out_specs = pl.BlockSpec((tm // s, None, s, tn), lambda j, i: (i, layer, 0, j))
        else:
            out_specs = pl.BlockSpec((None, tm, tn), lambda j, i: (layer, i, j))
        out_shape = jax.ShapeDtypeStruct(out_buf.shape, out_buf.dtype)
        in_specs.append(ANY)
        operands.append(out_buf)
        aliases = {len(operands) - 1: 0}
    return pl.pallas_call(
        body, name=name, grid=(N // tn, M // tm), in_specs=in_specs, out_specs=out_specs, out_shape=out_shape,
        input_output_aliases=aliases, compiler_params=_params("parallel", "parallel"),
    )(*operands)


def _weight_block(b, b_kind, transposed, tn):
    if b_kind == "rows":
        s = b.shape[2]
        rows, cols = b.shape[0] * s, b.shape[3]
        if transposed:
            return (lambda at: pl.BlockSpec((b.shape[0], None, s, cols), lambda *g: (0, 0, 0, 0))), rows, cols
        return (lambda at: pl.BlockSpec((b.shape[0], None, s, tn), lambda *g: (0, 0, 0, at(*g)))), rows, cols
    rows, cols = b.shape[1:]
    if transposed:
        return (lambda at: pl.BlockSpec((None, rows, cols), lambda *g: (0, 0, 0))), rows, cols
    return (lambda at: pl.BlockSpec((None, rows, tn), lambda *g: (0, 0, at(*g)))), rows, cols


def _norm_mm(x, gain, b, *, b_kind, name, relu2=False):
    M, K = x.shape
    tm = min(M, MM_TILE)
    spec_of, rows, N = _weight_block(b, b_kind, False, min(MM_TILE, b.shape[-1]))
    tn = min(N, MM_TILE)
    assert rows == K and M % tm == 0 and N % tn == 0

    sub = tm // SUB_BLOCKS if tm % (SUB_BLOCKS * 16) == 0 else tm

    def body(x_ref, g_ref, b_ref, xn_ref, o_ref):
        bv = b_ref[...].astype(BF16).reshape(K, tn)

        def product(xn):
            v = jnp.dot(xn, bv, preferred_element_type=F32)
            if relu2:
                r2 = jnp.maximum(v, 0.0)
                v = r2 * r2
            return v.astype(o_ref.dtype)

        @pl.when(pl.program_id(1) == 0)
        def _():
            for r0 in range(0, tm, sub):
                rows = pl.ds(r0, sub)
                xv = x_ref[rows, :]
                r = lax.rsqrt(jnp.mean(xv * xv, axis=-1, keepdims=True) + EPS)
                xn = ((xv * r) * g_ref[...]).astype(xn_ref.dtype)
                xn_ref[rows, :] = xn
                o_ref[rows, :] = product(xn)

        @pl.when(pl.program_id(1) > 0)
        def _():
            o_ref[...] = product(xn_ref[...])

    rows_spec = pl.BlockSpec((tm, K), lambda i, j: (i, 0))
    return pl.pallas_call(
        body, name=name, grid=(M // tm, N // tn),
        in_specs=[rows_spec, pl.BlockSpec((1, K), lambda i, j: (0, 0)), spec_of(lambda i, j: j)],
        out_specs=(rows_spec, pl.BlockSpec((tm, tn), lambda i, j: (i, j))),
        out_shape=(jax.ShapeDtypeStruct((M, K), BF16), jax.ShapeDtypeStruct((M, N), BF16 if relu2 else F32)),
        compiler_params=_params("parallel", "arbitrary"),
    )(x, gain.reshape(1, K), b)


def _mm_rms_bwd(a, b, x, gain, res, *, b_kind, name, also_bf16):
    M, K = a.shape
    spec_of, N, cols = _weight_block(b, b_kind, True, None)
    assert cols == K and x.shape == (M, N)
    tm = min(M, MM_TILE if K <= MM_TILE else MM_TILE // 2)
    assert M % tm == 0

    sub = tm // SUB_BLOCKS if tm % (SUB_BLOCKS * 16) == 0 else tm

    def body(a_ref, b_ref, x_ref, g_ref, res_ref, *outs):
        i = pl.program_id(0)
        bv = b_ref[...].astype(BF16).reshape(N, K)
        dg_ref = outs[-1]
        part = jnp.zeros((1, N), F32)
        for r0 in range(0, tm, sub):
            rows = pl.ds(r0, sub)
            dy = lax.dot_general(a_ref[rows, :].astype(BF16), bv, (((1,), (1,)), ((), ())), preferred_element_type=F32)
            xv = x_ref[rows, :]
            r = lax.rsqrt(jnp.mean(xv * xv, axis=-1, keepdims=True) + EPS)
            xh = xv * r
            part = part + jnp.sum(dy * xh, axis=0, keepdims=True)
            gdy = dy * g_ref[...]
            dx = res_ref[rows, :] + r * (gdy - xh * jnp.mean(gdy * xh, axis=-1, keepdims=True))
            outs[0][rows, :] = dx
            if also_bf16:
                outs[1][rows, :] = dx.astype(BF16)

        @pl.when(i == 0)
        def _():
            dg_ref[...] = part

        @pl.when(i > 0)
        def _():
            dg_ref[...] += part

    row = pl.BlockSpec((tm, N), lambda i: (i, 0))
    vec = pl.BlockSpec((1, N), lambda i: (0, 0))
    out_specs = [row] + ([row] if also_bf16 else []) + [vec]
    out_shape = [jax.ShapeDtypeStruct((M, N), F32)] + ([jax.ShapeDtypeStruct((M, N), BF16)] if also_bf16 else [])
    outs = pl.pallas_call(
        body, name=name, grid=(M // tm,),
        in_specs=[pl.BlockSpec((tm, K), lambda i: (i, 0)), spec_of(None), row, vec, row], out_specs=out_specs,
        out_shape=out_shape + [jax.ShapeDtypeStruct((1, N), F32)], compiler_params=_params("arbitrary"),
    )(a, b, x, gain.reshape(1, N), res)
    return (*outs[:-1], outs[-1].reshape(N))


def _row_tile(rows, d):
    t = min(rows, (512 * 1024) // d)
    while rows % t or (t != rows and t % 16):
        t -= 1
    return t


def _rms_fwd(x, g, *, name, out_dtype=BF16):
    R, D = x.shape
    tr = _row_tile(R, D)

    def body(x_ref, g_ref, o_ref):
        xv = x_ref[...].astype(F32)
        r = lax.rsqrt(jnp.mean(xv * xv, axis=-1, keepdims=True) + EPS)
        o_ref[...] = ((xv * r) * g_ref[...]).astype(o_ref.dtype)

    return pl.pallas_call(
        body, name=name, grid=(R // tr,),
        in_specs=[pl.BlockSpec((tr, D), lambda i: (i, 0)), pl.BlockSpec((1, D), lambda i: (0, 0))],
        out_specs=pl.BlockSpec((tr, D), lambda i: (i, 0)), out_shape=jax.ShapeDtypeStruct((R, D), out_dtype),
        compiler_params=_params("parallel"),
    )(x, g.reshape(1, D))


def _rms_bwd(x, g, dys, *, name, res=None, want_dx=True, also_bf16=False):
    R, D = x.shape
    tr = _row_tile(R, D)
    n_dy = len(dys)
    has_res = res is not None

    def body(*refs):
        x_ref, g_ref = refs[0], refs[1]
        dy_refs = refs[2:2 + n_dy]
        res_ref = refs[2 + n_dy] if has_res else None
        outs = refs[2 + n_dy + (1 if has_res else 0):]
        dg_ref = outs[-1]
        i = pl.program_id(0)
        xv = x_ref[...].astype(F32)
        dy = dy_refs[0][...].astype(F32)
        for extra in dy_refs[1:]:
            dy = dy + extra[...].astype(F32)
        r = lax.rsqrt(jnp.mean(xv * xv, axis=-1, keepdims=True) + EPS)
        xh = xv * r
        part = jnp.sum(dy * xh, axis=0, keepdims=True)

        @pl.when(i == 0)
        def _():
            dg_ref[...] = part

        @pl.when(i > 0)
        def _():
            dg_ref[...] += part

        if want_dx:
            gdy = dy * g_ref[...]
            dx = r * (gdy - xh * jnp.mean(gdy * xh, axis=-1, keepdims=True))
            if has_res:
                dx = res_ref[...] + dx
            outs[0][...] = dx
            if also_bf16:
                outs[1][...] = dx.astype(BF16)

    row = pl.BlockSpec((tr, D), lambda i: (i, 0))
    vec = pl.BlockSpec((1, D), lambda i: (0, 0))
    out_shape = [jax.ShapeDtypeStruct((1, D), F32)]
    out_specs = [vec]
    if also_bf16:
        out_shape = [jax.ShapeDtypeStruct((R, D), BF16)] + out_shape
        out_specs = [row] + out_specs
    if want_dx:
        out_shape = [jax.ShapeDtypeStruct((R, D), F32)] + out_shape
        out_specs = [row] + out_specs
    outs = pl.pallas_call(
        body, name=name, grid=(R // tr,),
        in_specs=[row, vec] + [row] * (n_dy + (1 if has_res else 0)), out_specs=out_specs, out_shape=out_shape,
        compiler_params=_params("arbitrary"),
    )(x, g.reshape(1, D), *dys, *([res] if has_res else []))
    return (*outs[:-1], outs[-1].reshape(D)) if want_dx else outs[0].reshape(D)


POOL_TILE = 512
MEM_Q_TILE = 512


def _softmax(q, k, bias, valid, sink):
    s = lax.dot_general(q, k, (((1,), (1,)), ((), ())), preferred_element_type=F32) * SCALE
    if bias is not None:
        s = s - bias
    if valid is not None:
        s = jnp.where(valid, s, NEG)
    m = jnp.max(s, axis=-1, keepdims=True)
    if sink is not None:
        m = jnp.maximum(m, sink)
    e = jnp.exp(s - m)
    z = jnp.sum(e, axis=-1, keepdims=True)
    if sink is None:
        return e * (1.0 / z), None
    es = jnp.exp(sink - m)
    inv = 1.0 / (z + es)
    return e * inv, es * inv


LANES = 128


def _seg_mean(v):
    r = lax.broadcasted_iota(jnp.int32, (LANES, LANES), 0) // HEAD_DIM
    c = lax.broadcasted_iota(jnp.int32, (LANES, LANES), 1) // HEAD_DIM
    seg = jnp.where(r == c, 1.0 / HEAD_DIM, 0.0).astype(BF16)
    hi = v.astype(BF16)
    lo = (v - hi.astype(F32)).astype(BF16)
    parts = []
    for g in range(v.shape[1] // LANES):
        sl = slice(g * LANES, (g + 1) * LANES)
        parts.append(jnp.dot(hi[:, sl], seg, preferred_element_type=F32) + jnp.dot(lo[:, sl], seg, preferred_element_type=F32))
    return parts[0] if len(parts) == 1 else jnp.concatenate(parts, axis=1)


def _cols(rows, width, col):
    return pl.BlockSpec((rows, width), lambda i: (i, col))


def _head_gain(g, heads):
    return jnp.tile(g, heads).reshape(1, heads * HEAD_DIM)


def _fold_heads(dg, heads):
    return dg.reshape(heads, HEAD_DIM).sum(axis=0)


def _seg_rms_fwd(x, gain, *, width, col, name):
    R = x.shape[0]
    tr = _row_tile(R, width)

    def body(x_ref, g_ref, o_ref):
        xv = x_ref[...]
        r = lax.rsqrt(_seg_mean(xv * xv) + EPS)
        o_ref[...] = ((xv * r) * g_ref[...]).astype(o_ref.dtype)

    return pl.pallas_call(
        body, name=name, grid=(R // tr,), in_specs=[_cols(tr, width, col), pl.BlockSpec((1, width), lambda i: (0, 0))],
        out_specs=_cols(tr, width, 0), out_shape=jax.ShapeDtypeStruct((R, width), BF16), compiler_params=_params("parallel"),
    )(x, gain)


def _seg_rms_bwd(x, gain, dys, *, width, col, name, out_buf=None, out_col=0):
    R = x.shape[0]
    tr = _row_tile(R, width)
    n_dy = len(dys)

    def body(*refs):
        x_ref, g_ref = refs[0], refs[1]
        dy_refs = refs[2:2 + n_dy]
        dx_ref, dg_ref = refs[-2], refs[-1]
        i = pl.program_id(0)
        xv = x_ref[...]
        dy = dy_refs[0][...]
        for extra in dy_refs[1:]:
            dy = dy + extra[...]
        r = lax.rsqrt(_seg_mean(xv * xv) + EPS)
        xh = xv * r
        part = jnp.sum(dy * xh, axis=0, keepdims=True)

        @pl.when(i == 0)
        def _():
            dg_ref[...] = part

        @pl.when(i > 0)
        def _():
            dg_ref[...] += part

        gdy = dy * g_ref[...]
        dx_ref[...] = (r * (gdy - xh * _seg_mean(gdy * xh))).astype(dx_ref.dtype)

    vec = pl.BlockSpec((1, width), lambda i: (0, 0))
    in_specs = [_cols(tr, width, col), vec] + [_cols(tr, width, 0)] * n_dy
    operands = [x, gain, *dys]
    aliases = {}
    dx_shape = jax.ShapeDtypeStruct((R, width), BF16)
    if out_buf is not None:
        in_specs.append(ANY)
        operands.append(out_buf)
        aliases = {len(operands) - 1: 0}
        dx_shape = jax.ShapeDtypeStruct(out_buf.shape, out_buf.dtype)
    return pl.pallas_call(
        body, name=name, grid=(R // tr,), in_specs=in_specs, out_specs=(_cols(tr, width, out_col), vec),
        out_shape=(dx_shape, jax.ShapeDtypeStruct((1, width), F32)), input_output_aliases=aliases,
        compiler_params=_params("arbitrary"),
    )(*operands)


def _sum_into(a, b, out_buf, out_col, *, name):
    R, width = a.shape
    tr = _row_tile(R, width)

    def body(a_ref, b_ref, _, o_ref):
        o_ref[...] = (a_ref[...] + b_ref[...]).astype(o_ref.dtype)

    return pl.pallas_call(
        body, name=name, grid=(R // tr,), in_specs=[_cols(tr, width, 0), _cols(tr, width, 0), ANY],
        out_specs=_cols(tr, width, out_col), out_shape=jax.ShapeDtypeStruct(out_buf.shape, out_buf.dtype),
        input_output_aliases={2: 0}, compiler_params=_params("parallel"),
    )(a, b, out_buf)


def _pool_lane_group():
    return lax.broadcasted_iota(jnp.int32, (1, MAIN_WIDTH), 1) // POOL_GROUP_DIM


def _pool_pick(group, per_window):
    s1, s2, s3, s4 = per_window
    return jnp.where(group == 0, s1, jnp.where(group == 1, s2, jnp.where(group == 2, s3, s4)))


def _pool_delta(u_ref, halo_ref, tile):
    group = _pool_lane_group()
    halo = jnp.where(tile == 0, 0.0, halo_ref[...])
    ext = jnp.concatenate([halo, u_ref[...]], axis=0)
    n = ext.shape[0]
    s1 = ext + pltpu.roll(ext, 1, 0)
    s2 = s1 + pltpu.roll(s1, 2, 0)
    s3 = s2 + pltpu.roll(s2, 4, 0)
    s4 = s3 + pltpu.roll(s3, 8, 0)
    ws = _pool_pick(group, (s1, s2, s3, s4))[POOL_HALO:n]
    t = tile * POOL_TILE + lax.broadcasted_iota(jnp.int32, (POOL_TILE, 1), 0)
    cnt = jnp.minimum(t + 1, _pool_pick(group, (2, 4, 8, 16))).astype(F32)
    return ws / cnt - u_ref[...], cnt


def _pool_in_specs():
    per_tile = POOL_TILE // POOL_HALO
    cur = _cols(POOL_TILE, MAIN_WIDTH, 0)
    prev = pl.BlockSpec((POOL_HALO, MAIN_WIDTH), lambda i: (jnp.maximum(i * per_tile - 1, 0), 0))
    mix = pl.BlockSpec((MAIN_WIDTH, MAIN_WIDTH), lambda i: (0, 0))
    vec = pl.BlockSpec((1, MAIN_WIDTH), lambda i: (0, 0))
    return cur, prev, mix, vec


def _pool_mix_fwd(proj, mix, scale, cat, *, name):
    T = proj.shape[0]
    assert T % POOL_TILE == 0
    cur, prev, mix_spec, vec = _pool_in_specs()

    def body(u_ref, halo_ref, mix_ref, sc_ref, _, o_ref):
        d, _cnt = _pool_delta(u_ref, halo_ref, pl.program_id(0))
        mixed = jnp.dot(d.astype(BF16), mix_ref[...].astype(BF16), preferred_element_type=F32)
        o_ref[...] = (mixed * sc_ref[...]).astype(o_ref.dtype)

    return pl.pallas_call(
        body, name=name, grid=(T // POOL_TILE,), in_specs=[cur, prev, mix_spec, vec, ANY], out_specs=cur,
        out_shape=jax.ShapeDtypeStruct(cat.shape, cat.dtype), input_output_aliases={4: 0}, compiler_params=_params("parallel"),
    )(proj, proj, mix, scale, cat)


def _pool_mix_bwd(proj, mix, scale, dcat, *, name):
    T = proj.shape[0]
    nt = T // POOL_TILE
    per_tile = POOL_TILE // POOL_HALO
    cur, prev, mix_spec, vec = _pool_in_specs()
    nxt = pl.BlockSpec((POOL_HALO, MAIN_WIDTH), lambda i: (jnp.minimum((i + 1) * per_tile, nt * per_tile - 1), 0))

    def body(u_ref, halo_ref, mix_ref, sc_ref, do_ref, donext_ref, du_ref, dmix_ref, dsc_ref):
        tile = pl.program_id(0)
        group = _pool_lane_group()
        d, cnt = _pool_delta(u_ref, halo_ref, tile)
        mixb = mix_ref[...].astype(BF16)
        db = d.astype(BF16)
        mixed = jnp.dot(db, mixb, preferred_element_type=F32)
        dout = do_ref[...]
        dsc = jnp.sum(dout * mixed, axis=0, keepdims=True)
        sc = sc_ref[...]
        dmixed = (dout * sc).astype(BF16)
        dmix = lax.dot_general(db, dmixed, (((0,), (0,)), ((), ())), preferred_element_type=F32)

        @pl.when(tile == 0)
        def _():
            dmix_ref[...] = dmix
            dsc_ref[...] = dsc

        @pl.when(tile > 0)
        def _():
            dmix_ref[...] += dmix
            dsc_ref[...] += dsc

        dnext = jnp.where(tile == nt - 1, 0.0, donext_ref[...])
        dmixed_ext = jnp.concatenate([dmixed, (dnext * sc).astype(BF16)], axis=0)
        dd_ext = lax.dot_general(dmixed_ext, mixb, (((1,), (1,)), ((), ())), preferred_element_type=F32)
        window = _pool_pick(group, (2.0, 4.0, 8.0, 16.0))
        cnt_ext = jnp.concatenate([cnt, jnp.broadcast_to(window, (POOL_HALO, MAIN_WIDTH))], axis=0)
        q = dd_ext / cnt_ext
        n = q.shape[0]
        r1 = q + pltpu.roll(q, n - 1, 0)
        r2 = r1 + pltpu.roll(r1, n - 2, 0)
        r3 = r2 + pltpu.roll(r2, n - 4, 0)
        r4 = r3 + pltpu.roll(r3, n - 8, 0)
        back = _pool_pick(group, (r1, r2, r3, r4))
        du_ref[...] = (back[0:POOL_TILE] - dd_ext[0:POOL_TILE]).astype(du_ref.dtype)

    return pl.pallas_call(
        body, name=name, grid=(nt,), in_specs=[cur, prev, mix_spec, vec, cur, nxt], out_specs=(cur, mix_spec, vec),
        out_shape=(jax.ShapeDtypeStruct((T, D_MODEL), BF16), jax.ShapeDtypeStruct((MAIN_WIDTH, MAIN_WIDTH), F32),
                   jax.ShapeDtypeStruct((1, MAIN_WIDTH), F32)),
        compiler_params=_params("arbitrary"),
    )(proj, proj, mix, scale, dcat, dcat)


def _head(a, h):
    return a[:, h * HEAD_DIM:(h + 1) * HEAD_DIM]


def _swa_mask(blk):
    rows = SWA_GROUP * BLOCK
    qi = lax.broadcasted_iota(jnp.int32, (rows, 2 * BLOCK), 0) % BLOCK
    kj = lax.broadcasted_iota(jnp.int32, (rows, 2 * BLOCK), 1)
    dist = qi + BLOCK - kj
    valid = (dist >= 0) & (dist < BLOCK) & ((blk > 0) | (kj >= BLOCK))
    return dist.astype(F32), valid


def _swa_head_terms(sink_ref, kvh, dist):
    grp = lax.broadcasted_iota(jnp.int32, (SWA_GROUP * BLOCK, 1), 0) // BLOCK
    slopes = [2.0 ** (-8.0 * (kvh * SWA_GROUP + g + 1) / SWA_Q_HEADS) for g in range(SWA_GROUP)]
    sinks = [sink_ref[kvh * SWA_GROUP + g] for g in range(SWA_GROUP)]
    slope = jnp.where(grp == 0, slopes[0], jnp.where(grp == 1, slopes[1], slopes[2]))
    sink = jnp.where(grp == 0, sinks[0], jnp.where(grp == 1, sinks[1], sinks[2]))
    return slope * dist, sink


def _stack_heads(a, kvh):
    return jnp.concatenate([_head(a, kvh * SWA_GROUP + g) for g in range(SWA_GROUP)], axis=0)


def _swa_specs(nb):
    def at(n):
        return jnp.minimum(n, nb - 1)

    q = pl.BlockSpec((BLOCK, MAIN_WIDTH), lambda n: (at(n), 0))
    k_prev = pl.BlockSpec((BLOCK, KV_HALF), lambda n: (jnp.maximum(at(n) - 1, 0), 0))
    k_cur = pl.BlockSpec((BLOCK, KV_HALF), lambda n: (at(n), 0))
    v_prev = pl.BlockSpec((BLOCK, KV_HALF), lambda n: (jnp.maximum(at(n) - 1, 0), 1))
    v_cur = pl.BlockSpec((BLOCK, KV_HALF), lambda n: (at(n), 1))
    return q, k_prev, k_cur, v_prev, v_cur


def _swa_attn_fwd(qn, kn, kv, sinks, cat, *, name):
    T = qn.shape[0]
    nb = T // BLOCK
    q_spec, k_prev, k_cur, v_prev, v_cur = _swa_specs(nb)

    def body(sink_ref, q_ref, kp_ref, kc_ref, vp_ref, vc_ref, _, o_ref):
        dist, valid = _swa_mask(pl.program_id(0))
        kk = jnp.concatenate([kp_ref[...], kc_ref[...]], axis=0)
        vv = jnp.concatenate([vp_ref[...], vc_ref[...]], axis=0).astype(BF16)
        q = q_ref[...]
        outs = []
        for kvh in range(SWA_KV_HEADS):
            bias, sink = _swa_head_terms(sink_ref, kvh, dist)
            p, _ps = _softmax(_stack_heads(q, kvh), _head(kk, kvh), bias, valid, sink)
            o = jnp.dot(p.astype(BF16), _head(vv, kvh), preferred_element_type=F32)
            outs += [o[g * BLOCK:(g + 1) * BLOCK] for g in range(SWA_GROUP)]
        o_ref[...] = jnp.concatenate(outs, axis=1).astype(o_ref.dtype)

    return pl.pallas_call(
        body, name=name, grid=(nb,),
        in_specs=[pl.BlockSpec(memory_space=pltpu.SMEM), q_spec, k_prev, k_cur, v_prev, v_cur, ANY], out_specs=q_spec,
        out_shape=jax.ShapeDtypeStruct(cat.shape, cat.dtype), input_output_aliases={6: 0}, compiler_params=_params("parallel"),
    )(sinks, qn, kn, kn, kv, kv, cat)


def _swa_attn_bwd(qn, kn, kv, sinks, dcat, dqn, *, name):
    T = qn.shape[0]
    nb = T // BLOCK
    q_spec, k_prev, k_cur, v_prev, v_cur = _swa_specs(nb)
    late = pl.BlockSpec((BLOCK, KV_HALF), lambda n: (jnp.maximum(n - 1, 0), 0))
    tn_dims = (((0,), (0,)), ((), ()))

    def body(sink_ref, q_ref, do_ref, kp_ref, kc_ref, vp_ref, vc_ref, _, dq_ref, dk_ref, dv_ref, ds_ref, ck, cv):
        blk = pl.program_id(0)

        @pl.when(blk == 0)
        def _():
            ck[...] = jnp.zeros_like(ck)
            cv[...] = jnp.zeros_like(cv)
            ds_ref[...] = jnp.zeros_like(ds_ref)

        @pl.when(blk < nb)
        def _():
            dist, valid = _swa_mask(blk)
            kk = jnp.concatenate([kp_ref[...], kc_ref[...]], axis=0)
            vv = jnp.concatenate([vp_ref[...], vc_ref[...]], axis=0).astype(BF16)
            q = q_ref[...]
            dout = do_ref[...].astype(BF16)
            lane = lax.broadcasted_iota(jnp.int32, (1, LANES), 1)
            dsinks = jnp.zeros((1, LANES), F32)
            dqs, dks, dvs = [], [], []
            for kvh in range(SWA_KV_HEADS):
                bias, sink = _swa_head_terms(sink_ref, kvh, dist)
                qq, kh, vh, dd = _stack_heads(q, kvh), _head(kk, kvh), _head(vv, kvh), _stack_heads(dout, kvh)
                p, ps = _softmax(qq, kh, bias, valid, sink)
                dp = lax.dot_general(dd, vh, (((1,), (1,)), ((), ())), preferred_element_type=F32)
                dsum = jnp.sum(p * dp, axis=-1, keepdims=True)
                ds = (p * (dp - dsum)).astype(BF16)
                dq = jnp.dot(ds, kh, preferred_element_type=F32) * SCALE
                dqs += [dq[g * BLOCK:(g + 1) * BLOCK] for g in range(SWA_GROUP)]
                dks.append(lax.dot_general(ds, qq, tn_dims, preferred_element_type=F32) * SCALE)
                dvs.append(lax.dot_general(p.astype(BF16), dd, tn_dims, preferred_element_type=F32))
                dsink = -(ps * dsum)
                for g in range(SWA_GROUP):
                    dsinks = dsinks + jnp.where(lane == kvh * SWA_GROUP + g, jnp.sum(dsink[g * BLOCK:(g + 1) * BLOCK]), 0.0)
            dq_ref[...] = jnp.concatenate(dqs, axis=1)
            dk = jnp.concatenate(dks, axis=1)
            dv = jnp.concatenate(dvs, axis=1)
            dk_ref[...] = ck[...] + dk[0:BLOCK]
            dv_ref[...] = cv[...] + dv[0:BLOCK]
            ck[...] = dk[BLOCK:2 * BLOCK]
            cv[...] = dv[BLOCK:2 * BLOCK]
            ds_ref[...] += dsinks

        @pl.when(blk == nb)
        def _():
            dk_ref[...] = ck[...]
            dv_ref[...] = cv[...]

    return pl.pallas_call(
        body, name=name, grid=(nb + 1,),
        in_specs=[pl.BlockSpec(memory_space=pltpu.SMEM), q_spec, q_spec, k_prev, k_cur, v_prev, v_cur, ANY],
        out_specs=(q_spec, late, late, pl.BlockSpec((1, LANES), lambda n: (0, 0))),
        out_shape=(jax.ShapeDtypeStruct(dqn.shape, dqn.dtype), jax.ShapeDtypeStruct((T, KV_HALF), F32),
                   jax.ShapeDtypeStruct((T, KV_HALF), F32), jax.ShapeDtypeStruct((1, LANES), F32)),
        scratch_shapes=[pltpu.VMEM((BLOCK, KV_HALF), F32), pltpu.VMEM((BLOCK, KV_HALF), F32)],
        input_output_aliases={7: 0}, compiler_params=_params("arbitrary"),
    )(sinks, qn, dcat, kn, kn, kv, kv, dqn)


def _mem_specs(M, tq, q_col):
    q = _cols(tq, MEM_WIDTH, q_col)
    k = pl.BlockSpec((M, MEM_WIDTH), lambda i: (0, 0))
    v = pl.BlockSpec((M, MEM_WIDTH), lambda i: (0, 1))
    return q, k, v


def _mem_attn_fwd(q, q_col, mkn, mkv, *, name):
    T = q.shape[0]
    M = mkn.shape[0]
    tq = min(T, MEM_Q_TILE)
    q_spec, k_spec, v_spec = _mem_specs(M, tq, q_col)

    def body(q_ref, k_ref, v_ref, o_ref):
        qq, kk, vv = q_ref[...], k_ref[...], v_ref[...].astype(BF16)
        outs = []
        for h in range(MEM_HEADS):
            p, _ps = _softmax(_head(qq, h), _head(kk, h), None, None, None)
            outs.append(jnp.dot(p.astype(BF16), _head(vv, h), preferred_element_type=F32))
        o_ref[...] = jnp.concatenate(outs, axis=1).astype(o_ref.dtype)

    return pl.pallas_call(
        body, name=name, grid=(T // tq,), in_specs=[q_spec, k_spec, v_spec], out_specs=_cols(tq, MEM_WIDTH, MAIN_WIDTH // MEM_WIDTH),
        out_shape=jax.ShapeDtypeStruct((T, D_MODEL), BF16), compiler_params=_params("parallel"),
    )(q, mkn, mkv)


def _mem_attn_bwd(q, q_col, mkn, mkv, dcat, *, dq_width, name):
    T = q.shape[0]
    M = mkn.shape[0]
    tq = min(T, MEM_Q_TILE)
    q_spec, k_spec, v_spec = _mem_specs(M, tq, q_col)
    last = MAIN_WIDTH // MEM_WIDTH
    tn_dims = (((0,), (0,)), ((), ()))

    def body(q_ref, do_ref, k_ref, v_ref, dq_ref, dk_ref, dv_ref):
        i = pl.program_id(0)
        qq, kk, vv, dout = q_ref[...], k_ref[...], v_ref[...].astype(BF16), do_ref[...].astype(BF16)
        dqs, dks, dvs = [], [], []
        for h in range(MEM_HEADS):
            qh, kh, vh, dh = _head(qq, h), _head(kk, h), _head(vv, h), _head(dout, h)
            p, _ps = _softmax(qh, kh, None, None, None)
            dp = lax.dot_general(dh, vh, (((1,), (1,)), ((), ())), preferred_element_type=F32)
            dsum = jnp.sum(p * dp, axis=-1, keepdims=True)
            ds = (p * (dp - dsum)).astype(BF16)
            dqs.append(jnp.dot(ds, kh, preferred_element_type=F32) * SCALE)
            dks.append(lax.dot_general(ds, qh, tn_dims, preferred_element_type=F32) * SCALE)
            dvs.append(lax.dot_general(p.astype(BF16), dh, tn_dims, preferred_element_type=F32))
        dq_ref[...] = jnp.concatenate(dqs, axis=1)
        dk = jnp.concatenate(dks, axis=1)
        dv = jnp.concatenate(dvs, axis=1)

        @pl.when(i == 0)
        def _():
            dk_ref[...] = dk
            dv_ref[...] = dv

        @pl.when(i > 0)
        def _():
            dk_ref[...] += dk
            dv_ref[...] += dv

    acc = pl.BlockSpec((M, MEM_WIDTH), lambda i: (0, 0))
    return pl.pallas_call(
        body, name=name, grid=(T // tq,), in_specs=[q_spec, _cols(tq, MEM_WIDTH, last), k_spec, v_spec],
        out_specs=(_cols(tq, MEM_WIDTH, dq_width // MEM_WIDTH - 1), acc, acc),
        out_shape=(jax.ShapeDtypeStruct((T, dq_width), F32), jax.ShapeDtypeStruct((M, MEM_WIDTH), F32),
                   jax.ShapeDtypeStruct((M, MEM_WIDTH), F32)),
        compiler_params=_params("arbitrary"),
    )(q, dcat, mkn, mkv)


def _loss(y, target, *, name):
    T, D = y.shape
    tr = _row_tile(T, D)

    def body(y_ref, t_ref, l_ref, dy_ref, dyb_ref):
        i = pl.program_id(0)
        err = y_ref[...] - t_ref[...]
        dy = err / float(D)
        dy_ref[...] = dy
        dyb_ref[...] = dy.astype(BF16)
        part = jnp.full((8, 128), 0.5 * jnp.sum(jnp.mean(err * err, axis=-1)), F32)

        @pl.when(i == 0)
        def _():
            l_ref[...] = part

        @pl.when(i > 0)
        def _():
            l_ref[...] += part

    row = pl.BlockSpec((tr, D), lambda i: (i, 0))
    return pl.pallas_call(
        body, name=name, grid=(T // tr,), in_specs=[row, row],
        out_specs=(pl.BlockSpec((8, 128), lambda i: (0, 0)), row, row),
        out_shape=(jax.ShapeDtypeStruct((8, 128), F32), jax.ShapeDtypeStruct((T, D), F32), jax.ShapeDtypeStruct((T, D), BF16)),
        compiler_params=_params("arbitrary"),
    )(y, target)


def _position():
    return lax.axis_index("x"), lax.axis_index("y"), lax.axis_index("c")


def _all_gather(arrays, *, name):
    n = len(arrays)

    def body(*refs):
        srcs, outs = refs[:n], refs[n:2 * n]
        token, send_sems, recv_sems, local_sems = refs[2 * n:]
        token[...] = jnp.zeros_like(token)
        x, y, c = _position()
        me, sibling = (x, y, c), (x, y, 1 - c)
        chips = [(1 - x, y), (x, 1 - y), (1 - x, 1 - y)]

        def slot(a, px, py, pc):
            return outs[a].at[4 * px + 2 * py + pc]

        def copy(a, k, block, to, src=None):
            return pltpu.make_async_remote_copy(
                src_ref=slot(a, *block) if src is None else src, dst_ref=slot(a, *block),
                send_sem=send_sems.at[a, k], recv_sem=recv_sems.at[a, k], device_id=to, device_id_type=MESH)

        mine = [pltpu.make_async_copy(srcs[a], slot(a, *me), local_sems.at[a]) for a in range(n)]
        for cp in mine:
            cp.start()
        first, passed = [], []
        for a in range(n):
            first.append(copy(a, 0, me, sibling, src=srcs[a]))
            first += [copy(a, 1 + j, me, (*chip, c), src=srcs[a]) for j, chip in enumerate(chips)]
        for cp in first:
            cp.start()
        for a in range(n):
            for j, chip in enumerate(chips):
                copy(a, 1 + j, (*chip, c), me).wait_recv()
                fwd = copy(a, 4 + j, (*chip, c), sibling)
                fwd.start()
                passed.append(fwd)
        for a in range(n):
            copy(a, 0, sibling, me).wait_recv()
            for j, chip in enumerate(chips):
                copy(a, 4 + j, (*chip, 1 - c), me).wait_recv()
        for cp in first + passed:
            cp.wait_send()
        for cp in mine:
            cp.wait()

    return pl.pallas_call(
        body, name=name, in_specs=[ANY] * n, out_specs=[ANY] * n + [pl.BlockSpec(memory_space=pltpu.VMEM)],
        out_shape=[jax.ShapeDtypeStruct((N_DEV,) + a.shape, a.dtype) for a in arrays] + [jax.ShapeDtypeStruct((8, 128), F32)],
        scratch_shapes=[pltpu.SemaphoreType.DMA((n, 7)), pltpu.SemaphoreType.DMA((n, 7)), pltpu.SemaphoreType.DMA((n,))],
    )(*arrays)


def _sibling_exchange(by_core, whole, *, name):
    n1, n = len(by_core), len(by_core) + len(whole)

    def body(*refs):
        srcs, outs = refs[:n], refs[n:2 * n]
        send_sems, recv_sems = refs[2 * n:]
        x, y, c = _position()
        copies = [
            pltpu.make_async_remote_copy(src_ref=srcs[a].at[:, 1 - c] if a < n1 else srcs[a], dst_ref=outs[a],
                                         send_sem=send_sems.at[a], recv_sem=recv_sems.at[a], device_id=(x, y, 1 - c),
                                         device_id_type=MESH)
            for a in range(n)]
        for cp in copies:
            cp.start()
        for cp in copies:
            cp.wait()

    out_shape = [jax.ShapeDtypeStruct(a.shape[:1] + a.shape[2:], a.dtype) for a in by_core]
    out_shape += [jax.ShapeDtypeStruct(a.shape, a.dtype) for a in whole]
    outs = pl.pallas_call(
        body, name=name, in_specs=[ANY] * n, out_specs=[ANY] * n, out_shape=out_shape,
        scratch_shapes=[pltpu.SemaphoreType.DMA((n,)), pltpu.SemaphoreType.DMA((n,))],
    )(*by_core, *whole)
    return outs[:n1], outs[n1:]


HBM = pl.BlockSpec(memory_space=pltpu.HBM)
SEM = pl.BlockSpec(memory_space=pltpu.SEMAPHORE)
DATAFLOW = pltpu.SideEffectType.DATAFLOW_SIDE_EFFECTING


def _device(flat):
    return flat // 4, (flat // 2) % 2, flat % 2


def _gather_copies(srcs, lands, send_sems, recv_sems, incoming):
    x, y, c = _position()
    me = 4 * x + 2 * y + c
    pairs = []
    for a in range(len(srcs)):
        for d in range(1, N_DEV):
            to, frm = (me + d) % N_DEV, (me + N_DEV - d) % N_DEV
            k = a * (N_DEV - 1) + d - 1
            sems = dict(send_sem=send_sems.at[k], recv_sem=recv_sems.at[k], device_id_type=MESH)
            out = pltpu.make_async_remote_copy(src_ref=srcs[a], dst_ref=lands[a].at[me], device_id=_device(to), **sems)
            inc = pltpu.make_async_remote_copy(src_ref=srcs[a], dst_ref=lands[a].at[frm], device_id=_device(frm),
                                               **sems) if incoming else None
            pairs.append((out, inc))
    return pairs


def _chip_copies(srcs, lands, send_sems, recv_sems, incoming, n_whole=0):
    x, y, c = _position()
    my_chip = 2 * x + y
    pairs = []
    for a in range(len(srcs)):
        for k, (px, py) in enumerate([(1 - x, y), (x, 1 - y), (1 - x, 1 - y)]):
            sem = a * (N_CHIP - 1) + k
            sems = dict(send_sem=send_sems.at[sem], recv_sem=recv_sems.at[sem], device_id=(px, py, c), device_id_type=MESH)
            src = srcs[a] if a >= len(srcs) - n_whole else srcs[a].at[2 * px + py]
            out = pltpu.make_async_remote_copy(src_ref=src, dst_ref=lands[a].at[my_chip], **sems)
            inc = pltpu.make_async_remote_copy(src_ref=src, dst_ref=lands[a].at[2 * px + py], **sems) if incoming else None
            pairs.append((out, inc))
    return pairs


def _push_start(copies, fan, srcs, lands, *, name):
    n = len(srcs)

    def body(*refs):
        src_refs, land_refs = refs[:n], refs[n:2 * n]
        send_sems, recv_sems = refs[2 * n], refs[2 * n + 1]
        token = refs[-1]
        for out, _ in copies(src_refs, land_refs, send_sems, recv_sems, False):
            out.start()
        token[...] = jnp.zeros_like(token)

    outs = pl.pallas_call(
        body, name=name,
        out_shape=(pltpu.SemaphoreType.DMA((n * fan,)), pltpu.SemaphoreType.DMA((n * fan,)),
                   *(pltpu.HBM(a.shape, a.dtype) for a in srcs), *(pltpu.HBM(a.shape, a.dtype) for a in lands),
                   jax.ShapeDtypeStruct((8, 128), F32)),
        in_specs=[HBM] * (2 * n), out_specs=(SEM, SEM, *([HBM] * (2 * n)), pl.BlockSpec(memory_space=pltpu.VMEM)),
        input_output_aliases={i: 2 + i for i in range(2 * n)},
        compiler_params=pltpu.CompilerParams(has_side_effects=DATAFLOW),
    )(*(pltpu.with_memory_space_constraint(a, pltpu.HBM) for a in (*srcs, *lands)))
    return outs[0], outs[1], list(outs[2:2 + n]), list(outs[2 + n:2 + 2 * n]), outs[-1]


def _push_wait(copies, send_sems, recv_sems, srcs, lands, after, *, name):
    n = len(srcs)

    def body(*refs):
        src_refs, land_refs = refs[:n], refs[n:2 * n]
        for out, inc in copies(src_refs, land_refs, refs[2 * n], refs[2 * n + 1], True):
            out.wait_send()
            inc.wait_recv()
        refs[-1][...] = jnp.zeros_like(refs[-1])

    outs = pl.pallas_call(
        body, name=name,
        out_shape=(*(pltpu.HBM(a.shape, a.dtype) for a in (*srcs, *lands)), jax.ShapeDtypeStruct((8, 128), F32)),
        in_specs=[HBM] * (2 * n) + [SEM, SEM, ANY], out_specs=(*([HBM] * (2 * n)), pl.BlockSpec(memory_space=pltpu.VMEM)),
        input_output_aliases={i: i for i in range(2 * n)},
        compiler_params=pltpu.CompilerParams(has_side_effects=DATAFLOW),
    )(*srcs, *lands, send_sems, recv_sems, after)
    return list(outs[n:2 * n]), outs[-1]


def _with_own_slot(block, index, slots):
    buf = lax.empty((slots,) + block.shape, block.dtype)
    return lax.dynamic_update_slice(buf, block[None], (index,) + (0,) * block.ndim)


def _view2d(shape):
    return math.prod(shape[:-1]), shape[-1]


def _pair_sum(mine, other, core, *, name, out_dtype):
    by_core = mine.ndim == 4
    n, w = other.shape[-2:]
    tr = _row_tile(n, w * 2)
    lead = other.shape[0] if by_core else 1

    def body(core_ref, a_ref, b_ref, o_ref):
        o_ref[...] = (a_ref[...].astype(F32) + b_ref[...].astype(F32)).astype(o_ref.dtype)

    if by_core:
        a_spec = pl.BlockSpec((None, None, tr, w), lambda j, i, core_ref: (j, core_ref[0], i, 0))
        o_spec = pl.BlockSpec((None, tr, w), lambda j, i, core_ref: (j, i, 0))
    else:
        a_spec = o_spec = pl.BlockSpec((tr, w), lambda j, i, core_ref: (i, 0))
    grid_spec = pltpu.PrefetchScalarGridSpec(num_scalar_prefetch=1, grid=(lead, n // tr), in_specs=[a_spec, o_spec],
                                             out_specs=o_spec)
    return pl.pallas_call(body, name=name, grid_spec=grid_spec, out_shape=jax.ShapeDtypeStruct(other.shape, out_dtype),
                          compiler_params=_params("parallel", "parallel"))(core.reshape(1), mine, other)


def _adamw(parts, w, m, v, *, name):
    layers = len(parts)
    n_parts, R, W = parts[0].shape
    tr = _row_tile(R, W * 2)
    per_layer = R // tr

    def update(p_ref, w_ref, m_ref, v_ref, g_out, d_out, m_out, v_out):
        g = p_ref[0].astype(F32)
        for j in range(1, n_parts):
            g = g + p_ref[j].astype(F32)
        m_new = ADAM_B1 * m_ref[...] + (1.0 - ADAM_B1) * g
        v_new = ADAM_B2 * v_ref[...] + (1.0 - ADAM_B2) * (g * g)
        m_hat = m_new / (1.0 - ADAM_B1 ** ADAM_STEP)
        v_hat = v_new / (1.0 - ADAM_B2 ** ADAM_STEP)
        g_out[...] = g
        d_out[...] = -ADAM_LR * (m_hat / (jnp.sqrt(v_hat) + ADAM_EPS) + ADAM_WD * w_ref[...])
        m_out[...] = m_new
        v_out[...] = v_new

    def body(*refs):
        for k in range(layers):
            pl.when(pl.program_id(0) == k)(lambda k=k: update(refs[k], *refs[layers:]))

    def parts_spec(k):
        return pl.BlockSpec((n_parts, tr, W), lambda l, i: (0, jnp.where(l == k, i, 0), 0))

    row = pl.BlockSpec((tr, W), lambda l, i: (l * per_layer + i, 0))
    out = jax.ShapeDtypeStruct((layers * R, W), F32)
    return pl.pallas_call(
        body, name=name, grid=(layers, per_layer), in_specs=[parts_spec(k) for k in range(layers)] + [row, row, row],
        out_specs=(row, row, row, row), out_shape=(out, out, out, out), compiler_params=_params("arbitrary", "arbitrary"),
    )(*parts, w, m, v)


SMALL_ROWS = 608


def _pack_small(p):
    flat = jnp.concatenate([p[n].reshape(-1).astype(F32) for n in SMALL_NAMES])
    return jnp.pad(flat, (0, SMALL_ROWS * PACK_W - flat.shape[0])).reshape(SMALL_ROWS, PACK_W)


def _unpack_small(buf, like):
    out, at = {}, 0
    flat = buf.reshape(-1)
    for n in SMALL_NAMES:
        size = math.prod(like[n].shape)
        out[n] = flat[at:at + size].reshape(like[n].shape)
        at += size
    return out


def _block_diag(pw):
    out = jnp.zeros((MAIN_WIDTH, MAIN_WIDTH), pw.dtype)
    for g in range(POOL_GROUPS):
        out = lax.dynamic_update_slice(out, pw[g], (g * POOL_GROUP_DIM, g * POOL_GROUP_DIM))
    return out


def _diag_blocks(m):
    return jnp.stack([m[g * POOL_GROUP_DIM:(g + 1) * POOL_GROUP_DIM, g * POOL_GROUP_DIM:(g + 1) * POOL_GROUP_DIM]
                      for g in range(POOL_GROUPS)])


def _train_pass(x, mem, target, p, w_kv, fetch, reduce_layer, reduce_wait):
    T = x.shape[0]
    mem_cols = MAIN_WIDTH // MEM_WIDTH
    k_gain = _head_gain(p["k_norm"], SWA_KV_HEADS)
    saved = []
    h = x
    kn = kv = h_kv = hn_kv = None
    for l in range(DEPTH):
        s = {}
        wl, token = fetch(l, h)
        s["w"] = wl
        if l == N_A:
            h_kv = h
            hn_kv, kv = _norm_mm(h, p["kv_norm"], w_kv, b_kind="rows", name="kv_proj")
            kn = _seg_rms_fwd(kv, k_gain, width=KV_HALF, col=0, name="k_norm_fwd")
        s["h"] = h
        s["xn1"], proj = _norm_mm(h, p["norm_mix"][l] + token, wl["w_in"], b_kind="rows", name="in_proj")
        s["proj"] = proj
        s["memn"] = _rms_fwd(mem, p["mem_norm"][l], name="mem_norm_fwd")
        s["mkv"] = _mm(s["memn"], wl["w_mem_kv"], b_kind="rows", name="mem_kv_proj")
        s["mk_gain"] = _head_gain(p["mem_k_norm"][l], MEM_HEADS)
        s["mkn"] = _seg_rms_fwd(s["mkv"], s["mk_gain"], width=MEM_WIDTH, col=0, name="mem_k_norm_fwd")
        if l < N_A:
            s["q_gain"] = _head_gain(p["mem_q_norm"][l], MEM_HEADS)
            s["qn"] = _seg_rms_fwd(proj, s["q_gain"], width=MEM_WIDTH, col=mem_cols, name="mem_q_norm_fwd")
            s["q_col"] = 0
        else:
            j = l - N_A
            s["q_gain"] = jnp.concatenate([_head_gain(p["q_norm"][j], SWA_Q_HEADS), _head_gain(p["mem_q_norm"][l], MEM_HEADS)],
                                          axis=1)
            s["qn"] = _seg_rms_fwd(proj, s["q_gain"], width=D_MODEL, col=0, name="q_norm_fwd")
            s["q_col"] = mem_cols
        cat = _mem_attn_fwd(s["qn"], s["q_col"], s["mkn"], s["mkv"], name="mem_attn_fwd")
        if l < N_A:
            s["mix"] = _block_diag(p["pool_w"][l])
            s["scale"] = p["pool_scale"][l].reshape(1, MAIN_WIDTH)
            s["cat"] = _pool_mix_fwd(proj, s["mix"], s["scale"], cat, name="pool_fwd")
        else:
            s["cat"] = _swa_attn_fwd(s["qn"], kn, kv, p["sinks"][l - N_A], cat, name="swa_fwd")
        s["h1"] = _mm(s["cat"], wl["w_out"], b_kind="rows", res=h, name="out_proj")
        s["xn2"], s["a"] = _norm_mm(s["h1"], p["norm_mlp"][l], wl["w_up"], b_kind="layers", relu2=True, name="mlp_up")
        h = _mm(s["a"], wl["w_down"], b_kind="rows", res=s["h1"], name="mlp_down")
        saved.append(s)

    loss, dh, dh_b = _loss(h, target, name="loss_head")

    g = {n: [None] * DEPTH for n in ("norm_mix", "mem_norm", "mem_q_norm", "mem_k_norm", "norm_mlp")}
    g.update({n: [None] * N_A for n in ("pool_w", "pool_scale", "q_norm", "sinks")})
    g_kv = None
    token = None
    dks, dvs = [], []
    for l in reversed(range(DEPTH)):
        s = saved[l]
        wl = s["w"]
        gb = {}

        def dw(a, dy, n):
            return _mm(a, dy, ta=True, out_kind="layers" if n == "w_up" else "rows", out_buf=lax.empty(wl[n].shape, BF16),
                       name=n + "_grad")

        norm_mlp_gain = p["norm_mlp"][l] if token is None else p["norm_mlp"][l] + token
        gb["w_down"] = dw(s["a"], dh_b, "w_down")
        du = _mm(dh_b, wl["w_down"], tb=True, b_kind="rows", mul2=s["a"], out_dtype=BF16, name="mlp_down_dx")
        gb["w_up"] = dw(s["xn2"], du, "w_up")
        early = reduce_layer(l, gb, early=True)
        if early is not None:
            norm_mlp_gain = norm_mlp_gain + early
        dh1, dh1_b, g["norm_mlp"][l] = _mm_rms_bwd(du, wl["w_up"], s["h1"], norm_mlp_gain, dh, b_kind="layers", also_bf16=True,
                                                   name="mlp_up_dx")
        gb["w_out"] = dw(s["cat"], dh1_b, "w_out")
        dcat = _mm(dh1_b, wl["w_out"], tb=True, b_kind="rows", name="out_proj_dx")
        if l < N_A:
            dq, dmk, dmv = _mem_attn_bwd(s["qn"], s["q_col"], s["mkn"], s["mkv"], dcat, dq_width=MEM_WIDTH, name="mem_attn_bwd")
            dproj, dmix, dscale = _pool_mix_bwd(s["proj"], s["mix"], s["scale"], dcat, name="pool_bwd")
            g["pool_w"][l] = _diag_blocks(dmix)
            g["pool_scale"][l] = dscale.reshape(MAIN_WIDTH)
            dproj, dgain = _seg_rms_bwd(s["proj"], s["q_gain"], [dq], width=MEM_WIDTH, col=mem_cols, out_buf=dproj,
                                        out_col=mem_cols, name="mem_q_norm_bwd")
            g["mem_q_norm"][l] = _fold_heads(dgain, MEM_HEADS)
        else:
            j = l - N_A
            dqn, dmk, dmv = _mem_attn_bwd(s["qn"], s["q_col"], s["mkn"], s["mkv"], dcat, dq_width=D_MODEL, name="mem_attn_bwd")
            dqn, dk_l, dv_l, dsinks = _swa_attn_bwd(s["qn"], kn, kv, p["sinks"][j], dcat, dqn, name="swa_bwd")
            dks.append(dk_l)
            dvs.append(dv_l)
            g["sinks"][j] = dsinks[0, :SWA_Q_HEADS]
            dproj, dgain = _seg_rms_bwd(s["proj"], s["q_gain"], [dqn], width=D_MODEL, col=0, name="q_norm_bwd")
            g["q_norm"][j] = _fold_heads(dgain[:, :MAIN_WIDTH], SWA_Q_HEADS)
            g["mem_q_norm"][l] = _fold_heads(dgain[:, MAIN_WIDTH:], MEM_HEADS)
        dmk_raw, dgain = _seg_rms_bwd(s["mkv"], s["mk_gain"], [dmk], width=MEM_WIDTH, col=0, name="mem_k_norm_bwd")
        g["mem_k_norm"][l] = _fold_heads(dgain, MEM_HEADS)
        dmkv = jnp.concatenate([dmk_raw, dmv.astype(BF16)], axis=1)
        gb["w_mem_kv"] = dw(s["memn"], dmkv, "w_mem_kv")
        dmemn = _mm(dmkv, wl["w_mem_kv"], tb=True, b_kind="rows", name="mem_kv_proj_dx")
        g["mem_norm"][l] = _rms_bwd(mem, p["mem_norm"][l], [dmemn], want_dx=False, name="mem_norm_bwd")
        gb["w_in"] = dw(s["xn1"], dproj, "w_in")
        if l in (0, N_A):
            dh, g["norm_mix"][l] = _mm_rms_bwd(dproj, wl["w_in"], s["h"], p["norm_mix"][l], dh1, b_kind="rows", also_bf16=False,
                                               name="in_proj_dx")
        else:
            dh, dh_b, g["norm_mix"][l] = _mm_rms_bwd(dproj, wl["w_in"], s["h"], p["norm_mix"][l], dh1, b_kind="rows",
                                                     also_bf16=True, name="in_proj_dx")
        if l == N_A:
            dkv, dgain = _seg_rms_bwd(kv, k_gain, dks, width=KV_HALF, col=0, out_buf=lax.empty((T, 2 * KV_HALF), BF16),
                                      name="k_norm_bwd")
            g["k_norm"] = _fold_heads(dgain, SWA_KV_HEADS)
            dkv = _sum_into(dvs[0], dvs[1], dkv, 1, name="dv_sum")
            g_kv = _mm(hn_kv, dkv, ta=True, out_kind="rows", out_buf=lax.empty(w_kv.shape, BF16), name="w_kv_grad")
            dh, dh_b, g["kv_norm"] = _mm_rms_bwd(dkv, w_kv, h_kv, p["kv_norm"], dh, b_kind="rows", also_bf16=True,
                                                 name="kv_proj_dx")
        if l + 1 < DEPTH:
            reduce_wait(l + 1, dh)
        token = reduce_layer(l, gb)
    grads = {n: (jnp.stack(v) if isinstance(v, list) else v) for n, v in g.items()}
    return loss, dh, grads, g_kv


def kernel(x, mem, norm_mix, w_in, pool_w, pool_scale, kv_norm, w_kv, k_norm, q_norm, sinks, mem_norm, w_mem_kv, mem_q_norm, mem_k_norm, w_out, norm_mlp, w_up, w_down, loss_target, m_norm_mix, m_w_in, m_pool_w, m_pool_scale, m_kv_norm, m_w_kv, m_k_norm, m_q_norm, m_sinks, m_mem_norm, m_w_mem_kv, m_mem_q_norm, m_mem_k_norm, m_w_out, m_norm_mlp, m_w_up, m_w_down, v_norm_mix, v_w_in, v_pool_w, v_pool_scale, v_kv_norm, v_w_kv, v_k_norm, v_q_norm, v_sinks, v_mem_norm, v_w_mem_kv, v_mem_q_norm, v_mem_k_norm, v_w_out, v_norm_mlp, v_w_up, v_w_down):
    weights = dict(norm_mix=norm_mix, w_in=w_in, pool_w=pool_w, pool_scale=pool_scale, kv_norm=kv_norm, w_kv=w_kv,
                   k_norm=k_norm, q_norm=q_norm, sinks=sinks, mem_norm=mem_norm, w_mem_kv=w_mem_kv,
                   mem_q_norm=mem_q_norm, mem_k_norm=mem_k_norm, w_out=w_out, norm_mlp=norm_mlp, w_up=w_up, w_down=w_down)
    mom1 = dict(norm_mix=m_norm_mix, w_in=m_w_in, pool_w=m_pool_w, pool_scale=m_pool_scale, kv_norm=m_kv_norm, w_kv=m_w_kv,
                k_norm=m_k_norm, q_norm=m_q_norm, sinks=m_sinks, mem_norm=m_mem_norm, w_mem_kv=m_w_mem_kv,
                mem_q_norm=m_mem_q_norm, mem_k_norm=m_mem_k_norm, w_out=m_w_out, norm_mlp=m_norm_mlp, w_up=m_w_up,
                w_down=m_w_down)
    mom2 = dict(norm_mix=v_norm_mix, w_in=v_w_in, pool_w=v_pool_w, pool_scale=v_pool_scale, kv_norm=v_kv_norm, w_kv=v_w_kv,
                k_norm=v_k_norm, q_norm=v_q_norm, sinks=v_sinks, mem_norm=v_mem_norm, w_mem_kv=v_w_mem_kv,
                mem_q_norm=v_mem_q_norm, mem_k_norm=v_mem_k_norm, w_out=v_w_out, norm_mlp=v_norm_mlp, w_up=v_w_up,
                w_down=v_w_down)
    names = list(weights)
    x_pos, y_pos, core = (lax.axis_index(n).astype(jnp.int32) for n in AXES)
    me, my_chip = 4 * x_pos + 2 * y_pos + core, 2 * x_pos + y_pos
    shard = MAIN_WIDTH // N_DEV

    def layer_shards(l, zero=0.0):
        return [(weights[n][l:l + 1] + zero).astype(BF16) for n in LAYERED]

    def usable(arrays):
        wl = dict(zip(LAYERED, arrays))
        wl["w_up"] = wl["w_up"].transpose(1, 2, 0, 3).reshape(1, D_MODEL, D_FF)
        return wl

    scale_block = jnp.pad(pool_scale, ((0, 8 - N_A), (0, 128 - shard)))
    *first, first_done = _all_gather(layer_shards(0) + [w_kv[None].astype(BF16), scale_block], name="gather_first")
    p = {n: weights[n] for n in SMALL_NAMES}
    p["pool_scale"] = first[-1][:, :N_A, :shard].transpose(1, 0, 2).reshape(N_A, MAIN_WIDTH)
    gathers, reduces, parts = {}, {}, {}

    def fetch(l, after):
        if l == 0:
            got, done = first[:len(LAYERED)], first_done
        else:
            got, done = _push_wait(_gather_copies, *gathers.pop(l), after, name=f"gather_wait_{l}")
        token = 0.0
        if l + 1 < DEPTH:
            srcs = layer_shards(l + 1, done[0, 0])
            lands = [_with_own_slot(a, me, N_DEV) for a in srcs]
            *handles, block = _push_start(_gather_copies, N_DEV - 1, srcs, lands, name=f"gather_start_{l + 1}")
            gathers[l + 1], token = handles, block[0, 0]
        return usable(got), token

    def by_core(gb):
        gb = dict(gb)
        if "w_up" in gb:
            gb["w_up"] = gb["w_up"].reshape(D_MODEL, N_DEV, D_FF // N_DEV).transpose(1, 0, 2)
        order = [n for n in LAYERED if n in gb] + [n for n in gb if n not in LAYERED]
        return {n: gb[n].reshape((N_CHIP, 2) + _view2d(gb[n].shape[1:] if n == "w_up" else gb[n].shape[2:])) for n in order}

    def pair_sums(views, sib, tag):
        return [_pair_sum(a, b, core, name=f"chip_sum_{n}_{tag}", out_dtype=BF16) for (n, a), b in zip(views.items(), sib)]

    def chip_sums(gb, tag, whole=()):
        views = by_core(gb)
        sib, sib_whole = _sibling_exchange(list(views.values()), list(whole), name="reduce_sibling_" + tag)
        return pair_sums(views, sib, tag), sib_whole

    def start_chip_exchange(sums, tag, whole=()):
        lands = [_with_own_slot(lax.dynamic_index_in_dim(a, my_chip, 0, keepdims=False), my_chip, N_CHIP) for a in sums]
        lands += [_with_own_slot(a, my_chip, N_CHIP) for a in whole]
        copies = functools.partial(_chip_copies, n_whole=len(whole))
        *handles, block = _push_start(copies, N_CHIP - 1, [*sums, *whole], lands, name="reduce_start_" + tag)
        return (copies, *handles), block[0, 0]

    mlp = ("w_up", "w_down")

    def reduce_layer(l, gb, early=False):
        if early and l > 0:
            return None
        if l == 0 and not early:
            reduces["rest"] = {n: a for n, a in gb.items() if n not in mlp}
            return None
        tag = "0_mlp" if early else str(l)
        sums, _ = chip_sums({n: gb[n] for n in mlp} if early else gb, tag)
        reduces[l], token = start_chip_exchange(sums, tag)
        return token

    def reduce_wait(l, after):
        copies, *handles = reduces.pop(l)
        return _push_wait(copies, *handles, after, name=f"reduce_wait_{l}")[0]

    def layer_wait(l, after):
        parts[l] = reduce_wait(l, after)

    loss, grad_x, grads, g_kv = _train_pass(x[0], mem[0], loss_target[0], p, first[len(LAYERED)], fetch, reduce_layer, layer_wait)

    last = dict(reduces.pop("rest"))
    last["w_kv"] = g_kv
    last["pool_scale"] = grads["pool_scale"].reshape(N_A, N_DEV, shard).transpose(1, 0, 2).astype(BF16)[:, None]
    small = _pack_small(grads)
    sums, (sib_small,) = chip_sums(last, "0", whole=[small])
    chip_small = _pair_sum(small, sib_small, core, name="chip_sum_small", out_dtype=F32)
    reduces["rest"], _ = start_chip_exchange(sums, "0_rest", whole=[chip_small])

    def adamw(n, n_parts):
        res = _adamw(n_parts, *(d[n].reshape(_view2d(d[n].shape)) for d in (weights, mom1, mom2)), name="adamw_" + n)
        return [r.reshape(weights[n].shape) for r in res]

    p_up, p_down = reduce_wait(0, chip_small)
    parts[0] = [None, None, None, p_up, p_down]
    new = {n: adamw(n, [parts[l][LAYERED.index(n)] for l in range(DEPTH)]) for n in mlp}
    p_in, p_mem_kv, p_out, parts_kv, parts_scale, parts_small = reduce_wait("rest", new["w_down"][0])
    parts[0][:3] = [p_in, p_mem_kv, p_out]
    new.update({n: adamw(n, [parts[l][k] for l in range(DEPTH)]) for k, n in enumerate(LAYERED) if n not in mlp})
    new["w_kv"] = adamw("w_kv", [parts_kv])
    new["pool_scale"] = adamw("pool_scale", [parts_scale])
    res = _adamw([parts_small], _pack_small(weights), _pack_small(mom1), _pack_small(mom2), name="adamw_replicated")
    for n, vals in zip(SMALL_NAMES, zip(*(_unpack_small(r, weights).values() for r in res))):
        new[n] = list(vals)
    outs = [new[n][k] for k in range(4) for n in names]
    total = lax.psum(loss[0, 0], AXES)
    return (total, grad_x[None], *outs)
```

```python
import functools
import math

import jax
import jax.numpy as jnp
from jax import lax
from jax.experimental import pallas as pl
from jax.experimental.pallas import tpu as pltpu

F32 = jnp.float32
BF16 = jnp.bfloat16
MESH = pl.DeviceIdType.MESH
AXES = ("x", "y", "c")

D_MODEL = 1024
DEPTH = 4
N_A = 2
HEAD_DIM = 64
MEM_HEADS = 4
MEM_WIDTH = MEM_HEADS * HEAD_DIM
MAIN_WIDTH = D_MODEL - MEM_WIDTH
POOL_GROUPS = 4
POOL_GROUP_DIM = MAIN_WIDTH // POOL_GROUPS
POOL_HALO = 16
SWA_Q_HEADS = MAIN_WIDTH // HEAD_DIM
SWA_KV_HEADS = 4
SWA_GROUP = SWA_Q_HEADS // SWA_KV_HEADS
KV_HALF = SWA_KV_HEADS * HEAD_DIM
BLOCK = 128
D_FF = 4 * D_MODEL
EPS = 1e-6
SCALE = HEAD_DIM ** -0.5
NEG = float(jnp.finfo(jnp.float32).min)
N_DEV = 8
N_CHIP = 4

ADAM_LR = 0.001
ADAM_B1 = 0.9
ADAM_B2 = 0.999
ADAM_EPS = 1e-08
ADAM_WD = 0.01
ADAM_STEP = 10

PACK_W = 512
VMEM_LIMIT = 52 * 1024 * 1024
MM_TILE = 1024
LAYERED = ("w_in", "w_mem_kv", "w_out", "w_up", "w_down")
SMALL_NAMES = ("norm_mix", "pool_w", "kv_norm", "k_norm", "q_norm", "sinks", "mem_norm", "mem_q_norm", "mem_k_norm",
               "norm_mlp")


ANY = pl.BlockSpec(memory_space=pl.ANY)


def _params(*sem):
    return pltpu.CompilerParams(dimension_semantics=sem, vmem_limit_bytes=VMEM_LIMIT)


def _mm(a, b, *, name, ta=False, tb=False, b_kind=None, layer=0, res=None, relu2=False, mul2=None, out_dtype=F32,
        out_kind=None, out_buf=None):
    if ta:
        K, M = a.shape
    else:
        M, K = a.shape
    if b_kind is None:
        rows_b, cols_b = b.shape
    elif b_kind == "rows":
        rows_b, cols_b = b.shape[0] * b.shape[2], b.shape[3]
    else:
        rows_b, cols_b = b.shape[1:]
    N, K2 = (rows_b, cols_b) if tb else (cols_b, rows_b)
    assert K == K2, (a.shape, b.shape)
    tm = min(M, MM_TILE if K <= MM_TILE else MM_TILE // 2)
    tn = min(N, MM_TILE)
    assert M % tm == 0 and N % tn == 0
    row_tile, col_tile = (tn, K) if tb else (K, tn)
    a_spec = pl.BlockSpec((K, tm), lambda j, i: (0, i)) if ta else pl.BlockSpec((tm, K), lambda j, i: (i, 0))

    def rc(j):
        return (j, 0) if tb else (0, j)

    if b_kind is None:
        b_spec = pl.BlockSpec((row_tile, col_tile), lambda j, i: rc(j))
    elif b_kind == "rows":
        per = row_tile // b.shape[2]
        b_spec = pl.BlockSpec((per, None, b.shape[2], col_tile), lambda j, i: (rc(j)[0], layer, 0, rc(j)[1]))
    else:
        b_spec = pl.BlockSpec((None, row_tile, col_tile), lambda j, i: (layer, *rc(j)))
    o_spec = pl.BlockSpec((tm, tn), lambda j, i: (i, j))
    dn = (((0 if ta else 1,), (1 if tb else 0,)), ((), ()))
    extra = [e for e in (res, mul2) if e is not None]
    n_in = 2 + len(extra) + (1 if out_buf is not None else 0)

    def body(*refs):
        a_ref, b_ref = refs[0], refs[1]
        extra_refs = refs[2:2 + len(extra)]
        out = refs[n_in]
        bv = b_ref[...].astype(BF16).reshape(row_tile, col_tile)
        v = lax.dot_general(a_ref[...].astype(BF16), bv, dn, preferred_element_type=F32)
        if res is not None:
            v = extra_refs[0][...] + v
        elif mul2 is not None:
            v = v * (2.0 * jnp.sqrt(extra_refs[0][...].astype(F32)))
        if relu2:
            r = jnp.maximum(v, 0.0)
            v = r * r
        out[...] = v.astype(out.dtype).reshape(out.shape)

    in_specs = [a_spec, b_spec] + [o_spec] * len(extra)
    operands = [a, b, *extra]
    aliases = {}
    if out_kind is None:
        out_shape = jax.ShapeDtypeStruct((M, N), BF16 if relu2 else out_dtype)
        out_specs = o_spec
    else:
        if out_kind == "rows":
            s = out_buf.shape[2]
            out_specs = pl.BlockSpec((tm // s, None, s, tn), lambda j, i: (i, layer, 0, j))
        else:
            out_specs = pl.BlockSpec((None, tm, tn), lambda j, i: (layer, i, j))
        out_shape = jax.ShapeDtypeStruct(out_buf.shape, out_buf.dtype)
        in_specs.append(ANY)
        operands.append(out_buf)
        aliases = {len(operands) - 1: 0}
    return pl.pallas_call(
        body, name=name, grid=(N // tn, M // tm), in_specs=in_specs, out_specs=out_specs, out_shape=out_shape,
        input_output_aliases=aliases, compiler_params=_params("parallel", "parallel"),
    )(*operands)


def _weight_block(b, b_kind, transposed, tn):
    if b_kind == "rows":
        s = b.shape[2]
        rows, cols = b.shape[0] * s, b.shape[3]
        if transposed:
            return (lambda at: pl.BlockSpec((b.shape[0], None, s, cols), lambda *g: (0, 0, 0, 0))), rows, cols
        return (lambda at: pl.BlockSpec((b.shape[0], None, s, tn), lambda *g: (0, 0, 0, at(*g)))), rows, cols
    rows, cols = b.shape[1:]
    if transposed:
        return (lambda at: pl.BlockSpec((None, rows, cols), lambda *g: (0, 0, 0))), rows, cols
    return (lambda at: pl.BlockSpec((None, rows, tn), lambda *g: (0, 0, at(*g)))), rows, cols


def _norm_mm(x, gain, b, *, b_kind, name, relu2=False):
    M, K = x.shape
    tm = min(M, MM_TILE)
    spec_of, rows, N = _weight_block(b, b_kind, False, min(MM_TILE, b.shape[-1]))
    tn = min(N, MM_TILE)
    assert rows == K and M % tm == 0 and N % tn == 0

    def body(x_ref, g_ref, b_ref, xn_ref, o_ref):
        @pl.when(pl.program_id(1) == 0)
        def _():
            xv = x_ref[...]
            r = lax.rsqrt(jnp.mean(xv * xv, axis=-1, keepdims=True) + EPS)
            xn_ref[...] = ((xv * r) * g_ref[...]).astype(xn_ref.dtype)

        v = jnp.dot(xn_ref[...], b_ref[...].astype(BF16).reshape(K, tn), preferred_element_type=F32)
        if relu2:
            r2 = jnp.maximum(v, 0.0)
            v = r2 * r2
        o_ref[...] = v.astype(o_ref.dtype)

    rows_spec = pl.BlockSpec((tm, K), lambda i, j: (i, 0))
    return pl.pallas_call(
        body, name=name, grid=(M // tm, N // tn),
        in_specs=[rows_spec, pl.BlockSpec((1, K), lambda i, j: (0, 0)), spec_of(lambda i, j: j)],
        out_specs=(rows_spec, pl.BlockSpec((tm, tn), lambda i, j: (i, j))),
        out_shape=(jax.ShapeDtypeStruct((M, K), BF16), jax.ShapeDtypeStruct((M, N), BF16 if relu2 else F32)),
        compiler_params=_params("parallel", "arbitrary"),
    )(x, gain.reshape(1, K), b)


def _mm_rms_bwd(a, b, x, gain, res, *, b_kind, name, also_bf16):
    M, K = a.shape
    spec_of, N, cols = _weight_block(b, b_kind, True, None)
    assert cols == K and x.shape == (M, N)
    tm = min(M, MM_TILE if K <= MM_TILE else MM_TILE // 2)
    assert M % tm == 0

    def body(a_ref, b_ref, x_ref, g_ref, res_ref, *outs):
        i = pl.program_id(0)
        dy = lax.dot_general(a_ref[...].astype(BF16), b_ref[...].astype(BF16).reshape(N, K), (((1,), (1,)), ((), ())),
                             preferred_element_type=F32)
        xv = x_ref[...]
        r = lax.rsqrt(jnp.mean(xv * xv, axis=-1, keepdims=True) + EPS)
        xh = xv * r
        part = jnp.sum(dy * xh, axis=0, keepdims=True)
        dg_ref = outs[-1]

        @pl.when(i == 0)
        def _():
            dg_ref[...] = part

        @pl.when(i > 0)
        def _():
            dg_ref[...] += part

        gdy = dy * g_ref[...]
        dx = res_ref[...] + r * (gdy - xh * jnp.mean(gdy * xh, axis=-1, keepdims=True))
        outs[0][...] = dx
        if also_bf16:
            outs[1][...] = dx.astype(BF16)

    row = pl.BlockSpec((tm, N), lambda i: (i, 0))
    vec = pl.BlockSpec((1, N), lambda i: (0, 0))
    out_specs = [row] + ([row] if also_bf16 else []) + [vec]
    out_shape = [jax.ShapeDtypeStruct((M, N), F32)] + ([jax.ShapeDtypeStruct((M, N), BF16)] if also_bf16 else [])
    outs = pl.pallas_call(
        body, name=name, grid=(M // tm,),
        in_specs=[pl.BlockSpec((tm, K), lambda i: (i, 0)), spec_of(None), row, vec, row], out_specs=out_specs,
        out_shape=out_shape + [jax.ShapeDtypeStruct((1, N), F32)], compiler_params=_params("arbitrary"),
    )(a, b, x, gain.reshape(1, N), res)
    return (*outs[:-1], outs[-1].reshape(N))


def _row_tile(rows, d):
    t = min(rows, (512 * 1024) // d)
    while rows % t or (t != rows and t % 16):
        t -= 1
    return t


def _rms_fwd(x, g, *, name, out_dtype=BF16):
    R, D = x.shape
    tr = _row_tile(R, D)

    def body(x_ref, g_ref, o_ref):
        xv = x_ref[...].astype(F32)
        r = lax.rsqrt(jnp.mean(xv * xv, axis=-1, keepdims=True) + EPS)
        o_ref[...] = ((xv * r) * g_ref[...]).astype(o_ref.dtype)

    return pl.pallas_call(
        body, name=name, grid=(R // tr,),
        in_specs=[pl.BlockSpec((tr, D), lambda i: (i, 0)), pl.BlockSpec((1, D), lambda i: (0, 0))],
        out_specs=pl.BlockSpec((tr, D), lambda i: (i, 0)), out_shape=jax.ShapeDtypeStruct((R, D), out_dtype),
        compiler_params=_params("parallel"),
    )(x, g.reshape(1, D))


def _rms_bwd(x, g, dys, *, name, res=None, want_dx=True, also_bf16=False):
    R, D = x.shape
    tr = _row_tile(R, D)
    n_dy = len(dys)
    has_res = res is not None

    def body(*refs):
        x_ref, g_ref = refs[0], refs[1]
        dy_refs = refs[2:2 + n_dy]
        res_ref = refs[2 + n_dy] if has_res else None
        outs = refs[2 + n_dy + (1 if has_res else 0):]
        dg_ref = outs[-1]
        i = pl.program_id(0)
        xv = x_ref[...].astype(F32)
        dy = dy_refs[0][...].astype(F32)
        for extra in dy_refs[1:]:
            dy = dy + extra[...].astype(F32)
        r = lax.rsqrt(jnp.mean(xv * xv, axis=-1, keepdims=True) + EPS)
        xh = xv * r
        part = jnp.sum(dy * xh, axis=0, keepdims=True)

        @pl.when(i == 0)
        def _():
            dg_ref[...] = part

        @pl.when(i > 0)
        def _():
            dg_ref[...] += part

        if want_dx:
            gdy = dy * g_ref[...]
            dx = r * (gdy - xh * jnp.mean(gdy * xh, axis=-1, keepdims=True))
            if has_res:
                dx = res_ref[...] + dx
            outs[0][...] = dx
            if also_bf16:
                outs[1][...] = dx.astype(BF16)

    row = pl.BlockSpec((tr, D), lambda i: (i, 0))
    vec = pl.BlockSpec((1, D), lambda i: (0, 0))
    out_shape = [jax.ShapeDtypeStruct((1, D), F32)]
    out_specs = [vec]
    if also_bf16:
        out_shape = [jax.ShapeDtypeStruct((R, D), BF16)] + out_shape
        out_specs = [row] + out_specs
    if want_dx:
        out_shape = [jax.ShapeDtypeStruct((R, D), F32)] + out_shape
        out_specs = [row] + out_specs
    outs = pl.pallas_call(
        body, name=name, grid=(R // tr,),
        in_specs=[row, vec] + [row] * (n_dy + (1 if has_res else 0)), out_specs=out_specs, out_shape=out_shape,
        compiler_params=_params("arbitrary"),
    )(x, g.reshape(1, D), *dys, *([res] if has_res else []))
    return (*outs[:-1], outs[-1].reshape(D)) if want_dx else outs[0].reshape(D)


POOL_TILE = 512
MEM_Q_TILE = 512


def _softmax(q, k, bias, valid, sink):
    s = lax.dot_general(q, k, (((1,), (1,)), ((), ())), preferred_element_type=F32) * SCALE
    if bias is not None:
        s = s - bias
    if valid is not None:
        s = jnp.where(valid, s, NEG)
    m = jnp.max(s, axis=-1, keepdims=True)
    if sink is not None:
        m = jnp.maximum(m, sink)
    e = jnp.exp(s - m)
    z = jnp.sum(e, axis=-1, keepdims=True)
    if sink is None:
        return e * (1.0 / z), None
    es = jnp.exp(sink - m)
    inv = 1.0 / (z + es)
    return e * inv, es * inv


LANES = 128


def _seg_mean(v):
    r = lax.broadcasted_iota(jnp.int32, (LANES, LANES), 0) // HEAD_DIM
    c = lax.broadcasted_iota(jnp.int32, (LANES, LANES), 1) // HEAD_DIM
    seg = jnp.where(r == c, 1.0 / HEAD_DIM, 0.0).astype(BF16)
    hi = v.astype(BF16)
    lo = (v - hi.astype(F32)).astype(BF16)
    parts = []
    for g in range(v.shape[1] // LANES):
        sl = slice(g * LANES, (g + 1) * LANES)
        parts.append(jnp.dot(hi[:, sl], seg, preferred_element_type=F32) + jnp.dot(lo[:, sl], seg, preferred_element_type=F32))
    return parts[0] if len(parts) == 1 else jnp.concatenate(parts, axis=1)


def _cols(rows, width, col):
    return pl.BlockSpec((rows, width), lambda i: (i, col))


def _head_gain(g, heads):
    return jnp.tile(g, heads).reshape(1, heads * HEAD_DIM)


def _fold_heads(dg, heads):
    return dg.reshape(heads, HEAD_DIM).sum(axis=0)


def _seg_rms_fwd(x, gain, *, width, col, name):
    R = x.shape[0]
    tr = _row_tile(R, width)

    def body(x_ref, g_ref, o_ref):
        xv = x_ref[...]
        r = lax.rsqrt(_seg_mean(xv * xv) + EPS)
        o_ref[...] = ((xv * r) * g_ref[...]).astype(o_ref.dtype)

    return pl.pallas_call(
        body, name=name, grid=(R // tr,), in_specs=[_cols(tr, width, col), pl.BlockSpec((1, width), lambda i: (0, 0))],
        out_specs=_cols(tr, width, 0), out_shape=jax.ShapeDtypeStruct((R, width), BF16), compiler_params=_params("parallel"),
    )(x, gain)


def _seg_rms_bwd(x, gain, dys, *, width, col, name, out_buf=None, out_col=0):
    R = x.shape[0]
    tr = _row_tile(R, width)
    n_dy = len(dys)

    def body(*refs):
        x_ref, g_ref = refs[0], refs[1]
        dy_refs = refs[2:2 + n_dy]
        dx_ref, dg_ref = refs[-2], refs[-1]
        i = pl.program_id(0)
        xv = x_ref[...]
        dy = dy_refs[0][...]
        for extra in dy_refs[1:]:
            dy = dy + extra[...]
        r = lax.rsqrt(_seg_mean(xv * xv) + EPS)
        xh = xv * r
        part = jnp.sum(dy * xh, axis=0, keepdims=True)

        @pl.when(i == 0)
        def _():
            dg_ref[...] = part

        @pl.when(i > 0)
        def _():
            dg_ref[...] += part

        gdy = dy * g_ref[...]
        dx_ref[...] = (r * (gdy - xh * _seg_mean(gdy * xh))).astype(dx_ref.dtype)

    vec = pl.BlockSpec((1, width), lambda i: (0, 0))
    in_specs = [_cols(tr, width, col), vec] + [_cols(tr, width, 0)] * n_dy
    operands = [x, gain, *dys]
    aliases = {}
    dx_shape = jax.ShapeDtypeStruct((R, width), BF16)
    if out_buf is not None:
        in_specs.append(ANY)
        operands.append(out_buf)
        aliases = {len(operands) - 1: 0}
        dx_shape = jax.ShapeDtypeStruct(out_buf.shape, out_buf.dtype)
    return pl.pallas_call(
        body, name=name, grid=(R // tr,), in_specs=in_specs, out_specs=(_cols(tr, width, out_col), vec),
        out_shape=(dx_shape, jax.ShapeDtypeStruct((1, width), F32)), input_output_aliases=aliases,
        compiler_params=_params("arbitrary"),
    )(*operands)


def _sum_into(a, b, out_buf, out_col, *, name):
    R, width = a.shape
    tr = _row_tile(R, width)

    def body(a_ref, b_ref, _, o_ref):
        o_ref[...] = (a_ref[...] + b_ref[...]).astype(o_ref.dtype)

    return pl.pallas_call(
        body, name=name, grid=(R // tr,), in_specs=[_cols(tr, width, 0), _cols(tr, width, 0), ANY],
        out_specs=_cols(tr, width, out_col), out_shape=jax.ShapeDtypeStruct(out_buf.shape, out_buf.dtype),
        input_output_aliases={2: 0}, compiler_params=_params("parallel"),
    )(a, b, out_buf)


def _pool_lane_group():
    return lax.broadcasted_iota(jnp.int32, (1, MAIN_WIDTH), 1) // POOL_GROUP_DIM


def _pool_pick(group, per_window):
    s1, s2, s3, s4 = per_window
    return jnp.where(group == 0, s1, jnp.where(group == 1, s2, jnp.where(group == 2, s3, s4)))


def _pool_delta(u_ref, halo_ref, tile):
    group = _pool_lane_group()
    halo = jnp.where(tile == 0, 0.0, halo_ref[...])
    ext = jnp.concatenate([halo, u_ref[...]], axis=0)
    n = ext.shape[0]
    s1 = ext + pltpu.roll(ext, 1, 0)
    s2 = s1 + pltpu.roll(s1, 2, 0)
    s3 = s2 + pltpu.roll(s2, 4, 0)
    s4 = s3 + pltpu.roll(s3, 8, 0)
    ws = _pool_pick(group, (s1, s2, s3, s4))[POOL_HALO:n]
    t = tile * POOL_TILE + lax.broadcasted_iota(jnp.int32, (POOL_TILE, 1), 0)
    cnt = jnp.minimum(t + 1, _pool_pick(group, (2, 4, 8, 16))).astype(F32)
    return ws / cnt - u_ref[...], cnt


def _pool_in_specs():
    per_tile = POOL_TILE // POOL_HALO
    cur = _cols(POOL_TILE, MAIN_WIDTH, 0)
    prev = pl.BlockSpec((POOL_HALO, MAIN_WIDTH), lambda i: (jnp.maximum(i * per_tile - 1, 0), 0))
    mix = pl.BlockSpec((MAIN_WIDTH, MAIN_WIDTH), lambda i: (0, 0))
    vec = pl.BlockSpec((1, MAIN_WIDTH), lambda i: (0, 0))
    return cur, prev, mix, vec


def _pool_mix_fwd(proj, mix, scale, cat, *, name):
    T = proj.shape[0]
    assert T % POOL_TILE == 0
    cur, prev, mix_spec, vec = _pool_in_specs()

    def body(u_ref, halo_ref, mix_ref, sc_ref, _, o_ref):
        d, _cnt = _pool_delta(u_ref, halo_ref, pl.program_id(0))
        mixed = jnp.dot(d.astype(BF16), mix_ref[...].astype(BF16), preferred_element_type=F32)
        o_ref[...] = (mixed * sc_ref[...]).astype(o_ref.dtype)

    return pl.pallas_call(
        body, name=name, grid=(T // POOL_TILE,), in_specs=[cur, prev, mix_spec, vec, ANY], out_specs=cur,
        out_shape=jax.ShapeDtypeStruct(cat.shape, cat.dtype), input_output_aliases={4: 0}, compiler_params=_params("parallel"),
    )(proj, proj, mix, scale, cat)


def _pool_mix_bwd(proj, mix, scale, dcat, *, name):
    T = proj.shape[0]
    nt = T // POOL_TILE
    per_tile = POOL_TILE // POOL_HALO
    cur, prev, mix_spec, vec = _pool_in_specs()
    nxt = pl.BlockSpec((POOL_HALO, MAIN_WIDTH), lambda i: (jnp.minimum((i + 1) * per_tile, nt * per_tile - 1), 0))

    def body(u_ref, halo_ref, mix_ref, sc_ref, do_ref, donext_ref, du_ref, dmix_ref, dsc_ref):
        tile = pl.program_id(0)
        group = _pool_lane_group()
        d, cnt = _pool_delta(u_ref, halo_ref, tile)
        mixb = mix_ref[...].astype(BF16)
        db = d.astype(BF16)
        mixed = jnp.dot(db, mixb, preferred_element_type=F32)
        dout = do_ref[...]
        dsc = jnp.sum(dout * mixed, axis=0, keepdims=True)
        sc = sc_ref[...]
        dmixed = (dout * sc).astype(BF16)
        dmix = lax.dot_general(db, dmixed, (((0,), (0,)), ((), ())), preferred_element_type=F32)

        @pl.when(tile == 0)
        def _():
            dmix_ref[...] = dmix
            dsc_ref[...] = dsc

        @pl.when(tile > 0)
        def _():
            dmix_ref[...] += dmix
            dsc_ref[...] += dsc

        dnext = jnp.where(tile == nt - 1, 0.0, donext_ref[...])
        dmixed_ext = jnp.concatenate([dmixed, (dnext * sc).astype(BF16)], axis=0)
        dd_ext = lax.dot_general(dmixed_ext, mixb, (((1,), (1,)), ((), ())), preferred_element_type=F32)
        window = _pool_pick(group, (2.0, 4.0, 8.0, 16.0))
        cnt_ext = jnp.concatenate([cnt, jnp.broadcast_to(window, (POOL_HALO, MAIN_WIDTH))], axis=0)
        q = dd_ext / cnt_ext
        n = q.shape[0]
        r1 = q + pltpu.roll(q, n - 1, 0)
        r2 = r1 + pltpu.roll(r1, n - 2, 0)
        r3 = r2 + pltpu.roll(r2, n - 4, 0)
        r4 = r3 + pltpu.roll(r3, n - 8, 0)
        back = _pool_pick(group, (r1, r2, r3, r4))
        du_ref[...] = (back[0:POOL_TILE] - dd_ext[0:POOL_TILE]).astype(du_ref.dtype)

    return pl.pallas_call(
        body, name=name, grid=(nt,), in_specs=[cur, prev, mix_spec, vec, cur, nxt], out_specs=(cur, mix_spec, vec),
        out_shape=(jax.ShapeDtypeStruct((T, D_MODEL), BF16), jax.ShapeDtypeStruct((MAIN_WIDTH, MAIN_WIDTH), F32),
                   jax.ShapeDtypeStruct((1, MAIN_WIDTH), F32)),
        compiler_params=_params("arbitrary"),
    )(proj, proj, mix, scale, dcat, dcat)


def _head(a, h):
    return a[:, h * HEAD_DIM:(h + 1) * HEAD_DIM]


def _swa_mask(blk):
    rows = SWA_GROUP * BLOCK
    qi = lax.broadcasted_iota(jnp.int32, (rows, 2 * BLOCK), 0) % BLOCK
    kj = lax.broadcasted_iota(jnp.int32, (rows, 2 * BLOCK), 1)
    dist = qi + BLOCK - kj
    valid = (dist >= 0) & (dist < BLOCK) & ((blk > 0) | (kj >= BLOCK))
    return dist.astype(F32), valid


def _swa_head_terms(sink_ref, kvh, dist):
    grp = lax.broadcasted_iota(jnp.int32, (SWA_GROUP * BLOCK, 1), 0) // BLOCK
    slopes = [2.0 ** (-8.0 * (kvh * SWA_GROUP + g + 1) / SWA_Q_HEADS) for g in range(SWA_GROUP)]
    sinks = [sink_ref[kvh * SWA_GROUP + g] for g in range(SWA_GROUP)]
    slope = jnp.where(grp == 0, slopes[0], jnp.where(grp == 1, slopes[1], slopes[2]))
    sink = jnp.where(grp == 0, sinks[0], jnp.where(grp == 1, sinks[1], sinks[2]))
    return slope * dist, sink


def _stack_heads(a, kvh):
    return jnp.concatenate([_head(a, kvh * SWA_GROUP + g) for g in range(SWA_GROUP)], axis=0)


def _swa_specs(nb):
    def at(n):
        return jnp.minimum(n, nb - 1)

    q = pl.BlockSpec((BLOCK, MAIN_WIDTH), lambda n: (at(n), 0))
    k_prev = pl.BlockSpec((BLOCK, KV_HALF), lambda n: (jnp.maximum(at(n) - 1, 0), 0))
    k_cur = pl.BlockSpec((BLOCK, KV_HALF), lambda n: (at(n), 0))
    v_prev = pl.BlockSpec((BLOCK, KV_HALF), lambda n: (jnp.maximum(at(n) - 1, 0), 1))
    v_cur = pl.BlockSpec((BLOCK, KV_HALF), lambda n: (at(n), 1))
    return q, k_prev, k_cur, v_prev, v_cur


def _swa_attn_fwd(qn, kn, kv, sinks, cat, *, name):
    T = qn.shape[0]
    nb = T // BLOCK
    q_spec, k_prev, k_cur, v_prev, v_cur = _swa_specs(nb)

    def body(sink_ref, q_ref, kp_ref, kc_ref, vp_ref, vc_ref, _, o_ref):
        dist, valid = _swa_mask(pl.program_id(0))
        kk = jnp.concatenate([kp_ref[...], kc_ref[...]], axis=0)
        vv = jnp.concatenate([vp_ref[...], vc_ref[...]], axis=0).astype(BF16)
        q = q_ref[...]
        outs = []
        for kvh in range(SWA_KV_HEADS):
            bias, sink = _swa_head_terms(sink_ref, kvh, dist)
            p, _ps = _softmax(_stack_heads(q, kvh), _head(kk, kvh), bias, valid, sink)
            o = jnp.dot(p.astype(BF16), _head(vv, kvh), preferred_element_type=F32)
            outs += [o[g * BLOCK:(g + 1) * BLOCK] for g in range(SWA_GROUP)]
        o_ref[...] = jnp.concatenate(outs, axis=1).astype(o_ref.dtype)

    return pl.pallas_call(
        body, name=name, grid=(nb,),
        in_specs=[pl.BlockSpec(memory_space=pltpu.SMEM), q_spec, k_prev, k_cur, v_prev, v_cur, ANY], out_specs=q_spec,
        out_shape=jax.ShapeDtypeStruct(cat.shape, cat.dtype), input_output_aliases={6: 0}, compiler_params=_params("parallel"),
    )(sinks, qn, kn, kn, kv, kv, cat)


def _swa_attn_bwd(qn, kn, kv, sinks, dcat, dqn, *, name):
    T = qn.shape[0]
    nb = T // BLOCK
    q_spec, k_prev, k_cur, v_prev, v_cur = _swa_specs(nb)
    late = pl.BlockSpec((BLOCK, KV_HALF), lambda n: (jnp.maximum(n - 1, 0), 0))
    tn_dims = (((0,), (0,)), ((), ()))

    def body(sink_ref, q_ref, do_ref, kp_ref, kc_ref, vp_ref, vc_ref, _, dq_ref, dk_ref, dv_ref, ds_ref, ck, cv):
        blk = pl.program_id(0)

        @pl.when(blk == 0)
        def _():
            ck[...] = jnp.zeros_like(ck)
            cv[...] = jnp.zeros_like(cv)
            ds_ref[...] = jnp.zeros_like(ds_ref)

        @pl.when(blk < nb)
        def _():
            dist, valid = _swa_mask(blk)
            kk = jnp.concatenate([kp_ref[...], kc_ref[...]], axis=0)
            vv = jnp.concatenate([vp_ref[...], vc_ref[...]], axis=0).astype(BF16)
            q = q_ref[...]
            dout = do_ref[...].astype(BF16)
            lane = lax.broadcasted_iota(jnp.int32, (1, LANES), 1)
            dsinks = jnp.zeros((1, LANES), F32)
            dqs, dks, dvs = [], [], []
            for kvh in range(SWA_KV_HEADS):
                bias, sink = _swa_head_terms(sink_ref, kvh, dist)
                qq, kh, vh, dd = _stack_heads(q, kvh), _head(kk, kvh), _head(vv, kvh), _stack_heads(dout, kvh)
                p, ps = _softmax(qq, kh, bias, valid, sink)
                dp = lax.dot_general(dd, vh, (((1,), (1,)), ((), ())), preferred_element_type=F32)
                dsum = jnp.sum(p * dp, axis=-1, keepdims=True)
                ds = (p * (dp - dsum)).astype(BF16)
                dq = jnp.dot(ds, kh, preferred_element_type=F32) * SCALE
                dqs += [dq[g * BLOCK:(g + 1) * BLOCK] for g in range(SWA_GROUP)]
                dks.append(lax.dot_general(qq, ds, tn_dims, preferred_element_type=F32) * SCALE)
                dvs.append(lax.dot_general(dd, p.astype(BF16), tn_dims, preferred_element_type=F32))
                dsink = -(ps * dsum)
                for g in range(SWA_GROUP):
                    dsinks = dsinks + jnp.where(lane == kvh * SWA_GROUP + g, jnp.sum(dsink[g * BLOCK:(g + 1) * BLOCK]), 0.0)
            dq_ref[...] = jnp.concatenate(dqs, axis=1)
            dk = jnp.concatenate(dks, axis=0).T
            dv = jnp.concatenate(dvs, axis=0).T
            dk_ref[...] = ck[...] + dk[0:BLOCK]
            dv_ref[...] = cv[...] + dv[0:BLOCK]
            ck[...] = dk[BLOCK:2 * BLOCK]
            cv[...] = dv[BLOCK:2 * BLOCK]
            ds_ref[...] += dsinks

        @pl.when(blk == nb)
        def _():
            dk_ref[...] = ck[...]
            dv_ref[...] = cv[...]

    return pl.pallas_call(
        body, name=name, grid=(nb + 1,),
        in_specs=[pl.BlockSpec(memory_space=pltpu.SMEM), q_spec, q_spec, k_prev, k_cur, v_prev, v_cur, ANY],
        out_specs=(q_spec, late, late, pl.BlockSpec((1, LANES), lambda n: (0, 0))),
        out_shape=(jax.ShapeDtypeStruct(dqn.shape, dqn.dtype), jax.ShapeDtypeStruct((T, KV_HALF), F32),
                   jax.ShapeDtypeStruct((T, KV_HALF), F32), jax.ShapeDtypeStruct((1, LANES), F32)),
        scratch_shapes=[pltpu.VMEM((BLOCK, KV_HALF), F32), pltpu.VMEM((BLOCK, KV_HALF), F32)],
        input_output_aliases={7: 0}, compiler_params=_params("arbitrary"),
    )(sinks, qn, dcat, kn, kn, kv, kv, dqn)


def _mem_specs(M, tq, q_col):
    q = _cols(tq, MEM_WIDTH, q_col)
    k = pl.BlockSpec((M, MEM_WIDTH), lambda i: (0, 0))
    v = pl.BlockSpec((M, MEM_WIDTH), lambda i: (0, 1))
    return q, k, v


def _mem_attn_fwd(q, q_col, mkn, mkv, *, name):
    T = q.shape[0]
    M = mkn.shape[0]
    tq = min(T, MEM_Q_TILE)
    q_spec, k_spec, v_spec = _mem_specs(M, tq, q_col)

    def body(q_ref, k_ref, v_ref, o_ref):
        qq, kk, vv = q_ref[...], k_ref[...], v_ref[...].astype(BF16)
        outs = []
        for h in range(MEM_HEADS):
            p, _ps = _softmax(_head(qq, h), _head(kk, h), None, None, None)
            outs.append(jnp.dot(p.astype(BF16), _head(vv, h), preferred_element_type=F32))
        o_ref[...] = jnp.concatenate(outs, axis=1).astype(o_ref.dtype)

    return pl.pallas_call(
        body, name=name, grid=(T // tq,), in_specs=[q_spec, k_spec, v_spec], out_specs=_cols(tq, MEM_WIDTH, MAIN_WIDTH // MEM_WIDTH),
        out_shape=jax.ShapeDtypeStruct((T, D_MODEL), BF16), compiler_params=_params("parallel"),
    )(q, mkn, mkv)


def _mem_attn_bwd(q, q_col, mkn, mkv, dcat, *, dq_width, name):
    T = q.shape[0]
    M = mkn.shape[0]
    tq = min(T, MEM_Q_TILE)
    q_spec, k_spec, v_spec = _mem_specs(M, tq, q_col)
    last = MAIN_WIDTH // MEM_WIDTH
    tn_dims = (((0,), (0,)), ((), ()))

    def body(q_ref, do_ref, k_ref, v_ref, dq_ref, dk_ref, dv_ref):
        i = pl.program_id(0)
        qq, kk, vv, dout = q_ref[...], k_ref[...], v_ref[...].astype(BF16), do_ref[...].astype(BF16)
        dqs, dks, dvs = [], [], []
        for h in range(MEM_HEADS):
            qh, kh, vh, dh = _head(qq, h), _head(kk, h), _head(vv, h), _head(dout, h)
            p, _ps = _softmax(qh, kh, None, None, None)
            dp = lax.dot_general(dh, vh, (((1,), (1,)), ((), ())), preferred_element_type=F32)
            dsum = jnp.sum(p * dp, axis=-1, keepdims=True)
            ds = (p * (dp - dsum)).astype(BF16)
            dqs.append(jnp.dot(ds, kh, preferred_element_type=F32) * SCALE)
            dks.append(lax.dot_general(qh, ds, tn_dims, preferred_element_type=F32) * SCALE)
            dvs.append(lax.dot_general(dh, p.astype(BF16), tn_dims, preferred_element_type=F32))
        dq_ref[...] = jnp.concatenate(dqs, axis=1)
        dk = jnp.concatenate(dks, axis=0).T
        dv = jnp.concatenate(dvs, axis=0).T

        @pl.when(i == 0)
        def _():
            dk_ref[...] = dk
            dv_ref[...] = dv

        @pl.when(i > 0)
        def _():
            dk_ref[...] += dk
            dv_ref[...] += dv

    acc = pl.BlockSpec((M, MEM_WIDTH), lambda i: (0, 0))
    return pl.pallas_call(
        body, name=name, grid=(T // tq,), in_specs=[q_spec, _cols(tq, MEM_WIDTH, last), k_spec, v_spec],
        out_specs=(_cols(tq, MEM_WIDTH, dq_width // MEM_WIDTH - 1), acc, acc),
        out_shape=(jax.ShapeDtypeStruct((T, dq_width), F32), jax.ShapeDtypeStruct((M, MEM_WIDTH), F32),
                   jax.ShapeDtypeStruct((M, MEM_WIDTH), F32)),
        compiler_params=_params("arbitrary"),
    )(q, dcat, mkn, mkv)


def _loss(y, target, *, name):
    T, D = y.shape
    tr = _row_tile(T, D)

    def body(y_ref, t_ref, l_ref, dy_ref, dyb_ref):
        i = pl.program_id(0)
        err = y_ref[...] - t_ref[...]
        dy = err / float(D)
        dy_ref[...] = dy
        dyb_ref[...] = dy.astype(BF16)
        part = jnp.full((8, 128), 0.5 * jnp.sum(jnp.mean(err * err, axis=-1)), F32)

        @pl.when(i == 0)
        def _():
            l_ref[...] = part

        @pl.when(i > 0)
        def _():
            l_ref[...] += part

    row = pl.BlockSpec((tr, D), lambda i: (i, 0))
    return pl.pallas_call(
        body, name=name, grid=(T // tr,), in_specs=[row, row],
        out_specs=(pl.BlockSpec((8, 128), lambda i: (0, 0)), row, row),
        out_shape=(jax.ShapeDtypeStruct((8, 128), F32), jax.ShapeDtypeStruct((T, D), F32), jax.ShapeDtypeStruct((T, D), BF16)),
        compiler_params=_params("arbitrary"),
    )(y, target)


def _position():
    return lax.axis_index("x"), lax.axis_index("y"), lax.axis_index("c")


def _all_gather(arrays, *, name):
    n = len(arrays)

    def body(*refs):
        srcs, outs = refs[:n], refs[n:2 * n]
        token, send_sems, recv_sems, local_sems = refs[2 * n:]
        token[...] = jnp.zeros_like(token)
        x, y, c = _position()
        me, sibling = (x, y, c), (x, y, 1 - c)
        chips = [(1 - x, y), (x, 1 - y), (1 - x, 1 - y)]

        def slot(a, px, py, pc):
            return outs[a].at[4 * px + 2 * py + pc]

        def copy(a, k, block, to, src=None):
            return pltpu.make_async_remote_copy(
                src_ref=slot(a, *block) if src is None else src, dst_ref=slot(a, *block),
                send_sem=send_sems.at[a, k], recv_sem=recv_sems.at[a, k], device_id=to, device_id_type=MESH)

        mine = [pltpu.make_async_copy(srcs[a], slot(a, *me), local_sems.at[a]) for a in range(n)]
        for cp in mine:
            cp.start()
        first, passed = [], []
        for a in range(n):
            first.append(copy(a, 0, me, sibling, src=srcs[a]))
            first += [copy(a, 1 + j, me, (*chip, c), src=srcs[a]) for j, chip in enumerate(chips)]
        for cp in first:
            cp.start()
        for a in range(n):
            for j, chip in enumerate(chips):
                copy(a, 1 + j, (*chip, c), me).wait_recv()
                fwd = copy(a, 4 + j, (*chip, c), sibling)
                fwd.start()
                passed.append(fwd)
        for a in range(n):
            copy(a, 0, sibling, me).wait_recv()
            for j, chip in enumerate(chips):
                copy(a, 4 + j, (*chip, 1 - c), me).wait_recv()
        for cp in first + passed:
            cp.wait_send()
        for cp in mine:
            cp.wait()

    return pl.pallas_call(
        body, name=name, in_specs=[ANY] * n, out_specs=[ANY] * n + [pl.BlockSpec(memory_space=pltpu.VMEM)],
        out_shape=[jax.ShapeDtypeStruct((N_DEV,) + a.shape, a.dtype) for a in arrays] + [jax.ShapeDtypeStruct((8, 128), F32)],
        scratch_shapes=[pltpu.SemaphoreType.DMA((n, 7)), pltpu.SemaphoreType.DMA((n, 7)), pltpu.SemaphoreType.DMA((n,))],
    )(*arrays)


def _sibling_exchange(by_core, whole, *, name):
    n1, n = len(by_core), len(by_core) + len(whole)

    def body(*refs):
        srcs, outs = refs[:n], refs[n:2 * n]
        send_sems, recv_sems = refs[2 * n:]
        x, y, c = _position()
        copies = [
            pltpu.make_async_remote_copy(src_ref=srcs[a].at[:, 1 - c] if a < n1 else srcs[a], dst_ref=outs[a],
                                         send_sem=send_sems.at[a], recv_sem=recv_sems.at[a], device_id=(x, y, 1 - c),
                                         device_id_type=MESH)
            for a in range(n)]
        for cp in copies:
            cp.start()
        for cp in copies:
            cp.wait()

    out_shape = [jax.ShapeDtypeStruct(a.shape[:1] + a.shape[2:], a.dtype) for a in by_core]
    out_shape += [jax.ShapeDtypeStruct(a.shape, a.dtype) for a in whole]
    outs = pl.pallas_call(
        body, name=name, in_specs=[ANY] * n, out_specs=[ANY] * n, out_shape=out_shape,
        scratch_shapes=[pltpu.SemaphoreType.DMA((n,)), pltpu.SemaphoreType.DMA((n,))],
    )(*by_core, *whole)
    return outs[:n1], outs[n1:]


HBM = pl.BlockSpec(memory_space=pltpu.HBM)
SEM = pl.BlockSpec(memory_space=pltpu.SEMAPHORE)
DATAFLOW = pltpu.SideEffectType.DATAFLOW_SIDE_EFFECTING


def _device(flat):
    return flat // 4, (flat // 2) % 2, flat % 2


def _gather_copies(srcs, lands, send_sems, recv_sems, incoming):
    x, y, c = _position()
    me = 4 * x + 2 * y + c
    pairs = []
    for a in range(len(srcs)):
        for d in range(1, N_DEV):
            to, frm = (me + d) % N_DEV, (me + N_DEV - d) % N_DEV
            k = a * (N_DEV - 1) + d - 1
            sems = dict(send_sem=send_sems.at[k], recv_sem=recv_sems.at[k], device_id_type=MESH)
            out = pltpu.make_async_remote_copy(src_ref=srcs[a], dst_ref=lands[a].at[me], device_id=_device(to), **sems)
            inc = pltpu.make_async_remote_copy(src_ref=srcs[a], dst_ref=lands[a].at[frm], device_id=_device(frm),
                                               **sems) if incoming else None
            pairs.append((out, inc))
    return pairs


def _chip_copies(srcs, lands, send_sems, recv_sems, incoming, n_whole=0):
    x, y, c = _position()
    my_chip = 2 * x + y
    pairs = []
    for a in range(len(srcs)):
        for k, (px, py) in enumerate([(1 - x, y), (x, 1 - y), (1 - x, 1 - y)]):
            sem = a * (N_CHIP - 1) + k
            sems = dict(send_sem=send_sems.at[sem], recv_sem=recv_sems.at[sem], device_id=(px, py, c), device_id_type=MESH)
            src = srcs[a] if a >= len(srcs) - n_whole else srcs[a].at[2 * px + py]
            out = pltpu.make_async_remote_copy(src_ref=src, dst_ref=lands[a].at[my_chip], **sems)
            inc = pltpu.make_async_remote_copy(src_ref=src, dst_ref=lands[a].at[2 * px + py], **sems) if incoming else None
            pairs.append((out, inc))
    return pairs


def _push_start(copies, fan, srcs, lands, *, name):
    n = len(srcs)

    def body(*refs):
        src_refs, land_refs = refs[:n], refs[n:2 * n]
        send_sems, recv_sems = refs[2 * n], refs[2 * n + 1]
        token = refs[-1]
        for out, _ in copies(src_refs, land_refs, send_sems, recv_sems, False):
            out.start()
        token[...] = jnp.zeros_like(token)

    outs = pl.pallas_call(
        body, name=name,
        out_shape=(pltpu.SemaphoreType.DMA((n * fan,)), pltpu.SemaphoreType.DMA((n * fan,)),
                   *(pltpu.HBM(a.shape, a.dtype) for a in srcs), *(pltpu.HBM(a.shape, a.dtype) for a in lands),
                   jax.ShapeDtypeStruct((8, 128), F32)),
        in_specs=[HBM] * (2 * n), out_specs=(SEM, SEM, *([HBM] * (2 * n)), pl.BlockSpec(memory_space=pltpu.VMEM)),
        input_output_aliases={i: 2 + i for i in range(2 * n)},
        compiler_params=pltpu.CompilerParams(has_side_effects=DATAFLOW),
    )(*(pltpu.with_memory_space_constraint(a, pltpu.HBM) for a in (*srcs, *lands)))
    return outs[0], outs[1], list(outs[2:2 + n]), list(outs[2 + n:2 + 2 * n]), outs[-1]


def _push_wait(copies, send_sems, recv_sems, srcs, lands, after, *, name):
    n = len(srcs)

    def body(*refs):
        src_refs, land_refs = refs[:n], refs[n:2 * n]
        for out, inc in copies(src_refs, land_refs, refs[2 * n], refs[2 * n + 1], True):
            out.wait_send()
            inc.wait_recv()
        refs[-1][...] = jnp.zeros_like(refs[-1])

    outs = pl.pallas_call(
        body, name=name,
        out_shape=(*(pltpu.HBM(a.shape, a.dtype) for a in (*srcs, *lands)), jax.ShapeDtypeStruct((8, 128), F32)),
        in_specs=[HBM] * (2 * n) + [SEM, SEM, ANY], out_specs=(*([HBM] * (2 * n)), pl.BlockSpec(memory_space=pltpu.VMEM)),
        input_output_aliases={i: i for i in range(2 * n)},
        compiler_params=pltpu.CompilerParams(has_side_effects=DATAFLOW),
    )(*srcs, *lands, send_sems, recv_sems, after)
    return list(outs[n:2 * n]), outs[-1]


def _with_own_slot(block, index, slots):
    buf = lax.empty((slots,) + block.shape, block.dtype)
    return lax.dynamic_update_slice(buf, block[None], (index,) + (0,) * block.ndim)


def _view2d(shape):
    return math.prod(shape[:-1]), shape[-1]


def _pair_sum(mine, other, core, *, name, out_dtype):
    by_core = mine.ndim == 4
    n, w = other.shape[-2:]
    tr = _row_tile(n, w * 2)
    lead = other.shape[0] if by_core else 1

    def body(core_ref, a_ref, b_ref, o_ref):
        o_ref[...] = (a_ref[...].astype(F32) + b_ref[...].astype(F32)).astype(o_ref.dtype)

    if by_core:
        a_spec = pl.BlockSpec((None, None, tr, w), lambda j, i, core_ref: (j, core_ref[0], i, 0))
        o_spec = pl.BlockSpec((None, tr, w), lambda j, i, core_ref: (j, i, 0))
    else:
        a_spec = o_spec = pl.BlockSpec((tr, w), lambda j, i, core_ref: (i, 0))
    grid_spec = pltpu.PrefetchScalarGridSpec(num_scalar_prefetch=1, grid=(lead, n // tr), in_specs=[a_spec, o_spec],
                                             out_specs=o_spec)
    return pl.pallas_call(body, name=name, grid_spec=grid_spec, out_shape=jax.ShapeDtypeStruct(other.shape, out_dtype),
                          compiler_params=_params("parallel", "parallel"))(core.reshape(1), mine, other)


def _adamw(parts, w, m, v, *, name):
    layers = len(parts)
    n_parts, R, W = parts[0].shape
    tr = _row_tile(R, W * 2)
    per_layer = R // tr

    def update(p_ref, w_ref, m_ref, v_ref, g_out, d_out, m_out, v_out):
        g = p_ref[0].astype(F32)
        for j in range(1, n_parts):
            g = g + p_ref[j].astype(F32)
        m_new = ADAM_B1 * m_ref[...] + (1.0 - ADAM_B1) * g
        v_new = ADAM_B2 * v_ref[...] + (1.0 - ADAM_B2) * (g * g)
        m_hat = m_new / (1.0 - ADAM_B1 ** ADAM_STEP)
        v_hat = v_new / (1.0 - ADAM_B2 ** ADAM_STEP)
        g_out[...] = g
        d_out[...] = -ADAM_LR * (m_hat / (jnp.sqrt(v_hat) + ADAM_EPS) + ADAM_WD * w_ref[...])
        m_out[...] = m_new
        v_out[...] = v_new

    def body(*refs):
        for k in range(layers):
            pl.when(pl.program_id(0) == k)(lambda k=k: update(refs[k], *refs[layers:]))

    def parts_spec(k):
        return pl.BlockSpec((n_parts, tr, W), lambda l, i: (0, jnp.where(l == k, i, 0), 0))

    row = pl.BlockSpec((tr, W), lambda l, i: (l * per_layer + i, 0))
    out = jax.ShapeDtypeStruct((layers * R, W), F32)
    return pl.pallas_call(
        body, name=name, grid=(layers, per_layer), in_specs=[parts_spec(k) for k in range(layers)] + [row, row, row],
        out_specs=(row, row, row, row), out_shape=(out, out, out, out), compiler_params=_params("arbitrary", "arbitrary"),
    )(*parts, w, m, v)


SMALL_ROWS = 608


def _pack_small(p):
    flat = jnp.concatenate([p[n].reshape(-1).astype(F32) for n in SMALL_NAMES])
    return jnp.pad(flat, (0, SMALL_ROWS * PACK_W - flat.shape[0])).reshape(SMALL_ROWS, PACK_W)


def _unpack_small(buf, like):
    out, at = {}, 0
    flat = buf.reshape(-1)
    for n in SMALL_NAMES:
        size = math.prod(like[n].shape)
        out[n] = flat[at:at + size].reshape(like[n].shape)
        at += size
    return out


def _block_diag(pw):
    out = jnp.zeros((MAIN_WIDTH, MAIN_WIDTH), pw.dtype)
    for g in range(POOL_GROUPS):
        out = lax.dynamic_update_slice(out, pw[g], (g * POOL_GROUP_DIM, g * POOL_GROUP_DIM))
    return out


def _diag_blocks(m):
    return jnp.stack([m[g * POOL_GROUP_DIM:(g + 1) * POOL_GROUP_DIM, g * POOL_GROUP_DIM:(g + 1) * POOL_GROUP_DIM]
                      for g in range(POOL_GROUPS)])


def _train_pass(x, mem, target, p, w_kv, fetch, reduce_layer, reduce_wait):
    T = x.shape[0]
    mem_cols = MAIN_WIDTH // MEM_WIDTH
    k_gain = _head_gain(p["k_norm"], SWA_KV_HEADS)
    saved = []
    h = x
    kn = kv = h_kv = hn_kv = None
    for l in range(DEPTH):
        s = {}
        wl, token = fetch(l, h)
        s["w"] = wl
        if l == N_A:
            h_kv = h
            hn_kv, kv = _norm_mm(h, p["kv_norm"], w_kv, b_kind="rows", name="kv_proj")
            kn = _seg_rms_fwd(kv, k_gain, width=KV_HALF, col=0, name="k_norm_fwd")
        s["h"] = h
        s["xn1"], proj = _norm_mm(h, p["norm_mix"][l] + token, wl["w_in"], b_kind="rows", name="in_proj")
        s["proj"] = proj
        s["memn"] = _rms_fwd(mem, p["mem_norm"][l], name="mem_norm_fwd")
        s["mkv"] = _mm(s["memn"], wl["w_mem_kv"], b_kind="rows", name="mem_kv_proj")
        s["mk_gain"] = _head_gain(p["mem_k_norm"][l], MEM_HEADS)
        s["mkn"] = _seg_rms_fwd(s["mkv"], s["mk_gain"], width=MEM_WIDTH, col=0, name="mem_k_norm_fwd")
        if l < N_A:
            s["q_gain"] = _head_gain(p["mem_q_norm"][l], MEM_HEADS)
            s["qn"] = _seg_rms_fwd(proj, s["q_gain"], width=MEM_WIDTH, col=mem_cols, name="mem_q_norm_fwd")
            s["q_col"] = 0
        else:
            j = l - N_A
            s["q_gain"] = jnp.concatenate([_head_gain(p["q_norm"][j], SWA_Q_HEADS), _head_gain(p["mem_q_norm"][l], MEM_HEADS)],
                                          axis=1)
            s["qn"] = _seg_rms_fwd(proj, s["q_gain"], width=D_MODEL, col=0, name="q_norm_fwd")
            s["q_col"] = mem_cols
        cat = _mem_attn_fwd(s["qn"], s["q_col"], s["mkn"], s["mkv"], name="mem_attn_fwd")
        if l < N_A:
            s["mix"] = _block_diag(p["pool_w"][l])
            s["scale"] = p["pool_scale"][l].reshape(1, MAIN_WIDTH)
            s["cat"] = _pool_mix_fwd(proj, s["mix"], s["scale"], cat, name="pool_fwd")
        else:
            s["cat"] = _swa_attn_fwd(s["qn"], kn, kv, p["sinks"][l - N_A], cat, name="swa_fwd")
        s["h1"] = _mm(s["cat"], wl["w_out"], b_kind="rows", res=h, name="out_proj")
        s["xn2"], s["a"] = _norm_mm(s["h1"], p["norm_mlp"][l], wl["w_up"], b_kind="layers", relu2=True, name="mlp_up")
        h = _mm(s["a"], wl["w_down"], b_kind="rows", res=s["h1"], name="mlp_down")
        saved.append(s)

    loss, dh, dh_b = _loss(h, target, name="loss_head")

    g = {n: [None] * DEPTH for n in ("norm_mix", "mem_norm", "mem_q_norm", "mem_k_norm", "norm_mlp")}
    g.update({n: [None] * N_A for n in ("pool_w", "pool_scale", "q_norm", "sinks")})
    g_kv = None
    token = None
    dks, dvs = [], []
    for l in reversed(range(DEPTH)):
        s = saved[l]
        wl = s["w"]
        gb = {}

        def dw(a, dy, n):
            return _mm(a, dy, ta=True, out_kind="layers" if n == "w_up" else "rows", out_buf=lax.empty(wl[n].shape, BF16),
                       name=n + "_grad")

        norm_mlp_gain = p["norm_mlp"][l] if token is None else p["norm_mlp"][l] + token
        gb["w_down"] = dw(s["a"], dh_b, "w_down")
        du = _mm(dh_b, wl["w_down"], tb=True, b_kind="rows", mul2=s["a"], out_dtype=BF16, name="mlp_down_dx")
        gb["w_up"] = dw(s["xn2"], du, "w_up")
        early = reduce_layer(l, gb, early=True)
        if early is not None:
            norm_mlp_gain = norm_mlp_gain + early
        dh1, dh1_b, g["norm_mlp"][l] = _mm_rms_bwd(du, wl["w_up"], s["h1"], norm_mlp_gain, dh, b_kind="layers", also_bf16=True,
                                                   name="mlp_up_dx")
        gb["w_out"] = dw(s["cat"], dh1_b, "w_out")
        dcat = _mm(dh1_b, wl["w_out"], tb=True, b_kind="rows", name="out_proj_dx")
        if l < N_A:
            dq, dmk, dmv = _mem_attn_bwd(s["qn"], s["q_col"], s["mkn"], s["mkv"], dcat, dq_width=MEM_WIDTH, name="mem_attn_bwd")
            dproj, dmix, dscale = _pool_mix_bwd(s["proj"], s["mix"], s["scale"], dcat, name="pool_bwd")
            g["pool_w"][l] = _diag_blocks(dmix)
            g["pool_scale"][l] = dscale.reshape(MAIN_WIDTH)
            dproj, dgain = _seg_rms_bwd(s["proj"], s["q_gain"], [dq], width=MEM_WIDTH, col=mem_cols, out_buf=dproj,
                                        out_col=mem_cols, name="mem_q_norm_bwd")
            g["mem_q_norm"][l] = _fold_heads(dgain, MEM_HEADS)
        else:
            j = l - N_A
            dqn, dmk, dmv = _mem_attn_bwd(s["qn"], s["q_col"], s["mkn"], s["mkv"], dcat, dq_width=D_MODEL, name="mem_attn_bwd")
            dqn, dk_l, dv_l, dsinks = _swa_attn_bwd(s["qn"], kn, kv, p["sinks"][j], dcat, dqn, name="swa_bwd")
            dks.append(dk_l)
            dvs.append(dv_l)
            g["sinks"][j] = dsinks[0, :SWA_Q_HEADS]
            dproj, dgain = _seg_rms_bwd(s["proj"], s["q_gain"], [dqn], width=D_MODEL, col=0, name="q_norm_bwd")
            g["q_norm"][j] = _fold_heads(dgain[:, :MAIN_WIDTH], SWA_Q_HEADS)
            g["mem_q_norm"][l] = _fold_heads(dgain[:, MAIN_WIDTH:], MEM_HEADS)
        dmk_raw, dgain = _seg_rms_bwd(s["mkv"], s["mk_gain"], [dmk], width=MEM_WIDTH, col=0, name="mem_k_norm_bwd")
        g["mem_k_norm"][l] = _fold_heads(dgain, MEM_HEADS)
        dmkv = jnp.concatenate([dmk_raw, dmv.astype(BF16)], axis=1)
        gb["w_mem_kv"] = dw(s["memn"], dmkv, "w_mem_kv")
        dmemn = _mm(dmkv, wl["w_mem_kv"], tb=True, b_kind="rows", name="mem_kv_proj_dx")
        g["mem_norm"][l] = _rms_bwd(mem, p["mem_norm"][l], [dmemn], want_dx=False, name="mem_norm_bwd")
        gb["w_in"] = dw(s["xn1"], dproj, "w_in")
        if l in (0, N_A):
            dh, g["norm_mix"][l] = _mm_rms_bwd(dproj, wl["w_in"], s["h"], p["norm_mix"][l], dh1, b_kind="rows", also_bf16=False,
                                               name="in_proj_dx")
        else:
            dh, dh_b, g["norm_mix"][l] = _mm_rms_bwd(dproj, wl["w_in"], s["h"], p["norm_mix"][l], dh1, b_kind="rows",
                                                     also_bf16=True, name="in_proj_dx")
        if l == N_A:
            dkv, dgain = _seg_rms_bwd(kv, k_gain, dks, width=KV_HALF, col=0, out_buf=lax.empty((T, 2 * KV_HALF), BF16),
                                      name="k_norm_bwd")
            g["k_norm"] = _fold_heads(dgain, SWA_KV_HEADS)
            dkv = _sum_into(dvs[0], dvs[1], dkv, 1, name="dv_sum")
            g_kv = _mm(hn_kv, dkv, ta=True, out_kind="rows", out_buf=lax.empty(w_kv.shape, BF16), name="w_kv_grad")
            dh, dh_b, g["kv_norm"] = _mm_rms_bwd(dkv, w_kv, h_kv, p["kv_norm"], dh, b_kind="rows", also_bf16=True,
                                                 name="kv_proj_dx")
        if l + 1 < DEPTH:
            reduce_wait(l + 1, dh)
        token = reduce_layer(l, gb)
    grads = {n: (jnp.stack(v) if isinstance(v, list) else v) for n, v in g.items()}
    return loss, dh, grads, g_kv


def kernel(x, mem, norm_mix, w_in, pool_w, pool_scale, kv_norm, w_kv, k_norm, q_norm, sinks, mem_norm, w_mem_kv, mem_q_norm, mem_k_norm, w_out, norm_mlp, w_up, w_down, loss_target, m_norm_mix, m_w_in, m_pool_w, m_pool_scale, m_kv_norm, m_w_kv, m_k_norm, m_q_norm, m_sinks, m_mem_norm, m_w_mem_kv, m_mem_q_norm, m_mem_k_norm, m_w_out, m_norm_mlp, m_w_up, m_w_down, v_norm_mix, v_w_in, v_pool_w, v_pool_scale, v_kv_norm, v_w_kv, v_k_norm, v_q_norm, v_sinks, v_mem_norm, v_w_mem_kv, v_mem_q_norm, v_mem_k_norm, v_w_out, v_norm_mlp, v_w_up, v_w_down):
    weights = dict(norm_mix=norm_mix, w_in=w_in, pool_w=pool_w, pool_scale=pool_scale, kv_norm=kv_norm, w_kv=w_kv,
                   k_norm=k_norm, q_norm=q_norm, sinks=sinks, mem_norm=mem_norm, w_mem_kv=w_mem_kv,
                   mem_q_norm=mem_q_norm, mem_k_norm=mem_k_norm, w_out=w_out, norm_mlp=norm_mlp, w_up=w_up, w_down=w_down)
    mom1 = dict(norm_mix=m_norm_mix, w_in=m_w_in, pool_w=m_pool_w, pool_scale=m_pool_scale, kv_norm=m_kv_norm, w_kv=m_w_kv,
                k_norm=m_k_norm, q_norm=m_q_norm, sinks=m_sinks, mem_norm=m_mem_norm, w_mem_kv=m_w_mem_kv,
                mem_q_norm=m_mem_q_norm, mem_k_norm=m_mem_k_norm, w_out=m_w_out, norm_mlp=m_norm_mlp, w_up=m_w_up,
                w_down=m_w_down)
    mom2 = dict(norm_mix=v_norm_mix, w_in=v_w_in, pool_w=v_pool_w, pool_scale=v_pool_scale, kv_norm=v_kv_norm, w_kv=v_w_kv,
                k_norm=v_k_norm, q_norm=v_q_norm, sinks=v_sinks, mem_norm=v_mem_norm, w_mem_kv=v_w_mem_kv,
                mem_q_norm=v_mem_q_norm, mem_k_norm=v_mem_k_norm, w_out=v_w_out, norm_mlp=v_norm_mlp, w_up=v_w_up,
                w_down=v_w_down)
    names = list(weights)
    x_pos, y_pos, core = (lax.axis_index(n).astype(jnp.int32) for n in AXES)
    me, my_chip = 4 * x_pos + 2 * y_pos + core, 2 * x_pos + y_pos
    shard = MAIN_WIDTH // N_DEV

    def layer_shards(l, zero=0.0):
        return [(weights[n][l:l + 1] + zero).astype(BF16) for n in LAYERED]

    def usable(arrays):
        wl = dict(zip(LAYERED, arrays))
        wl["w_up"] = wl["w_up"].transpose(1, 2, 0, 3).reshape(1, D_MODEL, D_FF)
        return wl

    scale_block = jnp.pad(pool_scale, ((0, 8 - N_A), (0, 128 - shard)))
    *first, first_done = _all_gather(layer_shards(0) + [w_kv[None].astype(BF16), scale_block], name="gather_first")
    p = {n: weights[n] for n in SMALL_NAMES}
    p["pool_scale"] = first[-1][:, :N_A, :shard].transpose(1, 0, 2).reshape(N_A, MAIN_WIDTH)
    gathers, reduces, parts = {}, {}, {}

    def fetch(l, after):
        if l == 0:
            got, done = first[:len(LAYERED)], first_done
        else:
            got, done = _push_wait(_gather_copies, *gathers.pop(l), after, name=f"gather_wait_{l}")
        token = 0.0
        if l + 1 < DEPTH:
            srcs = layer_shards(l + 1, done[0, 0])
            lands = [_with_own_slot(a, me, N_DEV) for a in srcs]
            *handles, block = _push_start(_gather_copies, N_DEV - 1, srcs, lands, name=f"gather_start_{l + 1}")
            gathers[l + 1], token = handles, block[0, 0]
        return usable(got), token

    def by_core(gb):
        gb = dict(gb)
        if "w_up" in gb:
            gb["w_up"] = gb["w_up"].reshape(D_MODEL, N_DEV, D_FF // N_DEV).transpose(1, 0, 2)
        order = [n for n in LAYERED if n in gb] + [n for n in gb if n not in LAYERED]
        return {n: gb[n].reshape((N_CHIP, 2) + _view2d(gb[n].shape[1:] if n == "w_up" else gb[n].shape[2:])) for n in order}

    def pair_sums(views, sib, tag):
        return [_pair_sum(a, b, core, name=f"chip_sum_{n}_{tag}", out_dtype=BF16) for (n, a), b in zip(views.items(), sib)]

    def chip_sums(gb, tag, whole=()):
        views = by_core(gb)
        sib, sib_whole = _sibling_exchange(list(views.values()), list(whole), name="reduce_sibling_" + tag)
        return pair_sums(views, sib, tag), sib_whole

    def start_chip_exchange(sums, tag, whole=()):
        lands = [_with_own_slot(lax.dynamic_index_in_dim(a, my_chip, 0, keepdims=False), my_chip, N_CHIP) for a in sums]
        lands += [_with_own_slot(a, my_chip, N_CHIP) for a in whole]
        copies = functools.partial(_chip_copies, n_whole=len(whole))
        *handles, block = _push_start(copies, N_CHIP - 1, [*sums, *whole], lands, name="reduce_start_" + tag)
        return (copies, *handles), block[0, 0]

    mlp = ("w_up", "w_down")

    def reduce_layer(l, gb, early=False):
        if early and l > 0:
            return None
        if l == 0 and not early:
            reduces["rest"] = {n: a for n, a in gb.items() if n not in mlp}
            return None
        tag = "0_mlp" if early else str(l)
        sums, _ = chip_sums({n: gb[n] for n in mlp} if early else gb, tag)
        reduces[l], token = start_chip_exchange(sums, tag)
        return token

    def reduce_wait(l, after):
        copies, *handles = reduces.pop(l)
        return _push_wait(copies, *handles, after, name=f"reduce_wait_{l}")[0]

    def layer_wait(l, after):
        parts[l] = reduce_wait(l, after)

    loss, grad_x, grads, g_kv = _train_pass(x[0], mem[0], loss_target[0], p, first[len(LAYERED)], fetch, reduce_layer, layer_wait)

    last = dict(reduces.pop("rest"))
    last["w_kv"] = g_kv
    last["pool_scale"] = grads["pool_scale"].reshape(N_A, N_DEV, shard).transpose(1, 0, 2).astype(BF16)[:, None]
    small = _pack_small(grads)
    sums, (sib_small,) = chip_sums(last, "0", whole=[small])
    chip_small = _pair_sum(small, sib_small, core, name="chip_sum_small", out_dtype=F32)
    reduces["rest"], _ = start_chip_exchange(sums, "0_rest", whole=[chip_small])

    def adamw(n, n_parts):
        res = _adamw(n_parts, *(d[n].reshape(_view2d(d[n].shape)) for d in (weights, mom1, mom2)), name="adamw_" + n)
        return [r.reshape(weights[n].shape) for r in res]

    p_up, p_down = reduce_wait(0, chip_small)
    parts[0] = [None, None, None, p_up, p_down]
    new = {n: adamw(n, [parts[l][LAYERED.index(n)] for l in range(DEPTH)]) for n in mlp}
    p_in, p_mem_kv, p_out, parts_kv, parts_scale, parts_small = reduce_wait("rest", new["w_down"][0])
    parts[0][:3] = [p_in, p_mem_kv, p_out]
    new.update({n: adamw(n, [parts[l][k] for l in range(DEPTH)]) for k, n in enumerate(LAYERED) if n not in mlp})
    new["w_kv"] = adamw("w_kv", [parts_kv])
    new["pool_scale"] = adamw("pool_scale", [parts_scale])
    res = _adamw([parts_small], _pack_small(weights), _pack_small(mom1), _pack_small(mom2), name="adamw_replicated")
    for n, vals in zip(SMALL_NAMES, zip(*(_unpack_small(r, weights).values() for r in res))):
        new[n] = list(vals)
    outs = [new[n][k] for k in range(4) for n in names]
    total = lax.psum(loss[0, 0], AXES)
    return (total, grad_x[None], *outs)
```

```python
import functools
import math

import jax
import jax.numpy as jnp
from jax import lax
from jax.experimental import pallas as pl
from jax.experimental.pallas import tpu as pltpu

F32 = jnp.float32
BF16 = jnp.bfloat16
MESH = pl.DeviceIdType.MESH
AXES = ("x", "y", "c")

D_MODEL = 1024
DEPTH = 4
N_A = 2
HEAD_DIM = 64
MEM_HEADS = 4
MEM_WIDTH = MEM_HEADS * HEAD_DIM
MAIN_WIDTH = D_MODEL - MEM_WIDTH
POOL_GROUPS = 4
POOL_GROUP_DIM = MAIN_WIDTH // POOL_GROUPS
POOL_HALO = 16
SWA_Q_HEADS = MAIN_WIDTH // HEAD_DIM
SWA_KV_HEADS = 4
SWA_GROUP = SWA_Q_HEADS // SWA_KV_HEADS
KV_HALF = SWA_KV_HEADS * HEAD_DIM
BLOCK = 128
D_FF = 4 * D_MODEL
EPS = 1e-6
SCALE = HEAD_DIM ** -0.5
NEG = float(jnp.finfo(jnp.float32).min)
N_DEV = 8
N_CHIP = 4

ADAM_LR = 0.001
ADAM_B1 = 0.9
ADAM_B2 = 0.999
ADAM_EPS = 1e-08
ADAM_WD = 0.01
ADAM_STEP = 10

PACK_W = 512
VMEM_LIMIT = 52 * 1024 * 1024
MM_TILE = 1024
LAYERED = ("w_in", "w_mem_kv", "w_out", "w_up", "w_down")
PREFETCH = {0: (1, 2), 1: (3,)}
SMALL_NAMES = ("norm_mix", "pool_w", "kv_norm", "k_norm", "q_norm", "sinks", "mem_norm", "mem_q_norm", "mem_k_norm",
               "norm_mlp")


ANY = pl.BlockSpec(memory_space=pl.ANY)


def _params(*sem):
    return pltpu.CompilerParams(dimension_semantics=sem, vmem_limit_bytes=VMEM_LIMIT)


def _mm(a, b, *, name, ta=False, tb=False, b_kind=None, layer=0, res=None, relu2=False, mul2=None, out_dtype=F32,
        out_kind=None, out_buf=None):
    if ta:
        K, M = a.shape
    else:
        M, K = a.shape
    if b_kind is None:
        rows_b, cols_b = b.shape
    elif b_kind == "rows":
        rows_b, cols_b = b.shape[0] * b.shape[2], b.shape[3]
    else:
        rows_b, cols_b = b.shape[1:]
    N, K2 = (rows_b, cols_b) if tb else (cols_b, rows_b)
    assert K == K2, (a.shape, b.shape)
    tm = min(M, MM_TILE if K <= MM_TILE else MM_TILE // 2)
    tn = min(N, MM_TILE)
    assert M % tm == 0 and N % tn == 0
    row_tile, col_tile = (tn, K) if tb else (K, tn)
    a_spec = pl.BlockSpec((K, tm), lambda j, i: (0, i)) if ta else pl.BlockSpec((tm, K), lambda j, i: (i, 0))

    def rc(j):
        return (j, 0) if tb else (0, j)

    if b_kind is None:
        b_spec = pl.BlockSpec((row_tile, col_tile), lambda j, i: rc(j))
    elif b_kind == "rows":
        per = row_tile // b.shape[2]
        b_spec = pl.BlockSpec((per, None, b.shape[2], col_tile), lambda j, i: (rc(j)[0], layer, 0, rc(j)[1]))
    else:
        b_spec = pl.BlockSpec((None, row_tile, col_tile), lambda j, i: (layer, *rc(j)))
    o_spec = pl.BlockSpec((tm, tn), lambda j, i: (i, j))
    dn = (((0 if ta else 1,), (1 if tb else 0,)), ((), ()))
    extra = [e for e in (res, mul2) if e is not None]
    n_in = 2 + len(extra) + (1 if out_buf is not None else 0)

    def body(*refs):
        a_ref, b_ref = refs[0], refs[1]
        extra_refs = refs[2:2 + len(extra)]
        out = refs[n_in]
        bv = b_ref[...].astype(BF16).reshape(row_tile, col_tile)
        v = lax.dot_general(a_ref[...].astype(BF16), bv, dn, preferred_element_type=F32)
        if res is not None:
            v = extra_refs[0][...] + v
        elif mul2 is not None:
            v = v * (2.0 * jnp.sqrt(extra_refs[0][...].astype(F32)))
        if relu2:
            r = jnp.maximum(v, 0.0)
            v = r * r
        out[...] = v.astype(out.dtype).reshape(out.shape)

    in_specs = [a_spec, b_spec] + [o_spec] * len(extra)
    operands = [a, b, *extra]
    aliases = {}
    if out_kind is None:
        out_shape = jax.ShapeDtypeStruct((M, N), BF16 if relu2 else out_dtype)
        out_specs = o_spec
    else:
        if out_kind == "rows":
            s = out_buf.shape[2]
            out_specs = pl.BlockSpec((tm // s, None, s, tn), lambda j, i: (i, layer, 0, j))
        else:
            out_specs = pl.BlockSpec((None, tm, tn), lambda j, i: (layer, i, j))
        out_shape = jax.ShapeDtypeStruct(out_buf.shape, out_buf.dtype)
        in_specs.append(ANY)
        operands.append(out_buf)
        aliases = {len(operands) - 1: 0}
    return pl.pallas_call(
        body, name=name, grid=(N // tn, M // tm), in_specs=in_specs, out_specs=out_specs, out_shape=out_shape,
        input_output_aliases=aliases, compiler_params=_params("parallel", "parallel"),
    )(*operands)


def _weight_block(b, b_kind, transposed, tn):
    if b_kind == "rows":
        s = b.shape[2]
        rows, cols = b.shape[0] * s, b.shape[3]
        if transposed:
            return (lambda at: pl.BlockSpec((b.shape[0], None, s, cols), lambda *g: (0, 0, 0, 0))), rows, cols
        return (lambda at: pl.BlockSpec((b.shape[0], None, s, tn), lambda *g: (0, 0, 0, at(*g)))), rows, cols
    rows, cols = b.shape[1:]
    if transposed:
        return (lambda at: pl.BlockSpec((None, rows, cols), lambda *g: (0, 0, 0))), rows, cols
    return (lambda at: pl.BlockSpec((None, rows, tn), lambda *g: (0, 0, at(*g)))), rows, cols


def _norm_mm(x, gain, b, *, b_kind, name, relu2=False):
    M, K = x.shape
    tm = min(M, MM_TILE)
    spec_of, rows, N = _weight_block(b, b_kind, False, min(MM_TILE, b.shape[-1]))
    tn = min(N, MM_TILE)
    assert rows == K and M % tm == 0 and N % tn == 0

    def body(x_ref, g_ref, b_ref, xn_ref, o_ref):
        @pl.when(pl.program_id(1) == 0)
        def _():
            xv = x_ref[...]
            r = lax.rsqrt(jnp.mean(xv * xv, axis=-1, keepdims=True) + EPS)
            xn_ref[...] = ((xv * r) * g_ref[...]).astype(xn_ref.dtype)

        v = jnp.dot(xn_ref[...], b_ref[...].astype(BF16).reshape(K, tn), preferred_element_type=F32)
        if relu2:
            r2 = jnp.maximum(v, 0.0)
            v = r2 * r2
        o_ref[...] = v.astype(o_ref.dtype)

    rows_spec = pl.BlockSpec((tm, K), lambda i, j: (i, 0))
    return pl.pallas_call(
        body, name=name, grid=(M // tm, N // tn),
        in_specs=[rows_spec, pl.BlockSpec((1, K), lambda i, j: (0, 0)), spec_of(lambda i, j: j)],
        out_specs=(rows_spec, pl.BlockSpec((tm, tn), lambda i, j: (i, j))),
        out_shape=(jax.ShapeDtypeStruct((M, K), BF16), jax.ShapeDtypeStruct((M, N), BF16 if relu2 else F32)),
        compiler_params=_params("parallel", "arbitrary"),
    )(x, gain.reshape(1, K), b)


def _mm_rms_bwd(a, b, x, gain, res, *, b_kind, name, also_bf16):
    M, K = a.shape
    spec_of, N, cols = _weight_block(b, b_kind, True, None)
    assert cols == K and x.shape == (M, N)
    tm = min(M, MM_TILE if K <= MM_TILE else MM_TILE // 2)
    assert M % tm == 0

    def body(a_ref, b_ref, x_ref, g_ref, res_ref, *outs):
        i = pl.program_id(0)
        dy = lax.dot_general(a_ref[...].astype(BF16), b_ref[...].astype(BF16).reshape(N, K), (((1,), (1,)), ((), ())),
                             preferred_element_type=F32)
        xv = x_ref[...]
        r = lax.rsqrt(jnp.mean(xv * xv, axis=-1, keepdims=True) + EPS)
        xh = xv * r
        part = jnp.sum(dy * xh, axis=0, keepdims=True)
        dg_ref = outs[-1]

        @pl.when(i == 0)
        def _():
            dg_ref[...] = part

        @pl.when(i > 0)
        def _():
            dg_ref[...] += part

        gdy = dy * g_ref[...]
        dx = res_ref[...] + r * (gdy - xh * jnp.mean(gdy * xh, axis=-1, keepdims=True))
        outs[0][...] = dx
        if also_bf16:
            outs[1][...] = dx.astype(BF16)

    row = pl.BlockSpec((tm, N), lambda i: (i, 0))
    vec = pl.BlockSpec((1, N), lambda i: (0, 0))
    out_specs = [row] + ([row] if also_bf16 else []) + [vec]
    out_shape = [jax.ShapeDtypeStruct((M, N), F32)] + ([jax.ShapeDtypeStruct((M, N), BF16)] if also_bf16 else [])
    outs = pl.pallas_call(
        body, name=name, grid=(M // tm,),
        in_specs=[pl.BlockSpec((tm, K), lambda i: (i, 0)), spec_of(None), row, vec, row], out_specs=out_specs,
        out_shape=out_shape + [jax.ShapeDtypeStruct((1, N), F32)], compiler_params=_params("arbitrary"),
    )(a, b, x, gain.reshape(1, N), res)
    return (*outs[:-1], outs[-1].reshape(N))


def _row_tile(rows, d):
    t = min(rows, (512 * 1024) // d)
    while rows % t or (t != rows and t % 16):
        t -= 1
    return t


def _rms_fwd(x, g, *, name, out_dtype=BF16):
    R, D = x.shape
    tr = _row_tile(R, D)

    def body(x_ref, g_ref, o_ref):
        xv = x_ref[...].astype(F32)
        r = lax.rsqrt(jnp.mean(xv * xv, axis=-1, keepdims=True) + EPS)
        o_ref[...] = ((xv * r) * g_ref[...]).astype(o_ref.dtype)

    return pl.pallas_call(
        body, name=name, grid=(R // tr,),
        in_specs=[pl.BlockSpec((tr, D), lambda i: (i, 0)), pl.BlockSpec((1, D), lambda i: (0, 0))],
        out_specs=pl.BlockSpec((tr, D), lambda i: (i, 0)), out_shape=jax.ShapeDtypeStruct((R, D), out_dtype),
        compiler_params=_params("parallel"),
    )(x, g.reshape(1, D))


def _rms_bwd(x, g, dys, *, name, res=None, want_dx=True, also_bf16=False):
    R, D = x.shape
    tr = _row_tile(R, D)
    n_dy = len(dys)
    has_res = res is not None

    def body(*refs):
        x_ref, g_ref = refs[0], refs[1]
        dy_refs = refs[2:2 + n_dy]
        res_ref = refs[2 + n_dy] if has_res else None
        outs = refs[2 + n_dy + (1 if has_res else 0):]
        dg_ref = outs[-1]
        i = pl.program_id(0)
        xv = x_ref[...].astype(F32)
        dy = dy_refs[0][...].astype(F32)
        for extra in dy_refs[1:]:
            dy = dy + extra[...].astype(F32)
        r = lax.rsqrt(jnp.mean(xv * xv, axis=-1, keepdims=True) + EPS)
        xh = xv * r
        part = jnp.sum(dy * xh, axis=0, keepdims=True)

        @pl.when(i == 0)
        def _():
            dg_ref[...] = part

        @pl.when(i > 0)
        def _():
            dg_ref[...] += part

        if want_dx:
            gdy = dy * g_ref[...]
            dx = r * (gdy - xh * jnp.mean(gdy * xh, axis=-1, keepdims=True))
            if has_res:
                dx = res_ref[...] + dx
            outs[0][...] = dx
            if also_bf16:
                outs[1][...] = dx.astype(BF16)

    row = pl.BlockSpec((tr, D), lambda i: (i, 0))
    vec = pl.BlockSpec((1, D), lambda i: (0, 0))
    out_shape = [jax.ShapeDtypeStruct((1, D), F32)]
    out_specs = [vec]
    if also_bf16:
        out_shape = [jax.ShapeDtypeStruct((R, D), BF16)] + out_shape
        out_specs = [row] + out_specs
    if want_dx:
        out_shape = [jax.ShapeDtypeStruct((R, D), F32)] + out_shape
        out_specs = [row] + out_specs
    outs = pl.pallas_call(
        body, name=name, grid=(R // tr,),
        in_specs=[row, vec] + [row] * (n_dy + (1 if has_res else 0)), out_specs=out_specs, out_shape=out_shape,
        compiler_params=_params("arbitrary"),
    )(x, g.reshape(1, D), *dys, *([res] if has_res else []))
    return (*outs[:-1], outs[-1].reshape(D)) if want_dx else outs[0].reshape(D)


POOL_TILE = 512
MEM_Q_TILE = 512


def _softmax(q, k, bias, valid, sink):
    s = lax.dot_general(q, k, (((1,), (1,)), ((), ())), preferred_element_type=F32) * SCALE
    if bias is not None:
        s = s - bias
    if valid is not None:
        s = jnp.where(valid, s, NEG)
    m = jnp.max(s, axis=-1, keepdims=True)
    if sink is not None:
        m = jnp.maximum(m, sink)
    e = jnp.exp(s - m)
    z = jnp.sum(e, axis=-1, keepdims=True)
    if sink is None:
        return e * (1.0 / z), None
    es = jnp.exp(sink - m)
    inv = 1.0 / (z + es)
    return e * inv, es * inv


LANES = 128


def _seg_mean(v):
    r = lax.broadcasted_iota(jnp.int32, (LANES, LANES), 0) // HEAD_DIM
    c = lax.broadcasted_iota(jnp.int32, (LANES, LANES), 1) // HEAD_DIM
    seg = jnp.where(r == c, 1.0 / HEAD_DIM, 0.0).astype(BF16)
    hi = v.astype(BF16)
    lo = (v - hi.astype(F32)).astype(BF16)
    parts = []
    for g in range(v.shape[1] // LANES):
        sl = slice(g * LANES, (g + 1) * LANES)
        parts.append(jnp.dot(hi[:, sl], seg, preferred_element_type=F32) + jnp.dot(lo[:, sl], seg, preferred_element_type=F32))
    return parts[0] if len(parts) == 1 else jnp.concatenate(parts, axis=1)


def _cols(rows, width, col):
    return pl.BlockSpec((rows, width), lambda i: (i, col))


def _head_gain(g, heads):
    return jnp.tile(g, heads).reshape(1, heads * HEAD_DIM)


def _fold_heads(dg, heads):
    return dg.reshape(heads, HEAD_DIM).sum(axis=0)


def _seg_rms_fwd(x, gain, *, width, col, name):
    R = x.shape[0]
    tr = _row_tile(R, width)

    def body(x_ref, g_ref, o_ref):
        xv = x_ref[...]
        r = lax.rsqrt(_seg_mean(xv * xv) + EPS)
        o_ref[...] = ((xv * r) * g_ref[...]).astype(o_ref.dtype)

    return pl.pallas_call(
        body, name=name, grid=(R // tr,), in_specs=[_cols(tr, width, col), pl.BlockSpec((1, width), lambda i: (0, 0))],
        out_specs=_cols(tr, width, 0), out_shape=jax.ShapeDtypeStruct((R, width), BF16), compiler_params=_params("parallel"),
    )(x, gain)


def _seg_rms_bwd(x, gain, dys, *, width, col, name, out_buf=None, out_col=0):
    R = x.shape[0]
    tr = _row_tile(R, width)
    n_dy = len(dys)

    def body(*refs):
        x_ref, g_ref = refs[0], refs[1]
        dy_refs = refs[2:2 + n_dy]
        dx_ref, dg_ref = refs[-2], refs[-1]
        i = pl.program_id(0)
        xv = x_ref[...]
        dy = dy_refs[0][...]
        for extra in dy_refs[1:]:
            dy = dy + extra[...]
        r = lax.rsqrt(_seg_mean(xv * xv) + EPS)
        xh = xv * r
        part = jnp.sum(dy * xh, axis=0, keepdims=True)

        @pl.when(i == 0)
        def _():
            dg_ref[...] = part

        @pl.when(i > 0)
        def _():
            dg_ref[...] += part

        gdy = dy * g_ref[...]
        dx_ref[...] = (r * (gdy - xh * _seg_mean(gdy * xh))).astype(dx_ref.dtype)

    vec = pl.BlockSpec((1, width), lambda i: (0, 0))
    in_specs = [_cols(tr, width, col), vec] + [_cols(tr, width, 0)] * n_dy
    operands = [x, gain, *dys]
    aliases = {}
    dx_shape = jax.ShapeDtypeStruct((R, width), BF16)
    if out_buf is not None:
        in_specs.append(ANY)
        operands.append(out_buf)
        aliases = {len(operands) - 1: 0}
        dx_shape = jax.ShapeDtypeStruct(out_buf.shape, out_buf.dtype)
    return pl.pallas_call(
        body, name=name, grid=(R // tr,), in_specs=in_specs, out_specs=(_cols(tr, width, out_col), vec),
        out_shape=(dx_shape, jax.ShapeDtypeStruct((1, width), F32)), input_output_aliases=aliases,
        compiler_params=_params("arbitrary"),
    )(*operands)


def _sum_into(a, b, out_buf, out_col, *, name):
    R, width = a.shape
    tr = _row_tile(R, width)

    def body(a_ref, b_ref, _, o_ref):
        o_ref[...] = (a_ref[...] + b_ref[...]).astype(o_ref.dtype)

    return pl.pallas_call(
        body, name=name, grid=(R // tr,), in_specs=[_cols(tr, width, 0), _cols(tr, width, 0), ANY],
        out_specs=_cols(tr, width, out_col), out_shape=jax.ShapeDtypeStruct(out_buf.shape, out_buf.dtype),
        input_output_aliases={2: 0}, compiler_params=_params("parallel"),
    )(a, b, out_buf)


def _pool_lane_group():
    return lax.broadcasted_iota(jnp.int32, (1, MAIN_WIDTH), 1) // POOL_GROUP_DIM


def _pool_pick(group, per_window):
    s1, s2, s3, s4 = per_window
    return jnp.where(group == 0, s1, jnp.where(group == 1, s2, jnp.where(group == 2, s3, s4)))


def _pool_delta(u_ref, halo_ref, tile):
    group = _pool_lane_group()
    halo = jnp.where(tile == 0, 0.0, halo_ref[...])
    ext = jnp.concatenate([halo, u_ref[...]], axis=0)
    n = ext.shape[0]
    s1 = ext + pltpu.roll(ext, 1, 0)
    s2 = s1 + pltpu.roll(s1, 2, 0)
    s3 = s2 + pltpu.roll(s2, 4, 0)
    s4 = s3 + pltpu.roll(s3, 8, 0)
    ws = _pool_pick(group, (s1, s2, s3, s4))[POOL_HALO:n]
    t = tile * POOL_TILE + lax.broadcasted_iota(jnp.int32, (POOL_TILE, 1), 0)
    cnt = jnp.minimum(t + 1, _pool_pick(group, (2, 4, 8, 16))).astype(F32)
    return ws / cnt - u_ref[...], cnt


def _pool_in_specs():
    per_tile = POOL_TILE // POOL_HALO
    cur = _cols(POOL_TILE, MAIN_WIDTH, 0)
    prev = pl.BlockSpec((POOL_HALO, MAIN_WIDTH), lambda i: (jnp.maximum(i * per_tile - 1, 0), 0))
    mix = pl.BlockSpec((MAIN_WIDTH, MAIN_WIDTH), lambda i: (0, 0))
    vec = pl.BlockSpec((1, MAIN_WIDTH), lambda i: (0, 0))
    return cur, prev, mix, vec


def _pool_mix_fwd(proj, mix, scale, cat, *, name):
    T = proj.shape[0]
    assert T % POOL_TILE == 0
    cur, prev, mix_spec, vec = _pool_in_specs()

    def body(u_ref, halo_ref, mix_ref, sc_ref, _, o_ref):
        d, _cnt = _pool_delta(u_ref, halo_ref, pl.program_id(0))
        mixed = jnp.dot(d.astype(BF16), mix_ref[...].astype(BF16), preferred_element_type=F32)
        o_ref[...] = (mixed * sc_ref[...]).astype(o_ref.dtype)

    return pl.pallas_call(
        body, name=name, grid=(T // POOL_TILE,), in_specs=[cur, prev, mix_spec, vec, ANY], out_specs=cur,
        out_shape=jax.ShapeDtypeStruct(cat.shape, cat.dtype), input_output_aliases={4: 0}, compiler_params=_params("parallel"),
    )(proj, proj, mix, scale, cat)


def _pool_mix_bwd(proj, mix, scale, dcat, *, name):
    T = proj.shape[0]
    nt = T // POOL_TILE
    per_tile = POOL_TILE // POOL_HALO
    cur, prev, mix_spec, vec = _pool_in_specs()
    nxt = pl.BlockSpec((POOL_HALO, MAIN_WIDTH), lambda i: (jnp.minimum((i + 1) * per_tile, nt * per_tile - 1), 0))

    def body(u_ref, halo_ref, mix_ref, sc_ref, do_ref, donext_ref, du_ref, dmix_ref, dsc_ref):
        tile = pl.program_id(0)
        group = _pool_lane_group()
        d, cnt = _pool_delta(u_ref, halo_ref, tile)
        mixb = mix_ref[...].astype(BF16)
        db = d.astype(BF16)
        mixed = jnp.dot(db, mixb, preferred_element_type=F32)
        dout = do_ref[...]
        dsc = jnp.sum(dout * mixed, axis=0, keepdims=True)
        sc = sc_ref[...]
        dmixed = (dout * sc).astype(BF16)
        dmix = lax.dot_general(db, dmixed, (((0,), (0,)), ((), ())), preferred_element_type=F32)

        @pl.when(tile == 0)
        def _():
            dmix_ref[...] = dmix
            dsc_ref[...] = dsc

        @pl.when(tile > 0)
        def _():
            dmix_ref[...] += dmix
            dsc_ref[...] += dsc

        dnext = jnp.where(tile == nt - 1, 0.0, donext_ref[...])
        dmixed_ext = jnp.concatenate([dmixed, (dnext * sc).astype(BF16)], axis=0)
        dd_ext = lax.dot_general(dmixed_ext, mixb, (((1,), (1,)), ((), ())), preferred_element_type=F32)
        window = _pool_pick(group, (2.0, 4.0, 8.0, 16.0))
        cnt_ext = jnp.concatenate([cnt, jnp.broadcast_to(window, (POOL_HALO, MAIN_WIDTH))], axis=0)
        q = dd_ext / cnt_ext
        n = q.shape[0]
        r1 = q + pltpu.roll(q, n - 1, 0)
        r2 = r1 + pltpu.roll(r1, n - 2, 0)
        r3 = r2 + pltpu.roll(r2, n - 4, 0)
        r4 = r3 + pltpu.roll(r3, n - 8, 0)
        back = _pool_pick(group, (r1, r2, r3, r4))
        du_ref[...] = (back[0:POOL_TILE] - dd_ext[0:POOL_TILE]).astype(du_ref.dtype)

    return pl.pallas_call(
        body, name=name, grid=(nt,), in_specs=[cur, prev, mix_spec, vec, cur, nxt], out_specs=(cur, mix_spec, vec),
        out_shape=(jax.ShapeDtypeStruct((T, D_MODEL), BF16), jax.ShapeDtypeStruct((MAIN_WIDTH, MAIN_WIDTH), F32),
                   jax.ShapeDtypeStruct((1, MAIN_WIDTH), F32)),
        compiler_params=_params("arbitrary"),
    )(proj, proj, mix, scale, dcat, dcat)


def _head(a, h):
    return a[:, h * HEAD_DIM:(h + 1) * HEAD_DIM]


def _swa_mask(blk):
    rows = SWA_GROUP * BLOCK
    qi = lax.broadcasted_iota(jnp.int32, (rows, 2 * BLOCK), 0) % BLOCK
    kj = lax.broadcasted_iota(jnp.int32, (rows, 2 * BLOCK), 1)
    dist = qi + BLOCK - kj
    valid = (dist >= 0) & (dist < BLOCK) & ((blk > 0) | (kj >= BLOCK))
    return dist.astype(F32), valid


def _swa_head_terms(sink_ref, kvh, dist):
    grp = lax.broadcasted_iota(jnp.int32, (SWA_GROUP * BLOCK, 1), 0) // BLOCK
    slopes = [2.0 ** (-8.0 * (kvh * SWA_GROUP + g + 1) / SWA_Q_HEADS) for g in range(SWA_GROUP)]
    sinks = [sink_ref[kvh * SWA_GROUP + g] for g in range(SWA_GROUP)]
    slope = jnp.where(grp == 0, slopes[0], jnp.where(grp == 1, slopes[1], slopes[2]))
    sink = jnp.where(grp == 0, sinks[0], jnp.where(grp == 1, sinks[1], sinks[2]))
    return slope * dist, sink


def _stack_heads(a, kvh):
    return jnp.concatenate([_head(a, kvh * SWA_GROUP + g) for g in range(SWA_GROUP)], axis=0)


def _swa_specs(nb):
    def at(n):
        return jnp.minimum(n, nb - 1)

    q = pl.BlockSpec((BLOCK, MAIN_WIDTH), lambda n: (at(n), 0))
    k_prev = pl.BlockSpec((BLOCK, KV_HALF), lambda n: (jnp.maximum(at(n) - 1, 0), 0))
    k_cur = pl.BlockSpec((BLOCK, KV_HALF), lambda n: (at(n), 0))
    v_prev = pl.BlockSpec((BLOCK, KV_HALF), lambda n: (jnp.maximum(at(n) - 1, 0), 1))
    v_cur = pl.BlockSpec((BLOCK, KV_HALF), lambda n: (at(n), 1))
    return q, k_prev, k_cur, v_prev, v_cur


def _swa_attn_fwd(qn, kn, kv, sinks, cat, *, name):
    T = qn.shape[0]
    nb = T // BLOCK
    q_spec, k_prev, k_cur, v_prev, v_cur = _swa_specs(nb)

    def body(sink_ref, q_ref, kp_ref, kc_ref, vp_ref, vc_ref, _, o_ref):
        dist, valid = _swa_mask(pl.program_id(0))
        kk = jnp.concatenate([kp_ref[...], kc_ref[...]], axis=0)
        vv = jnp.concatenate([vp_ref[...], vc_ref[...]], axis=0).astype(BF16)
        q = q_ref[...]
        outs = []
        for kvh in range(SWA_KV_HEADS):
            bias, sink = _swa_head_terms(sink_ref, kvh, dist)
            p, _ps = _softmax(_stack_heads(q, kvh), _head(kk, kvh), bias, valid, sink)
            o = jnp.dot(p.astype(BF16), _head(vv, kvh), preferred_element_type=F32)
            outs += [o[g * BLOCK:(g + 1) * BLOCK] for g in range(SWA_GROUP)]
        o_ref[...] = jnp.concatenate(outs, axis=1).astype(o_ref.dtype)

    return pl.pallas_call(
        body, name=name, grid=(nb,),
        in_specs=[pl.BlockSpec(memory_space=pltpu.SMEM), q_spec, k_prev, k_cur, v_prev, v_cur, ANY], out_specs=q_spec,
        out_shape=jax.ShapeDtypeStruct(cat.shape, cat.dtype), input_output_aliases={6: 0}, compiler_params=_params("parallel"),
    )(sinks, qn, kn, kn, kv, kv, cat)


def _swa_attn_bwd(qn, kn, kv, sinks, dcat, dqn, *, name):
    T = qn.shape[0]
    nb = T // BLOCK
    q_spec, k_prev, k_cur, v_prev, v_cur = _swa_specs(nb)
    late = pl.BlockSpec((BLOCK, KV_HALF), lambda n: (jnp.maximum(n - 1, 0), 0))
    tn_dims = (((0,), (0,)), ((), ()))

    def body(sink_ref, q_ref, do_ref, kp_ref, kc_ref, vp_ref, vc_ref, _, dq_ref, dk_ref, dv_ref, ds_ref, ck, cv):
        blk = pl.program_id(0)

        @pl.when(blk == 0)
        def _():
            ck[...] = jnp.zeros_like(ck)
            cv[...] = jnp.zeros_like(cv)
            ds_ref[...] = jnp.zeros_like(ds_ref)

        @pl.when(blk < nb)
        def _():
            dist, valid = _swa_mask(blk)
            kk = jnp.concatenate([kp_ref[...], kc_ref[...]], axis=0)
            vv = jnp.concatenate([vp_ref[...], vc_ref[...]], axis=0).astype(BF16)
            q = q_ref[...]
            dout = do_ref[...].astype(BF16)
            lane = lax.broadcasted_iota(jnp.int32, (1, LANES), 1)
            dsinks = jnp.zeros((1, LANES), F32)
            dqs, dks, dvs = [], [], []
            for kvh in range(SWA_KV_HEADS):
                bias, sink = _swa_head_terms(sink_ref, kvh, dist)
                qq, kh, vh, dd = _stack_heads(q, kvh), _head(kk, kvh), _head(vv, kvh), _stack_heads(dout, kvh)
                p, ps = _softmax(qq, kh, bias, valid, sink)
                dp = lax.dot_general(dd, vh, (((1,), (1,)), ((), ())), preferred_element_type=F32)
                dsum = jnp.sum(p * dp, axis=-1, keepdims=True)
                ds = (p * (dp - dsum)).astype(BF16)
                dq = jnp.dot(ds, kh, preferred_element_type=F32) * SCALE
                dqs += [dq[g * BLOCK:(g + 1) * BLOCK] for g in range(SWA_GROUP)]
                dks.append(lax.dot_general(qq, ds, tn_dims, preferred_element_type=F32) * SCALE)
                dvs.append(lax.dot_general(dd, p.astype(BF16), tn_dims, preferred_element_type=F32))
                dsink = -(ps * dsum)
                for g in range(SWA_GROUP):
                    dsinks = dsinks + jnp.where(lane == kvh * SWA_GROUP + g, jnp.sum(dsink[g * BLOCK:(g + 1) * BLOCK]), 0.0)
            dq_ref[...] = jnp.concatenate(dqs, axis=1)
            dk = jnp.concatenate(dks, axis=0).T
            dv = jnp.concatenate(dvs, axis=0).T
            dk_ref[...] = ck[...] + dk[0:BLOCK]
            dv_ref[...] = cv[...] + dv[0:BLOCK]
            ck[...] = dk[BLOCK:2 * BLOCK]
            cv[...] = dv[BLOCK:2 * BLOCK]
            ds_ref[...] += dsinks

        @pl.when(blk == nb)
        def _():
            dk_ref[...] = ck[...]
            dv_ref[...] = cv[...]

    return pl.pallas_call(
        body, name=name, grid=(nb + 1,),
        in_specs=[pl.BlockSpec(memory_space=pltpu.SMEM), q_spec, q_spec, k_prev, k_cur, v_prev, v_cur, ANY],
        out_specs=(q_spec, late, late, pl.BlockSpec((1, LANES), lambda n: (0, 0))),
        out_shape=(jax.ShapeDtypeStruct(dqn.shape, dqn.dtype), jax.ShapeDtypeStruct((T, KV_HALF), F32),
                   jax.ShapeDtypeStruct((T, KV_HALF), F32), jax.ShapeDtypeStruct((1, LANES), F32)),
        scratch_shapes=[pltpu.VMEM((BLOCK, KV_HALF), F32), pltpu.VMEM((BLOCK, KV_HALF), F32)],
        input_output_aliases={7: 0}, compiler_params=_params("arbitrary"),
    )(sinks, qn, dcat, kn, kn, kv, kv, dqn)


def _mem_specs(M, tq, q_col):
    q = _cols(tq, MEM_WIDTH, q_col)
    k = pl.BlockSpec((M, MEM_WIDTH), lambda i: (0, 0))
    v = pl.BlockSpec((M, MEM_WIDTH), lambda i: (0, 1))
    return q, k, v


def _mem_attn_fwd(q, q_col, mkn, mkv, *, name):
    T = q.shape[0]
    M = mkn.shape[0]
    tq = min(T, MEM_Q_TILE)
    q_spec, k_spec, v_spec = _mem_specs(M, tq, q_col)

    def body(q_ref, k_ref, v_ref, o_ref):
        qq, kk, vv = q_ref[...], k_ref[...], v_ref[...].astype(BF16)
        outs = []
        for h in range(MEM_HEADS):
            p, _ps = _softmax(_head(qq, h), _head(kk, h), None, None, None)
            outs.append(jnp.dot(p.astype(BF16), _head(vv, h), preferred_element_type=F32))
        o_ref[...] = jnp.concatenate(outs, axis=1).astype(o_ref.dtype)

    return pl.pallas_call(
        body, name=name, grid=(T // tq,), in_specs=[q_spec, k_spec, v_spec], out_specs=_cols(tq, MEM_WIDTH, MAIN_WIDTH // MEM_WIDTH),
        out_shape=jax.ShapeDtypeStruct((T, D_MODEL), BF16), compiler_params=_params("parallel"),
    )(q, mkn, mkv)


def _mem_attn_bwd(q, q_col, mkn, mkv, dcat, *, dq_width, name):
    T = q.shape[0]
    M = mkn.shape[0]
    tq = min(T, MEM_Q_TILE)
    q_spec, k_spec, v_spec = _mem_specs(M, tq, q_col)
    last = MAIN_WIDTH // MEM_WIDTH
    tn_dims = (((0,), (0,)), ((), ()))

    def body(q_ref, do_ref, k_ref, v_ref, dq_ref, dk_ref, dv_ref):
        i = pl.program_id(0)
        qq, kk, vv, dout = q_ref[...], k_ref[...], v_ref[...].astype(BF16), do_ref[...].astype(BF16)
        dqs, dks, dvs = [], [], []
        for h in range(MEM_HEADS):
            qh, kh, vh, dh = _head(qq, h), _head(kk, h), _head(vv, h), _head(dout, h)
            p, _ps = _softmax(qh, kh, None, None, None)
            dp = lax.dot_general(dh, vh, (((1,), (1,)), ((), ())), preferred_element_type=F32)
            dsum = jnp.sum(p * dp, axis=-1, keepdims=True)
            ds = (p * (dp - dsum)).astype(BF16)
            dqs.append(jnp.dot(ds, kh, preferred_element_type=F32) * SCALE)
            dks.append(lax.dot_general(qh, ds, tn_dims, preferred_element_type=F32) * SCALE)
            dvs.append(lax.dot_general(dh, p.astype(BF16), tn_dims, preferred_element_type=F32))
        dq_ref[...] = jnp.concatenate(dqs, axis=1)
        dk = jnp.concatenate(dks, axis=0).T
        dv = jnp.concatenate(dvs, axis=0).T

        @pl.when(i == 0)
        def _():
            dk_ref[...] = dk
            dv_ref[...] = dv

        @pl.when(i > 0)
        def _():
            dk_ref[...] += dk
            dv_ref[...] += dv

    acc = pl.BlockSpec((M, MEM_WIDTH), lambda i: (0, 0))
    return pl.pallas_call(
        body, name=name, grid=(T // tq,), in_specs=[q_spec, _cols(tq, MEM_WIDTH, last), k_spec, v_spec],
        out_specs=(_cols(tq, MEM_WIDTH, dq_width // MEM_WIDTH - 1), acc, acc),
        out_shape=(jax.ShapeDtypeStruct((T, dq_width), F32), jax.ShapeDtypeStruct((M, MEM_WIDTH), F32),
                   jax.ShapeDtypeStruct((M, MEM_WIDTH), F32)),
        compiler_params=_params("arbitrary"),
    )(q, dcat, mkn, mkv)


def _loss(y, target, *, name):
    T, D = y.shape
    tr = _row_tile(T, D)

    def body(y_ref, t_ref, l_ref, dy_ref, dyb_ref):
        i = pl.program_id(0)
        err = y_ref[...] - t_ref[...]
        dy = err / float(D)
        dy_ref[...] = dy
        dyb_ref[...] = dy.astype(BF16)
        part = jnp.full((8, 128), 0.5 * jnp.sum(jnp.mean(err * err, axis=-1)), F32)

        @pl.when(i == 0)
        def _():
            l_ref[...] = part

        @pl.when(i > 0)
        def _():
            l_ref[...] += part

    row = pl.BlockSpec((tr, D), lambda i: (i, 0))
    return pl.pallas_call(
        body, name=name, grid=(T // tr,), in_specs=[row, row],
        out_specs=(pl.BlockSpec((8, 128), lambda i: (0, 0)), row, row),
        out_shape=(jax.ShapeDtypeStruct((8, 128), F32), jax.ShapeDtypeStruct((T, D), F32), jax.ShapeDtypeStruct((T, D), BF16)),
        compiler_params=_params("arbitrary"),
    )(y, target)


def _position():
    return lax.axis_index("x"), lax.axis_index("y"), lax.axis_index("c")


def _all_gather(arrays, *, name):
    n = len(arrays)

    def body(*refs):
        srcs, outs = refs[:n], refs[n:2 * n]
        token, send_sems, recv_sems, local_sems = refs[2 * n:]
        token[...] = jnp.zeros_like(token)
        x, y, c = _position()
        me, sibling = (x, y, c), (x, y, 1 - c)
        chips = [(1 - x, y), (x, 1 - y), (1 - x, 1 - y)]

        def slot(a, px, py, pc):
            return outs[a].at[4 * px + 2 * py + pc]

        def copy(a, k, block, to, src=None):
            return pltpu.make_async_remote_copy(
                src_ref=slot(a, *block) if src is None else src, dst_ref=slot(a, *block),
                send_sem=send_sems.at[a, k], recv_sem=recv_sems.at[a, k], device_id=to, device_id_type=MESH)

        mine = [pltpu.make_async_copy(srcs[a], slot(a, *me), local_sems.at[a]) for a in range(n)]
        for cp in mine:
            cp.start()
        first, passed = [], []
        for a in range(n):
            first.append(copy(a, 0, me, sibling, src=srcs[a]))
            first += [copy(a, 1 + j, me, (*chip, c), src=srcs[a]) for j, chip in enumerate(chips)]
        for cp in first:
            cp.start()
        for a in range(n):
            for j, chip in enumerate(chips):
                copy(a, 1 + j, (*chip, c), me).wait_recv()
                fwd = copy(a, 4 + j, (*chip, c), sibling)
                fwd.start()
                passed.append(fwd)
        for a in range(n):
            copy(a, 0, sibling, me).wait_recv()
            for j, chip in enumerate(chips):
                copy(a, 4 + j, (*chip, 1 - c), me).wait_recv()
        for cp in first + passed:
            cp.wait_send()
        for cp in mine:
            cp.wait()

    return pl.pallas_call(
        body, name=name, in_specs=[ANY] * n, out_specs=[ANY] * n + [pl.BlockSpec(memory_space=pltpu.VMEM)],
        out_shape=[jax.ShapeDtypeStruct((N_DEV,) + a.shape, a.dtype) for a in arrays] + [jax.ShapeDtypeStruct((8, 128), F32)],
        scratch_shapes=[pltpu.SemaphoreType.DMA((n, 7)), pltpu.SemaphoreType.DMA((n, 7)), pltpu.SemaphoreType.DMA((n,))],
    )(*arrays)


def _sibling_exchange(by_core, whole, *, name):
    n1, n = len(by_core), len(by_core) + len(whole)

    def body(*refs):
        srcs, outs = refs[:n], refs[n:2 * n]
        send_sems, recv_sems = refs[2 * n:]
        x, y, c = _position()
        copies = [
            pltpu.make_async_remote_copy(src_ref=srcs[a].at[:, 1 - c] if a < n1 else srcs[a], dst_ref=outs[a],
                                         send_sem=send_sems.at[a], recv_sem=recv_sems.at[a], device_id=(x, y, 1 - c),
                                         device_id_type=MESH)
            for a in range(n)]
        for cp in copies:
            cp.start()
        for cp in copies:
            cp.wait()

    out_shape = [jax.ShapeDtypeStruct(a.shape[:1] + a.shape[2:], a.dtype) for a in by_core]
    out_shape += [jax.ShapeDtypeStruct(a.shape, a.dtype) for a in whole]
    outs = pl.pallas_call(
        body, name=name, in_specs=[ANY] * n, out_specs=[ANY] * n, out_shape=out_shape,
        scratch_shapes=[pltpu.SemaphoreType.DMA((n,)), pltpu.SemaphoreType.DMA((n,))],
    )(*by_core, *whole)
    return outs[:n1], outs[n1:]


HBM = pl.BlockSpec(memory_space=pltpu.HBM)
SEM = pl.BlockSpec(memory_space=pltpu.SEMAPHORE)
DATAFLOW = pltpu.SideEffectType.DATAFLOW_SIDE_EFFECTING


def _device(flat):
    return flat // 4, (flat // 2) % 2, flat % 2


def _gather_copies(srcs, lands, send_sems, recv_sems, incoming):
    x, y, c = _position()
    me = 4 * x + 2 * y + c
    pairs = []
    for a in range(len(srcs)):
        for d in range(1, N_DEV):
            to, frm = (me + d) % N_DEV, (me + N_DEV - d) % N_DEV
            k = a * (N_DEV - 1) + d - 1
            sems = dict(send_sem=send_sems.at[k], recv_sem=recv_sems.at[k], device_id_type=MESH)
            out = pltpu.make_async_remote_copy(src_ref=srcs[a], dst_ref=lands[a].at[me], device_id=_device(to), **sems)
            inc = pltpu.make_async_remote_copy(src_ref=srcs[a], dst_ref=lands[a].at[frm], device_id=_device(frm),
                                               **sems) if incoming else None
            pairs.append((out, inc))
    return pairs


def _chip_copies(srcs, lands, send_sems, recv_sems, incoming, n_whole=0):
    x, y, c = _position()
    my_chip = 2 * x + y
    pairs = []
    for a in range(len(srcs)):
        for k, (px, py) in enumerate([(1 - x, y), (x, 1 - y), (1 - x, 1 - y)]):
            sem = a * (N_CHIP - 1) + k
            sems = dict(send_sem=send_sems.at[sem], recv_sem=recv_sems.at[sem], device_id=(px, py, c), device_id_type=MESH)
            src = srcs[a] if a >= len(srcs) - n_whole else srcs[a].at[2 * px + py]
            out = pltpu.make_async_remote_copy(src_ref=src, dst_ref=lands[a].at[my_chip], **sems)
            inc = pltpu.make_async_remote_copy(src_ref=src, dst_ref=lands[a].at[2 * px + py], **sems) if incoming else None
            pairs.append((out, inc))
    return pairs


def _push_start(copies, fan, srcs, lands, *, name):
    n = len(srcs)

    def body(*refs):
        src_refs, land_refs = refs[:n], refs[n:2 * n]
        send_sems, recv_sems = refs[2 * n], refs[2 * n + 1]
        token = refs[-1]
        for out, _ in copies(src_refs, land_refs, send_sems, recv_sems, False):
            out.start()
        token[...] = jnp.zeros_like(token)

    outs = pl.pallas_call(
        body, name=name,
        out_shape=(pltpu.SemaphoreType.DMA((n * fan,)), pltpu.SemaphoreType.DMA((n * fan,)),
                   *(pltpu.HBM(a.shape, a.dtype) for a in srcs), *(pltpu.HBM(a.shape, a.dtype) for a in lands),
                   jax.ShapeDtypeStruct((8, 128), F32)),
        in_specs=[HBM] * (2 * n), out_specs=(SEM, SEM, *([HBM] * (2 * n)), pl.BlockSpec(memory_space=pltpu.VMEM)),
        input_output_aliases={i: 2 + i for i in range(2 * n)},
        compiler_params=pltpu.CompilerParams(has_side_effects=DATAFLOW),
    )(*(pltpu.with_memory_space_constraint(a, pltpu.HBM) for a in (*srcs, *lands)))
    return outs[0], outs[1], list(outs[2:2 + n]), list(outs[2 + n:2 + 2 * n]), outs[-1]


def _push_wait(copies, send_sems, recv_sems, srcs, lands, after, *, name):
    n = len(srcs)

    def body(*refs):
        src_refs, land_refs = refs[:n], refs[n:2 * n]
        for out, inc in copies(src_refs, land_refs, refs[2 * n], refs[2 * n + 1], True):
            out.wait_send()
            inc.wait_recv()
        refs[-1][...] = jnp.zeros_like(refs[-1])

    outs = pl.pallas_call(
        body, name=name,
        out_shape=(*(pltpu.HBM(a.shape, a.dtype) for a in (*srcs, *lands)), jax.ShapeDtypeStruct((8, 128), F32)),
        in_specs=[HBM] * (2 * n) + [SEM, SEM, ANY], out_specs=(*([HBM] * (2 * n)), pl.BlockSpec(memory_space=pltpu.VMEM)),
        input_output_aliases={i: i for i in range(2 * n)},
        compiler_params=pltpu.CompilerParams(has_side_effects=DATAFLOW),
    )(*srcs, *lands, send_sems, recv_sems, after)
    return list(outs[n:2 * n]), outs[-1]


def _with_own_slot(block, index, slots):
    buf = lax.empty((slots,) + block.shape, block.dtype)
    return lax.dynamic_update_slice(buf, block[None], (index,) + (0,) * block.ndim)


def _view2d(shape):
    return math.prod(shape[:-1]), shape[-1]


def _pair_sum(mine, other, core, *, name, out_dtype):
    by_core = mine.ndim == 4
    n, w = other.shape[-2:]
    tr = _row_tile(n, w * 2)
    lead = other.shape[0] if by_core else 1

    def body(core_ref, a_ref, b_ref, o_ref):
        o_ref[...] = (a_ref[...].astype(F32) + b_ref[...].astype(F32)).astype(o_ref.dtype)

    if by_core:
        a_spec = pl.BlockSpec((None, None, tr, w), lambda j, i, core_ref: (j, core_ref[0], i, 0))
        o_spec = pl.BlockSpec((None, tr, w), lambda j, i, core_ref: (j, i, 0))
    else:
        a_spec = o_spec = pl.BlockSpec((tr, w), lambda j, i, core_ref: (i, 0))
    grid_spec = pltpu.PrefetchScalarGridSpec(num_scalar_prefetch=1, grid=(lead, n // tr), in_specs=[a_spec, o_spec],
                                             out_specs=o_spec)
    return pl.pallas_call(body, name=name, grid_spec=grid_spec, out_shape=jax.ShapeDtypeStruct(other.shape, out_dtype),
                          compiler_params=_params("parallel", "parallel"))(core.reshape(1), mine, other)


def _adamw(parts, w, m, v, *, name):
    layers = len(parts)
    n_parts, R, W = parts[0].shape
    tr = _row_tile(R, W * 2)
    per_layer = R // tr

    def update(p_ref, w_ref, m_ref, v_ref, g_out, d_out, m_out, v_out):
        g = p_ref[0].astype(F32)
        for j in range(1, n_parts):
            g = g + p_ref[j].astype(F32)
        m_new = ADAM_B1 * m_ref[...] + (1.0 - ADAM_B1) * g
        v_new = ADAM_B2 * v_ref[...] + (1.0 - ADAM_B2) * (g * g)
        m_hat = m_new / (1.0 - ADAM_B1 ** ADAM_STEP)
        v_hat = v_new / (1.0 - ADAM_B2 ** ADAM_STEP)
        g_out[...] = g
        d_out[...] = -ADAM_LR * (m_hat / (jnp.sqrt(v_hat) + ADAM_EPS) + ADAM_WD * w_ref[...])
        m_out[...] = m_new
        v_out[...] = v_new

    def body(*refs):
        for k in range(layers):
            pl.when(pl.program_id(0) == k)(lambda k=k: update(refs[k], *refs[layers:]))

    def parts_spec(k):
        return pl.BlockSpec((n_parts, tr, W), lambda l, i: (0, jnp.where(l == k, i, 0), 0))

    row = pl.BlockSpec((tr, W), lambda l, i: (l * per_layer + i, 0))
    out = jax.ShapeDtypeStruct((layers * R, W), F32)
    return pl.pallas_call(
        body, name=name, grid=(layers, per_layer), in_specs=[parts_spec(k) for k in range(layers)] + [row, row, row],
        out_specs=(row, row, row, row), out_shape=(out, out, out, out), compiler_params=_params("arbitrary", "arbitrary"),
    )(*parts, w, m, v)


SMALL_ROWS = 608


def _pack_small(p):
    flat = jnp.concatenate([p[n].reshape(-1).astype(F32) for n in SMALL_NAMES])
    return jnp.pad(flat, (0, SMALL_ROWS * PACK_W - flat.shape[0])).reshape(SMALL_ROWS, PACK_W)


def _unpack_small(buf, like):
    out, at = {}, 0
    flat = buf.reshape(-1)
    for n in SMALL_NAMES:
        size = math.prod(like[n].shape)
        out[n] = flat[at:at + size].reshape(like[n].shape)
        at += size
    return out


def _block_diag(pw):
    out = jnp.zeros((MAIN_WIDTH, MAIN_WIDTH), pw.dtype)
    for g in range(POOL_GROUPS):
        out = lax.dynamic_update_slice(out, pw[g], (g * POOL_GROUP_DIM, g * POOL_GROUP_DIM))
    return out


def _diag_blocks(m):
    return jnp.stack([m[g * POOL_GROUP_DIM:(g + 1) * POOL_GROUP_DIM, g * POOL_GROUP_DIM:(g + 1) * POOL_GROUP_DIM]
                      for g in range(POOL_GROUPS)])


def _train_pass(x, mem, target, p, w_kv, fetch, reduce_layer, reduce_wait):
    T = x.shape[0]
    mem_cols = MAIN_WIDTH // MEM_WIDTH
    k_gain = _head_gain(p["k_norm"], SWA_KV_HEADS)
    saved = []
    h = x
    kn = kv = h_kv = hn_kv = None
    for l in range(DEPTH):
        s = {}
        wl, token = fetch(l, h)
        s["w"] = wl
        if l == N_A:
            h_kv = h
            hn_kv, kv = _norm_mm(h, p["kv_norm"], w_kv, b_kind="rows", name="kv_proj")
            kn = _seg_rms_fwd(kv, k_gain, width=KV_HALF, col=0, name="k_norm_fwd")
        s["h"] = h
        s["xn1"], proj = _norm_mm(h, p["norm_mix"][l] + token, wl["w_in"], b_kind="rows", name="in_proj")
        s["proj"] = proj
        s["memn"] = _rms_fwd(mem, p["mem_norm"][l], name="mem_norm_fwd")
        s["mkv"] = _mm(s["memn"], wl["w_mem_kv"], b_kind="rows", name="mem_kv_proj")
        s["mk_gain"] = _head_gain(p["mem_k_norm"][l], MEM_HEADS)
        s["mkn"] = _seg_rms_fwd(s["mkv"], s["mk_gain"], width=MEM_WIDTH, col=0, name="mem_k_norm_fwd")
        if l < N_A:
            s["q_gain"] = _head_gain(p["mem_q_norm"][l], MEM_HEADS)
            s["qn"] = _seg_rms_fwd(proj, s["q_gain"], width=MEM_WIDTH, col=mem_cols, name="mem_q_norm_fwd")
            s["q_col"] = 0
        else:
            j = l - N_A
            s["q_gain"] = jnp.concatenate([_head_gain(p["q_norm"][j], SWA_Q_HEADS), _head_gain(p["mem_q_norm"][l], MEM_HEADS)],
                                          axis=1)
            s["qn"] = _seg_rms_fwd(proj, s["q_gain"], width=D_MODEL, col=0, name="q_norm_fwd")
            s["q_col"] = mem_cols
        cat = _mem_attn_fwd(s["qn"], s["q_col"], s["mkn"], s["mkv"], name="mem_attn_fwd")
        if l < N_A:
            s["mix"] = _block_diag(p["pool_w"][l])
            s["scale"] = p["pool_scale"][l].reshape(1, MAIN_WIDTH)
            s["cat"] = _pool_mix_fwd(proj, s["mix"], s["scale"], cat, name="pool_fwd")
        else:
            s["cat"] = _swa_attn_fwd(s["qn"], kn, kv, p["sinks"][l - N_A], cat, name="swa_fwd")
        s["h1"] = _mm(s["cat"], wl["w_out"], b_kind="rows", res=h, name="out_proj")
        s["xn2"], s["a"] = _norm_mm(s["h1"], p["norm_mlp"][l], wl["w_up"], b_kind="layers", relu2=True, name="mlp_up")
        h = _mm(s["a"], wl["w_down"], b_kind="rows", res=s["h1"], name="mlp_down")
        saved.append(s)

    loss, dh, dh_b = _loss(h, target, name="loss_head")

    g = {n: [None] * DEPTH for n in ("norm_mix", "mem_norm", "mem_q_norm", "mem_k_norm", "norm_mlp")}
    g.update({n: [None] * N_A for n in ("pool_w", "pool_scale", "q_norm", "sinks")})
    g_kv = None
    token = None
    dks, dvs = [], []
    for l in reversed(range(DEPTH)):
        s = saved[l]
        wl = s["w"]
        gb = {}

        def dw(a, dy, n):
            return _mm(a, dy, ta=True, out_kind="layers" if n == "w_up" else "rows", out_buf=lax.empty(wl[n].shape, BF16),
                       name=n + "_grad")

        norm_mlp_gain = p["norm_mlp"][l] if token is None else p["norm_mlp"][l] + token
        gb["w_down"] = dw(s["a"], dh_b, "w_down")
        du = _mm(dh_b, wl["w_down"], tb=True, b_kind="rows", mul2=s["a"], out_dtype=BF16, name="mlp_down_dx")
        gb["w_up"] = dw(s["xn2"], du, "w_up")
        early = reduce_layer(l, gb, early=True)
        if early is not None:
            norm_mlp_gain = norm_mlp_gain + early
        dh1, dh1_b, g["norm_mlp"][l] = _mm_rms_bwd(du, wl["w_up"], s["h1"], norm_mlp_gain, dh, b_kind="layers", also_bf16=True,
                                                   name="mlp_up_dx")
        gb["w_out"] = dw(s["cat"], dh1_b, "w_out")
        dcat = _mm(dh1_b, wl["w_out"], tb=True, b_kind="rows", name="out_proj_dx")
        if l < N_A:
            dq, dmk, dmv = _mem_attn_bwd(s["qn"], s["q_col"], s["mkn"], s["mkv"], dcat, dq_width=MEM_WIDTH, name="mem_attn_bwd")
            dproj, dmix, dscale = _pool_mix_bwd(s["proj"], s["mix"], s["scale"], dcat, name="pool_bwd")
            g["pool_w"][l] = _diag_blocks(dmix)
            g["pool_scale"][l] = dscale.reshape(MAIN_WIDTH)
            dproj, dgain = _seg_rms_bwd(s["proj"], s["q_gain"], [dq], width=MEM_WIDTH, col=mem_cols, out_buf=dproj,
                                        out_col=mem_cols, name="mem_q_norm_bwd")
            g["mem_q_norm"][l] = _fold_heads(dgain, MEM_HEADS)
        else:
            j = l - N_A
            dqn, dmk, dmv = _mem_attn_bwd(s["qn"], s["q_col"], s["mkn"], s["mkv"], dcat, dq_width=D_MODEL, name="mem_attn_bwd")
            dqn, dk_l, dv_l, dsinks = _swa_attn_bwd(s["qn"], kn, kv, p["sinks"][j], dcat, dqn, name="swa_bwd")
            dks.append(dk_l)
            dvs.append(dv_l)
            g["sinks"][j] = dsinks[0, :SWA_Q_HEADS]
            dproj, dgain = _seg_rms_bwd(s["proj"], s["q_gain"], [dqn], width=D_MODEL, col=0, name="q_norm_bwd")
            g["q_norm"][j] = _fold_heads(dgain[:, :MAIN_WIDTH], SWA_Q_HEADS)
            g["mem_q_norm"][l] = _fold_heads(dgain[:, MAIN_WIDTH:], MEM_HEADS)
        dmk_raw, dgain = _seg_rms_bwd(s["mkv"], s["mk_gain"], [dmk], width=MEM_WIDTH, col=0, name="mem_k_norm_bwd")
        g["mem_k_norm"][l] = _fold_heads(dgain, MEM_HEADS)
        dmkv = jnp.concatenate([dmk_raw, dmv.astype(BF16)], axis=1)
        gb["w_mem_kv"] = dw(s["memn"], dmkv, "w_mem_kv")
        dmemn = _mm(dmkv, wl["w_mem_kv"], tb=True, b_kind="rows", name="mem_kv_proj_dx")
        g["mem_norm"][l] = _rms_bwd(mem, p["mem_norm"][l], [dmemn], want_dx=False, name="mem_norm_bwd")
        gb["w_in"] = dw(s["xn1"], dproj, "w_in")
        if l in (0, N_A):
            dh, g["norm_mix"][l] = _mm_rms_bwd(dproj, wl["w_in"], s["h"], p["norm_mix"][l], dh1, b_kind="rows", also_bf16=False,
                                               name="in_proj_dx")
        else:
            dh, dh_b, g["norm_mix"][l] = _mm_rms_bwd(dproj, wl["w_in"], s["h"], p["norm_mix"][l], dh1, b_kind="rows",
                                                     also_bf16=True, name="in_proj_dx")
        if l == N_A:
            dkv, dgain = _seg_rms_bwd(kv, k_gain, dks, width=KV_HALF, col=0, out_buf=lax.empty((T, 2 * KV_HALF), BF16),
                                      name="k_norm_bwd")
            g["k_norm"] = _fold_heads(dgain, SWA_KV_HEADS)
            dkv = _sum_into(dvs[0], dvs[1], dkv, 1, name="dv_sum")
            g_kv = _mm(hn_kv, dkv, ta=True, out_kind="rows", out_buf=lax.empty(w_kv.shape, BF16), name="w_kv_grad")
            dh, dh_b, g["kv_norm"] = _mm_rms_bwd(dkv, w_kv, h_kv, p["kv_norm"], dh, b_kind="rows", also_bf16=True,
                                                 name="kv_proj_dx")
        if l + 1 < DEPTH:
            reduce_wait(l + 1, dh)
        token = reduce_layer(l, gb)
    grads = {n: (jnp.stack(v) if isinstance(v, list) else v) for n, v in g.items()}
    return loss, dh, grads, g_kv


def kernel(x, mem, norm_mix, w_in, pool_w, pool_scale, kv_norm, w_kv, k_norm, q_norm, sinks, mem_norm, w_mem_kv, mem_q_norm, mem_k_norm, w_out, norm_mlp, w_up, w_down, loss_target, m_norm_mix, m_w_in, m_pool_w, m_pool_scale, m_kv_norm, m_w_kv, m_k_norm, m_q_norm, m_sinks, m_mem_norm, m_w_mem_kv, m_mem_q_norm, m_mem_k_norm, m_w_out, m_norm_mlp, m_w_up, m_w_down, v_norm_mix, v_w_in, v_pool_w, v_pool_scale, v_kv_norm, v_w_kv, v_k_norm, v_q_norm, v_sinks, v_mem_norm, v_w_mem_kv, v_mem_q_norm, v_mem_k_norm, v_w_out, v_norm_mlp, v_w_up, v_w_down):
    weights = dict(norm_mix=norm_mix, w_in=w_in, pool_w=pool_w, pool_scale=pool_scale, kv_norm=kv_norm, w_kv=w_kv,
                   k_norm=k_norm, q_norm=q_norm, sinks=sinks, mem_norm=mem_norm, w_mem_kv=w_mem_kv,
                   mem_q_norm=mem_q_norm, mem_k_norm=mem_k_norm, w_out=w_out, norm_mlp=norm_mlp, w_up=w_up, w_down=w_down)
    mom1 = dict(norm_mix=m_norm_mix, w_in=m_w_in, pool_w=m_pool_w, pool_scale=m_pool_scale, kv_norm=m_kv_norm, w_kv=m_w_kv,
                k_norm=m_k_norm, q_norm=m_q_norm, sinks=m_sinks, mem_norm=m_mem_norm, w_mem_kv=m_w_mem_kv,
                mem_q_norm=m_mem_q_norm, mem_k_norm=m_mem_k_norm, w_out=m_w_out, norm_mlp=m_norm_mlp, w_up=m_w_up,
                w_down=m_w_down)
    mom2 = dict(norm_mix=v_norm_mix, w_in=v_w_in, pool_w=v_pool_w, pool_scale=v_pool_scale, kv_norm=v_kv_norm, w_kv=v_w_kv,
                k_norm=v_k_norm, q_norm=v_q_norm, sinks=v_sinks, mem_norm=v_mem_norm, w_mem_kv=v_w_mem_kv,
                mem_q_norm=v_mem_q_norm, mem_k_norm=v_mem_k_norm, w_out=v_w_out, norm_mlp=v_norm_mlp, w_up=v_w_up,
                w_down=v_w_down)
    names = list(weights)
    x_pos, y_pos, core = (lax.axis_index(n).astype(jnp.int32) for n in AXES)
    me, my_chip = 4 * x_pos + 2 * y_pos + core, 2 * x_pos + y_pos
    shard = MAIN_WIDTH // N_DEV

    def layer_shards(l, zero=0.0):
        return [(weights[n][l:l + 1] + zero).astype(BF16) for n in LAYERED]

    def usable(arrays):
        wl = dict(zip(LAYERED, arrays))
        wl["w_up"] = wl["w_up"].transpose(1, 2, 0, 3).reshape(1, D_MODEL, D_FF)
        return wl

    scale_block = jnp.pad(pool_scale, ((0, 8 - N_A), (0, 128 - shard)))
    *first, first_done = _all_gather(layer_shards(0) + [w_kv[None].astype(BF16), scale_block], name="gather_first")
    p = {n: weights[n] for n in SMALL_NAMES}
    p["pool_scale"] = first[-1][:, :N_A, :shard].transpose(1, 0, 2).reshape(N_A, MAIN_WIDTH)
    gathers, reduces, parts = {}, {}, {}

    def fetch(l, after):
        if l == 0:
            got, done = first[:len(LAYERED)], first_done
        else:
            got, done = _push_wait(_gather_copies, *gathers.pop(l), after, name=f"gather_wait_{l}")
        token = 0.0
        for ahead in PREFETCH.get(l, ()):
            srcs = layer_shards(ahead, done[0, 0])
            lands = [_with_own_slot(a, me, N_DEV) for a in srcs]
            *handles, block = _push_start(_gather_copies, N_DEV - 1, srcs, lands, name=f"gather_start_{ahead}")
            gathers[ahead], token = handles, token + block[0, 0]
        return usable(got), token

    def by_core(gb):
        gb = dict(gb)
        if "w_up" in gb:
            gb["w_up"] = gb["w_up"].reshape(D_MODEL, N_DEV, D_FF // N_DEV).transpose(1, 0, 2)
        order = [n for n in LAYERED if n in gb] + [n for n in gb if n not in LAYERED]
        return {n: gb[n].reshape((N_CHIP, 2) + _view2d(gb[n].shape[1:] if n == "w_up" else gb[n].shape[2:])) for n in order}

    def pair_sums(views, sib, tag):
        return [_pair_sum(a, b, core, name=f"chip_sum_{n}_{tag}", out_dtype=BF16) for (n, a), b in zip(views.items(), sib)]

    def chip_sums(gb, tag, whole=()):
        views = by_core(gb)
        sib, sib_whole = _sibling_exchange(list(views.values()), list(whole), name="reduce_sibling_" + tag)
        return pair_sums(views, sib, tag), sib_whole

    def start_chip_exchange(sums, tag, whole=()):
        lands = [_with_own_slot(lax.dynamic_index_in_dim(a, my_chip, 0, keepdims=False), my_chip, N_CHIP) for a in sums]
        lands += [_with_own_slot(a, my_chip, N_CHIP) for a in whole]
        copies = functools.partial(_chip_copies, n_whole=len(whole))
        *handles, block = _push_start(copies, N_CHIP - 1, [*sums, *whole], lands, name="reduce_start_" + tag)
        return (copies, *handles), block[0, 0]

    mlp = ("w_up", "w_down")

    def reduce_layer(l, gb, early=False):
        if early and l > 0:
            return None
        if l == 0 and not early:
            reduces["rest"] = {n: a for n, a in gb.items() if n not in mlp}
            return None
        tag = "0_mlp" if early else str(l)
        sums, _ = chip_sums({n: gb[n] for n in mlp} if early else gb, tag)
        reduces[l], token = start_chip_exchange(sums, tag)
        return token

    def reduce_wait(l, after):
        copies, *handles = reduces.pop(l)
        return _push_wait(copies, *handles, after, name=f"reduce_wait_{l}")[0]

    def layer_wait(l, after):
        parts[l] = reduce_wait(l, after)

    loss, grad_x, grads, g_kv = _train_pass(x[0], mem[0], loss_target[0], p, first[len(LAYERED)], fetch, reduce_layer, layer_wait)

    last = dict(reduces.pop("rest"))
    last["w_kv"] = g_kv
    last["pool_scale"] = grads["pool_scale"].reshape(N_A, N_DEV, shard).transpose(1, 0, 2).astype(BF16)[:, None]
    small = _pack_small(grads)
    sums, (sib_small,) = chip_sums(last, "0", whole=[small])
    chip_small = _pair_sum(small, sib_small, core, name="chip_sum_small", out_dtype=F32)
    reduces["rest"], _ = start_chip_exchange(sums, "0_rest", whole=[chip_small])

    def adamw(n, n_parts):
        res = _adamw(n_parts, *(d[n].reshape(_view2d(d[n].shape)) for d in (weights, mom1, mom2)), name="adamw_" + n)
        return [r.reshape(weights[n].shape) for r in res]

    p_up, p_down = reduce_wait(0, chip_small)
    parts[0] = [None, None, None, p_up, p_down]
    new = {n: adamw(n, [parts[l][LAYERED.index(n)] for l in range(DEPTH)]) for n in mlp}
    p_in, p_mem_kv, p_out, parts_kv, parts_scale, parts_small = reduce_wait("rest", new["w_down"][0])
    parts[0][:3] = [p_in, p_mem_kv, p_out]
    new.update({n: adamw(n, [parts[l][k] for l in range(DEPTH)]) for k, n in enumerate(LAYERED) if n not in mlp})
    new["w_kv"] = adamw("w_kv", [parts_kv])
    new["pool_scale"] = adamw("pool_scale", [parts_scale])
    res = _adamw([parts_small], _pack_small(weights), _pack_small(mom1), _pack_small(mom2), name="adamw_replicated")
    for n, vals in zip(SMALL_NAMES, zip(*(_unpack_small(r, weights).values() for r in res))):
        new[n] = list(vals)
    outs = [new[n][k] for k in range(4) for n in names]
    total = lax.psum(loss[0, 0], AXES)
    return (total, grad_x[None], *outs)
```

```python
import functools
import math

import jax
import jax.numpy as jnp
from jax import lax
from jax.experimental import pallas as pl
from jax.experimental.pallas import tpu as pltpu

F32 = jnp.float32
BF16 = jnp.bfloat16
MESH = pl.DeviceIdType.MESH
AXES = ("x", "y", "c")

D_MODEL = 1024
DEPTH = 4
N_A = 2
HEAD_DIM = 64
MEM_HEADS = 4
MEM_WIDTH = MEM_HEADS * HEAD_DIM
MAIN_WIDTH = D_MODEL - MEM_WIDTH
POOL_GROUPS = 4
POOL_GROUP_DIM = MAIN_WIDTH // POOL_GROUPS
POOL_HALO = 16
SWA_Q_HEADS = MAIN_WIDTH // HEAD_DIM
SWA_KV_HEADS = 4
SWA_GROUP = SWA_Q_HEADS // SWA_KV_HEADS
KV_HALF = SWA_KV_HEADS * HEAD_DIM
BLOCK = 128
D_FF = 4 * D_MODEL
EPS = 1e-6
SCALE = HEAD_DIM ** -0.5
NEG = float(jnp.finfo(jnp.float32).min)
N_DEV = 8
N_CHIP = 4

ADAM_LR = 0.001
ADAM_B1 = 0.9
ADAM_B2 = 0.999
ADAM_EPS = 1e-08
ADAM_WD = 0.01
ADAM_STEP = 10

PACK_W = 512
VMEM_LIMIT = 52 * 1024 * 1024
MM_TILE = 1024
LAYERED = ("w_in", "w_mem_kv", "w_out", "w_up", "w_down")
SMALL_NAMES = ("norm_mix", "pool_w", "kv_norm", "k_norm", "q_norm", "sinks", "mem_norm", "mem_q_norm", "mem_k_norm",
               "norm_mlp")


ANY = pl.BlockSpec(memory_space=pl.ANY)


def _params(*sem):
    return pltpu.CompilerParams(dimension_semantics=sem, vmem_limit_bytes=VMEM_LIMIT)


def _mm(a, b, *, name, ta=False, tb=False, b_kind=None, layer=0, res=None, relu2=False, mul2=None, out_dtype=F32,
        out_kind=None, out_buf=None):
    if ta:
        K, M = a.shape
    else:
        M, K = a.shape
    if b_kind is None:
        rows_b, cols_b = b.shape
    elif b_kind == "rows":
        rows_b, cols_b = b.shape[0] * b.shape[2], b.shape[3]
    else:
        rows_b, cols_b = b.shape[1:]
    N, K2 = (rows_b, cols_b) if tb else (cols_b, rows_b)
    assert K == K2, (a.shape, b.shape)
    tm = min(M, MM_TILE if K <= MM_TILE else MM_TILE // 2)
    tn = min(N, MM_TILE)
    assert M % tm == 0 and N % tn == 0
    row_tile, col_tile = (tn, K) if tb else (K, tn)
    a_spec = pl.BlockSpec((K, tm), lambda j, i: (0, i)) if ta else pl.BlockSpec((tm, K), lambda j, i: (i, 0))

    def rc(j):
        return (j, 0) if tb else (0, j)

    if b_kind is None:
        b_spec = pl.BlockSpec((row_tile, col_tile), lambda j, i: rc(j))
    elif b_kind == "rows":
        per = row_tile // b.shape[2]
        b_spec = pl.BlockSpec((per, None, b.shape[2], col_tile), lambda j, i: (rc(j)[0], layer, 0, rc(j)[1]))
    else:
        b_spec = pl.BlockSpec((None, row_tile, col_tile), lambda j, i: (layer, *rc(j)))
    o_spec = pl.BlockSpec((tm, tn), lambda j, i: (i, j))
    dn = (((0 if ta else 1,), (1 if tb else 0,)), ((), ()))
    extra = [e for e in (res, mul2) if e is not None]
    n_in = 2 + len(extra) + (1 if out_buf is not None else 0)

    def body(*refs):
        a_ref, b_ref = refs[0], refs[1]
        extra_refs = refs[2:2 + len(extra)]
        out = refs[n_in]
        bv = b_ref[...].astype(BF16).reshape(row_tile, col_tile)
        v = lax.dot_general(a_ref[...].astype(BF16), bv, dn, preferred_element_type=F32)
        if res is not None:
            v = extra_refs[0][...] + v
        elif mul2 is not None:
            v = v * (2.0 * jnp.sqrt(extra_refs[0][...].astype(F32)))
        if relu2:
            r = jnp.maximum(v, 0.0)
            v = r * r
        out[...] = v.astype(out.dtype).reshape(out.shape)

    in_specs = [a_spec, b_spec] + [o_spec] * len(extra)
    operands = [a, b, *extra]
    aliases = {}
    if out_kind is None:
        out_shape = jax.ShapeDtypeStruct((M, N), BF16 if relu2 else out_dtype)
        out_specs = o_spec
    else:
        if out_kind == "rows":
            s = out_buf.shape[2]
            out_specs = pl.BlockSpec((tm // s, None, s, tn), lambda j, i: (i, layer, 0, j))
        else:
            out_specs = pl.BlockSpec((None, tm, tn), lambda j, i: (layer, i, j))
        out_shape = jax.ShapeDtypeStruct(out_buf.shape, out_buf.dtype)
        in_specs.append(ANY)
        operands.append(out_buf)
        aliases = {len(operands) - 1: 0}
    return pl.pallas_call(
        body, name=name, grid=(N // tn, M // tm), in_specs=in_specs, out_specs=out_specs, out_shape=out_shape,
        input_output_aliases=aliases, compiler_params=_params("parallel", "parallel"),
    )(*operands)


def _weight_block(b, b_kind, transposed, tn):
    if b_kind == "rows":
        s = b.shape[2]
        rows, cols = b.shape[0] * s, b.shape[3]
        if transposed:
            return (lambda at: pl.BlockSpec((b.shape[0], None, s, cols), lambda *g: (0, 0, 0, 0))), rows, cols
        return (lambda at: pl.BlockSpec((b.shape[0], None, s, tn), lambda *g: (0, 0, 0, at(*g)))), rows, cols
    rows, cols = b.shape[1:]
    if transposed:
        return (lambda at: pl.BlockSpec((None, rows, cols), lambda *g: (0, 0, 0))), rows, cols
    return (lambda at: pl.BlockSpec((None, rows, tn), lambda *g: (0, 0, at(*g)))), rows, cols


def _norm_mm(x, gain, b, *, b_kind, name, relu2=False):
    M, K = x.shape
    tm = min(M, MM_TILE)
    spec_of, rows, N = _weight_block(b, b_kind, False, min(MM_TILE, b.shape[-1]))
    tn = min(N, MM_TILE)
    assert rows == K and M % tm == 0 and N % tn == 0

    def body(x_ref, g_ref, b_ref, xn_ref, o_ref):
        @pl.when(pl.program_id(1) == 0)
        def _():
            xv = x_ref[...]
            r = lax.rsqrt(jnp.mean(xv * xv, axis=-1, keepdims=True) + EPS)
            xn_ref[...] = ((xv * r) * g_ref[...]).astype(xn_ref.dtype)

        v = jnp.dot(xn_ref[...], b_ref[...].astype(BF16).reshape(K, tn), preferred_element_type=F32)
        if relu2:
            r2 = jnp.maximum(v, 0.0)
            v = r2 * r2
        o_ref[...] = v.astype(o_ref.dtype)

    rows_spec = pl.BlockSpec((tm, K), lambda i, j: (i, 0))
    return pl.pallas_call(
        body, name=name, grid=(M // tm, N // tn),
        in_specs=[rows_spec, pl.BlockSpec((1, K), lambda i, j: (0, 0)), spec_of(lambda i, j: j)],
        out_specs=(rows_spec, pl.BlockSpec((tm, tn), lambda i, j: (i, j))),
        out_shape=(jax.ShapeDtypeStruct((M, K), BF16), jax.ShapeDtypeStruct((M, N), BF16 if relu2 else F32)),
        compiler_params=_params("parallel", "arbitrary"),
    )(x, gain.reshape(1, K), b)


def _mm_rms_bwd(a, b, x, gain, res, *, b_kind, name, also_bf16):
    M, K = a.shape
    spec_of, N, cols = _weight_block(b, b_kind, True, None)
    assert cols == K and x.shape == (M, N)
    tm = min(M, MM_TILE if K <= MM_TILE else MM_TILE // 2)
    assert M % tm == 0

    def body(a_ref, b_ref, x_ref, g_ref, res_ref, *outs):
        i = pl.program_id(0)
        dy = lax.dot_general(a_ref[...].astype(BF16), b_ref[...].astype(BF16).reshape(N, K), (((1,), (1,)), ((), ())),
                             preferred_element_type=F32)
        xv = x_ref[...]
        r = lax.rsqrt(jnp.mean(xv * xv, axis=-1, keepdims=True) + EPS)
        xh = xv * r
        part = jnp.sum(dy * xh, axis=0, keepdims=True)
        dg_ref = outs[-1]

        @pl.when(i == 0)
        def _():
            dg_ref[...] = part

        @pl.when(i > 0)
        def _():
            dg_ref[...] += part

        gdy = dy * g_ref[...]
        dx = res_ref[...] + r * (gdy - xh * jnp.mean(gdy * xh, axis=-1, keepdims=True))
        outs[0][...] = dx
        if also_bf16:
            outs[1][...] = dx.astype(BF16)

    row = pl.BlockSpec((tm, N), lambda i: (i, 0))
    vec = pl.BlockSpec((1, N), lambda i: (0, 0))
    out_specs = [row] + ([row] if also_bf16 else []) + [vec]
    out_shape = [jax.ShapeDtypeStruct((M, N), F32)] + ([jax.ShapeDtypeStruct((M, N), BF16)] if also_bf16 else [])
    outs = pl.pallas_call(
        body, name=name, grid=(M // tm,),
        in_specs=[pl.BlockSpec((tm, K), lambda i: (i, 0)), spec_of(None), row, vec, row], out_specs=out_specs,
        out_shape=out_shape + [jax.ShapeDtypeStruct((1, N), F32)], compiler_params=_params("arbitrary"),
    )(a, b, x, gain.reshape(1, N), res)
    return (*outs[:-1], outs[-1].reshape(N))


def _row_tile(rows, d):
    t = min(rows, (512 * 1024) // d)
    while rows % t or (t != rows and t % 16):
        t -= 1
    return t


def _rms_fwd(x, g, *, name, out_dtype=BF16):
    R, D = x.shape
    tr = _row_tile(R, D)

    def body(x_ref, g_ref, o_ref):
        xv = x_ref[...].astype(F32)
        r = lax.rsqrt(jnp.mean(xv * xv, axis=-1, keepdims=True) + EPS)
        o_ref[...] = ((xv * r) * g_ref[...]).astype(o_ref.dtype)

    return pl.pallas_call(
        body, name=name, grid=(R // tr,),
        in_specs=[pl.BlockSpec((tr, D), lambda i: (i, 0)), pl.BlockSpec((1, D), lambda i: (0, 0))],
        out_specs=pl.BlockSpec((tr, D), lambda i: (i, 0)), out_shape=jax.ShapeDtypeStruct((R, D), out_dtype),
        compiler_params=_params("parallel"),
    )(x, g.reshape(1, D))


def _rms_bwd(x, g, dys, *, name, res=None, want_dx=True, also_bf16=False):
    R, D = x.shape
    tr = _row_tile(R, D)
    n_dy = len(dys)
    has_res = res is not None

    def body(*refs):
        x_ref, g_ref = refs[0], refs[1]
        dy_refs = refs[2:2 + n_dy]
        res_ref = refs[2 + n_dy] if has_res else None
        outs = refs[2 + n_dy + (1 if has_res else 0):]
        dg_ref = outs[-1]
        i = pl.program_id(0)
        xv = x_ref[...].astype(F32)
        dy = dy_refs[0][...].astype(F32)
        for extra in dy_refs[1:]:
            dy = dy + extra[...].astype(F32)
        r = lax.rsqrt(jnp.mean(xv * xv, axis=-1, keepdims=True) + EPS)
        xh = xv * r
        part = jnp.sum(dy * xh, axis=0, keepdims=True)

        @pl.when(i == 0)
        def _():
            dg_ref[...] = part

        @pl.when(i > 0)
        def _():
            dg_ref[...] += part

        if want_dx:
            gdy = dy * g_ref[...]
            dx = r * (gdy - xh * jnp.mean(gdy * xh, axis=-1, keepdims=True))
            if has_res:
                dx = res_ref[...] + dx
            outs[0][...] = dx
            if also_bf16:
                outs[1][...] = dx.astype(BF16)

    row = pl.BlockSpec((tr, D), lambda i: (i, 0))
    vec = pl.BlockSpec((1, D), lambda i: (0, 0))
    out_shape = [jax.ShapeDtypeStruct((1, D), F32)]
    out_specs = [vec]
    if also_bf16:
        out_shape = [jax.ShapeDtypeStruct((R, D), BF16)] + out_shape
        out_specs = [row] + out_specs
    if want_dx:
        out_shape = [jax.ShapeDtypeStruct((R, D), F32)] + out_shape
        out_specs = [row] + out_specs
    outs = pl.pallas_call(
        body, name=name, grid=(R // tr,),
        in_specs=[row, vec] + [row] * (n_dy + (1 if has_res else 0)), out_specs=out_specs, out_shape=out_shape,
        compiler_params=_params("arbitrary"),
    )(x, g.reshape(1, D), *dys, *([res] if has_res else []))
    return (*outs[:-1], outs[-1].reshape(D)) if want_dx else outs[0].reshape(D)


POOL_TILE = 512
MEM_Q_TILE = 512


def _softmax(q, k, bias, valid, sink):
    s = lax.dot_general(q, k, (((1,), (1,)), ((), ())), preferred_element_type=F32) * SCALE
    if bias is not None:
        s = s - bias
    if valid is not None:
        s = jnp.where(valid, s, NEG)
    m = jnp.max(s, axis=-1, keepdims=True)
    if sink is not None:
        m = jnp.maximum(m, sink)
    e = jnp.exp(s - m)
    z = jnp.sum(e, axis=-1, keepdims=True)
    if sink is None:
        return e * (1.0 / z), None
    es = jnp.exp(sink - m)
    inv = 1.0 / (z + es)
    return e * inv, es * inv


LANES = 128


def _seg_mean(v):
    r = lax.broadcasted_iota(jnp.int32, (LANES, LANES), 0) // HEAD_DIM
    c = lax.broadcasted_iota(jnp.int32, (LANES, LANES), 1) // HEAD_DIM
    seg = jnp.where(r == c, 1.0 / HEAD_DIM, 0.0).astype(BF16)
    hi = v.astype(BF16)
    lo = (v - hi.astype(F32)).astype(BF16)
    parts = []
    for g in range(v.shape[1] // LANES):
        sl = slice(g * LANES, (g + 1) * LANES)
        parts.append(jnp.dot(hi[:, sl], seg, preferred_element_type=F32) + jnp.dot(lo[:, sl], seg, preferred_element_type=F32))
    return parts[0] if len(parts) == 1 else jnp.concatenate(parts, axis=1)


def _cols(rows, width, col):
    return pl.BlockSpec((rows, width), lambda i: (i, col))


def _head_gain(g, heads):
    return jnp.tile(g, heads).reshape(1, heads * HEAD_DIM)


def _fold_heads(dg, heads):
    return dg.reshape(heads, HEAD_DIM).sum(axis=0)


def _seg_rms_fwd(x, gain, *, width, col, name):
    R = x.shape[0]
    tr = _row_tile(R, width)

    def body(x_ref, g_ref, o_ref):
        xv = x_ref[...]
        r = lax.rsqrt(_seg_mean(xv * xv) + EPS)
        o_ref[...] = ((xv * r) * g_ref[...]).astype(o_ref.dtype)

    return pl.pallas_call(
        body, name=name, grid=(R // tr,), in_specs=[_cols(tr, width, col), pl.BlockSpec((1, width), lambda i: (0, 0))],
        out_specs=_cols(tr, width, 0), out_shape=jax.ShapeDtypeStruct((R, width), BF16), compiler_params=_params("parallel"),
    )(x, gain)


def _seg_rms_bwd(x, gain, dys, *, width, col, name, out_buf=None, out_col=0):
    R = x.shape[0]
    tr = _row_tile(R, width)
    n_dy = len(dys)

    def body(*refs):
        x_ref, g_ref = refs[0], refs[1]
        dy_refs = refs[2:2 + n_dy]
        dx_ref, dg_ref = refs[-2], refs[-1]
        i = pl.program_id(0)
        xv = x_ref[...]
        dy = dy_refs[0][...]
        for extra in dy_refs[1:]:
            dy = dy + extra[...]
        r = lax.rsqrt(_seg_mean(xv * xv) + EPS)
        xh = xv * r
        part = jnp.sum(dy * xh, axis=0, keepdims=True)

        @pl.when(i == 0)
        def _():
            dg_ref[...] = part

        @pl.when(i > 0)
        def _():
            dg_ref[...] += part

        gdy = dy * g_ref[...]
        dx_ref[...] = (r * (gdy - xh * _seg_mean(gdy * xh))).astype(dx_ref.dtype)

    vec = pl.BlockSpec((1, width), lambda i: (0, 0))
    in_specs = [_cols(tr, width, col), vec] + [_cols(tr, width, 0)] * n_dy
    operands = [x, gain, *dys]
    aliases = {}
    dx_shape = jax.ShapeDtypeStruct((R, width), BF16)
    if out_buf is not None:
        in_specs.append(ANY)
        operands.append(out_buf)
        aliases = {len(operands) - 1: 0}
        dx_shape = jax.ShapeDtypeStruct(out_buf.shape, out_buf.dtype)
    return pl.pallas_call(
        body, name=name, grid=(R // tr,), in_specs=in_specs, out_specs=(_cols(tr, width, out_col), vec),
        out_shape=(dx_shape, jax.ShapeDtypeStruct((1, width), F32)), input_output_aliases=aliases,
        compiler_params=_params("arbitrary"),
    )(*operands)


def _sum_into(a, b, out_buf, out_col, *, name):
    R, width = a.shape
    tr = _row_tile(R, width)

    def body(a_ref, b_ref, _, o_ref):
        o_ref[...] = (a_ref[...] + b_ref[...]).astype(o_ref.dtype)

    return pl.pallas_call(
        body, name=name, grid=(R // tr,), in_specs=[_cols(tr, width, 0), _cols(tr, width, 0), ANY],
        out_specs=_cols(tr, width, out_col), out_shape=jax.ShapeDtypeStruct(out_buf.shape, out_buf.dtype),
        input_output_aliases={2: 0}, compiler_params=_params("parallel"),
    )(a, b, out_buf)


def _pool_lane_group():
    return lax.broadcasted_iota(jnp.int32, (1, MAIN_WIDTH), 1) // POOL_GROUP_DIM


def _pool_pick(group, per_window):
    s1, s2, s3, s4 = per_window
    return jnp.where(group == 0, s1, jnp.where(group == 1, s2, jnp.where(group == 2, s3, s4)))


def _pool_delta(u_ref, halo_ref, tile):
    group = _pool_lane_group()
    halo = jnp.where(tile == 0, 0.0, halo_ref[...])
    ext = jnp.concatenate([halo, u_ref[...]], axis=0)
    n = ext.shape[0]
    s1 = ext + pltpu.roll(ext, 1, 0)
    s2 = s1 + pltpu.roll(s1, 2, 0)
    s3 = s2 + pltpu.roll(s2, 4, 0)
    s4 = s3 + pltpu.roll(s3, 8, 0)
    ws = _pool_pick(group, (s1, s2, s3, s4))[POOL_HALO:n]
    t = tile * POOL_TILE + lax.broadcasted_iota(jnp.int32, (POOL_TILE, 1), 0)
    cnt = jnp.minimum(t + 1, _pool_pick(group, (2, 4, 8, 16))).astype(F32)
    return ws / cnt - u_ref[...], cnt


def _pool_in_specs():
    per_tile = POOL_TILE // POOL_HALO
    cur = _cols(POOL_TILE, MAIN_WIDTH, 0)
    prev = pl.BlockSpec((POOL_HALO, MAIN_WIDTH), lambda i: (jnp.maximum(i * per_tile - 1, 0), 0))
    mix = pl.BlockSpec((MAIN_WIDTH, MAIN_WIDTH), lambda i: (0, 0))
    vec = pl.BlockSpec((1, MAIN_WIDTH), lambda i: (0, 0))
    return cur, prev, mix, vec


def _pool_mix_fwd(proj, mix, scale, cat, *, name):
    T = proj.shape[0]
    assert T % POOL_TILE == 0
    cur, prev, mix_spec, vec = _pool_in_specs()

    def body(u_ref, halo_ref, mix_ref, sc_ref, _, o_ref):
        d, _cnt = _pool_delta(u_ref, halo_ref, pl.program_id(0))
        mixed = jnp.dot(d.astype(BF16), mix_ref[...].astype(BF16), preferred_element_type=F32)
        o_ref[...] = (mixed * sc_ref[...]).astype(o_ref.dtype)

    return pl.pallas_call(
        body, name=name, grid=(T // POOL_TILE,), in_specs=[cur, prev, mix_spec, vec, ANY], out_specs=cur,
        out_shape=jax.ShapeDtypeStruct(cat.shape, cat.dtype), input_output_aliases={4: 0}, compiler_params=_params("parallel"),
    )(proj, proj, mix, scale, cat)


def _pool_mix_bwd(proj, mix, scale, dcat, *, name):
    T = proj.shape[0]
    nt = T // POOL_TILE
    per_tile = POOL_TILE // POOL_HALO
    cur, prev, mix_spec, vec = _pool_in_specs()
    nxt = pl.BlockSpec((POOL_HALO, MAIN_WIDTH), lambda i: (jnp.minimum((i + 1) * per_tile, nt * per_tile - 1), 0))

    def body(u_ref, halo_ref, mix_ref, sc_ref, do_ref, donext_ref, du_ref, dmix_ref, dsc_ref):
        tile = pl.program_id(0)
        group = _pool_lane_group()
        d, cnt = _pool_delta(u_ref, halo_ref, tile)
        mixb = mix_ref[...].astype(BF16)
        db = d.astype(BF16)
        mixed = jnp.dot(db, mixb, preferred_element_type=F32)
        dout = do_ref[...]
        dsc = jnp.sum(dout * mixed, axis=0, keepdims=True)
        sc = sc_ref[...]
        dmixed = (dout * sc).astype(BF16)
        dmix = lax.dot_general(db, dmixed, (((0,), (0,)), ((), ())), preferred_element_type=F32)

        @pl.when(tile == 0)
        def _():
            dmix_ref[...] = dmix
            dsc_ref[...] = dsc

        @pl.when(tile > 0)
        def _():
            dmix_ref[...] += dmix
            dsc_ref[...] += dsc

        dnext = jnp.where(tile == nt - 1, 0.0, donext_ref[...])
        dmixed_ext = jnp.concatenate([dmixed, (dnext * sc).astype(BF16)], axis=0)
        dd_ext = lax.dot_general(dmixed_ext, mixb, (((1,), (1,)), ((), ())), preferred_element_type=F32)
        window = _pool_pick(group, (2.0, 4.0, 8.0, 16.0))
        cnt_ext = jnp.concatenate([cnt, jnp.broadcast_to(window, (POOL_HALO, MAIN_WIDTH))], axis=0)
        q = dd_ext / cnt_ext
        n = q.shape[0]
        r1 = q + pltpu.roll(q, n - 1, 0)
        r2 = r1 + pltpu.roll(r1, n - 2, 0)
        r3 = r2 + pltpu.roll(r2, n - 4, 0)
        r4 = r3 + pltpu.roll(r3, n - 8, 0)
        back = _pool_pick(group, (r1, r2, r3, r4))
        du_ref[...] = (back[0:POOL_TILE] - dd_ext[0:POOL_TILE]).astype(du_ref.dtype)

    return pl.pallas_call(
        body, name=name, grid=(nt,), in_specs=[cur, prev, mix_spec, vec, cur, nxt], out_specs=(cur, mix_spec, vec),
        out_shape=(jax.ShapeDtypeStruct((T, D_MODEL), BF16), jax.ShapeDtypeStruct((MAIN_WIDTH, MAIN_WIDTH), F32),
                   jax.ShapeDtypeStruct((1, MAIN_WIDTH), F32)),
        compiler_params=_params("arbitrary"),
    )(proj, proj, mix, scale, dcat, dcat)


def _head(a, h):
    return a[:, h * HEAD_DIM:(h + 1) * HEAD_DIM]


def _swa_mask(blk):
    rows = SWA_GROUP * BLOCK
    qi = lax.broadcasted_iota(jnp.int32, (rows, 2 * BLOCK), 0) % BLOCK
    kj = lax.broadcasted_iota(jnp.int32, (rows, 2 * BLOCK), 1)
    dist = qi + BLOCK - kj
    valid = (dist >= 0) & (dist < BLOCK) & ((blk > 0) | (kj >= BLOCK))
    return dist.astype(F32), valid


def _swa_head_terms(sink_ref, kvh, dist):
    grp = lax.broadcasted_iota(jnp.int32, (SWA_GROUP * BLOCK, 1), 0) // BLOCK
    slopes = [2.0 ** (-8.0 * (kvh * SWA_GROUP + g + 1) / SWA_Q_HEADS) for g in range(SWA_GROUP)]
    sinks = [sink_ref[kvh * SWA_GROUP + g] for g in range(SWA_GROUP)]
    slope = jnp.where(grp == 0, slopes[0], jnp.where(grp == 1, slopes[1], slopes[2]))
    sink = jnp.where(grp == 0, sinks[0], jnp.where(grp == 1, sinks[1], sinks[2]))
    return slope * dist, sink


def _stack_heads(a, kvh):
    return jnp.concatenate([_head(a, kvh * SWA_GROUP + g) for g in range(SWA_GROUP)], axis=0)


def _swa_specs(nb):
    def at(n):
        return jnp.minimum(n, nb - 1)

    q = pl.BlockSpec((BLOCK, MAIN_WIDTH), lambda n: (at(n), 0))
    k_prev = pl.BlockSpec((BLOCK, KV_HALF), lambda n: (jnp.maximum(at(n) - 1, 0), 0))
    k_cur = pl.BlockSpec((BLOCK, KV_HALF), lambda n: (at(n), 0))
    v_prev = pl.BlockSpec((BLOCK, KV_HALF), lambda n: (jnp.maximum(at(n) - 1, 0), 1))
    v_cur = pl.BlockSpec((BLOCK, KV_HALF), lambda n: (at(n), 1))
    return q, k_prev, k_cur, v_prev, v_cur


def _swa_attn_fwd(qn, kn, kv, sinks, cat, *, name):
    T = qn.shape[0]
    nb = T // BLOCK
    q_spec, k_prev, k_cur, v_prev, v_cur = _swa_specs(nb)

    def body(sink_ref, q_ref, kp_ref, kc_ref, vp_ref, vc_ref, _, o_ref):
        dist, valid = _swa_mask(pl.program_id(0))
        kk = jnp.concatenate([kp_ref[...], kc_ref[...]], axis=0)
        vv = jnp.concatenate([vp_ref[...], vc_ref[...]], axis=0).astype(BF16)
        q = q_ref[...]
        outs = []
        for kvh in range(SWA_KV_HEADS):
            bias, sink = _swa_head_terms(sink_ref, kvh, dist)
            p, _ps = _softmax(_stack_heads(q, kvh), _head(kk, kvh), bias, valid, sink)
            o = jnp.dot(p.astype(BF16), _head(vv, kvh), preferred_element_type=F32)
            outs += [o[g * BLOCK:(g + 1) * BLOCK] for g in range(SWA_GROUP)]
        o_ref[...] = jnp.concatenate(outs, axis=1).astype(o_ref.dtype)

    return pl.pallas_call(
        body, name=name, grid=(nb,),
        in_specs=[pl.BlockSpec(memory_space=pltpu.SMEM), q_spec, k_prev, k_cur, v_prev, v_cur, ANY], out_specs=q_spec,
        out_shape=jax.ShapeDtypeStruct(cat.shape, cat.dtype), input_output_aliases={6: 0}, compiler_params=_params("parallel"),
    )(sinks, qn, kn, kn, kv, kv, cat)


def _swa_attn_bwd(qn, kn, kv, sinks, dcat, dqn, *, name):
    T = qn.shape[0]
    nb = T // BLOCK
    q_spec, k_prev, k_cur, v_prev, v_cur = _swa_specs(nb)
    late = pl.BlockSpec((BLOCK, KV_HALF), lambda n: (jnp.maximum(n - 1, 0), 0))
    tn_dims = (((0,), (0,)), ((), ()))

    def body(sink_ref, q_ref, do_ref, kp_ref, kc_ref, vp_ref, vc_ref, _, dq_ref, dk_ref, dv_ref, ds_ref, ck, cv):
        blk = pl.program_id(0)

        @pl.when(blk == 0)
        def _():
            ck[...] = jnp.zeros_like(ck)
            cv[...] = jnp.zeros_like(cv)
            ds_ref[...] = jnp.zeros_like(ds_ref)

        @pl.when(blk < nb)
        def _():
            dist, valid = _swa_mask(blk)
            kk = jnp.concatenate([kp_ref[...], kc_ref[...]], axis=0)
            vv = jnp.concatenate([vp_ref[...], vc_ref[...]], axis=0).astype(BF16)
            q = q_ref[...]
            dout = do_ref[...].astype(BF16)
            lane = lax.broadcasted_iota(jnp.int32, (1, LANES), 1)
            dsinks = jnp.zeros((1, LANES), F32)
            dqs, dks, dvs = [], [], []
            for kvh in range(SWA_KV_HEADS):
                bias, sink = _swa_head_terms(sink_ref, kvh, dist)
                qq, kh, vh, dd = _stack_heads(q, kvh), _head(kk, kvh), _head(vv, kvh), _stack_heads(dout, kvh)
                p, ps = _softmax(qq, kh, bias, valid, sink)
                dp = lax.dot_general(dd, vh, (((1,), (1,)), ((), ())), preferred_element_type=F32)
                dsum = jnp.sum(p * dp, axis=-1, keepdims=True)
                ds = (p * (dp - dsum)).astype(BF16)
                dq = jnp.dot(ds, kh, preferred_element_type=F32) * SCALE
                dqs += [dq[g * BLOCK:(g + 1) * BLOCK] for g in range(SWA_GROUP)]
                dks.append(lax.dot_general(qq, ds, tn_dims, preferred_element_type=F32) * SCALE)
                dvs.append(lax.dot_general(dd, p.astype(BF16), tn_dims, preferred_element_type=F32))
                dsink = -(ps * dsum)
                for g in range(SWA_GROUP):
                    dsinks = dsinks + jnp.where(lane == kvh * SWA_GROUP + g, jnp.sum(dsink[g * BLOCK:(g + 1) * BLOCK]), 0.0)
            dq_ref[...] = jnp.concatenate(dqs, axis=1)
            dk = jnp.concatenate(dks, axis=0).T
            dv = jnp.concatenate(dvs, axis=0).T
            dk_ref[...] = ck[...] + dk[0:BLOCK]
            dv_ref[...] = cv[...] + dv[0:BLOCK]
            ck[...] = dk[BLOCK:2 * BLOCK]
            cv[...] = dv[BLOCK:2 * BLOCK]
            ds_ref[...] += dsinks

        @pl.when(blk == nb)
        def _():
            dk_ref[...] = ck[...]
            dv_ref[...] = cv[...]

    return pl.pallas_call(
        body, name=name, grid=(nb + 1,),
        in_specs=[pl.BlockSpec(memory_space=pltpu.SMEM), q_spec, q_spec, k_prev, k_cur, v_prev, v_cur, ANY],
        out_specs=(q_spec, late, late, pl.BlockSpec((1, LANES), lambda n: (0, 0))),
        out_shape=(jax.ShapeDtypeStruct(dqn.shape, dqn.dtype), jax.ShapeDtypeStruct((T, KV_HALF), F32),
                   jax.ShapeDtypeStruct((T, KV_HALF), F32), jax.ShapeDtypeStruct((1, LANES), F32)),
        scratch_shapes=[pltpu.VMEM((BLOCK, KV_HALF), F32), pltpu.VMEM((BLOCK, KV_HALF), F32)],
        input_output_aliases={7: 0}, compiler_params=_params("arbitrary"),
    )(sinks, qn, dcat, kn, kn, kv, kv, dqn)


def _mem_specs(M, tq, q_col):
    q = _cols(tq, MEM_WIDTH, q_col)
    k = pl.BlockSpec((M, MEM_WIDTH), lambda i: (0, 0))
    v = pl.BlockSpec((M, MEM_WIDTH), lambda i: (0, 1))
    return q, k, v


def _mem_attn_fwd(q, q_col, mkn, mkv, *, name):
    T = q.shape[0]
    M = mkn.shape[0]
    tq = min(T, MEM_Q_TILE)
    q_spec, k_spec, v_spec = _mem_specs(M, tq, q_col)

    def body(q_ref, k_ref, v_ref, o_ref):
        qq, kk, vv = q_ref[...], k_ref[...], v_ref[...].astype(BF16)
        outs = []
        for h in range(MEM_HEADS):
            p, _ps = _softmax(_head(qq, h), _head(kk, h), None, None, None)
            outs.append(jnp.dot(p.astype(BF16), _head(vv, h), preferred_element_type=F32))
        o_ref[...] = jnp.concatenate(outs, axis=1).astype(o_ref.dtype)

    return pl.pallas_call(
        body, name=name, grid=(T // tq,), in_specs=[q_spec, k_spec, v_spec], out_specs=_cols(tq, MEM_WIDTH, MAIN_WIDTH // MEM_WIDTH),
        out_shape=jax.ShapeDtypeStruct((T, D_MODEL), BF16), compiler_params=_params("parallel"),
    )(q, mkn, mkv)


def _mem_attn_bwd(q, q_col, mkn, mkv, dcat, *, dq_width, name):
    T = q.shape[0]
    M = mkn.shape[0]
    tq = min(T, MEM_Q_TILE)
    q_spec, k_spec, v_spec = _mem_specs(M, tq, q_col)
    last = MAIN_WIDTH // MEM_WIDTH
    tn_dims = (((0,), (0,)), ((), ()))

    def body(q_ref, do_ref, k_ref, v_ref, dq_ref, dk_ref, dv_ref):
        i = pl.program_id(0)
        qq, kk, vv, dout = q_ref[...], k_ref[...], v_ref[...].astype(BF16), do_ref[...].astype(BF16)
        dqs, dks, dvs = [], [], []
        for h in range(MEM_HEADS):
            qh, kh, vh, dh = _head(qq, h), _head(kk, h), _head(vv, h), _head(dout, h)
            p, _ps = _softmax(qh, kh, None, None, None)
            dp = lax.dot_general(dh, vh, (((1,), (1,)), ((), ())), preferred_element_type=F32)
            dsum = jnp.sum(p * dp, axis=-1, keepdims=True)
            ds = (p * (dp - dsum)).astype(BF16)
            dqs.append(jnp.dot(ds, kh, preferred_element_type=F32) * SCALE)
            dks.append(lax.dot_general(qh, ds, tn_dims, preferred_element_type=F32) * SCALE)
            dvs.append(lax.dot_general(dh, p.astype(BF16), tn_dims, preferred_element_type=F32))
        dq_ref[...] = jnp.concatenate(dqs, axis=1)
        dk = jnp.concatenate(dks, axis=0).T
        dv = jnp.concatenate(dvs, axis=0).T

        @pl.when(i == 0)
        def _():
            dk_ref[...] = dk
            dv_ref[...] = dv

        @pl.when(i > 0)
        def _():
            dk_ref[...] += dk
            dv_ref[...] += dv

    acc = pl.BlockSpec((M, MEM_WIDTH), lambda i: (0, 0))
    return pl.pallas_call(
        body, name=name, grid=(T // tq,), in_specs=[q_spec, _cols(tq, MEM_WIDTH, last), k_spec, v_spec],
        out_specs=(_cols(tq, MEM_WIDTH, dq_width // MEM_WIDTH - 1), acc, acc),
        out_shape=(jax.ShapeDtypeStruct((T, dq_width), F32), jax.ShapeDtypeStruct((M, MEM_WIDTH), F32),
                   jax.ShapeDtypeStruct((M, MEM_WIDTH), F32)),
        compiler_params=_params("arbitrary"),
    )(q, dcat, mkn, mkv)


def _loss(y, target, *, name):
    T, D = y.shape
    tr = _row_tile(T, D)

    def body(y_ref, t_ref, l_ref, dy_ref, dyb_ref):
        i = pl.program_id(0)
        err = y_ref[...] - t_ref[...]
        dy = err / float(D)
        dy_ref[...] = dy
        dyb_ref[...] = dy.astype(BF16)
        part = jnp.full((8, 128), 0.5 * jnp.sum(jnp.mean(err * err, axis=-1)), F32)

        @pl.when(i == 0)
        def _():
            l_ref[...] = part

        @pl.when(i > 0)
        def _():
            l_ref[...] += part

    row = pl.BlockSpec((tr, D), lambda i: (i, 0))
    return pl.pallas_call(
        body, name=name, grid=(T // tr,), in_specs=[row, row],
        out_specs=(pl.BlockSpec((8, 128), lambda i: (0, 0)), row, row),
        out_shape=(jax.ShapeDtypeStruct((8, 128), F32), jax.ShapeDtypeStruct((T, D), F32), jax.ShapeDtypeStruct((T, D), BF16)),
        compiler_params=_params("arbitrary"),
    )(y, target)


def _position():
    return lax.axis_index("x"), lax.axis_index("y"), lax.axis_index("c")


def _all_gather(arrays, *, name):
    n = len(arrays)

    def body(*refs):
        srcs, outs = refs[:n], refs[n:2 * n]
        token, send_sems, recv_sems, local_sems = refs[2 * n:]
        token[...] = jnp.zeros_like(token)
        x, y, c = _position()
        me, sibling = (x, y, c), (x, y, 1 - c)
        chips = [(1 - x, y), (x, 1 - y), (1 - x, 1 - y)]

        def slot(a, px, py, pc):
            return outs[a].at[4 * px + 2 * py + pc]

        def copy(a, k, block, to, src=None):
            return pltpu.make_async_remote_copy(
                src_ref=slot(a, *block) if src is None else src, dst_ref=slot(a, *block),
                send_sem=send_sems.at[a, k], recv_sem=recv_sems.at[a, k], device_id=to, device_id_type=MESH)

        mine = [pltpu.make_async_copy(srcs[a], slot(a, *me), local_sems.at[a]) for a in range(n)]
        for cp in mine:
            cp.start()
        first, passed = [], []
        for a in range(n):
            first.append(copy(a, 0, me, sibling, src=srcs[a]))
            first += [copy(a, 1 + j, me, (*chip, c), src=srcs[a]) for j, chip in enumerate(chips)]
        for cp in first:
            cp.start()
        for a in range(n):
            for j, chip in enumerate(chips):
                copy(a, 1 + j, (*chip, c), me).wait_recv()
                fwd = copy(a, 4 + j, (*chip, c), sibling)
                fwd.start()
                passed.append(fwd)
        for a in range(n):
            copy(a, 0, sibling, me).wait_recv()
            for j, chip in enumerate(chips):
                copy(a, 4 + j, (*chip, 1 - c), me).wait_recv()
        for cp in first + passed:
            cp.wait_send()
        for cp in mine:
            cp.wait()

    return pl.pallas_call(
        body, name=name, in_specs=[ANY] * n, out_specs=[ANY] * n + [pl.BlockSpec(memory_space=pltpu.VMEM)],
        out_shape=[jax.ShapeDtypeStruct((N_DEV,) + a.shape, a.dtype) for a in arrays] + [jax.ShapeDtypeStruct((8, 128), F32)],
        scratch_shapes=[pltpu.SemaphoreType.DMA((n, 7)), pltpu.SemaphoreType.DMA((n, 7)), pltpu.SemaphoreType.DMA((n,))],
    )(*arrays)


def _sibling_exchange(by_core, whole, *, name):
    n1, n = len(by_core), len(by_core) + len(whole)

    def body(*refs):
        srcs, outs = refs[:n], refs[n:2 * n]
        send_sems, recv_sems = refs[2 * n:]
        x, y, c = _position()
        copies = [
            pltpu.make_async_remote_copy(src_ref=srcs[a].at[:, 1 - c] if a < n1 else srcs[a], dst_ref=outs[a],
                                         send_sem=send_sems.at[a], recv_sem=recv_sems.at[a], device_id=(x, y, 1 - c),
                                         device_id_type=MESH)
            for a in range(n)]
        for cp in copies:
            cp.start()
        for cp in copies:
            cp.wait()

    out_shape = [jax.ShapeDtypeStruct(a.shape[:1] + a.shape[2:], a.dtype) for a in by_core]
    out_shape += [jax.ShapeDtypeStruct(a.shape, a.dtype) for a in whole]
    outs = pl.pallas_call(
        body, name=name, in_specs=[ANY] * n, out_specs=[ANY] * n, out_shape=out_shape,
        scratch_shapes=[pltpu.SemaphoreType.DMA((n,)), pltpu.SemaphoreType.DMA((n,))],
    )(*by_core, *whole)
    return outs[:n1], outs[n1:]


HBM = pl.BlockSpec(memory_space=pltpu.HBM)
SEM = pl.BlockSpec(memory_space=pltpu.SEMAPHORE)
DATAFLOW = pltpu.SideEffectType.DATAFLOW_SIDE_EFFECTING


def _device(flat):
    return flat // 4, (flat // 2) % 2, flat % 2


def _gather_copies(srcs, lands, send_sems, recv_sems, incoming):
    x, y, c = _position()
    me = 4 * x + 2 * y + c
    pairs = []
    for a in range(len(srcs)):
        for d in range(1, N_DEV):
            to, frm = (me + d) % N_DEV, (me + N_DEV - d) % N_DEV
            k = a * (N_DEV - 1) + d - 1
            sems = dict(send_sem=send_sems.at[k], recv_sem=recv_sems.at[k], device_id_type=MESH)
            out = pltpu.make_async_remote_copy(src_ref=srcs[a], dst_ref=lands[a].at[me], device_id=_device(to), **sems)
            inc = pltpu.make_async_remote_copy(src_ref=srcs[a], dst_ref=lands[a].at[frm], device_id=_device(frm),
                                               **sems) if incoming else None
            pairs.append((out, inc))
    return pairs


def _chip_copies(srcs, lands, send_sems, recv_sems, incoming, n_whole=0):
    x, y, c = _position()
    my_chip = 2 * x + y
    pairs = []
    for a in range(len(srcs)):
        for k, (px, py) in enumerate([(1 - x, y), (x, 1 - y), (1 - x, 1 - y)]):
            sem = a * (N_CHIP - 1) + k
            sems = dict(send_sem=send_sems.at[sem], recv_sem=recv_sems.at[sem], device_id=(px, py, c), device_id_type=MESH)
            src = srcs[a] if a >= len(srcs) - n_whole else srcs[a].at[2 * px + py]
            out = pltpu.make_async_remote_copy(src_ref=src, dst_ref=lands[a].at[my_chip], **sems)
            inc = pltpu.make_async_remote_copy(src_ref=src, dst_ref=lands[a].at[2 * px + py], **sems) if incoming else None
            pairs.append((out, inc))
    return pairs


def _push_start(copies, fan, srcs, lands, *, name):
    n = len(srcs)

    def body(*refs):
        src_refs, land_refs = refs[:n], refs[n:2 * n]
        send_sems, recv_sems = refs[2 * n], refs[2 * n + 1]
        token = refs[-1]
        for out, _ in copies(src_refs, land_refs, send_sems, recv_sems, False):
            out.start()
        token[...] = jnp.zeros_like(token)

    outs = pl.pallas_call(
        body, name=name,
        out_shape=(pltpu.SemaphoreType.DMA((n * fan,)), pltpu.SemaphoreType.DMA((n * fan,)),
                   *(pltpu.HBM(a.shape, a.dtype) for a in srcs), *(pltpu.HBM(a.shape, a.dtype) for a in lands),
                   jax.ShapeDtypeStruct((8, 128), F32)),
        in_specs=[HBM] * (2 * n), out_specs=(SEM, SEM, *([HBM] * (2 * n)), pl.BlockSpec(memory_space=pltpu.VMEM)),
        input_output_aliases={i: 2 + i for i in range(2 * n)},
        compiler_params=pltpu.CompilerParams(has_side_effects=DATAFLOW),
    )(*(pltpu.with_memory_space_constraint(a, pltpu.HBM) for a in (*srcs, *lands)))
    return outs[0], outs[1], list(outs[2:2 + n]), list(outs[2 + n:2 + 2 * n]), outs[-1]


def _push_wait(copies, send_sems, recv_sems, srcs, lands, after, *, name):
    n = len(srcs)

    def body(*refs):
        src_refs, land_refs = refs[:n], refs[n:2 * n]
        for out, inc in copies(src_refs, land_refs, refs[2 * n], refs[2 * n + 1], True):
            out.wait_send()
            inc.wait_recv()
        refs[-1][...] = jnp.zeros_like(refs[-1])

    outs = pl.pallas_call(
        body, name=name,
        out_shape=(*(pltpu.HBM(a.shape, a.dtype) for a in (*srcs, *lands)), jax.ShapeDtypeStruct((8, 128), F32)),
        in_specs=[HBM] * (2 * n) + [SEM, SEM, ANY], out_specs=(*([HBM] * (2 * n)), pl.BlockSpec(memory_space=pltpu.VMEM)),
        input_output_aliases={i: i for i in range(2 * n)},
        compiler_params=pltpu.CompilerParams(has_side_effects=DATAFLOW),
    )(*srcs, *lands, send_sems, recv_sems, after)
    return list(outs[n:2 * n]), outs[-1]


def _with_own_slot(block, index, slots):
    buf = lax.empty((slots,) + block.shape, block.dtype)
    return lax.dynamic_update_slice(buf, block[None], (index,) + (0,) * block.ndim)


def _view2d(shape):
    return math.prod(shape[:-1]), shape[-1]


def _pair_sum(mine, other, core, *, name, out_dtype, own_chip=None):
    by_core = mine.ndim == 4
    n, w = other.shape[-2:]
    tr = _row_tile(n, w * 2)
    out = jax.ShapeDtypeStruct(other.shape, out_dtype)

    if not by_core:
        def body(a_ref, b_ref, o_ref):
            o_ref[...] = (a_ref[...].astype(F32) + b_ref[...].astype(F32)).astype(o_ref.dtype)

        row = pl.BlockSpec((tr, w), lambda i: (i, 0))
        return pl.pallas_call(body, name=name, grid=(n // tr,), in_specs=[row, row], out_specs=row, out_shape=out,
                              compiler_params=_params("parallel"))(mine, other)

    def body(where_ref, a_ref, b_ref, o_ref, land_ref):
        v = (a_ref[...].astype(F32) + b_ref[...].astype(F32)).astype(o_ref.dtype)
        o_ref[...] = v

        @pl.when(pl.program_id(1) == where_ref[1])
        def _():
            land_ref[...] = v

    a_spec = pl.BlockSpec((None, None, tr, w), lambda i, j, where_ref: (j, where_ref[0], i, 0))
    o_spec = pl.BlockSpec((None, tr, w), lambda i, j, where_ref: (j, i, 0))
    land_spec = pl.BlockSpec((None, tr, w), lambda i, j, where_ref: (where_ref[1], i, 0))
    grid_spec = pltpu.PrefetchScalarGridSpec(num_scalar_prefetch=1, grid=(n // tr, other.shape[0]), in_specs=[a_spec, o_spec],
                                             out_specs=(o_spec, land_spec))
    return pl.pallas_call(body, name=name, grid_spec=grid_spec, out_shape=(out, out),
                          compiler_params=_params("parallel", "arbitrary"))(jnp.stack([core, own_chip]), mine, other)


def _adamw(parts, w, m, v, *, name):
    layers = len(parts)
    n_parts, R, W = parts[0].shape
    tr = _row_tile(R, W * 2)
    per_layer = R // tr

    def update(p_ref, w_ref, m_ref, v_ref, g_out, d_out, m_out, v_out):
        g = p_ref[0].astype(F32)
        for j in range(1, n_parts):
            g = g + p_ref[j].astype(F32)
        m_new = ADAM_B1 * m_ref[...] + (1.0 - ADAM_B1) * g
        v_new = ADAM_B2 * v_ref[...] + (1.0 - ADAM_B2) * (g * g)
        m_hat = m_new / (1.0 - ADAM_B1 ** ADAM_STEP)
        v_hat = v_new / (1.0 - ADAM_B2 ** ADAM_STEP)
        g_out[...] = g
        d_out[...] = -ADAM_LR * (m_hat / (jnp.sqrt(v_hat) + ADAM_EPS) + ADAM_WD * w_ref[...])
        m_out[...] = m_new
        v_out[...] = v_new

    def body(*refs):
        for k in range(layers):
            pl.when(pl.program_id(0) == k)(lambda k=k: update(refs[k], *refs[layers:]))

    def parts_spec(k):
        return pl.BlockSpec((n_parts, tr, W), lambda l, i: (0, jnp.where(l == k, i, 0), 0))

    row = pl.BlockSpec((tr, W), lambda l, i: (l * per_layer + i, 0))
    out = jax.ShapeDtypeStruct((layers * R, W), F32)
    return pl.pallas_call(
        body, name=name, grid=(layers, per_layer), in_specs=[parts_spec(k) for k in range(layers)] + [row, row, row],
        out_specs=(row, row, row, row), out_shape=(out, out, out, out), compiler_params=_params("arbitrary", "arbitrary"),
    )(*parts, w, m, v)


SMALL_ROWS = 608


def _pack_small(p):
    flat = jnp.concatenate([p[n].reshape(-1).astype(F32) for n in SMALL_NAMES])
    return jnp.pad(flat, (0, SMALL_ROWS * PACK_W - flat.shape[0])).reshape(SMALL_ROWS, PACK_W)


def _unpack_small(buf, like):
    out, at = {}, 0
    flat = buf.reshape(-1)
    for n in SMALL_NAMES:
        size = math.prod(like[n].shape)
        out[n] = flat[at:at + size].reshape(like[n].shape)
        at += size
    return out


def _block_diag(pw):
    out = jnp.zeros((MAIN_WIDTH, MAIN_WIDTH), pw.dtype)
    for g in range(POOL_GROUPS):
        out = lax.dynamic_update_slice(out, pw[g], (g * POOL_GROUP_DIM, g * POOL_GROUP_DIM))
    return out


def _diag_blocks(m):
    return jnp.stack([m[g * POOL_GROUP_DIM:(g + 1) * POOL_GROUP_DIM, g * POOL_GROUP_DIM:(g + 1) * POOL_GROUP_DIM]
                      for g in range(POOL_GROUPS)])


def _train_pass(x, mem, target, p, w_kv, fetch, reduce_layer, reduce_wait):
    T = x.shape[0]
    mem_cols = MAIN_WIDTH // MEM_WIDTH
    k_gain = _head_gain(p["k_norm"], SWA_KV_HEADS)
    saved = []
    h = x
    kn = kv = h_kv = hn_kv = None
    for l in range(DEPTH):
        s = {}
        wl, token = fetch(l, h)
        s["w"] = wl
        if l == N_A:
            h_kv = h
            hn_kv, kv = _norm_mm(h, p["kv_norm"], w_kv, b_kind="rows", name="kv_proj")
            kn = _seg_rms_fwd(kv, k_gain, width=KV_HALF, col=0, name="k_norm_fwd")
        s["h"] = h
        s["xn1"], proj = _norm_mm(h, p["norm_mix"][l] + token, wl["w_in"], b_kind="rows", name="in_proj")
        s["proj"] = proj
        s["memn"] = _rms_fwd(mem, p["mem_norm"][l], name="mem_norm_fwd")
        s["mkv"] = _mm(s["memn"], wl["w_mem_kv"], b_kind="rows", name="mem_kv_proj")
        s["mk_gain"] = _head_gain(p["mem_k_norm"][l], MEM_HEADS)
        s["mkn"] = _seg_rms_fwd(s["mkv"], s["mk_gain"], width=MEM_WIDTH, col=0, name="mem_k_norm_fwd")
        if l < N_A:
            s["q_gain"] = _head_gain(p["mem_q_norm"][l], MEM_HEADS)
            s["qn"] = _seg_rms_fwd(proj, s["q_gain"], width=MEM_WIDTH, col=mem_cols, name="mem_q_norm_fwd")
            s["q_col"] = 0
        else:
            j = l - N_A
            s["q_gain"] = jnp.concatenate([_head_gain(p["q_norm"][j], SWA_Q_HEADS), _head_gain(p["mem_q_norm"][l], MEM_HEADS)],
                                          axis=1)
            s["qn"] = _seg_rms_fwd(proj, s["q_gain"], width=D_MODEL, col=0, name="q_norm_fwd")
            s["q_col"] = mem_cols
        cat = _mem_attn_fwd(s["qn"], s["q_col"], s["mkn"], s["mkv"], name="mem_attn_fwd")
        if l < N_A:
            s["mix"] = _block_diag(p["pool_w"][l])
            s["scale"] = p["pool_scale"][l].reshape(1, MAIN_WIDTH)
            s["cat"] = _pool_mix_fwd(proj, s["mix"], s["scale"], cat, name="pool_fwd")
        else:
            s["cat"] = _swa_attn_fwd(s["qn"], kn, kv, p["sinks"][l - N_A], cat, name="swa_fwd")
        s["h1"] = _mm(s["cat"], wl["w_out"], b_kind="rows", res=h, name="out_proj")
        s["xn2"], s["a"] = _norm_mm(s["h1"], p["norm_mlp"][l], wl["w_up"], b_kind="layers", relu2=True, name="mlp_up")
        h = _mm(s["a"], wl["w_down"], b_kind="rows", res=s["h1"], name="mlp_down")
        saved.append(s)

    loss, dh, dh_b = _loss(h, target, name="loss_head")

    g = {n: [None] * DEPTH for n in ("norm_mix", "mem_norm", "mem_q_norm", "mem_k_norm", "norm_mlp")}
    g.update({n: [None] * N_A for n in ("pool_w", "pool_scale", "q_norm", "sinks")})
    g_kv = None
    token = None
    dks, dvs = [], []
    for l in reversed(range(DEPTH)):
        s = saved[l]
        wl = s["w"]
        gb = {}

        def dw(a, dy, n):
            return _mm(a, dy, ta=True, out_kind="layers" if n == "w_up" else "rows", out_buf=lax.empty(wl[n].shape, BF16),
                       name=n + "_grad")

        norm_mlp_gain = p["norm_mlp"][l] if token is None else p["norm_mlp"][l] + token
        gb["w_down"] = dw(s["a"], dh_b, "w_down")
        du = _mm(dh_b, wl["w_down"], tb=True, b_kind="rows", mul2=s["a"], out_dtype=BF16, name="mlp_down_dx")
        gb["w_up"] = dw(s["xn2"], du, "w_up")
        early = reduce_layer(l, gb, early=True)
        if early is not None:
            norm_mlp_gain = norm_mlp_gain + early
        dh1, dh1_b, g["norm_mlp"][l] = _mm_rms_bwd(du, wl["w_up"], s["h1"], norm_mlp_gain, dh, b_kind="layers", also_bf16=True,
                                                   name="mlp_up_dx")
        gb["w_out"] = dw(s["cat"], dh1_b, "w_out")
        dcat = _mm(dh1_b, wl["w_out"], tb=True, b_kind="rows", name="out_proj_dx")
        if l < N_A:
            dq, dmk, dmv = _mem_attn_bwd(s["qn"], s["q_col"], s["mkn"], s["mkv"], dcat, dq_width=MEM_WIDTH, name="mem_attn_bwd")
            dproj, dmix, dscale = _pool_mix_bwd(s["proj"], s["mix"], s["scale"], dcat, name="pool_bwd")
            g["pool_w"][l] = _diag_blocks(dmix)
            g["pool_scale"][l] = dscale.reshape(MAIN_WIDTH)
            dproj, dgain = _seg_rms_bwd(s["proj"], s["q_gain"], [dq], width=MEM_WIDTH, col=mem_cols, out_buf=dproj,
                                        out_col=mem_cols, name="mem_q_norm_bwd")
            g["mem_q_norm"][l] = _fold_heads(dgain, MEM_HEADS)
        else:
            j = l - N_A
            dqn, dmk, dmv = _mem_attn_bwd(s["qn"], s["q_col"], s["mkn"], s["mkv"], dcat, dq_width=D_MODEL, name="mem_attn_bwd")
            dqn, dk_l, dv_l, dsinks = _swa_attn_bwd(s["qn"], kn, kv, p["sinks"][j], dcat, dqn, name="swa_bwd")
            dks.append(dk_l)
            dvs.append(dv_l)
            g["sinks"][j] = dsinks[0, :SWA_Q_HEADS]
            dproj, dgain = _seg_rms_bwd(s["proj"], s["q_gain"], [dqn], width=D_MODEL, col=0, name="q_norm_bwd")
            g["q_norm"][j] = _fold_heads(dgain[:, :MAIN_WIDTH], SWA_Q_HEADS)
            g["mem_q_norm"][l] = _fold_heads(dgain[:, MAIN_WIDTH:], MEM_HEADS)
        dmk_raw, dgain = _seg_rms_bwd(s["mkv"], s["mk_gain"], [dmk], width=MEM_WIDTH, col=0, name="mem_k_norm_bwd")
        g["mem_k_norm"][l] = _fold_heads(dgain, MEM_HEADS)
        dmkv = jnp.concatenate([dmk_raw, dmv.astype(BF16)], axis=1)
        gb["w_mem_kv"] = dw(s["memn"], dmkv, "w_mem_kv")
        dmemn = _mm(dmkv, wl["w_mem_kv"], tb=True, b_kind="rows", name="mem_kv_proj_dx")
        g["mem_norm"][l] = _rms_bwd(mem, p["mem_norm"][l], [dmemn], want_dx=False, name="mem_norm_bwd")
        gb["w_in"] = dw(s["xn1"], dproj, "w_in")
        if l in (0, N_A):
            dh, g["norm_mix"][l] = _mm_rms_bwd(dproj, wl["w_in"], s["h"], p["norm_mix"][l], dh1, b_kind="rows", also_bf16=False,
                                               name="in_proj_dx")
        else:
            dh, dh_b, g["norm_mix"][l] = _mm_rms_bwd(dproj, wl["w_in"], s["h"], p["norm_mix"][l], dh1, b_kind="rows",
                                                     also_bf16=True, name="in_proj_dx")
        if l == N_A:
            dkv, dgain = _seg_rms_bwd(kv, k_gain, dks, width=KV_HALF, col=0, out_buf=lax.empty((T, 2 * KV_HALF), BF16),
                                      name="k_norm_bwd")
            g["k_norm"] = _fold_heads(dgain, SWA_KV_HEADS)
            dkv = _sum_into(dvs[0], dvs[1], dkv, 1, name="dv_sum")
            g_kv = _mm(hn_kv, dkv, ta=True, out_kind="rows", out_buf=lax.empty(w_kv.shape, BF16), name="w_kv_grad")
            dh, dh_b, g["kv_norm"] = _mm_rms_bwd(dkv, w_kv, h_kv, p["kv_norm"], dh, b_kind="rows", also_bf16=True,
                                                 name="kv_proj_dx")
        if l + 1 < DEPTH:
            reduce_wait(l + 1, dh)
        token = reduce_layer(l, gb)
    grads = {n: (jnp.stack(v) if isinstance(v, list) else v) for n, v in g.items()}
    return loss, dh, grads, g_kv


def kernel(x, mem, norm_mix, w_in, pool_w, pool_scale, kv_norm, w_kv, k_norm, q_norm, sinks, mem_norm, w_mem_kv, mem_q_norm, mem_k_norm, w_out, norm_mlp, w_up, w_down, loss_target, m_norm_mix, m_w_in, m_pool_w, m_pool_scale, m_kv_norm, m_w_kv, m_k_norm, m_q_norm, m_sinks, m_mem_norm, m_w_mem_kv, m_mem_q_norm, m_mem_k_norm, m_w_out, m_norm_mlp, m_w_up, m_w_down, v_norm_mix, v_w_in, v_pool_w, v_pool_scale, v_kv_norm, v_w_kv, v_k_norm, v_q_norm, v_sinks, v_mem_norm, v_w_mem_kv, v_mem_q_norm, v_mem_k_norm, v_w_out, v_norm_mlp, v_w_up, v_w_down):
    weights = dict(norm_mix=norm_mix, w_in=w_in, pool_w=pool_w, pool_scale=pool_scale, kv_norm=kv_norm, w_kv=w_kv,
                   k_norm=k_norm, q_norm=q_norm, sinks=sinks, mem_norm=mem_norm, w_mem_kv=w_mem_kv,
                   mem_q_norm=mem_q_norm, mem_k_norm=mem_k_norm, w_out=w_out, norm_mlp=norm_mlp, w_up=w_up, w_down=w_down)
    mom1 = dict(norm_mix=m_norm_mix, w_in=m_w_in, pool_w=m_pool_w, pool_scale=m_pool_scale, kv_norm=m_kv_norm, w_kv=m_w_kv,
                k_norm=m_k_norm, q_norm=m_q_norm, sinks=m_sinks, mem_norm=m_mem_norm, w_mem_kv=m_w_mem_kv,
                mem_q_norm=m_mem_q_norm, mem_k_norm=m_mem_k_norm, w_out=m_w_out, norm_mlp=m_norm_mlp, w_up=m_w_up,
                w_down=m_w_down)
    mom2 = dict(norm_mix=v_norm_mix, w_in=v_w_in, pool_w=v_pool_w, pool_scale=v_pool_scale, kv_norm=v_kv_norm, w_kv=v_w_kv,
                k_norm=v_k_norm, q_norm=v_q_norm, sinks=v_sinks, mem_norm=v_mem_norm, w_mem_kv=v_w_mem_kv,
                mem_q_norm=v_mem_q_norm, mem_k_norm=v_mem_k_norm, w_out=v_w_out, norm_mlp=v_norm_mlp, w_up=v_w_up,
                w_down=v_w_down)
    names = list(weights)
    x_pos, y_pos, core = (lax.axis_index(n).astype(jnp.int32) for n in AXES)
    me, my_chip = 4 * x_pos + 2 * y_pos + core, 2 * x_pos + y_pos
    shard = MAIN_WIDTH // N_DEV

    def layer_shards(l, zero=0.0):
        return [(weights[n][l:l + 1] + zero).astype(BF16) for n in LAYERED]

    def usable(arrays):
        wl = dict(zip(LAYERED, arrays))
        wl["w_up"] = wl["w_up"].transpose(1, 2, 0, 3).reshape(1, D_MODEL, D_FF)
        return wl

    scale_block = jnp.pad(pool_scale, ((0, 8 - N_A), (0, 128 - shard)))
    *first, first_done = _all_gather(layer_shards(0) + [w_kv[None].astype(BF16), scale_block], name="gather_first")
    p = {n: weights[n] for n in SMALL_NAMES}
    p["pool_scale"] = first[-1][:, :N_A, :shard].transpose(1, 0, 2).reshape(N_A, MAIN_WIDTH)
    gathers, reduces, parts = {}, {}, {}

    def fetch(l, after):
        if l == 0:
            got, done = first[:len(LAYERED)], first_done
        else:
            got, done = _push_wait(_gather_copies, *gathers.pop(l), after, name=f"gather_wait_{l}")
        token = 0.0
        if l + 1 < DEPTH:
            srcs = layer_shards(l + 1, done[0, 0])
            lands = [_with_own_slot(a, me, N_DEV) for a in srcs]
            *handles, block = _push_start(_gather_copies, N_DEV - 1, srcs, lands, name=f"gather_start_{l + 1}")
            gathers[l + 1], token = handles, block[0, 0]
        return usable(got), token

    def by_core(gb):
        gb = dict(gb)
        if "w_up" in gb:
            gb["w_up"] = gb["w_up"].reshape(D_MODEL, N_DEV, D_FF // N_DEV).transpose(1, 0, 2)
        order = [n for n in LAYERED if n in gb] + [n for n in gb if n not in LAYERED]
        return {n: gb[n].reshape((N_CHIP, 2) + _view2d(gb[n].shape[1:] if n == "w_up" else gb[n].shape[2:])) for n in order}

    def pair_sums(views, sib, tag):
        return [_pair_sum(a, b, core, name=f"chip_sum_{n}_{tag}", out_dtype=BF16, own_chip=my_chip)
                for (n, a), b in zip(views.items(), sib)]

    def chip_sums(gb, tag, whole=()):
        views = by_core(gb)
        sib, sib_whole = _sibling_exchange(list(views.values()), list(whole), name="reduce_sibling_" + tag)
        return pair_sums(views, sib, tag), sib_whole

    def start_chip_exchange(sums, tag, whole=()):
        lands = [land for _, land in sums] + [_with_own_slot(a, my_chip, N_CHIP) for a in whole]
        copies = functools.partial(_chip_copies, n_whole=len(whole))
        *handles, block = _push_start(copies, N_CHIP - 1, [*(s for s, _ in sums), *whole], lands, name="reduce_start_" + tag)
        return (copies, *handles), block[0, 0]

    mlp = ("w_up", "w_down")

    def reduce_layer(l, gb, early=False):
        if early and l > 0:
            return None
        if l == 0 and not early:
            reduces["rest"] = {n: a for n, a in gb.items() if n not in mlp}
            return None
        tag = "0_mlp" if early else str(l)
        sums, _ = chip_sums({n: gb[n] for n in mlp} if early else gb, tag)
        reduces[l], token = start_chip_exchange(sums, tag)
        return token

    def reduce_wait(l, after):
        copies, *handles = reduces.pop(l)
        return _push_wait(copies, *handles, after, name=f"reduce_wait_{l}")[0]

    def layer_wait(l, after):
        parts[l] = reduce_wait(l, after)

    loss, grad_x, grads, g_kv = _train_pass(x[0], mem[0], loss_target[0], p, first[len(LAYERED)], fetch, reduce_layer, layer_wait)

    last = dict(reduces.pop("rest"))
    last["w_kv"] = g_kv
    last["pool_scale"] = grads["pool_scale"].reshape(N_A, N_DEV, shard).transpose(1, 0, 2).astype(BF16)[:, None]
    small = _pack_small(grads)
    sums, (sib_small,) = chip_sums(last, "0", whole=[small])
    chip_small = _pair_sum(small, sib_small, core, name="chip_sum_small", out_dtype=F32)
    reduces["rest"], _ = start_chip_exchange(sums, "0_rest", whole=[chip_small])

    def adamw(n, n_parts):
        res = _adamw(n_parts, *(d[n].reshape(_view2d(d[n].shape)) for d in (weights, mom1, mom2)), name="adamw_" + n)
        return [r.reshape(weights[n].shape) for r in res]

    p_up, p_down = reduce_wait(0, chip_small)
    parts[0] = [None, None, None, p_up, p_down]
    new = {n: adamw(n, [parts[l][LAYERED.index(n)] for l in range(DEPTH)]) for n in mlp}
    p_in, p_mem_kv, p_out, parts_kv, parts_scale, parts_small = reduce_wait("rest", new["w_down"][0])
    parts[0][:3] = [p_in, p_mem_kv, p_out]
    new.update({n: adamw(n, [parts[l][k] for l in range(DEPTH)]) for k, n in enumerate(LAYERED) if n not in mlp})
    new["w_kv"] = adamw("w_kv", [parts_kv])
    new["pool_scale"] = adamw("pool_scale", [parts_scale])
    res = _adamw([parts_small], _pack_small(weights), _pack_small(mom1), _pack_small(mom2), name="adamw_replicated")
    for n, vals in zip(SMALL_NAMES, zip(*(_unpack_small(r, weights).values() for r in res))):
        new[n] = list(vals)
    outs = [new[n][k] for k in range(4) for n in names]
    total = lax.psum(loss[0, 0], AXES)
    return (total, grad_x[None], *outs)
```

```python
import functools
import math

import jax
import jax.numpy as jnp
from jax import lax
from jax.experimental import pallas as pl
from jax.experimental.pallas import tpu as pltpu

F32 = jnp.float32
BF16 = jnp.bfloat16
MESH = pl.DeviceIdType.MESH
AXES = ("x", "y", "c")

D_MODEL = 1024
DEPTH = 4
N_A = 2
HEAD_DIM = 64
MEM_HEADS = 4
MEM_WIDTH = MEM_HEADS * HEAD_DIM
MAIN_WIDTH = D_MODEL - MEM_WIDTH
POOL_GROUPS = 4
POOL_GROUP_DIM = MAIN_WIDTH // POOL_GROUPS
POOL_HALO = 16
SWA_Q_HEADS = MAIN_WIDTH // HEAD_DIM
SWA_KV_HEADS = 4
SWA_GROUP = SWA_Q_HEADS // SWA_KV_HEADS
KV_HALF = SWA_KV_HEADS * HEAD_DIM
BLOCK = 128
D_FF = 4 * D_MODEL
EPS = 1e-6
SCALE = HEAD_DIM ** -0.5
NEG = float(jnp.finfo(jnp.float32).min)
N_DEV = 8
N_CHIP = 4

ADAM_LR = 0.001
ADAM_B1 = 0.9
ADAM_B2 = 0.999
ADAM_EPS = 1e-08
ADAM_WD = 0.01
ADAM_STEP = 10

PACK_W = 512
VMEM_LIMIT = 52 * 1024 * 1024
MM_TILE = 1024
LAYERED = ("w_in", "w_mem_kv", "w_out", "w_up", "w_down")
SMALL_NAMES = ("norm_mix", "pool_w", "kv_norm", "k_norm", "q_norm", "sinks", "mem_norm", "mem_q_norm", "mem_k_norm",
               "norm_mlp")


ANY = pl.BlockSpec(memory_space=pl.ANY)


def _params(*sem):
    return pltpu.CompilerParams(dimension_semantics=sem, vmem_limit_bytes=VMEM_LIMIT)


def _mm(a, b, *, name, ta=False, tb=False, b_kind=None, layer=0, res=None, relu2=False, mul2=None, out_dtype=F32,
        out_kind=None, out_buf=None):
    if ta:
        K, M = a.shape
    else:
        M, K = a.shape
    if b_kind is None:
        rows_b, cols_b = b.shape
    elif b_kind == "rows":
        rows_b, cols_b = b.shape[0] * b.shape[2], b.shape[3]
    else:
        rows_b, cols_b = b.shape[1:]
    N, K2 = (rows_b, cols_b) if tb else (cols_b, rows_b)
    assert K == K2, (a.shape, b.shape)
    tm = min(M, MM_TILE if K <= MM_TILE else MM_TILE // 2)
    tn = min(N, MM_TILE)
    assert M % tm == 0 and N % tn == 0
    row_tile, col_tile = (tn, K) if tb else (K, tn)
    a_spec = pl.BlockSpec((K, tm), lambda j, i: (0, i)) if ta else pl.BlockSpec((tm, K), lambda j, i: (i, 0))

    def rc(j):
        return (j, 0) if tb else (0, j)

    if b_kind is None:
        b_spec = pl.BlockSpec((row_tile, col_tile), lambda j, i: rc(j))
    elif b_kind == "rows":
        per = row_tile // b.shape[2]
        b_spec = pl.BlockSpec((per, None, b.shape[2], col_tile), lambda j, i: (rc(j)[0], layer, 0, rc(j)[1]))
    else:
        b_spec = pl.BlockSpec((None, row_tile, col_tile), lambda j, i: (layer, *rc(j)))
    o_spec = pl.BlockSpec((tm, tn), lambda j, i: (i, j))
    dn = (((0 if ta else 1,), (1 if tb else 0,)), ((), ()))
    extra = [e for e in (res, mul2) if e is not None]
    n_in = 2 + len(extra) + (1 if out_buf is not None else 0)

    def body(*refs):
        a_ref, b_ref = refs[0], refs[1]
        extra_refs = refs[2:2 + len(extra)]
        out = refs[n_in]
        bv = b_ref[...].astype(BF16).reshape(row_tile, col_tile)
        v = lax.dot_general(a_ref[...].astype(BF16), bv, dn, preferred_element_type=F32)
        if res is not None:
            v = extra_refs[0][...] + v
        elif mul2 is not None:
            v = v * (2.0 * jnp.sqrt(extra_refs[0][...].astype(F32)))
        if relu2:
            r = jnp.maximum(v, 0.0)
            v = r * r
        out[...] = v.astype(out.dtype).reshape(out.shape)

    in_specs = [a_spec, b_spec] + [o_spec] * len(extra)
    operands = [a, b, *extra]
    aliases = {}
    if out_kind is None:
        out_shape = jax.ShapeDtypeStruct((M, N), BF16 if relu2 else out_dtype)
        out_specs = o_spec
    else:
        if out_kind == "rows":
            s = out_buf.shape[2]
            out_specs = pl.BlockSpec((tm // s, None, s, tn), lambda j, i: (i, layer, 0, j))
        else:
            out_specs = pl.BlockSpec((None, tm, tn), lambda j, i: (layer, i, j))
        out_shape = jax.ShapeDtypeStruct(out_buf.shape, out_buf.dtype)
        in_specs.append(ANY)
        operands.append(out_buf)
        aliases = {len(operands) - 1: 0}
    return pl.pallas_call(
        body, name=name, grid=(N // tn, M // tm), in_specs=in_specs, out_specs=out_specs, out_shape=out_shape,
        input_output_aliases=aliases, compiler_params=_params("parallel", "parallel"),
    )(*operands)


def _weight_block(b, b_kind, transposed, tn):
    if b_kind == "rows":
        s = b.shape[2]
        rows, cols = b.shape[0] * s, b.shape[3]
        if transposed:
            return (lambda at: pl.BlockSpec((b.shape[0], None, s, cols), lambda *g: (0, 0, 0, 0))), rows, cols
        return (lambda at: pl.BlockSpec((b.shape[0], None, s, tn), lambda *g: (0, 0, 0, at(*g)))), rows, cols
    rows, cols = b.shape[1:]
    if transposed:
        return (lambda at: pl.BlockSpec((None, rows, cols), lambda *g: (0, 0, 0))), rows, cols
    return (lambda at: pl.BlockSpec((None, rows, tn), lambda *g: (0, 0, at(*g)))), rows, cols


def _norm_mm(x, gain, b, *, b_kind, name, relu2=False):
    M, K = x.shape
    tm = min(M, MM_TILE)
    spec_of, rows, N = _weight_block(b, b_kind, False, min(MM_TILE, b.shape[-1]))
    tn = min(N, MM_TILE)
    assert rows == K and M % tm == 0 and N % tn == 0

    def body(x_ref, g_ref, b_ref, xn_ref, o_ref):
        @pl.when(pl.program_id(1) == 0)
        def _():
            xv = x_ref[...]
            r = lax.rsqrt(jnp.mean(xv * xv, axis=-1, keepdims=True) + EPS)
            xn_ref[...] = ((xv * r) * g_ref[...]).astype(xn_ref.dtype)

        v = jnp.dot(xn_ref[...], b_ref[...].astype(BF16).reshape(K, tn), preferred_element_type=F32)
        if relu2:
            r2 = jnp.maximum(v, 0.0)
            v = r2 * r2
        o_ref[...] = v.astype(o_ref.dtype)

    rows_spec = pl.BlockSpec((tm, K), lambda i, j: (i, 0))
    return pl.pallas_call(
        body, name=name, grid=(M // tm, N // tn),
        in_specs=[rows_spec, pl.BlockSpec((1, K), lambda i, j: (0, 0)), spec_of(lambda i, j: j)],
        out_specs=(rows_spec, pl.BlockSpec((tm, tn), lambda i, j: (i, j))),
        out_shape=(jax.ShapeDtypeStruct((M, K), BF16), jax.ShapeDtypeStruct((M, N), BF16 if relu2 else F32)),
        compiler_params=_params("parallel", "arbitrary"),
    )(x, gain.reshape(1, K), b)


def _mm_rms_bwd(a, b, x, gain, res, *, b_kind, name, also_bf16):
    M, K = a.shape
    spec_of, N, cols = _weight_block(b, b_kind, True, None)
    assert cols == K and x.shape == (M, N)
    tm = min(M, MM_TILE if K <= MM_TILE else MM_TILE // 2)
    assert M % tm == 0

    def body(a_ref, b_ref, x_ref, g_ref, res_ref, *outs):
        i = pl.program_id(0)
        dy = lax.dot_general(a_ref[...].astype(BF16), b_ref[...].astype(BF16).reshape(N, K), (((1,), (1,)), ((), ())),
                             preferred_element_type=F32)
        xv = x_ref[...]
        r = lax.rsqrt(jnp.mean(xv * xv, axis=-1, keepdims=True) + EPS)
        xh = xv * r
        part = jnp.sum(dy * xh, axis=0, keepdims=True)
        dg_ref = outs[-1]

        @pl.when(i == 0)
        def _():
            dg_ref[...] = part

        @pl.when(i > 0)
        def _():
            dg_ref[...] += part

        gdy = dy * g_ref[...]
        dx = res_ref[...] + r * (gdy - xh * jnp.mean(gdy * xh, axis=-1, keepdims=True))
        outs[0][...] = dx
        if also_bf16:
            outs[1][...] = dx.astype(BF16)

    row = pl.BlockSpec((tm, N), lambda i: (i, 0))
    vec = pl.BlockSpec((1, N), lambda i: (0, 0))
    out_specs = [row] + ([row] if also_bf16 else []) + [vec]
    out_shape = [jax.ShapeDtypeStruct((M, N), F32)] + ([jax.ShapeDtypeStruct((M, N), BF16)] if also_bf16 else [])
    outs = pl.pallas_call(
        body, name=name, grid=(M // tm,),
        in_specs=[pl.BlockSpec((tm, K), lambda i: (i, 0)), spec_of(None), row, vec, row], out_specs=out_specs,
        out_shape=out_shape + [jax.ShapeDtypeStruct((1, N), F32)], compiler_params=_params("arbitrary"),
    )(a, b, x, gain.reshape(1, N), res)
    return (*outs[:-1], outs[-1].reshape(N))


def _row_tile(rows, d):
    t = min(rows, (512 * 1024) // d)
    while rows % t or (t != rows and t % 16):
        t -= 1
    return t


def _rms_fwd(x, g, *, name, out_dtype=BF16):
    R, D = x.shape
    tr = _row_tile(R, D)

    def body(x_ref, g_ref, o_ref):
        xv = x_ref[...].astype(F32)
        r = lax.rsqrt(jnp.mean(xv * xv, axis=-1, keepdims=True) + EPS)
        o_ref[...] = ((xv * r) * g_ref[...]).astype(o_ref.dtype)

    return pl.pallas_call(
        body, name=name, grid=(R // tr,),
        in_specs=[pl.BlockSpec((tr, D), lambda i: (i, 0)), pl.BlockSpec((1, D), lambda i: (0, 0))],
        out_specs=pl.BlockSpec((tr, D), lambda i: (i, 0)), out_shape=jax.ShapeDtypeStruct((R, D), out_dtype),
        compiler_params=_params("parallel"),
    )(x, g.reshape(1, D))


def _rms_bwd(x, g, dys, *, name, res=None, want_dx=True, also_bf16=False):
    R, D = x.shape
    tr = _row_tile(R, D)
    n_dy = len(dys)
    has_res = res is not None

    def body(*refs):
        x_ref, g_ref = refs[0], refs[1]
        dy_refs = refs[2:2 + n_dy]
        res_ref = refs[2 + n_dy] if has_res else None
        outs = refs[2 + n_dy + (1 if has_res else 0):]
        dg_ref = outs[-1]
        i = pl.program_id(0)
        xv = x_ref[...].astype(F32)
        dy = dy_refs[0][...].astype(F32)
        for extra in dy_refs[1:]:
            dy = dy + extra[...].astype(F32)
        r = lax.rsqrt(jnp.mean(xv * xv, axis=-1, keepdims=True) + EPS)
        xh = xv * r
        part = jnp.sum(dy * xh, axis=0, keepdims=True)

        @pl.when(i == 0)
        def _():
            dg_ref[...] = part

        @pl.when(i > 0)
        def _():
            dg_ref[...] += part

        if want_dx:
            gdy = dy * g_ref[...]
            dx = r * (gdy - xh * jnp.mean(gdy * xh, axis=-1, keepdims=True))
            if has_res:
                dx = res_ref[...] + dx
            outs[0][...] = dx
            if also_bf16:
                outs[1][...] = dx.astype(BF16)

    row = pl.BlockSpec((tr, D), lambda i: (i, 0))
    vec = pl.BlockSpec((1, D), lambda i: (0, 0))
    out_shape = [jax.ShapeDtypeStruct((1, D), F32)]
    out_specs = [vec]
    if also_bf16:
        out_shape = [jax.ShapeDtypeStruct((R, D), BF16)] + out_shape
        out_specs = [row] + out_specs
    if want_dx:
        out_shape = [jax.ShapeDtypeStruct((R, D), F32)] + out_shape
        out_specs = [row] + out_specs
    outs = pl.pallas_call(
        body, name=name, grid=(R // tr,),
        in_specs=[row, vec] + [row] * (n_dy + (1 if has_res else 0)), out_specs=out_specs, out_shape=out_shape,
        compiler_params=_params("arbitrary"),
    )(x, g.reshape(1, D), *dys, *([res] if has_res else []))
    return (*outs[:-1], outs[-1].reshape(D)) if want_dx else outs[0].reshape(D)


POOL_TILE = 512
MEM_Q_TILE = 1024


def _softmax(q, k, bias, valid, sink):
    s = lax.dot_general(q, k, (((1,), (1,)), ((), ())), preferred_element_type=F32) * SCALE
    if bias is not None:
        s = s - bias
    if valid is not None:
        s = jnp.where(valid, s, NEG)
    m = jnp.max(s, axis=-1, keepdims=True)
    if sink is not None:
        m = jnp.maximum(m, sink)
    e = jnp.exp(s - m)
    z = jnp.sum(e, axis=-1, keepdims=True)
    if sink is None:
        return e * (1.0 / z), None
    es = jnp.exp(sink - m)
    inv = 1.0 / (z + es)
    return e * inv, es * inv


LANES = 128


def _seg_mean(v):
    r = lax.broadcasted_iota(jnp.int32, (LANES, LANES), 0) // HEAD_DIM
    c = lax.broadcasted_iota(jnp.int32, (LANES, LANES), 1) // HEAD_DIM
    seg = jnp.where(r == c, 1.0 / HEAD_DIM, 0.0).astype(BF16)
    hi = v.astype(BF16)
    lo = (v - hi.astype(F32)).astype(BF16)
    parts = []
    for g in range(v.shape[1] // LANES):
        sl = slice(g * LANES, (g + 1) * LANES)
        parts.append(jnp.dot(hi[:, sl], seg, preferred_element_type=F32) + jnp.dot(lo[:, sl], seg, preferred_element_type=F32))
    return parts[0] if len(parts) == 1 else jnp.concatenate(parts, axis=1)


def _cols(rows, width, col):
    return pl.BlockSpec((rows, width), lambda i: (i, col))


def _head_gain(g, heads):
    return jnp.tile(g, heads).reshape(1, heads * HEAD_DIM)


def _fold_heads(dg, heads):
    return dg.reshape(heads, HEAD_DIM).sum(axis=0)


def _seg_rms_fwd(x, gain, *, width, col, name):
    R = x.shape[0]
    tr = _row_tile(R, width)

    def body(x_ref, g_ref, o_ref):
        xv = x_ref[...]
        r = lax.rsqrt(_seg_mean(xv * xv) + EPS)
        o_ref[...] = ((xv * r) * g_ref[...]).astype(o_ref.dtype)

    return pl.pallas_call(
        body, name=name, grid=(R // tr,), in_specs=[_cols(tr, width, col), pl.BlockSpec((1, width), lambda i: (0, 0))],
        out_specs=_cols(tr, width, 0), out_shape=jax.ShapeDtypeStruct((R, width), BF16), compiler_params=_params("parallel"),
    )(x, gain)


def _seg_rms_bwd(x, gain, dys, *, width, col, name, out_buf=None, out_col=0):
    R = x.shape[0]
    tr = _row_tile(R, width)
    n_dy = len(dys)

    def body(*refs):
        x_ref, g_ref = refs[0], refs[1]
        dy_refs = refs[2:2 + n_dy]
        dx_ref, dg_ref = refs[-2], refs[-1]
        i = pl.program_id(0)
        xv = x_ref[...]
        dy = dy_refs[0][...]
        for extra in dy_refs[1:]:
            dy = dy + extra[...]
        r = lax.rsqrt(_seg_mean(xv * xv) + EPS)
        xh = xv * r
        part = jnp.sum(dy * xh, axis=0, keepdims=True)

        @pl.when(i == 0)
        def _():
            dg_ref[...] = part

        @pl.when(i > 0)
        def _():
            dg_ref[...] += part

        gdy = dy * g_ref[...]
        dx_ref[...] = (r * (gdy - xh * _seg_mean(gdy * xh))).astype(dx_ref.dtype)

    vec = pl.BlockSpec((1, width), lambda i: (0, 0))
    in_specs = [_cols(tr, width, col), vec] + [_cols(tr, width, 0)] * n_dy
    operands = [x, gain, *dys]
    aliases = {}
    dx_shape = jax.ShapeDtypeStruct((R, width), BF16)
    if out_buf is not None:
        in_specs.append(ANY)
        operands.append(out_buf)
        aliases = {len(operands) - 1: 0}
        dx_shape = jax.ShapeDtypeStruct(out_buf.shape, out_buf.dtype)
    return pl.pallas_call(
        body, name=name, grid=(R // tr,), in_specs=in_specs, out_specs=(_cols(tr, width, out_col), vec),
        out_shape=(dx_shape, jax.ShapeDtypeStruct((1, width), F32)), input_output_aliases=aliases,
        compiler_params=_params("arbitrary"),
    )(*operands)


def _sum_into(a, b, out_buf, out_col, *, name):
    R, width = a.shape
    tr = _row_tile(R, width)

    def body(a_ref, b_ref, _, o_ref):
        o_ref[...] = (a_ref[...] + b_ref[...]).astype(o_ref.dtype)

    return pl.pallas_call(
        body, name=name, grid=(R // tr,), in_specs=[_cols(tr, width, 0), _cols(tr, width, 0), ANY],
        out_specs=_cols(tr, width, out_col), out_shape=jax.ShapeDtypeStruct(out_buf.shape, out_buf.dtype),
        input_output_aliases={2: 0}, compiler_params=_params("parallel"),
    )(a, b, out_buf)


def _pool_lane_group():
    return lax.broadcasted_iota(jnp.int32, (1, MAIN_WIDTH), 1) // POOL_GROUP_DIM


def _pool_pick(group, per_window):
    s1, s2, s3, s4 = per_window
    return jnp.where(group == 0, s1, jnp.where(group == 1, s2, jnp.where(group == 2, s3, s4)))


def _pool_delta(u_ref, halo_ref, tile):
    group = _pool_lane_group()
    halo = jnp.where(tile == 0, 0.0, halo_ref[...])
    ext = jnp.concatenate([halo, u_ref[...]], axis=0)
    n = ext.shape[0]
    s1 = ext + pltpu.roll(ext, 1, 0)
    s2 = s1 + pltpu.roll(s1, 2, 0)
    s3 = s2 + pltpu.roll(s2, 4, 0)
    s4 = s3 + pltpu.roll(s3, 8, 0)
    ws = _pool_pick(group, (s1, s2, s3, s4))[POOL_HALO:n]
    t = tile * POOL_TILE + lax.broadcasted_iota(jnp.int32, (POOL_TILE, 1), 0)
    cnt = jnp.minimum(t + 1, _pool_pick(group, (2, 4, 8, 16))).astype(F32)
    return ws / cnt - u_ref[...], cnt


def _pool_in_specs():
    per_tile = POOL_TILE // POOL_HALO
    cur = _cols(POOL_TILE, MAIN_WIDTH, 0)
    prev = pl.BlockSpec((POOL_HALO, MAIN_WIDTH), lambda i: (jnp.maximum(i * per_tile - 1, 0), 0))
    mix = pl.BlockSpec((MAIN_WIDTH, MAIN_WIDTH), lambda i: (0, 0))
    vec = pl.BlockSpec((1, MAIN_WIDTH), lambda i: (0, 0))
    return cur, prev, mix, vec


def _pool_mix_fwd(proj, mix, scale, cat, *, name):
    T = proj.shape[0]
    assert T % POOL_TILE == 0
    cur, prev, mix_spec, vec = _pool_in_specs()

    def body(u_ref, halo_ref, mix_ref, sc_ref, _, o_ref):
        d, _cnt = _pool_delta(u_ref, halo_ref, pl.program_id(0))
        mixed = jnp.dot(d.astype(BF16), mix_ref[...].astype(BF16), preferred_element_type=F32)
        o_ref[...] = (mixed * sc_ref[...]).astype(o_ref.dtype)

    return pl.pallas_call(
        body, name=name, grid=(T // POOL_TILE,), in_specs=[cur, prev, mix_spec, vec, ANY], out_specs=cur,
        out_shape=jax.ShapeDtypeStruct(cat.shape, cat.dtype), input_output_aliases={4: 0}, compiler_params=_params("parallel"),
    )(proj, proj, mix, scale, cat)


def _pool_mix_bwd(proj, mix, scale, dcat, *, name):
    T = proj.shape[0]
    nt = T // POOL_TILE
    per_tile = POOL_TILE // POOL_HALO
    cur, prev, mix_spec, vec = _pool_in_specs()
    nxt = pl.BlockSpec((POOL_HALO, MAIN_WIDTH), lambda i: (jnp.minimum((i + 1) * per_tile, nt * per_tile - 1), 0))

    def body(u_ref, halo_ref, mix_ref, sc_ref, do_ref, donext_ref, du_ref, dmix_ref, dsc_ref):
        tile = pl.program_id(0)
        group = _pool_lane_group()
        d, cnt = _pool_delta(u_ref, halo_ref, tile)
        mixb = mix_ref[...].astype(BF16)
        db = d.astype(BF16)
        mixed = jnp.dot(db, mixb, preferred_element_type=F32)
        dout = do_ref[...]
        dsc = jnp.sum(dout * mixed, axis=0, keepdims=True)
        sc = sc_ref[...]
        dmixed = (dout * sc).astype(BF16)
        dmix = lax.dot_general(db, dmixed, (((0,), (0,)), ((), ())), preferred_element_type=F32)

        @pl.when(tile == 0)
        def _():
            dmix_ref[...] = dmix
            dsc_ref[...] = dsc

        @pl.when(tile > 0)
        def _():
            dmix_ref[...] += dmix
            dsc_ref[...] += dsc

        dnext = jnp.where(tile == nt - 1, 0.0, donext_ref[...])
        dmixed_ext = jnp.concatenate([dmixed, (dnext * sc).astype(BF16)], axis=0)
        dd_ext = lax.dot_general(dmixed_ext, mixb, (((1,), (1,)), ((), ())), preferred_element_type=F32)
        window = _pool_pick(group, (2.0, 4.0, 8.0, 16.0))
        cnt_ext = jnp.concatenate([cnt, jnp.broadcast_to(window, (POOL_HALO, MAIN_WIDTH))], axis=0)
        q = dd_ext / cnt_ext
        n = q.shape[0]
        r1 = q + pltpu.roll(q, n - 1, 0)
        r2 = r1 + pltpu.roll(r1, n - 2, 0)
        r3 = r2 + pltpu.roll(r2, n - 4, 0)
        r4 = r3 + pltpu.roll(r3, n - 8, 0)
        back = _pool_pick(group, (r1, r2, r3, r4))
        du_ref[...] = (back[0:POOL_TILE] - dd_ext[0:POOL_TILE]).astype(du_ref.dtype)

    return pl.pallas_call(
        body, name=name, grid=(nt,), in_specs=[cur, prev, mix_spec, vec, cur, nxt], out_specs=(cur, mix_spec, vec),
        out_shape=(jax.ShapeDtypeStruct((T, D_MODEL), BF16), jax.ShapeDtypeStruct((MAIN_WIDTH, MAIN_WIDTH), F32),
                   jax.ShapeDtypeStruct((1, MAIN_WIDTH), F32)),
        compiler_params=_params("arbitrary"),
    )(proj, proj, mix, scale, dcat, dcat)


def _head(a, h):
    return a[:, h * HEAD_DIM:(h + 1) * HEAD_DIM]


def _swa_mask(blk):
    rows = SWA_GROUP * BLOCK
    qi = lax.broadcasted_iota(jnp.int32, (rows, 2 * BLOCK), 0) % BLOCK
    kj = lax.broadcasted_iota(jnp.int32, (rows, 2 * BLOCK), 1)
    dist = qi + BLOCK - kj
    valid = (dist >= 0) & (dist < BLOCK) & ((blk > 0) | (kj >= BLOCK))
    return dist.astype(F32), valid


def _swa_head_terms(sink_ref, kvh, dist):
    grp = lax.broadcasted_iota(jnp.int32, (SWA_GROUP * BLOCK, 1), 0) // BLOCK
    slopes = [2.0 ** (-8.0 * (kvh * SWA_GROUP + g + 1) / SWA_Q_HEADS) for g in range(SWA_GROUP)]
    sinks = [sink_ref[kvh * SWA_GROUP + g] for g in range(SWA_GROUP)]
    slope = jnp.where(grp == 0, slopes[0], jnp.where(grp == 1, slopes[1], slopes[2]))
    sink = jnp.where(grp == 0, sinks[0], jnp.where(grp == 1, sinks[1], sinks[2]))
    return slope * dist, sink


def _stack_heads(a, kvh):
    return jnp.concatenate([_head(a, kvh * SWA_GROUP + g) for g in range(SWA_GROUP)], axis=0)


def _swa_specs(nb):
    def at(n):
        return jnp.minimum(n, nb - 1)

    q = pl.BlockSpec((BLOCK, MAIN_WIDTH), lambda n: (at(n), 0))
    k_prev = pl.BlockSpec((BLOCK, KV_HALF), lambda n: (jnp.maximum(at(n) - 1, 0), 0))
    k_cur = pl.BlockSpec((BLOCK, KV_HALF), lambda n: (at(n), 0))
    v_prev = pl.BlockSpec((BLOCK, KV_HALF), lambda n: (jnp.maximum(at(n) - 1, 0), 1))
    v_cur = pl.BlockSpec((BLOCK, KV_HALF), lambda n: (at(n), 1))
    return q, k_prev, k_cur, v_prev, v_cur


def _swa_attn_fwd(qn, kn, kv, sinks, cat, *, name):
    T = qn.shape[0]
    nb = T // BLOCK
    q_spec, k_prev, k_cur, v_prev, v_cur = _swa_specs(nb)

    def body(sink_ref, q_ref, kp_ref, kc_ref, vp_ref, vc_ref, _, o_ref):
        dist, valid = _swa_mask(pl.program_id(0))
        kk = jnp.concatenate([kp_ref[...], kc_ref[...]], axis=0)
        vv = jnp.concatenate([vp_ref[...], vc_ref[...]], axis=0).astype(BF16)
        q = q_ref[...]
        outs = []
        for kvh in range(SWA_KV_HEADS):
            bias, sink = _swa_head_terms(sink_ref, kvh, dist)
            p, _ps = _softmax(_stack_heads(q, kvh), _head(kk, kvh), bias, valid, sink)
            o = jnp.dot(p.astype(BF16), _head(vv, kvh), preferred_element_type=F32)
            outs += [o[g * BLOCK:(g + 1) * BLOCK] for g in range(SWA_GROUP)]
        o_ref[...] = jnp.concatenate(outs, axis=1).astype(o_ref.dtype)

    return pl.pallas_call(
        body, name=name, grid=(nb,),
        in_specs=[pl.BlockSpec(memory_space=pltpu.SMEM), q_spec, k_prev, k_cur, v_prev, v_cur, ANY], out_specs=q_spec,
        out_shape=jax.ShapeDtypeStruct(cat.shape, cat.dtype), input_output_aliases={6: 0}, compiler_params=_params("parallel"),
    )(sinks, qn, kn, kn, kv, kv, cat)


def _swa_attn_bwd(qn, kn, kv, sinks, dcat, dqn, *, name):
    T = qn.shape[0]
    nb = T // BLOCK
    q_spec, k_prev, k_cur, v_prev, v_cur = _swa_specs(nb)
    late = pl.BlockSpec((BLOCK, KV_HALF), lambda n: (jnp.maximum(n - 1, 0), 0))
    tn_dims = (((0,), (0,)), ((), ()))

    def body(sink_ref, q_ref, do_ref, kp_ref, kc_ref, vp_ref, vc_ref, _, dq_ref, dk_ref, dv_ref, ds_ref, ck, cv):
        blk = pl.program_id(0)

        @pl.when(blk == 0)
        def _():
            ck[...] = jnp.zeros_like(ck)
            cv[...] = jnp.zeros_like(cv)
            ds_ref[...] = jnp.zeros_like(ds_ref)

        @pl.when(blk < nb)
        def _():
            dist, valid = _swa_mask(blk)
            kk = jnp.concatenate([kp_ref[...], kc_ref[...]], axis=0)
            vv = jnp.concatenate([vp_ref[...], vc_ref[...]], axis=0).astype(BF16)
            q = q_ref[...]
            dout = do_ref[...].astype(BF16)
            lane = lax.broadcasted_iota(jnp.int32, (1, LANES), 1)
            dsinks = jnp.zeros((1, LANES), F32)
            dqs, dks, dvs = [], [], []
            for kvh in range(SWA_KV_HEADS):
                bias, sink = _swa_head_terms(sink_ref, kvh, dist)
                qq, kh, vh, dd = _stack_heads(q, kvh), _head(kk, kvh), _head(vv, kvh), _stack_heads(dout, kvh)
                p, ps = _softmax(qq, kh, bias, valid, sink)
                dp = lax.dot_general(dd, vh, (((1,), (1,)), ((), ())), preferred_element_type=F32)
                dsum = jnp.sum(p * dp, axis=-1, keepdims=True)
                ds = (p * (dp - dsum)).astype(BF16)
                dq = jnp.dot(ds, kh, preferred_element_type=F32) * SCALE
                dqs += [dq[g * BLOCK:(g + 1) * BLOCK] for g in range(SWA_GROUP)]
                dks.append(lax.dot_general(qq, ds, tn_dims, preferred_element_type=F32) * SCALE)
                dvs.append(lax.dot_general(dd, p.astype(BF16), tn_dims, preferred_element_type=F32))
                dsink = -(ps * dsum)
                for g in range(SWA_GROUP):
                    dsinks = dsinks + jnp.where(lane == kvh * SWA_GROUP + g, jnp.sum(dsink[g * BLOCK:(g + 1) * BLOCK]), 0.0)
            dq_ref[...] = jnp.concatenate(dqs, axis=1)
            dk = jnp.concatenate(dks, axis=0).T
            dv = jnp.concatenate(dvs, axis=0).T
            dk_ref[...] = ck[...] + dk[0:BLOCK]
            dv_ref[...] = cv[...] + dv[0:BLOCK]
            ck[...] = dk[BLOCK:2 * BLOCK]
            cv[...] = dv[BLOCK:2 * BLOCK]
            ds_ref[...] += dsinks

        @pl.when(blk == nb)
        def _():
            dk_ref[...] = ck[...]
            dv_ref[...] = cv[...]

    return pl.pallas_call(
        body, name=name, grid=(nb + 1,),
        in_specs=[pl.BlockSpec(memory_space=pltpu.SMEM), q_spec, q_spec, k_prev, k_cur, v_prev, v_cur, ANY],
        out_specs=(q_spec, late, late, pl.BlockSpec((1, LANES), lambda n: (0, 0))),
        out_shape=(jax.ShapeDtypeStruct(dqn.shape, dqn.dtype), jax.ShapeDtypeStruct((T, KV_HALF), F32),
                   jax.ShapeDtypeStruct((T, KV_HALF), F32), jax.ShapeDtypeStruct((1, LANES), F32)),
        scratch_shapes=[pltpu.VMEM((BLOCK, KV_HALF), F32), pltpu.VMEM((BLOCK, KV_HALF), F32)],
        input_output_aliases={7: 0}, compiler_params=_params("arbitrary"),
    )(sinks, qn, dcat, kn, kn, kv, kv, dqn)


def _mem_specs(M, tq, q_col):
    q = _cols(tq, MEM_WIDTH, q_col)
    k = pl.BlockSpec((M, MEM_WIDTH), lambda i: (0, 0))
    v = pl.BlockSpec((M, MEM_WIDTH), lambda i: (0, 1))
    return q, k, v


def _mem_attn_fwd(q, q_col, mkn, mkv, *, name):
    T = q.shape[0]
    M = mkn.shape[0]
    tq = min(T, MEM_Q_TILE)
    q_spec, k_spec, v_spec = _mem_specs(M, tq, q_col)

    def body(q_ref, k_ref, v_ref, o_ref):
        qq, kk, vv = q_ref[...], k_ref[...], v_ref[...].astype(BF16)
        outs = []
        for h in range(MEM_HEADS):
            p, _ps = _softmax(_head(qq, h), _head(kk, h), None, None, None)
            outs.append(jnp.dot(p.astype(BF16), _head(vv, h), preferred_element_type=F32))
        o_ref[...] = jnp.concatenate(outs, axis=1).astype(o_ref.dtype)

    return pl.pallas_call(
        body, name=name, grid=(T // tq,), in_specs=[q_spec, k_spec, v_spec], out_specs=_cols(tq, MEM_WIDTH, MAIN_WIDTH // MEM_WIDTH),
        out_shape=jax.ShapeDtypeStruct((T, D_MODEL), BF16), compiler_params=_params("parallel"),
    )(q, mkn, mkv)


def _mem_attn_bwd(q, q_col, mkn, mkv, dcat, *, dq_width, name):
    T = q.shape[0]
    M = mkn.shape[0]
    tq = min(T, MEM_Q_TILE)
    q_spec, k_spec, v_spec = _mem_specs(M, tq, q_col)
    last = MAIN_WIDTH // MEM_WIDTH
    tn_dims = (((0,), (0,)), ((), ()))

    def body(q_ref, do_ref, k_ref, v_ref, dq_ref, dk_ref, dv_ref):
        i = pl.program_id(0)
        qq, kk, vv, dout = q_ref[...], k_ref[...], v_ref[...].astype(BF16), do_ref[...].astype(BF16)
        dqs, dks, dvs = [], [], []
        for h in range(MEM_HEADS):
            qh, kh, vh, dh = _head(qq, h), _head(kk, h), _head(vv, h), _head(dout, h)
            p, _ps = _softmax(qh, kh, None, None, None)
            dp = lax.dot_general(dh, vh, (((1,), (1,)), ((), ())), preferred_element_type=F32)
            dsum = jnp.sum(p * dp, axis=-1, keepdims=True)
            ds = (p * (dp - dsum)).astype(BF16)
            dqs.append(jnp.dot(ds, kh, preferred_element_type=F32) * SCALE)
            dks.append(lax.dot_general(qh, ds, tn_dims, preferred_element_type=F32) * SCALE)
            dvs.append(lax.dot_general(dh, p.astype(BF16), tn_dims, preferred_element_type=F32))
        dq_ref[...] = jnp.concatenate(dqs, axis=1)
        dk = jnp.concatenate(dks, axis=0).T
        dv = jnp.concatenate(dvs, axis=0).T

        @pl.when(i == 0)
        def _():
            dk_ref[...] = dk
            dv_ref[...] = dv

        @pl.when(i > 0)
        def _():
            dk_ref[...] += dk
            dv_ref[...] += dv

    acc = pl.BlockSpec((M, MEM_WIDTH), lambda i: (0, 0))
    return pl.pallas_call(
        body, name=name, grid=(T // tq,), in_specs=[q_spec, _cols(tq, MEM_WIDTH, last), k_spec, v_spec],
        out_specs=(_cols(tq, MEM_WIDTH, dq_width // MEM_WIDTH - 1), acc, acc),
        out_shape=(jax.ShapeDtypeStruct((T, dq_width), F32), jax.ShapeDtypeStruct((M, MEM_WIDTH), F32),
                   jax.ShapeDtypeStruct((M, MEM_WIDTH), F32)),
        compiler_params=_params("arbitrary"),
    )(q, dcat, mkn, mkv)


def _loss(y, target, *, name):
    T, D = y.shape
    tr = _row_tile(T, D)

    def body(y_ref, t_ref, l_ref, dy_ref, dyb_ref):
        i = pl.program_id(0)
        err = y_ref[...] - t_ref[...]
        dy = err / float(D)
        dy_ref[...] = dy
        dyb_ref[...] = dy.astype(BF16)
        part = jnp.full((8, 128), 0.5 * jnp.sum(jnp.mean(err * err, axis=-1)), F32)

        @pl.when(i == 0)
        def _():
            l_ref[...] = part

        @pl.when(i > 0)
        def _():
            l_ref[...] += part

    row = pl.BlockSpec((tr, D), lambda i: (i, 0))
    return pl.pallas_call(
        body, name=name, grid=(T // tr,), in_specs=[row, row],
        out_specs=(pl.BlockSpec((8, 128), lambda i: (0, 0)), row, row),
        out_shape=(jax.ShapeDtypeStruct((8, 128), F32), jax.ShapeDtypeStruct((T, D), F32), jax.ShapeDtypeStruct((T, D), BF16)),
        compiler_params=_params("arbitrary"),
    )(y, target)


def _position():
    return lax.axis_index("x"), lax.axis_index("y"), lax.axis_index("c")


def _all_gather(arrays, *, name):
    n = len(arrays)

    def body(*refs):
        srcs, outs = refs[:n], refs[n:2 * n]
        token, send_sems, recv_sems, local_sems = refs[2 * n:]
        token[...] = jnp.zeros_like(token)
        x, y, c = _position()
        me, sibling = (x, y, c), (x, y, 1 - c)
        chips = [(1 - x, y), (x, 1 - y), (1 - x, 1 - y)]

        def slot(a, px, py, pc):
            return outs[a].at[4 * px + 2 * py + pc]

        def copy(a, k, block, to, src=None):
            return pltpu.make_async_remote_copy(
                src_ref=slot(a, *block) if src is None else src, dst_ref=slot(a, *block),
                send_sem=send_sems.at[a, k], recv_sem=recv_sems.at[a, k], device_id=to, device_id_type=MESH)

        mine = [pltpu.make_async_copy(srcs[a], slot(a, *me), local_sems.at[a]) for a in range(n)]
        for cp in mine:
            cp.start()
        first, passed = [], []
        for a in range(n):
            first.append(copy(a, 0, me, sibling, src=srcs[a]))
            first += [copy(a, 1 + j, me, (*chip, c), src=srcs[a]) for j, chip in enumerate(chips)]
        for cp in first:
            cp.start()
        for a in range(n):
            for j, chip in enumerate(chips):
                copy(a, 1 + j, (*chip, c), me).wait_recv()
                fwd = copy(a, 4 + j, (*chip, c), sibling)
                fwd.start()
                passed.append(fwd)
        for a in range(n):
            copy(a, 0, sibling, me).wait_recv()
            for j, chip in enumerate(chips):
                copy(a, 4 + j, (*chip, 1 - c), me).wait_recv()
        for cp in first + passed:
            cp.wait_send()
        for cp in mine:
            cp.wait()

    return pl.pallas_call(
        body, name=name, in_specs=[ANY] * n, out_specs=[ANY] * n + [pl.BlockSpec(memory_space=pltpu.VMEM)],
        out_shape=[jax.ShapeDtypeStruct((N_DEV,) + a.shape, a.dtype) for a in arrays] + [jax.ShapeDtypeStruct((8, 128), F32)],
        scratch_shapes=[pltpu.SemaphoreType.DMA((n, 7)), pltpu.SemaphoreType.DMA((n, 7)), pltpu.SemaphoreType.DMA((n,))],
    )(*arrays)


def _sibling_exchange(by_core, whole, *, name):
    n1, n = len(by_core), len(by_core) + len(whole)

    def body(*refs):
        srcs, outs = refs[:n], refs[n:2 * n]
        send_sems, recv_sems = refs[2 * n:]
        x, y, c = _position()
        copies = [
            pltpu.make_async_remote_copy(src_ref=srcs[a].at[:, 1 - c] if a < n1 else srcs[a], dst_ref=outs[a],
                                         send_sem=send_sems.at[a], recv_sem=recv_sems.at[a], device_id=(x, y, 1 - c),
                                         device_id_type=MESH)
            for a in range(n)]
        for cp in copies:
            cp.start()
        for cp in copies:
            cp.wait()

    out_shape = [jax.ShapeDtypeStruct(a.shape[:1] + a.shape[2:], a.dtype) for a in by_core]
    out_shape += [jax.ShapeDtypeStruct(a.shape, a.dtype) for a in whole]
    outs = pl.pallas_call(
        body, name=name, in_specs=[ANY] * n, out_specs=[ANY] * n, out_shape=out_shape,
        scratch_shapes=[pltpu.SemaphoreType.DMA((n,)), pltpu.SemaphoreType.DMA((n,))],
    )(*by_core, *whole)
    return outs[:n1], outs[n1:]


HBM = pl.BlockSpec(memory_space=pltpu.HBM)
SEM = pl.BlockSpec(memory_space=pltpu.SEMAPHORE)
DATAFLOW = pltpu.SideEffectType.DATAFLOW_SIDE_EFFECTING


def _device(flat):
    return flat // 4, (flat // 2) % 2, flat % 2


def _gather_copies(srcs, lands, send_sems, recv_sems, incoming):
    x, y, c = _position()
    me = 4 * x + 2 * y + c
    pairs = []
    for a in range(len(srcs)):
        for d in range(1, N_DEV):
            to, frm = (me + d) % N_DEV, (me + N_DEV - d) % N_DEV
            k = a * (N_DEV - 1) + d - 1
            sems = dict(send_sem=send_sems.at[k], recv_sem=recv_sems.at[k], device_id_type=MESH)
            out = pltpu.make_async_remote_copy(src_ref=srcs[a], dst_ref=lands[a].at[me], device_id=_device(to), **sems)
            inc = pltpu.make_async_remote_copy(src_ref=srcs[a], dst_ref=lands[a].at[frm], device_id=_device(frm),
                                               **sems) if incoming else None
            pairs.append((out, inc))
    return pairs


def _chip_copies(srcs, lands, send_sems, recv_sems, incoming, n_whole=0):
    x, y, c = _position()
    my_chip = 2 * x + y
    pairs = []
    for a in range(len(srcs)):
        for k, (px, py) in enumerate([(1 - x, y), (x, 1 - y), (1 - x, 1 - y)]):
            sem = a * (N_CHIP - 1) + k
            sems = dict(send_sem=send_sems.at[sem], recv_sem=recv_sems.at[sem], device_id=(px, py, c), device_id_type=MESH)
            src = srcs[a] if a >= len(srcs) - n_whole else srcs[a].at[2 * px + py]
            out = pltpu.make_async_remote_copy(src_ref=src, dst_ref=lands[a].at[my_chip], **sems)
            inc = pltpu.make_async_remote_copy(src_ref=src, dst_ref=lands[a].at[2 * px + py], **sems) if incoming else None
            pairs.append((out, inc))
    return pairs


def _push_start(copies, fan, srcs, lands, *, name):
    n = len(srcs)

    def body(*refs):
        src_refs, land_refs = refs[:n], refs[n:2 * n]
        send_sems, recv_sems = refs[2 * n], refs[2 * n + 1]
        token = refs[-1]
        for out, _ in copies(src_refs, land_refs, send_sems, recv_sems, False):
            out.start()
        token[...] = jnp.zeros_like(token)

    outs = pl.pallas_call(
        body, name=name,
        out_shape=(pltpu.SemaphoreType.DMA((n * fan,)), pltpu.SemaphoreType.DMA((n * fan,)),
                   *(pltpu.HBM(a.shape, a.dtype) for a in srcs), *(pltpu.HBM(a.shape, a.dtype) for a in lands),
                   jax.ShapeDtypeStruct((8, 128), F32)),
        in_specs=[HBM] * (2 * n), out_specs=(SEM, SEM, *([HBM] * (2 * n)), pl.BlockSpec(memory_space=pltpu.VMEM)),
        input_output_aliases={i: 2 + i for i in range(2 * n)},
        compiler_params=pltpu.CompilerParams(has_side_effects=DATAFLOW),
    )(*(pltpu.with_memory_space_constraint(a, pltpu.HBM) for a in (*srcs, *lands)))
    return outs[0], outs[1], list(outs[2:2 + n]), list(outs[2 + n:2 + 2 * n]), outs[-1]


def _push_wait(copies, send_sems, recv_sems, srcs, lands, after, *, name):
    n = len(srcs)

    def body(*refs):
        src_refs, land_refs = refs[:n], refs[n:2 * n]
        for out, inc in copies(src_refs, land_refs, refs[2 * n], refs[2 * n + 1], True):
            out.wait_send()
            inc.wait_recv()
        refs[-1][...] = jnp.zeros_like(refs[-1])

    outs = pl.pallas_call(
        body, name=name,
        out_shape=(*(pltpu.HBM(a.shape, a.dtype) for a in (*srcs, *lands)), jax.ShapeDtypeStruct((8, 128), F32)),
        in_specs=[HBM] * (2 * n) + [SEM, SEM, ANY], out_specs=(*([HBM] * (2 * n)), pl.BlockSpec(memory_space=pltpu.VMEM)),
        input_output_aliases={i: i for i in range(2 * n)},
        compiler_params=pltpu.CompilerParams(has_side_effects=DATAFLOW),
    )(*srcs, *lands, send_sems, recv_sems, after)
    return list(outs[n:2 * n]), outs[-1]


def _with_own_slot(block, index, slots):
    buf = lax.empty((slots,) + block.shape, block.dtype)
    return lax.dynamic_update_slice(buf, block[None], (index,) + (0,) * block.ndim)


def _view2d(shape):
    return math.prod(shape[:-1]), shape[-1]


def _pair_sum(mine, other, core, *, name, out_dtype):
    by_core = mine.ndim == 4
    n, w = other.shape[-2:]
    tr = _row_tile(n, w * 2)
    lead = other.shape[0] if by_core else 1

    def body(core_ref, a_ref, b_ref, o_ref):
        o_ref[...] = (a_ref[...].astype(F32) + b_ref[...].astype(F32)).astype(o_ref.dtype)

    if by_core:
        a_spec = pl.BlockSpec((None, None, tr, w), lambda j, i, core_ref: (j, core_ref[0], i, 0))
        o_spec = pl.BlockSpec((None, tr, w), lambda j, i, core_ref: (j, i, 0))
    else:
        a_spec = o_spec = pl.BlockSpec((tr, w), lambda j, i, core_ref: (i, 0))
    grid_spec = pltpu.PrefetchScalarGridSpec(num_scalar_prefetch=1, grid=(lead, n // tr), in_specs=[a_spec, o_spec],
                                             out_specs=o_spec)
    return pl.pallas_call(body, name=name, grid_spec=grid_spec, out_shape=jax.ShapeDtypeStruct(other.shape, out_dtype),
                          compiler_params=_params("parallel", "parallel"))(core.reshape(1), mine, other)


def _adamw(parts, w, m, v, *, name):
    layers = len(parts)
    n_parts, R, W = parts[0].shape
    tr = _row_tile(R, W * 2)
    per_layer = R // tr

    def update(p_ref, w_ref, m_ref, v_ref, g_out, d_out, m_out, v_out):
        g = p_ref[0].astype(F32)
        for j in range(1, n_parts):
            g = g + p_ref[j].astype(F32)
        m_new = ADAM_B1 * m_ref[...] + (1.0 - ADAM_B1) * g
        v_new = ADAM_B2 * v_ref[...] + (1.0 - ADAM_B2) * (g * g)
        m_hat = m_new / (1.0 - ADAM_B1 ** ADAM_STEP)
        v_hat = v_new / (1.0 - ADAM_B2 ** ADAM_STEP)
        g_out[...] = g
        d_out[...] = -ADAM_LR * (m_hat / (jnp.sqrt(v_hat) + ADAM_EPS) + ADAM_WD * w_ref[...])
        m_out[...] = m_new
        v_out[...] = v_new

    def body(*refs):
        for k in range(layers):
            pl.when(pl.program_id(0) == k)(lambda k=k: update(refs[k], *refs[layers:]))

    def parts_spec(k):
        return pl.BlockSpec((n_parts, tr, W), lambda l, i: (0, jnp.where(l == k, i, 0), 0))

    row = pl.BlockSpec((tr, W), lambda l, i: (l * per_layer + i, 0))
    out = jax.ShapeDtypeStruct((layers * R, W), F32)
    return pl.pallas_call(
        body, name=name, grid=(layers, per_layer), in_specs=[parts_spec(k) for k in range(layers)] + [row, row, row],
        out_specs=(row, row, row, row), out_shape=(out, out, out, out), compiler_params=_params("arbitrary", "arbitrary"),
    )(*parts, w, m, v)


SMALL_ROWS = 608


def _pack_small(p):
    flat = jnp.concatenate([p[n].reshape(-1).astype(F32) for n in SMALL_NAMES])
    return jnp.pad(flat, (0, SMALL_ROWS * PACK_W - flat.shape[0])).reshape(SMALL_ROWS, PACK_W)


def _unpack_small(buf, like):
    out, at = {}, 0
    flat = buf.reshape(-1)
    for n in SMALL_NAMES:
        size = math.prod(like[n].shape)
        out[n] = flat[at:at + size].reshape(like[n].shape)
        at += size
    return out


def _block_diag(pw):
    out = jnp.zeros((MAIN_WIDTH, MAIN_WIDTH), pw.dtype)
    for g in range(POOL_GROUPS):
        out = lax.dynamic_update_slice(out, pw[g], (g * POOL_GROUP_DIM, g * POOL_GROUP_DIM))
    return out


def _diag_blocks(m):
    return jnp.stack([m[g * POOL_GROUP_DIM:(g + 1) * POOL_GROUP_DIM, g * POOL_GROUP_DIM:(g + 1) * POOL_GROUP_DIM]
                      for g in range(POOL_GROUPS)])


def _train_pass(x, mem, target, p, w_kv, fetch, reduce_layer, reduce_wait):
    T = x.shape[0]
    mem_cols = MAIN_WIDTH // MEM_WIDTH
    k_gain = _head_gain(p["k_norm"], SWA_KV_HEADS)
    saved = []
    h = x
    kn = kv = h_kv = hn_kv = None
    for l in range(DEPTH):
        s = {}
        wl, token = fetch(l, h)
        s["w"] = wl
        if l == N_A:
            h_kv = h
            hn_kv, kv = _norm_mm(h, p["kv_norm"], w_kv, b_kind="rows", name="kv_proj")
            kn = _seg_rms_fwd(kv, k_gain, width=KV_HALF, col=0, name="k_norm_fwd")
        s["h"] = h
        s["xn1"], proj = _norm_mm(h, p["norm_mix"][l] + token, wl["w_in"], b_kind="rows", name="in_proj")
        s["proj"] = proj
        s["memn"] = _rms_fwd(mem, p["mem_norm"][l], name="mem_norm_fwd")
        s["mkv"] = _mm(s["memn"], wl["w_mem_kv"], b_kind="rows", name="mem_kv_proj")
        s["mk_gain"] = _head_gain(p["mem_k_norm"][l], MEM_HEADS)
        s["mkn"] = _seg_rms_fwd(s["mkv"], s["mk_gain"], width=MEM_WIDTH, col=0, name="mem_k_norm_fwd")
        if l < N_A:
            s["q_gain"] = _head_gain(p["mem_q_norm"][l], MEM_HEADS)
            s["qn"] = _seg_rms_fwd(proj, s["q_gain"], width=MEM_WIDTH, col=mem_cols, name="mem_q_norm_fwd")
            s["q_col"] = 0
        else:
            j = l - N_A
            s["q_gain"] = jnp.concatenate([_head_gain(p["q_norm"][j], SWA_Q_HEADS), _head_gain(p["mem_q_norm"][l], MEM_HEADS)],
                                          axis=1)
            s["qn"] = _seg_rms_fwd(proj, s["q_gain"], width=D_MODEL, col=0, name="q_norm_fwd")
            s["q_col"] = mem_cols
        cat = _mem_attn_fwd(s["qn"], s["q_col"], s["mkn"], s["mkv"], name="mem_attn_fwd")
        if l < N_A:
            s["mix"] = _block_diag(p["pool_w"][l])
            s["scale"] = p["pool_scale"][l].reshape(1, MAIN_WIDTH)
            s["cat"] = _pool_mix_fwd(proj, s["mix"], s["scale"], cat, name="pool_fwd")
        else:
            s["cat"] = _swa_attn_fwd(s["qn"], kn, kv, p["sinks"][l - N_A], cat, name="swa_fwd")
        s["h1"] = _mm(s["cat"], wl["w_out"], b_kind="rows", res=h, name="out_proj")
        s["xn2"], s["a"] = _norm_mm(s["h1"], p["norm_mlp"][l], wl["w_up"], b_kind="layers", relu2=True, name="mlp_up")
        h = _mm(s["a"], wl["w_down"], b_kind="rows", res=s["h1"], name="mlp_down")
        saved.append(s)

    loss, dh, dh_b = _loss(h, target, name="loss_head")

    g = {n: [None] * DEPTH for n in ("norm_mix", "mem_norm", "mem_q_norm", "mem_k_norm", "norm_mlp")}
    g.update({n: [None] * N_A for n in ("pool_w", "pool_scale", "q_norm", "sinks")})
    g_kv = None
    token = None
    dks, dvs = [], []
    for l in reversed(range(DEPTH)):
        s = saved[l]
        wl = s["w"]
        gb = {}

        def dw(a, dy, n):
            return _mm(a, dy, ta=True, out_kind="layers" if n == "w_up" else "rows", out_buf=lax.empty(wl[n].shape, BF16),
                       name=n + "_grad")

        norm_mlp_gain = p["norm_mlp"][l] if token is None else p["norm_mlp"][l] + token
        gb["w_down"] = dw(s["a"], dh_b, "w_down")
        du = _mm(dh_b, wl["w_down"], tb=True, b_kind="rows", mul2=s["a"], out_dtype=BF16, name="mlp_down_dx")
        gb["w_up"] = dw(s["xn2"], du, "w_up")
        early = reduce_layer(l, gb, early=True)
        if early is not None:
            norm_mlp_gain = norm_mlp_gain + early
        dh1, dh1_b, g["norm_mlp"][l] = _mm_rms_bwd(du, wl["w_up"], s["h1"], norm_mlp_gain, dh, b_kind="layers", also_bf16=True,
                                                   name="mlp_up_dx")
        gb["w_out"] = dw(s["cat"], dh1_b, "w_out")
        dcat = _mm(dh1_b, wl["w_out"], tb=True, b_kind="rows", name="out_proj_dx")
        if l < N_A:
            dq, dmk, dmv = _mem_attn_bwd(s["qn"], s["q_col"], s["mkn"], s["mkv"], dcat, dq_width=MEM_WIDTH, name="mem_attn_bwd")
            dproj, dmix, dscale = _pool_mix_bwd(s["proj"], s["mix"], s["scale"], dcat, name="pool_bwd")
            g["pool_w"][l] = _diag_blocks(dmix)
            g["pool_scale"][l] = dscale.reshape(MAIN_WIDTH)
            dproj, dgain = _seg_rms_bwd(s["proj"], s["q_gain"], [dq], width=MEM_WIDTH, col=mem_cols, out_buf=dproj,
                                        out_col=mem_cols, name="mem_q_norm_bwd")
            g["mem_q_norm"][l] = _fold_heads(dgain, MEM_HEADS)
        else:
            j = l - N_A
            dqn, dmk, dmv = _mem_attn_bwd(s["qn"], s["q_col"], s["mkn"], s["mkv"], dcat, dq_width=D_MODEL, name="mem_attn_bwd")
            dqn, dk_l, dv_l, dsinks = _swa_attn_bwd(s["qn"], kn, kv, p["sinks"][j], dcat, dqn, name="swa_bwd")
            dks.append(dk_l)
            dvs.append(dv_l)
            g["sinks"][j] = dsinks[0, :SWA_Q_HEADS]
            dproj, dgain = _seg_rms_bwd(s["proj"], s["q_gain"], [dqn], width=D_MODEL, col=0, name="q_norm_bwd")
            g["q_norm"][j] = _fold_heads(dgain[:, :MAIN_WIDTH], SWA_Q_HEADS)
            g["mem_q_norm"][l] = _fold_heads(dgain[:, MAIN_WIDTH:], MEM_HEADS)
        dmk_raw, dgain = _seg_rms_bwd(s["mkv"], s["mk_gain"], [dmk], width=MEM_WIDTH, col=0, name="mem_k_norm_bwd")
        g["mem_k_norm"][l] = _fold_heads(dgain, MEM_HEADS)
        dmkv = jnp.concatenate([dmk_raw, dmv.astype(BF16)], axis=1)
        gb["w_mem_kv"] = dw(s["memn"], dmkv, "w_mem_kv")
        dmemn = _mm(dmkv, wl["w_mem_kv"], tb=True, b_kind="rows", name="mem_kv_proj_dx")
        g["mem_norm"][l] = _rms_bwd(mem, p["mem_norm"][l], [dmemn], want_dx=False, name="mem_norm_bwd")
        gb["w_in"] = dw(s["xn1"], dproj, "w_in")
        if l in (0, N_A):
            dh, g["norm_mix"][l] = _mm_rms_bwd(dproj, wl["w_in"], s["h"], p["norm_mix"][l], dh1, b_kind="rows", also_bf16=False,
                                               name="in_proj_dx")
        else:
            dh, dh_b, g["norm_mix"][l] = _mm_rms_bwd(dproj, wl["w_in"], s["h"], p["norm_mix"][l], dh1, b_kind="rows",
                                                     also_bf16=True, name="in_proj_dx")
        if l == N_A:
            dkv, dgain = _seg_rms_bwd(kv, k_gain, dks, width=KV_HALF, col=0, out_buf=lax.empty((T, 2 * KV_HALF), BF16),
                                      name="k_norm_bwd")
            g["k_norm"] = _fold_heads(dgain, SWA_KV_HEADS)
            dkv = _sum_into(dvs[0], dvs[1], dkv, 1, name="dv_sum")
            g_kv = _mm(hn_kv, dkv, ta=True, out_kind="rows", out_buf=lax.empty(w_kv.shape, BF16), name="w_kv_grad")
            dh, dh_b, g["kv_norm"] = _mm_rms_bwd(dkv, w_kv, h_kv, p["kv_norm"], dh, b_kind="rows", also_bf16=True,
                                                 name="kv_proj_dx")
        if l + 1 < DEPTH:
            reduce_wait(l + 1, dh)
        token = reduce_layer(l, gb)
    grads = {n: (jnp.stack(v) if isinstance(v, list) else v) for n, v in g.items()}
    return loss, dh, grads, g_kv


def kernel(x, mem, norm_mix, w_in, pool_w, pool_scale, kv_norm, w_kv, k_norm, q_norm, sinks, mem_norm, w_mem_kv, mem_q_norm, mem_k_norm, w_out, norm_mlp, w_up, w_down, loss_target, m_norm_mix, m_w_in, m_pool_w, m_pool_scale, m_kv_norm, m_w_kv, m_k_norm, m_q_norm, m_sinks, m_mem_norm, m_w_mem_kv, m_mem_q_norm, m_mem_k_norm, m_w_out, m_norm_mlp, m_w_up, m_w_down, v_norm_mix, v_w_in, v_pool_w, v_pool_scale, v_kv_norm, v_w_kv, v_k_norm, v_q_norm, v_sinks, v_mem_norm, v_w_mem_kv, v_mem_q_norm, v_mem_k_norm, v_w_out, v_norm_mlp, v_w_up, v_w_down):
    weights = dict(norm_mix=norm_mix, w_in=w_in, pool_w=pool_w, pool_scale=pool_scale, kv_norm=kv_norm, w_kv=w_kv,
                   k_norm=k_norm, q_norm=q_norm, sinks=sinks, mem_norm=mem_norm, w_mem_kv=w_mem_kv,
                   mem_q_norm=mem_q_norm, mem_k_norm=mem_k_norm, w_out=w_out, norm_mlp=norm_mlp, w_up=w_up, w_down=w_down)
    mom1 = dict(norm_mix=m_norm_mix, w_in=m_w_in, pool_w=m_pool_w, pool_scale=m_pool_scale, kv_norm=m_kv_norm, w_kv=m_w_kv,
                k_norm=m_k_norm, q_norm=m_q_norm, sinks=m_sinks, mem_norm=m_mem_norm, w_mem_kv=m_w_mem_kv,
                mem_q_norm=m_mem_q_norm, mem_k_norm=m_mem_k_norm, w_out=m_w_out, norm_mlp=m_norm_mlp, w_up=m_w_up,
                w_down=m_w_down)
    mom2 = dict(norm_mix=v_norm_mix, w_in=v_w_in, pool_w=v_pool_w, pool_scale=v_pool_scale, kv_norm=v_kv_norm, w_kv=v_w_kv,
                k_norm=v_k_norm, q_norm=v_q_norm, sinks=v_sinks, mem_norm=v_mem_norm, w_mem_kv=v_w_mem_kv,
                mem_q_norm=v_mem_q_norm, mem_k_norm=v_mem_k_norm, w_out=v_w_out, norm_mlp=v_norm_mlp, w_up=v_w_up,
                w_down=v_w_down)
    names = list(weights)
    x_pos, y_pos, core = (lax.axis_index(n).astype(jnp.int32) for n in AXES)
    me, my_chip = 4 * x_pos + 2 * y_pos + core, 2 * x_pos + y_pos
    shard = MAIN_WIDTH // N_DEV

    def layer_shards(l, zero=0.0):
        return [(weights[n][l:l + 1] + zero).astype(BF16) for n in LAYERED]

    def usable(arrays):
        wl = dict(zip(LAYERED, arrays))
        wl["w_up"] = wl["w_up"].transpose(1, 2, 0, 3).reshape(1, D_MODEL, D_FF)
        return wl

    scale_block = jnp.pad(pool_scale, ((0, 8 - N_A), (0, 128 - shard)))
    *first, first_done = _all_gather(layer_shards(0) + [w_kv[None].astype(BF16), scale_block], name="gather_first")
    p = {n: weights[n] for n in SMALL_NAMES}
    p["pool_scale"] = first[-1][:, :N_A, :shard].transpose(1, 0, 2).reshape(N_A, MAIN_WIDTH)
    gathers, reduces, parts = {}, {}, {}

    def fetch(l, after):
        if l == 0:
            got, done = first[:len(LAYERED)], first_done
        else:
            got, done = _push_wait(_gather_copies, *gathers.pop(l), after, name=f"gather_wait_{l}")
        token = 0.0
        if l + 1 < DEPTH:
            srcs = layer_shards(l + 1, done[0, 0])
            lands = [_with_own_slot(a, me, N_DEV) for a in srcs]
            *handles, block = _push_start(_gather_copies, N_DEV - 1, srcs, lands, name=f"gather_start_{l + 1}")
            gathers[l + 1], token = handles, block[0, 0]
        return usable(got), token

    def by_core(gb):
        gb = dict(gb)
        if "w_up" in gb:
            gb["w_up"] = gb["w_up"].reshape(D_MODEL, N_DEV, D_FF // N_DEV).transpose(1, 0, 2)
        order = [n for n in LAYERED if n in gb] + [n for n in gb if n not in LAYERED]
        return {n: gb[n].reshape((N_CHIP, 2) + _view2d(gb[n].shape[1:] if n == "w_up" else gb[n].shape[2:])) for n in order}

    def pair_sums(views, sib, tag):
        return [_pair_sum(a, b, core, name=f"chip_sum_{n}_{tag}", out_dtype=BF16) for (n, a), b in zip(views.items(), sib)]

    def chip_sums(gb, tag, whole=()):
        views = by_core(gb)
        sib, sib_whole = _sibling_exchange(list(views.values()), list(whole), name="reduce_sibling_" + tag)
        return pair_sums(views, sib, tag), sib_whole

    def start_chip_exchange(sums, tag, whole=()):
        lands = [_with_own_slot(lax.dynamic_index_in_dim(a, my_chip, 0, keepdims=False), my_chip, N_CHIP) for a in sums]
        lands += [_with_own_slot(a, my_chip, N_CHIP) for a in whole]
        copies = functools.partial(_chip_copies, n_whole=len(whole))
        *handles, block = _push_start(copies, N_CHIP - 1, [*sums, *whole], lands, name="reduce_start_" + tag)
        return (copies, *handles), block[0, 0]

    mlp = ("w_up", "w_down")

    def reduce_layer(l, gb, early=False):
        if early and l > 0:
            return None
        if l == 0 and not early:
            reduces["rest"] = {n: a for n, a in gb.items() if n not in mlp}
            return None
        tag = "0_mlp" if early else str(l)
        sums, _ = chip_sums({n: gb[n] for n in mlp} if early else gb, tag)
        reduces[l], token = start_chip_exchange(sums, tag)
        return token

    def reduce_wait(l, after):
        copies, *handles = reduces.pop(l)
        return _push_wait(copies, *handles, after, name=f"reduce_wait_{l}")[0]

    def layer_wait(l, after):
        parts[l] = reduce_wait(l, after)

    loss, grad_x, grads, g_kv = _train_pass(x[0], mem[0], loss_target[0], p, first[len(LAYERED)], fetch, reduce_layer, layer_wait)

    last = dict(reduces.pop("rest"))
    last["w_kv"] = g_kv
    last["pool_scale"] = grads["pool_scale"].reshape(N_A, N_DEV, shard).transpose(1, 0, 2).astype(BF16)[:, None]
    small = _pack_small(grads)
    sums, (sib_small,) = chip_sums(last, "0", whole=[small])
    chip_small = _pair_sum(small, sib_small, core, name="chip_sum_small", out_dtype=F32)
    reduces["rest"], _ = start_chip_exchange(sums, "0_rest", whole=[chip_small])

    def adamw(n, n_parts):
        res = _adamw(n_parts, *(d[n].reshape(_view2d(d[n].shape)) for d in (weights, mom1, mom2)), name="adamw_" + n)
        return [r.reshape(weights[n].shape) for r in res]

    p_up, p_down = reduce_wait(0, chip_small)
    parts[0] = [None, None, None, p_up, p_down]
    new = {n: adamw(n, [parts[l][LAYERED.index(n)] for l in range(DEPTH)]) for n in mlp}
    p_in, p_mem_kv, p_out, parts_kv, parts_scale, parts_small = reduce_wait("rest", new["w_down"][0])
    parts[0][:3] = [p_in, p_mem_kv, p_out]
    new.update({n: adamw(n, [parts[l][k] for l in range(DEPTH)]) for k, n in enumerate(LAYERED) if n not in mlp})
    new["w_kv"] = adamw("w_kv", [parts_kv])
    new["pool_scale"] = adamw("pool_scale", [parts_scale])
    res = _adamw([parts_small], _pack_small(weights), _pack_small(mom1), _pack_small(mom2), name="adamw_replicated")
    for n, vals in zip(SMALL_NAMES, zip(*(_unpack_small(r, weights).values() for r in res))):
        new[n] = list(vals)
    outs = [new[n][k] for k in range(4) for n in names]
    total = lax.psum(loss[0, 0], AXES)
    return (total, grad_x[None], *outs)
```

```python
import functools
import math

import jax
import jax.numpy as jnp
from jax import lax
from jax.experimental import pallas as pl
from jax.experimental.pallas import tpu as pltpu

F32 = jnp.float32
BF16 = jnp.bfloat16
MESH = pl.DeviceIdType.MESH
AXES = ("x", "y", "c")

D_MODEL = 1024
DEPTH = 4
N_A = 2
HEAD_DIM = 64
MEM_HEADS = 4
MEM_WIDTH = MEM_HEADS * HEAD_DIM
MAIN_WIDTH = D_MODEL - MEM_WIDTH
POOL_GROUPS = 4
POOL_GROUP_DIM = MAIN_WIDTH // POOL_GROUPS
POOL_HALO = 16
SWA_Q_HEADS = MAIN_WIDTH // HEAD_DIM
SWA_KV_HEADS = 4
SWA_GROUP = SWA_Q_HEADS // SWA_KV_HEADS
KV_HALF = SWA_KV_HEADS * HEAD_DIM
BLOCK = 128
D_FF = 4 * D_MODEL
EPS = 1e-6
SCALE = HEAD_DIM ** -0.5
NEG = float(jnp.finfo(jnp.float32).min)
N_DEV = 8
N_CHIP = 4

ADAM_LR = 0.001
ADAM_B1 = 0.9
ADAM_B2 = 0.999
ADAM_EPS = 1e-08
ADAM_WD = 0.01
ADAM_STEP = 10

PACK_W = 512
VMEM_LIMIT = 52 * 1024 * 1024
MM_TILE = 1024
LAYERED = ("w_in", "w_mem_kv", "w_out", "w_up", "w_down")
SMALL_NAMES = ("norm_mix", "pool_w", "kv_norm", "k_norm", "q_norm", "sinks", "mem_norm", "mem_q_norm", "mem_k_norm",
               "norm_mlp")


ANY = pl.BlockSpec(memory_space=pl.ANY)


def _params(*sem):
    return pltpu.CompilerParams(dimension_semantics=sem, vmem_limit_bytes=VMEM_LIMIT)


def _mm(a, b, *, name, ta=False, tb=False, b_kind=None, layer=0, res=None, relu2=False, mul2=None, out_dtype=F32,
        out_kind=None, out_buf=None):
    if ta:
        K, M = a.shape
    else:
        M, K = a.shape
    if b_kind is None:
        rows_b, cols_b = b.shape
    elif b_kind == "rows":
        rows_b, cols_b = b.shape[0] * b.shape[2], b.shape[3]
    else:
        rows_b, cols_b = b.shape[1:]
    N, K2 = (rows_b, cols_b) if tb else (cols_b, rows_b)
    assert K == K2, (a.shape, b.shape)
    tm = min(M, MM_TILE if K <= MM_TILE else MM_TILE // 2)
    tn = min(N, MM_TILE)
    assert M % tm == 0 and N % tn == 0
    row_tile, col_tile = (tn, K) if tb else (K, tn)
    a_spec = pl.BlockSpec((K, tm), lambda j, i: (0, i)) if ta else pl.BlockSpec((tm, K), lambda j, i: (i, 0))

    def rc(j):
        return (j, 0) if tb else (0, j)

    if b_kind is None:
        b_spec = pl.BlockSpec((row_tile, col_tile), lambda j, i: rc(j))
    elif b_kind == "rows":
        per = row_tile // b.shape[2]
        b_spec = pl.BlockSpec((per, None, b.shape[2], col_tile), lambda j, i: (rc(j)[0], layer, 0, rc(j)[1]))
    else:
        b_spec = pl.BlockSpec((None, row_tile, col_tile), lambda j, i: (layer, *rc(j)))
    o_spec = pl.BlockSpec((tm, tn), lambda j, i: (i, j))
    dn = (((0 if ta else 1,), (1 if tb else 0,)), ((), ()))
    extra = [e for e in (res, mul2) if e is not None]
    n_in = 2 + len(extra) + (1 if out_buf is not None else 0)

    def body(*refs):
        a_ref, b_ref = refs[0], refs[1]
        extra_refs = refs[2:2 + len(extra)]
        out = refs[n_in]
        bv = b_ref[...].astype(BF16).reshape(row_tile, col_tile)
        v = lax.dot_general(a_ref[...].astype(BF16), bv, dn, preferred_element_type=F32)
        if res is not None:
            v = extra_refs[0][...] + v
        elif mul2 is not None:
            v = v * (2.0 * jnp.sqrt(extra_refs[0][...].astype(F32)))
        if relu2:
            r = jnp.maximum(v, 0.0)
            v = r * r
        out[...] = v.astype(out.dtype).reshape(out.shape)

    in_specs = [a_spec, b_spec] + [o_spec] * len(extra)
    operands = [a, b, *extra]
    aliases = {}
    if out_kind is None:
        out_shape = jax.ShapeDtypeStruct((M, N), BF16 if relu2 else out_dtype)
        out_specs = o_spec
    else:
        if out_kind == "rows":
            s = out_buf.shape[2]
            out_specs = pl.BlockSpec((tm // s, None, s, tn), lambda j, i: (i, layer, 0, j))
        else:
            out_specs = pl.BlockSpec((None, tm, tn), lambda j, i: (layer, i, j))
        out_shape = jax.ShapeDtypeStruct(out_buf.shape, out_buf.dtype)
        in_specs.append(ANY)
        operands.append(out_buf)
        aliases = {len(operands) - 1: 0}
    return pl.pallas_call(
        body, name=name, grid=(N // tn, M // tm), in_specs=in_specs, out_specs=out_specs, out_shape=out_shape,
        input_output_aliases=aliases, compiler_params=_params("parallel", "parallel"),
    )(*operands)


def _weight_block(b, b_kind, transposed, tn):
    if b_kind == "rows":
        s = b.shape[2]
        rows, cols = b.shape[0] * s, b.shape[3]
        if transposed:
            return (lambda at: pl.BlockSpec((b.shape[0], None, s, cols), lambda *g: (0, 0, 0, 0))), rows, cols
        return (lambda at: pl.BlockSpec((b.shape[0], None, s, tn), lambda *g: (0, 0, 0, at(*g)))), rows, cols
    rows, cols = b.shape[1:]
    if transposed:
        return (lambda at: pl.BlockSpec((None, rows, cols), lambda *g: (0, 0, 0))), rows, cols
    return (lambda at: pl.BlockSpec((None, rows, tn), lambda *g: (0, 0, at(*g)))), rows, cols


def _norm_mm(x, gain, b, *, b_kind, name, relu2=False):
    M, K = x.shape
    tm = min(M, MM_TILE)
    spec_of, rows, N = _weight_block(b, b_kind, False, min(MM_TILE, b.shape[-1]))
    tn = min(N, MM_TILE)
    assert rows == K and M % tm == 0 and N % tn == 0

    def body(x_ref, g_ref, b_ref, xn_ref, o_ref):
        @pl.when(pl.program_id(1) == 0)
        def _():
            xv = x_ref[...]
            r = lax.rsqrt(jnp.mean(xv * xv, axis=-1, keepdims=True) + EPS)
            xn_ref[...] = ((xv * r) * g_ref[...]).astype(xn_ref.dtype)

        v = jnp.dot(xn_ref[...], b_ref[...].astype(BF16).reshape(K, tn), preferred_element_type=F32)
        if relu2:
            r2 = jnp.maximum(v, 0.0)
            v = r2 * r2
        o_ref[...] = v.astype(o_ref.dtype)

    rows_spec = pl.BlockSpec((tm, K), lambda i, j: (i, 0))
    return pl.pallas_call(
        body, name=name, grid=(M // tm, N // tn),
        in_specs=[rows_spec, pl.BlockSpec((1, K), lambda i, j: (0, 0)), spec_of(lambda i, j: j)],
        out_specs=(rows_spec, pl.BlockSpec((tm, tn), lambda i, j: (i, j))),
        out_shape=(jax.ShapeDtypeStruct((M, K), BF16), jax.ShapeDtypeStruct((M, N), BF16 if relu2 else F32)),
        compiler_params=_params("parallel", "arbitrary"),
    )(x, gain.reshape(1, K), b)


def _mm_rms_bwd(a, b, x, gain, res, *, b_kind, name, also_bf16):
    M, K = a.shape
    spec_of, N, cols = _weight_block(b, b_kind, True, None)
    assert cols == K and x.shape == (M, N)
    tm = min(M, MM_TILE if K <= MM_TILE else MM_TILE // 2)
    assert M % tm == 0

    def body(a_ref, b_ref, x_ref, g_ref, res_ref, *outs):
        i = pl.program_id(0)
        dy = lax.dot_general(a_ref[...].astype(BF16), b_ref[...].astype(BF16).reshape(N, K), (((1,), (1,)), ((), ())),
                             preferred_element_type=F32)
        xv = x_ref[...]
        r = lax.rsqrt(jnp.mean(xv * xv, axis=-1, keepdims=True) + EPS)
        xh = xv * r
        part = jnp.sum(dy * xh, axis=0, keepdims=True)
        dg_ref = outs[-1]

        @pl.when(i == 0)
        def _():
            dg_ref[...] = part

        @pl.when(i > 0)
        def _():
            dg_ref[...] += part

        gdy = dy * g_ref[...]
        dx = res_ref[...] + r * (gdy - xh * jnp.mean(gdy * xh, axis=-1, keepdims=True))
        outs[0][...] = dx
        if also_bf16:
            outs[1][...] = dx.astype(BF16)

    row = pl.BlockSpec((tm, N), lambda i: (i, 0))
    vec = pl.BlockSpec((1, N), lambda i: (0, 0))
    out_specs = [row] + ([row] if also_bf16 else []) + [vec]
    out_shape = [jax.ShapeDtypeStruct((M, N), F32)] + ([jax.ShapeDtypeStruct((M, N), BF16)] if also_bf16 else [])
    outs = pl.pallas_call(
        body, name=name, grid=(M // tm,),
        in_specs=[pl.BlockSpec((tm, K), lambda i: (i, 0)), spec_of(None), row, vec, row], out_specs=out_specs,
        out_shape=out_shape + [jax.ShapeDtypeStruct((1, N), F32)], compiler_params=_params("arbitrary"),
    )(a, b, x, gain.reshape(1, N), res)
    return (*outs[:-1], outs[-1].reshape(N))


def _row_tile(rows, d):
    t = min(rows, (512 * 1024) // d)
    while rows % t or (t != rows and t % 16):
        t -= 1
    return t


def _rms_fwd(x, g, *, name, out_dtype=BF16):
    R, D = x.shape
    tr = _row_tile(R, D)

    def body(x_ref, g_ref, o_ref):
        xv = x_ref[...].astype(F32)
        r = lax.rsqrt(jnp.mean(xv * xv, axis=-1, keepdims=True) + EPS)
        o_ref[...] = ((xv * r) * g_ref[...]).astype(o_ref.dtype)

    return pl.pallas_call(
        body, name=name, grid=(R // tr,),
        in_specs=[pl.BlockSpec((tr, D), lambda i: (i, 0)), pl.BlockSpec((1, D), lambda i: (0, 0))],
        out_specs=pl.BlockSpec((tr, D), lambda i: (i, 0)), out_shape=jax.ShapeDtypeStruct((R, D), out_dtype),
        compiler_params=_params("parallel"),
    )(x, g.reshape(1, D))


def _rms_bwd(x, g, dys, *, name, res=None, want_dx=True, also_bf16=False):
    R, D = x.shape
    tr = _row_tile(R, D)
    n_dy = len(dys)
    has_res = res is not None

    def body(*refs):
        x_ref, g_ref = refs[0], refs[1]
        dy_refs = refs[2:2 + n_dy]
        res_ref = refs[2 + n_dy] if has_res else None
        outs = refs[2 + n_dy + (1 if has_res else 0):]
        dg_ref = outs[-1]
        i = pl.program_id(0)
        xv = x_ref[...].astype(F32)
        dy = dy_refs[0][...].astype(F32)
        for extra in dy_refs[1:]:
            dy = dy + extra[...].astype(F32)
        r = lax.rsqrt(jnp.mean(xv * xv, axis=-1, keepdims=True) + EPS)
        xh = xv * r
        part = jnp.sum(dy * xh, axis=0, keepdims=True)

        @pl.when(i == 0)
        def _():
            dg_ref[...] = part

        @pl.when(i > 0)
        def _():
            dg_ref[...] += part

        if want_dx:
            gdy = dy * g_ref[...]
            dx = r * (gdy - xh * jnp.mean(gdy * xh, axis=-1, keepdims=True))
            if has_res:
                dx = res_ref[...] + dx
            outs[0][...] = dx
            if also_bf16:
                outs[1][...] = dx.astype(BF16)

    row = pl.BlockSpec((tr, D), lambda i: (i, 0))
    vec = pl.BlockSpec((1, D), lambda i: (0, 0))
    out_shape = [jax.ShapeDtypeStruct((1, D), F32)]
    out_specs = [vec]
    if also_bf16:
        out_shape = [jax.ShapeDtypeStruct((R, D), BF16)] + out_shape
        out_specs = [row] + out_specs
    if want_dx:
        out_shape = [jax.ShapeDtypeStruct((R, D), F32)] + out_shape
        out_specs = [row] + out_specs
    outs = pl.pallas_call(
        body, name=name, grid=(R // tr,),
        in_specs=[row, vec] + [row] * (n_dy + (1 if has_res else 0)), out_specs=out_specs, out_shape=out_shape,
        compiler_params=_params("arbitrary"),
    )(x, g.reshape(1, D), *dys, *([res] if has_res else []))
    return (*outs[:-1], outs[-1].reshape(D)) if want_dx else outs[0].reshape(D)


POOL_TILE = 512
MEM_Q_TILE = 2048


def _softmax(q, k, bias, valid, sink):
    s = lax.dot_general(q, k, (((1,), (1,)), ((), ())), preferred_element_type=F32) * SCALE
    if bias is not None:
        s = s - bias
    if valid is not None:
        s = jnp.where(valid, s, NEG)
    m = jnp.max(s, axis=-1, keepdims=True)
    if sink is not None:
        m = jnp.maximum(m, sink)
    e = jnp.exp(s - m)
    z = jnp.sum(e, axis=-1, keepdims=True)
    if sink is None:
        return e * (1.0 / z), None
    es = jnp.exp(sink - m)
    inv = 1.0 / (z + es)
    return e * inv, es * inv


LANES = 128


def _seg_mean(v):
    r = lax.broadcasted_iota(jnp.int32, (LANES, LANES), 0) // HEAD_DIM
    c = lax.broadcasted_iota(jnp.int32, (LANES, LANES), 1) // HEAD_DIM
    seg = jnp.where(r == c, 1.0 / HEAD_DIM, 0.0).astype(BF16)
    hi = v.astype(BF16)
    lo = (v - hi.astype(F32)).astype(BF16)
    parts = []
    for g in range(v.shape[1] // LANES):
        sl = slice(g * LANES, (g + 1) * LANES)
        parts.append(jnp.dot(hi[:, sl], seg, preferred_element_type=F32) + jnp.dot(lo[:, sl], seg, preferred_element_type=F32))
    return parts[0] if len(parts) == 1 else jnp.concatenate(parts, axis=1)


def _cols(rows, width, col):
    return pl.BlockSpec((rows, width), lambda i: (i, col))


def _head_gain(g, heads):
    return jnp.tile(g, heads).reshape(1, heads * HEAD_DIM)


def _fold_heads(dg, heads):
    return dg.reshape(heads, HEAD_DIM).sum(axis=0)


def _seg_rms_fwd(x, gain, *, width, col, name):
    R = x.shape[0]
    tr = _row_tile(R, width)

    def body(x_ref, g_ref, o_ref):
        xv = x_ref[...]
        r = lax.rsqrt(_seg_mean(xv * xv) + EPS)
        o_ref[...] = ((xv * r) * g_ref[...]).astype(o_ref.dtype)

    return pl.pallas_call(
        body, name=name, grid=(R // tr,), in_specs=[_cols(tr, width, col), pl.BlockSpec((1, width), lambda i: (0, 0))],
        out_specs=_cols(tr, width, 0), out_shape=jax.ShapeDtypeStruct((R, width), BF16), compiler_params=_params("parallel"),
    )(x, gain)


def _seg_rms_bwd(x, gain, dys, *, width, col, name, out_buf=None, out_col=0):
    R = x.shape[0]
    tr = _row_tile(R, width)
    n_dy = len(dys)

    def body(*refs):
        x_ref, g_ref = refs[0], refs[1]
        dy_refs = refs[2:2 + n_dy]
        dx_ref, dg_ref = refs[-2], refs[-1]
        i = pl.program_id(0)
        xv = x_ref[...]
        dy = dy_refs[0][...]
        for extra in dy_refs[1:]:
            dy = dy + extra[...]
        r = lax.rsqrt(_seg_mean(xv * xv) + EPS)
        xh = xv * r
        part = jnp.sum(dy * xh, axis=0, keepdims=True)

        @pl.when(i == 0)
        def _():
            dg_ref[...] = part

        @pl.when(i > 0)
        def _():
            dg_ref[...] += part

        gdy = dy * g_ref[...]
        dx_ref[...] = (r * (gdy - xh * _seg_mean(gdy * xh))).astype(dx_ref.dtype)

    vec = pl.BlockSpec((1, width), lambda i: (0, 0))
    in_specs = [_cols(tr, width, col), vec] + [_cols(tr, width, 0)] * n_dy
    operands = [x, gain, *dys]
    aliases = {}
    dx_shape = jax.ShapeDtypeStruct((R, width), BF16)
    if out_buf is not None:
        in_specs.append(ANY)
        operands.append(out_buf)
        aliases = {len(operands) - 1: 0}
        dx_shape = jax.ShapeDtypeStruct(out_buf.shape, out_buf.dtype)
    return pl.pallas_call(
        body, name=name, grid=(R // tr,), in_specs=in_specs, out_specs=(_cols(tr, width, out_col), vec),
        out_shape=(dx_shape, jax.ShapeDtypeStruct((1, width), F32)), input_output_aliases=aliases,
        compiler_params=_params("arbitrary"),
    )(*operands)


def _sum_into(a, b, out_buf, out_col, *, name):
    R, width = a.shape
    tr = _row_tile(R, width)

    def body(a_ref, b_ref, _, o_ref):
        o_ref[...] = (a_ref[...] + b_ref[...]).astype(o_ref.dtype)

    return pl.pallas_call(
        body, name=name, grid=(R // tr,), in_specs=[_cols(tr, width, 0), _cols(tr, width, 0), ANY],
        out_specs=_cols(tr, width, out_col), out_shape=jax.ShapeDtypeStruct(out_buf.shape, out_buf.dtype),
        input_output_aliases={2: 0}, compiler_params=_params("parallel"),
    )(a, b, out_buf)


def _pool_lane_group():
    return lax.broadcasted_iota(jnp.int32, (1, MAIN_WIDTH), 1) // POOL_GROUP_DIM


def _pool_pick(group, per_window):
    s1, s2, s3, s4 = per_window
    return jnp.where(group == 0, s1, jnp.where(group == 1, s2, jnp.where(group == 2, s3, s4)))


def _pool_delta(u_ref, halo_ref, tile):
    group = _pool_lane_group()
    halo = jnp.where(tile == 0, 0.0, halo_ref[...])
    ext = jnp.concatenate([halo, u_ref[...]], axis=0)
    n = ext.shape[0]
    s1 = ext + pltpu.roll(ext, 1, 0)
    s2 = s1 + pltpu.roll(s1, 2, 0)
    s3 = s2 + pltpu.roll(s2, 4, 0)
    s4 = s3 + pltpu.roll(s3, 8, 0)
    ws = _pool_pick(group, (s1, s2, s3, s4))[POOL_HALO:n]
    t = tile * POOL_TILE + lax.broadcasted_iota(jnp.int32, (POOL_TILE, 1), 0)
    cnt = jnp.minimum(t + 1, _pool_pick(group, (2, 4, 8, 16))).astype(F32)
    return ws / cnt - u_ref[...], cnt


def _pool_in_specs():
    per_tile = POOL_TILE // POOL_HALO
    cur = _cols(POOL_TILE, MAIN_WIDTH, 0)
    prev = pl.BlockSpec((POOL_HALO, MAIN_WIDTH), lambda i: (jnp.maximum(i * per_tile - 1, 0), 0))
    mix = pl.BlockSpec((MAIN_WIDTH, MAIN_WIDTH), lambda i: (0, 0))
    vec = pl.BlockSpec((1, MAIN_WIDTH), lambda i: (0, 0))
    return cur, prev, mix, vec


def _pool_mix_fwd(proj, mix, scale, cat, *, name):
    T = proj.shape[0]
    assert T % POOL_TILE == 0
    cur, prev, mix_spec, vec = _pool_in_specs()

    def body(u_ref, halo_ref, mix_ref, sc_ref, _, o_ref):
        d, _cnt = _pool_delta(u_ref, halo_ref, pl.program_id(0))
        mixed = jnp.dot(d.astype(BF16), mix_ref[...].astype(BF16), preferred_element_type=F32)
        o_ref[...] = (mixed * sc_ref[...]).astype(o_ref.dtype)

    return pl.pallas_call(
        body, name=name, grid=(T // POOL_TILE,), in_specs=[cur, prev, mix_spec, vec, ANY], out_specs=cur,
        out_shape=jax.ShapeDtypeStruct(cat.shape, cat.dtype), input_output_aliases={4: 0}, compiler_params=_params("parallel"),
    )(proj, proj, mix, scale, cat)


def _pool_mix_bwd(proj, mix, scale, dcat, *, name):
    T = proj.shape[0]
    nt = T // POOL_TILE
    per_tile = POOL_TILE // POOL_HALO
    cur, prev, mix_spec, vec = _pool_in_specs()
    nxt = pl.BlockSpec((POOL_HALO, MAIN_WIDTH), lambda i: (jnp.minimum((i + 1) * per_tile, nt * per_tile - 1), 0))

    def body(u_ref, halo_ref, mix_ref, sc_ref, do_ref, donext_ref, du_ref, dmix_ref, dsc_ref):
        tile = pl.program_id(0)
        group = _pool_lane_group()
        d, cnt = _pool_delta(u_ref, halo_ref, tile)
        mixb = mix_ref[...].astype(BF16)
        db = d.astype(BF16)
        mixed = jnp.dot(db, mixb, preferred_element_type=F32)
        dout = do_ref[...]
        dsc = jnp.sum(dout * mixed, axis=0, keepdims=True)
        sc = sc_ref[...]
        dmixed = (dout * sc).astype(BF16)
        dmix = lax.dot_general(db, dmixed, (((0,), (0,)), ((), ())), preferred_element_type=F32)

        @pl.when(tile == 0)
        def _():
            dmix_ref[...] = dmix
            dsc_ref[...] = dsc

        @pl.when(tile > 0)
        def _():
            dmix_ref[...] += dmix
            dsc_ref[...] += dsc

        dnext = jnp.where(tile == nt - 1, 0.0, donext_ref[...])
        dmixed_ext = jnp.concatenate([dmixed, (dnext * sc).astype(BF16)], axis=0)
        dd_ext = lax.dot_general(dmixed_ext, mixb, (((1,), (1,)), ((), ())), preferred_element_type=F32)
        window = _pool_pick(group, (2.0, 4.0, 8.0, 16.0))
        cnt_ext = jnp.concatenate([cnt, jnp.broadcast_to(window, (POOL_HALO, MAIN_WIDTH))], axis=0)
        q = dd_ext / cnt_ext
        n = q.shape[0]
        r1 = q + pltpu.roll(q, n - 1, 0)
        r2 = r1 + pltpu.roll(r1, n - 2, 0)
        r3 = r2 + pltpu.roll(r2, n - 4, 0)
        r4 = r3 + pltpu.roll(r3, n - 8, 0)
        back = _pool_pick(group, (r1, r2, r3, r4))
        du_ref[...] = (back[0:POOL_TILE] - dd_ext[0:POOL_TILE]).astype(du_ref.dtype)

    return pl.pallas_call(
        body, name=name, grid=(nt,), in_specs=[cur, prev, mix_spec, vec, cur, nxt], out_specs=(cur, mix_spec, vec),
        out_shape=(jax.ShapeDtypeStruct((T, D_MODEL), BF16), jax.ShapeDtypeStruct((MAIN_WIDTH, MAIN_WIDTH), F32),
                   jax.ShapeDtypeStruct((1, MAIN_WIDTH), F32)),
        compiler_params=_params("arbitrary"),
    )(proj, proj, mix, scale, dcat, dcat)


def _head(a, h):
    return a[:, h * HEAD_DIM:(h + 1) * HEAD_DIM]


def _swa_mask(blk):
    rows = SWA_GROUP * BLOCK
    qi = lax.broadcasted_iota(jnp.int32, (rows, 2 * BLOCK), 0) % BLOCK
    kj = lax.broadcasted_iota(jnp.int32, (rows, 2 * BLOCK), 1)
    dist = qi + BLOCK - kj
    valid = (dist >= 0) & (dist < BLOCK) & ((blk > 0) | (kj >= BLOCK))
    return dist.astype(F32), valid


def _swa_head_terms(sink_ref, kvh, dist):
    grp = lax.broadcasted_iota(jnp.int32, (SWA_GROUP * BLOCK, 1), 0) // BLOCK
    slopes = [2.0 ** (-8.0 * (kvh * SWA_GROUP + g + 1) / SWA_Q_HEADS) for g in range(SWA_GROUP)]
    sinks = [sink_ref[kvh * SWA_GROUP + g] for g in range(SWA_GROUP)]
    slope = jnp.where(grp == 0, slopes[0], jnp.where(grp == 1, slopes[1], slopes[2]))
    sink = jnp.where(grp == 0, sinks[0], jnp.where(grp == 1, sinks[1], sinks[2]))
    return slope * dist, sink


def _stack_heads(a, kvh):
    return jnp.concatenate([_head(a, kvh * SWA_GROUP + g) for g in range(SWA_GROUP)], axis=0)


def _swa_specs(nb):
    def at(n):
        return jnp.minimum(n, nb - 1)

    q = pl.BlockSpec((BLOCK, MAIN_WIDTH), lambda n: (at(n), 0))
    k_prev = pl.BlockSpec((BLOCK, KV_HALF), lambda n: (jnp.maximum(at(n) - 1, 0), 0))
    k_cur = pl.BlockSpec((BLOCK, KV_HALF), lambda n: (at(n), 0))
    v_prev = pl.BlockSpec((BLOCK, KV_HALF), lambda n: (jnp.maximum(at(n) - 1, 0), 1))
    v_cur = pl.BlockSpec((BLOCK, KV_HALF), lambda n: (at(n), 1))
    return q, k_prev, k_cur, v_prev, v_cur


def _swa_attn_fwd(qn, kn, kv, sinks, cat, *, name):
    T = qn.shape[0]
    nb = T // BLOCK
    q_spec, k_prev, k_cur, v_prev, v_cur = _swa_specs(nb)

    def body(sink_ref, q_ref, kp_ref, kc_ref, vp_ref, vc_ref, _, o_ref):
        dist, valid = _swa_mask(pl.program_id(0))
        kk = jnp.concatenate([kp_ref[...], kc_ref[...]], axis=0)
        vv = jnp.concatenate([vp_ref[...], vc_ref[...]], axis=0).astype(BF16)
        q = q_ref[...]
        outs = []
        for kvh in range(SWA_KV_HEADS):
            bias, sink = _swa_head_terms(sink_ref, kvh, dist)
            p, _ps = _softmax(_stack_heads(q, kvh), _head(kk, kvh), bias, valid, sink)
            o = jnp.dot(p.astype(BF16), _head(vv, kvh), preferred_element_type=F32)
            outs += [o[g * BLOCK:(g + 1) * BLOCK] for g in range(SWA_GROUP)]
        o_ref[...] = jnp.concatenate(outs, axis=1).astype(o_ref.dtype)

    return pl.pallas_call(
        body, name=name, grid=(nb,),
        in_specs=[pl.BlockSpec(memory_space=pltpu.SMEM), q_spec, k_prev, k_cur, v_prev, v_cur, ANY], out_specs=q_spec,
        out_shape=jax.ShapeDtypeStruct(cat.shape, cat.dtype), input_output_aliases={6: 0}, compiler_params=_params("parallel"),
    )(sinks, qn, kn, kn, kv, kv, cat)


def _swa_attn_bwd(qn, kn, kv, sinks, dcat, dqn, *, name):
    T = qn.shape[0]
    nb = T // BLOCK
    q_spec, k_prev, k_cur, v_prev, v_cur = _swa_specs(nb)
    late = pl.BlockSpec((BLOCK, KV_HALF), lambda n: (jnp.maximum(n - 1, 0), 0))
    tn_dims = (((0,), (0,)), ((), ()))

    def body(sink_ref, q_ref, do_ref, kp_ref, kc_ref, vp_ref, vc_ref, _, dq_ref, dk_ref, dv_ref, ds_ref, ck, cv):
        blk = pl.program_id(0)

        @pl.when(blk == 0)
        def _():
            ck[...] = jnp.zeros_like(ck)
            cv[...] = jnp.zeros_like(cv)
            ds_ref[...] = jnp.zeros_like(ds_ref)

        @pl.when(blk < nb)
        def _():
            dist, valid = _swa_mask(blk)
            kk = jnp.concatenate([kp_ref[...], kc_ref[...]], axis=0)
            vv = jnp.concatenate([vp_ref[...], vc_ref[...]], axis=0).astype(BF16)
            q = q_ref[...]
            dout = do_ref[...].astype(BF16)
            lane = lax.broadcasted_iota(jnp.int32, (1, LANES), 1)
            dsinks = jnp.zeros((1, LANES), F32)
            dqs, dks, dvs = [], [], []
            for kvh in range(SWA_KV_HEADS):
                bias, sink = _swa_head_terms(sink_ref, kvh, dist)
                qq, kh, vh, dd = _stack_heads(q, kvh), _head(kk, kvh), _head(vv, kvh), _stack_heads(dout, kvh)
                p, ps = _softmax(qq, kh, bias, valid, sink)
                dp = lax.dot_general(dd, vh, (((1,), (1,)), ((), ())), preferred_element_type=F32)
                dsum = jnp.sum(p * dp, axis=-1, keepdims=True)
                ds = (p * (dp - dsum)).astype(BF16)
                dq = jnp.dot(ds, kh, preferred_element_type=F32) * SCALE
                dqs += [dq[g * BLOCK:(g + 1) * BLOCK] for g in range(SWA_GROUP)]
                dks.append(lax.dot_general(qq, ds, tn_dims, preferred_element_type=F32) * SCALE)
                dvs.append(lax.dot_general(dd, p.astype(BF16), tn_dims, preferred_element_type=F32))
                dsink = -(ps * dsum)
                for g in range(SWA_GROUP):
                    dsinks = dsinks + jnp.where(lane == kvh * SWA_GROUP + g, jnp.sum(dsink[g * BLOCK:(g + 1) * BLOCK]), 0.0)
            dq_ref[...] = jnp.concatenate(dqs, axis=1)
            dk = jnp.concatenate(dks, axis=0).T
            dv = jnp.concatenate(dvs, axis=0).T
            dk_ref[...] = ck[...] + dk[0:BLOCK]
            dv_ref[...] = cv[...] + dv[0:BLOCK]
            ck[...] = dk[BLOCK:2 * BLOCK]
            cv[...] = dv[BLOCK:2 * BLOCK]
            ds_ref[...] += dsinks

        @pl.when(blk == nb)
        def _():
            dk_ref[...] = ck[...]
            dv_ref[...] = cv[...]

    return pl.pallas_call(
        body, name=name, grid=(nb + 1,),
        in_specs=[pl.BlockSpec(memory_space=pltpu.SMEM), q_spec, q_spec, k_prev, k_cur, v_prev, v_cur, ANY],
        out_specs=(q_spec, late, late, pl.BlockSpec((1, LANES), lambda n: (0, 0))),
        out_shape=(jax.ShapeDtypeStruct(dqn.shape, dqn.dtype), jax.ShapeDtypeStruct((T, KV_HALF), F32),
                   jax.ShapeDtypeStruct((T, KV_HALF), F32), jax.ShapeDtypeStruct((1, LANES), F32)),
        scratch_shapes=[pltpu.VMEM((BLOCK, KV_HALF), F32), pltpu.VMEM((BLOCK, KV_HALF), F32)],
        input_output_aliases={7: 0}, compiler_params=_params("arbitrary"),
    )(sinks, qn, dcat, kn, kn, kv, kv, dqn)


def _mem_specs(M, tq, q_col):
    q = _cols(tq, MEM_WIDTH, q_col)
    k = pl.BlockSpec((M, MEM_WIDTH), lambda i: (0, 0))
    v = pl.BlockSpec((M, MEM_WIDTH), lambda i: (0, 1))
    return q, k, v


def _mem_attn_fwd(q, q_col, mkn, mkv, *, name):
    T = q.shape[0]
    M = mkn.shape[0]
    tq = min(T, MEM_Q_TILE)
    q_spec, k_spec, v_spec = _mem_specs(M, tq, q_col)

    def body(q_ref, k_ref, v_ref, o_ref):
        qq, kk, vv = q_ref[...], k_ref[...], v_ref[...].astype(BF16)
        outs = []
        for h in range(MEM_HEADS):
            p, _ps = _softmax(_head(qq, h), _head(kk, h), None, None, None)
            outs.append(jnp.dot(p.astype(BF16), _head(vv, h), preferred_element_type=F32))
        o_ref[...] = jnp.concatenate(outs, axis=1).astype(o_ref.dtype)

    return pl.pallas_call(
        body, name=name, grid=(T // tq,), in_specs=[q_spec, k_spec, v_spec], out_specs=_cols(tq, MEM_WIDTH, MAIN_WIDTH // MEM_WIDTH),
        out_shape=jax.ShapeDtypeStruct((T, D_MODEL), BF16), compiler_params=_params("parallel"),
    )(q, mkn, mkv)


def _mem_attn_bwd(q, q_col, mkn, mkv, dcat, *, dq_width, name):
    T = q.shape[0]
    M = mkn.shape[0]
    tq = min(T, MEM_Q_TILE)
    q_spec, k_spec, v_spec = _mem_specs(M, tq, q_col)
    last = MAIN_WIDTH // MEM_WIDTH
    tn_dims = (((0,), (0,)), ((), ()))

    def body(q_ref, do_ref, k_ref, v_ref, dq_ref, dk_ref, dv_ref):
        i = pl.program_id(0)
        qq, kk, vv, dout = q_ref[...], k_ref[...], v_ref[...].astype(BF16), do_ref[...].astype(BF16)
        dqs, dks, dvs = [], [], []
        for h in range(MEM_HEADS):
            qh, kh, vh, dh = _head(qq, h), _head(kk, h), _head(vv, h), _head(dout, h)
            p, _ps = _softmax(qh, kh, None, None, None)
            dp = lax.dot_general(dh, vh, (((1,), (1,)), ((), ())), preferred_element_type=F32)
            dsum = jnp.sum(p * dp, axis=-1, keepdims=True)
            ds = (p * (dp - dsum)).astype(BF16)
            dqs.append(jnp.dot(ds, kh, preferred_element_type=F32) * SCALE)
            dks.append(lax.dot_general(qh, ds, tn_dims, preferred_element_type=F32) * SCALE)
            dvs.append(lax.dot_general(dh, p.astype(BF16), tn_dims, preferred_element_type=F32))
        dq_ref[...] = jnp.concatenate(dqs, axis=1)
        dk = jnp.concatenate(dks, axis=0).T
        dv = jnp.concatenate(dvs, axis=0).T

        @pl.when(i == 0)
        def _():
            dk_ref[...] = dk
            dv_ref[...] = dv

        @pl.when(i > 0)
        def _():
            dk_ref[...] += dk
            dv_ref[...] += dv

    acc = pl.BlockSpec((M, MEM_WIDTH), lambda i: (0, 0))
    return pl.pallas_call(
        body, name=name, grid=(T // tq,), in_specs=[q_spec, _cols(tq, MEM_WIDTH, last), k_spec, v_spec],
        out_specs=(_cols(tq, MEM_WIDTH, dq_width // MEM_WIDTH - 1), acc, acc),
        out_shape=(jax.ShapeDtypeStruct((T, dq_width), F32), jax.ShapeDtypeStruct((M, MEM_WIDTH), F32),
                   jax.ShapeDtypeStruct((M, MEM_WIDTH), F32)),
        compiler_params=_params("arbitrary"),
    )(q, dcat, mkn, mkv)


def _loss(y, target, *, name):
    T, D = y.shape
    tr = _row_tile(T, D)

    def body(y_ref, t_ref, l_ref, dy_ref, dyb_ref):
        i = pl.program_id(0)
        err = y_ref[...] - t_ref[...]
        dy = err / float(D)
        dy_ref[...] = dy
        dyb_ref[...] = dy.astype(BF16)
        part = jnp.full((8, 128), 0.5 * jnp.sum(jnp.mean(err * err, axis=-1)), F32)

        @pl.when(i == 0)
        def _():
            l_ref[...] = part

        @pl.when(i > 0)
        def _():
            l_ref[...] += part

    row = pl.BlockSpec((tr, D), lambda i: (i, 0))
    return pl.pallas_call(
        body, name=name, grid=(T // tr,), in_specs=[row, row],
        out_specs=(pl.BlockSpec((8, 128), lambda i: (0, 0)), row, row),
        out_shape=(jax.ShapeDtypeStruct((8, 128), F32), jax.ShapeDtypeStruct((T, D), F32), jax.ShapeDtypeStruct((T, D), BF16)),
        compiler_params=_params("arbitrary"),
    )(y, target)


def _position():
    return lax.axis_index("x"), lax.axis_index("y"), lax.axis_index("c")


def _all_gather(arrays, *, name):
    n = len(arrays)

    def body(*refs):
        srcs, outs = refs[:n], refs[n:2 * n]
        token, send_sems, recv_sems, local_sems = refs[2 * n:]
        token[...] = jnp.zeros_like(token)
        x, y, c = _position()
        me, sibling = (x, y, c), (x, y, 1 - c)
        chips = [(1 - x, y), (x, 1 - y), (1 - x, 1 - y)]

        def slot(a, px, py, pc):
            return outs[a].at[4 * px + 2 * py + pc]

        def copy(a, k, block, to, src=None):
            return pltpu.make_async_remote_copy(
                src_ref=slot(a, *block) if src is None else src, dst_ref=slot(a, *block),
                send_sem=send_sems.at[a, k], recv_sem=recv_sems.at[a, k], device_id=to, device_id_type=MESH)

        mine = [pltpu.make_async_copy(srcs[a], slot(a, *me), local_sems.at[a]) for a in range(n)]
        for cp in mine:
            cp.start()
        first, passed = [], []
        for a in range(n):
            first.append(copy(a, 0, me, sibling, src=srcs[a]))
            first += [copy(a, 1 + j, me, (*chip, c), src=srcs[a]) for j, chip in enumerate(chips)]
        for cp in first:
            cp.start()
        for a in range(n):
            for j, chip in enumerate(chips):
                copy(a, 1 + j, (*chip, c), me).wait_recv()
                fwd = copy(a, 4 + j, (*chip, c), sibling)
                fwd.start()
                passed.append(fwd)
        for a in range(n):
            copy(a, 0, sibling, me).wait_recv()
            for j, chip in enumerate(chips):
                copy(a, 4 + j, (*chip, 1 - c), me).wait_recv()
        for cp in first + passed:
            cp.wait_send()
        for cp in mine:
            cp.wait()

    return pl.pallas_call(
        body, name=name, in_specs=[ANY] * n, out_specs=[ANY] * n + [pl.BlockSpec(memory_space=pltpu.VMEM)],
        out_shape=[jax.ShapeDtypeStruct((N_DEV,) + a.shape, a.dtype) for a in arrays] + [jax.ShapeDtypeStruct((8, 128), F32)],
        scratch_shapes=[pltpu.SemaphoreType.DMA((n, 7)), pltpu.SemaphoreType.DMA((n, 7)), pltpu.SemaphoreType.DMA((n,))],
    )(*arrays)


def _sibling_exchange(by_core, whole, *, name):
    n1, n = len(by_core), len(by_core) + len(whole)

    def body(*refs):
        srcs, outs = refs[:n], refs[n:2 * n]
        send_sems, recv_sems = refs[2 * n:]
        x, y, c = _position()
        copies = [
            pltpu.make_async_remote_copy(src_ref=srcs[a].at[:, 1 - c] if a < n1 else srcs[a], dst_ref=outs[a],
                                         send_sem=send_sems.at[a], recv_sem=recv_sems.at[a], device_id=(x, y, 1 - c),
                                         device_id_type=MESH)
            for a in range(n)]
        for cp in copies:
            cp.start()
        for cp in copies:
            cp.wait()

    out_shape = [jax.ShapeDtypeStruct(a.shape[:1] + a.shape[2:], a.dtype) for a in by_core]
    out_shape += [jax.ShapeDtypeStruct(a.shape, a.dtype) for a in whole]
    outs = pl.pallas_call(
        body, name=name, in_specs=[ANY] * n, out_specs=[ANY] * n, out_shape=out_shape,
        scratch_shapes=[pltpu.SemaphoreType.DMA((n,)), pltpu.SemaphoreType.DMA((n,))],
    )(*by_core, *whole)
    return outs[:n1], outs[n1:]


HBM = pl.BlockSpec(memory_space=pltpu.HBM)
SEM = pl.BlockSpec(memory_space=pltpu.SEMAPHORE)
DATAFLOW = pltpu.SideEffectType.DATAFLOW_SIDE_EFFECTING


def _device(flat):
    return flat // 4, (flat // 2) % 2, flat % 2


def _gather_copies(srcs, lands, send_sems, recv_sems, incoming):
    x, y, c = _position()
    me = 4 * x + 2 * y + c
    pairs = []
    for a in range(len(srcs)):
        for d in range(1, N_DEV):
            to, frm = (me + d) % N_DEV, (me + N_DEV - d) % N_DEV
            k = a * (N_DEV - 1) + d - 1
            sems = dict(send_sem=send_sems.at[k], recv_sem=recv_sems.at[k], device_id_type=MESH)
            out = pltpu.make_async_remote_copy(src_ref=srcs[a], dst_ref=lands[a].at[me], device_id=_device(to), **sems)
            inc = pltpu.make_async_remote_copy(src_ref=srcs[a], dst_ref=lands[a].at[frm], device_id=_device(frm),
                                               **sems) if incoming else None
            pairs.append((out, inc))
    return pairs


def _chip_copies(srcs, lands, send_sems, recv_sems, incoming, n_whole=0):
    x, y, c = _position()
    my_chip = 2 * x + y
    pairs = []
    for a in range(len(srcs)):
        for k, (px, py) in enumerate([(1 - x, y), (x, 1 - y), (1 - x, 1 - y)]):
            sem = a * (N_CHIP - 1) + k
            sems = dict(send_sem=send_sems.at[sem], recv_sem=recv_sems.at[sem], device_id=(px, py, c), device_id_type=MESH)
            src = srcs[a] if a >= len(srcs) - n_whole else srcs[a].at[2 * px + py]
            out = pltpu.make_async_remote_copy(src_ref=src, dst_ref=lands[a].at[my_chip], **sems)
            inc = pltpu.make_async_remote_copy(src_ref=src, dst_ref=lands[a].at[2 * px + py], **sems) if incoming else None
            pairs.append((out, inc))
    return pairs


def _push_start(copies, fan, srcs, lands, *, name):
    n = len(srcs)

    def body(*refs):
        src_refs, land_refs = refs[:n], refs[n:2 * n]
        send_sems, recv_sems = refs[2 * n], refs[2 * n + 1]
        token = refs[-1]
        for out, _ in copies(src_refs, land_refs, send_sems, recv_sems, False):
            out.start()
        token[...] = jnp.zeros_like(token)

    outs = pl.pallas_call(
        body, name=name,
        out_shape=(pltpu.SemaphoreType.DMA((n * fan,)), pltpu.SemaphoreType.DMA((n * fan,)),
                   *(pltpu.HBM(a.shape, a.dtype) for a in srcs), *(pltpu.HBM(a.shape, a.dtype) for a in lands),
                   jax.ShapeDtypeStruct((8, 128), F32)),
        in_specs=[HBM] * (2 * n), out_specs=(SEM, SEM, *([HBM] * (2 * n)), pl.BlockSpec(memory_space=pltpu.VMEM)),
        input_output_aliases={i: 2 + i for i in range(2 * n)},
        compiler_params=pltpu.CompilerParams(has_side_effects=DATAFLOW),
    )(*(pltpu.with_memory_space_constraint(a, pltpu.HBM) for a in (*srcs, *lands)))
    return outs[0], outs[1], list(outs[2:2 + n]), list(outs[2 + n:2 + 2 * n]), outs[-1]


def _push_wait(copies, send_sems, recv_sems, srcs, lands, after, *, name):
    n = len(srcs)

    def body(*refs):
        src_refs, land_refs = refs[:n], refs[n:2 * n]
        for out, inc in copies(src_refs, land_refs, refs[2 * n], refs[2 * n + 1], True):
            out.wait_send()
            inc.wait_recv()
        refs[-1][...] = jnp.zeros_like(refs[-1])

    outs = pl.pallas_call(
        body, name=name,
        out_shape=(*(pltpu.HBM(a.shape, a.dtype) for a in (*srcs, *lands)), jax.ShapeDtypeStruct((8, 128), F32)),
        in_specs=[HBM] * (2 * n) + [SEM, SEM, ANY], out_specs=(*([HBM] * (2 * n)), pl.BlockSpec(memory_space=pltpu.VMEM)),
        input_output_aliases={i: i for i in range(2 * n)},
        compiler_params=pltpu.CompilerParams(has_side_effects=DATAFLOW),
    )(*srcs, *lands, send_sems, recv_sems, after)
    return list(outs[n:2 * n]), outs[-1]


def _with_own_slot(block, index, slots):
    buf = lax.empty((slots,) + block.shape, block.dtype)
    return lax.dynamic_update_slice(buf, block[None], (index,) + (0,) * block.ndim)


def _view2d(shape):
    return math.prod(shape[:-1]), shape[-1]


def _pair_sum(mine, other, core, *, name, out_dtype):
    by_core = mine.ndim == 4
    n, w = other.shape[-2:]
    tr = _row_tile(n, w * 2)
    lead = other.shape[0] if by_core else 1

    def body(core_ref, a_ref, b_ref, o_ref):
        o_ref[...] = (a_ref[...].astype(F32) + b_ref[...].astype(F32)).astype(o_ref.dtype)

    if by_core:
        a_spec = pl.BlockSpec((None, None, tr, w), lambda j, i, core_ref: (j, core_ref[0], i, 0))
        o_spec = pl.BlockSpec((None, tr, w), lambda j, i, core_ref: (j, i, 0))
    else:
        a_spec = o_spec = pl.BlockSpec((tr, w), lambda j, i, core_ref: (i, 0))
    grid_spec = pltpu.PrefetchScalarGridSpec(num_scalar_prefetch=1, grid=(lead, n // tr), in_specs=[a_spec, o_spec],
                                             out_specs=o_spec)
    return pl.pallas_call(body, name=name, grid_spec=grid_spec, out_shape=jax.ShapeDtypeStruct(other.shape, out_dtype),
                          compiler_params=_params("parallel", "parallel"))(core.reshape(1), mine, other)


def _adamw(parts, w, m, v, *, name):
    layers = len(parts)
    n_parts, R, W = parts[0].shape
    tr = _row_tile(R, W * 2)
    per_layer = R // tr

    def update(p_ref, w_ref, m_ref, v_ref, g_out, d_out, m_out, v_out):
        g = p_ref[0].astype(F32)
        for j in range(1, n_parts):
            g = g + p_ref[j].astype(F32)
        m_new = ADAM_B1 * m_ref[...] + (1.0 - ADAM_B1) * g
        v_new = ADAM_B2 * v_ref[...] + (1.0 - ADAM_B2) * (g * g)
        m_hat = m_new / (1.0 - ADAM_B1 ** ADAM_STEP)
        v_hat = v_new / (1.0 - ADAM_B2 ** ADAM_STEP)
        g_out[...] = g
        d_out[...] = -ADAM_LR * (m_hat / (jnp.sqrt(v_hat) + ADAM_EPS) + ADAM_WD * w_ref[...])
        m_out[...] = m_new
        v_out[...] = v_new

    def body(*refs):
        for k in range(layers):
            pl.when(pl.program_id(0) == k)(lambda k=k: update(refs[k], *refs[layers:]))

    def parts_spec(k):
        return pl.BlockSpec((n_parts, tr, W), lambda l, i: (0, jnp.where(l == k, i, 0), 0))

    row = pl.BlockSpec((tr, W), lambda l, i: (l * per_layer + i, 0))
    out = jax.ShapeDtypeStruct((layers * R, W), F32)
    return pl.pallas_call(
        body, name=name, grid=(layers, per_layer), in_specs=[parts_spec(k) for k in range(layers)] + [row, row, row],
        out_specs=(row, row, row, row), out_shape=(out, out, out, out), compiler_params=_params("arbitrary", "arbitrary"),
    )(*parts, w, m, v)


SMALL_ROWS = 608


def _pack_small(p):
    flat = jnp.concatenate([p[n].reshape(-1).astype(F32) for n in SMALL_NAMES])
    return jnp.pad(flat, (0, SMALL_ROWS * PACK_W - flat.shape[0])).reshape(SMALL_ROWS, PACK_W)


def _unpack_small(buf, like):
    out, at = {}, 0
    flat = buf.reshape(-1)
    for n in SMALL_NAMES:
        size = math.prod(like[n].shape)
        out[n] = flat[at:at + size].reshape(like[n].shape)
        at += size
    return out


def _block_diag(pw):
    out = jnp.zeros((MAIN_WIDTH, MAIN_WIDTH), pw.dtype)
    for g in range(POOL_GROUPS):
        out = lax.dynamic_update_slice(out, pw[g], (g * POOL_GROUP_DIM, g * POOL_GROUP_DIM))
    return out


def _diag_blocks(m):
    return jnp.stack([m[g * POOL_GROUP_DIM:(g + 1) * POOL_GROUP_DIM, g * POOL_GROUP_DIM:(g + 1) * POOL_GROUP_DIM]
                      for g in range(POOL_GROUPS)])


def _train_pass(x, mem, target, p, w_kv, fetch, reduce_layer, reduce_wait):
    T = x.shape[0]
    mem_cols = MAIN_WIDTH // MEM_WIDTH
    k_gain = _head_gain(p["k_norm"], SWA_KV_HEADS)
    saved = []
    h = x
    kn = kv = h_kv = hn_kv = None
    for l in range(DEPTH):
        s = {}
        wl, token = fetch(l, h)
        s["w"] = wl
        if l == N_A:
            h_kv = h
            hn_kv, kv = _norm_mm(h, p["kv_norm"], w_kv, b_kind="rows", name="kv_proj")
            kn = _seg_rms_fwd(kv, k_gain, width=KV_HALF, col=0, name="k_norm_fwd")
        s["h"] = h
        s["xn1"], proj = _norm_mm(h, p["norm_mix"][l] + token, wl["w_in"], b_kind="rows", name="in_proj")
        s["proj"] = proj
        s["memn"] = _rms_fwd(mem, p["mem_norm"][l], name="mem_norm_fwd")
        s["mkv"] = _mm(s["memn"], wl["w_mem_kv"], b_kind="rows", name="mem_kv_proj")
        s["mk_gain"] = _head_gain(p["mem_k_norm"][l], MEM_HEADS)
        s["mkn"] = _seg_rms_fwd(s["mkv"], s["mk_gain"], width=MEM_WIDTH, col=0, name="mem_k_norm_fwd")
        if l < N_A:
            s["q_gain"] = _head_gain(p["mem_q_norm"][l], MEM_HEADS)
            s["qn"] = _seg_rms_fwd(proj, s["q_gain"], width=MEM_WIDTH, col=mem_cols, name="mem_q_norm_fwd")
            s["q_col"] = 0
        else:
            j = l - N_A
            s["q_gain"] = jnp.concatenate([_head_gain(p["q_norm"][j], SWA_Q_HEADS), _head_gain(p["mem_q_norm"][l], MEM_HEADS)],
                                          axis=1)
            s["qn"] = _seg_rms_fwd(proj, s["q_gain"], width=D_MODEL, col=0, name="q_norm_fwd")
            s["q_col"] = mem_cols
        cat = _mem_attn_fwd(s["qn"], s["q_col"], s["mkn"], s["mkv"], name="mem_attn_fwd")
        if l < N_A:
            s["mix"] = _block_diag(p["pool_w"][l])
            s["scale"] = p["pool_scale"][l].reshape(1, MAIN_WIDTH)
            s["cat"] = _pool_mix_fwd(proj, s["mix"], s["scale"], cat, name="pool_fwd")
        else:
            s["cat"] = _swa_attn_fwd(s["qn"], kn, kv, p["sinks"][l - N_A], cat, name="swa_fwd")
        s["h1"] = _mm(s["cat"], wl["w_out"], b_kind="rows", res=h, name="out_proj")
        s["xn2"], s["a"] = _norm_mm(s["h1"], p["norm_mlp"][l], wl["w_up"], b_kind="layers", relu2=True, name="mlp_up")
        h = _mm(s["a"], wl["w_down"], b_kind="rows", res=s["h1"], name="mlp_down")
        saved.append(s)

    loss, dh, dh_b = _loss(h, target, name="loss_head")

    g = {n: [None] * DEPTH for n in ("norm_mix", "mem_norm", "mem_q_norm", "mem_k_norm", "norm_mlp")}
    g.update({n: [None] * N_A for n in ("pool_w", "pool_scale", "q_norm", "sinks")})
    g_kv = None
    token = None
    dks, dvs = [], []
    for l in reversed(range(DEPTH)):
        s = saved[l]
        wl = s["w"]
        gb = {}

        def dw(a, dy, n):
            return _mm(a, dy, ta=True, out_kind="layers" if n == "w_up" else "rows", out_buf=lax.empty(wl[n].shape, BF16),
                       name=n + "_grad")

        norm_mlp_gain = p["norm_mlp"][l] if token is None else p["norm_mlp"][l] + token
        gb["w_down"] = dw(s["a"], dh_b, "w_down")
        du = _mm(dh_b, wl["w_down"], tb=True, b_kind="rows", mul2=s["a"], out_dtype=BF16, name="mlp_down_dx")
        gb["w_up"] = dw(s["xn2"], du, "w_up")
        early = reduce_layer(l, gb, early=True)
        if early is not None:
            norm_mlp_gain = norm_mlp_gain + early
        dh1, dh1_b, g["norm_mlp"][l] = _mm_rms_bwd(du, wl["w_up"], s["h1"], norm_mlp_gain, dh, b_kind="layers", also_bf16=True,
                                                   name="mlp_up_dx")
        gb["w_out"] = dw(s["cat"], dh1_b, "w_out")
        dcat = _mm(dh1_b, wl["w_out"], tb=True, b_kind="rows", name="out_proj_dx")
        if l < N_A:
            dq, dmk, dmv = _mem_attn_bwd(s["qn"], s["q_col"], s["mkn"], s["mkv"], dcat, dq_width=MEM_WIDTH, name="mem_attn_bwd")
            dproj, dmix, dscale = _pool_mix_bwd(s["proj"], s["mix"], s["scale"], dcat, name="pool_bwd")
            g["pool_w"][l] = _diag_blocks(dmix)
            g["pool_scale"][l] = dscale.reshape(MAIN_WIDTH)
            dproj, dgain = _seg_rms_bwd(s["proj"], s["q_gain"], [dq], width=MEM_WIDTH, col=mem_cols, out_buf=dproj,
                                        out_col=mem_cols, name="mem_q_norm_bwd")
            g["mem_q_norm"][l] = _fold_heads(dgain, MEM_HEADS)
        else:
            j = l - N_A
            dqn, dmk, dmv = _mem_attn_bwd(s["qn"], s["q_col"], s["mkn"], s["mkv"], dcat, dq_width=D_MODEL, name="mem_attn_bwd")
            dqn, dk_l, dv_l, dsinks = _swa_attn_bwd(s["qn"], kn, kv, p["sinks"][j], dcat, dqn, name="swa_bwd")
            dks.append(dk_l)
            dvs.append(dv_l)
            g["sinks"][j] = dsinks[0, :SWA_Q_HEADS]
            dproj, dgain = _seg_rms_bwd(s["proj"], s["q_gain"], [dqn], width=D_MODEL, col=0, name="q_norm_bwd")
            g["q_norm"][j] = _fold_heads(dgain[:, :MAIN_WIDTH], SWA_Q_HEADS)
            g["mem_q_norm"][l] = _fold_heads(dgain[:, MAIN_WIDTH:], MEM_HEADS)
        dmk_raw, dgain = _seg_rms_bwd(s["mkv"], s["mk_gain"], [dmk], width=MEM_WIDTH, col=0, name="mem_k_norm_bwd")
        g["mem_k_norm"][l] = _fold_heads(dgain, MEM_HEADS)
        dmkv = jnp.concatenate([dmk_raw, dmv.astype(BF16)], axis=1)
        gb["w_mem_kv"] = dw(s["memn"], dmkv, "w_mem_kv")
        dmemn = _mm(dmkv, wl["w_mem_kv"], tb=True, b_kind="rows", name="mem_kv_proj_dx")
        g["mem_norm"][l] = _rms_bwd(mem, p["mem_norm"][l], [dmemn], want_dx=False, name="mem_norm_bwd")
        gb["w_in"] = dw(s["xn1"], dproj, "w_in")
        if l in (0, N_A):
            dh, g["norm_mix"][l] = _mm_rms_bwd(dproj, wl["w_in"], s["h"], p["norm_mix"][l], dh1, b_kind="rows", also_bf16=False,
                                               name="in_proj_dx")
        else:
            dh, dh_b, g["norm_mix"][l] = _mm_rms_bwd(dproj, wl["w_in"], s["h"], p["norm_mix"][l], dh1, b_kind="rows",
                                                     also_bf16=True, name="in_proj_dx")
        if l == N_A:
            dkv, dgain = _seg_rms_bwd(kv, k_gain, dks, width=KV_HALF, col=0, out_buf=lax.empty((T, 2 * KV_HALF), BF16),
                                      name="k_norm_bwd")
            g["k_norm"] = _fold_heads(dgain, SWA_KV_HEADS)
            dkv = _sum_into(dvs[0], dvs[1], dkv, 1, name="dv_sum")
            g_kv = _mm(hn_kv, dkv, ta=True, out_kind="rows", out_buf=lax.empty(w_kv.shape, BF16), name="w_kv_grad")
            dh, dh_b, g["kv_norm"] = _mm_rms_bwd(dkv, w_kv, h_kv, p["kv_norm"], dh, b_kind="rows", also_bf16=True,
                                                 name="kv_proj_dx")
        if l + 1 < DEPTH:
            reduce_wait(l + 1, dh)
        token = reduce_layer(l, gb)
    grads = {n: (jnp.stack(v) if isinstance(v, list) else v) for n, v in g.items()}
    return loss, dh, grads, g_kv


def kernel(x, mem, norm_mix, w_in, pool_w, pool_scale, kv_norm, w_kv, k_norm, q_norm, sinks, mem_norm, w_mem_kv, mem_q_norm, mem_k_norm, w_out, norm_mlp, w_up, w_down, loss_target, m_norm_mix, m_w_in, m_pool_w, m_pool_scale, m_kv_norm, m_w_kv, m_k_norm, m_q_norm, m_sinks, m_mem_norm, m_w_mem_kv, m_mem_q_norm, m_mem_k_norm, m_w_out, m_norm_mlp, m_w_up, m_w_down, v_norm_mix, v_w_in, v_pool_w, v_pool_scale, v_kv_norm, v_w_kv, v_k_norm, v_q_norm, v_sinks, v_mem_norm, v_w_mem_kv, v_mem_q_norm, v_mem_k_norm, v_w_out, v_norm_mlp, v_w_up, v_w_down):
    weights = dict(norm_mix=norm_mix, w_in=w_in, pool_w=pool_w, pool_scale=pool_scale, kv_norm=kv_norm, w_kv=w_kv,
                   k_norm=k_norm, q_norm=q_norm, sinks=sinks, mem_norm=mem_norm, w_mem_kv=w_mem_kv,
                   mem_q_norm=mem_q_norm, mem_k_norm=mem_k_norm, w_out=w_out, norm_mlp=norm_mlp, w_up=w_up, w_down=w_down)
    mom1 = dict(norm_mix=m_norm_mix, w_in=m_w_in, pool_w=m_pool_w, pool_scale=m_pool_scale, kv_norm=m_kv_norm, w_kv=m_w_kv,
                k_norm=m_k_norm, q_norm=m_q_norm, sinks=m_sinks, mem_norm=m_mem_norm, w_mem_kv=m_w_mem_kv,
                mem_q_norm=m_mem_q_norm, mem_k_norm=m_mem_k_norm, w_out=m_w_out, norm_mlp=m_norm_mlp, w_up=m_w_up,
                w_down=m_w_down)
    mom2 = dict(norm_mix=v_norm_mix, w_in=v_w_in, pool_w=v_pool_w, pool_scale=v_pool_scale, kv_norm=v_kv_norm, w_kv=v_w_kv,
                k_norm=v_k_norm, q_norm=v_q_norm, sinks=v_sinks, mem_norm=v_mem_norm, w_mem_kv=v_w_mem_kv,
                mem_q_norm=v_mem_q_norm, mem_k_norm=v_mem_k_norm, w_out=v_w_out, norm_mlp=v_norm_mlp, w_up=v_w_up,
                w_down=v_w_down)
    names = list(weights)
    x_pos, y_pos, core = (lax.axis_index(n).astype(jnp.int32) for n in AXES)
    me, my_chip = 4 * x_pos + 2 * y_pos + core, 2 * x_pos + y_pos
    shard = MAIN_WIDTH // N_DEV

    def layer_shards(l, zero=0.0):
        return [(weights[n][l:l + 1] + zero).astype(BF16) for n in LAYERED]

    def usable(arrays):
        wl = dict(zip(LAYERED, arrays))
        wl["w_up"] = wl["w_up"].transpose(1, 2, 0, 3).reshape(1, D_MODEL, D_FF)
        return wl

    scale_block = jnp.pad(pool_scale, ((0, 8 - N_A), (0, 128 - shard)))
    *first, first_done = _all_gather(layer_shards(0) + [w_kv[None].astype(BF16), scale_block], name="gather_first")
    p = {n: weights[n] for n in SMALL_NAMES}
    p["pool_scale"] = first[-1][:, :N_A, :shard].transpose(1, 0, 2).reshape(N_A, MAIN_WIDTH)
    gathers, reduces, parts = {}, {}, {}

    def fetch(l, after):
        if l == 0:
            got, done = first[:len(LAYERED)], first_done
        else:
            got, done = _push_wait(_gather_copies, *gathers.pop(l), after, name=f"gather_wait_{l}")
        token = 0.0
        if l + 1 < DEPTH:
            srcs = layer_shards(l + 1, done[0, 0])
            lands = [_with_own_slot(a, me, N_DEV) for a in srcs]
            *handles, block = _push_start(_gather_copies, N_DEV - 1, srcs, lands, name=f"gather_start_{l + 1}")
            gathers[l + 1], token = handles, block[0, 0]
        return usable(got), token

    def by_core(gb):
        gb = dict(gb)
        if "w_up" in gb:
            gb["w_up"] = gb["w_up"].reshape(D_MODEL, N_DEV, D_FF // N_DEV).transpose(1, 0, 2)
        order = [n for n in LAYERED if n in gb] + [n for n in gb if n not in LAYERED]
        return {n: gb[n].reshape((N_CHIP, 2) + _view2d(gb[n].shape[1:] if n == "w_up" else gb[n].shape[2:])) for n in order}

    def pair_sums(views, sib, tag):
        return [_pair_sum(a, b, core, name=f"chip_sum_{n}_{tag}", out_dtype=BF16) for (n, a), b in zip(views.items(), sib)]

    def chip_sums(gb, tag, whole=()):
        views = by_core(gb)
        sib, sib_whole = _sibling_exchange(list(views.values()), list(whole), name="reduce_sibling_" + tag)
        return pair_sums(views, sib, tag), sib_whole

    def start_chip_exchange(sums, tag, whole=()):
        lands = [_with_own_slot(lax.dynamic_index_in_dim(a, my_chip, 0, keepdims=False), my_chip, N_CHIP) for a in sums]
        lands += [_with_own_slot(a, my_chip, N_CHIP) for a in whole]
        copies = functools.partial(_chip_copies, n_whole=len(whole))
        *handles, block = _push_start(copies, N_CHIP - 1, [*sums, *whole], lands, name="reduce_start_" + tag)
        return (copies, *handles), block[0, 0]

    mlp = ("w_up", "w_down")

    def reduce_layer(l, gb, early=False):
        if early and l > 0:
            return None
        if l == 0 and not early:
            reduces["rest"] = {n: a for n, a in gb.items() if n not in mlp}
            return None
        tag = "0_mlp" if early else str(l)
        sums, _ = chip_sums({n: gb[n] for n in mlp} if early else gb, tag)
        reduces[l], token = start_chip_exchange(sums, tag)
        return token

    def reduce_wait(l, after):
        copies, *handles = reduces.pop(l)
        return _push_wait(copies, *handles, after, name=f"reduce_wait_{l}")[0]

    def layer_wait(l, after):
        parts[l] = reduce_wait(l, after)

    loss, grad_x, grads, g_kv = _train_pass(x[0], mem[0], loss_target[0], p, first[len(LAYERED)], fetch, reduce_layer, layer_wait)

    last = dict(reduces.pop("rest"))
    last["w_kv"] = g_kv
    last["pool_scale"] = grads["pool_scale"].reshape(N_A, N_DEV, shard).transpose(1, 0, 2).astype(BF16)[:, None]
    small = _pack_small(grads)
    sums, (sib_small,) = chip_sums(last, "0", whole=[small])
    chip_small = _pair_sum(small, sib_small, core, name="chip_sum_small", out_dtype=F32)
    reduces["rest"], _ = start_chip_exchange(sums, "0_rest", whole=[chip_small])

    def adamw(n, n_parts):
        res = _adamw(n_parts, *(d[n].reshape(_view2d(d[n].shape)) for d in (weights, mom1, mom2)), name="adamw_" + n)
        return [r.reshape(weights[n].shape) for r in res]

    p_up, p_down = reduce_wait(0, chip_small)
    parts[0] = [None, None, None, p_up, p_down]
    new = {n: adamw(n, [parts[l][LAYERED.index(n)] for l in range(DEPTH)]) for n in mlp}
    p_in, p_mem_kv, p_out, parts_kv, parts_scale, parts_small = reduce_wait("rest", new["w_down"][0])
    parts[0][:3] = [p_in, p_mem_kv, p_out]
    new.update({n: adamw(n, [parts[l][k] for l in range(DEPTH)]) for k, n in enumerate(LAYERED) if n not in mlp})
    new["w_kv"] = adamw("w_kv", [parts_kv])
    new["pool_scale"] = adamw("pool_scale", [parts_scale])
    res = _adamw([parts_small], _pack_small(weights), _pack_small(mom1), _pack_small(mom2), name="adamw_replicated")
    for n, vals in zip(SMALL_NAMES, zip(*(_unpack_small(r, weights).values() for r in res))):
        new[n] = list(vals)
    outs = [new[n][k] for k in range(4) for n in names]
    total = lax.psum(loss[0, 0], AXES)
    return (total, grad_x[None], *outs)
```

```python
import functools
import math

import jax
import jax.numpy as jnp
from jax import lax
from jax.experimental import pallas as pl
from jax.experimental.pallas import tpu as pltpu

F32 = jnp.float32
BF16 = jnp.bfloat16
MESH = pl.DeviceIdType.MESH
AXES = ("x", "y", "c")

D_MODEL = 1024
DEPTH = 4
N_A = 2
HEAD_DIM = 64
MEM_HEADS = 4
MEM_WIDTH = MEM_HEADS * HEAD_DIM
MAIN_WIDTH = D_MODEL - MEM_WIDTH
POOL_GROUPS = 4
POOL_GROUP_DIM = MAIN_WIDTH // POOL_GROUPS
POOL_HALO = 16
SWA_Q_HEADS = MAIN_WIDTH // HEAD_DIM
SWA_KV_HEADS = 4
SWA_GROUP = SWA_Q_HEADS // SWA_KV_HEADS
KV_HALF = SWA_KV_HEADS * HEAD_DIM
BLOCK = 128
D_FF = 4 * D_MODEL
EPS = 1e-6
SCALE = HEAD_DIM ** -0.5
NEG = float(jnp.finfo(jnp.float32).min)
N_DEV = 8
N_CHIP = 4

ADAM_LR = 0.001
ADAM_B1 = 0.9
ADAM_B2 = 0.999
ADAM_EPS = 1e-08
ADAM_WD = 0.01
ADAM_STEP = 10

PACK_W = 512
VMEM_LIMIT = 52 * 1024 * 1024
MM_TILE = 1024
LAYERED = ("w_in", "w_mem_kv", "w_out", "w_up", "w_down")
SMALL_NAMES = ("norm_mix", "pool_w", "kv_norm", "k_norm", "q_norm", "sinks", "mem_norm", "mem_q_norm", "mem_k_norm",
               "norm_mlp")


ANY = pl.BlockSpec(memory_space=pl.ANY)


def _params(*sem):
    return pltpu.CompilerParams(dimension_semantics=sem, vmem_limit_bytes=VMEM_LIMIT)


def _mm(a, b, *, name, ta=False, tb=False, b_kind=None, layer=0, res=None, relu2=False, mul2=None, out_dtype=F32,
        out_kind=None, out_buf=None):
    if ta:
        K, M = a.shape
    else:
        M, K = a.shape
    if b_kind is None:
        rows_b, cols_b = b.shape
    elif b_kind == "rows":
        rows_b, cols_b = b.shape[0] * b.shape[2], b.shape[3]
    else:
        rows_b, cols_b = b.shape[1:]
    N, K2 = (rows_b, cols_b) if tb else (cols_b, rows_b)
    assert K == K2, (a.shape, b.shape)
    tm = min(M, MM_TILE if K <= MM_TILE else MM_TILE // 2)
    tn = min(N, MM_TILE)
    assert M % tm == 0 and N % tn == 0
    row_tile, col_tile = (tn, K) if tb else (K, tn)
    a_spec = pl.BlockSpec((K, tm), lambda j, i: (0, i)) if ta else pl.BlockSpec((tm, K), lambda j, i: (i, 0))

    def rc(j):
        return (j, 0) if tb else (0, j)

    if b_kind is None:
        b_spec = pl.BlockSpec((row_tile, col_tile), lambda j, i: rc(j))
    elif b_kind == "rows":
        per = row_tile // b.shape[2]
        b_spec = pl.BlockSpec((per, None, b.shape[2], col_tile), lambda j, i: (rc(j)[0], layer, 0, rc(j)[1]))
    else:
        b_spec = pl.BlockSpec((None, row_tile, col_tile), lambda j, i: (layer, *rc(j)))
    o_spec = pl.BlockSpec((tm, tn), lambda j, i: (i, j))
    dn = (((0 if ta else 1,), (1 if tb else 0,)), ((), ()))
    extra = [e for e in (res, mul2) if e is not None]
    n_in = 2 + len(extra) + (1 if out_buf is not None else 0)

    def body(*refs):
        a_ref, b_ref = refs[0], refs[1]
        extra_refs = refs[2:2 + len(extra)]
        out = refs[n_in]
        bv = b_ref[...].astype(BF16).reshape(row_tile, col_tile)
        v = lax.dot_general(a_ref[...].astype(BF16), bv, dn, preferred_element_type=F32)
        if res is not None:
            v = extra_refs[0][...] + v
        elif mul2 is not None:
            v = v * (2.0 * jnp.sqrt(extra_refs[0][...].astype(F32)))
        if relu2:
            r = jnp.maximum(v, 0.0)
            v = r * r
        out[...] = v.astype(out.dtype).reshape(out.shape)

    in_specs = [a_spec, b_spec] + [o_spec] * len(extra)
    operands = [a, b, *extra]
    aliases = {}
    if out_kind is None:
        out_shape = jax.ShapeDtypeStruct((M, N), BF16 if relu2 else out_dtype)
        out_specs = o_spec
    else:
        if out_kind == "rows":
            s = out_buf.shape[2]
            out_specs = pl.BlockSpec((tm // s, None, s, tn), lambda j, i: (i, layer, 0, j))
        else:
            out_specs = pl.BlockSpec((None, tm, tn), lambda j, i: (layer, i, j))
        out_shape = jax.ShapeDtypeStruct(out_buf.shape, out_buf.dtype)
        in_specs.append(ANY)
        operands.append(out_buf)
        aliases = {len(operands) - 1: 0}
    return pl.pallas_call(
        body, name=name, grid=(N // tn, M // tm), in_specs=in_specs, out_specs=out_specs, out_shape=out_shape,
        input_output_aliases=aliases, compiler_params=_params("parallel", "parallel"),
    )(*operands)


def _weight_block(b, b_kind, transposed, tn):
    if b_kind == "rows":
        s = b.shape[2]
        rows, cols = b.shape[0] * s, b.shape[3]
        if transposed:
            return (lambda at: pl.BlockSpec((b.shape[0], None, s, cols), lambda *g: (0, 0, 0, 0))), rows, cols
        return (lambda at: pl.BlockSpec((b.shape[0], None, s, tn), lambda *g: (0, 0, 0, at(*g)))), rows, cols
    rows, cols = b.shape[1:]
    if transposed:
        return (lambda at: pl.BlockSpec((None, rows, cols), lambda *g: (0, 0, 0))), rows, cols
    return (lambda at: pl.BlockSpec((None, rows, tn), lambda *g: (0, 0, at(*g)))), rows, cols


def _norm_mm(x, gain, b, *, b_kind, name, relu2=False):
    M, K = x.shape
    tm = min(M, MM_TILE)
    spec_of, rows, N = _weight_block(b, b_kind, False, min(MM_TILE, b.shape[-1]))
    tn = min(N, MM_TILE)
    assert rows == K and M % tm == 0 and N % tn == 0

    def body(x_ref, g_ref, b_ref, xn_ref, o_ref):
        @pl.when(pl.program_id(1) == 0)
        def _():
            xv = x_ref[...]
            r = lax.rsqrt(jnp.mean(xv * xv, axis=-1, keepdims=True) + EPS)
            xn_ref[...] = ((xv * r) * g_ref[...]).astype(xn_ref.dtype)

        v = jnp.dot(xn_ref[...], b_ref[...].astype(BF16).reshape(K, tn), preferred_element_type=F32)
        if relu2:
            r2 = jnp.maximum(v, 0.0)
            v = r2 * r2
        o_ref[...] = v.astype(o_ref.dtype)

    rows_spec = pl.BlockSpec((tm, K), lambda i, j: (i, 0))
    return pl.pallas_call(
        body, name=name, grid=(M // tm, N // tn),
        in_specs=[rows_spec, pl.BlockSpec((1, K), lambda i, j: (0, 0)), spec_of(lambda i, j: j)],
        out_specs=(rows_spec, pl.BlockSpec((tm, tn), lambda i, j: (i, j))),
        out_shape=(jax.ShapeDtypeStruct((M, K), BF16), jax.ShapeDtypeStruct((M, N), BF16 if relu2 else F32)),
        compiler_params=_params("parallel", "arbitrary"),
    )(x, gain.reshape(1, K), b)


def _mm_rms_bwd(a, b, x, gain, res, *, b_kind, name, also_bf16):
    M, K = a.shape
    spec_of, N, cols = _weight_block(b, b_kind, True, None)
    assert cols == K and x.shape == (M, N)
    tm = min(M, MM_TILE if K <= MM_TILE else MM_TILE // 2)
    assert M % tm == 0

    def body(a_ref, b_ref, x_ref, g_ref, res_ref, *outs):
        i = pl.program_id(0)
        dy = lax.dot_general(a_ref[...].astype(BF16), b_ref[...].astype(BF16).reshape(N, K), (((1,), (1,)), ((), ())),
                             preferred_element_type=F32)
        xv = x_ref[...]
        r = lax.rsqrt(jnp.mean(xv * xv, axis=-1, keepdims=True) + EPS)
        xh = xv * r
        part = jnp.sum(dy * xh, axis=0, keepdims=True)
        dg_ref = outs[-1]

        @pl.when(i == 0)
        def _():
            dg_ref[...] = part

        @pl.when(i > 0)
        def _():
            dg_ref[...] += part

        gdy = dy * g_ref[...]
        dx = res_ref[...] + r * (gdy - xh * jnp.mean(gdy * xh, axis=-1, keepdims=True))
        outs[0][...] = dx
        if also_bf16:
            outs[1][...] = dx.astype(BF16)

    row = pl.BlockSpec((tm, N), lambda i: (i, 0))
    vec = pl.BlockSpec((1, N), lambda i: (0, 0))
    out_specs = [row] + ([row] if also_bf16 else []) + [vec]
    out_shape = [jax.ShapeDtypeStruct((M, N), F32)] + ([jax.ShapeDtypeStruct((M, N), BF16)] if also_bf16 else [])
    outs = pl.pallas_call(
        body, name=name, grid=(M // tm,),
        in_specs=[pl.BlockSpec((tm, K), lambda i: (i, 0)), spec_of(None), row, vec, row], out_specs=out_specs,
        out_shape=out_shape + [jax.ShapeDtypeStruct((1, N), F32)], compiler_params=_params("arbitrary"),
    )(a, b, x, gain.reshape(1, N), res)
    return (*outs[:-1], outs[-1].reshape(N))


def _row_tile(rows, d):
    t = min(rows, (1024 * 1024) // d)
    while rows % t or (t != rows and t % 16):
        t -= 1
    return t


def _rms_fwd(x, g, *, name, out_dtype=BF16):
    R, D = x.shape
    tr = _row_tile(R, D)

    def body(x_ref, g_ref, o_ref):
        xv = x_ref[...].astype(F32)
        r = lax.rsqrt(jnp.mean(xv * xv, axis=-1, keepdims=True) + EPS)
        o_ref[...] = ((xv * r) * g_ref[...]).astype(o_ref.dtype)

    return pl.pallas_call(
        body, name=name, grid=(R // tr,),
        in_specs=[pl.BlockSpec((tr, D), lambda i: (i, 0)), pl.BlockSpec((1, D), lambda i: (0, 0))],
        out_specs=pl.BlockSpec((tr, D), lambda i: (i, 0)), out_shape=jax.ShapeDtypeStruct((R, D), out_dtype),
        compiler_params=_params("parallel"),
    )(x, g.reshape(1, D))


def _rms_bwd(x, g, dys, *, name, res=None, want_dx=True, also_bf16=False):
    R, D = x.shape
    tr = _row_tile(R, D)
    n_dy = len(dys)
    has_res = res is not None

    def body(*refs):
        x_ref, g_ref = refs[0], refs[1]
        dy_refs = refs[2:2 + n_dy]
        res_ref = refs[2 + n_dy] if has_res else None
        outs = refs[2 + n_dy + (1 if has_res else 0):]
        dg_ref = outs[-1]
        i = pl.program_id(0)
        xv = x_ref[...].astype(F32)
        dy = dy_refs[0][...].astype(F32)
        for extra in dy_refs[1:]:
            dy = dy + extra[...].astype(F32)
        r = lax.rsqrt(jnp.mean(xv * xv, axis=-1, keepdims=True) + EPS)
        xh = xv * r
        part = jnp.sum(dy * xh, axis=0, keepdims=True)

        @pl.when(i == 0)
        def _():
            dg_ref[...] = part

        @pl.when(i > 0)
        def _():
            dg_ref[...] += part

        if want_dx:
            gdy = dy * g_ref[...]
            dx = r * (gdy - xh * jnp.mean(gdy * xh, axis=-1, keepdims=True))
            if has_res:
                dx = res_ref[...] + dx
            outs[0][...] = dx
            if also_bf16:
                outs[1][...] = dx.astype(BF16)

    row = pl.BlockSpec((tr, D), lambda i: (i, 0))
    vec = pl.BlockSpec((1, D), lambda i: (0, 0))
    out_shape = [jax.ShapeDtypeStruct((1, D), F32)]
    out_specs = [vec]
    if also_bf16:
        out_shape = [jax.ShapeDtypeStruct((R, D), BF16)] + out_shape
        out_specs = [row] + out_specs
    if want_dx:
        out_shape = [jax.ShapeDtypeStruct((R, D), F32)] + out_shape
        out_specs = [row] + out_specs
    outs = pl.pallas_call(
        body, name=name, grid=(R // tr,),
        in_specs=[row, vec] + [row] * (n_dy + (1 if has_res else 0)), out_specs=out_specs, out_shape=out_shape,
        compiler_params=_params("arbitrary"),
    )(x, g.reshape(1, D), *dys, *([res] if has_res else []))
    return (*outs[:-1], outs[-1].reshape(D)) if want_dx else outs[0].reshape(D)


POOL_TILE = 512
MEM_Q_TILE = 2048


def _softmax(q, k, bias, valid, sink):
    s = lax.dot_general(q, k, (((1,), (1,)), ((), ())), preferred_element_type=F32) * SCALE
    if bias is not None:
        s = s - bias
    if valid is not None:
        s = jnp.where(valid, s, NEG)
    m = jnp.max(s, axis=-1, keepdims=True)
    if sink is not None:
        m = jnp.maximum(m, sink)
    e = jnp.exp(s - m)
    z = jnp.sum(e, axis=-1, keepdims=True)
    if sink is None:
        return e * (1.0 / z), None
    es = jnp.exp(sink - m)
    inv = 1.0 / (z + es)
    return e * inv, es * inv


LANES = 128


def _seg_mean(v):
    r = lax.broadcasted_iota(jnp.int32, (LANES, LANES), 0) // HEAD_DIM
    c = lax.broadcasted_iota(jnp.int32, (LANES, LANES), 1) // HEAD_DIM
    seg = jnp.where(r == c, 1.0 / HEAD_DIM, 0.0).astype(BF16)
    hi = v.astype(BF16)
    lo = (v - hi.astype(F32)).astype(BF16)
    parts = []
    for g in range(v.shape[1] // LANES):
        sl = slice(g * LANES, (g + 1) * LANES)
        parts.append(jnp.dot(hi[:, sl], seg, preferred_element_type=F32) + jnp.dot(lo[:, sl], seg, preferred_element_type=F32))
    return parts[0] if len(parts) == 1 else jnp.concatenate(parts, axis=1)


def _cols(rows, width, col):
    return pl.BlockSpec((rows, width), lambda i: (i, col))


def _head_gain(g, heads):
    return jnp.tile(g, heads).reshape(1, heads * HEAD_DIM)


def _fold_heads(dg, heads):
    return dg.reshape(heads, HEAD_DIM).sum(axis=0)


def _seg_rms_fwd(x, gain, *, width, col, name):
    R = x.shape[0]
    tr = _row_tile(R, width)

    def body(x_ref, g_ref, o_ref):
        xv = x_ref[...]
        r = lax.rsqrt(_seg_mean(xv * xv) + EPS)
        o_ref[...] = ((xv * r) * g_ref[...]).astype(o_ref.dtype)

    return pl.pallas_call(
        body, name=name, grid=(R // tr,), in_specs=[_cols(tr, width, col), pl.BlockSpec((1, width), lambda i: (0, 0))],
        out_specs=_cols(tr, width, 0), out_shape=jax.ShapeDtypeStruct((R, width), BF16), compiler_params=_params("parallel"),
    )(x, gain)


def _seg_rms_bwd(x, gain, dys, *, width, col, name, out_buf=None, out_col=0):
    R = x.shape[0]
    tr = _row_tile(R, width)
    n_dy = len(dys)

    def body(*refs):
        x_ref, g_ref = refs[0], refs[1]
        dy_refs = refs[2:2 + n_dy]
        dx_ref, dg_ref = refs[-2], refs[-1]
        i = pl.program_id(0)
        xv = x_ref[...]
        dy = dy_refs[0][...]
        for extra in dy_refs[1:]:
            dy = dy + extra[...]
        r = lax.rsqrt(_seg_mean(xv * xv) + EPS)
        xh = xv * r
        part = jnp.sum(dy * xh, axis=0, keepdims=True)

        @pl.when(i == 0)
        def _():
            dg_ref[...] = part

        @pl.when(i > 0)
        def _():
            dg_ref[...] += part

        gdy = dy * g_ref[...]
        dx_ref[...] = (r * (gdy - xh * _seg_mean(gdy * xh))).astype(dx_ref.dtype)

    vec = pl.BlockSpec((1, width), lambda i: (0, 0))
    in_specs = [_cols(tr, width, col), vec] + [_cols(tr, width, 0)] * n_dy
    operands = [x, gain, *dys]
    aliases = {}
    dx_shape = jax.ShapeDtypeStruct((R, width), BF16)
    if out_buf is not None:
        in_specs.append(ANY)
        operands.append(out_buf)
        aliases = {len(operands) - 1: 0}
        dx_shape = jax.ShapeDtypeStruct(out_buf.shape, out_buf.dtype)
    return pl.pallas_call(
        body, name=name, grid=(R // tr,), in_specs=in_specs, out_specs=(_cols(tr, width, out_col), vec),
        out_shape=(dx_shape, jax.ShapeDtypeStruct((1, width), F32)), input_output_aliases=aliases,
        compiler_params=_params("arbitrary"),
    )(*operands)


def _sum_into(a, b, out_buf, out_col, *, name):
    R, width = a.shape
    tr = _row_tile(R, width)

    def body(a_ref, b_ref, _, o_ref):
        o_ref[...] = (a_ref[...] + b_ref[...]).astype(o_ref.dtype)

    return pl.pallas_call(
        body, name=name, grid=(R // tr,), in_specs=[_cols(tr, width, 0), _cols(tr, width, 0), ANY],
        out_specs=_cols(tr, width, out_col), out_shape=jax.ShapeDtypeStruct(out_buf.shape, out_buf.dtype),
        input_output_aliases={2: 0}, compiler_params=_params("parallel"),
    )(a, b, out_buf)


def _pool_lane_group():
    return lax.broadcasted_iota(jnp.int32, (1, MAIN_WIDTH), 1) // POOL_GROUP_DIM


def _pool_pick(group, per_window):
    s1, s2, s3, s4 = per_window
    return jnp.where(group == 0, s1, jnp.where(group == 1, s2, jnp.where(group == 2, s3, s4)))


def _pool_delta(u_ref, halo_ref, tile):
    group = _pool_lane_group()
    halo = jnp.where(tile == 0, 0.0, halo_ref[...])
    ext = jnp.concatenate([halo, u_ref[...]], axis=0)
    n = ext.shape[0]
    s1 = ext + pltpu.roll(ext, 1, 0)
    s2 = s1 + pltpu.roll(s1, 2, 0)
    s3 = s2 + pltpu.roll(s2, 4, 0)
    s4 = s3 + pltpu.roll(s3, 8, 0)
    ws = _pool_pick(group, (s1, s2, s3, s4))[POOL_HALO:n]
    t = tile * POOL_TILE + lax.broadcasted_iota(jnp.int32, (POOL_TILE, 1), 0)
    cnt = jnp.minimum(t + 1, _pool_pick(group, (2, 4, 8, 16))).astype(F32)
    return ws / cnt - u_ref[...], cnt


def _pool_in_specs():
    per_tile = POOL_TILE // POOL_HALO
    cur = _cols(POOL_TILE, MAIN_WIDTH, 0)
    prev = pl.BlockSpec((POOL_HALO, MAIN_WIDTH), lambda i: (jnp.maximum(i * per_tile - 1, 0), 0))
    mix = pl.BlockSpec((MAIN_WIDTH, MAIN_WIDTH), lambda i: (0, 0))
    vec = pl.BlockSpec((1, MAIN_WIDTH), lambda i: (0, 0))
    return cur, prev, mix, vec


def _pool_mix_fwd(proj, mix, scale, cat, *, name):
    T = proj.shape[0]
    assert T % POOL_TILE == 0
    cur, prev, mix_spec, vec = _pool_in_specs()

    def body(u_ref, halo_ref, mix_ref, sc_ref, _, o_ref):
        d, _cnt = _pool_delta(u_ref, halo_ref, pl.program_id(0))
        mixed = jnp.dot(d.astype(BF16), mix_ref[...].astype(BF16), preferred_element_type=F32)
        o_ref[...] = (mixed * sc_ref[...]).astype(o_ref.dtype)

    return pl.pallas_call(
        body, name=name, grid=(T // POOL_TILE,), in_specs=[cur, prev, mix_spec, vec, ANY], out_specs=cur,
        out_shape=jax.ShapeDtypeStruct(cat.shape, cat.dtype), input_output_aliases={4: 0}, compiler_params=_params("parallel"),
    )(proj, proj, mix, scale, cat)


def _pool_mix_bwd(proj, mix, scale, dcat, *, name):
    T = proj.shape[0]
    nt = T // POOL_TILE
    per_tile = POOL_TILE // POOL_HALO
    cur, prev, mix_spec, vec = _pool_in_specs()
    nxt = pl.BlockSpec((POOL_HALO, MAIN_WIDTH), lambda i: (jnp.minimum((i + 1) * per_tile, nt * per_tile - 1), 0))

    def body(u_ref, halo_ref, mix_ref, sc_ref, do_ref, donext_ref, du_ref, dmix_ref, dsc_ref):
        tile = pl.program_id(0)
        group = _pool_lane_group()
        d, cnt = _pool_delta(u_ref, halo_ref, tile)
        mixb = mix_ref[...].astype(BF16)
        db = d.astype(BF16)
        mixed = jnp.dot(db, mixb, preferred_element_type=F32)
        dout = do_ref[...]
        dsc = jnp.sum(dout * mixed, axis=0, keepdims=True)
        sc = sc_ref[...]
        dmixed = (dout * sc).astype(BF16)
        dmix = lax.dot_general(db, dmixed, (((0,), (0,)), ((), ())), preferred_element_type=F32)

        @pl.when(tile == 0)
        def _():
            dmix_ref[...] = dmix
            dsc_ref[...] = dsc

        @pl.when(tile > 0)
        def _():
            dmix_ref[...] += dmix
            dsc_ref[...] += dsc

        dnext = jnp.where(tile == nt - 1, 0.0, donext_ref[...])
        dmixed_ext = jnp.concatenate([dmixed, (dnext * sc).astype(BF16)], axis=0)
        dd_ext = lax.dot_general(dmixed_ext, mixb, (((1,), (1,)), ((), ())), preferred_element_type=F32)
        window = _pool_pick(group, (2.0, 4.0, 8.0, 16.0))
        cnt_ext = jnp.concatenate([cnt, jnp.broadcast_to(window, (POOL_HALO, MAIN_WIDTH))], axis=0)
        q = dd_ext / cnt_ext
        n = q.shape[0]
        r1 = q + pltpu.roll(q, n - 1, 0)
        r2 = r1 + pltpu.roll(r1, n - 2, 0)
        r3 = r2 + pltpu.roll(r2, n - 4, 0)
        r4 = r3 + pltpu.roll(r3, n - 8, 0)
        back = _pool_pick(group, (r1, r2, r3, r4))
        du_ref[...] = (back[0:POOL_TILE] - dd_ext[0:POOL_TILE]).astype(du_ref.dtype)

    return pl.pallas_call(
        body, name=name, grid=(nt,), in_specs=[cur, prev, mix_spec, vec, cur, nxt], out_specs=(cur, mix_spec, vec),
        out_shape=(jax.ShapeDtypeStruct((T, D_MODEL), BF16), jax.ShapeDtypeStruct((MAIN_WIDTH, MAIN_WIDTH), F32),
                   jax.ShapeDtypeStruct((1, MAIN_WIDTH), F32)),
        compiler_params=_params("arbitrary"),
    )(proj, proj, mix, scale, dcat, dcat)


def _head(a, h):
    return a[:, h * HEAD_DIM:(h + 1) * HEAD_DIM]


def _swa_mask(blk):
    rows = SWA_GROUP * BLOCK
    qi = lax.broadcasted_iota(jnp.int32, (rows, 2 * BLOCK), 0) % BLOCK
    kj = lax.broadcasted_iota(jnp.int32, (rows, 2 * BLOCK), 1)
    dist = qi + BLOCK - kj
    valid = (dist >= 0) & (dist < BLOCK) & ((blk > 0) | (kj >= BLOCK))
    return dist.astype(F32), valid


def _swa_head_terms(sink_ref, kvh, dist):
    grp = lax.broadcasted_iota(jnp.int32, (SWA_GROUP * BLOCK, 1), 0) // BLOCK
    slopes = [2.0 ** (-8.0 * (kvh * SWA_GROUP + g + 1) / SWA_Q_HEADS) for g in range(SWA_GROUP)]
    sinks = [sink_ref[kvh * SWA_GROUP + g] for g in range(SWA_GROUP)]
    slope = jnp.where(grp == 0, slopes[0], jnp.where(grp == 1, slopes[1], slopes[2]))
    sink = jnp.where(grp == 0, sinks[0], jnp.where(grp == 1, sinks[1], sinks[2]))
    return slope * dist, sink


def _stack_heads(a, kvh):
    return jnp.concatenate([_head(a, kvh * SWA_GROUP + g) for g in range(SWA_GROUP)], axis=0)


def _swa_specs(nb):
    def at(n):
        return jnp.minimum(n, nb - 1)

    q = pl.BlockSpec((BLOCK, MAIN_WIDTH), lambda n: (at(n), 0))
    k_prev = pl.BlockSpec((BLOCK, KV_HALF), lambda n: (jnp.maximum(at(n) - 1, 0), 0))
    k_cur = pl.BlockSpec((BLOCK, KV_HALF), lambda n: (at(n), 0))
    v_prev = pl.BlockSpec((BLOCK, KV_HALF), lambda n: (jnp.maximum(at(n) - 1, 0), 1))
    v_cur = pl.BlockSpec((BLOCK, KV_HALF), lambda n: (at(n), 1))
    return q, k_prev, k_cur, v_prev, v_cur


def _swa_attn_fwd(qn, kn, kv, sinks, cat, *, name):
    T = qn.shape[0]
    nb = T // BLOCK
    q_spec, k_prev, k_cur, v_prev, v_cur = _swa_specs(nb)

    def body(sink_ref, q_ref, kp_ref, kc_ref, vp_ref, vc_ref, _, o_ref):
        dist, valid = _swa_mask(pl.program_id(0))
        kk = jnp.concatenate([kp_ref[...], kc_ref[...]], axis=0)
        vv = jnp.concatenate([vp_ref[...], vc_ref[...]], axis=0).astype(BF16)
        q = q_ref[...]
        outs = []
        for kvh in range(SWA_KV_HEADS):
            bias, sink = _swa_head_terms(sink_ref, kvh, dist)
            p, _ps = _softmax(_stack_heads(q, kvh), _head(kk, kvh), bias, valid, sink)
            o = jnp.dot(p.astype(BF16), _head(vv, kvh), preferred_element_type=F32)
            outs += [o[g * BLOCK:(g + 1) * BLOCK] for g in range(SWA_GROUP)]
        o_ref[...] = jnp.concatenate(outs, axis=1).astype(o_ref.dtype)

    return pl.pallas_call(
        body, name=name, grid=(nb,),
        in_specs=[pl.BlockSpec(memory_space=pltpu.SMEM), q_spec, k_prev, k_cur, v_prev, v_cur, ANY], out_specs=q_spec,
        out_shape=jax.ShapeDtypeStruct(cat.shape, cat.dtype), input_output_aliases={6: 0}, compiler_params=_params("parallel"),
    )(sinks, qn, kn, kn, kv, kv, cat)


def _swa_attn_bwd(qn, kn, kv, sinks, dcat, dqn, *, name):
    T = qn.shape[0]
    nb = T // BLOCK
    q_spec, k_prev, k_cur, v_prev, v_cur = _swa_specs(nb)
    late = pl.BlockSpec((BLOCK, KV_HALF), lambda n: (jnp.maximum(n - 1, 0), 0))
    tn_dims = (((0,), (0,)), ((), ()))

    def body(sink_ref, q_ref, do_ref, kp_ref, kc_ref, vp_ref, vc_ref, _, dq_ref, dk_ref, dv_ref, ds_ref, ck, cv):
        blk = pl.program_id(0)

        @pl.when(blk == 0)
        def _():
            ck[...] = jnp.zeros_like(ck)
            cv[...] = jnp.zeros_like(cv)
            ds_ref[...] = jnp.zeros_like(ds_ref)

        @pl.when(blk < nb)
        def _():
            dist, valid = _swa_mask(blk)
            kk = jnp.concatenate([kp_ref[...], kc_ref[...]], axis=0)
            vv = jnp.concatenate([vp_ref[...], vc_ref[...]], axis=0).astype(BF16)
            q = q_ref[...]
            dout = do_ref[...].astype(BF16)
            lane = lax.broadcasted_iota(jnp.int32, (1, LANES), 1)
            dsinks = jnp.zeros((1, LANES), F32)
            dqs, dks, dvs = [], [], []
            for kvh in range(SWA_KV_HEADS):
                bias, sink = _swa_head_terms(sink_ref, kvh, dist)
                qq, kh, vh, dd = _stack_heads(q, kvh), _head(kk, kvh), _head(vv, kvh), _stack_heads(dout, kvh)
                p, ps = _softmax(qq, kh, bias, valid, sink)
                dp = lax.dot_general(dd, vh, (((1,), (1,)), ((), ())), preferred_element_type=F32)
                dsum = jnp.sum(p * dp, axis=-1, keepdims=True)
                ds = (p * (dp - dsum)).astype(BF16)
                dq = jnp.dot(ds, kh, preferred_element_type=F32) * SCALE
                dqs += [dq[g * BLOCK:(g + 1) * BLOCK] for g in range(SWA_GROUP)]
                dks.append(lax.dot_general(qq, ds, tn_dims, preferred_element_type=F32) * SCALE)
                dvs.append(lax.dot_general(dd, p.astype(BF16), tn_dims, preferred_element_type=F32))
                dsink = -(ps * dsum)
                for g in range(SWA_GROUP):
                    dsinks = dsinks + jnp.where(lane == kvh * SWA_GROUP + g, jnp.sum(dsink[g * BLOCK:(g + 1) * BLOCK]), 0.0)
            dq_ref[...] = jnp.concatenate(dqs, axis=1)
            dk = jnp.concatenate(dks, axis=0).T
            dv = jnp.concatenate(dvs, axis=0).T
            dk_ref[...] = ck[...] + dk[0:BLOCK]
            dv_ref[...] = cv[...] + dv[0:BLOCK]
            ck[...] = dk[BLOCK:2 * BLOCK]
            cv[...] = dv[BLOCK:2 * BLOCK]
            ds_ref[...] += dsinks

        @pl.when(blk == nb)
        def _():
            dk_ref[...] = ck[...]
            dv_ref[...] = cv[...]

    return pl.pallas_call(
        body, name=name, grid=(nb + 1,),
        in_specs=[pl.BlockSpec(memory_space=pltpu.SMEM), q_spec, q_spec, k_prev, k_cur, v_prev, v_cur, ANY],
        out_specs=(q_spec, late, late, pl.BlockSpec((1, LANES), lambda n: (0, 0))),
        out_shape=(jax.ShapeDtypeStruct(dqn.shape, dqn.dtype), jax.ShapeDtypeStruct((T, KV_HALF), F32),
                   jax.ShapeDtypeStruct((T, KV_HALF), F32), jax.ShapeDtypeStruct((1, LANES), F32)),
        scratch_shapes=[pltpu.VMEM((BLOCK, KV_HALF), F32), pltpu.VMEM((BLOCK, KV_HALF), F32)],
        input_output_aliases={7: 0}, compiler_params=_params("arbitrary"),
    )(sinks, qn, dcat, kn, kn, kv, kv, dqn)


def _mem_specs(M, tq, q_col):
    q = _cols(tq, MEM_WIDTH, q_col)
    k = pl.BlockSpec((M, MEM_WIDTH), lambda i: (0, 0))
    v = pl.BlockSpec((M, MEM_WIDTH), lambda i: (0, 1))
    return q, k, v


def _mem_attn_fwd(q, q_col, mkn, mkv, *, name):
    T = q.shape[0]
    M = mkn.shape[0]
    tq = min(T, MEM_Q_TILE)
    q_spec, k_spec, v_spec = _mem_specs(M, tq, q_col)

    def body(q_ref, k_ref, v_ref, o_ref):
        qq, kk, vv = q_ref[...], k_ref[...], v_ref[...].astype(BF16)
        outs = []
        for h in range(MEM_HEADS):
            p, _ps = _softmax(_head(qq, h), _head(kk, h), None, None, None)
            outs.append(jnp.dot(p.astype(BF16), _head(vv, h), preferred_element_type=F32))
        o_ref[...] = jnp.concatenate(outs, axis=1).astype(o_ref.dtype)

    return pl.pallas_call(
        body, name=name, grid=(T // tq,), in_specs=[q_spec, k_spec, v_spec], out_specs=_cols(tq, MEM_WIDTH, MAIN_WIDTH // MEM_WIDTH),
        out_shape=jax.ShapeDtypeStruct((T, D_MODEL), BF16), compiler_params=_params("parallel"),
    )(q, mkn, mkv)


def _mem_attn_bwd(q, q_col, mkn, mkv, dcat, *, dq_width, name):
    T = q.shape[0]
    M = mkn.shape[0]
    tq = min(T, MEM_Q_TILE)
    q_spec, k_spec, v_spec = _mem_specs(M, tq, q_col)
    last = MAIN_WIDTH // MEM_WIDTH
    tn_dims = (((0,), (0,)), ((), ()))

    def body(q_ref, do_ref, k_ref, v_ref, dq_ref, dk_ref, dv_ref):
        i = pl.program_id(0)
        qq, kk, vv, dout = q_ref[...], k_ref[...], v_ref[...].astype(BF16), do_ref[...].astype(BF16)
        dqs, dks, dvs = [], [], []
        for h in range(MEM_HEADS):
            qh, kh, vh, dh = _head(qq, h), _head(kk, h), _head(vv, h), _head(dout, h)
            p, _ps = _softmax(qh, kh, None, None, None)
            dp = lax.dot_general(dh, vh, (((1,), (1,)), ((), ())), preferred_element_type=F32)
            dsum = jnp.sum(p * dp, axis=-1, keepdims=True)
            ds = (p * (dp - dsum)).astype(BF16)
            dqs.append(jnp.dot(ds, kh, preferred_element_type=F32) * SCALE)
            dks.append(lax.dot_general(qh, ds, tn_dims, preferred_element_type=F32) * SCALE)
            dvs.append(lax.dot_general(dh, p.astype(BF16), tn_dims, preferred_element_type=F32))
        dq_ref[...] = jnp.concatenate(dqs, axis=1)
        dk = jnp.concatenate(dks, axis=0).T
        dv = jnp.concatenate(dvs, axis=0).T

        @pl.when(i == 0)
        def _():
            dk_ref[...] = dk
            dv_ref[...] = dv

        @pl.when(i > 0)
        def _():
            dk_ref[...] += dk
            dv_ref[...] += dv

    acc = pl.BlockSpec((M, MEM_WIDTH), lambda i: (0, 0))
    return pl.pallas_call(
        body, name=name, grid=(T // tq,), in_specs=[q_spec, _cols(tq, MEM_WIDTH, last), k_spec, v_spec],
        out_specs=(_cols(tq, MEM_WIDTH, dq_width // MEM_WIDTH - 1), acc, acc),
        out_shape=(jax.ShapeDtypeStruct((T, dq_width), F32), jax.ShapeDtypeStruct((M, MEM_WIDTH), F32),
                   jax.ShapeDtypeStruct((M, MEM_WIDTH), F32)),
        compiler_params=_params("arbitrary"),
    )(q, dcat, mkn, mkv)


def _loss(y, target, *, name):
    T, D = y.shape
    tr = _row_tile(T, D)

    def body(y_ref, t_ref, l_ref, dy_ref, dyb_ref):
        i = pl.program_id(0)
        err = y_ref[...] - t_ref[...]
        dy = err / float(D)
        dy_ref[...] = dy
        dyb_ref[...] = dy.astype(BF16)
        part = jnp.full((8, 128), 0.5 * jnp.sum(jnp.mean(err * err, axis=-1)), F32)

        @pl.when(i == 0)
        def _():
            l_ref[...] = part

        @pl.when(i > 0)
        def _():
            l_ref[...] += part

    row = pl.BlockSpec((tr, D), lambda i: (i, 0))
    return pl.pallas_call(
        body, name=name, grid=(T // tr,), in_specs=[row, row],
        out_specs=(pl.BlockSpec((8, 128), lambda i: (0, 0)), row, row),
        out_shape=(jax.ShapeDtypeStruct((8, 128), F32), jax.ShapeDtypeStruct((T, D), F32), jax.ShapeDtypeStruct((T, D), BF16)),
        compiler_params=_params("arbitrary"),
    )(y, target)


def _position():
    return lax.axis_index("x"), lax.axis_index("y"), lax.axis_index("c")


def _all_gather(arrays, *, name):
    n = len(arrays)

    def body(*refs):
        srcs, outs = refs[:n], refs[n:2 * n]
        token, send_sems, recv_sems, local_sems = refs[2 * n:]
        token[...] = jnp.zeros_like(token)
        x, y, c = _position()
        me, sibling = (x, y, c), (x, y, 1 - c)
        chips = [(1 - x, y), (x, 1 - y), (1 - x, 1 - y)]

        def slot(a, px, py, pc):
            return outs[a].at[4 * px + 2 * py + pc]

        def copy(a, k, block, to, src=None):
            return pltpu.make_async_remote_copy(
                src_ref=slot(a, *block) if src is None else src, dst_ref=slot(a, *block),
                send_sem=send_sems.at[a, k], recv_sem=recv_sems.at[a, k], device_id=to, device_id_type=MESH)

        mine = [pltpu.make_async_copy(srcs[a], slot(a, *me), local_sems.at[a]) for a in range(n)]
        for cp in mine:
            cp.start()
        first, passed = [], []
        for a in range(n):
            first.append(copy(a, 0, me, sibling, src=srcs[a]))
            first += [copy(a, 1 + j, me, (*chip, c), src=srcs[a]) for j, chip in enumerate(chips)]
        for cp in first:
            cp.start()
        for a in range(n):
            for j, chip in enumerate(chips):
                copy(a, 1 + j, (*chip, c), me).wait_recv()
                fwd = copy(a, 4 + j, (*chip, c), sibling)
                fwd.start()
                passed.append(fwd)
        for a in range(n):
            copy(a, 0, sibling, me).wait_recv()
            for j, chip in enumerate(chips):
                copy(a, 4 + j, (*chip, 1 - c), me).wait_recv()
        for cp in first + passed:
            cp.wait_send()
        for cp in mine:
            cp.wait()

    return pl.pallas_call(
        body, name=name, in_specs=[ANY] * n, out_specs=[ANY] * n + [pl.BlockSpec(memory_space=pltpu.VMEM)],
        out_shape=[jax.ShapeDtypeStruct((N_DEV,) + a.shape, a.dtype) for a in arrays] + [jax.ShapeDtypeStruct((8, 128), F32)],
        scratch_shapes=[pltpu.SemaphoreType.DMA((n, 7)), pltpu.SemaphoreType.DMA((n, 7)), pltpu.SemaphoreType.DMA((n,))],
    )(*arrays)


def _sibling_exchange(by_core, whole, *, name):
    n1, n = len(by_core), len(by_core) + len(whole)

    def body(*refs):
        srcs, outs = refs[:n], refs[n:2 * n]
        send_sems, recv_sems = refs[2 * n:]
        x, y, c = _position()
        copies = [
            pltpu.make_async_remote_copy(src_ref=srcs[a].at[:, 1 - c] if a < n1 else srcs[a], dst_ref=outs[a],
                                         send_sem=send_sems.at[a], recv_sem=recv_sems.at[a], device_id=(x, y, 1 - c),
                                         device_id_type=MESH)
            for a in range(n)]
        for cp in copies:
            cp.start()
        for cp in copies:
            cp.wait()

    out_shape = [jax.ShapeDtypeStruct(a.shape[:1] + a.shape[2:], a.dtype) for a in by_core]
    out_shape += [jax.ShapeDtypeStruct(a.shape, a.dtype) for a in whole]
    outs = pl.pallas_call(
        body, name=name, in_specs=[ANY] * n, out_specs=[ANY] * n, out_shape=out_shape,
        scratch_shapes=[pltpu.SemaphoreType.DMA((n,)), pltpu.SemaphoreType.DMA((n,))],
    )(*by_core, *whole)
    return outs[:n1], outs[n1:]


HBM = pl.BlockSpec(memory_space=pltpu.HBM)
SEM = pl.BlockSpec(memory_space=pltpu.SEMAPHORE)
DATAFLOW = pltpu.SideEffectType.DATAFLOW_SIDE_EFFECTING


def _device(flat):
    return flat // 4, (flat // 2) % 2, flat % 2


def _gather_copies(srcs, lands, send_sems, recv_sems, incoming):
    x, y, c = _position()
    me = 4 * x + 2 * y + c
    pairs = []
    for a in range(len(srcs)):
        for d in range(1, N_DEV):
            to, frm = (me + d) % N_DEV, (me + N_DEV - d) % N_DEV
            k = a * (N_DEV - 1) + d - 1
            sems = dict(send_sem=send_sems.at[k], recv_sem=recv_sems.at[k], device_id_type=MESH)
            out = pltpu.make_async_remote_copy(src_ref=srcs[a], dst_ref=lands[a].at[me], device_id=_device(to), **sems)
            inc = pltpu.make_async_remote_copy(src_ref=srcs[a], dst_ref=lands[a].at[frm], device_id=_device(frm),
                                               **sems) if incoming else None
            pairs.append((out, inc))
    return pairs


def _chip_copies(srcs, lands, send_sems, recv_sems, incoming, n_whole=0):
    x, y, c = _position()
    my_chip = 2 * x + y
    pairs = []
    for a in range(len(srcs)):
        for k, (px, py) in enumerate([(1 - x, y), (x, 1 - y), (1 - x, 1 - y)]):
            sem = a * (N_CHIP - 1) + k
            sems = dict(send_sem=send_sems.at[sem], recv_sem=recv_sems.at[sem], device_id=(px, py, c), device_id_type=MESH)
            src = srcs[a] if a >= len(srcs) - n_whole else srcs[a].at[2 * px + py]
            out = pltpu.make_async_remote_copy(src_ref=src, dst_ref=lands[a].at[my_chip], **sems)
            inc = pltpu.make_async_remote_copy(src_ref=src, dst_ref=lands[a].at[2 * px + py], **sems) if incoming else None
            pairs.append((out, inc))
    return pairs


def _push_start(copies, fan, srcs, lands, *, name):
    n = len(srcs)

    def body(*refs):
        src_refs, land_refs = refs[:n], refs[n:2 * n]
        send_sems, recv_sems = refs[2 * n], refs[2 * n + 1]
        token = refs[-1]
        for out, _ in copies(src_refs, land_refs, send_sems, recv_sems, False):
            out.start()
        token[...] = jnp.zeros_like(token)

    outs = pl.pallas_call(
        body, name=name,
        out_shape=(pltpu.SemaphoreType.DMA((n * fan,)), pltpu.SemaphoreType.DMA((n * fan,)),
                   *(pltpu.HBM(a.shape, a.dtype) for a in srcs), *(pltpu.HBM(a.shape, a.dtype) for a in lands),
                   jax.ShapeDtypeStruct((8, 128), F32)),
        in_specs=[HBM] * (2 * n), out_specs=(SEM, SEM, *([HBM] * (2 * n)), pl.BlockSpec(memory_space=pltpu.VMEM)),
        input_output_aliases={i: 2 + i for i in range(2 * n)},
        compiler_params=pltpu.CompilerParams(has_side_effects=DATAFLOW),
    )(*(pltpu.with_memory_space_constraint(a, pltpu.HBM) for a in (*srcs, *lands)))
    return outs[0], outs[1], list(outs[2:2 + n]), list(outs[2 + n:2 + 2 * n]), outs[-1]


def _push_wait(copies, send_sems, recv_sems, srcs, lands, after, *, name):
    n = len(srcs)

    def body(*refs):
        src_refs, land_refs = refs[:n], refs[n:2 * n]
        for out, inc in copies(src_refs, land_refs, refs[2 * n], refs[2 * n + 1], True):
            out.wait_send()
            inc.wait_recv()
        refs[-1][...] = jnp.zeros_like(refs[-1])

    outs = pl.pallas_call(
        body, name=name,
        out_shape=(*(pltpu.HBM(a.shape, a.dtype) for a in (*srcs, *lands)), jax.ShapeDtypeStruct((8, 128), F32)),
        in_specs=[HBM] * (2 * n) + [SEM, SEM, ANY], out_specs=(*([HBM] * (2 * n)), pl.BlockSpec(memory_space=pltpu.VMEM)),
        input_output_aliases={i: i for i in range(2 * n)},
        compiler_params=pltpu.CompilerParams(has_side_effects=DATAFLOW),
    )(*srcs, *lands, send_sems, recv_sems, after)
    return list(outs[n:2 * n]), outs[-1]


def _with_own_slot(block, index, slots):
    buf = lax.empty((slots,) + block.shape, block.dtype)
    return lax.dynamic_update_slice(buf, block[None], (index,) + (0,) * block.ndim)


def _view2d(shape):
    return math.prod(shape[:-1]), shape[-1]


def _pair_sum(mine, other, core, *, name, out_dtype):
    by_core = mine.ndim == 4
    n, w = other.shape[-2:]
    tr = _row_tile(n, w * 2)
    lead = other.shape[0] if by_core else 1

    def body(core_ref, a_ref, b_ref, o_ref):
        o_ref[...] = (a_ref[...].astype(F32) + b_ref[...].astype(F32)).astype(o_ref.dtype)

    if by_core:
        a_spec = pl.BlockSpec((None, None, tr, w), lambda j, i, core_ref: (j, core_ref[0], i, 0))
        o_spec = pl.BlockSpec((None, tr, w), lambda j, i, core_ref: (j, i, 0))
    else:
        a_spec = o_spec = pl.BlockSpec((tr, w), lambda j, i, core_ref: (i, 0))
    grid_spec = pltpu.PrefetchScalarGridSpec(num_scalar_prefetch=1, grid=(lead, n // tr), in_specs=[a_spec, o_spec],
                                             out_specs=o_spec)
    return pl.pallas_call(body, name=name, grid_spec=grid_spec, out_shape=jax.ShapeDtypeStruct(other.shape, out_dtype),
                          compiler_params=_params("parallel", "parallel"))(core.reshape(1), mine, other)


def _adamw(parts, w, m, v, *, name):
    layers = len(parts)
    n_parts, R, W = parts[0].shape
    tr = _row_tile(R, W * 4)
    per_layer = R // tr

    def update(p_ref, w_ref, m_ref, v_ref, g_out, d_out, m_out, v_out):
        g = p_ref[0].astype(F32)
        for j in range(1, n_parts):
            g = g + p_ref[j].astype(F32)
        m_new = ADAM_B1 * m_ref[...] + (1.0 - ADAM_B1) * g
        v_new = ADAM_B2 * v_ref[...] + (1.0 - ADAM_B2) * (g * g)
        m_hat = m_new / (1.0 - ADAM_B1 ** ADAM_STEP)
        v_hat = v_new / (1.0 - ADAM_B2 ** ADAM_STEP)
        g_out[...] = g
        d_out[...] = -ADAM_LR * (m_hat / (jnp.sqrt(v_hat) + ADAM_EPS) + ADAM_WD * w_ref[...])
        m_out[...] = m_new
        v_out[...] = v_new

    def body(*refs):
        for k in range(layers):
            pl.when(pl.program_id(0) == k)(lambda k=k: update(refs[k], *refs[layers:]))

    def parts_spec(k):
        return pl.BlockSpec((n_parts, tr, W), lambda l, i: (0, jnp.where(l == k, i, 0), 0))

    row = pl.BlockSpec((tr, W), lambda l, i: (l * per_layer + i, 0))
    out = jax.ShapeDtypeStruct((layers * R, W), F32)
    return pl.pallas_call(
        body, name=name, grid=(layers, per_layer), in_specs=[parts_spec(k) for k in range(layers)] + [row, row, row],
        out_specs=(row, row, row, row), out_shape=(out, out, out, out), compiler_params=_params("arbitrary", "arbitrary"),
    )(*parts, w, m, v)


SMALL_ROWS = 608


def _pack_small(p):
    flat = jnp.concatenate([p[n].reshape(-1).astype(F32) for n in SMALL_NAMES])
    return jnp.pad(flat, (0, SMALL_ROWS * PACK_W - flat.shape[0])).reshape(SMALL_ROWS, PACK_W)


def _unpack_small(buf, like):
    out, at = {}, 0
    flat = buf.reshape(-1)
    for n in SMALL_NAMES:
        size = math.prod(like[n].shape)
        out[n] = flat[at:at + size].reshape(like[n].shape)
        at += size
    return out


def _block_diag(pw):
    out = jnp.zeros((MAIN_WIDTH, MAIN_WIDTH), pw.dtype)
    for g in range(POOL_GROUPS):
        out = lax.dynamic_update_slice(out, pw[g], (g * POOL_GROUP_DIM, g * POOL_GROUP_DIM))
    return out


def _diag_blocks(m):
    return jnp.stack([m[g * POOL_GROUP_DIM:(g + 1) * POOL_GROUP_DIM, g * POOL_GROUP_DIM:(g + 1) * POOL_GROUP_DIM]
                      for g in range(POOL_GROUPS)])


def _train_pass(x, mem, target, p, w_kv, fetch, reduce_layer, reduce_wait):
    T = x.shape[0]
    mem_cols = MAIN_WIDTH // MEM_WIDTH
    k_gain = _head_gain(p["k_norm"], SWA_KV_HEADS)
    saved = []
    h = x
    kn = kv = h_kv = hn_kv = None
    for l in range(DEPTH):
        s = {}
        wl, token = fetch(l, h)
        s["w"] = wl
        if l == N_A:
            h_kv = h
            hn_kv, kv = _norm_mm(h, p["kv_norm"], w_kv, b_kind="rows", name="kv_proj")
            kn = _seg_rms_fwd(kv, k_gain, width=KV_HALF, col=0, name="k_norm_fwd")
        s["h"] = h
        s["xn1"], proj = _norm_mm(h, p["norm_mix"][l] + token, wl["w_in"], b_kind="rows", name="in_proj")
        s["proj"] = proj
        s["memn"] = _rms_fwd(mem, p["mem_norm"][l], name="mem_norm_fwd")
        s["mkv"] = _mm(s["memn"], wl["w_mem_kv"], b_kind="rows", name="mem_kv_proj")
        s["mk_gain"] = _head_gain(p["mem_k_norm"][l], MEM_HEADS)
        s["mkn"] = _seg_rms_fwd(s["mkv"], s["mk_gain"], width=MEM_WIDTH, col=0, name="mem_k_norm_fwd")
        if l < N_A:
            s["q_gain"] = _head_gain(p["mem_q_norm"][l], MEM_HEADS)
            s["qn"] = _seg_rms_fwd(proj, s["q_gain"], width=MEM_WIDTH, col=mem_cols, name="mem_q_norm_fwd")
            s["q_col"] = 0
        else:
            j = l - N_A
            s["q_gain"] = jnp.concatenate([_head_gain(p["q_norm"][j], SWA_Q_HEADS), _head_gain(p["mem_q_norm"][l], MEM_HEADS)],
                                          axis=1)
            s["qn"] = _seg_rms_fwd(proj, s["q_gain"], width=D_MODEL, col=0, name="q_norm_fwd")
            s["q_col"] = mem_cols
        cat = _mem_attn_fwd(s["qn"], s["q_col"], s["mkn"], s["mkv"], name="mem_attn_fwd")
        if l < N_A:
            s["mix"] = _block_diag(p["pool_w"][l])
            s["scale"] = p["pool_scale"][l].reshape(1, MAIN_WIDTH)
            s["cat"] = _pool_mix_fwd(proj, s["mix"], s["scale"], cat, name="pool_fwd")
        else:
            s["cat"] = _swa_attn_fwd(s["qn"], kn, kv, p["sinks"][l - N_A], cat, name="swa_fwd")
        s["h1"] = _mm(s["cat"], wl["w_out"], b_kind="rows", res=h, name="out_proj")
        s["xn2"], s["a"] = _norm_mm(s["h1"], p["norm_mlp"][l], wl["w_up"], b_kind="layers", relu2=True, name="mlp_up")
        h = _mm(s["a"], wl["w_down"], b_kind="rows", res=s["h1"], name="mlp_down")
        saved.append(s)

    loss, dh, dh_b = _loss(h, target, name="loss_head")

    g = {n: [None] * DEPTH for n in ("norm_mix", "mem_norm", "mem_q_norm", "mem_k_norm", "norm_mlp")}
    g.update({n: [None] * N_A for n in ("pool_w", "pool_scale", "q_norm", "sinks")})
    g_kv = None
    token = None
    dks, dvs = [], []
    for l in reversed(range(DEPTH)):
        s = saved[l]
        wl = s["w"]
        gb = {}

        def dw(a, dy, n):
            return _mm(a, dy, ta=True, out_kind="layers" if n == "w_up" else "rows", out_buf=lax.empty(wl[n].shape, BF16),
                       name=n + "_grad")

        norm_mlp_gain = p["norm_mlp"][l] if token is None else p["norm_mlp"][l] + token
        gb["w_down"] = dw(s["a"], dh_b, "w_down")
        du = _mm(dh_b, wl["w_down"], tb=True, b_kind="rows", mul2=s["a"], out_dtype=BF16, name="mlp_down_dx")
        gb["w_up"] = dw(s["xn2"], du, "w_up")
        early = reduce_layer(l, gb, early=True)
        if early is not None:
            norm_mlp_gain = norm_mlp_gain + early
        dh1, dh1_b, g["norm_mlp"][l] = _mm_rms_bwd(du, wl["w_up"], s["h1"], norm_mlp_gain, dh, b_kind="layers", also_bf16=True,
                                                   name="mlp_up_dx")
        gb["w_out"] = dw(s["cat"], dh1_b, "w_out")
        dcat = _mm(dh1_b, wl["w_out"], tb=True, b_kind="rows", name="out_proj_dx")
        if l < N_A:
            dq, dmk, dmv = _mem_attn_bwd(s["qn"], s["q_col"], s["mkn"], s["mkv"], dcat, dq_width=MEM_WIDTH, name="mem_attn_bwd")
            dproj, dmix, dscale = _pool_mix_bwd(s["proj"], s["mix"], s["scale"], dcat, name="pool_bwd")
            g["pool_w"][l] = _diag_blocks(dmix)
            g["pool_scale"][l] = dscale.reshape(MAIN_WIDTH)
            dproj, dgain = _seg_rms_bwd(s["proj"], s["q_gain"], [dq], width=MEM_WIDTH, col=mem_cols, out_buf=dproj,
                                        out_col=mem_cols, name="mem_q_norm_bwd")
            g["mem_q_norm"][l] = _fold_heads(dgain, MEM_HEADS)
        else:
            j = l - N_A
            dqn, dmk, dmv = _mem_attn_bwd(s["qn"], s["q_col"], s["mkn"], s["mkv"], dcat, dq_width=D_MODEL, name="mem_attn_bwd")
            dqn, dk_l, dv_l, dsinks = _swa_attn_bwd(s["qn"], kn, kv, p["sinks"][j], dcat, dqn, name="swa_bwd")
            dks.append(dk_l)
            dvs.append(dv_l)
            g["sinks"][j] = dsinks[0, :SWA_Q_HEADS]
            dproj, dgain = _seg_rms_bwd(s["proj"], s["q_gain"], [dqn], width=D_MODEL, col=0, name="q_norm_bwd")
            g["q_norm"][j] = _fold_heads(dgain[:, :MAIN_WIDTH], SWA_Q_HEADS)
            g["mem_q_norm"][l] = _fold_heads(dgain[:, MAIN_WIDTH:], MEM_HEADS)
        dmk_raw, dgain = _seg_rms_bwd(s["mkv"], s["mk_gain"], [dmk], width=MEM_WIDTH, col=0, name="mem_k_norm_bwd")
        g["mem_k_norm"][l] = _fold_heads(dgain, MEM_HEADS)
        dmkv = jnp.concatenate([dmk_raw, dmv.astype(BF16)], axis=1)
        gb["w_mem_kv"] = dw(s["memn"], dmkv, "w_mem_kv")
        dmemn = _mm(dmkv, wl["w_mem_kv"], tb=True, b_kind="rows", name="mem_kv_proj_dx")
        g["mem_norm"][l] = _rms_bwd(mem, p["mem_norm"][l], [dmemn], want_dx=False, name="mem_norm_bwd")
        gb["w_in"] = dw(s["xn1"], dproj, "w_in")
        if l in (0, N_A):
            dh, g["norm_mix"][l] = _mm_rms_bwd(dproj, wl["w_in"], s["h"], p["norm_mix"][l], dh1, b_kind="rows", also_bf16=False,
                                               name="in_proj_dx")
        else:
            dh, dh_b, g["norm_mix"][l] = _mm_rms_bwd(dproj, wl["w_in"], s["h"], p["norm_mix"][l], dh1, b_kind="rows",
                                                     also_bf16=True, name="in_proj_dx")
        if l == N_A:
            dkv, dgain = _seg_rms_bwd(kv, k_gain, dks, width=KV_HALF, col=0, out_buf=lax.empty((T, 2 * KV_HALF), BF16),
                                      name="k_norm_bwd")
            g["k_norm"] = _fold_heads(dgain, SWA_KV_HEADS)
            dkv = _sum_into(dvs[0], dvs[1], dkv, 1, name="dv_sum")
            g_kv = _mm(hn_kv, dkv, ta=True, out_kind="rows", out_buf=lax.empty(w_kv.shape, BF16), name="w_kv_grad")
            dh, dh_b, g["kv_norm"] = _mm_rms_bwd(dkv, w_kv, h_kv, p["kv_norm"], dh, b_kind="rows", also_bf16=True,
                                                 name="kv_proj_dx")
        if l + 1 < DEPTH:
            reduce_wait(l + 1, dh)
        token = reduce_layer(l, gb)
    grads = {n: (jnp.stack(v) if isinstance(v, list) else v) for n, v in g.items()}
    return loss, dh, grads, g_kv


def kernel(x, mem, norm_mix, w_in, pool_w, pool_scale, kv_norm, w_kv, k_norm, q_norm, sinks, mem_norm, w_mem_kv, mem_q_norm, mem_k_norm, w_out, norm_mlp, w_up, w_down, loss_target, m_norm_mix, m_w_in, m_pool_w, m_pool_scale, m_kv_norm, m_w_kv, m_k_norm, m_q_norm, m_sinks, m_mem_norm, m_w_mem_kv, m_mem_q_norm, m_mem_k_norm, m_w_out, m_norm_mlp, m_w_up, m_w_down, v_norm_mix, v_w_in, v_pool_w, v_pool_scale, v_kv_norm, v_w_kv, v_k_norm, v_q_norm, v_sinks, v_mem_norm, v_w_mem_kv, v_mem_q_norm, v_mem_k_norm, v_w_out, v_norm_mlp, v_w_up, v_w_down):
    weights = dict(norm_mix=norm_mix, w_in=w_in, pool_w=pool_w, pool_scale=pool_scale, kv_norm=kv_norm, w_kv=w_kv,
                   k_norm=k_norm, q_norm=q_norm, sinks=sinks, mem_norm=mem_norm, w_mem_kv=w_mem_kv,
                   mem_q_norm=mem_q_norm, mem_k_norm=mem_k_norm, w_out=w_out, norm_mlp=norm_mlp, w_up=w_up, w_down=w_down)
    mom1 = dict(norm_mix=m_norm_mix, w_in=m_w_in, pool_w=m_pool_w, pool_scale=m_pool_scale, kv_norm=m_kv_norm, w_kv=m_w_kv,
                k_norm=m_k_norm, q_norm=m_q_norm, sinks=m_sinks, mem_norm=m_mem_norm, w_mem_kv=m_w_mem_kv,
                mem_q_norm=m_mem_q_norm, mem_k_norm=m_mem_k_norm, w_out=m_w_out, norm_mlp=m_norm_mlp, w_up=m_w_up,
                w_down=m_w_down)
    mom2 = dict(norm_mix=v_norm_mix, w_in=v_w_in, pool_w=v_pool_w, pool_scale=v_pool_scale, kv_norm=v_kv_norm, w_kv=v_w_kv,
                k_norm=v_k_norm, q_norm=v_q_norm, sinks=v_sinks, mem_norm=v_mem_norm, w_mem_kv=v_w_mem_kv,
                mem_q_norm=v_mem_q_norm, mem_k_norm=v_mem_k_norm, w_out=v_w_out, norm_mlp=v_norm_mlp, w_up=v_w_up,
                w_down=v_w_down)
    names = list(weights)
    x_pos, y_pos, core = (lax.axis_index(n).astype(jnp.int32) for n in AXES)
    me, my_chip = 4 * x_pos + 2 * y_pos + core, 2 * x_pos + y_pos
    shard = MAIN_WIDTH // N_DEV

    def layer_shards(l, zero=0.0):
        return [(weights[n][l:l + 1] + zero).astype(BF16) for n in LAYERED]

    def usable(arrays):
        wl = dict(zip(LAYERED, arrays))
        wl["w_up"] = wl["w_up"].transpose(1, 2, 0, 3).reshape(1, D_MODEL, D_FF)
        return wl

    scale_block = jnp.pad(pool_scale, ((0, 8 - N_A), (0, 128 - shard)))
    *first, first_done = _all_gather(layer_shards(0) + [w_kv[None].astype(BF16), scale_block], name="gather_first")
    p = {n: weights[n] for n in SMALL_NAMES}
    p["pool_scale"] = first[-1][:, :N_A, :shard].transpose(1, 0, 2).reshape(N_A, MAIN_WIDTH)
    gathers, reduces, parts = {}, {}, {}

    def fetch(l, after):
        if l == 0:
            got, done = first[:len(LAYERED)], first_done
        else:
            got, done = _push_wait(_gather_copies, *gathers.pop(l), after, name=f"gather_wait_{l}")
        token = 0.0
        if l + 1 < DEPTH:
            srcs = layer_shards(l + 1, done[0, 0])
            lands = [_with_own_slot(a, me, N_DEV) for a in srcs]
            *handles, block = _push_start(_gather_copies, N_DEV - 1, srcs, lands, name=f"gather_start_{l + 1}")
            gathers[l + 1], token = handles, block[0, 0]
        return usable(got), token

    def by_core(gb):
        gb = dict(gb)
        if "w_up" in gb:
            gb["w_up"] = gb["w_up"].reshape(D_MODEL, N_DEV, D_FF // N_DEV).transpose(1, 0, 2)
        order = [n for n in LAYERED if n in gb] + [n for n in gb if n not in LAYERED]
        return {n: gb[n].reshape((N_CHIP, 2) + _view2d(gb[n].shape[1:] if n == "w_up" else gb[n].shape[2:])) for n in order}

    def pair_sums(views, sib, tag):
        return [_pair_sum(a, b, core, name=f"chip_sum_{n}_{tag}", out_dtype=BF16) for (n, a), b in zip(views.items(), sib)]

    def chip_sums(gb, tag, whole=()):
        views = by_core(gb)
        sib, sib_whole = _sibling_exchange(list(views.values()), list(whole), name="reduce_sibling_" + tag)
        return pair_sums(views, sib, tag), sib_whole

    def start_chip_exchange(sums, tag, whole=()):
        lands = [_with_own_slot(lax.dynamic_index_in_dim(a, my_chip, 0, keepdims=False), my_chip, N_CHIP) for a in sums]
        lands += [_with_own_slot(a, my_chip, N_CHIP) for a in whole]
        copies = functools.partial(_chip_copies, n_whole=len(whole))
        *handles, block = _push_start(copies, N_CHIP - 1, [*sums, *whole], lands, name="reduce_start_" + tag)
        return (copies, *handles), block[0, 0]

    mlp = ("w_up", "w_down")

    def reduce_layer(l, gb, early=False):
        if early and l > 0:
            return None
        if l == 0 and not early:
            reduces["rest"] = {n: a for n, a in gb.items() if n not in mlp}
            return None
        tag = "0_mlp" if early else str(l)
        sums, _ = chip_sums({n: gb[n] for n in mlp} if early else gb, tag)
        reduces[l], token = start_chip_exchange(sums, tag)
        return token

    def reduce_wait(l, after):
        copies, *handles = reduces.pop(l)
        return _push_wait(copies, *handles, after, name=f"reduce_wait_{l}")[0]

    def layer_wait(l, after):
        parts[l] = reduce_wait(l, after)

    loss, grad_x, grads, g_kv = _train_pass(x[0], mem[0], loss_target[0], p, first[len(LAYERED)], fetch, reduce_layer, layer_wait)

    last = dict(reduces.pop("rest"))
    last["w_kv"] = g_kv
    last["pool_scale"] = grads["pool_scale"].reshape(N_A, N_DEV, shard).transpose(1, 0, 2).astype(BF16)[:, None]
    small = _pack_small(grads)
    sums, (sib_small,) = chip_sums(last, "0", whole=[small])
    chip_small = _pair_sum(small, sib_small, core, name="chip_sum_small", out_dtype=F32)
    reduces["rest"], _ = start_chip_exchange(sums, "0_rest", whole=[chip_small])

    def adamw(n, n_parts):
        res = _adamw(n_parts, *(d[n].reshape(_view2d(d[n].shape)) for d in (weights, mom1, mom2)), name="adamw_" + n)
        return [r.reshape(weights[n].shape) for r in res]

    p_up, p_down = reduce_wait(0, chip_small)
    parts[0] = [None, None, None, p_up, p_down]
    new = {n: adamw(n, [parts[l][LAYERED.index(n)] for l in range(DEPTH)]) for n in mlp}
    p_in, p_mem_kv, p_out, parts_kv, parts_scale, parts_small = reduce_wait("rest", new["w_down"][0])
    parts[0][:3] = [p_in, p_mem_kv, p_out]
    new.update({n: adamw(n, [parts[l][k] for l in range(DEPTH)]) for k, n in enumerate(LAYERED) if n not in mlp})
    new["w_kv"] = adamw("w_kv", [parts_kv])
    new["pool_scale"] = adamw("pool_scale", [parts_scale])
    res = _adamw([parts_small], _pack_small(weights), _pack_small(mom1), _pack_small(mom2), name="adamw_replicated")
    for n, vals in zip(SMALL_NAMES, zip(*(_unpack_small(r, weights).values() for r in res))):
        new[n] = list(vals)
    outs = [new[n][k] for k in range(4) for n in names]
    total = lax.psum(loss[0, 0], AXES)
    return (total, grad_x[None], *outs)
```
